```python
import jax, jax.numpy as jnp
from jax import lax
import numpy as np

D_MODEL = 1024
BATCH = 8
SEQ = 8192
DEPTH = 2

CHUNK = 64
Q_BLOCK = 128
D_FF = 2816
BRANCH_WIDTH = 512
N_BRANCH = 3
LRU_WIDTH = BRANCH_WIDTH
LRU_BLOCKS = 8
LRU_BLOCK = LRU_WIDTH // LRU_BLOCKS
CONV_WIDTH = 4
LRU_C = 8.0
GLA_HEADS = 4
GLA_DV = BRANCH_WIDTH // GLA_HEADS
GLA_DK = GLA_DV // 2
GLA_LOWRANK = 16
GLA_TAU = 16.0
FOX_HEADS = 8
FOX_DH = BRANCH_WIDTH // FOX_HEADS
PLE_DIM = 256
LN_EPS = 1e-5
RMS_EPS = 1e-6
DEEPNORM_ALPHA = (2 * DEPTH) ** 0.25
DEEPNORM_BETA = (8 * DEPTH) ** -0.25

SPLIT_SIZES = (
    LRU_WIDTH,
    LRU_WIDTH,
    GLA_HEADS * GLA_DK,
    GLA_HEADS * GLA_DK,
    GLA_HEADS * GLA_DV,
    GLA_LOWRANK,
    GLA_HEADS * GLA_DV,
    FOX_HEADS * FOX_DH,
    FOX_HEADS * FOX_DH,
    FOX_HEADS * FOX_DH,
    FOX_HEADS,
    N_BRANCH * D_MODEL,
)
SPLIT_POINTS = tuple(int(v) for v in np.cumsum(SPLIT_SIZES)[:-1])
D_IN = int(sum(SPLIT_SIZES))

kernel_name = 'hybrid_rglru_gla_fox_macaron_deepnorm'


def layer_norm(x, g, b):
    xf = x.astype(jnp.float32)
    mu = jnp.mean(xf, axis=-1, keepdims=True)
    var = jnp.mean(jnp.square(xf - mu), axis=-1, keepdims=True)
    y = (xf - mu) * lax.rsqrt(var + LN_EPS) * g.astype(jnp.float32) + b.astype(jnp.float32)
    return y.astype(x.dtype)


def swiglu(x, w_up, w_down):
    gate, up = jnp.split(x @ w_up, 2, axis=-1)
    return (jax.nn.silu(gate) * up) @ w_down


def linear_scan(a, b):
    def combine(left, right):
        return (left[0] * right[0], right[0] * left[1] + right[1])
    return lax.associative_scan(combine, (a, b), axis=1)[1]


def causal_depthwise_conv(x, w, b):
    y = lax.conv_general_dilated(
        x, w[:, None, :], window_strides=(1,), padding=[(CONV_WIDTH - 1, 0)],
        dimension_numbers=('NWC', 'WIO', 'NWC'), feature_group_count=x.shape[-1])
    return y + b


def rg_lru(x, wa, ba, wx, bx, lam):
    bsz, seq, width = x.shape
    f32 = jnp.float32
    xf = x.astype(f32)
    xb = xf.reshape(bsz, seq, LRU_BLOCKS, LRU_BLOCK)
    r = jax.nn.sigmoid(jnp.einsum('bsnc,ncd->bsnd', xb, wa.astype(f32)).reshape(bsz, seq, width) + ba.astype(f32))
    i = jax.nn.sigmoid(jnp.einsum('bsnc,ncd->bsnd', xb, wx.astype(f32)).reshape(bsz, seq, width) + bx.astype(f32))
    log_a = -LRU_C * r * jax.nn.softplus(-lam.astype(f32))
    a = jnp.exp(log_a)
    mult = jnp.sqrt(-jnp.expm1(2.0 * log_a))
    h = linear_scan(a, mult * (i * xf))
    return h.astype(x.dtype)


def gla(q, k, v, g_low, w_g2, b_g, norm_g, out_gate):
    bsz, seq, _ = q.shape
    nc = seq // CHUNK
    f32 = jnp.float32
    qc = q.astype(f32).reshape(bsz, nc, CHUNK, GLA_HEADS, GLA_DK) * (GLA_DK ** -0.5)
    kc = k.astype(f32).reshape(bsz, nc, CHUNK, GLA_HEADS, GLA_DK)
    vc = v.astype(f32).reshape(bsz, nc, CHUNK, GLA_HEADS, GLA_DV)
    log_alpha = jax.nn.log_sigmoid((g_low @ w_g2 + b_g).astype(f32)) / GLA_TAU
    log_alpha = log_alpha.reshape(bsz, nc, CHUNK, GLA_HEADS, GLA_DK)
    g_cum = jnp.cumsum(log_alpha, axis=2)
    g_tot = g_cum[:, :, -1]
    k_dec = kc * jnp.exp(g_tot[:, :, None] - g_cum)
    delta = jnp.einsum('bnchk,bnchv->bnhkv', k_dec, vc)
    state = linear_scan(jnp.exp(g_tot)[..., None], delta)
    o = jnp.einsum('bnchk,bnhkv->bnchv', qc, state)
    o = o * lax.rsqrt(jnp.mean(jnp.square(o), axis=-1, keepdims=True) + RMS_EPS)
    o = o.reshape(bsz, seq, GLA_HEADS * GLA_DV) * norm_g.astype(f32)
    return o.astype(q.dtype) * jax.nn.silu(out_gate)


def forgetting_attention(q, k, v, f_logit, b_f):
    bsz, seq, _ = q.shape
    nb = seq // Q_BLOCK
    f32 = jnp.float32
    scale = FOX_DH ** -0.5
    qh = q.reshape(bsz, seq, FOX_HEADS, FOX_DH)
    kh = k.reshape(bsz, seq, FOX_HEADS, FOX_DH).transpose(0, 2, 1, 3)
    vh = v.reshape(bsz, seq, FOX_HEADS, FOX_DH).transpose(0, 2, 1, 3)
    log_f = jax.nn.log_sigmoid((f_logit + b_f).astype(f32))
    f_cum = jnp.cumsum(log_f, axis=1).transpose(0, 2, 1)
    q_blk = qh.reshape(bsz, nb, Q_BLOCK, FOX_HEADS, FOX_DH).transpose(1, 0, 3, 2, 4)
    f_blk = f_cum.reshape(bsz, FOX_HEADS, nb, Q_BLOCK).transpose(2, 0, 1, 3)
    kpos = jnp.arange(seq)

    def attend(args):
        qb, fb, n = args
        logits = jnp.einsum('bhqd,bhkd->bhqk', qb, kh).astype(f32) * scale
        logits = logits + (fb[..., None] - f_cum[:, :, None, :])
        qpos = n * Q_BLOCK + jnp.arange(Q_BLOCK)
        mask = kpos[None, :] <= qpos[:, None]
        probs = jax.nn.softmax(jnp.where(mask, logits, -jnp.inf), axis=-1)
        return jnp.einsum('bhqk,bhkd->bhqd', probs.astype(vh.dtype), vh)

    out = lax.map(attend, (q_blk, f_blk, jnp.arange(nb)))
    return out.transpose(1, 0, 3, 2, 4).reshape(bsz, seq, FOX_HEADS * FOX_DH)


def hybrid_mixer(x, w_in, conv_w, conv_b, lru_wa, lru_ba, lru_wx, lru_bx, lru_lambda,
                 gla_w_g2, gla_b_g, gla_norm_g, fox_b_f, w_branch, w_out):
    bsz, seq, _ = x.shape
    (a_x, a_y, b_q, b_k, b_v, b_low, b_r,
     c_q, c_k, c_v, c_f, gate_logits) = jnp.split(x @ w_in, SPLIT_POINTS, axis=-1)
    y_a = jax.nn.gelu(a_y) * rg_lru(causal_depthwise_conv(a_x, conv_w, conv_b),
                                    lru_wa, lru_ba, lru_wx, lru_bx, lru_lambda)
    y_b = gla(b_q, b_k, b_v, b_low, gla_w_g2, gla_b_g, gla_norm_g, b_r)
    y_c = forgetting_attention(c_q, c_k, c_v, c_f, fox_b_f)
    gates = jax.nn.sigmoid(gate_logits).reshape(bsz, seq, N_BRANCH, D_MODEL)
    merged = (gates[:, :, 0] * (y_a @ w_branch[0])
              + gates[:, :, 1] * (y_b @ w_branch[1])
              + gates[:, :, 2] * (y_c @ w_branch[2]))
    return merged @ w_out


def _fwd_setup_inputs(seed: int = 0) -> dict:
    key = jax.random.key(seed)
    ks = jax.random.split(key, 40)
    f32 = jnp.float32

    def nrm(k, shape, scale):
        return jax.random.normal(k, shape, f32) * scale

    def gain(k, shape):
        return 1.0 + 0.02 * jax.random.normal(k, shape, f32)

    u = jax.random.uniform(ks[12], (DEPTH, LRU_WIDTH), f32, minval=0.9, maxval=0.999)
    a0 = u ** (1.0 / LRU_C)
    lru_lambda = jnp.log(a0) - jnp.log1p(-a0)

    return {
        'x': nrm(ks[0], (BATCH, SEQ, D_MODEL), 1.0),
        'p': nrm(ks[1], (DEPTH, BATCH, SEQ, PLE_DIM), 1.0),
        'ffn1_w_up': nrm(ks[2], (DEPTH, D_MODEL, 2 * D_FF), D_MODEL ** -0.5),
        'ffn1_w_down': nrm(ks[3], (DEPTH, D_FF, D_MODEL), D_FF ** -0.5 * DEEPNORM_BETA),
        'ln1_g': gain(ks[4], (DEPTH, D_MODEL)),
        'ln1_b': nrm(ks[5], (DEPTH, D_MODEL), 0.02),
        'w_in': nrm(ks[6], (DEPTH, D_MODEL, D_IN), D_MODEL ** -0.5),
        'conv_w': nrm(ks[7], (DEPTH, CONV_WIDTH, LRU_WIDTH), CONV_WIDTH ** -0.5),
        'conv_b': nrm(ks[8], (DEPTH, LRU_WIDTH), 0.02),
        'lru_wa': nrm(ks[9], (DEPTH, LRU_BLOCKS, LRU_BLOCK, LRU_BLOCK), LRU_BLOCK ** -0.5),
        'lru_ba': nrm(ks[10], (DEPTH, LRU_WIDTH), 0.02),
        'lru_wx': nrm(ks[11], (DEPTH, LRU_BLOCKS, LRU_BLOCK, LRU_BLOCK), LRU_BLOCK ** -0.5),
        'lru_bx': nrm(ks[13], (DEPTH, LRU_WIDTH), 0.02),
        'lru_lambda': lru_lambda,
        'gla_w_g2': nrm(ks[14], (DEPTH, GLA_LOWRANK, GLA_HEADS * GLA_DK), GLA_LOWRANK ** -0.5),
        'gla_b_g': nrm(ks[15], (DEPTH, GLA_HEADS * GLA_DK), 0.02),
        'gla_norm_g': gain(ks[16], (DEPTH, GLA_HEADS * GLA_DV)),
        'fox_b_f': jax.random.uniform(ks[17], (DEPTH, FOX_HEADS), f32, minval=1.0, maxval=4.0),
        'w_branch': nrm(ks[18], (DEPTH, N_BRANCH, BRANCH_WIDTH, D_MODEL), BRANCH_WIDTH ** -0.5),
        'w_out': nrm(ks[19], (DEPTH, D_MODEL, D_MODEL), D_MODEL ** -0.5 * DEEPNORM_BETA),
        'ln2_g': gain(ks[20], (DEPTH, D_MODEL)),
        'ln2_b': nrm(ks[21], (DEPTH, D_MODEL), 0.02),
        'ffn2_w_up': nrm(ks[22], (DEPTH, D_MODEL, 2 * D_FF), D_MODEL ** -0.5),
        'ffn2_w_down': nrm(ks[23], (DEPTH, D_FF, D_MODEL), D_FF ** -0.5 * DEEPNORM_BETA),
        'ln3_g': gain(ks[24], (DEPTH, D_MODEL)),
        'ln3_b': nrm(ks[25], (DEPTH, D_MODEL), 0.02),
        'ple_w_proj': nrm(ks[26], (DEPTH, PLE_DIM, D_MODEL), PLE_DIM ** -0.5 * DEEPNORM_BETA),
        'ple_w_gate': nrm(ks[27], (DEPTH, D_MODEL, D_MODEL), D_MODEL ** -0.5),
        'ple_b_gate': nrm(ks[28], (DEPTH, D_MODEL), 0.02),
        'ln4_g': gain(ks[29], (DEPTH, D_MODEL)),
        'ln4_b': nrm(ks[30], (DEPTH, D_MODEL), 0.02),
    }


def _fwd_reference(x, p, ffn1_w_up, ffn1_w_down, ln1_g, ln1_b, w_in, conv_w, conv_b,
              lru_wa, lru_ba, lru_wx, lru_bx, lru_lambda, gla_w_g2, gla_b_g, gla_norm_g,
              fox_b_f, w_branch, w_out, ln2_g, ln2_b, ffn2_w_up, ffn2_w_down, ln3_g, ln3_b,
              ple_w_proj, ple_w_gate, ple_b_gate, ln4_g, ln4_b):
    for i in range(DEPTH):
        x = layer_norm(DEEPNORM_ALPHA * x + 0.5 * swiglu(x, ffn1_w_up[i], ffn1_w_down[i]),
                       ln1_g[i], ln1_b[i])
        mix = hybrid_mixer(x, w_in[i], conv_w[i], conv_b[i], lru_wa[i], lru_ba[i], lru_wx[i],
                           lru_bx[i], lru_lambda[i], gla_w_g2[i], gla_b_g[i], gla_norm_g[i],
                           fox_b_f[i], w_branch[i], w_out[i])
        x = layer_norm(DEEPNORM_ALPHA * x + mix, ln2_g[i], ln2_b[i])
        x = layer_norm(DEEPNORM_ALPHA * x + 0.5 * swiglu(x, ffn2_w_up[i], ffn2_w_down[i]),
                       ln3_g[i], ln3_b[i])
        pe = p[i] @ ple_w_proj[i]
        x = layer_norm(DEEPNORM_ALPHA * x + jax.nn.sigmoid(x @ ple_w_gate[i] + ple_b_gate[i]) * pe,
                       ln4_g[i], ln4_b[i])
    return x


import jax as _jax
import jax.numpy as _jnp

TWIN_FORMAT = 'train_step'
FWD_PARAMS = ['x', 'p', 'ffn1_w_up', 'ffn1_w_down', 'ln1_g', 'ln1_b', 'w_in', 'conv_w', 'conv_b', 'lru_wa', 'lru_ba', 'lru_wx', 'lru_bx', 'lru_lambda', 'gla_w_g2', 'gla_b_g', 'gla_norm_g', 'fox_b_f', 'w_branch', 'w_out', 'ln2_g', 'ln2_b', 'ffn2_w_up', 'ffn2_w_down', 'ln3_g', 'ln3_b', 'ple_w_proj', 'ple_w_gate', 'ple_b_gate', 'ln4_g', 'ln4_b']
TWIN_WEIGHTS = ['ffn1_w_up', 'ffn1_w_down', 'ln1_g', 'ln1_b', 'w_in', 'conv_w', 'conv_b', 'lru_wa', 'lru_ba', 'lru_wx', 'lru_bx', 'lru_lambda', 'gla_w_g2', 'gla_b_g', 'gla_norm_g', 'fox_b_f', 'w_branch', 'w_out', 'ln2_g', 'ln2_b', 'ffn2_w_up', 'ffn2_w_down', 'ln3_g', 'ln3_b', 'ple_w_proj', 'ple_w_gate', 'ple_b_gate', 'ln4_g', 'ln4_b']
TWIN_DIFF_INPUT = 'x'
TWIN_INPUTS = ['x', 'p', 'ffn1_w_up', 'ffn1_w_down', 'ln1_g', 'ln1_b', 'w_in', 'conv_w', 'conv_b', 'lru_wa', 'lru_ba', 'lru_wx', 'lru_bx', 'lru_lambda', 'gla_w_g2', 'gla_b_g', 'gla_norm_g', 'fox_b_f', 'w_branch', 'w_out', 'ln2_g', 'ln2_b', 'ffn2_w_up', 'ffn2_w_down', 'ln3_g', 'ln3_b', 'ple_w_proj', 'ple_w_gate', 'ple_b_gate', 'ln4_g', 'ln4_b', 'loss_target', 'm_ffn1_w_up', 'm_ffn1_w_down', 'm_ln1_g', 'm_ln1_b', 'm_w_in', 'm_conv_w', 'm_conv_b', 'm_lru_wa', 'm_lru_ba', 'm_lru_wx', 'm_lru_bx', 'm_lru_lambda', 'm_gla_w_g2', 'm_gla_b_g', 'm_gla_norm_g', 'm_fox_b_f', 'm_w_branch', 'm_w_out', 'm_ln2_g', 'm_ln2_b', 'm_ffn2_w_up', 'm_ffn2_w_down', 'm_ln3_g', 'm_ln3_b', 'm_ple_w_proj', 'm_ple_w_gate', 'm_ple_b_gate', 'm_ln4_g', 'm_ln4_b', 'v_ffn1_w_up', 'v_ffn1_w_down', 'v_ln1_g', 'v_ln1_b', 'v_w_in', 'v_conv_w', 'v_conv_b', 'v_lru_wa', 'v_lru_ba', 'v_lru_wx', 'v_lru_bx', 'v_lru_lambda', 'v_gla_w_g2', 'v_gla_b_g', 'v_gla_norm_g', 'v_fox_b_f', 'v_w_branch', 'v_w_out', 'v_ln2_g', 'v_ln2_b', 'v_ffn2_w_up', 'v_ffn2_w_down', 'v_ln3_g', 'v_ln3_b', 'v_ple_w_proj', 'v_ple_w_gate', 'v_ple_b_gate', 'v_ln4_g', 'v_ln4_b']
TWIN_OUTPUTS = ['loss', 'grad_x', 'grad_ffn1_w_up', 'grad_ffn1_w_down', 'grad_ln1_g', 'grad_ln1_b', 'grad_w_in', 'grad_conv_w', 'grad_conv_b', 'grad_lru_wa', 'grad_lru_ba', 'grad_lru_wx', 'grad_lru_bx', 'grad_lru_lambda', 'grad_gla_w_g2', 'grad_gla_b_g', 'grad_gla_norm_g', 'grad_fox_b_f', 'grad_w_branch', 'grad_w_out', 'grad_ln2_g', 'grad_ln2_b', 'grad_ffn2_w_up', 'grad_ffn2_w_down', 'grad_ln3_g', 'grad_ln3_b', 'grad_ple_w_proj', 'grad_ple_w_gate', 'grad_ple_b_gate', 'grad_ln4_g', 'grad_ln4_b', 'delta_ffn1_w_up', 'delta_ffn1_w_down', 'delta_ln1_g', 'delta_ln1_b', 'delta_w_in', 'delta_conv_w', 'delta_conv_b', 'delta_lru_wa', 'delta_lru_ba', 'delta_lru_wx', 'delta_lru_bx', 'delta_lru_lambda', 'delta_gla_w_g2', 'delta_gla_b_g', 'delta_gla_norm_g', 'delta_fox_b_f', 'delta_w_branch', 'delta_w_out', 'delta_ln2_g', 'delta_ln2_b', 'delta_ffn2_w_up', 'delta_ffn2_w_down', 'delta_ln3_g', 'delta_ln3_b', 'delta_ple_w_proj', 'delta_ple_w_gate', 'delta_ple_b_gate', 'delta_ln4_g', 'delta_ln4_b', 'new_m_ffn1_w_up', 'new_m_ffn1_w_down', 'new_m_ln1_g', 'new_m_ln1_b', 'new_m_w_in', 'new_m_conv_w', 'new_m_conv_b', 'new_m_lru_wa', 'new_m_lru_ba', 'new_m_lru_wx', 'new_m_lru_bx', 'new_m_lru_lambda', 'new_m_gla_w_g2', 'new_m_gla_b_g', 'new_m_gla_norm_g', 'new_m_fox_b_f', 'new_m_w_branch', 'new_m_w_out', 'new_m_ln2_g', 'new_m_ln2_b', 'new_m_ffn2_w_up', 'new_m_ffn2_w_down', 'new_m_ln3_g', 'new_m_ln3_b', 'new_m_ple_w_proj', 'new_m_ple_w_gate', 'new_m_ple_b_gate', 'new_m_ln4_g', 'new_m_ln4_b', 'new_v_ffn1_w_up', 'new_v_ffn1_w_down', 'new_v_ln1_g', 'new_v_ln1_b', 'new_v_w_in', 'new_v_conv_w', 'new_v_conv_b', 'new_v_lru_wa', 'new_v_lru_ba', 'new_v_lru_wx', 'new_v_lru_bx', 'new_v_lru_lambda', 'new_v_gla_w_g2', 'new_v_gla_b_g', 'new_v_gla_norm_g', 'new_v_fox_b_f', 'new_v_w_branch', 'new_v_w_out', 'new_v_ln2_g', 'new_v_ln2_b', 'new_v_ffn2_w_up', 'new_v_ffn2_w_down', 'new_v_ln3_g', 'new_v_ln3_b', 'new_v_ple_w_proj', 'new_v_ple_w_gate', 'new_v_ple_b_gate', 'new_v_ln4_g', 'new_v_ln4_b']
TWIN_LEAF_KINDS = {'loss': 'loss', 'grad_x': 'grad_x', 'grad_ffn1_w_up': 'grad_w', 'grad_ffn1_w_down': 'grad_w', 'grad_ln1_g': 'grad_w', 'grad_ln1_b': 'grad_w', 'grad_w_in': 'grad_w', 'grad_conv_w': 'grad_w', 'grad_conv_b': 'grad_w', 'grad_lru_wa': 'grad_w', 'grad_lru_ba': 'grad_w', 'grad_lru_wx': 'grad_w', 'grad_lru_bx': 'grad_w', 'grad_lru_lambda': 'grad_w', 'grad_gla_w_g2': 'grad_w', 'grad_gla_b_g': 'grad_w', 'grad_gla_norm_g': 'grad_w', 'grad_fox_b_f': 'grad_w', 'grad_w_branch': 'grad_w', 'grad_w_out': 'grad_w', 'grad_ln2_g': 'grad_w', 'grad_ln2_b': 'grad_w', 'grad_ffn2_w_up': 'grad_w', 'grad_ffn2_w_down': 'grad_w', 'grad_ln3_g': 'grad_w', 'grad_ln3_b': 'grad_w', 'grad_ple_w_proj': 'grad_w', 'grad_ple_w_gate': 'grad_w', 'grad_ple_b_gate': 'grad_w', 'grad_ln4_g': 'grad_w', 'grad_ln4_b': 'grad_w', 'delta_ffn1_w_up': 'delta_w', 'delta_ffn1_w_down': 'delta_w', 'delta_ln1_g': 'delta_w', 'delta_ln1_b': 'delta_w', 'delta_w_in': 'delta_w', 'delta_conv_w': 'delta_w', 'delta_conv_b': 'delta_w', 'delta_lru_wa': 'delta_w', 'delta_lru_ba': 'delta_w', 'delta_lru_wx': 'delta_w', 'delta_lru_bx': 'delta_w', 'delta_lru_lambda': 'delta_w', 'delta_gla_w_g2': 'delta_w', 'delta_gla_b_g': 'delta_w', 'delta_gla_norm_g': 'delta_w', 'delta_fox_b_f': 'delta_w', 'delta_w_branch': 'delta_w', 'delta_w_out': 'delta_w', 'delta_ln2_g': 'delta_w', 'delta_ln2_b': 'delta_w', 'delta_ffn2_w_up': 'delta_w', 'delta_ffn2_w_down': 'delta_w', 'delta_ln3_g': 'delta_w', 'delta_ln3_b': 'delta_w', 'delta_ple_w_proj': 'delta_w', 'delta_ple_w_gate': 'delta_w', 'delta_ple_b_gate': 'delta_w', 'delta_ln4_g': 'delta_w', 'delta_ln4_b': 'delta_w', 'new_m_ffn1_w_up': 'new_m', 'new_m_ffn1_w_down': 'new_m', 'new_m_ln1_g': 'new_m', 'new_m_ln1_b': 'new_m', 'new_m_w_in': 'new_m', 'new_m_conv_w': 'new_m', 'new_m_conv_b': 'new_m', 'new_m_lru_wa': 'new_m', 'new_m_lru_ba': 'new_m', 'new_m_lru_wx': 'new_m', 'new_m_lru_bx': 'new_m', 'new_m_lru_lambda': 'new_m', 'new_m_gla_w_g2': 'new_m', 'new_m_gla_b_g': 'new_m', 'new_m_gla_norm_g': 'new_m', 'new_m_fox_b_f': 'new_m', 'new_m_w_branch': 'new_m', 'new_m_w_out': 'new_m', 'new_m_ln2_g': 'new_m', 'new_m_ln2_b': 'new_m', 'new_m_ffn2_w_up': 'new_m', 'new_m_ffn2_w_down': 'new_m', 'new_m_ln3_g': 'new_m', 'new_m_ln3_b': 'new_m', 'new_m_ple_w_proj': 'new_m', 'new_m_ple_w_gate': 'new_m', 'new_m_ple_b_gate': 'new_m', 'new_m_ln4_g': 'new_m', 'new_m_ln4_b': 'new_m', 'new_v_ffn1_w_up': 'new_v', 'new_v_ffn1_w_down': 'new_v', 'new_v_ln1_g': 'new_v', 'new_v_ln1_b': 'new_v', 'new_v_w_in': 'new_v', 'new_v_conv_w': 'new_v', 'new_v_conv_b': 'new_v', 'new_v_lru_wa': 'new_v', 'new_v_lru_ba': 'new_v', 'new_v_lru_wx': 'new_v', 'new_v_lru_bx': 'new_v', 'new_v_lru_lambda': 'new_v', 'new_v_gla_w_g2': 'new_v', 'new_v_gla_b_g': 'new_v', 'new_v_gla_norm_g': 'new_v', 'new_v_fox_b_f': 'new_v', 'new_v_w_branch': 'new_v', 'new_v_w_out': 'new_v', 'new_v_ln2_g': 'new_v', 'new_v_ln2_b': 'new_v', 'new_v_ffn2_w_up': 'new_v', 'new_v_ffn2_w_down': 'new_v', 'new_v_ln3_g': 'new_v', 'new_v_ln3_b': 'new_v', 'new_v_ple_w_proj': 'new_v', 'new_v_ple_w_gate': 'new_v', 'new_v_ple_b_gate': 'new_v', 'new_v_ln4_g': 'new_v', 'new_v_ln4_b': 'new_v'}


def _forward(args):
    return _fwd_reference(*[args[k] for k in FWD_PARAMS])


def _output_shape():
    def fwd():
        inp = _fwd_setup_inputs(0)
        return _fwd_reference(*[inp[k] for k in FWD_PARAMS])
    out = _jax.eval_shape(fwd)
    return out.shape, out.dtype

N_MICROBATCH = 1
ADAM_LR = 0.001
ADAM_B1 = 0.9
ADAM_B2 = 0.999
ADAM_EPS = 1e-08
ADAM_WD = 0.01
ADAM_STEP = 10
PER_EXAMPLE_BATCH_AXIS = {'x': 0, 'p': 1, 'loss_target': 0}
SHARED_INPUTS = []
_WEIGHT_DTYPES = {'ffn1_w_up': _jnp.float32, 'ffn1_w_down': _jnp.float32, 'ln1_g': _jnp.float32, 'ln1_b': _jnp.float32, 'w_in': _jnp.float32, 'conv_w': _jnp.float32, 'conv_b': _jnp.float32, 'lru_wa': _jnp.float32, 'lru_ba': _jnp.float32, 'lru_wx': _jnp.float32, 'lru_bx': _jnp.float32, 'lru_lambda': _jnp.float32, 'gla_w_g2': _jnp.float32, 'gla_b_g': _jnp.float32, 'gla_norm_g': _jnp.float32, 'fox_b_f': _jnp.float32, 'w_branch': _jnp.float32, 'w_out': _jnp.float32, 'ln2_g': _jnp.float32, 'ln2_b': _jnp.float32, 'ffn2_w_up': _jnp.float32, 'ffn2_w_down': _jnp.float32, 'ln3_g': _jnp.float32, 'ln3_b': _jnp.float32, 'ple_w_proj': _jnp.float32, 'ple_w_gate': _jnp.float32, 'ple_b_gate': _jnp.float32, 'ln4_g': _jnp.float32, 'ln4_b': _jnp.float32}
MOMENT_SCALE = {'ffn1_w_up': 1.659065e-02, 'ffn1_w_down': 5.415292e-02, 'ln1_g': 2.065251e+00, 'ln1_b': 1.135416e+00, 'w_in': 2.768290e-02, 'conv_w': 4.529209e-02, 'conv_b': 5.733412e-01, 'lru_wa': 1.654038e-02, 'lru_ba': 1.470422e-02, 'lru_wx': 3.042606e-02, 'lru_bx': 1.349917e-02, 'lru_lambda': 2.325981e-02, 'gla_w_g2': 6.702110e-03, 'gla_b_g': 2.592947e-02, 'gla_norm_g': 4.016829e-02, 'fox_b_f': 2.256110e-01, 'w_branch': 2.540995e-02, 'w_out': 8.794451e-02, 'ln2_g': 2.113140e+00, 'ln2_b': 1.022316e+00, 'ffn2_w_up': 1.607055e-02, 'ffn2_w_down': 5.259768e-02, 'ln3_g': 2.149061e+00, 'ln3_b': 1.032670e+00, 'ple_w_proj': 9.507511e-02, 'ple_w_gate': 1.862648e-02, 'ple_b_gate': 2.499947e-02, 'ln4_g': 4.547828e+01, 'ln4_b': 2.750209e+00}


def _to_microbatches(a, axis):
    t = _jnp.moveaxis(a, axis, 0)
    t = t.reshape((N_MICROBATCH, t.shape[0] // N_MICROBATCH) + t.shape[1:])
    return _jnp.moveaxis(t, 1, axis + 1)


def setup_inputs(seed: int = 0) -> dict:
    inp = _fwd_setup_inputs(seed)
    key = _jax.random.fold_in(_jax.random.key(seed), 7919)
    shape, _ = _output_shape()
    out = dict(inp)
    out["loss_target"] = _jax.random.normal(_jax.random.fold_in(key, 0), shape, _jnp.float32)
    for i, name in enumerate(TWIN_WEIGHTS):
        w = inp[name].astype(_jnp.float32)
        if MOMENT_SCALE is None:
            s = _jnp.sqrt(_jnp.mean(_jnp.square(w)) + 1e-30)
        else:
            s = MOMENT_SCALE[name]
        km, kv = _jax.random.split(_jax.random.fold_in(key, i + 1))
        out[name] = w
        out["m_" + name] = s * _jax.random.normal(km, w.shape, _jnp.float32)
        out["v_" + name] = (s * s) * _jax.random.uniform(kv, w.shape, _jnp.float32, 0.5, 1.5)
    if N_MICROBATCH > 1:
        for name, axis in PER_EXAMPLE_BATCH_AXIS.items():
            out[name] = _to_microbatches(out[name], axis)
    return {'x': out['x'], 'p': out['p'], 'ffn1_w_up': out['ffn1_w_up'], 'ffn1_w_down': out['ffn1_w_down'], 'ln1_g': out['ln1_g'], 'ln1_b': out['ln1_b'], 'w_in': out['w_in'], 'conv_w': out['conv_w'], 'conv_b': out['conv_b'], 'lru_wa': out['lru_wa'], 'lru_ba': out['lru_ba'], 'lru_wx': out['lru_wx'], 'lru_bx': out['lru_bx'], 'lru_lambda': out['lru_lambda'], 'gla_w_g2': out['gla_w_g2'], 'gla_b_g': out['gla_b_g'], 'gla_norm_g': out['gla_norm_g'], 'fox_b_f': out['fox_b_f'], 'w_branch': out['w_branch'], 'w_out': out['w_out'], 'ln2_g': out['ln2_g'], 'ln2_b': out['ln2_b'], 'ffn2_w_up': out['ffn2_w_up'], 'ffn2_w_down': out['ffn2_w_down'], 'ln3_g': out['ln3_g'], 'ln3_b': out['ln3_b'], 'ple_w_proj': out['ple_w_proj'], 'ple_w_gate': out['ple_w_gate'], 'ple_b_gate': out['ple_b_gate'], 'ln4_g': out['ln4_g'], 'ln4_b': out['ln4_b'], 'loss_target': out['loss_target'], 'm_ffn1_w_up': out['m_ffn1_w_up'], 'm_ffn1_w_down': out['m_ffn1_w_down'], 'm_ln1_g': out['m_ln1_g'], 'm_ln1_b': out['m_ln1_b'], 'm_w_in': out['m_w_in'], 'm_conv_w': out['m_conv_w'], 'm_conv_b': out['m_conv_b'], 'm_lru_wa': out['m_lru_wa'], 'm_lru_ba': out['m_lru_ba'], 'm_lru_wx': out['m_lru_wx'], 'm_lru_bx': out['m_lru_bx'], 'm_lru_lambda': out['m_lru_lambda'], 'm_gla_w_g2': out['m_gla_w_g2'], 'm_gla_b_g': out['m_gla_b_g'], 'm_gla_norm_g': out['m_gla_norm_g'], 'm_fox_b_f': out['m_fox_b_f'], 'm_w_branch': out['m_w_branch'], 'm_w_out': out['m_w_out'], 'm_ln2_g': out['m_ln2_g'], 'm_ln2_b': out['m_ln2_b'], 'm_ffn2_w_up': out['m_ffn2_w_up'], 'm_ffn2_w_down': out['m_ffn2_w_down'], 'm_ln3_g': out['m_ln3_g'], 'm_ln3_b': out['m_ln3_b'], 'm_ple_w_proj': out['m_ple_w_proj'], 'm_ple_w_gate': out['m_ple_w_gate'], 'm_ple_b_gate': out['m_ple_b_gate'], 'm_ln4_g': out['m_ln4_g'], 'm_ln4_b': out['m_ln4_b'], 'v_ffn1_w_up': out['v_ffn1_w_up'], 'v_ffn1_w_down': out['v_ffn1_w_down'], 'v_ln1_g': out['v_ln1_g'], 'v_ln1_b': out['v_ln1_b'], 'v_w_in': out['v_w_in'], 'v_conv_w': out['v_conv_w'], 'v_conv_b': out['v_conv_b'], 'v_lru_wa': out['v_lru_wa'], 'v_lru_ba': out['v_lru_ba'], 'v_lru_wx': out['v_lru_wx'], 'v_lru_bx': out['v_lru_bx'], 'v_lru_lambda': out['v_lru_lambda'], 'v_gla_w_g2': out['v_gla_w_g2'], 'v_gla_b_g': out['v_gla_b_g'], 'v_gla_norm_g': out['v_gla_norm_g'], 'v_fox_b_f': out['v_fox_b_f'], 'v_w_branch': out['v_w_branch'], 'v_w_out': out['v_w_out'], 'v_ln2_g': out['v_ln2_g'], 'v_ln2_b': out['v_ln2_b'], 'v_ffn2_w_up': out['v_ffn2_w_up'], 'v_ffn2_w_down': out['v_ffn2_w_down'], 'v_ln3_g': out['v_ln3_g'], 'v_ln3_b': out['v_ln3_b'], 'v_ple_w_proj': out['v_ple_w_proj'], 'v_ple_w_gate': out['v_ple_w_gate'], 'v_ple_b_gate': out['v_ple_b_gate'], 'v_ln4_g': out['v_ln4_g'], 'v_ln4_b': out['v_ln4_b']}


def _loss(weights, diff, rest, loss_target):
    with _jax.named_scope("forward"):
        args = {**rest, TWIN_DIFF_INPUT: diff, **{k: w.astype(_WEIGHT_DTYPES[k]) for k, w in weights.items()}}
        y = _forward(args)
    with _jax.named_scope("loss_head"):
        err = _jnp.square(y.astype(_jnp.float32) - loss_target)
        return 0.5 * _jnp.sum(_jnp.mean(err, axis=-1)) if err.ndim else 0.5 * err


def _adamw(w, g, m, v):
    m = ADAM_B1 * m + (1.0 - ADAM_B1) * g
    v = ADAM_B2 * v + (1.0 - ADAM_B2) * _jnp.square(g)
    m_hat = m / (1.0 - ADAM_B1 ** ADAM_STEP)
    v_hat = v / (1.0 - ADAM_B2 ** ADAM_STEP)
    delta = -ADAM_LR * (m_hat / (_jnp.sqrt(v_hat) + ADAM_EPS) + ADAM_WD * w)
    return delta, m, v


def reference(x, p, ffn1_w_up, ffn1_w_down, ln1_g, ln1_b, w_in, conv_w, conv_b, lru_wa, lru_ba, lru_wx, lru_bx, lru_lambda, gla_w_g2, gla_b_g, gla_norm_g, fox_b_f, w_branch, w_out, ln2_g, ln2_b, ffn2_w_up, ffn2_w_down, ln3_g, ln3_b, ple_w_proj, ple_w_gate, ple_b_gate, ln4_g, ln4_b, loss_target, m_ffn1_w_up, m_ffn1_w_down, m_ln1_g, m_ln1_b, m_w_in, m_conv_w, m_conv_b, m_lru_wa, m_lru_ba, m_lru_wx, m_lru_bx, m_lru_lambda, m_gla_w_g2, m_gla_b_g, m_gla_norm_g, m_fox_b_f, m_w_branch, m_w_out, m_ln2_g, m_ln2_b, m_ffn2_w_up, m_ffn2_w_down, m_ln3_g, m_ln3_b, m_ple_w_proj, m_ple_w_gate, m_ple_b_gate, m_ln4_g, m_ln4_b, v_ffn1_w_up, v_ffn1_w_down, v_ln1_g, v_ln1_b, v_w_in, v_conv_w, v_conv_b, v_lru_wa, v_lru_ba, v_lru_wx, v_lru_bx, v_lru_lambda, v_gla_w_g2, v_gla_b_g, v_gla_norm_g, v_fox_b_f, v_w_branch, v_w_out, v_ln2_g, v_ln2_b, v_ffn2_w_up, v_ffn2_w_down, v_ln3_g, v_ln3_b, v_ple_w_proj, v_ple_w_gate, v_ple_b_gate, v_ln4_g, v_ln4_b):
    given = dict(x=x, p=p, ffn1_w_up=ffn1_w_up, ffn1_w_down=ffn1_w_down, ln1_g=ln1_g, ln1_b=ln1_b, w_in=w_in, conv_w=conv_w, conv_b=conv_b, lru_wa=lru_wa, lru_ba=lru_ba, lru_wx=lru_wx, lru_bx=lru_bx, lru_lambda=lru_lambda, gla_w_g2=gla_w_g2, gla_b_g=gla_b_g, gla_norm_g=gla_norm_g, fox_b_f=fox_b_f, w_branch=w_branch, w_out=w_out, ln2_g=ln2_g, ln2_b=ln2_b, ffn2_w_up=ffn2_w_up, ffn2_w_down=ffn2_w_down, ln3_g=ln3_g, ln3_b=ln3_b, ple_w_proj=ple_w_proj, ple_w_gate=ple_w_gate, ple_b_gate=ple_b_gate, ln4_g=ln4_g, ln4_b=ln4_b, loss_target=loss_target, m_ffn1_w_up=m_ffn1_w_up, m_ffn1_w_down=m_ffn1_w_down, m_ln1_g=m_ln1_g, m_ln1_b=m_ln1_b, m_w_in=m_w_in, m_conv_w=m_conv_w, m_conv_b=m_conv_b, m_lru_wa=m_lru_wa, m_lru_ba=m_lru_ba, m_lru_wx=m_lru_wx, m_lru_bx=m_lru_bx, m_lru_lambda=m_lru_lambda, m_gla_w_g2=m_gla_w_g2, m_gla_b_g=m_gla_b_g, m_gla_norm_g=m_gla_norm_g, m_fox_b_f=m_fox_b_f, m_w_branch=m_w_branch, m_w_out=m_w_out, m_ln2_g=m_ln2_g, m_ln2_b=m_ln2_b, m_ffn2_w_up=m_ffn2_w_up, m_ffn2_w_down=m_ffn2_w_down, m_ln3_g=m_ln3_g, m_ln3_b=m_ln3_b, m_ple_w_proj=m_ple_w_proj, m_ple_w_gate=m_ple_w_gate, m_ple_b_gate=m_ple_b_gate, m_ln4_g=m_ln4_g, m_ln4_b=m_ln4_b, v_ffn1_w_up=v_ffn1_w_up, v_ffn1_w_down=v_ffn1_w_down, v_ln1_g=v_ln1_g, v_ln1_b=v_ln1_b, v_w_in=v_w_in, v_conv_w=v_conv_w, v_conv_b=v_conv_b, v_lru_wa=v_lru_wa, v_lru_ba=v_lru_ba, v_lru_wx=v_lru_wx, v_lru_bx=v_lru_bx, v_lru_lambda=v_lru_lambda, v_gla_w_g2=v_gla_w_g2, v_gla_b_g=v_gla_b_g, v_gla_norm_g=v_gla_norm_g, v_fox_b_f=v_fox_b_f, v_w_branch=v_w_branch, v_w_out=v_w_out, v_ln2_g=v_ln2_g, v_ln2_b=v_ln2_b, v_ffn2_w_up=v_ffn2_w_up, v_ffn2_w_down=v_ffn2_w_down, v_ln3_g=v_ln3_g, v_ln3_b=v_ln3_b, v_ple_w_proj=v_ple_w_proj, v_ple_w_gate=v_ple_w_gate, v_ple_b_gate=v_ple_b_gate, v_ln4_g=v_ln4_g, v_ln4_b=v_ln4_b)
    weights = {n: given[n] for n in TWIN_WEIGHTS}
    shared = {n: given[n] for n in SHARED_INPUTS}
    per_example = {n: given[n] for n in ['x', 'p']}
    grad_fn = _jax.value_and_grad(_loss, argnums=(0, 1))

    def one_microbatch(ex, loss_target):
        ex = dict(ex)
        diff = ex.pop(TWIN_DIFF_INPUT)
        return grad_fn(weights, diff, {**shared, **ex}, loss_target)

    if N_MICROBATCH == 1:
        loss, (grad_w, grad_x) = one_microbatch(per_example, given["loss_target"])
    else:
        def body(carry, xs):
            loss_sum, grad_sum = carry
            l_k, (gw_k, gx_k) = one_microbatch(xs[0], xs[1])
            with _jax.named_scope("update"):
                return (loss_sum + l_k, _jax.tree.map(_jnp.add, grad_sum, gw_k)), gx_k

        init = (_jnp.zeros((), _jnp.float32), _jax.tree.map(_jnp.zeros_like, weights))
        (loss, grad_w), grad_x = _jax.lax.scan(body, init, (per_example, given["loss_target"]))
    with _jax.named_scope("update"):
        delta_w, new_m, new_v = {}, {}, {}
        for n in TWIN_WEIGHTS:
            delta_w[n], new_m[n], new_v[n] = _adamw(weights[n], grad_w[n], given["m_" + n], given["v_" + n])
    return (loss, grad_x, *[grad_w[n] for n in TWIN_WEIGHTS], *[delta_w[n] for n in TWIN_WEIGHTS],
            *[new_m[n] for n in TWIN_WEIGHTS], *[new_v[n] for n in TWIN_WEIGHTS])
```

```python
import functools
import math

import jax
import jax.numpy as jnp
from jax import lax
from jax.experimental import pallas as pl
from jax.experimental.pallas import tpu as pltpu

F32 = jnp.float32
BF16 = jnp.bfloat16

D = 1024
DFF = 2816
BW = 512
PLE = 256
DEPTH = 2
ALPHA = (2 * DEPTH) ** 0.25
LN_EPS = 1e-5
RMS_EPS = 1e-6
LRU_C = 8.0
GLA_TAU = 16.0
CHUNK = 64
D_IN = 7192
ZW = 7296
AX, AY, BQ, BK, BV, BR, CQ, CK, CV, G0, MISC = 0, 512, 1024, 1280, 1536, 2048, 2560, 3072, 3584, 4096, 7168
LOW_W, FOX_H = 16, 8
ADAM_LR, ADAM_B1, ADAM_B2, ADAM_EPS, ADAM_WD, ADAM_STEP = 0.001, 0.9, 0.999, 1e-08, 0.01, 10
PACK_W = 1024
PACK_ROWS = 14336
VMEM_LIMIT = 56 << 20

MESH = pl.DeviceIdType.MESH
ANY = pl.BlockSpec(memory_space=pl.ANY)


def _pcall(body, **kw):
    return pl.pallas_call(body, **kw)


def _cp(*dims):
    return pltpu.CompilerParams(dimension_semantics=dims, vmem_limit_bytes=VMEM_LIMIT)


def _dot(a, b):
    return jnp.dot(a, b, preferred_element_type=F32)


def _dot_nt(a, b):
    return lax.dot_general(a, b, (((1,), (1,)), ((), ())), preferred_element_type=F32)


def _dot_tn(a, b):
    return lax.dot_general(a, b, (((0,), (0,)), ((), ())), preferred_element_type=F32)


def _dot_hi(a, b):
    return jnp.dot(a, b, preferred_element_type=F32, precision=lax.Precision.HIGHEST)


def _sigmoid(x):
    return 1.0 / (1.0 + jnp.exp(-x))


def _softplus(x):
    return jnp.maximum(x, 0.0) + jnp.log(1.0 + jnp.exp(-jnp.abs(x)))


def _log_sigmoid(x):
    return -_softplus(-x)


def _expm1(x):
    poly = x * (1.0 + x * (0.5 + x * (1.0 / 6.0 + x * (1.0 / 24.0 + x * (1.0 / 120.0 + x * (1.0 / 720.0))))))
    return jnp.where(jnp.abs(x) < 0.1, poly, jnp.exp(x) - 1.0)


_GELU_C = math.sqrt(2.0 / math.pi)


def _gelu(x):
    return 0.5 * x * (1.0 + jnp.tanh(_GELU_C * (x + 0.044715 * x * x * x)))


def _gelu_grad(x):
    t = jnp.tanh(_GELU_C * (x + 0.044715 * x * x * x))
    return 0.5 * (1.0 + t) + 0.5 * x * (1.0 - t * t) * _GELU_C * (1.0 + 3.0 * 0.044715 * x * x)


def _ln_stats(r):
    mu = jnp.mean(r, axis=-1, keepdims=True)
    xc = r - mu
    var = jnp.mean(xc * xc, axis=-1, keepdims=True)
    return xc, lax.rsqrt(var + LN_EPS)


def _pick(n, cands):
    for c in cands:
        if n % c == 0:
            return c
    return n


def _rows(tm, w, col=0):
    return pl.BlockSpec((tm, w), lambda i: (i, col))


def _fix(shape):
    nd = len(shape)
    return pl.BlockSpec(shape, lambda i: (0,) * nd)


def matmul(a, b, *, name, res=None, res_scale=1.0, also_bf16=False, tm=512, tn=512, tk=None):
    m, k = a.shape
    n = b.shape[1]
    tm, tn = min(tm, m), min(tn, n)
    tk = k if tk is None else tk
    nk = k // tk
    has_res = res is not None

    def body(*refs):
        a_ref, b_ref = refs[0], refs[1]
        pos = 2
        r_ref = None
        if has_res:
            r_ref = refs[pos]
            pos += 1
        o_ref = refs[pos]
        pos += 1
        ob_ref = None
        if also_bf16:
            ob_ref = refs[pos]
            pos += 1
        acc = refs[pos]
        kk = pl.program_id(2)

        @pl.when(kk == 0)
        def _():
            acc[...] = jnp.zeros_like(acc)

        acc[...] += _dot(a_ref[...], b_ref[...])

        @pl.when(kk == nk - 1)
        def _():
            v = acc[...]
            if has_res:
                v = v + res_scale * r_ref[...]
            o_ref[...] = v
            if also_bf16:
                ob_ref[...] = v.astype(BF16)

    in_specs = [pl.BlockSpec((tm, tk), lambda j, i, kk: (i, kk)),
                pl.BlockSpec((tk, tn), lambda j, i, kk: (kk, j))]
    args = [a, b]
    if has_res:
        in_specs.append(pl.BlockSpec((tm, tn), lambda j, i, kk: (i, j)))
        args.append(res)
    out_shape = [jax.ShapeDtypeStruct((m, n), F32)]
    out_specs = [pl.BlockSpec((tm, tn), lambda j, i, kk: (i, j))]
    if also_bf16:
        out_shape.append(jax.ShapeDtypeStruct((m, n), BF16))
        out_specs.append(pl.BlockSpec((tm, tn), lambda j, i, kk: (i, j)))
    out = _pcall(body, name=name, grid=(n // tn, m // tm, nk), in_specs=in_specs, out_specs=out_specs,
                 out_shape=out_shape, scratch_shapes=[pltpu.VMEM((tm, tn), F32)],
                 compiler_params=_cp("parallel", "parallel", "arbitrary"))(*args)
    return out if also_bf16 else out[0]


def matmul_tn(a, b, *, name):
    t, k = a.shape
    n = b.shape[1]
    tt = min(512, t)
    tk = _pick(k, (512, 256, 128))
    tn = _pick(n, (1024, 1408, 2432, 512, 256, 128))
    nt = t // tt

    def body(a_ref, b_ref, o_ref):
        @pl.when(pl.program_id(2) == 0)
        def _():
            o_ref[...] = jnp.zeros_like(o_ref)

        o_ref[...] += _dot_tn(a_ref[...], b_ref[...])

    return _pcall(body, name=name, grid=(k // tk, n // tn, nt),
                  in_specs=[pl.BlockSpec((tt, tk), lambda i, j, s: (s, i)),
                            pl.BlockSpec((tt, tn), lambda i, j, s: (s, j))],
                  out_specs=pl.BlockSpec((tk, tn), lambda i, j, s: (i, j)),
                  out_shape=jax.ShapeDtypeStruct((k, n), F32),
                  compiler_params=_cp("parallel", "parallel", "arbitrary"))(a, b)


def ffn_up(xb, wg, wu, *, name):
    t = xb.shape[0]
    tm, tn = min(256, t), 1408

    def body(x_ref, wg_ref, wu_ref, g_ref, u_ref, a_ref):
        x = x_ref[...]
        g = _dot(x, wg_ref[...])
        u = _dot(x, wu_ref[...])
        g_ref[...] = g
        u_ref[...] = u
        a_ref[...] = (g * _sigmoid(g) * u).astype(BF16)

    blk = pl.BlockSpec((tm, tn), lambda j, i: (i, j))
    return _pcall(body, name=name, grid=(DFF // tn, t // tm),
                  in_specs=[pl.BlockSpec((tm, D), lambda j, i: (i, 0)),
                            pl.BlockSpec((D, tn), lambda j, i: (0, j)),
                            pl.BlockSpec((D, tn), lambda j, i: (0, j))],
                  out_specs=[blk, blk, blk],
                  out_shape=[jax.ShapeDtypeStruct((t, DFF), F32), jax.ShapeDtypeStruct((t, DFF), F32),
                             jax.ShapeDtypeStruct((t, DFF), BF16)],
                  compiler_params=_cp("parallel", "parallel"))(xb, wg, wu)


def matmul_res_ln(a, w, res, g, b, *, mm_scale, name):
    t, k = a.shape
    tm = min(256, t)

    def body(a_ref, w_ref, res_ref, g_ref, b_ref, r_ref, y_ref, yb_ref):
        f = _dot(a_ref[...], w_ref[...])
        r = ALPHA * res_ref[...] + mm_scale * f
        xc, rstd = _ln_stats(r)
        y = xc * rstd * g_ref[...] + b_ref[...]
        r_ref[...] = r
        y_ref[...] = y
        yb_ref[...] = y.astype(BF16)

    return _pcall(body, name=name, grid=(t // tm,),
                  in_specs=[_rows(tm, k), _fix((k, D)), _rows(tm, D), _fix((1, D)), _fix((1, D))],
                  out_specs=[_rows(tm, D)] * 3,
                  out_shape=[jax.ShapeDtypeStruct((t, D), F32), jax.ShapeDtypeStruct((t, D), F32),
                             jax.ShapeDtypeStruct((t, D), BF16)],
                  compiler_params=_cp("parallel"))(a, w, res, g, b)


def ln_bwd(dy, r, g, *, out_scale, name):
    t = dy.shape[0]
    tm = min(256, t)

    def body(dy_ref, r_ref, g_ref, dr_ref, drb_ref, dg_ref, db_ref):
        @pl.when(pl.program_id(0) == 0)
        def _():
            dg_ref[...] = jnp.zeros_like(dg_ref)
            db_ref[...] = jnp.zeros_like(db_ref)

        xc, rstd = _ln_stats(r_ref[...])
        xhat = xc * rstd
        d = dy_ref[...]
        dxh = d * g_ref[...]
        dr = rstd * (dxh - jnp.mean(dxh, axis=-1, keepdims=True)
                     - xhat * jnp.mean(dxh * xhat, axis=-1, keepdims=True))
        dr_ref[...] = dr
        drb_ref[...] = (out_scale * dr).astype(BF16)
        dg_ref[...] += jnp.sum(d * xhat, axis=0, keepdims=True)
        db_ref[...] += jnp.sum(d, axis=0, keepdims=True)

    return _pcall(body, name=name, grid=(t // tm,),
                  in_specs=[_rows(tm, D), _rows(tm, D), _fix((1, D))],
                  out_specs=[_rows(tm, D), _rows(tm, D), _fix((1, D)), _fix((1, D))],
                  out_shape=[jax.ShapeDtypeStruct((t, D), F32), jax.ShapeDtypeStruct((t, D), BF16),
                             jax.ShapeDtypeStruct((1, D), F32), jax.ShapeDtypeStruct((1, D), F32)],
                  compiler_params=_cp("arbitrary"))(dy, r, g)


def ffn_down_bwd(dfb, wdt, gate, up, *, name):
    t = dfb.shape[0]
    tm, tn = min(256, t), 1408
    nj = DFF // tn

    def body(df_ref, w_ref, g_ref, u_ref, dg_ref, du_ref):
        da = _dot(df_ref[...], w_ref[...])
        g = g_ref[...]
        s = _sigmoid(g)
        dg_ref[...] = (da * u_ref[...] * s * (1.0 + g * (1.0 - s))).astype(BF16)
        du_ref[...] = (da * g * s).astype(BF16)

    blk = pl.BlockSpec((tm, tn), lambda j, i: (i, j))
    return _pcall(body, name=name, grid=(nj, t // tm),
                  in_specs=[pl.BlockSpec((tm, D), lambda j, i: (i, 0)),
                            pl.BlockSpec((D, tn), lambda j, i: (0, j)), blk, blk],
                  out_specs=[blk, blk],
                  out_shape=[jax.ShapeDtypeStruct((t, DFF), BF16), jax.ShapeDtypeStruct((t, DFF), BF16)],
                  compiler_params=_cp("parallel", "parallel"))(dfb, wdt, gate, up)


def ple_fwd(xb, x, pb, wgate, wproj, bgate, g, b, *, name):
    t = x.shape[0]
    tm = min(256, t)

    def body(xb_ref, x_ref, p_ref, wg_ref, wp_ref, bg_ref, g_ref, b_ref, r_ref, y_ref, yb_ref):
        gl = _dot(xb_ref[...], wg_ref[...]) + bg_ref[...]
        pe = _dot(p_ref[...], wp_ref[...])
        r = ALPHA * x_ref[...] + _sigmoid(gl) * pe
        xc, rstd = _ln_stats(r)
        y = xc * rstd * g_ref[...] + b_ref[...]
        r_ref[...] = r
        y_ref[...] = y
        yb_ref[...] = y.astype(BF16)

    return _pcall(body, name=name, grid=(t // tm,),
                  in_specs=[_rows(tm, D), _rows(tm, D), _rows(tm, PLE), _fix((D, D)), _fix((PLE, D)),
                            _fix((1, D)), _fix((1, D)), _fix((1, D))],
                  out_specs=[_rows(tm, D)] * 3,
                  out_shape=[jax.ShapeDtypeStruct((t, D), F32), jax.ShapeDtypeStruct((t, D), F32),
                             jax.ShapeDtypeStruct((t, D), BF16)],
                  compiler_params=_cp("parallel"))(xb, x, pb, wgate, wproj, bgate, g, b)


def ple_bwd(dy, r, xb, pb, wgate, wproj, bgate, g, *, name):
    t = dy.shape[0]
    tm = min(256, t)

    def body(dy_ref, r_ref, xb_ref, p_ref, wg_ref, wp_ref, bg_ref, g_ref,
             dr_ref, dgl_ref, dpe_ref, dg_ref, db_ref, dbg_ref):
        @pl.when(pl.program_id(0) == 0)
        def _():
            dg_ref[...] = jnp.zeros_like(dg_ref)
            db_ref[...] = jnp.zeros_like(db_ref)
            dbg_ref[...] = jnp.zeros_like(dbg_ref)

        xc, rstd = _ln_stats(r_ref[...])
        xhat = xc * rstd
        d = dy_ref[...]
        dxh = d * g_ref[...]
        dr = rstd * (dxh - jnp.mean(dxh, axis=-1, keepdims=True)
                     - xhat * jnp.mean(dxh * xhat, axis=-1, keepdims=True))
        s = _sigmoid(_dot(xb_ref[...], wg_ref[...]) + bg_ref[...])
        pe = _dot(p_ref[...], wp_ref[...])
        dgl = dr * pe * s * (1.0 - s)
        dr_ref[...] = dr
        dgl_ref[...] = dgl.astype(BF16)
        dpe_ref[...] = (dr * s).astype(BF16)
        dg_ref[...] += jnp.sum(d * xhat, axis=0, keepdims=True)
        db_ref[...] += jnp.sum(d, axis=0, keepdims=True)
        dbg_ref[...] += jnp.sum(dgl, axis=0, keepdims=True)

    vec = jax.ShapeDtypeStruct((1, D), F32)
    return _pcall(body, name=name, grid=(t // tm,),
                  in_specs=[_rows(tm, D), _rows(tm, D), _rows(tm, D), _rows(tm, PLE), _fix((D, D)),
                            _fix((PLE, D)), _fix((1, D)), _fix((1, D))],
                  out_specs=[_rows(tm, D), _rows(tm, D), _rows(tm, D), _fix((1, D)), _fix((1, D)), _fix((1, D))],
                  out_shape=[jax.ShapeDtypeStruct((t, D), F32), jax.ShapeDtypeStruct((t, D), BF16),
                             jax.ShapeDtypeStruct((t, D), BF16), vec, vec, vec],
                  compiler_params=_cp("arbitrary"))(dy, r, xb, pb, wgate, wproj, bgate, g)


def loss_head(y, tgt, *, name):
    t = y.shape[0]
    tm = min(256, t)

    def body(y_ref, t_ref, dy_ref, sq_ref):
        @pl.when(pl.program_id(0) == 0)
        def _():
            sq_ref[...] = jnp.zeros_like(sq_ref)

        e = y_ref[...] - t_ref[...]
        dy_ref[...] = e / float(D)
        sq_ref[...] += jnp.sum(e * e, axis=0, keepdims=True)

    return _pcall(body, name=name, grid=(t // tm,),
                  in_specs=[_rows(tm, D), _rows(tm, D)],
                  out_specs=[_rows(tm, D), _fix((1, D))],
                  out_shape=[jax.ShapeDtypeStruct((t, D), F32), jax.ShapeDtypeStruct((1, D), F32)],
                  compiler_params=_cp("arbitrary"))(y, tgt)


def _lru_gates(xc, wa_ref, wx_ref, ba_ref, bx_ref, lam_ref):
    xcb = xc.astype(BF16)
    r = _sigmoid(_dot(xcb, wa_ref[...]) + ba_ref[...])
    ig = _sigmoid(_dot(xcb, wx_ref[...]) + bx_ref[...])
    sp = _softplus(-lam_ref[...])
    la = -LRU_C * r * sp
    a = jnp.exp(la)
    mult = jnp.sqrt(-_expm1(2.0 * la))
    return r, ig, sp, la, a, mult


def lru_fwd(z, cw, cb, wa, wx, ba, bx, lam, *, name):
    t = z.shape[0]
    tm = min(256, t)
    hb = tm // 8

    def body(ax_ref, prev_ref, ay_ref, cw_ref, cb_ref, wa_ref, wx_ref, ba_ref, bx_ref, lam_ref,
             xc_ref, xcb_ref, h_ref, ya_ref, xs, a_s, b_s, hc):
        i = pl.program_id(0)

        @pl.when(i == 0)
        def _():
            hc[...] = jnp.zeros_like(hc)

        xs[0:8, :] = jnp.where(i == 0, 0.0, prev_ref[...])
        xs[8:, :] = ax_ref[...]
        xc = cb_ref[...] + cw_ref[0:1, :] * xs[5:5 + tm, :]
        for k in range(1, 4):
            xc = xc + cw_ref[k:k + 1, :] * xs[5 + k:5 + k + tm, :]
        r, ig, sp, la, a, mult = _lru_gates(xc, wa_ref, wx_ref, ba_ref, bx_ref, lam_ref)
        a_s[...] = a
        b_s[...] = mult * (ig * xc)
        xc_ref[...] = xc
        xcb_ref[...] = xc.astype(BF16)

        def step(g, h):
            base = pl.multiple_of(g * 8, 8)
            a8 = a_s[pl.ds(base, 8), :]
            b8 = b_s[pl.ds(base, 8), :]
            for j in range(8):
                h = a8[j:j + 1, :] * h + b8[j:j + 1, :]
                h_ref[pl.ds(base + j, 1), :] = h
            return h

        hc[...] = lax.fori_loop(0, tm // 8, step, hc[...])
        ya_ref[...] = (_gelu(ay_ref[...]) * h_ref[...]).astype(BF16)

    vec = _fix((1, BW))
    return _pcall(body, name=name, grid=(t // tm,),
                  in_specs=[_rows(tm, BW, AX // BW),
                            pl.BlockSpec((8, BW), lambda i: (jnp.maximum(i * hb - 1, 0), AX // BW)),
                            _rows(tm, BW, AY // BW), _fix((4, BW)), vec, _fix((BW, BW)), _fix((BW, BW)),
                            vec, vec, vec],
                  out_specs=[_rows(tm, BW)] * 4,
                  out_shape=[jax.ShapeDtypeStruct((t, BW), F32), jax.ShapeDtypeStruct((t, BW), BF16),
                             jax.ShapeDtypeStruct((t, BW), F32), jax.ShapeDtypeStruct((t, BW), BF16)],
                  scratch_shapes=[pltpu.VMEM((tm + 8, BW), F32), pltpu.VMEM((tm, BW), F32),
                                  pltpu.VMEM((tm, BW), F32), pltpu.VMEM((1, BW), F32)],
                  compiler_params=_cp("arbitrary"))(z, z, z, cw, cb, wa, wx, ba, bx, lam)


def lru_bwd(dya, z, h, xc, wa, wx, wat, wxt, ba, bx, lam, *, name):
    t = dya.shape[0]
    tm = min(256, t)
    nb = t // tm
    hb = tm // 8

    def body(dya_ref, ay_ref, h_ref, hprev_ref, xc_ref, wa_ref, wx_ref, wat_ref, wxt_ref, ba_ref, bx_ref,
             lam_ref, day_ref, dxc_ref, dpr_ref, dpi_ref, dba_ref, dbx_ref, dlam_ref,
             hs, a_s, g_s, d_s, cc):
        i = pl.program_id(0)

        @pl.when(i == 0)
        def _():
            cc[...] = jnp.zeros_like(cc)
            dba_ref[...] = jnp.zeros_like(dba_ref)
            dbx_ref[...] = jnp.zeros_like(dbx_ref)
            dlam_ref[...] = jnp.zeros_like(dlam_ref)

        xc = xc_ref[...]
        r, ig, sp, la, a, mult = _lru_gates(xc, wa_ref, wx_ref, ba_ref, bx_ref, lam_ref)
        ay = ay_ref[...]
        dya = dya_ref[...]
        hcur = h_ref[...]
        day_ref[...] = (dya * hcur * _gelu_grad(ay)).astype(BF16)
        a_s[...] = a
        g_s[...] = dya * _gelu(ay)

        def step(gg, cin):
            g = tm // 8 - 1 - gg
            base = pl.multiple_of(g * 8, 8)
            a8 = a_s[pl.ds(base, 8), :]
            g8 = g_s[pl.ds(base, 8), :]
            for j in range(7, -1, -1):
                d = g8[j:j + 1, :] + cin
                d_s[pl.ds(base + j, 1), :] = d
                cin = a8[j:j + 1, :] * d
            return cin

        cc[...] = lax.fori_loop(0, tm // 8, step, cc[...])
        dht = d_s[...]
        hs[0:8, :] = jnp.where(i == nb - 1, 0.0, hprev_ref[...])
        hs[8:, :] = hcur
        da = dht * hs[7:7 + tm, :]
        dmult = dht * ig * xc
        dig = dht * mult * xc
        dla = da * a - dmult * a * a / mult
        dpr = dla * (-LRU_C * sp) * r * (1.0 - r)
        dpi = dig * ig * (1.0 - ig)
        dprb = dpr.astype(BF16)
        dpib = dpi.astype(BF16)
        dxc_ref[...] = dht * mult * ig + _dot(dprb, wat_ref[...]) + _dot(dpib, wxt_ref[...])
        dpr_ref[...] = dprb
        dpi_ref[...] = dpib
        dba_ref[...] += jnp.sum(dpr, axis=0, keepdims=True)
        dbx_ref[...] += jnp.sum(dpi, axis=0, keepdims=True)
        dlam_ref[...] += jnp.sum(dla * (-LRU_C * r), axis=0, keepdims=True) * (-_sigmoid(-lam_ref[...]))

    vec = _fix((1, BW))
    mat = _fix((BW, BW))
    rev = lambda col: pl.BlockSpec((tm, BW), lambda i: (nb - 1 - i, col))
    vshape = jax.ShapeDtypeStruct((1, BW), F32)
    return _pcall(body, name=name, grid=(nb,),
                  in_specs=[rev(0), rev(AY // BW), rev(0),
                            pl.BlockSpec((8, BW), lambda i: (jnp.maximum((nb - 1 - i) * hb - 1, 0), 0)),
                            rev(0), mat, mat, mat, mat, vec, vec, vec],
                  out_specs=[rev(0), rev(0), rev(0), rev(0), vec, vec, vec],
                  out_shape=[jax.ShapeDtypeStruct((t, BW), BF16), jax.ShapeDtypeStruct((t, BW), F32),
                             jax.ShapeDtypeStruct((t, BW), BF16), jax.ShapeDtypeStruct((t, BW), BF16),
                             vshape, vshape, vshape],
                  scratch_shapes=[pltpu.VMEM((tm + 8, BW), F32), pltpu.VMEM((tm, BW), F32),
                                  pltpu.VMEM((tm, BW), F32), pltpu.VMEM((tm, BW), F32),
                                  pltpu.VMEM((1, BW), F32)],
                  compiler_params=_cp("arbitrary"))(dya, z, h, h, xc, wa, wx, wat, wxt, ba, bx, lam)


def conv_bwd(dxc, z, cw, *, name):
    t = dxc.shape[0]
    tm = min(256, t)
    nb = t // tm
    hb = tm // 8

    def body(d_ref, dnext_ref, ax_ref, prev_ref, cw_ref, dax_ref, dcw_ref, dcb_ref, ds, xs):
        i = pl.program_id(0)

        @pl.when(i == 0)
        def _():
            dcw_ref[...] = jnp.zeros_like(dcw_ref)
            dcb_ref[...] = jnp.zeros_like(dcb_ref)

        d = d_ref[...]
        ds[0:tm, :] = d
        ds[tm:, :] = jnp.where(i == nb - 1, 0.0, dnext_ref[...])
        xs[0:8, :] = jnp.where(i == 0, 0.0, prev_ref[...])
        xs[8:, :] = ax_ref[...]
        dax = cw_ref[3:4, :] * d
        for k in range(3):
            dax = dax + cw_ref[k:k + 1, :] * ds[3 - k:3 - k + tm, :]
        dax_ref[...] = dax.astype(BF16)
        for k in range(4):
            dcw_ref[k:k + 1, :] += jnp.sum(d * xs[5 + k:5 + k + tm, :], axis=0, keepdims=True)
        dcb_ref[...] += jnp.sum(d, axis=0, keepdims=True)

    return _pcall(body, name=name, grid=(nb,),
                  in_specs=[_rows(tm, BW),
                            pl.BlockSpec((8, BW), lambda i: (jnp.minimum((i + 1) * hb, nb * hb - 1), 0)),
                            _rows(tm, BW, AX // BW),
                            pl.BlockSpec((8, BW), lambda i: (jnp.maximum(i * hb - 1, 0), AX // BW)),
                            _fix((4, BW))],
                  out_specs=[_rows(tm, BW), _fix((4, BW)), _fix((1, BW))],
                  out_shape=[jax.ShapeDtypeStruct((t, BW), BF16), jax.ShapeDtypeStruct((4, BW), F32),
                             jax.ShapeDtypeStruct((1, BW), F32)],
                  scratch_shapes=[pltpu.VMEM((tm + 8, BW), F32), pltpu.VMEM((tm + 8, BW), F32)],
                  compiler_params=_cp("arbitrary"))(dxc, dxc, z, z, cw)


GLA_CB = 4


def _gla_consts():
    tri = (jnp.arange(CHUNK)[:, None] >= jnp.arange(CHUNK)[None, :]).astype(F32)
    mask = ((jnp.arange(BW)[:, None] // 128) == (jnp.arange(256)[None, :] // 64)).astype(F32)
    return tri, mask


def gla_fwd(z, zb, wg2p, bg, ng, *, name):
    t = z.shape[0]
    tm = GLA_CB * CHUNK
    nc = t // CHUNK
    tri, mask = _gla_consts()

    def body(q_ref, k_ref, v_ref, misc_ref, br_ref, w_ref, bg_ref, ng_ref, tri_ref, mask_ref,
             yb_ref, st_ref, st):
        @pl.when(pl.program_id(0) == 0)
        def _():
            st[...] = jnp.zeros_like(st)

        for c in range(GLA_CB):
            rows = slice(c * CHUNK, (c + 1) * CHUNK)
            pre = _dot(misc_ref[rows, :], w_ref[...]) + bg_ref[...]
            la = _log_sigmoid(pre) / GLA_TAU
            gc = _dot_hi(tri_ref[...], la)
            gt = gc[CHUNK - 1:CHUNK, :]
            kdec = k_ref[rows, :] * jnp.exp(gt - gc)
            delta = _dot_tn(v_ref[rows, :], kdec.astype(BF16))
            s_new = st[...] * jnp.exp(gt) + delta * mask_ref[...]
            st[...] = s_new
            st_ref[c] = s_new
            o = _dot_nt(q_ref[rows, :], s_new.astype(BF16)) * (64.0 ** -0.5)
            br = br_ref[rows, :]
            for hd in range(4):
                cols = slice(hd * 128, (hd + 1) * 128)
                oh = o[:, cols]
                rs = lax.rsqrt(jnp.mean(oh * oh, axis=-1, keepdims=True) + RMS_EPS)
                brh = br[:, cols]
                yb_ref[rows, cols] = (oh * rs * ng_ref[:, cols] * (brh * _sigmoid(brh))).astype(BF16)

    return _pcall(body, name=name, grid=(t // tm,),
                  in_specs=[_rows(tm, 256, BQ // 256), _rows(tm, 256, BK // 256), _rows(tm, BW, BV // BW),
                            _rows(tm, 128, MISC // 128), _rows(tm, BW, BR // BW), _fix((128, 256)),
                            _fix((1, 256)), _fix((1, BW)), _fix((CHUNK, CHUNK)), _fix((BW, 256))],
                  out_specs=[_rows(tm, BW), pl.BlockSpec((GLA_CB, BW, 256), lambda i: (i, 0, 0))],
                  out_shape=[jax.ShapeDtypeStruct((t, BW), BF16), jax.ShapeDtypeStruct((nc, BW, 256), F32)],
                  scratch_shapes=[pltpu.VMEM((BW, 256), F32)],
                  compiler_params=_cp("arbitrary"))(zb, z, zb, zb, z, wg2p, bg, ng, tri, mask)


def gla_bwd(dyb, z, zb, states, wg2p, bg, ng, *, name):
    t = z.shape[0]
    tm = GLA_CB * CHUNK
    nb = t // tm
    tri, mask = _gla_consts()
    triu = tri.T

    def body(dy_ref, q_ref, k_ref, v_ref, misc_ref, br_ref, st_ref, sp_ref, w_ref, bg_ref, ng_ref,
             tri_ref, triu_ref, mask_ref,
             dq_ref, dk_ref, dv_ref, dbr_ref, dmisc_ref, dpre_ref, dbg_ref, dng_ref, cc):
        i = pl.program_id(0)

        @pl.when(i == 0)
        def _():
            cc[...] = jnp.zeros_like(cc)
            dbg_ref[...] = jnp.zeros_like(dbg_ref)
            dng_ref[...] = jnp.zeros_like(dng_ref)

        last_row = lax.broadcasted_iota(jnp.int32, (CHUNK, 256), 0) == CHUNK - 1
        for c in range(GLA_CB - 1, -1, -1):
            rows = slice(c * CHUNK, (c + 1) * CHUNK)
            pre = _dot(misc_ref[rows, :], w_ref[...]) + bg_ref[...]
            la = _log_sigmoid(pre) / GLA_TAU
            gc = _dot_hi(tri_ref[...], la)
            gt = gc[CHUNK - 1:CHUNK, :]
            eg = jnp.exp(gt - gc)
            kdec = k_ref[rows, :] * eg
            e = jnp.exp(gt)
            s_n = st_ref[c]
            if c > 0:
                s_prev = st_ref[c - 1]
            else:
                s_prev = jnp.where(i == nb - 1, 0.0, sp_ref[0])
            sb = s_n.astype(BF16)
            qb = q_ref[rows, :]
            o = _dot_nt(qb, sb) * (64.0 ** -0.5)
            br = br_ref[rows, :]
            dy = dy_ref[rows, :]
            do_parts = []
            for hd in range(4):
                cols = slice(hd * 128, (hd + 1) * 128)
                oh = o[:, cols]
                rs = lax.rsqrt(jnp.mean(oh * oh, axis=-1, keepdims=True) + RMS_EPS)
                ohat = oh * rs
                brh = br[:, cols]
                sg = _sigmoid(brh)
                dyh = dy[:, cols]
                ngh = ng_ref[:, cols]
                don = dyh * (brh * sg)
                dbr_ref[rows, cols] = (dyh * (ohat * ngh) * sg * (1.0 + brh * (1.0 - sg))).astype(BF16)
                dng_ref[:, cols] += jnp.sum(don * ohat, axis=0, keepdims=True)
                doh = don * ngh
                do_parts.append(rs * (doh - ohat * jnp.mean(doh * ohat, axis=-1, keepdims=True)))
            dob = jnp.concatenate(do_parts, axis=1).astype(BF16)
            dq_ref[rows, :] = (_dot(dob, sb) * (64.0 ** -0.5)).astype(BF16)
            dst = cc[...] + _dot_tn(dob, qb) * (64.0 ** -0.5) * mask_ref[...]
            dsb = dst.astype(BF16)
            dkdec = _dot(v_ref[rows, :], dsb)
            dv_ref[rows, :] = _dot_nt(kdec.astype(BF16), dsb).astype(BF16)
            dgt = jnp.sum(dst * s_prev, axis=0, keepdims=True) * e
            dk_ref[rows, :] = (dkdec * eg).astype(BF16)
            dd = dkdec * kdec
            dgt = dgt + jnp.sum(dd, axis=0, keepdims=True)
            dgc = jnp.where(last_row, dgt - dd, -dd)
            dla = _dot_hi(triu_ref[...], dgc)
            dpre = dla * (1.0 / GLA_TAU) * _sigmoid(-pre)
            dpb = dpre.astype(BF16)
            dpre_ref[rows, :] = dpb
            dmisc_ref[rows, :] = _dot_nt(dpb, w_ref[...])
            dbg_ref[...] += jnp.sum(dpre, axis=0, keepdims=True)
            cc[...] = dst * e

    rev = lambda w, col: pl.BlockSpec((tm, w), lambda i: (nb - 1 - i, col))
    return _pcall(body, name=name, grid=(nb,),
                  in_specs=[rev(BW, 0), rev(256, BQ // 256), rev(256, BK // 256), rev(BW, BV // BW),
                            rev(128, MISC // 128), rev(BW, BR // BW),
                            pl.BlockSpec((GLA_CB, BW, 256), lambda i: (nb - 1 - i, 0, 0)),
                            pl.BlockSpec((1, BW, 256), lambda i: (jnp.maximum((nb - 1 - i) * GLA_CB - 1, 0), 0, 0)),
                            _fix((128, 256)), _fix((1, 256)), _fix((1, BW)),
                            _fix((CHUNK, CHUNK)), _fix((CHUNK, CHUNK)), _fix((BW, 256))],
                  out_specs=[rev(256, 0), rev(256, 0), rev(BW, 0), rev(BW, 0), rev(128, 0), rev(256, 0),
                             _fix((1, 256)), _fix((1, BW))],
                  out_shape=[jax.ShapeDtypeStruct((t, 256), BF16), jax.ShapeDtypeStruct((t, 256), BF16),
                             jax.ShapeDtypeStruct((t, BW), BF16), jax.ShapeDtypeStruct((t, BW), BF16),
                             jax.ShapeDtypeStruct((t, 128), F32), jax.ShapeDtypeStruct((t, 256), BF16),
                             jax.ShapeDtypeStruct((1, 256), F32), jax.ShapeDtypeStruct((1, BW), F32)],
                  scratch_shapes=[pltpu.VMEM((BW, 256), F32)],
                  compiler_params=_cp("arbitrary"))(dyb, zb, z, zb, zb, z, states, states, wg2p, bg, ng,
                                                    tri, triu, mask)


FOX_SCALE = 64.0 ** -0.5
NEG = -1e30


def fox_fcum(z, bfp, *, name):
    t = z.shape[0]
    tm = min(256, t)
    tri = (jnp.arange(tm)[:, None] >= jnp.arange(tm)[None, :]).astype(F32)

    def body(m_ref, b_ref, tri_ref, o_ref, cc):
        @pl.when(pl.program_id(0) == 0)
        def _():
            cc[...] = jnp.zeros_like(cc)

        lf = _log_sigmoid(m_ref[...] + b_ref[...])
        cs = _dot_hi(tri_ref[...], lf) + cc[...]
        o_ref[...] = cs
        cc[...] = cs[tm - 1:tm, :]

    return _pcall(body, name=name, grid=(t // tm,),
                  in_specs=[_rows(tm, 128, MISC // 128), _fix((1, 128)), _fix((tm, tm))],
                  out_specs=_rows(tm, 128), out_shape=jax.ShapeDtypeStruct((t, 128), F32),
                  scratch_shapes=[pltpu.VMEM((1, 128), F32)],
                  compiler_params=_cp("arbitrary"))(z, bfp, tri)


def fox_dcf(dfc, z, bfp, dmisc_g, *, name):
    t = z.shape[0]
    tm = min(256, t)
    nb = t // tm
    triu = (jnp.arange(tm)[:, None] <= jnp.arange(tm)[None, :]).astype(F32)

    def body(d_ref, m_ref, b_ref, g_ref, tri_ref, o_ref, dbf_ref, cc):
        @pl.when(pl.program_id(0) == 0)
        def _():
            cc[...] = jnp.zeros_like(cc)
            dbf_ref[...] = jnp.zeros_like(dbf_ref)

        rc = _dot_hi(tri_ref[...], d_ref[...]) + cc[...]
        cc[...] = rc[0:1, :]
        dcf = rc * _sigmoid(-(m_ref[...] + b_ref[...]))
        o_ref[...] = (dcf + g_ref[...]).astype(BF16)
        dbf_ref[...] += jnp.sum(dcf, axis=0, keepdims=True)

    rev = lambda col: pl.BlockSpec((tm, 128), lambda i: (nb - 1 - i, col))
    return _pcall(body, name=name, grid=(nb,),
                  in_specs=[rev(0), rev(MISC // 128), _fix((1, 128)), rev(0), _fix((tm, tm))],
                  out_specs=[rev(0), _fix((1, 128))],
                  out_shape=[jax.ShapeDtypeStruct((t, 128), BF16), jax.ShapeDtypeStruct((1, 128), F32)],
                  scratch_shapes=[pltpu.VMEM((1, 128), F32)],
                  compiler_params=_cp("arbitrary"))(dfc, z, bfp, dmisc_g, triu)


def fox_fwd(zb, fq, fk, *, name):
    t = zb.shape[0]
    tq = min(512, t)
    nq = t // tq

    def body(q_ref, k_ref, v_ref, fq_ref, fk_ref, y_ref, yf_ref, lse_ref, m_s, l_s, acc):
        i, j = pl.program_id(1), pl.program_id(2)

        @pl.when(j == 0)
        def _():
            m_s[...] = jnp.full_like(m_s, NEG)
            l_s[...] = jnp.zeros_like(l_s)
            acc[...] = jnp.zeros_like(acc)

        lo = lax.broadcasted_iota(jnp.int32, (tq, 128), 1) < 64

        @pl.when(j <= i)
        def _():
            q = q_ref[...]
            k = k_ref[...]
            v = v_ref[...]
            row = lax.broadcasted_iota(jnp.int32, (tq, tq), 0)
            col = lax.broadcasted_iota(jnp.int32, (tq, tq), 1)
            keep = jnp.logical_or(col <= row, j < i)
            upd = []
            for hh in range(2):
                sel = lo if hh == 0 else jnp.logical_not(lo)
                qh = jnp.where(sel, q, jnp.zeros_like(q))
                s = _dot_nt(qh, k) * FOX_SCALE + fq_ref[hh] - fk_ref[hh]
                s = jnp.where(keep, s, NEG)
                m_old = m_s[hh]
                m_new = jnp.maximum(m_old, jnp.max(s, axis=-1, keepdims=True))
                p = jnp.exp(s - m_new)
                corr = jnp.exp(m_old - m_new)
                l_s[hh] = l_s[hh] * corr + jnp.sum(p, axis=-1, keepdims=True)
                m_s[hh] = m_new
                upd.append(acc[...] * corr + _dot(p.astype(BF16), v))
            acc[...] = jnp.where(lo, upd[0], upd[1])

        @pl.when(j == nq - 1)
        def _():
            out = acc[...] * jnp.where(lo, 1.0 / l_s[0], 1.0 / l_s[1])
            y_ref[...] = out.astype(BF16)
            yf_ref[...] = out
            lse_ref[...] = m_s[...] + jnp.log(l_s[...])

    kv = lambda off: pl.BlockSpec((tq, 128), lambda h, i, j: (jnp.minimum(j, i), off // 128 + h))
    return _pcall(body, name=name, grid=(4, nq, nq),
                  in_specs=[pl.BlockSpec((tq, 128), lambda h, i, j: (i, CQ // 128 + h)), kv(CK), kv(CV),
                            pl.BlockSpec((2, tq, 1), lambda h, i, j: (h, i, 0)),
                            pl.BlockSpec((2, 1, tq), lambda h, i, j: (h, 0, jnp.minimum(j, i)))],
                  out_specs=[pl.BlockSpec((tq, 128), lambda h, i, j: (i, h)),
                             pl.BlockSpec((tq, 128), lambda h, i, j: (i, h)),
                             pl.BlockSpec((2, tq, 1), lambda h, i, j: (h, i, 0))],
                  out_shape=[jax.ShapeDtypeStruct((t, BW), BF16), jax.ShapeDtypeStruct((t, BW), F32),
                             jax.ShapeDtypeStruct((FOX_H, t, 1), F32)],
                  scratch_shapes=[pltpu.VMEM((2, tq, 1), F32), pltpu.VMEM((2, tq, 1), F32),
                                  pltpu.VMEM((tq, 128), F32)],
                  compiler_params=_cp("parallel", "parallel", "arbitrary"))(zb, zb, zb, fq, fk)


def fox_delta(dyc, ycf, *, name):
    t = dyc.shape[0]
    tm = min(256, t)
    seg = ((jnp.arange(BW)[:, None] // 64) == jnp.arange(128)[None, :]).astype(F32)

    def body(d_ref, o_ref, s_ref, out_ref):
        out_ref[...] = _dot_hi(d_ref[...] * o_ref[...], s_ref[...])

    return _pcall(body, name=name, grid=(t // tm,),
                  in_specs=[_rows(tm, BW), _rows(tm, BW), _fix((BW, 128))],
                  out_specs=_rows(tm, 128), out_shape=jax.ShapeDtypeStruct((t, 128), F32),
                  compiler_params=_cp("parallel"))(dyc, ycf, seg)


def fox_bwd(zb, dyc, fq, fk, lse, dl, *, name):
    t = zb.shape[0]
    tq = min(512, t)
    nq = t // tq

    def body(q_ref, k_ref, v_ref, do_ref, fq_ref, fk_ref, lse_ref, dl_ref,
             dq_ref, dk_ref, dv_ref, dfk_ref, dfq_ref, dk_s, dv_s, df_s):
        j, i = pl.program_id(1), pl.program_id(2)

        @pl.when(jnp.logical_and(j == 0, i == 0))
        def _():
            dq_ref[...] = jnp.zeros_like(dq_ref)
            dfq_ref[...] = jnp.zeros_like(dfq_ref)

        @pl.when(i == 0)
        def _():
            dk_s[...] = jnp.zeros_like(dk_s)
            dv_s[...] = jnp.zeros_like(dv_s)
            df_s[...] = jnp.zeros_like(df_s)

        lo = lax.broadcasted_iota(jnp.int32, (tq, 128), 1) < 64

        @pl.when(i >= j)
        def _():
            q = q_ref[...]
            k = k_ref[...]
            v = v_ref[...]
            dob = do_ref[...].astype(BF16)
            row = lax.broadcasted_iota(jnp.int32, (tq, tq), 0)
            col = lax.broadcasted_iota(jnp.int32, (tq, tq), 1)
            keep = jnp.logical_or(col <= row, j < i)
            dvs, dks, dqs, rsum = [], [], [], []
            for hh in range(2):
                sel = lo if hh == 0 else jnp.logical_not(lo)
                qh = jnp.where(sel, q, jnp.zeros_like(q))
                doh = jnp.where(sel, dob, jnp.zeros_like(dob))
                s = _dot_nt(qh, k) * FOX_SCALE + fq_ref[hh] - fk_ref[hh]
                p = jnp.where(keep, jnp.exp(s - lse_ref[hh]), 0.0)
                dp = _dot_nt(doh, v)
                ds = p * (dp - dl_ref[hh])
                dsb = (ds * FOX_SCALE).astype(BF16)
                dvs.append(_dot_tn(p.astype(BF16), dob))
                dks.append(_dot_tn(dsb, q))
                dqs.append(_dot(dsb, k))
                df_s[hh] += jnp.sum(ds, axis=0, keepdims=True)
                rsum.append(jnp.sum(ds, axis=-1, keepdims=True))
            dv_s[...] += jnp.where(lo, dvs[0], dvs[1])
            dk_s[...] += jnp.where(lo, dks[0], dks[1])
            r0 = pl.multiple_of(i * tq, tq)
            dq_ref[pl.ds(r0, tq), :] += jnp.where(lo, dqs[0], dqs[1])
            dfq_ref[pl.ds(r0, tq), :] += jnp.where(lo, rsum[0], rsum[1])

        @pl.when(i == nq - 1)
        def _():
            dk_ref[...] = dk_s[...].astype(BF16)
            dv_ref[...] = dv_s[...].astype(BF16)
            dfk_ref[...] = -df_s[...]

    qi = lambda j, i: jnp.maximum(i, j)
    return _pcall(body, name=name, grid=(4, nq, nq),
                  in_specs=[pl.BlockSpec((tq, 128), lambda h, j, i: (qi(j, i), CQ // 128 + h)),
                            pl.BlockSpec((tq, 128), lambda h, j, i: (j, CK // 128 + h)),
                            pl.BlockSpec((tq, 128), lambda h, j, i: (j, CV // 128 + h)),
                            pl.BlockSpec((tq, 128), lambda h, j, i: (qi(j, i), h)),
                            pl.BlockSpec((2, tq, 1), lambda h, j, i: (h, qi(j, i), 0)),
                            pl.BlockSpec((2, 1, tq), lambda h, j, i: (h, 0, j)),
                            pl.BlockSpec((2, tq, 1), lambda h, j, i: (h, qi(j, i), 0)),
                            pl.BlockSpec((2, tq, 1), lambda h, j, i: (h, qi(j, i), 0))],
                  out_specs=[pl.BlockSpec((t, 128), lambda h, j, i: (0, h)),
                             pl.BlockSpec((tq, 128), lambda h, j, i: (j, h)),
                             pl.BlockSpec((tq, 128), lambda h, j, i: (j, h)),
                             pl.BlockSpec((2, 1, tq), lambda h, j, i: (h, 0, j)),
                             pl.BlockSpec((t, 128), lambda h, j, i: (0, h))],
                  out_shape=[jax.ShapeDtypeStruct((t, BW), F32), jax.ShapeDtypeStruct((t, BW), BF16),
                             jax.ShapeDtypeStruct((t, BW), BF16), jax.ShapeDtypeStruct((FOX_H, 1, t), F32),
                             jax.ShapeDtypeStruct((t, BW), F32)],
                  scratch_shapes=[pltpu.VMEM((tq, 128), F32), pltpu.VMEM((tq, 128), F32),
                                  pltpu.VMEM((2, 1, tq), F32)],
                  compiler_params=_cp("parallel", "arbitrary", "arbitrary"))(zb, zb, zb, dyc, fq, fk, lse, dl)


def merge_fwd(ya, yb, yc, wbr, z, *, name):
    t = ya.shape[0]
    tm = min(256, t)

    def body(ya_ref, yb_ref, yc_ref, w_ref, g0_ref, g1_ref, g2_ref, o_ref):
        m = _sigmoid(g0_ref[...]) * _dot(ya_ref[...], w_ref[0])
        m = m + _sigmoid(g1_ref[...]) * _dot(yb_ref[...], w_ref[1])
        m = m + _sigmoid(g2_ref[...]) * _dot(yc_ref[...], w_ref[2])
        o_ref[...] = m.astype(BF16)

    return _pcall(body, name=name, grid=(t // tm,),
                  in_specs=[_rows(tm, BW)] * 3 + [_fix((3, BW, D))]
                  + [_rows(tm, D, G0 // D + j) for j in range(3)],
                  out_specs=_rows(tm, D), out_shape=jax.ShapeDtypeStruct((t, D), BF16),
                  compiler_params=_cp("parallel"))(ya, yb, yc, wbr, z, z, z)


def merge_bwd(doutb, wot, ya, yb, yc, wbr, wbrt, z, *, name):
    t = ya.shape[0]
    tm = min(256, t)

    def body(do_ref, wot_ref, ya_ref, yb_ref, yc_ref, w_ref, wt_ref, g0_ref, g1_ref, g2_ref,
             dya_ref, dyb_ref, dyc_ref, dp0_ref, dp1_ref, dp2_ref, dg0_ref, dg1_ref, dg2_ref):
        dm = _dot(do_ref[...], wot_ref[...])
        ys = (ya_ref, yb_ref, yc_ref)
        gs = (g0_ref, g1_ref, g2_ref)
        dys = (dya_ref, dyb_ref, dyc_ref)
        dps = (dp0_ref, dp1_ref, dp2_ref)
        dgs = (dg0_ref, dg1_ref, dg2_ref)
        for j in range(3):
            s = _sigmoid(gs[j][...])
            pj = _dot(ys[j][...], w_ref[j])
            dpb = (dm * s).astype(BF16)
            dps[j][...] = dpb
            dgs[j][...] = (dm * pj * s * (1.0 - s)).astype(BF16)
            dys[j][...] = _dot(dpb, wt_ref[j])

    yshape = jax.ShapeDtypeStruct((t, BW), F32)
    dshape = jax.ShapeDtypeStruct((t, D), BF16)
    return _pcall(body, name=name, grid=(t // tm,),
                  in_specs=[_rows(tm, D), _fix((D, D))] + [_rows(tm, BW)] * 3
                  + [_fix((3, BW, D)), _fix((3, D, BW))] + [_rows(tm, D, G0 // D + j) for j in range(3)],
                  out_specs=[_rows(tm, BW)] * 3 + [_rows(tm, D)] * 6,
                  out_shape=[yshape] * 3 + [dshape] * 6,
                  compiler_params=_cp("parallel"))(doutb, wot, ya, yb, yc, wbr, wbrt, z, z, z)


def adamw(w, g, m, v, *, name):
    r, c = w.shape
    tm = _pick(r, (128, 64, 32, 16, 8))

    def body(w_ref, g_ref, m_ref, v_ref, d_ref, mo_ref, vo_ref):
        gg = g_ref[...]
        mn = ADAM_B1 * m_ref[...] + (1.0 - ADAM_B1) * gg
        vn = ADAM_B2 * v_ref[...] + (1.0 - ADAM_B2) * (gg * gg)
        m_hat = mn / (1.0 - ADAM_B1 ** ADAM_STEP)
        v_hat = vn / (1.0 - ADAM_B2 ** ADAM_STEP)
        d_ref[...] = -ADAM_LR * (m_hat / (jnp.sqrt(v_hat) + ADAM_EPS) + ADAM_WD * w_ref[...])
        mo_ref[...] = mn
        vo_ref[...] = vn

    shp = jax.ShapeDtypeStruct((r, c), F32)
    return _pcall(body, name=name, grid=(r // tm,), in_specs=[_rows(tm, c)] * 4, out_specs=[_rows(tm, c)] * 3,
                  out_shape=[shp] * 3, compiler_params=_cp("parallel"))(w, g, m, v)


def _place():
    return lax.axis_index("x"), lax.axis_index("y"), lax.axis_index("c")


def _remote(src, dst, send_sems, recv_sems, k, to):
    return pltpu.make_async_remote_copy(src_ref=src, dst_ref=dst, send_sem=send_sems.at[k],
                                        recv_sem=recv_sems.at[k], device_id=to, device_id_type=MESH)


def gather_weights(wsh):
    rows, width = wsh.shape
    rh = rows // 2

    def body(w_ref, g_ref, send_sems, recv_sems, local_sem):
        x, y, c = _place()
        sib = (x, y, 1 - c)
        chips = [(1 - x, y), (x, 1 - y), (1 - x, 1 - y)]

        def blk(px, py, pc):
            return g_ref.at[2 * px + py, pl.ds(pc * rh, rh), :]

        mine = pltpu.make_async_copy(w_ref, g_ref.at[2 * x + y], local_sem)
        mine.start()
        first = [_remote(w_ref.at[pl.ds(c * rh, rh), :], blk(x, y, c), send_sems, recv_sems, j, (*chip, c))
                 for j, chip in enumerate(chips)]
        for cp in first:
            cp.start()
        passed = [_remote(blk(*chip, c), blk(*chip, c), send_sems, recv_sems, 3 + j, sib)
                  for j, chip in enumerate(chips)]
        for j, chip in enumerate(chips):
            _remote(blk(*chip, c), blk(*chip, c), send_sems, recv_sems, j, (*chip, c)).wait_recv()
            passed[j].start()
        for j, chip in enumerate(chips):
            _remote(blk(*chip, 1 - c), blk(*chip, 1 - c), send_sems, recv_sems, 3 + j, sib).wait_recv()
        for cp in first + passed:
            cp.wait_send()
        mine.wait()

    return _pcall(body, name="gather_weights", in_specs=[ANY], out_specs=ANY,
                  out_shape=jax.ShapeDtypeStruct((4, rows, width), wsh.dtype),
                  scratch_shapes=[pltpu.SemaphoreType.DMA((6,)), pltpu.SemaphoreType.DMA((6,)),
                                  pltpu.SemaphoreType.DMA])(wsh)


def pair_exchange(gp):
    _, rows, width = gp.shape
    rh = rows // 2

    def body(g_ref, o_ref, send_sems, recv_sems):
        x, y, c = _place()
        cp = _remote(g_ref.at[:, pl.ds((1 - c) * rh, rh), :], o_ref, send_sems, recv_sems, 0, (x, y, 1 - c))
        cp.start()
        cp.wait()

    return _pcall(body, name="pair_exchange", in_specs=[ANY], out_specs=ANY,
                  out_shape=jax.ShapeDtypeStruct((4, rh, width), gp.dtype),
                  scratch_shapes=[pltpu.SemaphoreType.DMA((1,)), pltpu.SemaphoreType.DMA((1,))])(gp)


def chip_exchange(s1):
    _, rh, width = s1.shape

    def body(s_ref, o_ref, send_sems, recv_sems):
        x, y, c = _place()
        chips = [(1 - x, y), (x, 1 - y), (1 - x, 1 - y)]
        cps = [_remote(s_ref.at[2 * chip[0] + chip[1]], o_ref.at[j], send_sems, recv_sems, j, (*chip, c))
               for j, chip in enumerate(chips)]
        for cp in cps:
            cp.start()
        for cp in cps:
            cp.wait()

    return _pcall(body, name="chip_exchange", in_specs=[ANY], out_specs=ANY,
                  out_shape=jax.ShapeDtypeStruct((3, rh, width), s1.dtype),
                  scratch_shapes=[pltpu.SemaphoreType.DMA((3,)), pltpu.SemaphoreType.DMA((3,))])(s1)


def pair_share(s2):
    rh, width = s2.shape

    def body(s_ref, o_ref, send_sems, recv_sems, local_sem):
        x, y, c = _place()
        mine = pltpu.make_async_copy(s_ref, o_ref.at[pl.ds(c * rh, rh), :], local_sem)
        mine.start()
        cp = _remote(s_ref, o_ref.at[pl.ds(c * rh, rh), :], send_sems, recv_sems, 0, (x, y, 1 - c))
        cp.start()
        cp.wait_send()
        _remote(s_ref, o_ref.at[pl.ds((1 - c) * rh, rh), :], send_sems, recv_sems, 0, (x, y, 1 - c)).wait_recv()
        mine.wait()

    return _pcall(body, name="pair_share", in_specs=[ANY], out_specs=ANY,
                  out_shape=jax.ShapeDtypeStruct((2 * rh, width), s2.dtype),
                  scratch_shapes=[pltpu.SemaphoreType.DMA((1,)), pltpu.SemaphoreType.DMA((1,)),
                                  pltpu.SemaphoreType.DMA])(s2)


def small_exchange(gs):
    rows, width = gs.shape

    def body(g_ref, o_ref, send_sems, recv_sems):
        x, y, c = _place()
        cps = []
        for r in range(1, 8):
            dx, dy, dc = (r >> 2) & 1, (r >> 1) & 1, r & 1
            to = (x if dx == 0 else 1 - x, y if dy == 0 else 1 - y, c if dc == 0 else 1 - c)
            cps.append(_remote(g_ref, o_ref.at[r - 1], send_sems, recv_sems, r - 1, to))
        for cp in cps:
            cp.start()
        for cp in cps:
            cp.wait()

    return _pcall(body, name="small_exchange", in_specs=[ANY], out_specs=ANY,
                  out_shape=jax.ShapeDtypeStruct((7, rows, width), gs.dtype),
                  scratch_shapes=[pltpu.SemaphoreType.DMA((7,)), pltpu.SemaphoreType.DMA((7,))])(gs)


def pair_add(gp, rb, core):
    _, rows, width = gp.shape
    rh = rows // 2
    tr = _pick(rh, (512, 256, 128, 64, 32, 16, 8))
    g4 = gp.reshape(4, 2, rh, width)

    def body(c_ref, g_ref, r_ref, o_ref):
        o_ref[...] = g_ref[...] + r_ref[...]

    gs = pltpu.PrefetchScalarGridSpec(
        num_scalar_prefetch=1, grid=(4, rh // tr),
        in_specs=[pl.BlockSpec((None, None, tr, width), lambda k, i, c_ref: (k, c_ref[0], i, 0)),
                  pl.BlockSpec((None, tr, width), lambda k, i, c_ref: (k, i, 0))],
        out_specs=pl.BlockSpec((None, tr, width), lambda k, i, c_ref: (k, i, 0)))
    return _pcall(body, name="pair_add", grid_spec=gs, out_shape=jax.ShapeDtypeStruct((4, rh, width), F32),
                  compiler_params=_cp("parallel", "parallel"))(core, g4, rb)


def chip_add(s1, rb2, chip):
    _, rh, width = s1.shape
    tr = _pick(rh, (512, 256, 128, 64, 32, 16, 8))

    def body(k_ref, s_ref, r_ref, o_ref):
        o_ref[...] = ((s_ref[...] + r_ref[0]) + r_ref[1]) + r_ref[2]

    gs = pltpu.PrefetchScalarGridSpec(
        num_scalar_prefetch=1, grid=(rh // tr,),
        in_specs=[pl.BlockSpec((None, tr, width), lambda i, k_ref: (k_ref[0], i, 0)),
                  pl.BlockSpec((3, tr, width), lambda i, k_ref: (0, i, 0))],
        out_specs=pl.BlockSpec((tr, width), lambda i, k_ref: (i, 0)))
    return _pcall(body, name="chip_add", grid_spec=gs, out_shape=jax.ShapeDtypeStruct((rh, width), F32),
                  compiler_params=_cp("parallel"))(chip, s1, rb2)


def small_add(gs_own, slots, me):
    rows, width = gs_own.shape
    tr = _pick(rows, (64, 32, 16, 8))

    def body(me_ref, g_ref, s_ref, o_ref):
        me_v = me_ref[0]
        total = None
        for d in range(8):
            rel = jnp.bitwise_xor(me_v, d)
            val = jnp.where(rel == 0, g_ref[...], s_ref[jnp.maximum(rel - 1, 0)])
            total = val if total is None else total + val
        o_ref[...] = total

    gs = pltpu.PrefetchScalarGridSpec(
        num_scalar_prefetch=1, grid=(rows // tr,),
        in_specs=[pl.BlockSpec((tr, width), lambda i, m_ref: (i, 0)),
                  pl.BlockSpec((7, tr, width), lambda i, m_ref: (0, i, 0))],
        out_specs=pl.BlockSpec((tr, width), lambda i, m_ref: (i, 0)))
    return _pcall(body, name="small_add", grid_spec=gs, out_shape=jax.ShapeDtypeStruct((rows, width), F32),
                  compiler_params=_cp("parallel"))(me, gs_own, slots)


SHARDED = (("ffn1_w_up", 2), ("ffn1_w_down", 1), ("w_in", 2), ("conv_w", 2), ("gla_w_g2", 2),
           ("w_branch", 3), ("w_out", 1), ("ffn2_w_up", 2), ("ffn2_w_down", 1), ("ple_w_proj", 2),
           ("ple_w_gate", 1))
SMALL = ("ln1_g", "ln1_b", "conv_b", "lru_wa", "lru_ba", "lru_wx", "lru_bx", "lru_lambda", "gla_b_g",
         "gla_norm_g", "fox_b_f", "ln2_g", "ln2_b", "ln3_g", "ln3_b", "ple_b_gate", "ln4_g", "ln4_b")
WEIGHTS = ('ffn1_w_up', 'ffn1_w_down', 'ln1_g', 'ln1_b', 'w_in', 'conv_w', 'conv_b', 'lru_wa', 'lru_ba',
           'lru_wx', 'lru_bx', 'lru_lambda', 'gla_w_g2', 'gla_b_g', 'gla_norm_g', 'fox_b_f', 'w_branch',
           'w_out', 'ln2_g', 'ln2_b', 'ffn2_w_up', 'ffn2_w_down', 'ln3_g', 'ln3_b', 'ple_w_proj',
           'ple_w_gate', 'ple_b_gate', 'ln4_g', 'ln4_b')


def _pack(parts, rows):
    flat = jnp.concatenate([p.reshape(-1) for p in parts])
    flat = jnp.pad(flat, (0, rows * PACK_W - flat.shape[0]))
    return flat.reshape(rows, PACK_W)


def _join(parts, axis):
    full = jnp.moveaxis(parts, 0, axis)
    shp = list(full.shape)
    shp[axis:axis + 2] = [shp[axis] * shp[axis + 1]]
    return full.reshape(shp)


def _split(full, axis):
    shp = list(full.shape)
    shp[axis:axis + 1] = [4, shp[axis] // 4]
    return jnp.moveaxis(full.reshape(shp), axis, 0)


def _regroup_in(w):
    pad = jnp.zeros(w.shape[:-1] + (ZW - D_IN,), w.dtype)
    return jnp.concatenate([w[..., 0:2048], w[..., 2064:4112], w[..., 4120:7192], w[..., 2048:2064],
                            w[..., 4112:4120], pad], axis=-1)


def _regroup_out(g):
    return jnp.concatenate([g[..., 0:2048], g[..., 7168:7184], g[..., 2048:4096], g[..., 7184:7192],
                            g[..., 4096:7168]], axis=-1)


def _block_diag(w):
    eye = jnp.eye(8, dtype=w.dtype)
    return (eye[:, None, :, None] * w[:, :, None, :]).reshape(BW, BW)


def _diag_blocks(dense):
    return jnp.stack([dense[64 * n:64 * (n + 1), 64 * n:64 * (n + 1)] for n in range(8)])


def _layer_weights(full, small, l):
    w = {}
    up1, up2 = full["ffn1_w_up"][l], full["ffn2_w_up"][l]
    for tag, up, dn in (("1", up1, full["ffn1_w_down"][l]), ("2", up2, full["ffn2_w_down"][l])):
        w["wg" + tag], w["wu" + tag] = up[:, :DFF], up[:, DFF:]
        w["wupt" + tag] = up.T
        w["wd" + tag], w["wdt" + tag] = dn, dn.T
    win = _regroup_in(full["w_in"][l])
    w["win"], w["wint"] = win, win.T
    w["cw"] = full["conv_w"][l]
    wa = _block_diag(small["lru_wa"][l]).astype(BF16)
    wx = _block_diag(small["lru_wx"][l]).astype(BF16)
    w["wa"], w["wx"], w["wat"], w["wxt"] = wa, wx, wa.T, wx.T
    w["wg2p"] = jnp.pad(full["gla_w_g2"][l], ((0, 128 - LOW_W), (0, 0)))
    wbr = full["w_branch"][l]
    w["wbr"], w["wbrt"] = wbr, jnp.swapaxes(wbr, 1, 2)
    w["wo"], w["wot"] = full["w_out"][l], full["w_out"][l].T
    w["wp"] = full["ple_w_proj"][l]
    w["wgt"], w["wgtt"] = full["ple_w_gate"][l], full["ple_w_gate"][l].T
    for n in ("ln1_g", "ln1_b", "ln2_g", "ln2_b", "ln3_g", "ln3_b", "ln4_g", "ln4_b", "conv_b", "lru_ba",
              "lru_bx", "lru_lambda", "gla_b_g", "gla_norm_g", "ple_b_gate"):
        w[n] = small[n][l][None, :]
    w["bfp"] = jnp.pad(small["fox_b_f"][l], (LOW_W, 128 - LOW_W - FOX_H))[None, :]
    return w


def _heads_t(a):
    ht = a[:, LOW_W:LOW_W + FOX_H].T
    return ht[:, :, None], ht[:, None, :]


def _layer_fwd(x, xb, pb, w, l):
    s = {"x0": x, "x0b": xb}
    tag = "l%d_" % l
    gate, up, act = ffn_up(xb, w["wg1"], w["wu1"], name=tag + "ffn1_up")
    r1, x1, x1b = matmul_res_ln(act, w["wd1"], x, w["ln1_g"], w["ln1_b"], mm_scale=0.5, name=tag + "ffn1_down")
    s.update(gate1=gate, up1=up, act1=act, r1=r1, x1=x1, x1b=x1b)
    z, zb = matmul(x1b, w["win"], also_bf16=True, tn=_pick(ZW, (2432,)), name=tag + "mix_in")
    xc, xcb, h, ya = lru_fwd(z, w["cw"], w["conv_b"], w["wa"], w["wx"], w["lru_ba"], w["lru_bx"],
                             w["lru_lambda"], name=tag + "lru_fwd")
    yb, states = gla_fwd(z, zb, w["wg2p"], w["gla_b_g"], w["gla_norm_g"], name=tag + "gla_fwd")
    fcum = fox_fcum(z, w["bfp"], name=tag + "fox_fcum")
    fq, fk = _heads_t(fcum)
    yc, ycf, lse = fox_fwd(zb, fq, fk, name=tag + "fox_fwd")
    merged = merge_fwd(ya, yb, yc, w["wbr"], z, name=tag + "merge_fwd")
    r2, x2, x2b = matmul_res_ln(merged, w["wo"], x1, w["ln2_g"], w["ln2_b"], mm_scale=1.0, name=tag + "mix_out")
    s.update(z=z, zb=zb, xc=xc, xcb=xcb, h=h, ya=ya, yb=yb, states=states, fq=fq, fk=fk, yc=yc, ycf=ycf,
             lse=lse, merged=merged, r2=r2, x2=x2, x2b=x2b)
    gate, up, act = ffn_up(x2b, w["wg2"], w["wu2"], name=tag + "ffn2_up")
    r3, x3, x3b = matmul_res_ln(act, w["wd2"], x2, w["ln3_g"], w["ln3_b"], mm_scale=0.5, name=tag + "ffn2_down")
    s.update(gate2=gate, up2=up, act2=act, r3=r3, x3=x3, x3b=x3b)
    r4, x4, x4b = ple_fwd(x3b, x3, pb, w["wgt"], w["wp"], w["ple_b_gate"], w["ln4_g"], w["ln4_b"],
                          name=tag + "ple_fwd")
    s.update(r4=r4, pb=pb)
    return x4, x4b, s


def _ffn_bwd(dy, s, w, n, xin_b, tag):
    k = {"1": ("r1", "ln1_g", "gate1", "up1", "act1"), "2": ("r3", "ln3_g", "gate2", "up2", "act2")}[n]
    dr, dfb, dg, db = ln_bwd(dy, s[k[0]], w[k[1]], out_scale=0.5, name=tag + "ln_bwd")
    dgate, dup = ffn_down_bwd(dfb, w["wdt" + n], s[k[2]], s[k[3]], name=tag + "down_bwd")
    dh = jnp.concatenate([dgate, dup], axis=1)
    dx = matmul(dh, w["wupt" + n], res=dr, res_scale=ALPHA, tk=DFF, name=tag + "dx")
    dwup = matmul_tn(xin_b, dh, name=tag + "dw_up")
    dwdn = matmul_tn(s[k[4]], dfb, name=tag + "dw_down")
    return dx, dwup, dwdn, dg[0], db[0]


def _layer_bwd(dy, s, w, l):
    g = {}
    tag = "l%d_" % l
    dr4, dglb, dpeb, dg4, db4, dbg = ple_bwd(dy, s["r4"], s["x3b"], s["pb"], w["wgt"], w["wp"], w["ple_b_gate"],
                                             w["ln4_g"], name=tag + "ple_bwd")
    dx3 = matmul(dglb, w["wgtt"], res=dr4, res_scale=ALPHA, name=tag + "ple_dx")
    g["ple_w_gate"] = matmul_tn(s["x3b"], dglb, name=tag + "ple_dw_gate")
    g["ple_w_proj"] = matmul_tn(s["pb"], dpeb, name=tag + "ple_dw_proj")
    g["ln4_g"], g["ln4_b"], g["ple_b_gate"] = dg4[0], db4[0], dbg[0]
    dx2, g["ffn2_w_up"], g["ffn2_w_down"], g["ln3_g"], g["ln3_b"] = _ffn_bwd(dx3, s, w, "2", s["x2b"], tag + "ffn2_")
    dr2, doutb, dg2, db2 = ln_bwd(dx2, s["r2"], w["ln2_g"], out_scale=1.0, name=tag + "mix_ln_bwd")
    g["ln2_g"], g["ln2_b"] = dg2[0], db2[0]
    g["w_out"] = matmul_tn(s["merged"], doutb, name=tag + "dw_out")
    z, zb = s["z"], s["zb"]
    (dya, dyb, dyc, dp0, dp1, dp2, dgl0, dgl1, dgl2) = merge_bwd(
        doutb, w["wot"], s["ya"], s["yb"], s["yc"], w["wbr"], w["wbrt"], z, name=tag + "merge_bwd")
    g["w_branch"] = jnp.stack([matmul_tn(s["ya"], dp0, name=tag + "dw_br0"),
                               matmul_tn(s["yb"], dp1, name=tag + "dw_br1"),
                               matmul_tn(s["yc"], dp2, name=tag + "dw_br2")])
    day, dxc, dprb, dpib, dba, dbx, dlam = lru_bwd(dya, z, s["h"], s["xc"], w["wa"], w["wx"], w["wat"], w["wxt"],
                                                   w["lru_ba"], w["lru_bx"], w["lru_lambda"], name=tag + "lru_bwd")
    dax, dcw, dcb = conv_bwd(dxc, z, w["cw"], name=tag + "conv_bwd")
    g["lru_wa"] = _diag_blocks(matmul_tn(s["xcb"], dprb, name=tag + "dw_lru_a"))
    g["lru_wx"] = _diag_blocks(matmul_tn(s["xcb"], dpib, name=tag + "dw_lru_x"))
    g["lru_ba"], g["lru_bx"], g["lru_lambda"] = dba[0], dbx[0], dlam[0]
    g["conv_w"], g["conv_b"] = dcw, dcb[0]
    dbq, dbk, dbv, dbr, dmisc_g, dpreb, dbgg, dng = gla_bwd(dyb, z, zb, s["states"], w["wg2p"], w["gla_b_g"],
                                                            w["gla_norm_g"], name=tag + "gla_bwd")
    miscb = zb[:, MISC:]
    g["gla_w_g2"] = matmul_tn(miscb, dpreb, name=tag + "dw_g2")[:LOW_W]
    g["gla_b_g"], g["gla_norm_g"] = dbgg[0], dng[0]
    dl = fox_delta(dyc, s["ycf"], name=tag + "fox_delta")
    dlq = dl[:, :FOX_H].T[:, :, None]
    dcq, dck, dcv, dfk, dfq = fox_bwd(zb, dyc, s["fq"], s["fk"], s["lse"], dlq, name=tag + "fox_bwd")
    dfc = jnp.pad(dfk[:, 0, :].T + dfq[:, ::64], ((0, 0), (LOW_W, 128 - LOW_W - FOX_H)))
    dmiscb, dbf = fox_dcf(dfc, z, w["bfp"], dmisc_g, name=tag + "fox_dcf")
    g["fox_b_f"] = dbf[0, LOW_W:LOW_W + FOX_H]
    dz = jnp.concatenate([dax, day, dbq, dbk, dbv, dbr, dcq.astype(BF16), dck, dcv, dgl0, dgl1, dgl2, dmiscb],
                         axis=1)
    dx1 = matmul(dz, w["wint"], res=dr2, res_scale=ALPHA, tk=_pick(ZW, (2432,)), name=tag + "mix_dx")
    g["w_in"] = _regroup_out(matmul_tn(s["x1b"], dz, name=tag + "dw_in"))
    dx0, g["ffn1_w_up"], g["ffn1_w_down"], g["ln1_g"], g["ln1_b"] = _ffn_bwd(dx1, s, w, "1", s["x0b"], tag + "ffn1_")
    return dx0, g


def _local_step(x, p, target, full, small):
    xcur = x
    xb = xcur.astype(BF16)
    layer_w, saved = [], []
    for l in range(DEPTH):
        w = _layer_weights(full, small, l)
        xcur, xb, s = _layer_fwd(xcur, xb, p[l].astype(BF16), w, l)
        layer_w.append(w)
        saved.append(s)
    dy, sq = loss_head(xcur, target, name="loss_head")
    grads = [None] * DEPTH
    for l in reversed(range(DEPTH)):
        dy, grads[l] = _layer_bwd(dy, saved[l], layer_w[l], l)
    gfull = {n: jnp.stack([grads[l][n] for l in range(DEPTH)]) for n in WEIGHTS}
    return 0.5 * jnp.sum(sq) / float(D), dy, gfull


def kernel(x, p, ffn1_w_up, ffn1_w_down, ln1_g, ln1_b, w_in, conv_w, conv_b, lru_wa, lru_ba, lru_wx, lru_bx, lru_lambda, gla_w_g2, gla_b_g, gla_norm_g, fox_b_f, w_branch, w_out, ln2_g, ln2_b, ffn2_w_up, ffn2_w_down, ln3_g, ln3_b, ple_w_proj, ple_w_gate, ple_b_gate, ln4_g, ln4_b, loss_target, m_ffn1_w_up, m_ffn1_w_down, m_ln1_g, m_ln1_b, m_w_in, m_conv_w, m_conv_b, m_lru_wa, m_lru_ba, m_lru_wx, m_lru_bx, m_lru_lambda, m_gla_w_g2, m_gla_b_g, m_gla_norm_g, m_fox_b_f, m_w_branch, m_w_out, m_ln2_g, m_ln2_b, m_ffn2_w_up, m_ffn2_w_down, m_ln3_g, m_ln3_b, m_ple_w_proj, m_ple_w_gate, m_ple_b_gate, m_ln4_g, m_ln4_b, v_ffn1_w_up, v_ffn1_w_down, v_ln1_g, v_ln1_b, v_w_in, v_conv_w, v_conv_b, v_lru_wa, v_lru_ba, v_lru_wx, v_lru_bx, v_lru_lambda, v_gla_w_g2, v_gla_b_g, v_gla_norm_g, v_fox_b_f, v_w_branch, v_w_out, v_ln2_g, v_ln2_b, v_ffn2_w_up, v_ffn2_w_down, v_ln3_g, v_ln3_b, v_ple_w_proj, v_ple_w_gate, v_ple_b_gate, v_ln4_g, v_ln4_b):
    args = dict(locals())
    wts = {n: args[n] for n in WEIGHTS}
    mom = {n: args["m_" + n] for n in WEIGHTS}
    var = {n: args["v_" + n] for n in WEIGHTS}
    cx, cy, cc = lax.axis_index("x"), lax.axis_index("y"), lax.axis_index("c")

    parts = []
    for n, _ in SHARDED:
        if n == "conv_w":
            parts.append(lax.bitcast_convert_type(wts[n], BF16))
        else:
            parts.append(wts[n].astype(BF16))
    gathered = gather_weights(_pack(parts, PACK_ROWS))
    full = {}
    off = 0
    for (n, axis), part in zip(SHARDED, parts):
        size = part.size
        blk = gathered.reshape(4, -1)[:, off:off + size].reshape((4,) + part.shape)
        off += size
        if n == "conv_w":
            blk = lax.bitcast_convert_type(blk, F32)
        full[n] = _join(blk, axis)
    small = {n: wts[n] for n in SMALL}

    loss_local, dx, gfull = _local_step(x[0], p[:, 0], loss_target[0], full, small)
    loss = lax.psum(loss_local, ("x", "y", "c"))
    grad_x = dx[None]

    gparts = [_split(gfull[n], axis) for n, axis in SHARDED]
    gp = jnp.stack([_pack([gpart[k] for gpart in gparts], PACK_ROWS) for k in range(4)])
    s1 = pair_add(gp, pair_exchange(gp), jnp.reshape(cc, (1,)).astype(jnp.int32))
    s2 = chip_add(s1, chip_exchange(s1), jnp.reshape(2 * cx + cy, (1,)).astype(jnp.int32))
    gsh = pair_share(s2).reshape(-1)
    gout = {}
    off = 0
    for n, _ in SHARDED:
        size = wts[n].size
        gout[n] = gsh[off:off + size].reshape(wts[n].shape)
        off += size

    small_sizes = [wts[n].size for n in SMALL]
    srows = -(-sum(small_sizes) // (8 * PACK_W)) * 8
    gs = _pack([gfull[n] for n in SMALL], srows)
    me = jnp.reshape(4 * cx + 2 * cy + cc, (1,)).astype(jnp.int32)
    gsum = small_add(gs, small_exchange(gs), me)

    delta, new_m, new_v = {}, {}, {}
    for n, _ in SHARDED:
        shp = wts[n].shape
        v2 = lambda a: a.reshape(-1, shp[-1])
        d, mn, vn = adamw(v2(wts[n]), v2(gout[n]), v2(mom[n]), v2(var[n]), name="adamw_" + n)
        delta[n], new_m[n], new_v[n] = d.reshape(shp), mn.reshape(shp), vn.reshape(shp)
    d, mn, vn = adamw(_pack([wts[n] for n in SMALL], srows), gsum, _pack([mom[n] for n in SMALL], srows),
                      _pack([var[n] for n in SMALL], srows), name="adamw_small")
    off = 0
    for n, size in zip(SMALL, small_sizes):
        shp = wts[n].shape
        take = lambda a: a.reshape(-1)[off:off + size].reshape(shp)
        gout[n], delta[n], new_m[n], new_v[n] = take(gsum), take(d), take(mn), take(vn)
        off += size

    return (loss, grad_x, *[gout[n] for n in WEIGHTS], *[delta[n] for n in WEIGHTS],
            *[new_m[n] for n in WEIGHTS], *[new_v[n] for n in WEIGHTS])
```

```python
import functools
import math

import jax
import jax.numpy as jnp
from jax import lax
from jax.experimental import pallas as pl
from jax.experimental.pallas import tpu as pltpu

F32 = jnp.float32
BF16 = jnp.bfloat16

D = 1024
DFF = 2816
BW = 512
PLE = 256
DEPTH = 2
ALPHA = (2 * DEPTH) ** 0.25
LN_EPS = 1e-5
RMS_EPS = 1e-6
LRU_C = 8.0
GLA_TAU = 16.0
CHUNK = 64
D_IN = 7192
ZW = 7296
AX, AY, BQ, BK, BV, BR, CQ, CK, CV, G0, MISC = 0, 512, 1024, 1280, 1536, 2048, 2560, 3072, 3584, 4096, 7168
LOW_W, FOX_H = 16, 8
ADAM_LR, ADAM_B1, ADAM_B2, ADAM_EPS, ADAM_WD, ADAM_STEP = 0.001, 0.9, 0.999, 1e-08, 0.01, 10
PACK_W = 1024
VMEM_LIMIT = 56 << 20

MESH = pl.DeviceIdType.MESH
ANY = pl.BlockSpec(memory_space=pl.ANY)


def _pcall(body, **kw):
    return pl.pallas_call(body, **kw)


def _cp(*dims):
    return pltpu.CompilerParams(dimension_semantics=dims, vmem_limit_bytes=VMEM_LIMIT)


def _dot(a, b):
    return jnp.dot(a, b, preferred_element_type=F32)


def _dot_nt(a, b):
    return lax.dot_general(a, b, (((1,), (1,)), ((), ())), preferred_element_type=F32)


def _dot_tn(a, b):
    return lax.dot_general(a, b, (((0,), (0,)), ((), ())), preferred_element_type=F32)


def _dot_hi(a, b):
    return jnp.dot(a, b, preferred_element_type=F32, precision=lax.Precision.HIGHEST)


def _sigmoid(x):
    return 1.0 / (1.0 + jnp.exp(-x))


def _softplus(x):
    return jnp.maximum(x, 0.0) + jnp.log(1.0 + jnp.exp(-jnp.abs(x)))


def _log_sigmoid(x):
    return -_softplus(-x)


def _expm1(x):
    poly = x * (1.0 + x * (0.5 + x * (1.0 / 6.0 + x * (1.0 / 24.0 + x * (1.0 / 120.0 + x * (1.0 / 720.0))))))
    return jnp.where(jnp.abs(x) < 0.1, poly, jnp.exp(x) - 1.0)


_GELU_C = math.sqrt(2.0 / math.pi)


def _gelu(x):
    return 0.5 * x * (1.0 + jnp.tanh(_GELU_C * (x + 0.044715 * x * x * x)))


def _gelu_grad(x):
    t = jnp.tanh(_GELU_C * (x + 0.044715 * x * x * x))
    return 0.5 * (1.0 + t) + 0.5 * x * (1.0 - t * t) * _GELU_C * (1.0 + 3.0 * 0.044715 * x * x)


def _ln_stats(r):
    mu = jnp.mean(r, axis=-1, keepdims=True)
    xc = r - mu
    var = jnp.mean(xc * xc, axis=-1, keepdims=True)
    return xc, lax.rsqrt(var + LN_EPS)


def _pick(n, cands):
    for c in cands:
        if n % c == 0:
            return c
    return n


def _rows(tm, w, col=0):
    return pl.BlockSpec((tm, w), lambda i: (i, col))


def _fix(shape):
    nd = len(shape)
    return pl.BlockSpec(shape, lambda i: (0,) * nd)


def _layer(l, shape):
    nd = len(shape)
    return pl.BlockSpec((None,) + tuple(shape), lambda i: (l,) + (0,) * nd)


def matmul(a, b, *, name, nt=False, b_lead=(), res=None, res_scale=1.0, also_bf16=False, tm=512, tn=512,
           tk=None):
    m, k = a.shape
    n = b.shape[-2] if nt else b.shape[-1]
    tm, tn = min(tm, m), min(tn, n)
    tk = k if tk is None else tk
    nk = k // tk
    has_res = res is not None
    lead = tuple(b_lead)
    dot = _dot_nt if nt else _dot

    def body(*refs):
        a_ref, b_ref = refs[0], refs[1]
        pos = 2
        r_ref = None
        if has_res:
            r_ref = refs[pos]
            pos += 1
        o_ref = refs[pos]
        pos += 1
        ob_ref = None
        if also_bf16:
            ob_ref = refs[pos]
            pos += 1
        acc = refs[pos]
        kk = pl.program_id(2)

        @pl.when(kk == 0)
        def _():
            acc[...] = jnp.zeros_like(acc)

        acc[...] += dot(a_ref[...], b_ref[...])

        @pl.when(kk == nk - 1)
        def _():
            v = acc[...]
            if has_res:
                v = v + res_scale * r_ref[...]
            o_ref[...] = v
            if also_bf16:
                ob_ref[...] = v.astype(BF16)

    none = (None,) * len(lead)
    if nt:
        b_spec = pl.BlockSpec(none + (tn, tk), lambda j, i, kk: lead + (j, kk))
    else:
        b_spec = pl.BlockSpec(none + (tk, tn), lambda j, i, kk: lead + (kk, j))
    in_specs = [pl.BlockSpec((tm, tk), lambda j, i, kk: (i, kk)), b_spec]
    args = [a, b]
    if has_res:
        in_specs.append(pl.BlockSpec((tm, tn), lambda j, i, kk: (i, j)))
        args.append(res)
    out_shape = [jax.ShapeDtypeStruct((m, n), F32)]
    out_specs = [pl.BlockSpec((tm, tn), lambda j, i, kk: (i, j))]
    if also_bf16:
        out_shape.append(jax.ShapeDtypeStruct((m, n), BF16))
        out_specs.append(pl.BlockSpec((tm, tn), lambda j, i, kk: (i, j)))
    out = _pcall(body, name=name, grid=(n // tn, m // tm, nk), in_specs=in_specs, out_specs=out_specs,
                 out_shape=out_shape, scratch_shapes=[pltpu.VMEM((tm, tn), F32)],
                 compiler_params=_cp("parallel", "parallel", "arbitrary"))(*args)
    return out if also_bf16 else out[0]


def matmul_tn(a, b, *, name):
    t, k = a.shape
    n = b.shape[1]
    tt = min(512, t)
    tk = _pick(k, (512, 256, 128))
    tn = _pick(n, (1024, 1408, 2432, 512, 256, 128))
    nt = t // tt

    def body(a_ref, b_ref, o_ref):
        @pl.when(pl.program_id(2) == 0)
        def _():
            o_ref[...] = jnp.zeros_like(o_ref)

        o_ref[...] += _dot_tn(a_ref[...], b_ref[...])

    return _pcall(body, name=name, grid=(k // tk, n // tn, nt),
                  in_specs=[pl.BlockSpec((tt, tk), lambda i, j, s: (s, i)),
                            pl.BlockSpec((tt, tn), lambda i, j, s: (s, j))],
                  out_specs=pl.BlockSpec((tk, tn), lambda i, j, s: (i, j)),
                  out_shape=jax.ShapeDtypeStruct((k, n), F32),
                  compiler_params=_cp("parallel", "parallel", "arbitrary"))(a, b)


UPW = 1408


def matmul_tn_up(a, dgate, dup, *, name):
    t, k = a.shape
    tt = min(512, t)
    tk = 512

    def body(a_ref, g_ref, u_ref, o_ref):
        j = pl.program_id(1)

        @pl.when(pl.program_id(2) == 0)
        def _():
            o_ref[...] = jnp.zeros_like(o_ref)

        @pl.when(j < 2)
        def _():
            o_ref[...] += _dot_tn(a_ref[...], g_ref[...])

        @pl.when(j >= 2)
        def _():
            o_ref[...] += _dot_tn(a_ref[...], u_ref[...])

    return _pcall(body, name=name, grid=(k // tk, 4, t // tt),
                  in_specs=[pl.BlockSpec((tt, tk), lambda i, j, s: (s, i)),
                            pl.BlockSpec((tt, UPW), lambda i, j, s: (jnp.where(j < 2, s, 0), jnp.minimum(j, 1))),
                            pl.BlockSpec((tt, UPW), lambda i, j, s: (jnp.where(j >= 2, s, 0), jnp.maximum(j - 2, 0)))],
                  out_specs=pl.BlockSpec((None, tk, UPW), lambda i, j, s: (j, i, 0)),
                  out_shape=jax.ShapeDtypeStruct((4, k, UPW), F32),
                  compiler_params=_cp("parallel", "parallel", "arbitrary"))(a, dgate, dup)


def ffn_dx(dgate, dup, wup, l, res, *, name):
    t = dgate.shape[0]
    tm, tn = min(512, t), 512

    def body(g_ref, u_ref, w_ref, r_ref, o_ref, acc):
        kk = pl.program_id(2)

        @pl.when(kk == 0)
        def _():
            acc[...] = jnp.zeros_like(acc)

        @pl.when(kk < 2)
        def _():
            acc[...] += _dot_nt(g_ref[...], w_ref[...])

        @pl.when(kk >= 2)
        def _():
            acc[...] += _dot_nt(u_ref[...], w_ref[...])

        @pl.when(kk == 3)
        def _():
            o_ref[...] = acc[...] + ALPHA * r_ref[...]

    return _pcall(body, name=name, grid=(D // tn, t // tm, 4),
                  in_specs=[pl.BlockSpec((tm, UPW), lambda j, i, kk: (i, jnp.minimum(kk, 1))),
                            pl.BlockSpec((tm, UPW), lambda j, i, kk: (i, jnp.maximum(kk - 2, 0))),
                            pl.BlockSpec((None, None, tn, UPW), lambda j, i, kk: (l, kk, j, 0)),
                            pl.BlockSpec((tm, tn), lambda j, i, kk: (i, j))],
                  out_specs=pl.BlockSpec((tm, tn), lambda j, i, kk: (i, j)),
                  out_shape=jax.ShapeDtypeStruct((t, D), F32),
                  scratch_shapes=[pltpu.VMEM((tm, tn), F32)],
                  compiler_params=_cp("parallel", "parallel", "arbitrary"))(dgate, dup, wup, res)


def ffn_up(xb, wup, l, *, name):
    t = xb.shape[0]
    tm, tn = min(256, t), UPW

    def body(x_ref, wg_ref, wu_ref, g_ref, u_ref, a_ref):
        x = x_ref[...]
        g = _dot(x, wg_ref[...])
        u = _dot(x, wu_ref[...])
        g_ref[...] = g
        u_ref[...] = u
        a_ref[...] = (g * _sigmoid(g) * u).astype(BF16)

    blk = pl.BlockSpec((tm, tn), lambda j, i: (i, j))
    return _pcall(body, name=name, grid=(DFF // tn, t // tm),
                  in_specs=[pl.BlockSpec((tm, D), lambda j, i: (i, 0)),
                            pl.BlockSpec((None, None, D, tn), lambda j, i: (l, j, 0, 0)),
                            pl.BlockSpec((None, None, D, tn), lambda j, i: (l, 2 + j, 0, 0))],
                  out_specs=[blk, blk, blk],
                  out_shape=[jax.ShapeDtypeStruct((t, DFF), F32), jax.ShapeDtypeStruct((t, DFF), F32),
                             jax.ShapeDtypeStruct((t, DFF), BF16)],
                  compiler_params=_cp("parallel", "parallel"))(xb, wup, wup)


def matmul_res_ln(a, w, l, res, g, b, *, mm_scale, name):
    t, k = a.shape
    tm = min(256, t)

    def body(a_ref, w_ref, res_ref, g_ref, b_ref, r_ref, y_ref, yb_ref):
        f = _dot(a_ref[...], w_ref[...])
        r = ALPHA * res_ref[...] + mm_scale * f
        xc, rstd = _ln_stats(r)
        y = xc * rstd * g_ref[...] + b_ref[...]
        r_ref[...] = r
        y_ref[...] = y
        yb_ref[...] = y.astype(BF16)

    return _pcall(body, name=name, grid=(t // tm,),
                  in_specs=[_rows(tm, k), _layer(l, (k, D)), _rows(tm, D), _fix((1, D)), _fix((1, D))],
                  out_specs=[_rows(tm, D)] * 3,
                  out_shape=[jax.ShapeDtypeStruct((t, D), F32), jax.ShapeDtypeStruct((t, D), F32),
                             jax.ShapeDtypeStruct((t, D), BF16)],
                  compiler_params=_cp("parallel"))(a, w, res, g, b)


def ln_bwd(dy, r, g, *, out_scale, name):
    t = dy.shape[0]
    tm = min(256, t)

    def body(dy_ref, r_ref, g_ref, dr_ref, drb_ref, dg_ref, db_ref):
        @pl.when(pl.program_id(0) == 0)
        def _():
            dg_ref[...] = jnp.zeros_like(dg_ref)
            db_ref[...] = jnp.zeros_like(db_ref)

        xc, rstd = _ln_stats(r_ref[...])
        xhat = xc * rstd
        d = dy_ref[...]
        dxh = d * g_ref[...]
        dr = rstd * (dxh - jnp.mean(dxh, axis=-1, keepdims=True)
                     - xhat * jnp.mean(dxh * xhat, axis=-1, keepdims=True))
        dr_ref[...] = dr
        drb_ref[...] = (out_scale * dr).astype(BF16)
        dg_ref[...] += jnp.sum(d * xhat, axis=0, keepdims=True)
        db_ref[...] += jnp.sum(d, axis=0, keepdims=True)

    return _pcall(body, name=name, grid=(t // tm,),
                  in_specs=[_rows(tm, D), _rows(tm, D), _fix((1, D))],
                  out_specs=[_rows(tm, D), _rows(tm, D), _fix((1, D)), _fix((1, D))],
                  out_shape=[jax.ShapeDtypeStruct((t, D), F32), jax.ShapeDtypeStruct((t, D), BF16),
                             jax.ShapeDtypeStruct((1, D), F32), jax.ShapeDtypeStruct((1, D), F32)],
                  compiler_params=_cp("arbitrary"))(dy, r, g)


def ffn_down_bwd(dfb, wd, l, gate, up, *, name):
    t = dfb.shape[0]
    tm, tn = min(256, t), UPW
    nj = DFF // tn

    def body(df_ref, w_ref, g_ref, u_ref, dg_ref, du_ref):
        da = _dot_nt(df_ref[...], w_ref[...])
        g = g_ref[...]
        s = _sigmoid(g)
        dg_ref[...] = (da * u_ref[...] * s * (1.0 + g * (1.0 - s))).astype(BF16)
        du_ref[...] = (da * g * s).astype(BF16)

    blk = pl.BlockSpec((tm, tn), lambda j, i: (i, j))
    return _pcall(body, name=name, grid=(nj, t // tm),
                  in_specs=[pl.BlockSpec((tm, D), lambda j, i: (i, 0)),
                            pl.BlockSpec((None, tn, D), lambda j, i: (l, j, 0)), blk, blk],
                  out_specs=[blk, blk],
                  out_shape=[jax.ShapeDtypeStruct((t, DFF), BF16), jax.ShapeDtypeStruct((t, DFF), BF16)],
                  compiler_params=_cp("parallel", "parallel"))(dfb, wd, gate, up)


def ple_fwd(xb, x, pb, wgate, l, wproj, bgate, g, b, *, name):
    t = x.shape[0]
    tm = min(256, t)

    def body(xb_ref, x_ref, p_ref, wg_ref, wp_ref, bg_ref, g_ref, b_ref, r_ref, y_ref, yb_ref):
        gl = _dot(xb_ref[...], wg_ref[...]) + bg_ref[...]
        pe = _dot(p_ref[...], wp_ref[...])
        r = ALPHA * x_ref[...] + _sigmoid(gl) * pe
        xc, rstd = _ln_stats(r)
        y = xc * rstd * g_ref[...] + b_ref[...]
        r_ref[...] = r
        y_ref[...] = y
        yb_ref[...] = y.astype(BF16)

    return _pcall(body, name=name, grid=(t // tm,),
                  in_specs=[_rows(tm, D), _rows(tm, D), _rows(tm, PLE), _layer(l, (D, D)), _fix((PLE, D)),
                            _fix((1, D)), _fix((1, D)), _fix((1, D))],
                  out_specs=[_rows(tm, D)] * 3,
                  out_shape=[jax.ShapeDtypeStruct((t, D), F32), jax.ShapeDtypeStruct((t, D), F32),
                             jax.ShapeDtypeStruct((t, D), BF16)],
                  compiler_params=_cp("parallel"))(xb, x, pb, wgate, wproj, bgate, g, b)


def ple_bwd(dy, r, xb, pb, wgate, l, wproj, bgate, g, *, name):
    t = dy.shape[0]
    tm = min(256, t)

    def body(dy_ref, r_ref, xb_ref, p_ref, wg_ref, wp_ref, bg_ref, g_ref,
             dr_ref, dgl_ref, dpe_ref, dg_ref, db_ref, dbg_ref):
        @pl.when(pl.program_id(0) == 0)
        def _():
            dg_ref[...] = jnp.zeros_like(dg_ref)
            db_ref[...] = jnp.zeros_like(db_ref)
            dbg_ref[...] = jnp.zeros_like(dbg_ref)

        xc, rstd = _ln_stats(r_ref[...])
        xhat = xc * rstd
        d = dy_ref[...]
        dxh = d * g_ref[...]
        dr = rstd * (dxh - jnp.mean(dxh, axis=-1, keepdims=True)
                     - xhat * jnp.mean(dxh * xhat, axis=-1, keepdims=True))
        s = _sigmoid(_dot(xb_ref[...], wg_ref[...]) + bg_ref[...])
        pe = _dot(p_ref[...], wp_ref[...])
        dgl = dr * pe * s * (1.0 - s)
        dr_ref[...] = dr
        dgl_ref[...] = dgl.astype(BF16)
        dpe_ref[...] = (dr * s).astype(BF16)
        dg_ref[...] += jnp.sum(d * xhat, axis=0, keepdims=True)
        db_ref[...] += jnp.sum(d, axis=0, keepdims=True)
        dbg_ref[...] += jnp.sum(dgl, axis=0, keepdims=True)

    vec = jax.ShapeDtypeStruct((1, D), F32)
    return _pcall(body, name=name, grid=(t // tm,),
                  in_specs=[_rows(tm, D), _rows(tm, D), _rows(tm, D), _rows(tm, PLE), _layer(l, (D, D)),
                            _fix((PLE, D)), _fix((1, D)), _fix((1, D))],
                  out_specs=[_rows(tm, D), _rows(tm, D), _rows(tm, D), _fix((1, D)), _fix((1, D)), _fix((1, D))],
                  out_shape=[jax.ShapeDtypeStruct((t, D), F32), jax.ShapeDtypeStruct((t, D), BF16),
                             jax.ShapeDtypeStruct((t, D), BF16), vec, vec, vec],
                  compiler_params=_cp("arbitrary"))(dy, r, xb, pb, wgate, wproj, bgate, g)


def loss_head(y, tgt, *, name):
    t = y.shape[0]
    tm = min(256, t)

    def body(y_ref, t_ref, dy_ref, sq_ref):
        @pl.when(pl.program_id(0) == 0)
        def _():
            sq_ref[...] = jnp.zeros_like(sq_ref)

        e = y_ref[...] - t_ref[...]
        dy_ref[...] = e / float(D)
        sq_ref[...] += jnp.sum(e * e, axis=0, keepdims=True)

    return _pcall(body, name=name, grid=(t // tm,),
                  in_specs=[_rows(tm, D), _rows(tm, D)],
                  out_specs=[_rows(tm, D), _fix((1, D))],
                  out_shape=[jax.ShapeDtypeStruct((t, D), F32), jax.ShapeDtypeStruct((1, D), F32)],
                  compiler_params=_cp("arbitrary"))(y, tgt)


def _lru_gates(xc, wa_ref, wx_ref, ba_ref, bx_ref, lam_ref):
    xcb = xc.astype(BF16)
    r = _sigmoid(_dot(xcb, wa_ref[...]) + ba_ref[...])
    ig = _sigmoid(_dot(xcb, wx_ref[...]) + bx_ref[...])
    sp = _softplus(-lam_ref[...])
    la = -LRU_C * r * sp
    a = jnp.exp(la)
    mult = jnp.sqrt(-_expm1(2.0 * la))
    return r, ig, sp, la, a, mult


def lru_fwd(z, cw, cb, wa, wx, ba, bx, lam, *, name):
    t = z.shape[0]
    tm = min(256, t)
    hb = tm // 8

    def body(ax_ref, prev_ref, ay_ref, cw_ref, cb_ref, wa_ref, wx_ref, ba_ref, bx_ref, lam_ref,
             xc_ref, xcb_ref, h_ref, ya_ref, xs, a_s, b_s, hc):
        i = pl.program_id(0)

        @pl.when(i == 0)
        def _():
            hc[...] = jnp.zeros_like(hc)

        xs[0:8, :] = jnp.where(i == 0, 0.0, prev_ref[...])
        xs[8:, :] = ax_ref[...]
        xc = cb_ref[...] + cw_ref[0:1, :] * xs[5:5 + tm, :]
        for k in range(1, 4):
            xc = xc + cw_ref[k:k + 1, :] * xs[5 + k:5 + k + tm, :]
        r, ig, sp, la, a, mult = _lru_gates(xc, wa_ref, wx_ref, ba_ref, bx_ref, lam_ref)
        a_s[...] = a
        b_s[...] = mult * (ig * xc)
        xc_ref[...] = xc
        xcb_ref[...] = xc.astype(BF16)

        def step(g, h):
            base = pl.multiple_of(g * 8, 8)
            a8 = a_s[pl.ds(base, 8), :]
            b8 = b_s[pl.ds(base, 8), :]
            for j in range(8):
                h = a8[j:j + 1, :] * h + b8[j:j + 1, :]
                h_ref[pl.ds(base + j, 1), :] = h
            return h

        hc[...] = lax.fori_loop(0, tm // 8, step, hc[...])
        ya_ref[...] = (_gelu(ay_ref[...]) * h_ref[...]).astype(BF16)

    vec = _fix((1, BW))
    return _pcall(body, name=name, grid=(t // tm,),
                  in_specs=[_rows(tm, BW, AX // BW),
                            pl.BlockSpec((8, BW), lambda i: (jnp.maximum(i * hb - 1, 0), AX // BW)),
                            _rows(tm, BW, AY // BW), _fix((4, BW)), vec, _fix((BW, BW)), _fix((BW, BW)),
                            vec, vec, vec],
                  out_specs=[_rows(tm, BW)] * 4,
                  out_shape=[jax.ShapeDtypeStruct((t, BW), F32), jax.ShapeDtypeStruct((t, BW), BF16),
                             jax.ShapeDtypeStruct((t, BW), F32), jax.ShapeDtypeStruct((t, BW), BF16)],
                  scratch_shapes=[pltpu.VMEM((tm + 8, BW), F32), pltpu.VMEM((tm, BW), F32),
                                  pltpu.VMEM((tm, BW), F32), pltpu.VMEM((1, BW), F32)],
                  compiler_params=_cp("arbitrary"))(z, z, z, cw, cb, wa, wx, ba, bx, lam)


def lru_bwd(dya, z, h, xc, wa, wx, ba, bx, lam, *, name):
    t = dya.shape[0]
    tm = min(256, t)
    nb = t // tm
    hb = tm // 8

    def body(dya_ref, ay_ref, h_ref, hprev_ref, xc_ref, wa_ref, wx_ref, ba_ref, bx_ref,
             lam_ref, day_ref, dxc_ref, dpr_ref, dpi_ref, dba_ref, dbx_ref, dlam_ref,
             hs, a_s, g_s, d_s, cc):
        i = pl.program_id(0)

        @pl.when(i == 0)
        def _():
            cc[...] = jnp.zeros_like(cc)
            dba_ref[...] = jnp.zeros_like(dba_ref)
            dbx_ref[...] = jnp.zeros_like(dbx_ref)
            dlam_ref[...] = jnp.zeros_like(dlam_ref)

        xc = xc_ref[...]
        r, ig, sp, la, a, mult = _lru_gates(xc, wa_ref, wx_ref, ba_ref, bx_ref, lam_ref)
        ay = ay_ref[...]
        dya = dya_ref[...]
        hcur = h_ref[...]
        day_ref[...] = (dya * hcur * _gelu_grad(ay)).astype(BF16)
        a_s[...] = a
        g_s[...] = dya * _gelu(ay)

        def step(gg, cin):
            g = tm // 8 - 1 - gg
            base = pl.multiple_of(g * 8, 8)
            a8 = a_s[pl.ds(base, 8), :]
            g8 = g_s[pl.ds(base, 8), :]
            for j in range(7, -1, -1):
                d = g8[j:j + 1, :] + cin
                d_s[pl.ds(base + j, 1), :] = d
                cin = a8[j:j + 1, :] * d
            return cin

        cc[...] = lax.fori_loop(0, tm // 8, step, cc[...])
        dht = d_s[...]
        hs[0:8, :] = jnp.where(i == nb - 1, 0.0, hprev_ref[...])
        hs[8:, :] = hcur
        da = dht * hs[7:7 + tm, :]
        dmult = dht * ig * xc
        dig = dht * mult * xc
        dla = da * a - dmult * a * a / mult
        dpr = dla * (-LRU_C * sp) * r * (1.0 - r)
        dpi = dig * ig * (1.0 - ig)
        dprb = dpr.astype(BF16)
        dpib = dpi.astype(BF16)
        dxc_ref[...] = dht * mult * ig + _dot_nt(dprb, wa_ref[...]) + _dot_nt(dpib, wx_ref[...])
        dpr_ref[...] = dprb
        dpi_ref[...] = dpib
        dba_ref[...] += jnp.sum(dpr, axis=0, keepdims=True)
        dbx_ref[...] += jnp.sum(dpi, axis=0, keepdims=True)
        dlam_ref[...] += jnp.sum(dla * (-LRU_C * r), axis=0, keepdims=True) * (-_sigmoid(-lam_ref[...]))

    vec = _fix((1, BW))
    mat = _fix((BW, BW))
    rev = lambda col: pl.BlockSpec((tm, BW), lambda i: (nb - 1 - i, col))
    vshape = jax.ShapeDtypeStruct((1, BW), F32)
    return _pcall(body, name=name, grid=(nb,),
                  in_specs=[rev(0), rev(AY // BW), rev(0),
                            pl.BlockSpec((8, BW), lambda i: (jnp.maximum((nb - 1 - i) * hb - 1, 0), 0)),
                            rev(0), mat, mat, vec, vec, vec],
                  out_specs=[rev(0), rev(0), rev(0), rev(0), vec, vec, vec],
                  out_shape=[jax.ShapeDtypeStruct((t, BW), BF16), jax.ShapeDtypeStruct((t, BW), F32),
                             jax.ShapeDtypeStruct((t, BW), BF16), jax.ShapeDtypeStruct((t, BW), BF16),
                             vshape, vshape, vshape],
                  scratch_shapes=[pltpu.VMEM((tm + 8, BW), F32), pltpu.VMEM((tm, BW), F32),
                                  pltpu.VMEM((tm, BW), F32), pltpu.VMEM((tm, BW), F32),
                                  pltpu.VMEM((1, BW), F32)],
                  compiler_params=_cp("arbitrary"))(dya, z, h, h, xc, wa, wx, ba, bx, lam)


def conv_bwd(dxc, z, cw, *, name):
    t = dxc.shape[0]
    tm = min(256, t)
    nb = t // tm
    hb = tm // 8

    def body(d_ref, dnext_ref, ax_ref, prev_ref, cw_ref, dax_ref, dcw_ref, dcb_ref, ds, xs):
        i = pl.program_id(0)

        @pl.when(i == 0)
        def _():
            dcw_ref[...] = jnp.zeros_like(dcw_ref)
            dcb_ref[...] = jnp.zeros_like(dcb_ref)

        d = d_ref[...]
        ds[0:tm, :] = d
        ds[tm:, :] = jnp.where(i == nb - 1, 0.0, dnext_ref[...])
        xs[0:8, :] = jnp.where(i == 0, 0.0, prev_ref[...])
        xs[8:, :] = ax_ref[...]
        dax = cw_ref[3:4, :] * d
        for k in range(3):
            dax = dax + cw_ref[k:k + 1, :] * ds[3 - k:3 - k + tm, :]
        dax_ref[...] = dax.astype(BF16)
        for k in range(4):
            dcw_ref[k:k + 1, :] += jnp.sum(d * xs[5 + k:5 + k + tm, :], axis=0, keepdims=True)
        dcb_ref[...] += jnp.sum(d, axis=0, keepdims=True)

    return _pcall(body, name=name, grid=(nb,),
                  in_specs=[_rows(tm, BW),
                            pl.BlockSpec((8, BW), lambda i: (jnp.minimum((i + 1) * hb, nb * hb - 1), 0)),
                            _rows(tm, BW, AX // BW),
                            pl.BlockSpec((8, BW), lambda i: (jnp.maximum(i * hb - 1, 0), AX // BW)),
                            _fix((4, BW))],
                  out_specs=[_rows(tm, BW), _fix((4, BW)), _fix((1, BW))],
                  out_shape=[jax.ShapeDtypeStruct((t, BW), BF16), jax.ShapeDtypeStruct((4, BW), F32),
                             jax.ShapeDtypeStruct((1, BW), F32)],
                  scratch_shapes=[pltpu.VMEM((tm + 8, BW), F32), pltpu.VMEM((tm + 8, BW), F32)],
                  compiler_params=_cp("arbitrary"))(dxc, dxc, z, z, cw)


GLA_CB = 4


def _gla_consts():
    tri = (jnp.arange(CHUNK)[:, None] >= jnp.arange(CHUNK)[None, :]).astype(F32)
    mask = ((jnp.arange(BW)[:, None] // 128) == (jnp.arange(256)[None, :] // 64)).astype(F32)
    return tri, mask


def gla_fwd(z, zb, wg2p, bg, ng, *, name):
    t = z.shape[0]
    tm = GLA_CB * CHUNK
    nc = t // CHUNK
    tri, mask = _gla_consts()

    def body(q_ref, k_ref, v_ref, misc_ref, br_ref, w_ref, bg_ref, ng_ref, tri_ref, mask_ref,
             yb_ref, st_ref, st):
        @pl.when(pl.program_id(0) == 0)
        def _():
            st[...] = jnp.zeros_like(st)

        for c in range(GLA_CB):
            rows = slice(c * CHUNK, (c + 1) * CHUNK)
            pre = _dot(misc_ref[rows, :], w_ref[...]) + bg_ref[...]
            la = _log_sigmoid(pre) / GLA_TAU
            gc = _dot_hi(tri_ref[...], la)
            gt = gc[CHUNK - 1:CHUNK, :]
            kdec = k_ref[rows, :] * jnp.exp(gt - gc)
            delta = _dot_tn(v_ref[rows, :], kdec.astype(BF16))
            s_new = st[...] * jnp.exp(gt) + delta * mask_ref[...]
            st[...] = s_new
            st_ref[c] = s_new
            o = _dot_nt(q_ref[rows, :], s_new.astype(BF16)) * (64.0 ** -0.5)
            br = br_ref[rows, :]
            for hd in range(4):
                cols = slice(hd * 128, (hd + 1) * 128)
                oh = o[:, cols]
                rs = lax.rsqrt(jnp.mean(oh * oh, axis=-1, keepdims=True) + RMS_EPS)
                brh = br[:, cols]
                yb_ref[rows, cols] = (oh * rs * ng_ref[:, cols] * (brh * _sigmoid(brh))).astype(BF16)

    return _pcall(body, name=name, grid=(t // tm,),
                  in_specs=[_rows(tm, 256, BQ // 256), _rows(tm, 256, BK // 256), _rows(tm, BW, BV // BW),
                            _rows(tm, 128, MISC // 128), _rows(tm, BW, BR // BW), _fix((128, 256)),
                            _fix((1, 256)), _fix((1, BW)), _fix((CHUNK, CHUNK)), _fix((BW, 256))],
                  out_specs=[_rows(tm, BW), pl.BlockSpec((GLA_CB, BW, 256), lambda i: (i, 0, 0))],
                  out_shape=[jax.ShapeDtypeStruct((t, BW), BF16), jax.ShapeDtypeStruct((nc, BW, 256), F32)],
                  scratch_shapes=[pltpu.VMEM((BW, 256), F32)],
                  compiler_params=_cp("arbitrary"))(zb, z, zb, zb, z, wg2p, bg, ng, tri, mask)


def gla_bwd(dyb, z, zb, states, wg2p, bg, ng, *, name):
    t = z.shape[0]
    tm = GLA_CB * CHUNK
    nb = t // tm
    tri, mask = _gla_consts()
    triu = tri.T

    def body(dy_ref, q_ref, k_ref, v_ref, misc_ref, br_ref, st_ref, sp_ref, w_ref, bg_ref, ng_ref,
             tri_ref, triu_ref, mask_ref,
             dq_ref, dk_ref, dv_ref, dbr_ref, dmisc_ref, dpre_ref, dbg_ref, dng_ref, cc):
        i = pl.program_id(0)

        @pl.when(i == 0)
        def _():
            cc[...] = jnp.zeros_like(cc)
            dbg_ref[...] = jnp.zeros_like(dbg_ref)
            dng_ref[...] = jnp.zeros_like(dng_ref)

        last_row = lax.broadcasted_iota(jnp.int32, (CHUNK, 256), 0) == CHUNK - 1
        for c in range(GLA_CB - 1, -1, -1):
            rows = slice(c * CHUNK, (c + 1) * CHUNK)
            pre = _dot(misc_ref[rows, :], w_ref[...]) + bg_ref[...]
            la = _log_sigmoid(pre) / GLA_TAU
            gc = _dot_hi(tri_ref[...], la)
            gt = gc[CHUNK - 1:CHUNK, :]
            eg = jnp.exp(gt - gc)
            kdec = k_ref[rows, :] * eg
            e = jnp.exp(gt)
            s_n = st_ref[c]
            if c > 0:
                s_prev = st_ref[c - 1]
            else:
                s_prev = jnp.where(i == nb - 1, 0.0, sp_ref[0])
            sb = s_n.astype(BF16)
            qb = q_ref[rows, :]
            o = _dot_nt(qb, sb) * (64.0 ** -0.5)
            br = br_ref[rows, :]
            dy = dy_ref[rows, :]
            do_parts = []
            for hd in range(4):
                cols = slice(hd * 128, (hd + 1) * 128)
                oh = o[:, cols]
                rs = lax.rsqrt(jnp.mean(oh * oh, axis=-1, keepdims=True) + RMS_EPS)
                ohat = oh * rs
                brh = br[:, cols]
                sg = _sigmoid(brh)
                dyh = dy[:, cols]
                ngh = ng_ref[:, cols]
                don = dyh * (brh * sg)
                dbr_ref[rows, cols] = (dyh * (ohat * ngh) * sg * (1.0 + brh * (1.0 - sg))).astype(BF16)
                dng_ref[:, cols] += jnp.sum(don * ohat, axis=0, keepdims=True)
                doh = don * ngh
                do_parts.append(rs * (doh - ohat * jnp.mean(doh * ohat, axis=-1, keepdims=True)))
            dob = jnp.concatenate(do_parts, axis=1).astype(BF16)
            dq_ref[rows, :] = (_dot(dob, sb) * (64.0 ** -0.5)).astype(BF16)
            dst = cc[...] + _dot_tn(dob, qb) * (64.0 ** -0.5) * mask_ref[...]
            dsb = dst.astype(BF16)
            dkdec = _dot(v_ref[rows, :], dsb)
            dv_ref[rows, :] = _dot_nt(kdec.astype(BF16), dsb).astype(BF16)
            dgt = jnp.sum(dst * s_prev, axis=0, keepdims=True) * e
            dk_ref[rows, :] = (dkdec * eg).astype(BF16)
            dd = dkdec * kdec
            dgt = dgt + jnp.sum(dd, axis=0, keepdims=True)
            dgc = jnp.where(last_row, dgt - dd, -dd)
            dla = _dot_hi(triu_ref[...], dgc)
            dpre = dla * (1.0 / GLA_TAU) * _sigmoid(-pre)
            dpb = dpre.astype(BF16)
            dpre_ref[rows, :] = dpb
            dmisc_ref[rows, :] = _dot_nt(dpb, w_ref[...])
            dbg_ref[...] += jnp.sum(dpre, axis=0, keepdims=True)
            cc[...] = dst * e

    rev = lambda w, col: pl.BlockSpec((tm, w), lambda i: (nb - 1 - i, col))
    return _pcall(body, name=name, grid=(nb,),
                  in_specs=[rev(BW, 0), rev(256, BQ // 256), rev(256, BK // 256), rev(BW, BV // BW),
                            rev(128, MISC // 128), rev(BW, BR // BW),
                            pl.BlockSpec((GLA_CB, BW, 256), lambda i: (nb - 1 - i, 0, 0)),
                            pl.BlockSpec((1, BW, 256), lambda i: (jnp.maximum((nb - 1 - i) * GLA_CB - 1, 0), 0, 0)),
                            _fix((128, 256)), _fix((1, 256)), _fix((1, BW)),
                            _fix((CHUNK, CHUNK)), _fix((CHUNK, CHUNK)), _fix((BW, 256))],
                  out_specs=[rev(256, 0), rev(256, 0), rev(BW, 0), rev(BW, 0), rev(128, 0), rev(256, 0),
                             _fix((1, 256)), _fix((1, BW))],
                  out_shape=[jax.ShapeDtypeStruct((t, 256), BF16), jax.ShapeDtypeStruct((t, 256), BF16),
                             jax.ShapeDtypeStruct((t, BW), BF16), jax.ShapeDtypeStruct((t, BW), BF16),
                             jax.ShapeDtypeStruct((t, 128), F32), jax.ShapeDtypeStruct((t, 256), BF16),
                             jax.ShapeDtypeStruct((1, 256), F32), jax.ShapeDtypeStruct((1, BW), F32)],
                  scratch_shapes=[pltpu.VMEM((BW, 256), F32)],
                  compiler_params=_cp("arbitrary"))(dyb, zb, z, zb, zb, z, states, states, wg2p, bg, ng,
                                                    tri, triu, mask)


FOX_SCALE = 64.0 ** -0.5
NEG = -1e30


def fox_fcum(z, bfp, *, name):
    t = z.shape[0]
    tm = min(256, t)
    tri = (jnp.arange(tm)[:, None] >= jnp.arange(tm)[None, :]).astype(F32)

    def body(m_ref, b_ref, tri_ref, o_ref, cc):
        @pl.when(pl.program_id(0) == 0)
        def _():
            cc[...] = jnp.zeros_like(cc)

        lf = _log_sigmoid(m_ref[...] + b_ref[...])
        cs = _dot_hi(tri_ref[...], lf) + cc[...]
        o_ref[...] = cs
        cc[...] = cs[tm - 1:tm, :]

    return _pcall(body, name=name, grid=(t // tm,),
                  in_specs=[_rows(tm, 128, MISC // 128), _fix((1, 128)), _fix((tm, tm))],
                  out_specs=_rows(tm, 128), out_shape=jax.ShapeDtypeStruct((t, 128), F32),
                  scratch_shapes=[pltpu.VMEM((1, 128), F32)],
                  compiler_params=_cp("arbitrary"))(z, bfp, tri)


def fox_dcf(dfc, z, bfp, dmisc_g, *, name):
    t = z.shape[0]
    tm = min(256, t)
    nb = t // tm
    triu = (jnp.arange(tm)[:, None] <= jnp.arange(tm)[None, :]).astype(F32)

    def body(d_ref, m_ref, b_ref, g_ref, tri_ref, o_ref, dbf_ref, cc):
        @pl.when(pl.program_id(0) == 0)
        def _():
            cc[...] = jnp.zeros_like(cc)
            dbf_ref[...] = jnp.zeros_like(dbf_ref)

        rc = _dot_hi(tri_ref[...], d_ref[...]) + cc[...]
        cc[...] = rc[0:1, :]
        dcf = rc * _sigmoid(-(m_ref[...] + b_ref[...]))
        o_ref[...] = (dcf + g_ref[...]).astype(BF16)
        dbf_ref[...] += jnp.sum(dcf, axis=0, keepdims=True)

    rev = lambda col: pl.BlockSpec((tm, 128), lambda i: (nb - 1 - i, col))
    return _pcall(body, name=name, grid=(nb,),
                  in_specs=[rev(0), rev(MISC // 128), _fix((1, 128)), rev(0), _fix((tm, tm))],
                  out_specs=[rev(0), _fix((1, 128))],
                  out_shape=[jax.ShapeDtypeStruct((t, 128), BF16), jax.ShapeDtypeStruct((1, 128), F32)],
                  scratch_shapes=[pltpu.VMEM((1, 128), F32)],
                  compiler_params=_cp("arbitrary"))(dfc, z, bfp, dmisc_g, triu)


def fox_fwd(zb, fq, fk, *, name):
    t = zb.shape[0]
    tq = min(512, t)
    nq = t // tq

    def body(q_ref, k_ref, v_ref, fq_ref, fk_ref, y_ref, yf_ref, lse_ref, m_s, l_s, acc):
        i, j = pl.program_id(1), pl.program_id(2)

        @pl.when(j == 0)
        def _():
            m_s[...] = jnp.full_like(m_s, NEG)
            l_s[...] = jnp.zeros_like(l_s)
            acc[...] = jnp.zeros_like(acc)

        lo = lax.broadcasted_iota(jnp.int32, (tq, 128), 1) < 64

        def work(diagonal):
            q = q_ref[...]
            k = k_ref[...]
            v = v_ref[...]
            if diagonal:
                row = lax.broadcasted_iota(jnp.int32, (tq, tq), 0)
                col = lax.broadcasted_iota(jnp.int32, (tq, tq), 1)
                keep = col <= row
            upd = []
            for hh in range(2):
                sel = lo if hh == 0 else jnp.logical_not(lo)
                qh = jnp.where(sel, q, jnp.zeros_like(q))
                s = _dot_nt(qh, k) * FOX_SCALE + fq_ref[hh] - fk_ref[hh]
                if diagonal:
                    s = jnp.where(keep, s, NEG)
                m_old = m_s[hh]
                m_new = jnp.maximum(m_old, jnp.max(s, axis=-1, keepdims=True))
                p = jnp.exp(s - m_new)
                corr = jnp.exp(m_old - m_new)
                l_s[hh] = l_s[hh] * corr + jnp.sum(p, axis=-1, keepdims=True)
                m_s[hh] = m_new
                upd.append(acc[...] * corr + _dot(p.astype(BF16), v))
            acc[...] = jnp.where(lo, upd[0], upd[1])

        @pl.when(j < i)
        def _():
            work(False)

        @pl.when(j == i)
        def _():
            work(True)

        @pl.when(j == nq - 1)
        def _():
            out = acc[...] * jnp.where(lo, 1.0 / l_s[0], 1.0 / l_s[1])
            y_ref[...] = out.astype(BF16)
            yf_ref[...] = out
            lse_ref[...] = m_s[...] + jnp.log(l_s[...])

    kv = lambda off: pl.BlockSpec((tq, 128), lambda h, i, j: (jnp.minimum(j, i), off // 128 + h))
    return _pcall(body, name=name, grid=(4, nq, nq),
                  in_specs=[pl.BlockSpec((tq, 128), lambda h, i, j: (i, CQ // 128 + h)), kv(CK), kv(CV),
                            pl.BlockSpec((2, tq, 1), lambda h, i, j: (h, i, 0)),
                            pl.BlockSpec((2, 1, tq), lambda h, i, j: (h, 0, jnp.minimum(j, i)))],
                  out_specs=[pl.BlockSpec((tq, 128), lambda h, i, j: (i, h)),
                             pl.BlockSpec((tq, 128), lambda h, i, j: (i, h)),
                             pl.BlockSpec((2, tq, 1), lambda h, i, j: (h, i, 0))],
                  out_shape=[jax.ShapeDtypeStruct((t, BW), BF16), jax.ShapeDtypeStruct((t, BW), F32),
                             jax.ShapeDtypeStruct((FOX_H, t, 1), F32)],
                  scratch_shapes=[pltpu.VMEM((2, tq, 1), F32), pltpu.VMEM((2, tq, 1), F32),
                                  pltpu.VMEM((tq, 128), F32)],
                  compiler_params=_cp("parallel", "parallel", "arbitrary"))(zb, zb, zb, fq, fk)


def fox_delta(dyc, ycf, *, name):
    t = dyc.shape[0]
    tm = min(256, t)
    seg = ((jnp.arange(BW)[:, None] // 64) == jnp.arange(128)[None, :]).astype(F32)

    def body(d_ref, o_ref, s_ref, out_ref):
        out_ref[...] = _dot_hi(d_ref[...] * o_ref[...], s_ref[...])

    return _pcall(body, name=name, grid=(t // tm,),
                  in_specs=[_rows(tm, BW), _rows(tm, BW), _fix((BW, 128))],
                  out_specs=_rows(tm, 128), out_shape=jax.ShapeDtypeStruct((t, 128), F32),
                  compiler_params=_cp("parallel"))(dyc, ycf, seg)


def fox_bwd(zb, dyc, fq, fk, lse, dl, *, name):
    t = zb.shape[0]
    tq = min(512, t)
    nq = t // tq

    def body(q_ref, k_ref, v_ref, do_ref, fq_ref, fk_ref, lse_ref, dl_ref,
             dq_ref, dk_ref, dv_ref, dfk_ref, dfq_ref, dk_s, dv_s, df_s):
        j, i = pl.program_id(1), pl.program_id(2)

        @pl.when(jnp.logical_and(j == 0, i == 0))
        def _():
            dq_ref[...] = jnp.zeros_like(dq_ref)
            dfq_ref[...] = jnp.zeros_like(dfq_ref)

        @pl.when(i == 0)
        def _():
            dk_s[...] = jnp.zeros_like(dk_s)
            dv_s[...] = jnp.zeros_like(dv_s)
            df_s[...] = jnp.zeros_like(df_s)

        lo = lax.broadcasted_iota(jnp.int32, (tq, 128), 1) < 64

        def work(diagonal):
            q = q_ref[...]
            k = k_ref[...]
            v = v_ref[...]
            dob = do_ref[...].astype(BF16)
            if diagonal:
                row = lax.broadcasted_iota(jnp.int32, (tq, tq), 0)
                col = lax.broadcasted_iota(jnp.int32, (tq, tq), 1)
                keep = col <= row
            dvs, dks, dqs, rsum = [], [], [], []
            for hh in range(2):
                sel = lo if hh == 0 else jnp.logical_not(lo)
                qh = jnp.where(sel, q, jnp.zeros_like(q))
                doh = jnp.where(sel, dob, jnp.zeros_like(dob))
                s = _dot_nt(qh, k) * FOX_SCALE + fq_ref[hh] - fk_ref[hh]
                p = jnp.exp(s - lse_ref[hh])
                if diagonal:
                    p = jnp.where(keep, p, 0.0)
                dp = _dot_nt(doh, v)
                ds = p * (dp - dl_ref[hh])
                dsb = (ds * FOX_SCALE).astype(BF16)
                dvs.append(_dot_tn(p.astype(BF16), dob))
                dks.append(_dot_tn(dsb, q))
                dqs.append(_dot(dsb, k))
                df_s[hh] += jnp.sum(ds, axis=0, keepdims=True)
                rsum.append(jnp.sum(ds, axis=-1, keepdims=True))
            dv_s[...] += jnp.where(lo, dvs[0], dvs[1])
            dk_s[...] += jnp.where(lo, dks[0], dks[1])
            r0 = pl.multiple_of(i * tq, tq)
            dq_ref[pl.ds(r0, tq), :] += jnp.where(lo, dqs[0], dqs[1])
            dfq_ref[pl.ds(r0, tq), :] += jnp.where(lo, rsum[0], rsum[1])

        @pl.when(i > j)
        def _():
            work(False)

        @pl.when(i == j)
        def _():
            work(True)

        @pl.when(i == nq - 1)
        def _():
            dk_ref[...] = dk_s[...].astype(BF16)
            dv_ref[...] = dv_s[...].astype(BF16)
            dfk_ref[...] = -df_s[...]

    qi = lambda j, i: jnp.maximum(i, j)
    return _pcall(body, name=name, grid=(4, nq, nq),
                  in_specs=[pl.BlockSpec((tq, 128), lambda h, j, i: (qi(j, i), CQ // 128 + h)),
                            pl.BlockSpec((tq, 128), lambda h, j, i: (j, CK // 128 + h)),
                            pl.BlockSpec((tq, 128), lambda h, j, i: (j, CV // 128 + h)),
                            pl.BlockSpec((tq, 128), lambda h, j, i: (qi(j, i), h)),
                            pl.BlockSpec((2, tq, 1), lambda h, j, i: (h, qi(j, i), 0)),
                            pl.BlockSpec((2, 1, tq), lambda h, j, i: (h, 0, j)),
                            pl.BlockSpec((2, tq, 1), lambda h, j, i: (h, qi(j, i), 0)),
                            pl.BlockSpec((2, tq, 1), lambda h, j, i: (h, qi(j, i), 0))],
                  out_specs=[pl.BlockSpec((t, 128), lambda h, j, i: (0, h)),
                             pl.BlockSpec((tq, 128), lambda h, j, i: (j, h)),
                             pl.BlockSpec((tq, 128), lambda h, j, i: (j, h)),
                             pl.BlockSpec((2, 1, tq), lambda h, j, i: (h, 0, j)),
                             pl.BlockSpec((t, 128), lambda h, j, i: (0, h))],
                  out_shape=[jax.ShapeDtypeStruct((t, BW), F32), jax.ShapeDtypeStruct((t, BW), BF16),
                             jax.ShapeDtypeStruct((t, BW), BF16), jax.ShapeDtypeStruct((FOX_H, 1, t), F32),
                             jax.ShapeDtypeStruct((t, BW), F32)],
                  scratch_shapes=[pltpu.VMEM((tq, 128), F32), pltpu.VMEM((tq, 128), F32),
                                  pltpu.VMEM((2, 1, tq), F32)],
                  compiler_params=_cp("parallel", "arbitrary", "arbitrary"))(zb, zb, zb, dyc, fq, fk, lse, dl)


def merge_fwd(ya, yb, yc, wbr, z, *, name):
    t = ya.shape[0]
    tm = min(256, t)

    def body(ya_ref, yb_ref, yc_ref, w_ref, g0_ref, g1_ref, g2_ref, o_ref):
        m = _sigmoid(g0_ref[...]) * _dot(ya_ref[...], w_ref[0])
        m = m + _sigmoid(g1_ref[...]) * _dot(yb_ref[...], w_ref[1])
        m = m + _sigmoid(g2_ref[...]) * _dot(yc_ref[...], w_ref[2])
        o_ref[...] = m.astype(BF16)

    return _pcall(body, name=name, grid=(t // tm,),
                  in_specs=[_rows(tm, BW)] * 3 + [_fix((3, BW, D))]
                  + [_rows(tm, D, G0 // D + j) for j in range(3)],
                  out_specs=_rows(tm, D), out_shape=jax.ShapeDtypeStruct((t, D), BF16),
                  compiler_params=_cp("parallel"))(ya, yb, yc, wbr, z, z, z)


def merge_bwd(doutb, wo, l, ya, yb, yc, wbr, z, *, name):
    t = ya.shape[0]
    tm = min(256, t)

    def body(do_ref, wo_ref, ya_ref, yb_ref, yc_ref, w_ref, g0_ref, g1_ref, g2_ref,
             dya_ref, dyb_ref, dyc_ref, dp0_ref, dp1_ref, dp2_ref, dg0_ref, dg1_ref, dg2_ref):
        dm = _dot_nt(do_ref[...], wo_ref[...])
        ys = (ya_ref, yb_ref, yc_ref)
        gs = (g0_ref, g1_ref, g2_ref)
        dys = (dya_ref, dyb_ref, dyc_ref)
        dps = (dp0_ref, dp1_ref, dp2_ref)
        dgs = (dg0_ref, dg1_ref, dg2_ref)
        for j in range(3):
            s = _sigmoid(gs[j][...])
            pj = _dot(ys[j][...], w_ref[j])
            dpb = (dm * s).astype(BF16)
            dps[j][...] = dpb
            dgs[j][...] = (dm * pj * s * (1.0 - s)).astype(BF16)
            dys[j][...] = _dot_nt(dpb, w_ref[j])

    yshape = jax.ShapeDtypeStruct((t, BW), F32)
    dshape = jax.ShapeDtypeStruct((t, D), BF16)
    return _pcall(body, name=name, grid=(t // tm,),
                  in_specs=[_rows(tm, D), _layer(l, (D, D))] + [_rows(tm, BW)] * 3
                  + [_fix((3, BW, D))] + [_rows(tm, D, G0 // D + j) for j in range(3)],
                  out_specs=[_rows(tm, BW)] * 3 + [_rows(tm, D)] * 6,
                  out_shape=[yshape] * 3 + [dshape] * 6,
                  compiler_params=_cp("parallel"))(doutb, wo, ya, yb, yc, wbr, z, z, z)


def adamw(w, g, m, v, *, name):
    r, c = w.shape
    tm = _pick(r, (128, 64, 32, 16, 8))

    def body(w_ref, g_ref, m_ref, v_ref, d_ref, mo_ref, vo_ref):
        gg = g_ref[...]
        mn = ADAM_B1 * m_ref[...] + (1.0 - ADAM_B1) * gg
        vn = ADAM_B2 * v_ref[...] + (1.0 - ADAM_B2) * (gg * gg)
        m_hat = mn / (1.0 - ADAM_B1 ** ADAM_STEP)
        v_hat = vn / (1.0 - ADAM_B2 ** ADAM_STEP)
        d_ref[...] = -ADAM_LR * (m_hat / (jnp.sqrt(v_hat) + ADAM_EPS) + ADAM_WD * w_ref[...])
        mo_ref[...] = mn
        vo_ref[...] = vn

    shp = jax.ShapeDtypeStruct((r, c), F32)
    return _pcall(body, name=name, grid=(r // tm,), in_specs=[_rows(tm, c)] * 4, out_specs=[_rows(tm, c)] * 3,
                  out_shape=[shp] * 3, compiler_params=_cp("parallel"))(w, g, m, v)


def _place():
    return lax.axis_index("x"), lax.axis_index("y"), lax.axis_index("c")


def _remote(src, dst, send_sems, recv_sems, k, to):
    return pltpu.make_async_remote_copy(src_ref=src, dst_ref=dst, send_sem=send_sems.at[k],
                                        recv_sem=recv_sems.at[k], device_id=to, device_id_type=MESH)


def gather_weights(shards):
    n = len(shards)

    def body(*refs):
        ins, outs = refs[:n], refs[n:2 * n]
        send_sems, recv_sems, local_sems = refs[2 * n:]
        x, y, c = _place()
        sib = (x, y, 1 - c)
        chips = [(1 - x, y), (x, 1 - y), (1 - x, 1 - y)]
        k_me = 2 * x + y
        mine, first, passed = [], [], []
        for t in range(n):
            for l in range(DEPTH):
                mine.append(pltpu.make_async_copy(ins[t].at[l], outs[t].at[l, k_me], local_sems.at[2 * t + l]))
            for j, chip in enumerate(chips):
                first.append(_remote(ins[t].at[c], outs[t].at[c, k_me], send_sems, recv_sems, 6 * t + j, (*chip, c)))
        for cp in mine + first:
            cp.start()
        for t in range(n):
            for j, chip in enumerate(chips):
                blk = outs[t].at[c, 2 * chip[0] + chip[1]]
                _remote(blk, blk, send_sems, recv_sems, 6 * t + j, (*chip, c)).wait_recv()
                cp = _remote(blk, blk, send_sems, recv_sems, 6 * t + 3 + j, sib)
                cp.start()
                passed.append(cp)
        for t in range(n):
            for j, chip in enumerate(chips):
                blk = outs[t].at[1 - c, 2 * chip[0] + chip[1]]
                _remote(blk, blk, send_sems, recv_sems, 6 * t + 3 + j, sib).wait_recv()
        for cp in first + passed:
            cp.wait_send()
        for cp in mine:
            cp.wait()

    return _pcall(body, name="gather_weights", in_specs=[ANY] * n, out_specs=[ANY] * n,
                  out_shape=[jax.ShapeDtypeStruct((DEPTH, 4) + s.shape[1:], s.dtype) for s in shards],
                  scratch_shapes=[pltpu.SemaphoreType.DMA((6 * n,)), pltpu.SemaphoreType.DMA((6 * n,)),
                                  pltpu.SemaphoreType.DMA((2 * n,))])(*shards)


def pair_exchange(g0, g1):
    n = len(g0)

    def body(*refs):
        a0, a1, outs = refs[:n], refs[n:2 * n], refs[2 * n:3 * n]
        send_sems, recv_sems = refs[3 * n:]
        x, y, c = _place()
        sib = (x, y, 1 - c)

        @pl.when(c == 0)
        def _():
            for t in range(n):
                _remote(a1[t], outs[t], send_sems, recv_sems, t, sib).start()

        @pl.when(c == 1)
        def _():
            for t in range(n):
                _remote(a0[t], outs[t], send_sems, recv_sems, t, sib).start()

        for t in range(n):
            _remote(a0[t], outs[t], send_sems, recv_sems, t, sib).wait()

    return _pcall(body, name="pair_exchange", in_specs=[ANY] * (2 * n), out_specs=[ANY] * n,
                  out_shape=[jax.ShapeDtypeStruct(a.shape, a.dtype) for a in g0],
                  scratch_shapes=[pltpu.SemaphoreType.DMA((n,)), pltpu.SemaphoreType.DMA((n,))])(*g0, *g1)


def chip_exchange(s1):
    n = len(s1)

    def body(*refs):
        ins, outs = refs[:n], refs[n:2 * n]
        send_sems, recv_sems = refs[2 * n:]
        x, y, c = _place()
        chips = [(1 - x, y), (x, 1 - y), (1 - x, 1 - y)]
        cps = [_remote(ins[t].at[2 * chip[0] + chip[1]], outs[t].at[j], send_sems, recv_sems, 3 * t + j, (*chip, c))
               for t in range(n) for j, chip in enumerate(chips)]
        for cp in cps:
            cp.start()
        for cp in cps:
            cp.wait()

    return _pcall(body, name="chip_exchange", in_specs=[ANY] * n, out_specs=[ANY] * n,
                  out_shape=[jax.ShapeDtypeStruct((3,) + a.shape[1:], a.dtype) for a in s1],
                  scratch_shapes=[pltpu.SemaphoreType.DMA((3 * n,)), pltpu.SemaphoreType.DMA((3 * n,))])(*s1)


def pair_share(s2):
    n = len(s2)

    def body(*refs):
        ins, outs = refs[:n], refs[n:2 * n]
        send_sems, recv_sems, local_sems = refs[2 * n:]
        x, y, c = _place()
        sib = (x, y, 1 - c)
        mine = [pltpu.make_async_copy(ins[t], outs[t].at[c], local_sems.at[t]) for t in range(n)]
        cps = [_remote(ins[t], outs[t].at[c], send_sems, recv_sems, t, sib) for t in range(n)]
        for cp in mine + cps:
            cp.start()
        for t in range(n):
            cps[t].wait_send()
            _remote(ins[t], outs[t].at[1 - c], send_sems, recv_sems, t, sib).wait_recv()
        for cp in mine:
            cp.wait()

    return _pcall(body, name="pair_share", in_specs=[ANY] * n, out_specs=[ANY] * n,
                  out_shape=[jax.ShapeDtypeStruct((DEPTH,) + a.shape, a.dtype) for a in s2],
                  scratch_shapes=[pltpu.SemaphoreType.DMA((n,)), pltpu.SemaphoreType.DMA((n,)),
                                  pltpu.SemaphoreType.DMA((n,))])(*s2)


def small_exchange(gs):
    rows, width = gs.shape

    def body(g_ref, o_ref, send_sems, recv_sems):
        x, y, c = _place()
        cps = []
        for r in range(1, 8):
            dx, dy, dc = (r >> 2) & 1, (r >> 1) & 1, r & 1
            to = (x if dx == 0 else 1 - x, y if dy == 0 else 1 - y, c if dc == 0 else 1 - c)
            cps.append(_remote(g_ref, o_ref.at[r - 1], send_sems, recv_sems, r - 1, to))
        for cp in cps:
            cp.start()
        for cp in cps:
            cp.wait()

    return _pcall(body, name="small_exchange", in_specs=[ANY], out_specs=ANY,
                  out_shape=jax.ShapeDtypeStruct((7, rows, width), gs.dtype),
                  scratch_shapes=[pltpu.SemaphoreType.DMA((7,)), pltpu.SemaphoreType.DMA((7,))])(gs)


def _row_tile(rows):
    return _pick(rows, (256, 352, 128, 64, 32, 16))


def pair_add(g0, g1, rb, core, *, name):
    _, rows, width = g0.shape
    tr = _row_tile(rows)

    def body(c_ref, g0_ref, g1_ref, r_ref, o_ref, ob_ref):
        s = jnp.where(c_ref[0] == 0, g0_ref[...], g1_ref[...]) + r_ref[...]
        o_ref[...] = s
        ob_ref[...] = s.astype(BF16)

    blk = pl.BlockSpec((None, tr, width), lambda k, i, c_ref: (k, i, 0))
    gs = pltpu.PrefetchScalarGridSpec(
        num_scalar_prefetch=1, grid=(4, rows // tr),
        in_specs=[pl.BlockSpec((None, tr, width), lambda k, i, c_ref: (k * (1 - c_ref[0]), i * (1 - c_ref[0]), 0)),
                  pl.BlockSpec((None, tr, width), lambda k, i, c_ref: (k * c_ref[0], i * c_ref[0], 0)), blk],
        out_specs=[blk, blk])
    return _pcall(body, name=name, grid_spec=gs,
                  out_shape=[jax.ShapeDtypeStruct(g0.shape, F32), jax.ShapeDtypeStruct(g0.shape, BF16)],
                  compiler_params=_cp("parallel", "parallel"))(core, g0, g1, rb)


def chip_add(s1, rb2, chip, *, name):
    _, rows, width = s1.shape
    tr = _row_tile(rows)

    def body(k_ref, s_ref, r_ref, o_ref):
        o_ref[...] = ((s_ref[...] + r_ref[0].astype(F32)) + r_ref[1].astype(F32)) + r_ref[2].astype(F32)

    gs = pltpu.PrefetchScalarGridSpec(
        num_scalar_prefetch=1, grid=(rows // tr,),
        in_specs=[pl.BlockSpec((None, tr, width), lambda i, k_ref: (k_ref[0], i, 0)),
                  pl.BlockSpec((3, tr, width), lambda i, k_ref: (0, i, 0))],
        out_specs=pl.BlockSpec((tr, width), lambda i, k_ref: (i, 0)))
    return _pcall(body, name=name, grid_spec=gs, out_shape=jax.ShapeDtypeStruct((rows, width), F32),
                  compiler_params=_cp("parallel"))(chip, s1, rb2)


def small_add(gs_own, slots, me):
    rows, width = gs_own.shape
    tr = _pick(rows, (64, 32, 16, 8))

    def body(me_ref, g_ref, s_ref, o_ref):
        me_v = me_ref[0]
        total = None
        for d in range(8):
            rel = jnp.bitwise_xor(me_v, d)
            val = jnp.where(rel == 0, g_ref[...], s_ref[jnp.maximum(rel - 1, 0)])
            total = val if total is None else total + val
        o_ref[...] = total

    gs = pltpu.PrefetchScalarGridSpec(
        num_scalar_prefetch=1, grid=(rows // tr,),
        in_specs=[pl.BlockSpec((tr, width), lambda i, m_ref: (i, 0)),
                  pl.BlockSpec((7, tr, width), lambda i, m_ref: (0, i, 0))],
        out_specs=pl.BlockSpec((tr, width), lambda i, m_ref: (i, 0)))
    return _pcall(body, name="small_add", grid_spec=gs, out_shape=jax.ShapeDtypeStruct((rows, width), F32),
                  compiler_params=_cp("parallel"))(me, gs_own, slots)


SHARDED = (("ffn1_w_up", (D, UPW)), ("ffn1_w_down", (DFF // 4, D)), ("w_in", (D, D_IN // 4)),
           ("conv_w", (4, BW // 4)), ("gla_w_g2", (LOW_W, 64)), ("w_branch", (3 * BW, D // 4)),
           ("w_out", (D // 4, D)), ("ffn2_w_up", (D, UPW)), ("ffn2_w_down", (DFF // 4, D)),
           ("ple_w_proj", (PLE, D // 4)), ("ple_w_gate", (D // 4, D)))
SMALL = ("ln1_g", "ln1_b", "conv_b", "lru_wa", "lru_ba", "lru_wx", "lru_bx", "lru_lambda", "gla_b_g",
         "gla_norm_g", "fox_b_f", "ln2_g", "ln2_b", "ln3_g", "ln3_b", "ple_b_gate", "ln4_g", "ln4_b")
WEIGHTS = ('ffn1_w_up', 'ffn1_w_down', 'ln1_g', 'ln1_b', 'w_in', 'conv_w', 'conv_b', 'lru_wa', 'lru_ba',
           'lru_wx', 'lru_bx', 'lru_lambda', 'gla_w_g2', 'gla_b_g', 'gla_norm_g', 'fox_b_f', 'w_branch',
           'w_out', 'ln2_g', 'ln2_b', 'ffn2_w_up', 'ffn2_w_down', 'ln3_g', 'ln3_b', 'ple_w_proj',
           'ple_w_gate', 'ple_b_gate', 'ln4_g', 'ln4_b')


def _pack(parts, rows):
    flat = jnp.concatenate([p.reshape(-1) for p in parts])
    flat = jnp.pad(flat, (0, rows * PACK_W - flat.shape[0]))
    return flat.reshape(rows, PACK_W)


def _cols_join(parts):
    return jnp.concatenate([parts[k] for k in range(4)], axis=-1)


def _cols_split(full):
    r, c4 = full.shape
    return full.reshape(r, 4, c4 // 4).transpose(1, 0, 2)


def _regroup_in(w):
    pad = jnp.zeros(w.shape[:-1] + (ZW - D_IN,), w.dtype)
    return jnp.concatenate([w[..., 0:2048], w[..., 2064:4112], w[..., 4120:7192], w[..., 2048:2064],
                            w[..., 4112:4120], pad], axis=-1)


def _regroup_out(g):
    return jnp.concatenate([g[..., 0:2048], g[..., 7168:7184], g[..., 2048:4096], g[..., 7184:7192],
                            g[..., 4096:7168]], axis=-1)


def _block_diag(w):
    eye = jnp.eye(8, dtype=w.dtype)
    return (eye[:, None, :, None] * w[:, :, None, :]).reshape(BW, BW)


def _diag_blocks(dense):
    return jnp.stack([dense[64 * n:64 * (n + 1), 64 * n:64 * (n + 1)] for n in range(8)])


def _layer_weights(gw, small, l):
    w = {"up1": gw["ffn1_w_up"], "up2": gw["ffn2_w_up"],
         "dn1": gw["ffn1_w_down"].reshape(DEPTH, DFF, D), "dn2": gw["ffn2_w_down"].reshape(DEPTH, DFF, D),
         "wo": gw["w_out"].reshape(DEPTH, D, D), "wgt": gw["ple_w_gate"].reshape(DEPTH, D, D)}
    w["win"] = _regroup_in(_cols_join(gw["w_in"][l]))
    w["cw"] = _cols_join(gw["conv_w"][l])
    w["wa"] = _block_diag(small["lru_wa"][l]).astype(BF16)
    w["wx"] = _block_diag(small["lru_wx"][l]).astype(BF16)
    w["wg2p"] = jnp.pad(_cols_join(gw["gla_w_g2"][l]), ((0, 128 - LOW_W), (0, 0)))
    w["wbr"] = _cols_join(gw["w_branch"][l].reshape(4, 3, BW, D // 4))
    w["wp"] = _cols_join(gw["ple_w_proj"][l])
    for n in ("ln1_g", "ln1_b", "ln2_g", "ln2_b", "ln3_g", "ln3_b", "ln4_g", "ln4_b", "conv_b", "lru_ba",
              "lru_bx", "lru_lambda", "gla_b_g", "gla_norm_g", "ple_b_gate"):
        w[n] = small[n][l][None, :]
    w["bfp"] = jnp.pad(small["fox_b_f"][l], (LOW_W, 128 - LOW_W - FOX_H))[None, :]
    return w


def _heads_t(a):
    ht = a[:, LOW_W:LOW_W + FOX_H].T
    return ht[:, :, None], ht[:, None, :]


def _layer_fwd(x, xb, pb, w, l):
    s = {"x0": x, "x0b": xb}
    tag = "l%d_" % l
    gate, up, act = ffn_up(xb, w["up1"], l, name=tag + "ffn1_up")
    r1, x1, x1b = matmul_res_ln(act, w["dn1"], l, x, w["ln1_g"], w["ln1_b"], mm_scale=0.5, name=tag + "ffn1_down")
    s.update(gate1=gate, up1=up, act1=act, r1=r1, x1=x1, x1b=x1b)
    z, zb = matmul(x1b, w["win"], also_bf16=True, tn=_pick(ZW, (2432,)), name=tag + "mix_in")
    xc, xcb, h, ya = lru_fwd(z, w["cw"], w["conv_b"], w["wa"], w["wx"], w["lru_ba"], w["lru_bx"],
                             w["lru_lambda"], name=tag + "lru_fwd")
    yb, states = gla_fwd(z, zb, w["wg2p"], w["gla_b_g"], w["gla_norm_g"], name=tag + "gla_fwd")
    fcum = fox_fcum(z, w["bfp"], name=tag + "fox_fcum")
    fq, fk = _heads_t(fcum)
    yc, ycf, lse = fox_fwd(zb, fq, fk, name=tag + "fox_fwd")
    merged = merge_fwd(ya, yb, yc, w["wbr"], z, name=tag + "merge_fwd")
    r2, x2, x2b = matmul_res_ln(merged, w["wo"], l, x1, w["ln2_g"], w["ln2_b"], mm_scale=1.0, name=tag + "mix_out")
    s.update(z=z, zb=zb, xc=xc, xcb=xcb, h=h, ya=ya, yb=yb, states=states, fq=fq, fk=fk, yc=yc, ycf=ycf,
             lse=lse, merged=merged, r2=r2, x2=x2, x2b=x2b)
    gate, up, act = ffn_up(x2b, w["up2"], l, name=tag + "ffn2_up")
    r3, x3, x3b = matmul_res_ln(act, w["dn2"], l, x2, w["ln3_g"], w["ln3_b"], mm_scale=0.5, name=tag + "ffn2_down")
    s.update(gate2=gate, up2=up, act2=act, r3=r3, x3=x3, x3b=x3b)
    r4, x4, x4b = ple_fwd(x3b, x3, pb, w["wgt"], l, w["wp"], w["ple_b_gate"], w["ln4_g"], w["ln4_b"],
                          name=tag + "ple_fwd")
    s.update(r4=r4, pb=pb)
    return x4, x4b, s


def _ffn_bwd(dy, s, w, n, xin_b, l, tag):
    k = {"1": ("r1", "ln1_g", "gate1", "up1", "act1"), "2": ("r3", "ln3_g", "gate2", "up2", "act2")}[n]
    dr, dfb, dg, db = ln_bwd(dy, s[k[0]], w[k[1]], out_scale=0.5, name=tag + "ln_bwd")
    dgate, dup = ffn_down_bwd(dfb, w["dn" + n], l, s[k[2]], s[k[3]], name=tag + "down_bwd")
    dx = ffn_dx(dgate, dup, w["up" + n], l, dr, name=tag + "dx")
    dwup = matmul_tn_up(xin_b, dgate, dup, name=tag + "dw_up")
    dwdn = matmul_tn(s[k[4]], dfb, name=tag + "dw_down").reshape(4, DFF // 4, D)
    return dx, dwup, dwdn, dg[0], db[0]


def _layer_bwd(dy, s, w, l):
    g = {}
    tag = "l%d_" % l
    dr4, dglb, dpeb, dg4, db4, dbg = ple_bwd(dy, s["r4"], s["x3b"], s["pb"], w["wgt"], l, w["wp"], w["ple_b_gate"],
                                             w["ln4_g"], name=tag + "ple_bwd")
    dx3 = matmul(dglb, w["wgt"], nt=True, b_lead=(l,), res=dr4, res_scale=ALPHA, name=tag + "ple_dx")
    g["ple_w_gate"] = matmul_tn(s["x3b"], dglb, name=tag + "ple_dw_gate").reshape(4, D // 4, D)
    g["ple_w_proj"] = _cols_split(matmul_tn(s["pb"], dpeb, name=tag + "ple_dw_proj"))
    g["ln4_g"], g["ln4_b"], g["ple_b_gate"] = dg4[0], db4[0], dbg[0]
    dx2, g["ffn2_w_up"], g["ffn2_w_down"], g["ln3_g"], g["ln3_b"] = _ffn_bwd(dx3, s, w, "2", s["x2b"], l,
                                                                             tag + "ffn2_")
    dr2, doutb, dg2, db2 = ln_bwd(dx2, s["r2"], w["ln2_g"], out_scale=1.0, name=tag + "mix_ln_bwd")
    g["ln2_g"], g["ln2_b"] = dg2[0], db2[0]
    g["w_out"] = matmul_tn(s["merged"], doutb, name=tag + "dw_out").reshape(4, D // 4, D)
    z, zb = s["z"], s["zb"]
    (dya, dyb, dyc, dp0, dp1, dp2, dgl0, dgl1, dgl2) = merge_bwd(
        doutb, w["wo"], l, s["ya"], s["yb"], s["yc"], w["wbr"], z, name=tag + "merge_bwd")
    dwbr = jnp.stack([matmul_tn(s["ya"], dp0, name=tag + "dw_br0"), matmul_tn(s["yb"], dp1, name=tag + "dw_br1"),
                      matmul_tn(s["yc"], dp2, name=tag + "dw_br2")])
    g["w_branch"] = _cols_split(dwbr.reshape(3 * BW, D))
    day, dxc, dprb, dpib, dba, dbx, dlam = lru_bwd(dya, z, s["h"], s["xc"], w["wa"], w["wx"],
                                                   w["lru_ba"], w["lru_bx"], w["lru_lambda"], name=tag + "lru_bwd")
    dax, dcw, dcb = conv_bwd(dxc, z, w["cw"], name=tag + "conv_bwd")
    g["lru_wa"] = _diag_blocks(matmul_tn(s["xcb"], dprb, name=tag + "dw_lru_a"))
    g["lru_wx"] = _diag_blocks(matmul_tn(s["xcb"], dpib, name=tag + "dw_lru_x"))
    g["lru_ba"], g["lru_bx"], g["lru_lambda"] = dba[0], dbx[0], dlam[0]
    g["conv_w"], g["conv_b"] = _cols_split(dcw), dcb[0]
    dbq, dbk, dbv, dbr, dmisc_g, dpreb, dbgg, dng = gla_bwd(dyb, z, zb, s["states"], w["wg2p"], w["gla_b_g"],
                                                            w["gla_norm_g"], name=tag + "gla_bwd")
    miscb = zb[:, MISC:]
    g["gla_w_g2"] = _cols_split(matmul_tn(miscb, dpreb, name=tag + "dw_g2")[:LOW_W])
    g["gla_b_g"], g["gla_norm_g"] = dbgg[0], dng[0]
    dl = fox_delta(dyc, s["ycf"], name=tag + "fox_delta")
    dlq = dl[:, :FOX_H].T[:, :, None]
    dcq, dck, dcv, dfk, dfq = fox_bwd(zb, dyc, s["fq"], s["fk"], s["lse"], dlq, name=tag + "fox_bwd")
    dfc = jnp.pad(dfk[:, 0, :].T + dfq[:, ::64], ((0, 0), (LOW_W, 128 - LOW_W - FOX_H)))
    dmiscb, dbf = fox_dcf(dfc, z, w["bfp"], dmisc_g, name=tag + "fox_dcf")
    g["fox_b_f"] = dbf[0, LOW_W:LOW_W + FOX_H]
    dz = jnp.concatenate([dax, day, dbq, dbk, dbv, dbr, dcq.astype(BF16), dck, dcv, dgl0, dgl1, dgl2, dmiscb],
                         axis=1)
    dx1 = matmul(dz, w["win"], nt=True, res=dr2, res_scale=ALPHA, tk=_pick(ZW, (2432,)), name=tag + "mix_dx")
    g["w_in"] = _cols_split(_regroup_out(matmul_tn(s["x1b"], dz, name=tag + "dw_in")))
    dx0, g["ffn1_w_up"], g["ffn1_w_down"], g["ln1_g"], g["ln1_b"] = _ffn_bwd(dx1, s, w, "1", s["x0b"], l,
                                                                             tag + "ffn1_")
    return dx0, g


def _local_step(x, p, target, gw, small):
    xcur = x
    xb = xcur.astype(BF16)
    layer_w, saved = [], []
    for l in range(DEPTH):
        w = _layer_weights(gw, small, l)
        xcur, xb, s = _layer_fwd(xcur, xb, p[l].astype(BF16), w, l)
        layer_w.append(w)
        saved.append(s)
    dy, sq = loss_head(xcur, target, name="loss_head")
    grads = [None] * DEPTH
    for l in reversed(range(DEPTH)):
        dy, grads[l] = _layer_bwd(dy, saved[l], layer_w[l], l)
    return 0.5 * jnp.sum(sq) / float(D), dy, grads


def kernel(x, p, ffn1_w_up, ffn1_w_down, ln1_g, ln1_b, w_in, conv_w, conv_b, lru_wa, lru_ba, lru_wx, lru_bx, lru_lambda, gla_w_g2, gla_b_g, gla_norm_g, fox_b_f, w_branch, w_out, ln2_g, ln2_b, ffn2_w_up, ffn2_w_down, ln3_g, ln3_b, ple_w_proj, ple_w_gate, ple_b_gate, ln4_g, ln4_b, loss_target, m_ffn1_w_up, m_ffn1_w_down, m_ln1_g, m_ln1_b, m_w_in, m_conv_w, m_conv_b, m_lru_wa, m_lru_ba, m_lru_wx, m_lru_bx, m_lru_lambda, m_gla_w_g2, m_gla_b_g, m_gla_norm_g, m_fox_b_f, m_w_branch, m_w_out, m_ln2_g, m_ln2_b, m_ffn2_w_up, m_ffn2_w_down, m_ln3_g, m_ln3_b, m_ple_w_proj, m_ple_w_gate, m_ple_b_gate, m_ln4_g, m_ln4_b, v_ffn1_w_up, v_ffn1_w_down, v_ln1_g, v_ln1_b, v_w_in, v_conv_w, v_conv_b, v_lru_wa, v_lru_ba, v_lru_wx, v_lru_bx, v_lru_lambda, v_gla_w_g2, v_gla_b_g, v_gla_norm_g, v_fox_b_f, v_w_branch, v_w_out, v_ln2_g, v_ln2_b, v_ffn2_w_up, v_ffn2_w_down, v_ln3_g, v_ln3_b, v_ple_w_proj, v_ple_w_gate, v_ple_b_gate, v_ln4_g, v_ln4_b):
    args = dict(locals())
    wts = {n: args[n] for n in WEIGHTS}
    mom = {n: args["m_" + n] for n in WEIGHTS}
    var = {n: args["v_" + n] for n in WEIGHTS}
    cx, cy, cc = lax.axis_index("x"), lax.axis_index("y"), lax.axis_index("c")

    shards = [wts[n].reshape((DEPTH,) + rc).astype(F32 if n == "conv_w" else BF16) for n, rc in SHARDED]
    gw = dict(zip([n for n, _ in SHARDED], gather_weights(shards)))
    small = {n: wts[n] for n in SMALL}

    loss_local, dx, grads = _local_step(x[0], p[:, 0], loss_target[0], gw, small)
    loss = lax.psum(loss_local, ("x", "y", "c"))
    grad_x = dx[None]

    core = jnp.reshape(cc, (1,)).astype(jnp.int32)
    chip = jnp.reshape(2 * cx + cy, (1,)).astype(jnp.int32)
    g0 = [grads[0][n] for n, _ in SHARDED]
    g1 = [grads[1][n] for n, _ in SHARDED]
    rb = pair_exchange(g0, g1)
    s1 = [pair_add(a0, a1, r, core, name="pair_add_" + n) for (n, _), a0, a1, r in zip(SHARDED, g0, g1, rb)]
    rb2 = chip_exchange([sb for _, sb in s1])
    s2 = [chip_add(sf, r, chip, name="chip_add_" + n) for (n, _), (sf, _), r in zip(SHARDED, s1, rb2)]
    gout = {n: gsh.reshape(wts[n].shape) for (n, _), gsh in zip(SHARDED, pair_share(s2))}

    small_sizes = [wts[n].size for n in SMALL]
    srows = -(-sum(small_sizes) // (8 * PACK_W)) * 8
    gs = _pack([jnp.stack([grads[l][n] for l in range(DEPTH)]) for n in SMALL], srows)
    me = jnp.reshape(4 * cx + 2 * cy + cc, (1,)).astype(jnp.int32)
    gsum = small_add(gs, small_exchange(gs), me)

    delta, new_m, new_v = {}, {}, {}
    for n, (_, cols) in SHARDED:
        shp = wts[n].shape
        v2 = lambda a: a.reshape(-1, cols)
        d, mn, vn = adamw(v2(wts[n]), v2(gout[n]), v2(mom[n]), v2(var[n]), name="adamw_" + n)
        delta[n], new_m[n], new_v[n] = d.reshape(shp), mn.reshape(shp), vn.reshape(shp)
    d, mn, vn = adamw(_pack([wts[n] for n in SMALL], srows), gsum, _pack([mom[n] for n in SMALL], srows),
                      _pack([var[n] for n in SMALL], srows), name="adamw_small")
    off = 0
    for n, size in zip(SMALL, small_sizes):
        shp = wts[n].shape
        take = lambda a: a.reshape(-1)[off:off + size].reshape(shp)
        gout[n], delta[n], new_m[n], new_v[n] = take(gsum), take(d), take(mn), take(vn)
        off += size

    return (loss, grad_x, *[gout[n] for n in WEIGHTS], *[delta[n] for n in WEIGHTS],
            *[new_m[n] for n in WEIGHTS], *[new_v[n] for n in WEIGHTS])
```

```python
import functools
import math

import jax
import jax.numpy as jnp
from jax import lax
from jax.experimental import pallas as pl
from jax.experimental.pallas import tpu as pltpu

F32 = jnp.float32
BF16 = jnp.bfloat16

D = 1024
DFF = 2816
BW = 512
PLE = 256
DEPTH = 2
ALPHA = (2 * DEPTH) ** 0.25
LN_EPS = 1e-5
RMS_EPS = 1e-6
LRU_C = 8.0
GLA_TAU = 16.0
CHUNK = 64
D_IN = 7192
ZW = 7296
AX, AY, BQ, BK, BV, BR, CQ, CK, CV, G0, MISC = 0, 512, 1024, 1280, 1536, 2048, 2560, 3072, 3584, 4096, 7168
LOW_W, FOX_H = 16, 8
ADAM_LR, ADAM_B1, ADAM_B2, ADAM_EPS, ADAM_WD, ADAM_STEP = 0.001, 0.9, 0.999, 1e-08, 0.01, 10
PACK_W = 1024
VMEM_LIMIT = 56 << 20

MESH = pl.DeviceIdType.MESH
ANY = pl.BlockSpec(memory_space=pl.ANY)


def _pcall(body, **kw):
    return pl.pallas_call(body, **kw)


def _cp(*dims):
    return pltpu.CompilerParams(dimension_semantics=dims, vmem_limit_bytes=VMEM_LIMIT)


def _dot(a, b):
    return jnp.dot(a, b, preferred_element_type=F32)


def _dot_nt(a, b):
    return lax.dot_general(a, b, (((1,), (1,)), ((), ())), preferred_element_type=F32)


def _dot_tn(a, b):
    return lax.dot_general(a, b, (((0,), (0,)), ((), ())), preferred_element_type=F32)


def _dot_hi(a, b):
    return jnp.dot(a, b, preferred_element_type=F32, precision=lax.Precision.HIGHEST)


def _sigmoid(x):
    return 1.0 / (1.0 + jnp.exp(-x))


def _softplus(x):
    return jnp.maximum(x, 0.0) + jnp.log(1.0 + jnp.exp(-jnp.abs(x)))


def _log_sigmoid(x):
    return -_softplus(-x)


def _expm1(x):
    poly = x * (1.0 + x * (0.5 + x * (1.0 / 6.0 + x * (1.0 / 24.0 + x * (1.0 / 120.0 + x * (1.0 / 720.0))))))
    return jnp.where(jnp.abs(x) < 0.1, poly, jnp.exp(x) - 1.0)


_GELU_C = math.sqrt(2.0 / math.pi)


def _gelu(x):
    return 0.5 * x * (1.0 + jnp.tanh(_GELU_C * (x + 0.044715 * x * x * x)))


def _gelu_grad(x):
    t = jnp.tanh(_GELU_C * (x + 0.044715 * x * x * x))
    return 0.5 * (1.0 + t) + 0.5 * x * (1.0 - t * t) * _GELU_C * (1.0 + 3.0 * 0.044715 * x * x)


def _ln_stats(r):
    mu = jnp.mean(r, axis=-1, keepdims=True)
    xc = r - mu
    var = jnp.mean(xc * xc, axis=-1, keepdims=True)
    return xc, lax.rsqrt(var + LN_EPS)


def _pick(n, cands):
    for c in cands:
        if n % c == 0:
            return c
    return n


def _rows(tm, w, col=0):
    return pl.BlockSpec((tm, w), lambda i: (i, col))


def _fix(shape):
    nd = len(shape)
    return pl.BlockSpec(shape, lambda i: (0,) * nd)


def _layer(l, shape):
    nd = len(shape)
    return pl.BlockSpec((None,) + tuple(shape), lambda i: (l,) + (0,) * nd)


def matmul(a, b, *, name, nt=False, b_lead=(), res=None, res_scale=1.0, also_bf16=False, tm=512, tn=512,
           tk=None):
    m, k = a.shape
    n = b.shape[-2] if nt else b.shape[-1]
    tm, tn = min(tm, m), min(tn, n)
    tk = k if tk is None else tk
    nk = k // tk
    has_res = res is not None
    lead = tuple(b_lead)
    dot = _dot_nt if nt else _dot

    def body(*refs):
        a_ref, b_ref = refs[0], refs[1]
        pos = 2
        r_ref = None
        if has_res:
            r_ref = refs[pos]
            pos += 1
        o_ref = refs[pos]
        pos += 1
        ob_ref = None
        if also_bf16:
            ob_ref = refs[pos]
            pos += 1
        acc = refs[pos]
        kk = pl.program_id(2)

        @pl.when(kk == 0)
        def _():
            acc[...] = jnp.zeros_like(acc)

        acc[...] += dot(a_ref[...], b_ref[...])

        @pl.when(kk == nk - 1)
        def _():
            v = acc[...]
            if has_res:
                v = v + res_scale * r_ref[...]
            o_ref[...] = v
            if also_bf16:
                ob_ref[...] = v.astype(BF16)

    none = (None,) * len(lead)
    if nt:
        b_spec = pl.BlockSpec(none + (tn, tk), lambda j, i, kk: lead + (j, kk))
    else:
        b_spec = pl.BlockSpec(none + (tk, tn), lambda j, i, kk: lead + (kk, j))
    in_specs = [pl.BlockSpec((tm, tk), lambda j, i, kk: (i, kk)), b_spec]
    args = [a, b]
    if has_res:
        in_specs.append(pl.BlockSpec((tm, tn), lambda j, i, kk: (i, j)))
        args.append(res)
    out_shape = [jax.ShapeDtypeStruct((m, n), F32)]
    out_specs = [pl.BlockSpec((tm, tn), lambda j, i, kk: (i, j))]
    if also_bf16:
        out_shape.append(jax.ShapeDtypeStruct((m, n), BF16))
        out_specs.append(pl.BlockSpec((tm, tn), lambda j, i, kk: (i, j)))
    out = _pcall(body, name=name, grid=(n // tn, m // tm, nk), in_specs=in_specs, out_specs=out_specs,
                 out_shape=out_shape, scratch_shapes=[pltpu.VMEM((tm, tn), F32)],
                 compiler_params=_cp("parallel", "parallel", "arbitrary"))(*args)
    return out if also_bf16 else out[0]


def matmul_tn(a, b, *, name):
    t, k = a.shape
    n = b.shape[1]
    tt = min(2048, t)
    tk = _pick(k, (512, 1408, 256, 128))
    tn = _pick(n, (1024, 1408, 2432, 512, 256, 128))
    nt = t // tt

    def body(a_ref, b_ref, o_ref):
        @pl.when(pl.program_id(2) == 0)
        def _():
            o_ref[...] = jnp.zeros_like(o_ref)

        o_ref[...] += _dot_tn(a_ref[...], b_ref[...])

    return _pcall(body, name=name, grid=(k // tk, n // tn, nt),
                  in_specs=[pl.BlockSpec((tt, tk), lambda i, j, s: (s, i)),
                            pl.BlockSpec((tt, tn), lambda i, j, s: (s, j))],
                  out_specs=pl.BlockSpec((tk, tn), lambda i, j, s: (i, j)),
                  out_shape=jax.ShapeDtypeStruct((k, n), F32),
                  compiler_params=_cp("parallel", "parallel", "arbitrary"))(a, b)


UPW = 1408


def matmul_tn_up(a, dgate, dup, *, name):
    t, k = a.shape
    tt = min(2048, t)
    tk = 512

    def body(a_ref, g_ref, u_ref, o_ref):
        j = pl.program_id(1)

        @pl.when(pl.program_id(2) == 0)
        def _():
            o_ref[...] = jnp.zeros_like(o_ref)

        @pl.when(j < 2)
        def _():
            o_ref[...] += _dot_tn(a_ref[...], g_ref[...])

        @pl.when(j >= 2)
        def _():
            o_ref[...] += _dot_tn(a_ref[...], u_ref[...])

    return _pcall(body, name=name, grid=(k // tk, 4, t // tt),
                  in_specs=[pl.BlockSpec((tt, tk), lambda i, j, s: (s, i)),
                            pl.BlockSpec((tt, UPW), lambda i, j, s: (jnp.where(j < 2, s, 0), jnp.minimum(j, 1))),
                            pl.BlockSpec((tt, UPW), lambda i, j, s: (jnp.where(j >= 2, s, 0), jnp.maximum(j - 2, 0)))],
                  out_specs=pl.BlockSpec((None, tk, UPW), lambda i, j, s: (j, i, 0)),
                  out_shape=jax.ShapeDtypeStruct((4, k, UPW), F32),
                  compiler_params=_cp("parallel", "parallel", "arbitrary"))(a, dgate, dup)


def ffn_dx(dgate, dup, wup, l, res, *, name):
    t = dgate.shape[0]
    tm, tn = min(1024, t), 1024

    def body(g_ref, u_ref, w_ref, r_ref, o_ref, acc):
        kk = pl.program_id(2)

        @pl.when(kk == 0)
        def _():
            acc[...] = jnp.zeros_like(acc)

        @pl.when(kk < 2)
        def _():
            acc[...] += _dot_nt(g_ref[...], w_ref[...])

        @pl.when(kk >= 2)
        def _():
            acc[...] += _dot_nt(u_ref[...], w_ref[...])

        @pl.when(kk == 3)
        def _():
            o_ref[...] = acc[...] + ALPHA * r_ref[...]

    return _pcall(body, name=name, grid=(D // tn, t // tm, 4),
                  in_specs=[pl.BlockSpec((tm, UPW), lambda j, i, kk: (i, jnp.minimum(kk, 1))),
                            pl.BlockSpec((tm, UPW), lambda j, i, kk: (i, jnp.maximum(kk - 2, 0))),
                            pl.BlockSpec((None, None, tn, UPW), lambda j, i, kk: (l, kk, j, 0)),
                            pl.BlockSpec((tm, tn), lambda j, i, kk: (i, j))],
                  out_specs=pl.BlockSpec((tm, tn), lambda j, i, kk: (i, j)),
                  out_shape=jax.ShapeDtypeStruct((t, D), F32),
                  scratch_shapes=[pltpu.VMEM((tm, tn), F32)],
                  compiler_params=_cp("parallel", "parallel", "arbitrary"))(dgate, dup, wup, res)


def ffn_up(xb, wup, l, *, name):
    t = xb.shape[0]
    tm, tn = min(512, t), UPW

    def body(x_ref, wg_ref, wu_ref, g_ref, u_ref, a_ref):
        x = x_ref[...]
        g = _dot(x, wg_ref[...])
        u = _dot(x, wu_ref[...])
        g_ref[...] = g
        u_ref[...] = u
        a_ref[...] = (g * _sigmoid(g) * u).astype(BF16)

    blk = pl.BlockSpec((tm, tn), lambda j, i: (i, j))
    return _pcall(body, name=name, grid=(DFF // tn, t // tm),
                  in_specs=[pl.BlockSpec((tm, D), lambda j, i: (i, 0)),
                            pl.BlockSpec((None, None, D, tn), lambda j, i: (l, j, 0, 0)),
                            pl.BlockSpec((None, None, D, tn), lambda j, i: (l, 2 + j, 0, 0))],
                  out_specs=[blk, blk, blk],
                  out_shape=[jax.ShapeDtypeStruct((t, DFF), F32), jax.ShapeDtypeStruct((t, DFF), F32),
                             jax.ShapeDtypeStruct((t, DFF), BF16)],
                  compiler_params=_cp("parallel", "parallel"))(xb, wup, wup)


def matmul_res_ln(a, w, l, res, g, b, *, mm_scale, name):
    t, k = a.shape
    tm = min(256, t)

    def body(a_ref, w_ref, res_ref, g_ref, b_ref, r_ref, y_ref, yb_ref):
        f = _dot(a_ref[...], w_ref[...])
        r = ALPHA * res_ref[...] + mm_scale * f
        xc, rstd = _ln_stats(r)
        y = xc * rstd * g_ref[...] + b_ref[...]
        r_ref[...] = r
        y_ref[...] = y
        yb_ref[...] = y.astype(BF16)

    return _pcall(body, name=name, grid=(t // tm,),
                  in_specs=[_rows(tm, k), _layer(l, (k, D)), _rows(tm, D), _fix((1, D)), _fix((1, D))],
                  out_specs=[_rows(tm, D)] * 3,
                  out_shape=[jax.ShapeDtypeStruct((t, D), F32), jax.ShapeDtypeStruct((t, D), F32),
                             jax.ShapeDtypeStruct((t, D), BF16)],
                  compiler_params=_cp("parallel"))(a, w, res, g, b)


def ln_bwd(dy, r, g, *, out_scale, name):
    t = dy.shape[0]
    tm = min(256, t)

    def body(dy_ref, r_ref, g_ref, dr_ref, drb_ref, dg_ref, db_ref):
        @pl.when(pl.program_id(0) == 0)
        def _():
            dg_ref[...] = jnp.zeros_like(dg_ref)
            db_ref[...] = jnp.zeros_like(db_ref)

        xc, rstd = _ln_stats(r_ref[...])
        xhat = xc * rstd
        d = dy_ref[...]
        dxh = d * g_ref[...]
        dr = rstd * (dxh - jnp.mean(dxh, axis=-1, keepdims=True)
                     - xhat * jnp.mean(dxh * xhat, axis=-1, keepdims=True))
        dr_ref[...] = dr
        drb_ref[...] = (out_scale * dr).astype(BF16)
        dg_ref[...] += jnp.sum(d * xhat, axis=0, keepdims=True)
        db_ref[...] += jnp.sum(d, axis=0, keepdims=True)

    return _pcall(body, name=name, grid=(t // tm,),
                  in_specs=[_rows(tm, D), _rows(tm, D), _fix((1, D))],
                  out_specs=[_rows(tm, D), _rows(tm, D), _fix((1, D)), _fix((1, D))],
                  out_shape=[jax.ShapeDtypeStruct((t, D), F32), jax.ShapeDtypeStruct((t, D), BF16),
                             jax.ShapeDtypeStruct((1, D), F32), jax.ShapeDtypeStruct((1, D), F32)],
                  compiler_params=_cp("arbitrary"))(dy, r, g)


def ffn_down_bwd(dfb, wd, l, gate, up, *, name):
    t = dfb.shape[0]
    tm, tn = min(512, t), UPW
    nj = DFF // tn

    def body(df_ref, w_ref, g_ref, u_ref, dg_ref, du_ref):
        da = _dot_nt(df_ref[...], w_ref[...])
        g = g_ref[...]
        s = _sigmoid(g)
        dg_ref[...] = (da * u_ref[...] * s * (1.0 + g * (1.0 - s))).astype(BF16)
        du_ref[...] = (da * g * s).astype(BF16)

    blk = pl.BlockSpec((tm, tn), lambda j, i: (i, j))
    return _pcall(body, name=name, grid=(nj, t // tm),
                  in_specs=[pl.BlockSpec((tm, D), lambda j, i: (i, 0)),
                            pl.BlockSpec((None, tn, D), lambda j, i: (l, j, 0)), blk, blk],
                  out_specs=[blk, blk],
                  out_shape=[jax.ShapeDtypeStruct((t, DFF), BF16), jax.ShapeDtypeStruct((t, DFF), BF16)],
                  compiler_params=_cp("parallel", "parallel"))(dfb, wd, gate, up)


def ple_fwd(xb, x, pb, wgate, l, wproj, bgate, g, b, *, name):
    t = x.shape[0]
    tm = min(256, t)

    def body(xb_ref, x_ref, p_ref, wg_ref, wp_ref, bg_ref, g_ref, b_ref, r_ref, y_ref, yb_ref):
        gl = _dot(xb_ref[...], wg_ref[...]) + bg_ref[...]
        pe = _dot(p_ref[...], wp_ref[...])
        r = ALPHA * x_ref[...] + _sigmoid(gl) * pe
        xc, rstd = _ln_stats(r)
        y = xc * rstd * g_ref[...] + b_ref[...]
        r_ref[...] = r
        y_ref[...] = y
        yb_ref[...] = y.astype(BF16)

    return _pcall(body, name=name, grid=(t // tm,),
                  in_specs=[_rows(tm, D), _rows(tm, D), _rows(tm, PLE), _layer(l, (D, D)), _fix((PLE, D)),
                            _fix((1, D)), _fix((1, D)), _fix((1, D))],
                  out_specs=[_rows(tm, D)] * 3,
                  out_shape=[jax.ShapeDtypeStruct((t, D), F32), jax.ShapeDtypeStruct((t, D), F32),
                             jax.ShapeDtypeStruct((t, D), BF16)],
                  compiler_params=_cp("parallel"))(xb, x, pb, wgate, wproj, bgate, g, b)


def ple_bwd(dy, r, xb, pb, wgate, l, wproj, bgate, g, *, name):
    t = dy.shape[0]
    tm = min(256, t)

    def body(dy_ref, r_ref, xb_ref, p_ref, wg_ref, wp_ref, bg_ref, g_ref,
             dr_ref, dgl_ref, dpe_ref, dg_ref, db_ref, dbg_ref):
        @pl.when(pl.program_id(0) == 0)
        def _():
            dg_ref[...] = jnp.zeros_like(dg_ref)
            db_ref[...] = jnp.zeros_like(db_ref)
            dbg_ref[...] = jnp.zeros_like(dbg_ref)

        xc, rstd = _ln_stats(r_ref[...])
        xhat = xc * rstd
        d = dy_ref[...]
        dxh = d * g_ref[...]
        dr = rstd * (dxh - jnp.mean(dxh, axis=-1, keepdims=True)
                     - xhat * jnp.mean(dxh * xhat, axis=-1, keepdims=True))
        s = _sigmoid(_dot(xb_ref[...], wg_ref[...]) + bg_ref[...])
        pe = _dot(p_ref[...], wp_ref[...])
        dgl = dr * pe * s * (1.0 - s)
        dr_ref[...] = dr
        dgl_ref[...] = dgl.astype(BF16)
        dpe_ref[...] = (dr * s).astype(BF16)
        dg_ref[...] += jnp.sum(d * xhat, axis=0, keepdims=True)
        db_ref[...] += jnp.sum(d, axis=0, keepdims=True)
        dbg_ref[...] += jnp.sum(dgl, axis=0, keepdims=True)

    vec = jax.ShapeDtypeStruct((1, D), F32)
    return _pcall(body, name=name, grid=(t // tm,),
                  in_specs=[_rows(tm, D), _rows(tm, D), _rows(tm, D), _rows(tm, PLE), _layer(l, (D, D)),
                            _fix((PLE, D)), _fix((1, D)), _fix((1, D))],
                  out_specs=[_rows(tm, D), _rows(tm, D), _rows(tm, D), _fix((1, D)), _fix((1, D)), _fix((1, D))],
                  out_shape=[jax.ShapeDtypeStruct((t, D), F32), jax.ShapeDtypeStruct((t, D), BF16),
                             jax.ShapeDtypeStruct((t, D), BF16), vec, vec, vec],
                  compiler_params=_cp("arbitrary"))(dy, r, xb, pb, wgate, wproj, bgate, g)


def loss_head(y, tgt, *, name):
    t = y.shape[0]
    tm = min(256, t)

    def body(y_ref, t_ref, dy_ref, sq_ref):
        @pl.when(pl.program_id(0) == 0)
        def _():
            sq_ref[...] = jnp.zeros_like(sq_ref)

        e = y_ref[...] - t_ref[...]
        dy_ref[...] = e / float(D)
        sq_ref[...] += jnp.sum(e * e, axis=0, keepdims=True)

    return _pcall(body, name=name, grid=(t // tm,),
                  in_specs=[_rows(tm, D), _rows(tm, D)],
                  out_specs=[_rows(tm, D), _fix((1, D))],
                  out_shape=[jax.ShapeDtypeStruct((t, D), F32), jax.ShapeDtypeStruct((1, D), F32)],
                  compiler_params=_cp("arbitrary"))(y, tgt)


def _lru_gates(xc, wa_ref, wx_ref, ba_ref, bx_ref, lam_ref):
    xcb = xc.astype(BF16)
    r = _sigmoid(_dot(xcb, wa_ref[...]) + ba_ref[...])
    ig = _sigmoid(_dot(xcb, wx_ref[...]) + bx_ref[...])
    sp = _softplus(-lam_ref[...])
    la = -LRU_C * r * sp
    a = jnp.exp(la)
    mult = jnp.sqrt(-_expm1(2.0 * la))
    return r, ig, sp, la, a, mult


def lru_fwd(z, cw, cb, wa, wx, ba, bx, lam, *, name):
    t = z.shape[0]
    tm = min(256, t)
    hb = tm // 8

    def body(ax_ref, prev_ref, ay_ref, cw_ref, cb_ref, wa_ref, wx_ref, ba_ref, bx_ref, lam_ref,
             xc_ref, xcb_ref, h_ref, ya_ref, xs, a_s, b_s, hc):
        i = pl.program_id(0)

        @pl.when(i == 0)
        def _():
            hc[...] = jnp.zeros_like(hc)

        xs[0:8, :] = jnp.where(i == 0, 0.0, prev_ref[...])
        xs[8:, :] = ax_ref[...]
        xc = cb_ref[...] + cw_ref[0:1, :] * xs[5:5 + tm, :]
        for k in range(1, 4):
            xc = xc + cw_ref[k:k + 1, :] * xs[5 + k:5 + k + tm, :]
        r, ig, sp, la, a, mult = _lru_gates(xc, wa_ref, wx_ref, ba_ref, bx_ref, lam_ref)
        a_s[...] = a
        b_s[...] = mult * (ig * xc)
        xc_ref[...] = xc
        xcb_ref[...] = xc.astype(BF16)

        def step(g, h):
            base = pl.multiple_of(g * 8, 8)
            a8 = a_s[pl.ds(base, 8), :]
            b8 = b_s[pl.ds(base, 8), :]
            for j in range(8):
                h = a8[j:j + 1, :] * h + b8[j:j + 1, :]
                h_ref[pl.ds(base + j, 1), :] = h
            return h

        hc[...] = lax.fori_loop(0, tm // 8, step, hc[...])
        ya_ref[...] = (_gelu(ay_ref[...]) * h_ref[...]).astype(BF16)

    vec = _fix((1, BW))
    return _pcall(body, name=name, grid=(t // tm,),
                  in_specs=[_rows(tm, BW, AX // BW),
                            pl.BlockSpec((8, BW), lambda i: (jnp.maximum(i * hb - 1, 0), AX // BW)),
                            _rows(tm, BW, AY // BW), _fix((4, BW)), vec, _fix((BW, BW)), _fix((BW, BW)),
                            vec, vec, vec],
                  out_specs=[_rows(tm, BW)] * 4,
                  out_shape=[jax.ShapeDtypeStruct((t, BW), F32), jax.ShapeDtypeStruct((t, BW), BF16),
                             jax.ShapeDtypeStruct((t, BW), F32), jax.ShapeDtypeStruct((t, BW), BF16)],
                  scratch_shapes=[pltpu.VMEM((tm + 8, BW), F32), pltpu.VMEM((tm, BW), F32),
                                  pltpu.VMEM((tm, BW), F32), pltpu.VMEM((1, BW), F32)],
                  compiler_params=_cp("arbitrary"))(z, z, z, cw, cb, wa, wx, ba, bx, lam)


def lru_bwd(dya, z, h, xc, wa, wx, ba, bx, lam, *, name):
    t = dya.shape[0]
    tm = min(256, t)
    nb = t // tm
    hb = tm // 8

    def body(dya_ref, ay_ref, h_ref, hprev_ref, xc_ref, wa_ref, wx_ref, ba_ref, bx_ref,
             lam_ref, day_ref, dxc_ref, dpr_ref, dpi_ref, dba_ref, dbx_ref, dlam_ref,
             hs, a_s, g_s, d_s, cc):
        i = pl.program_id(0)

        @pl.when(i == 0)
        def _():
            cc[...] = jnp.zeros_like(cc)
            dba_ref[...] = jnp.zeros_like(dba_ref)
            dbx_ref[...] = jnp.zeros_like(dbx_ref)
            dlam_ref[...] = jnp.zeros_like(dlam_ref)

        xc = xc_ref[...]
        r, ig, sp, la, a, mult = _lru_gates(xc, wa_ref, wx_ref, ba_ref, bx_ref, lam_ref)
        ay = ay_ref[...]
        dya = dya_ref[...]
        hcur = h_ref[...]
        day_ref[...] = (dya * hcur * _gelu_grad(ay)).astype(BF16)
        a_s[...] = a
        g_s[...] = dya * _gelu(ay)

        def step(gg, cin):
            g = tm // 8 - 1 - gg
            base = pl.multiple_of(g * 8, 8)
            a8 = a_s[pl.ds(base, 8), :]
            g8 = g_s[pl.ds(base, 8), :]
            for j in range(7, -1, -1):
                d = g8[j:j + 1, :] + cin
                d_s[pl.ds(base + j, 1), :] = d
                cin = a8[j:j + 1, :] * d
            return cin

        cc[...] = lax.fori_loop(0, tm // 8, step, cc[...])
        dht = d_s[...]
        hs[0:8, :] = jnp.where(i == nb - 1, 0.0, hprev_ref[...])
        hs[8:, :] = hcur
        da = dht * hs[7:7 + tm, :]
        dmult = dht * ig * xc
        dig = dht * mult * xc
        dla = da * a - dmult * a * a / mult
        dpr = dla * (-LRU_C * sp) * r * (1.0 - r)
        dpi = dig * ig * (1.0 - ig)
        dprb = dpr.astype(BF16)
        dpib = dpi.astype(BF16)
        dxc_ref[...] = dht * mult * ig + _dot_nt(dprb, wa_ref[...]) + _dot_nt(dpib, wx_ref[...])
        dpr_ref[...] = dprb
        dpi_ref[...] = dpib
        dba_ref[...] += jnp.sum(dpr, axis=0, keepdims=True)
        dbx_ref[...] += jnp.sum(dpi, axis=0, keepdims=True)
        dlam_ref[...] += jnp.sum(dla * (-LRU_C * r), axis=0, keepdims=True) * (-_sigmoid(-lam_ref[...]))

    vec = _fix((1, BW))
    mat = _fix((BW, BW))
    rev = lambda col: pl.BlockSpec((tm, BW), lambda i: (nb - 1 - i, col))
    vshape = jax.ShapeDtypeStruct((1, BW), F32)
    return _pcall(body, name=name, grid=(nb,),
                  in_specs=[rev(0), rev(AY // BW), rev(0),
                            pl.BlockSpec((8, BW), lambda i: (jnp.maximum((nb - 1 - i) * hb - 1, 0), 0)),
                            rev(0), mat, mat, vec, vec, vec],
                  out_specs=[rev(0), rev(0), rev(0), rev(0), vec, vec, vec],
                  out_shape=[jax.ShapeDtypeStruct((t, BW), BF16), jax.ShapeDtypeStruct((t, BW), F32),
                             jax.ShapeDtypeStruct((t, BW), BF16), jax.ShapeDtypeStruct((t, BW), BF16),
                             vshape, vshape, vshape],
                  scratch_shapes=[pltpu.VMEM((tm + 8, BW), F32), pltpu.VMEM((tm, BW), F32),
                                  pltpu.VMEM((tm, BW), F32), pltpu.VMEM((tm, BW), F32),
                                  pltpu.VMEM((1, BW), F32)],
                  compiler_params=_cp("arbitrary"))(dya, z, h, h, xc, wa, wx, ba, bx, lam)


def conv_bwd(dxc, z, cw, *, name):
    t = dxc.shape[0]
    tm = min(256, t)
    nb = t // tm
    hb = tm // 8

    def body(d_ref, dnext_ref, ax_ref, prev_ref, cw_ref, dax_ref, dcw_ref, dcb_ref, ds, xs):
        i = pl.program_id(0)

        @pl.when(i == 0)
        def _():
            dcw_ref[...] = jnp.zeros_like(dcw_ref)
            dcb_ref[...] = jnp.zeros_like(dcb_ref)

        d = d_ref[...]
        ds[0:tm, :] = d
        ds[tm:, :] = jnp.where(i == nb - 1, 0.0, dnext_ref[...])
        xs[0:8, :] = jnp.where(i == 0, 0.0, prev_ref[...])
        xs[8:, :] = ax_ref[...]
        dax = cw_ref[3:4, :] * d
        for k in range(3):
            dax = dax + cw_ref[k:k + 1, :] * ds[3 - k:3 - k + tm, :]
        dax_ref[...] = dax.astype(BF16)
        for k in range(4):
            dcw_ref[k:k + 1, :] += jnp.sum(d * xs[5 + k:5 + k + tm, :], axis=0, keepdims=True)
        dcb_ref[...] += jnp.sum(d, axis=0, keepdims=True)

    return _pcall(body, name=name, grid=(nb,),
                  in_specs=[_rows(tm, BW),
                            pl.BlockSpec((8, BW), lambda i: (jnp.minimum((i + 1) * hb, nb * hb - 1), 0)),
                            _rows(tm, BW, AX // BW),
                            pl.BlockSpec((8, BW), lambda i: (jnp.maximum(i * hb - 1, 0), AX // BW)),
                            _fix((4, BW))],
                  out_specs=[_rows(tm, BW), _fix((4, BW)), _fix((1, BW))],
                  out_shape=[jax.ShapeDtypeStruct((t, BW), BF16), jax.ShapeDtypeStruct((4, BW), F32),
                             jax.ShapeDtypeStruct((1, BW), F32)],
                  scratch_shapes=[pltpu.VMEM((tm + 8, BW), F32), pltpu.VMEM((tm + 8, BW), F32)],
                  compiler_params=_cp("arbitrary"))(dxc, dxc, z, z, cw)


GLA_CB = 4


def _gla_consts():
    tri = (jnp.arange(CHUNK)[:, None] >= jnp.arange(CHUNK)[None, :]).astype(F32)
    mask = ((jnp.arange(BW)[:, None] // 128) == (jnp.arange(256)[None, :] // 64)).astype(F32)
    return tri, mask


def gla_fwd(z, zb, wg2p, bg, ng, *, name):
    t = z.shape[0]
    tm = GLA_CB * CHUNK
    nc = t // CHUNK
    tri, mask = _gla_consts()

    def body(q_ref, k_ref, v_ref, misc_ref, br_ref, w_ref, bg_ref, ng_ref, tri_ref, mask_ref,
             yb_ref, st_ref, st):
        @pl.when(pl.program_id(0) == 0)
        def _():
            st[...] = jnp.zeros_like(st)

        for c in range(GLA_CB):
            rows = slice(c * CHUNK, (c + 1) * CHUNK)
            pre = _dot(misc_ref[rows, :], w_ref[...]) + bg_ref[...]
            la = _log_sigmoid(pre) / GLA_TAU
            gc = _dot_hi(tri_ref[...], la)
            gt = gc[CHUNK - 1:CHUNK, :]
            kdec = k_ref[rows, :] * jnp.exp(gt - gc)
            delta = _dot_tn(v_ref[rows, :], kdec.astype(BF16))
            s_new = st[...] * jnp.exp(gt) + delta * mask_ref[...]
            st[...] = s_new
            st_ref[c] = s_new
            o = _dot_nt(q_ref[rows, :], s_new.astype(BF16)) * (64.0 ** -0.5)
            br = br_ref[rows, :]
            for hd in range(4):
                cols = slice(hd * 128, (hd + 1) * 128)
                oh = o[:, cols]
                rs = lax.rsqrt(jnp.mean(oh * oh, axis=-1, keepdims=True) + RMS_EPS)
                brh = br[:, cols]
                yb_ref[rows, cols] = (oh * rs * ng_ref[:, cols] * (brh * _sigmoid(brh))).astype(BF16)

    return _pcall(body, name=name, grid=(t // tm,),
                  in_specs=[_rows(tm, 256, BQ // 256), _rows(tm, 256, BK // 256), _rows(tm, BW, BV // BW),
                            _rows(tm, 128, MISC // 128), _rows(tm, BW, BR // BW), _fix((128, 256)),
                            _fix((1, 256)), _fix((1, BW)), _fix((CHUNK, CHUNK)), _fix((BW, 256))],
                  out_specs=[_rows(tm, BW), pl.BlockSpec((GLA_CB, BW, 256), lambda i: (i, 0, 0))],
                  out_shape=[jax.ShapeDtypeStruct((t, BW), BF16), jax.ShapeDtypeStruct((nc, BW, 256), F32)],
                  scratch_shapes=[pltpu.VMEM((BW, 256), F32)],
                  compiler_params=_cp("arbitrary"))(zb, z, zb, zb, z, wg2p, bg, ng, tri, mask)


def gla_bwd(dyb, z, zb, states, wg2p, bg, ng, *, name):
    t = z.shape[0]
    tm = GLA_CB * CHUNK
    nb = t // tm
    tri, mask = _gla_consts()
    triu = tri.T

    def body(dy_ref, q_ref, k_ref, v_ref, misc_ref, br_ref, st_ref, sp_ref, w_ref, bg_ref, ng_ref,
             tri_ref, triu_ref, mask_ref,
             dq_ref, dk_ref, dv_ref, dbr_ref, dmisc_ref, dpre_ref, dbg_ref, dng_ref, cc):
        i = pl.program_id(0)

        @pl.when(i == 0)
        def _():
            cc[...] = jnp.zeros_like(cc)
            dbg_ref[...] = jnp.zeros_like(dbg_ref)
            dng_ref[...] = jnp.zeros_like(dng_ref)

        last_row = lax.broadcasted_iota(jnp.int32, (CHUNK, 256), 0) == CHUNK - 1
        for c in range(GLA_CB - 1, -1, -1):
            rows = slice(c * CHUNK, (c + 1) * CHUNK)
            pre = _dot(misc_ref[rows, :], w_ref[...]) + bg_ref[...]
            la = _log_sigmoid(pre) / GLA_TAU
            gc = _dot_hi(tri_ref[...], la)
            gt = gc[CHUNK - 1:CHUNK, :]
            eg = jnp.exp(gt - gc)
            kdec = k_ref[rows, :] * eg
            e = jnp.exp(gt)
            s_n = st_ref[c]
            if c > 0:
                s_prev = st_ref[c - 1]
            else:
                s_prev = jnp.where(i == nb - 1, 0.0, sp_ref[0])
            sb = s_n.astype(BF16)
            qb = q_ref[rows, :]
            o = _dot_nt(qb, sb) * (64.0 ** -0.5)
            br = br_ref[rows, :]
            dy = dy_ref[rows, :]
            do_parts = []
            for hd in range(4):
                cols = slice(hd * 128, (hd + 1) * 128)
                oh = o[:, cols]
                rs = lax.rsqrt(jnp.mean(oh * oh, axis=-1, keepdims=True) + RMS_EPS)
                ohat = oh * rs
                brh = br[:, cols]
                sg = _sigmoid(brh)
                dyh = dy[:, cols]
                ngh = ng_ref[:, cols]
                don = dyh * (brh * sg)
                dbr_ref[rows, cols] = (dyh * (ohat * ngh) * sg * (1.0 + brh * (1.0 - sg))).astype(BF16)
                dng_ref[:, cols] += jnp.sum(don * ohat, axis=0, keepdims=True)
                doh = don * ngh
                do_parts.append(rs * (doh - ohat * jnp.mean(doh * ohat, axis=-1, keepdims=True)))
            dob = jnp.concatenate(do_parts, axis=1).astype(BF16)
            dq_ref[rows, :] = (_dot(dob, sb) * (64.0 ** -0.5)).astype(BF16)
            dst = cc[...] + _dot_tn(dob, qb) * (64.0 ** -0.5) * mask_ref[...]
            dsb = dst.astype(BF16)
            dkdec = _dot(v_ref[rows, :], dsb)
            dv_ref[rows, :] = _dot_nt(kdec.astype(BF16), dsb).astype(BF16)
            dgt = jnp.sum(dst * s_prev, axis=0, keepdims=True) * e
            dk_ref[rows, :] = (dkdec * eg).astype(BF16)
            dd = dkdec * kdec
            dgt = dgt + jnp.sum(dd, axis=0, keepdims=True)
            dgc = jnp.where(last_row, dgt - dd, -dd)
            dla = _dot_hi(triu_ref[...], dgc)
            dpre = dla * (1.0 / GLA_TAU) * _sigmoid(-pre)
            dpb = dpre.astype(BF16)
            dpre_ref[rows, :] = dpb
            dmisc_ref[rows, :] = _dot_nt(dpb, w_ref[...])
            dbg_ref[...] += jnp.sum(dpre, axis=0, keepdims=True)
            cc[...] = dst * e

    rev = lambda w, col: pl.BlockSpec((tm, w), lambda i: (nb - 1 - i, col))
    return _pcall(body, name=name, grid=(nb,),
                  in_specs=[rev(BW, 0), rev(256, BQ // 256), rev(256, BK // 256), rev(BW, BV // BW),
                            rev(128, MISC // 128), rev(BW, BR // BW),
                            pl.BlockSpec((GLA_CB, BW, 256), lambda i: (nb - 1 - i, 0, 0)),
                            pl.BlockSpec((1, BW, 256), lambda i: (jnp.maximum((nb - 1 - i) * GLA_CB - 1, 0), 0, 0)),
                            _fix((128, 256)), _fix((1, 256)), _fix((1, BW)),
                            _fix((CHUNK, CHUNK)), _fix((CHUNK, CHUNK)), _fix((BW, 256))],
                  out_specs=[rev(256, 0), rev(256, 0), rev(BW, 0), rev(BW, 0), rev(128, 0), rev(256, 0),
                             _fix((1, 256)), _fix((1, BW))],
                  out_shape=[jax.ShapeDtypeStruct((t, 256), BF16), jax.ShapeDtypeStruct((t, 256), BF16),
                             jax.ShapeDtypeStruct((t, BW), BF16), jax.ShapeDtypeStruct((t, BW), BF16),
                             jax.ShapeDtypeStruct((t, 128), F32), jax.ShapeDtypeStruct((t, 256), BF16),
                             jax.ShapeDtypeStruct((1, 256), F32), jax.ShapeDtypeStruct((1, BW), F32)],
                  scratch_shapes=[pltpu.VMEM((BW, 256), F32)],
                  compiler_params=_cp("arbitrary"))(dyb, zb, z, zb, zb, z, states, states, wg2p, bg, ng,
                                                    tri, triu, mask)


FOX_SCALE = 64.0 ** -0.5
NEG = -1e30


def fox_fcum(z, bfp, *, name):
    t = z.shape[0]
    tm = min(256, t)
    tri = (jnp.arange(tm)[:, None] >= jnp.arange(tm)[None, :]).astype(F32)

    def body(m_ref, b_ref, tri_ref, o_ref, cc):
        @pl.when(pl.program_id(0) == 0)
        def _():
            cc[...] = jnp.zeros_like(cc)

        lf = _log_sigmoid(m_ref[...] + b_ref[...])
        cs = _dot_hi(tri_ref[...], lf) + cc[...]
        o_ref[...] = cs
        cc[...] = cs[tm - 1:tm, :]

    return _pcall(body, name=name, grid=(t // tm,),
                  in_specs=[_rows(tm, 128, MISC // 128), _fix((1, 128)), _fix((tm, tm))],
                  out_specs=_rows(tm, 128), out_shape=jax.ShapeDtypeStruct((t, 128), F32),
                  scratch_shapes=[pltpu.VMEM((1, 128), F32)],
                  compiler_params=_cp("arbitrary"))(z, bfp, tri)


def fox_dcf(dfc, z, bfp, dmisc_g, *, name):
    t = z.shape[0]
    tm = min(256, t)
    nb = t // tm
    triu = (jnp.arange(tm)[:, None] <= jnp.arange(tm)[None, :]).astype(F32)

    def body(d_ref, m_ref, b_ref, g_ref, tri_ref, o_ref, dbf_ref, cc):
        @pl.when(pl.program_id(0) == 0)
        def _():
            cc[...] = jnp.zeros_like(cc)
            dbf_ref[...] = jnp.zeros_like(dbf_ref)

        rc = _dot_hi(tri_ref[...], d_ref[...]) + cc[...]
        cc[...] = rc[0:1, :]
        dcf = rc * _sigmoid(-(m_ref[...] + b_ref[...]))
        o_ref[...] = (dcf + g_ref[...]).astype(BF16)
        dbf_ref[...] += jnp.sum(dcf, axis=0, keepdims=True)

    rev = lambda col: pl.BlockSpec((tm, 128), lambda i: (nb - 1 - i, col))
    return _pcall(body, name=name, grid=(nb,),
                  in_specs=[rev(0), rev(MISC // 128), _fix((1, 128)), rev(0), _fix((tm, tm))],
                  out_specs=[rev(0), _fix((1, 128))],
                  out_shape=[jax.ShapeDtypeStruct((t, 128), BF16), jax.ShapeDtypeStruct((1, 128), F32)],
                  scratch_shapes=[pltpu.VMEM((1, 128), F32)],
                  compiler_params=_cp("arbitrary"))(dfc, z, bfp, dmisc_g, triu)


def fox_fwd(zb, fq, fk, *, name):
    t = zb.shape[0]
    tq = min(512, t)
    nq = t // tq

    def body(q_ref, k_ref, v_ref, fq_ref, fk_ref, y_ref, yf_ref, lse_ref, m_s, l_s, acc):
        i, j = pl.program_id(1), pl.program_id(2)

        @pl.when(j == 0)
        def _():
            m_s[...] = jnp.full_like(m_s, NEG)
            l_s[...] = jnp.zeros_like(l_s)
            acc[...] = jnp.zeros_like(acc)

        lo = lax.broadcasted_iota(jnp.int32, (tq, 128), 1) < 64

        def work(diagonal):
            q = q_ref[...]
            k = k_ref[...]
            v = v_ref[...]
            if diagonal:
                row = lax.broadcasted_iota(jnp.int32, (tq, tq), 0)
                col = lax.broadcasted_iota(jnp.int32, (tq, tq), 1)
                keep = col <= row
            upd = []
            for hh in range(2):
                sel = lo if hh == 0 else jnp.logical_not(lo)
                qh = jnp.where(sel, q, jnp.zeros_like(q))
                s = _dot_nt(qh, k) * FOX_SCALE + fq_ref[hh] - fk_ref[hh]
                if diagonal:
                    s = jnp.where(keep, s, NEG)
                m_old = m_s[hh]
                m_new = jnp.maximum(m_old, jnp.max(s, axis=-1, keepdims=True))
                p = jnp.exp(s - m_new)
                corr = jnp.exp(m_old - m_new)
                l_s[hh] = l_s[hh] * corr + jnp.sum(p, axis=-1, keepdims=True)
                m_s[hh] = m_new
                upd.append(acc[...] * corr + _dot(p.astype(BF16), v))
            acc[...] = jnp.where(lo, upd[0], upd[1])

        @pl.when(j < i)
        def _():
            work(False)

        @pl.when(j == i)
        def _():
            work(True)

        @pl.when(j == nq - 1)
        def _():
            out = acc[...] * jnp.where(lo, 1.0 / l_s[0], 1.0 / l_s[1])
            y_ref[...] = out.astype(BF16)
            yf_ref[...] = out
            lse_ref[...] = m_s[...] + jnp.log(l_s[...])

    kv = lambda off: pl.BlockSpec((tq, 128), lambda h, i, j: (jnp.minimum(j, i), off // 128 + h))
    return _pcall(body, name=name, grid=(4, nq, nq),
                  in_specs=[pl.BlockSpec((tq, 128), lambda h, i, j: (i, CQ // 128 + h)), kv(CK), kv(CV),
                            pl.BlockSpec((2, tq, 1), lambda h, i, j: (h, i, 0)),
                            pl.BlockSpec((2, 1, tq), lambda h, i, j: (h, 0, jnp.minimum(j, i)))],
                  out_specs=[pl.BlockSpec((tq, 128), lambda h, i, j: (i, h)),
                             pl.BlockSpec((tq, 128), lambda h, i, j: (i, h)),
                             pl.BlockSpec((2, tq, 1), lambda h, i, j: (h, i, 0))],
                  out_shape=[jax.ShapeDtypeStruct((t, BW), BF16), jax.ShapeDtypeStruct((t, BW), F32),
                             jax.ShapeDtypeStruct((FOX_H, t, 1), F32)],
                  scratch_shapes=[pltpu.VMEM((2, tq, 1), F32), pltpu.VMEM((2, tq, 1), F32),
                                  pltpu.VMEM((tq, 128), F32)],
                  compiler_params=_cp("parallel", "parallel", "arbitrary"))(zb, zb, zb, fq, fk)


def fox_delta(dyc, ycf, *, name):
    t = dyc.shape[0]
    tm = min(256, t)
    seg = ((jnp.arange(BW)[:, None] // 64) == jnp.arange(128)[None, :]).astype(F32)

    def body(d_ref, o_ref, s_ref, out_ref):
        out_ref[...] = _dot_hi(d_ref[...] * o_ref[...], s_ref[...])

    return _pcall(body, name=name, grid=(t // tm,),
                  in_specs=[_rows(tm, BW), _rows(tm, BW), _fix((BW, 128))],
                  out_specs=_rows(tm, 128), out_shape=jax.ShapeDtypeStruct((t, 128), F32),
                  compiler_params=_cp("parallel"))(dyc, ycf, seg)


def fox_bwd(zb, dyc, fq, fk, lse, dl, *, name):
    t = zb.shape[0]
    tq = min(512, t)
    nq = t // tq

    def body(q_ref, k_ref, v_ref, do_ref, fq_ref, fk_ref, lse_ref, dl_ref,
             dq_ref, dk_ref, dv_ref, dfk_ref, dfq_ref, dk_s, dv_s, df_s):
        j, i = pl.program_id(1), pl.program_id(2)

        @pl.when(jnp.logical_and(j == 0, i == 0))
        def _():
            dq_ref[...] = jnp.zeros_like(dq_ref)
            dfq_ref[...] = jnp.zeros_like(dfq_ref)

        @pl.when(i == 0)
        def _():
            dk_s[...] = jnp.zeros_like(dk_s)
            dv_s[...] = jnp.zeros_like(dv_s)
            df_s[...] = jnp.zeros_like(df_s)

        lo = lax.broadcasted_iota(jnp.int32, (tq, 128), 1) < 64

        def work(diagonal):
            q = q_ref[...]
            k = k_ref[...]
            v = v_ref[...]
            dob = do_ref[...].astype(BF16)
            if diagonal:
                row = lax.broadcasted_iota(jnp.int32, (tq, tq), 0)
                col = lax.broadcasted_iota(jnp.int32, (tq, tq), 1)
                keep = col <= row
            dvs, dks, dqs, rsum = [], [], [], []
            for hh in range(2):
                sel = lo if hh == 0 else jnp.logical_not(lo)
                qh = jnp.where(sel, q, jnp.zeros_like(q))
                doh = jnp.where(sel, dob, jnp.zeros_like(dob))
                s = _dot_nt(qh, k) * FOX_SCALE + fq_ref[hh] - fk_ref[hh]
                p = jnp.exp(s - lse_ref[hh])
                if diagonal:
                    p = jnp.where(keep, p, 0.0)
                dp = _dot_nt(doh, v)
                ds = p * (dp - dl_ref[hh])
                dsb = (ds * FOX_SCALE).astype(BF16)
                dvs.append(_dot_tn(p.astype(BF16), dob))
                dks.append(_dot_tn(dsb, q))
                dqs.append(_dot(dsb, k))
                df_s[hh] += jnp.sum(ds, axis=0, keepdims=True)
                rsum.append(jnp.sum(ds, axis=-1, keepdims=True))
            dv_s[...] += jnp.where(lo, dvs[0], dvs[1])
            dk_s[...] += jnp.where(lo, dks[0], dks[1])
            r0 = pl.multiple_of(i * tq, tq)
            dq_ref[pl.ds(r0, tq), :] += jnp.where(lo, dqs[0], dqs[1])
            dfq_ref[pl.ds(r0, tq), :] += jnp.where(lo, rsum[0], rsum[1])

        @pl.when(i > j)
        def _():
            work(False)

        @pl.when(i == j)
        def _():
            work(True)

        @pl.when(i == nq - 1)
        def _():
            dk_ref[...] = dk_s[...].astype(BF16)
            dv_ref[...] = dv_s[...].astype(BF16)
            dfk_ref[...] = -df_s[...]

    qi = lambda j, i: jnp.maximum(i, j)
    return _pcall(body, name=name, grid=(4, nq, nq),
                  in_specs=[pl.BlockSpec((tq, 128), lambda h, j, i: (qi(j, i), CQ // 128 + h)),
                            pl.BlockSpec((tq, 128), lambda h, j, i: (j, CK // 128 + h)),
                            pl.BlockSpec((tq, 128), lambda h, j, i: (j, CV // 128 + h)),
                            pl.BlockSpec((tq, 128), lambda h, j, i: (qi(j, i), h)),
                            pl.BlockSpec((2, tq, 1), lambda h, j, i: (h, qi(j, i), 0)),
                            pl.BlockSpec((2, 1, tq), lambda h, j, i: (h, 0, j)),
                            pl.BlockSpec((2, tq, 1), lambda h, j, i: (h, qi(j, i), 0)),
                            pl.BlockSpec((2, tq, 1), lambda h, j, i: (h, qi(j, i), 0))],
                  out_specs=[pl.BlockSpec((t, 128), lambda h, j, i: (0, h)),
                             pl.BlockSpec((tq, 128), lambda h, j, i: (j, h)),
                             pl.BlockSpec((tq, 128), lambda h, j, i: (j, h)),
                             pl.BlockSpec((2, 1, tq), lambda h, j, i: (h, 0, j)),
                             pl.BlockSpec((t, 128), lambda h, j, i: (0, h))],
                  out_shape=[jax.ShapeDtypeStruct((t, BW), F32), jax.ShapeDtypeStruct((t, BW), BF16),
                             jax.ShapeDtypeStruct((t, BW), BF16), jax.ShapeDtypeStruct((FOX_H, 1, t), F32),
                             jax.ShapeDtypeStruct((t, BW), F32)],
                  scratch_shapes=[pltpu.VMEM((tq, 128), F32), pltpu.VMEM((tq, 128), F32),
                                  pltpu.VMEM((2, 1, tq), F32)],
                  compiler_params=_cp("parallel", "arbitrary", "arbitrary"))(zb, zb, zb, dyc, fq, fk, lse, dl)


def merge_fwd(ya, yb, yc, wbr, z, *, name):
    t = ya.shape[0]
    tm = min(256, t)

    def body(ya_ref, yb_ref, yc_ref, w_ref, g0_ref, g1_ref, g2_ref, o_ref):
        m = _sigmoid(g0_ref[...]) * _dot(ya_ref[...], w_ref[0])
        m = m + _sigmoid(g1_ref[...]) * _dot(yb_ref[...], w_ref[1])
        m = m + _sigmoid(g2_ref[...]) * _dot(yc_ref[...], w_ref[2])
        o_ref[...] = m.astype(BF16)

    return _pcall(body, name=name, grid=(t // tm,),
                  in_specs=[_rows(tm, BW)] * 3 + [_fix((3, BW, D))]
                  + [_rows(tm, D, G0 // D + j) for j in range(3)],
                  out_specs=_rows(tm, D), out_shape=jax.ShapeDtypeStruct((t, D), BF16),
                  compiler_params=_cp("parallel"))(ya, yb, yc, wbr, z, z, z)


def merge_bwd(doutb, wo, l, ya, yb, yc, wbr, z, *, name):
    t = ya.shape[0]
    tm = min(256, t)

    def body(do_ref, wo_ref, ya_ref, yb_ref, yc_ref, w_ref, g0_ref, g1_ref, g2_ref,
             dya_ref, dyb_ref, dyc_ref, dp0_ref, dp1_ref, dp2_ref, dg0_ref, dg1_ref, dg2_ref):
        dm = _dot_nt(do_ref[...], wo_ref[...])
        ys = (ya_ref, yb_ref, yc_ref)
        gs = (g0_ref, g1_ref, g2_ref)
        dys = (dya_ref, dyb_ref, dyc_ref)
        dps = (dp0_ref, dp1_ref, dp2_ref)
        dgs = (dg0_ref, dg1_ref, dg2_ref)
        for j in range(3):
            s = _sigmoid(gs[j][...])
            pj = _dot(ys[j][...], w_ref[j])
            dpb = (dm * s).astype(BF16)
            dps[j][...] = dpb
            dgs[j][...] = (dm * pj * s * (1.0 - s)).astype(BF16)
            dys[j][...] = _dot_nt(dpb, w_ref[j])

    yshape = jax.ShapeDtypeStruct((t, BW), F32)
    dshape = jax.ShapeDtypeStruct((t, D), BF16)
    return _pcall(body, name=name, grid=(t // tm,),
                  in_specs=[_rows(tm, D), _layer(l, (D, D))] + [_rows(tm, BW)] * 3
                  + [_fix((3, BW, D))] + [_rows(tm, D, G0 // D + j) for j in range(3)],
                  out_specs=[_rows(tm, BW)] * 3 + [_rows(tm, D)] * 6,
                  out_shape=[yshape] * 3 + [dshape] * 6,
                  compiler_params=_cp("parallel"))(doutb, wo, ya, yb, yc, wbr, z, z, z)


def adamw(w, g, m, v, *, name):
    r, c = w.shape
    tm = _pick(r, (128, 64, 32, 16, 8))

    def body(w_ref, g_ref, m_ref, v_ref, d_ref, mo_ref, vo_ref):
        gg = g_ref[...]
        mn = ADAM_B1 * m_ref[...] + (1.0 - ADAM_B1) * gg
        vn = ADAM_B2 * v_ref[...] + (1.0 - ADAM_B2) * (gg * gg)
        m_hat = mn / (1.0 - ADAM_B1 ** ADAM_STEP)
        v_hat = vn / (1.0 - ADAM_B2 ** ADAM_STEP)
        d_ref[...] = -ADAM_LR * (m_hat / (jnp.sqrt(v_hat) + ADAM_EPS) + ADAM_WD * w_ref[...])
        mo_ref[...] = mn
        vo_ref[...] = vn

    shp = jax.ShapeDtypeStruct((r, c), F32)
    return _pcall(body, name=name, grid=(r // tm,), in_specs=[_rows(tm, c)] * 4, out_specs=[_rows(tm, c)] * 3,
                  out_shape=[shp] * 3, compiler_params=_cp("parallel"))(w, g, m, v)


def _place():
    return lax.axis_index("x"), lax.axis_index("y"), lax.axis_index("c")


def _remote(src, dst, send_sems, recv_sems, k, to):
    return pltpu.make_async_remote_copy(src_ref=src, dst_ref=dst, send_sem=send_sems.at[k],
                                        recv_sem=recv_sems.at[k], device_id=to, device_id_type=MESH)


def gather_weights(shards):
    n = len(shards)

    def body(*refs):
        ins, outs = refs[:n], refs[n:2 * n]
        send_sems, recv_sems, own_send, own_recv = refs[2 * n:]
        x, y, c = _place()
        sib = (x, y, 1 - c)
        chips = [(1 - x, y), (x, 1 - y), (1 - x, 1 - y)]
        k_me = 2 * x + y
        mine, first, passed = [], [], []
        for t in range(n):
            for l in range(DEPTH):
                mine.append(_remote(ins[t].at[l], outs[t].at[l, k_me], own_send, own_recv, 2 * t + l, sib))
            for j, chip in enumerate(chips):
                first.append(_remote(ins[t].at[c], outs[t].at[c, k_me], send_sems, recv_sems, 6 * t + j, (*chip, c)))
        for cp in mine + first:
            cp.start()
        for t in range(n):
            for j, chip in enumerate(chips):
                blk = outs[t].at[c, 2 * chip[0] + chip[1]]
                _remote(blk, blk, send_sems, recv_sems, 6 * t + j, (*chip, c)).wait_recv()
                cp = _remote(blk, blk, send_sems, recv_sems, 6 * t + 3 + j, sib)
                cp.start()
                passed.append(cp)
        for t in range(n):
            for j, chip in enumerate(chips):
                blk = outs[t].at[1 - c, 2 * chip[0] + chip[1]]
                _remote(blk, blk, send_sems, recv_sems, 6 * t + 3 + j, sib).wait_recv()
        for cp in first + passed:
            cp.wait_send()
        for cp in mine:
            cp.wait()

    return _pcall(body, name="gather_weights", in_specs=[ANY] * n, out_specs=[ANY] * n,
                  out_shape=[jax.ShapeDtypeStruct((DEPTH, 4) + s.shape[1:], s.dtype) for s in shards],
                  scratch_shapes=[pltpu.SemaphoreType.DMA((6 * n,)), pltpu.SemaphoreType.DMA((6 * n,)),
                                  pltpu.SemaphoreType.DMA((2 * n,)), pltpu.SemaphoreType.DMA((2 * n,))])(*shards)


def pair_exchange(g0, g1):
    n = len(g0)

    def body(*refs):
        a0, a1, outs = refs[:n], refs[n:2 * n], refs[2 * n:3 * n]
        send_sems, recv_sems = refs[3 * n:]
        x, y, c = _place()
        sib = (x, y, 1 - c)

        @pl.when(c == 0)
        def _():
            for t in range(n):
                _remote(a1[t], outs[t], send_sems, recv_sems, t, sib).start()

        @pl.when(c == 1)
        def _():
            for t in range(n):
                _remote(a0[t], outs[t], send_sems, recv_sems, t, sib).start()

        for t in range(n):
            _remote(a0[t], outs[t], send_sems, recv_sems, t, sib).wait()

    return _pcall(body, name="pair_exchange", in_specs=[ANY] * (2 * n), out_specs=[ANY] * n,
                  out_shape=[jax.ShapeDtypeStruct(a.shape, a.dtype) for a in g0],
                  scratch_shapes=[pltpu.SemaphoreType.DMA((n,)), pltpu.SemaphoreType.DMA((n,))])(*g0, *g1)


def chip_exchange(s1):
    n = len(s1)

    def body(*refs):
        ins, outs = refs[:n], refs[n:2 * n]
        send_sems, recv_sems = refs[2 * n:]
        x, y, c = _place()
        chips = [(1 - x, y), (x, 1 - y), (1 - x, 1 - y)]
        cps = [_remote(ins[t].at[2 * chip[0] + chip[1]], outs[t].at[j], send_sems, recv_sems, 3 * t + j, (*chip, c))
               for t in range(n) for j, chip in enumerate(chips)]
        for cp in cps:
            cp.start()
        for cp in cps:
            cp.wait()

    return _pcall(body, name="chip_exchange", in_specs=[ANY] * n, out_specs=[ANY] * n,
                  out_shape=[jax.ShapeDtypeStruct((3,) + a.shape[1:], a.dtype) for a in s1],
                  scratch_shapes=[pltpu.SemaphoreType.DMA((3 * n,)), pltpu.SemaphoreType.DMA((3 * n,))])(*s1)


def pair_share(s2):
    n = len(s2)

    def body(*refs):
        ins, outs = refs[:n], refs[n:2 * n]
        send_sems, recv_sems = refs[2 * n:]
        x, y, c = _place()
        sib = (x, y, 1 - c)
        cps = [_remote(ins[t].at[c], outs[t].at[c], send_sems, recv_sems, t, sib) for t in range(n)]
        for cp in cps:
            cp.start()
        for t in range(n):
            cps[t].wait_send()
            _remote(ins[t].at[c], outs[t].at[1 - c], send_sems, recv_sems, t, sib).wait_recv()

    return _pcall(body, name="pair_share", in_specs=[ANY] * n, out_specs=[ANY] * n,
                  out_shape=[jax.ShapeDtypeStruct(a.shape, a.dtype) for a in s2],
                  input_output_aliases={t: t for t in range(n)},
                  scratch_shapes=[pltpu.SemaphoreType.DMA((n,)), pltpu.SemaphoreType.DMA((n,))])(*s2)


def small_exchange(gs):
    rows, width = gs.shape

    def body(g_ref, o_ref, send_sems, recv_sems):
        x, y, c = _place()
        cps = []
        for r in range(1, 8):
            dx, dy, dc = (r >> 2) & 1, (r >> 1) & 1, r & 1
            to = (x if dx == 0 else 1 - x, y if dy == 0 else 1 - y, c if dc == 0 else 1 - c)
            cps.append(_remote(g_ref, o_ref.at[r - 1], send_sems, recv_sems, r - 1, to))
        for cp in cps:
            cp.start()
        for cp in cps:
            cp.wait()

    return _pcall(body, name="small_exchange", in_specs=[ANY], out_specs=ANY,
                  out_shape=jax.ShapeDtypeStruct((7, rows, width), gs.dtype),
                  scratch_shapes=[pltpu.SemaphoreType.DMA((7,)), pltpu.SemaphoreType.DMA((7,))])(gs)


def _row_tile(rows):
    return _pick(rows, (256, 352, 128, 64, 32, 16))


def pair_add(g0, g1, rb, core, *, name):
    _, rows, width = g0.shape
    tr = _row_tile(rows)

    def body(c_ref, g0_ref, g1_ref, r_ref, o_ref, ob_ref):
        s = jnp.where(c_ref[0] == 0, g0_ref[...], g1_ref[...]) + r_ref[...]
        o_ref[...] = s
        ob_ref[...] = s.astype(BF16)

    blk = pl.BlockSpec((None, tr, width), lambda k, i, c_ref: (k, i, 0))
    gs = pltpu.PrefetchScalarGridSpec(
        num_scalar_prefetch=1, grid=(4, rows // tr),
        in_specs=[pl.BlockSpec((None, tr, width), lambda k, i, c_ref: (k * (1 - c_ref[0]), i * (1 - c_ref[0]), 0)),
                  pl.BlockSpec((None, tr, width), lambda k, i, c_ref: (k * c_ref[0], i * c_ref[0], 0)), blk],
        out_specs=[blk, blk])
    return _pcall(body, name=name, grid_spec=gs,
                  out_shape=[jax.ShapeDtypeStruct(g0.shape, F32), jax.ShapeDtypeStruct(g0.shape, BF16)],
                  compiler_params=_cp("parallel", "parallel"))(core, g0, g1, rb)


def chip_add(s1, rb2, chip, core, *, name):
    _, rows, width = s1.shape
    tr = _row_tile(rows)

    def body(k_ref, c_ref, s_ref, r_ref, o_ref):
        o_ref[...] = ((s_ref[...] + r_ref[0].astype(F32)) + r_ref[1].astype(F32)) + r_ref[2].astype(F32)

    gs = pltpu.PrefetchScalarGridSpec(
        num_scalar_prefetch=2, grid=(rows // tr,),
        in_specs=[pl.BlockSpec((None, tr, width), lambda i, k_ref, c_ref: (k_ref[0], i, 0)),
                  pl.BlockSpec((3, tr, width), lambda i, k_ref, c_ref: (0, i, 0))],
        out_specs=pl.BlockSpec((None, tr, width), lambda i, k_ref, c_ref: (c_ref[0], i, 0)))
    return _pcall(body, name=name, grid_spec=gs, out_shape=jax.ShapeDtypeStruct((DEPTH, rows, width), F32),
                  compiler_params=_cp("parallel"))(chip, core, s1, rb2)


def small_add(gs_own, slots, me):
    rows, width = gs_own.shape
    tr = _pick(rows, (64, 32, 16, 8))

    def body(me_ref, g_ref, s_ref, o_ref):
        me_v = me_ref[0]
        total = None
        for d in range(8):
            rel = jnp.bitwise_xor(me_v, d)
            val = jnp.where(rel == 0, g_ref[...], s_ref[jnp.maximum(rel - 1, 0)])
            total = val if total is None else total + val
        o_ref[...] = total

    gs = pltpu.PrefetchScalarGridSpec(
        num_scalar_prefetch=1, grid=(rows // tr,),
        in_specs=[pl.BlockSpec((tr, width), lambda i, m_ref: (i, 0)),
                  pl.BlockSpec((7, tr, width), lambda i, m_ref: (0, i, 0))],
        out_specs=pl.BlockSpec((tr, width), lambda i, m_ref: (i, 0)))
    return _pcall(body, name="small_add", grid_spec=gs, out_shape=jax.ShapeDtypeStruct((rows, width), F32),
                  compiler_params=_cp("parallel"))(me, gs_own, slots)


SHARDED = (("ffn1_w_up", (D, UPW)), ("ffn1_w_down", (DFF // 4, D)), ("w_in", (D, D_IN // 4)),
           ("conv_w", (4, BW // 4)), ("gla_w_g2", (LOW_W, 64)), ("w_branch", (3 * BW, D // 4)),
           ("w_out", (D // 4, D)), ("ffn2_w_up", (D, UPW)), ("ffn2_w_down", (DFF // 4, D)),
           ("ple_w_proj", (PLE, D // 4)), ("ple_w_gate", (D // 4, D)))
SMALL = ("ln1_g", "ln1_b", "conv_b", "lru_wa", "lru_ba", "lru_wx", "lru_bx", "lru_lambda", "gla_b_g",
         "gla_norm_g", "fox_b_f", "ln2_g", "ln2_b", "ln3_g", "ln3_b", "ple_b_gate", "ln4_g", "ln4_b")
WEIGHTS = ('ffn1_w_up', 'ffn1_w_down', 'ln1_g', 'ln1_b', 'w_in', 'conv_w', 'conv_b', 'lru_wa', 'lru_ba',
           'lru_wx', 'lru_bx', 'lru_lambda', 'gla_w_g2', 'gla_b_g', 'gla_norm_g', 'fox_b_f', 'w_branch',
           'w_out', 'ln2_g', 'ln2_b', 'ffn2_w_up', 'ffn2_w_down', 'ln3_g', 'ln3_b', 'ple_w_proj',
           'ple_w_gate', 'ple_b_gate', 'ln4_g', 'ln4_b')


def _pack(parts, rows):
    flat = jnp.concatenate([p.reshape(-1) for p in parts])
    flat = jnp.pad(flat, (0, rows * PACK_W - flat.shape[0]))
    return flat.reshape(rows, PACK_W)


def _cols_join(parts):
    return jnp.concatenate([parts[k] for k in range(4)], axis=-1)


def _cols_split(full):
    r, c4 = full.shape
    return full.reshape(r, 4, c4 // 4).transpose(1, 0, 2)


def _regroup_in(w):
    pad = jnp.zeros(w.shape[:-1] + (ZW - D_IN,), w.dtype)
    return jnp.concatenate([w[..., 0:2048], w[..., 2064:4112], w[..., 4120:7192], w[..., 2048:2064],
                            w[..., 4112:4120], pad], axis=-1)


def _regroup_out(g):
    return jnp.concatenate([g[..., 0:2048], g[..., 7168:7184], g[..., 2048:4096], g[..., 7184:7192],
                            g[..., 4096:7168]], axis=-1)


def _block_diag(w):
    eye = jnp.eye(8, dtype=w.dtype)
    return (eye[:, None, :, None] * w[:, :, None, :]).reshape(BW, BW)


def _diag_blocks(dense):
    return jnp.stack([dense[64 * n:64 * (n + 1), 64 * n:64 * (n + 1)] for n in range(8)])


def _layer_weights(gw, small, l):
    w = {"up1": gw["ffn1_w_up"], "up2": gw["ffn2_w_up"],
         "dn1": gw["ffn1_w_down"].reshape(DEPTH, DFF, D), "dn2": gw["ffn2_w_down"].reshape(DEPTH, DFF, D),
         "wo": gw["w_out"].reshape(DEPTH, D, D), "wgt": gw["ple_w_gate"].reshape(DEPTH, D, D)}
    w["win"] = _regroup_in(_cols_join(gw["w_in"][l]))
    w["cw"] = _cols_join(gw["conv_w"][l])
    w["wa"] = _block_diag(small["lru_wa"][l]).astype(BF16)
    w["wx"] = _block_diag(small["lru_wx"][l]).astype(BF16)
    w["wg2p"] = jnp.pad(_cols_join(gw["gla_w_g2"][l]), ((0, 128 - LOW_W), (0, 0)))
    w["wbr"] = _cols_join(gw["w_branch"][l].reshape(4, 3, BW, D // 4))
    w["wp"] = _cols_join(gw["ple_w_proj"][l])
    for n in ("ln1_g", "ln1_b", "ln2_g", "ln2_b", "ln3_g", "ln3_b", "ln4_g", "ln4_b", "conv_b", "lru_ba",
              "lru_bx", "lru_lambda", "gla_b_g", "gla_norm_g", "ple_b_gate"):
        w[n] = small[n][l][None, :]
    w["bfp"] = jnp.pad(small["fox_b_f"][l], (LOW_W, 128 - LOW_W - FOX_H))[None, :]
    return w


def _heads_t(a):
    ht = a[:, LOW_W:LOW_W + FOX_H].T
    return ht[:, :, None], ht[:, None, :]


def _layer_fwd(x, xb, pb, w, l):
    s = {"x0": x, "x0b": xb}
    tag = "l%d_" % l
    gate, up, act = ffn_up(xb, w["up1"], l, name=tag + "ffn1_up")
    r1, x1, x1b = matmul_res_ln(act, w["dn1"], l, x, w["ln1_g"], w["ln1_b"], mm_scale=0.5, name=tag + "ffn1_down")
    s.update(gate1=gate, up1=up, act1=act, r1=r1, x1=x1, x1b=x1b)
    z, zb = matmul(x1b, w["win"], also_bf16=True, tn=_pick(ZW, (2432,)), name=tag + "mix_in")
    xc, xcb, h, ya = lru_fwd(z, w["cw"], w["conv_b"], w["wa"], w["wx"], w["lru_ba"], w["lru_bx"],
                             w["lru_lambda"], name=tag + "lru_fwd")
    yb, states = gla_fwd(z, zb, w["wg2p"], w["gla_b_g"], w["gla_norm_g"], name=tag + "gla_fwd")
    fcum = fox_fcum(z, w["bfp"], name=tag + "fox_fcum")
    fq, fk = _heads_t(fcum)
    yc, ycf, lse = fox_fwd(zb, fq, fk, name=tag + "fox_fwd")
    merged = merge_fwd(ya, yb, yc, w["wbr"], z, name=tag + "merge_fwd")
    r2, x2, x2b = matmul_res_ln(merged, w["wo"], l, x1, w["ln2_g"], w["ln2_b"], mm_scale=1.0, name=tag + "mix_out")
    s.update(z=z, zb=zb, xc=xc, xcb=xcb, h=h, ya=ya, yb=yb, states=states, fq=fq, fk=fk, yc=yc, ycf=ycf,
             lse=lse, merged=merged, r2=r2, x2=x2, x2b=x2b)
    gate, up, act = ffn_up(x2b, w["up2"], l, name=tag + "ffn2_up")
    r3, x3, x3b = matmul_res_ln(act, w["dn2"], l, x2, w["ln3_g"], w["ln3_b"], mm_scale=0.5, name=tag + "ffn2_down")
    s.update(gate2=gate, up2=up, act2=act, r3=r3, x3=x3, x3b=x3b)
    r4, x4, x4b = ple_fwd(x3b, x3, pb, w["wgt"], l, w["wp"], w["ple_b_gate"], w["ln4_g"], w["ln4_b"],
                          name=tag + "ple_fwd")
    s.update(r4=r4, pb=pb)
    return x4, x4b, s


def _ffn_bwd(dy, s, w, n, xin_b, l, tag):
    k = {"1": ("r1", "ln1_g", "gate1", "up1", "act1"), "2": ("r3", "ln3_g", "gate2", "up2", "act2")}[n]
    dr, dfb, dg, db = ln_bwd(dy, s[k[0]], w[k[1]], out_scale=0.5, name=tag + "ln_bwd")
    dgate, dup = ffn_down_bwd(dfb, w["dn" + n], l, s[k[2]], s[k[3]], name=tag + "down_bwd")
    dx = ffn_dx(dgate, dup, w["up" + n], l, dr, name=tag + "dx")
    dwup = matmul_tn_up(xin_b, dgate, dup, name=tag + "dw_up")
    dwdn = matmul_tn(s[k[4]], dfb, name=tag + "dw_down").reshape(4, DFF // 4, D)
    return dx, dwup, dwdn, dg[0], db[0]


def _layer_bwd(dy, s, w, l):
    g = {}
    tag = "l%d_" % l
    dr4, dglb, dpeb, dg4, db4, dbg = ple_bwd(dy, s["r4"], s["x3b"], s["pb"], w["wgt"], l, w["wp"], w["ple_b_gate"],
                                             w["ln4_g"], name=tag + "ple_bwd")
    dx3 = matmul(dglb, w["wgt"], nt=True, b_lead=(l,), res=dr4, res_scale=ALPHA, tm=1024, tn=1024,
                 name=tag + "ple_dx")
    g["ple_w_gate"] = matmul_tn(s["x3b"], dglb, name=tag + "ple_dw_gate").reshape(4, D // 4, D)
    g["ple_w_proj"] = _cols_split(matmul_tn(s["pb"], dpeb, name=tag + "ple_dw_proj"))
    g["ln4_g"], g["ln4_b"], g["ple_b_gate"] = dg4[0], db4[0], dbg[0]
    dx2, g["ffn2_w_up"], g["ffn2_w_down"], g["ln3_g"], g["ln3_b"] = _ffn_bwd(dx3, s, w, "2", s["x2b"], l,
                                                                             tag + "ffn2_")
    dr2, doutb, dg2, db2 = ln_bwd(dx2, s["r2"], w["ln2_g"], out_scale=1.0, name=tag + "mix_ln_bwd")
    g["ln2_g"], g["ln2_b"] = dg2[0], db2[0]
    g["w_out"] = matmul_tn(s["merged"], doutb, name=tag + "dw_out").reshape(4, D // 4, D)
    z, zb = s["z"], s["zb"]
    (dya, dyb, dyc, dp0, dp1, dp2, dgl0, dgl1, dgl2) = merge_bwd(
        doutb, w["wo"], l, s["ya"], s["yb"], s["yc"], w["wbr"], z, name=tag + "merge_bwd")
    dwbr = jnp.stack([matmul_tn(s["ya"], dp0, name=tag + "dw_br0"), matmul_tn(s["yb"], dp1, name=tag + "dw_br1"),
                      matmul_tn(s["yc"], dp2, name=tag + "dw_br2")])
    g["w_branch"] = _cols_split(dwbr.reshape(3 * BW, D))
    day, dxc, dprb, dpib, dba, dbx, dlam = lru_bwd(dya, z, s["h"], s["xc"], w["wa"], w["wx"],
                                                   w["lru_ba"], w["lru_bx"], w["lru_lambda"], name=tag + "lru_bwd")
    dax, dcw, dcb = conv_bwd(dxc, z, w["cw"], name=tag + "conv_bwd")
    g["lru_wa"] = _diag_blocks(matmul_tn(s["xcb"], dprb, name=tag + "dw_lru_a"))
    g["lru_wx"] = _diag_blocks(matmul_tn(s["xcb"], dpib, name=tag + "dw_lru_x"))
    g["lru_ba"], g["lru_bx"], g["lru_lambda"] = dba[0], dbx[0], dlam[0]
    g["conv_w"], g["conv_b"] = _cols_split(dcw), dcb[0]
    dbq, dbk, dbv, dbr, dmisc_g, dpreb, dbgg, dng = gla_bwd(dyb, z, zb, s["states"], w["wg2p"], w["gla_b_g"],
                                                            w["gla_norm_g"], name=tag + "gla_bwd")
    miscb = zb[:, MISC:]
    g["gla_w_g2"] = _cols_split(matmul_tn(miscb, dpreb, name=tag + "dw_g2")[:LOW_W])
    g["gla_b_g"], g["gla_norm_g"] = dbgg[0], dng[0]
    dl = fox_delta(dyc, s["ycf"], name=tag + "fox_delta")
    dlq = dl[:, :FOX_H].T[:, :, None]
    dcq, dck, dcv, dfk, dfq = fox_bwd(zb, dyc, s["fq"], s["fk"], s["lse"], dlq, name=tag + "fox_bwd")
    dfc = jnp.pad(dfk[:, 0, :].T + dfq[:, ::64], ((0, 0), (LOW_W, 128 - LOW_W - FOX_H)))
    dmiscb, dbf = fox_dcf(dfc, z, w["bfp"], dmisc_g, name=tag + "fox_dcf")
    g["fox_b_f"] = dbf[0, LOW_W:LOW_W + FOX_H]
    dz = jnp.concatenate([dax, day, dbq, dbk, dbv, dbr, dcq.astype(BF16), dck, dcv, dgl0, dgl1, dgl2, dmiscb],
                         axis=1)
    dx1 = matmul(dz, w["win"], nt=True, res=dr2, res_scale=ALPHA, tm=1024, tn=1024, tk=_pick(ZW, (2432,)),
                 name=tag + "mix_dx")
    g["w_in"] = _cols_split(_regroup_out(matmul_tn(s["x1b"], dz, name=tag + "dw_in")))
    dx0, g["ffn1_w_up"], g["ffn1_w_down"], g["ln1_g"], g["ln1_b"] = _ffn_bwd(dx1, s, w, "1", s["x0b"], l,
                                                                             tag + "ffn1_")
    return dx0, g


def _local_step(x, p, target, gw, small):
    xcur = x
    xb = xcur.astype(BF16)
    layer_w, saved = [], []
    for l in range(DEPTH):
        w = _layer_weights(gw, small, l)
        xcur, xb, s = _layer_fwd(xcur, xb, p[l].astype(BF16), w, l)
        layer_w.append(w)
        saved.append(s)
    dy, sq = loss_head(xcur, target, name="loss_head")
    grads = [None] * DEPTH
    for l in reversed(range(DEPTH)):
        dy, grads[l] = _layer_bwd(dy, saved[l], layer_w[l], l)
    return 0.5 * jnp.sum(sq) / float(D), dy, grads


def kernel(x, p, ffn1_w_up, ffn1_w_down, ln1_g, ln1_b, w_in, conv_w, conv_b, lru_wa, lru_ba, lru_wx, lru_bx, lru_lambda, gla_w_g2, gla_b_g, gla_norm_g, fox_b_f, w_branch, w_out, ln2_g, ln2_b, ffn2_w_up, ffn2_w_down, ln3_g, ln3_b, ple_w_proj, ple_w_gate, ple_b_gate, ln4_g, ln4_b, loss_target, m_ffn1_w_up, m_ffn1_w_down, m_ln1_g, m_ln1_b, m_w_in, m_conv_w, m_conv_b, m_lru_wa, m_lru_ba, m_lru_wx, m_lru_bx, m_lru_lambda, m_gla_w_g2, m_gla_b_g, m_gla_norm_g, m_fox_b_f, m_w_branch, m_w_out, m_ln2_g, m_ln2_b, m_ffn2_w_up, m_ffn2_w_down, m_ln3_g, m_ln3_b, m_ple_w_proj, m_ple_w_gate, m_ple_b_gate, m_ln4_g, m_ln4_b, v_ffn1_w_up, v_ffn1_w_down, v_ln1_g, v_ln1_b, v_w_in, v_conv_w, v_conv_b, v_lru_wa, v_lru_ba, v_lru_wx, v_lru_bx, v_lru_lambda, v_gla_w_g2, v_gla_b_g, v_gla_norm_g, v_fox_b_f, v_w_branch, v_w_out, v_ln2_g, v_ln2_b, v_ffn2_w_up, v_ffn2_w_down, v_ln3_g, v_ln3_b, v_ple_w_proj, v_ple_w_gate, v_ple_b_gate, v_ln4_g, v_ln4_b):
    args = dict(locals())
    wts = {n: args[n] for n in WEIGHTS}
    mom = {n: args["m_" + n] for n in WEIGHTS}
    var = {n: args["v_" + n] for n in WEIGHTS}
    cx, cy, cc = lax.axis_index("x"), lax.axis_index("y"), lax.axis_index("c")

    shards = [wts[n].reshape((DEPTH,) + rc).astype(F32 if n == "conv_w" else BF16) for n, rc in SHARDED]
    gw = dict(zip([n for n, _ in SHARDED], gather_weights(shards)))
    small = {n: wts[n] for n in SMALL}

    loss_local, dx, grads = _local_step(x[0], p[:, 0], loss_target[0], gw, small)
    loss = lax.psum(loss_local, ("x", "y", "c"))
    grad_x = dx[None]

    core = jnp.reshape(cc, (1,)).astype(jnp.int32)
    chip = jnp.reshape(2 * cx + cy, (1,)).astype(jnp.int32)
    g0 = [grads[0][n] for n, _ in SHARDED]
    g1 = [grads[1][n] for n, _ in SHARDED]
    rb = pair_exchange(g0, g1)
    s1 = [pair_add(a0, a1, r, core, name="pair_add_" + n) for (n, _), a0, a1, r in zip(SHARDED, g0, g1, rb)]
    rb2 = chip_exchange([sb for _, sb in s1])
    s2 = [chip_add(sf, r, chip, core, name="chip_add_" + n) for (n, _), (sf, _), r in zip(SHARDED, s1, rb2)]
    gout = {n: gsh.reshape(wts[n].shape) for (n, _), gsh in zip(SHARDED, pair_share(s2))}

    small_sizes = [wts[n].size for n in SMALL]
    srows = -(-sum(small_sizes) // (8 * PACK_W)) * 8
    gs = _pack([jnp.stack([grads[l][n] for l in range(DEPTH)]) for n in SMALL], srows)
    me = jnp.reshape(4 * cx + 2 * cy + cc, (1,)).astype(jnp.int32)
    gsum = small_add(gs, small_exchange(gs), me)

    delta, new_m, new_v = {}, {}, {}
    for n, (_, cols) in SHARDED:
        shp = wts[n].shape
        v2 = lambda a: a.reshape(-1, cols)
        d, mn, vn = adamw(v2(wts[n]), v2(gout[n]), v2(mom[n]), v2(var[n]), name="adamw_" + n)
        delta[n], new_m[n], new_v[n] = d.reshape(shp), mn.reshape(shp), vn.reshape(shp)
    d, mn, vn = adamw(_pack([wts[n] for n in SMALL], srows), gsum, _pack([mom[n] for n in SMALL], srows),
                      _pack([var[n] for n in SMALL], srows), name="adamw_small")
    off = 0
    for n, size in zip(SMALL, small_sizes):
        shp = wts[n].shape
        take = lambda a: a.reshape(-1)[off:off + size].reshape(shp)
        gout[n], delta[n], new_m[n], new_v[n] = take(gsum), take(d), take(mn), take(vn)
        off += size

    return (loss, grad_x, *[gout[n] for n in WEIGHTS], *[delta[n] for n in WEIGHTS],
            *[new_m[n] for n in WEIGHTS], *[new_v[n] for n in WEIGHTS])
```

```python
import functools
import math

import jax
import jax.numpy as jnp
from jax import lax
from jax.experimental import pallas as pl
from jax.experimental.pallas import tpu as pltpu

F32 = jnp.float32
BF16 = jnp.bfloat16

D = 1024
DFF = 2816
BW = 512
PLE = 256
DEPTH = 2
ALPHA = (2 * DEPTH) ** 0.25
LN_EPS = 1e-5
RMS_EPS = 1e-6
LRU_C = 8.0
GLA_TAU = 16.0
CHUNK = 64
D_IN = 7192
ZW = 7296
AX, AY, BQ, BK, BV, BR, CQ, CK, CV, G0, MISC = 0, 512, 1024, 1280, 1536, 2048, 2560, 3072, 3584, 4096, 7168
LOW_W, FOX_H = 16, 8
ADAM_LR, ADAM_B1, ADAM_B2, ADAM_EPS, ADAM_WD, ADAM_STEP = 0.001, 0.9, 0.999, 1e-08, 0.01, 10
PACK_W = 1024
VMEM_LIMIT = 56 << 20

MESH = pl.DeviceIdType.MESH
ANY = pl.BlockSpec(memory_space=pl.ANY)


def _pcall(body, **kw):
    return pl.pallas_call(body, **kw)


def _cp(*dims):
    return pltpu.CompilerParams(dimension_semantics=dims, vmem_limit_bytes=VMEM_LIMIT)


def _dot(a, b):
    return jnp.dot(a, b, preferred_element_type=F32)


def _dot_nt(a, b):
    return lax.dot_general(a, b, (((1,), (1,)), ((), ())), preferred_element_type=F32)


def _dot_tn(a, b):
    return lax.dot_general(a, b, (((0,), (0,)), ((), ())), preferred_element_type=F32)


def _dot_hi(a, b):
    return jnp.dot(a, b, preferred_element_type=F32, precision=lax.Precision.HIGHEST)


def _sigmoid(x):
    return 1.0 / (1.0 + jnp.exp(-x))


def _softplus(x):
    return jnp.maximum(x, 0.0) + jnp.log(1.0 + jnp.exp(-jnp.abs(x)))


def _log_sigmoid(x):
    return -_softplus(-x)


def _expm1(x):
    poly = x * (1.0 + x * (0.5 + x * (1.0 / 6.0 + x * (1.0 / 24.0 + x * (1.0 / 120.0 + x * (1.0 / 720.0))))))
    return jnp.where(jnp.abs(x) < 0.1, poly, jnp.exp(x) - 1.0)


_GELU_C = math.sqrt(2.0 / math.pi)


def _gelu(x):
    return 0.5 * x * (1.0 + jnp.tanh(_GELU_C * (x + 0.044715 * x * x * x)))


def _gelu_grad(x):
    t = jnp.tanh(_GELU_C * (x + 0.044715 * x * x * x))
    return 0.5 * (1.0 + t) + 0.5 * x * (1.0 - t * t) * _GELU_C * (1.0 + 3.0 * 0.044715 * x * x)


def _ln_stats(r):
    mu = jnp.mean(r, axis=-1, keepdims=True)
    xc = r - mu
    var = jnp.mean(xc * xc, axis=-1, keepdims=True)
    return xc, lax.rsqrt(var + LN_EPS)


def _pick(n, cands):
    for c in cands:
        if n % c == 0:
            return c
    return n


def _rows(tm, w, col=0):
    return pl.BlockSpec((tm, w), lambda i: (i, col))


def _fix(shape):
    nd = len(shape)
    return pl.BlockSpec(shape, lambda i: (0,) * nd)


def _layer(l, shape):
    nd = len(shape)
    return pl.BlockSpec((None,) + tuple(shape), lambda i: (l,) + (0,) * nd)


def matmul(a, b, *, name, nt=False, b_lead=(), res=None, res_scale=1.0, also_bf16=False, tm=512, tn=512,
           tk=None):
    m, k = a.shape
    n = b.shape[-2] if nt else b.shape[-1]
    tm, tn = min(tm, m), min(tn, n)
    tk = k if tk is None else tk
    nk = k // tk
    has_res = res is not None
    lead = tuple(b_lead)
    dot = _dot_nt if nt else _dot

    def body(*refs):
        a_ref, b_ref = refs[0], refs[1]
        pos = 2
        r_ref = None
        if has_res:
            r_ref = refs[pos]
            pos += 1
        o_ref = refs[pos]
        pos += 1
        ob_ref = None
        if also_bf16:
            ob_ref = refs[pos]
            pos += 1
        acc = refs[pos]
        kk = pl.program_id(2)

        @pl.when(kk == 0)
        def _():
            acc[...] = jnp.zeros_like(acc)

        acc[...] += dot(a_ref[...], b_ref[...])

        @pl.when(kk == nk - 1)
        def _():
            v = acc[...]
            if has_res:
                v = v + res_scale * r_ref[...]
            o_ref[...] = v
            if also_bf16:
                ob_ref[...] = v.astype(BF16)

    none = (None,) * len(lead)
    if nt:
        b_spec = pl.BlockSpec(none + (tn, tk), lambda j, i, kk: lead + (j, kk))
    else:
        b_spec = pl.BlockSpec(none + (tk, tn), lambda j, i, kk: lead + (kk, j))
    in_specs = [pl.BlockSpec((tm, tk), lambda j, i, kk: (i, kk)), b_spec]
    args = [a, b]
    if has_res:
        in_specs.append(pl.BlockSpec((tm, tn), lambda j, i, kk: (i, j)))
        args.append(res)
    out_shape = [jax.ShapeDtypeStruct((m, n), F32)]
    out_specs = [pl.BlockSpec((tm, tn), lambda j, i, kk: (i, j))]
    if also_bf16:
        out_shape.append(jax.ShapeDtypeStruct((m, n), BF16))
        out_specs.append(pl.BlockSpec((tm, tn), lambda j, i, kk: (i, j)))
    out = _pcall(body, name=name, grid=(n // tn, m // tm, nk), in_specs=in_specs, out_specs=out_specs,
                 out_shape=out_shape, scratch_shapes=[pltpu.VMEM((tm, tn), F32)],
                 compiler_params=_cp("parallel", "parallel", "arbitrary"))(*args)
    return out if also_bf16 else out[0]


def matmul_tn(a, b, *, name):
    t, k = a.shape
    n = b.shape[1]
    tt = min(2048, t)
    tk = _pick(k, (512, 1408, 256, 128))
    tn = _pick(n, (1024, 1408, 2432, 512, 256, 128))
    nt = t // tt

    def body(a_ref, b_ref, o_ref):
        @pl.when(pl.program_id(2) == 0)
        def _():
            o_ref[...] = jnp.zeros_like(o_ref)

        o_ref[...] += _dot_tn(a_ref[...], b_ref[...])

    return _pcall(body, name=name, grid=(k // tk, n // tn, nt),
                  in_specs=[pl.BlockSpec((tt, tk), lambda i, j, s: (s, i)),
                            pl.BlockSpec((tt, tn), lambda i, j, s: (s, j))],
                  out_specs=pl.BlockSpec((tk, tn), lambda i, j, s: (i, j)),
                  out_shape=jax.ShapeDtypeStruct((k, n), F32),
                  compiler_params=_cp("parallel", "parallel", "arbitrary"))(a, b)


UPW = 1408


def matmul_tn_up(a, dgate, dup, *, name):
    t, k = a.shape
    tt = min(2048, t)
    tk = 512

    def body(a_ref, g_ref, u_ref, o_ref):
        j = pl.program_id(1)

        @pl.when(pl.program_id(2) == 0)
        def _():
            o_ref[...] = jnp.zeros_like(o_ref)

        @pl.when(j < 2)
        def _():
            o_ref[...] += _dot_tn(a_ref[...], g_ref[...])

        @pl.when(j >= 2)
        def _():
            o_ref[...] += _dot_tn(a_ref[...], u_ref[...])

    return _pcall(body, name=name, grid=(k // tk, 4, t // tt),
                  in_specs=[pl.BlockSpec((tt, tk), lambda i, j, s: (s, i)),
                            pl.BlockSpec((tt, UPW), lambda i, j, s: (jnp.where(j < 2, s, 0), jnp.minimum(j, 1))),
                            pl.BlockSpec((tt, UPW), lambda i, j, s: (jnp.where(j >= 2, s, 0), jnp.maximum(j - 2, 0)))],
                  out_specs=pl.BlockSpec((None, tk, UPW), lambda i, j, s: (j, i, 0)),
                  out_shape=jax.ShapeDtypeStruct((4, k, UPW), F32),
                  compiler_params=_cp("parallel", "parallel", "arbitrary"))(a, dgate, dup)


def ffn_dx(dgate, dup, wup, l, res, *, name):
    t = dgate.shape[0]
    tm, tn = min(1024, t), 1024

    def body(g_ref, u_ref, w_ref, r_ref, o_ref, acc):
        kk = pl.program_id(2)

        @pl.when(kk == 0)
        def _():
            acc[...] = jnp.zeros_like(acc)

        @pl.when(kk < 2)
        def _():
            acc[...] += _dot_nt(g_ref[...], w_ref[...])

        @pl.when(kk >= 2)
        def _():
            acc[...] += _dot_nt(u_ref[...], w_ref[...])

        @pl.when(kk == 3)
        def _():
            o_ref[...] = acc[...] + ALPHA * r_ref[...]

    return _pcall(body, name=name, grid=(D // tn, t // tm, 4),
                  in_specs=[pl.BlockSpec((tm, UPW), lambda j, i, kk: (i, jnp.minimum(kk, 1))),
                            pl.BlockSpec((tm, UPW), lambda j, i, kk: (i, jnp.maximum(kk - 2, 0))),
                            pl.BlockSpec((None, None, tn, UPW), lambda j, i, kk: (l, kk, j, 0)),
                            pl.BlockSpec((tm, tn), lambda j, i, kk: (i, j))],
                  out_specs=pl.BlockSpec((tm, tn), lambda j, i, kk: (i, j)),
                  out_shape=jax.ShapeDtypeStruct((t, D), F32),
                  scratch_shapes=[pltpu.VMEM((tm, tn), F32)],
                  compiler_params=_cp("parallel", "parallel", "arbitrary"))(dgate, dup, wup, res)


def ffn_up(xb, wup, l, *, name):
    t = xb.shape[0]
    tm, tn = min(512, t), UPW

    def body(x_ref, wg_ref, wu_ref, g_ref, u_ref, a_ref):
        x = x_ref[...]
        g = _dot(x, wg_ref[...])
        u = _dot(x, wu_ref[...])
        g_ref[...] = g
        u_ref[...] = u
        a_ref[...] = (g * _sigmoid(g) * u).astype(BF16)

    blk = pl.BlockSpec((tm, tn), lambda j, i: (i, j))
    return _pcall(body, name=name, grid=(DFF // tn, t // tm),
                  in_specs=[pl.BlockSpec((tm, D), lambda j, i: (i, 0)),
                            pl.BlockSpec((None, None, D, tn), lambda j, i: (l, j, 0, 0)),
                            pl.BlockSpec((None, None, D, tn), lambda j, i: (l, 2 + j, 0, 0))],
                  out_specs=[blk, blk, blk],
                  out_shape=[jax.ShapeDtypeStruct((t, DFF), F32), jax.ShapeDtypeStruct((t, DFF), F32),
                             jax.ShapeDtypeStruct((t, DFF), BF16)],
                  compiler_params=_cp("parallel", "parallel"))(xb, wup, wup)


def matmul_res_ln(a, w, l, res, g, b, *, mm_scale, name):
    t, k = a.shape
    tm = min(256, t)

    def body(a_ref, w_ref, res_ref, g_ref, b_ref, r_ref, y_ref, yb_ref):
        f = _dot(a_ref[...], w_ref[...])
        r = ALPHA * res_ref[...] + mm_scale * f
        xc, rstd = _ln_stats(r)
        y = xc * rstd * g_ref[...] + b_ref[...]
        r_ref[...] = r
        y_ref[...] = y
        yb_ref[...] = y.astype(BF16)

    return _pcall(body, name=name, grid=(t // tm,),
                  in_specs=[_rows(tm, k), _layer(l, (k, D)), _rows(tm, D), _fix((1, D)), _fix((1, D))],
                  out_specs=[_rows(tm, D)] * 3,
                  out_shape=[jax.ShapeDtypeStruct((t, D), F32), jax.ShapeDtypeStruct((t, D), F32),
                             jax.ShapeDtypeStruct((t, D), BF16)],
                  compiler_params=_cp("parallel"))(a, w, res, g, b)


def ln_bwd(dy, r, g, *, out_scale, name):
    t = dy.shape[0]
    tm = min(256, t)

    def body(dy_ref, r_ref, g_ref, dr_ref, drb_ref, dg_ref, db_ref):
        @pl.when(pl.program_id(0) == 0)
        def _():
            dg_ref[...] = jnp.zeros_like(dg_ref)
            db_ref[...] = jnp.zeros_like(db_ref)

        xc, rstd = _ln_stats(r_ref[...])
        xhat = xc * rstd
        d = dy_ref[...]
        dxh = d * g_ref[...]
        dr = rstd * (dxh - jnp.mean(dxh, axis=-1, keepdims=True)
                     - xhat * jnp.mean(dxh * xhat, axis=-1, keepdims=True))
        dr_ref[...] = dr
        drb_ref[...] = (out_scale * dr).astype(BF16)
        dg_ref[...] += jnp.sum(d * xhat, axis=0, keepdims=True)
        db_ref[...] += jnp.sum(d, axis=0, keepdims=True)

    return _pcall(body, name=name, grid=(t // tm,),
                  in_specs=[_rows(tm, D), _rows(tm, D), _fix((1, D))],
                  out_specs=[_rows(tm, D), _rows(tm, D), _fix((1, D)), _fix((1, D))],
                  out_shape=[jax.ShapeDtypeStruct((t, D), F32), jax.ShapeDtypeStruct((t, D), BF16),
                             jax.ShapeDtypeStruct((1, D), F32), jax.ShapeDtypeStruct((1, D), F32)],
                  compiler_params=_cp("arbitrary"))(dy, r, g)


def ffn_down_bwd(dfb, wd, l, gate, up, *, name):
    t = dfb.shape[0]
    tm, tn = min(512, t), UPW
    nj = DFF // tn

    def body(df_ref, w_ref, g_ref, u_ref, dg_ref, du_ref):
        da = _dot_nt(df_ref[...], w_ref[...])
        g = g_ref[...]
        s = _sigmoid(g)
        dg_ref[...] = (da * u_ref[...] * s * (1.0 + g * (1.0 - s))).astype(BF16)
        du_ref[...] = (da * g * s).astype(BF16)

    blk = pl.BlockSpec((tm, tn), lambda j, i: (i, j))
    return _pcall(body, name=name, grid=(nj, t // tm),
                  in_specs=[pl.BlockSpec((tm, D), lambda j, i: (i, 0)),
                            pl.BlockSpec((None, tn, D), lambda j, i: (l, j, 0)), blk, blk],
                  out_specs=[blk, blk],
                  out_shape=[jax.ShapeDtypeStruct((t, DFF), BF16), jax.ShapeDtypeStruct((t, DFF), BF16)],
                  compiler_params=_cp("parallel", "parallel"))(dfb, wd, gate, up)


def ple_fwd(xb, x, pb, wgate, l, wproj, bgate, g, b, *, name):
    t = x.shape[0]
    tm = min(256, t)

    def body(xb_ref, x_ref, p_ref, wg_ref, wp_ref, bg_ref, g_ref, b_ref, r_ref, y_ref, yb_ref):
        gl = _dot(xb_ref[...], wg_ref[...]) + bg_ref[...]
        pe = _dot(p_ref[...], wp_ref[...])
        r = ALPHA * x_ref[...] + _sigmoid(gl) * pe
        xc, rstd = _ln_stats(r)
        y = xc * rstd * g_ref[...] + b_ref[...]
        r_ref[...] = r
        y_ref[...] = y
        yb_ref[...] = y.astype(BF16)

    return _pcall(body, name=name, grid=(t // tm,),
                  in_specs=[_rows(tm, D), _rows(tm, D), _rows(tm, PLE), _layer(l, (D, D)), _fix((PLE, D)),
                            _fix((1, D)), _fix((1, D)), _fix((1, D))],
                  out_specs=[_rows(tm, D)] * 3,
                  out_shape=[jax.ShapeDtypeStruct((t, D), F32), jax.ShapeDtypeStruct((t, D), F32),
                             jax.ShapeDtypeStruct((t, D), BF16)],
                  compiler_params=_cp("parallel"))(xb, x, pb, wgate, wproj, bgate, g, b)


def ple_bwd(dy, r, xb, pb, wgate, l, wproj, bgate, g, *, name):
    t = dy.shape[0]
    tm = min(256, t)

    def body(dy_ref, r_ref, xb_ref, p_ref, wg_ref, wp_ref, bg_ref, g_ref,
             dr_ref, dgl_ref, dpe_ref, dg_ref, db_ref, dbg_ref):
        @pl.when(pl.program_id(0) == 0)
        def _():
            dg_ref[...] = jnp.zeros_like(dg_ref)
            db_ref[...] = jnp.zeros_like(db_ref)
            dbg_ref[...] = jnp.zeros_like(dbg_ref)

        xc, rstd = _ln_stats(r_ref[...])
        xhat = xc * rstd
        d = dy_ref[...]
        dxh = d * g_ref[...]
        dr = rstd * (dxh - jnp.mean(dxh, axis=-1, keepdims=True)
                     - xhat * jnp.mean(dxh * xhat, axis=-1, keepdims=True))
        s = _sigmoid(_dot(xb_ref[...], wg_ref[...]) + bg_ref[...])
        pe = _dot(p_ref[...], wp_ref[...])
        dgl = dr * pe * s * (1.0 - s)
        dr_ref[...] = dr
        dgl_ref[...] = dgl.astype(BF16)
        dpe_ref[...] = (dr * s).astype(BF16)
        dg_ref[...] += jnp.sum(d * xhat, axis=0, keepdims=True)
        db_ref[...] += jnp.sum(d, axis=0, keepdims=True)
        dbg_ref[...] += jnp.sum(dgl, axis=0, keepdims=True)

    vec = jax.ShapeDtypeStruct((1, D), F32)
    return _pcall(body, name=name, grid=(t // tm,),
                  in_specs=[_rows(tm, D), _rows(tm, D), _rows(tm, D), _rows(tm, PLE), _layer(l, (D, D)),
                            _fix((PLE, D)), _fix((1, D)), _fix((1, D))],
                  out_specs=[_rows(tm, D), _rows(tm, D), _rows(tm, D), _fix((1, D)), _fix((1, D)), _fix((1, D))],
                  out_shape=[jax.ShapeDtypeStruct((t, D), F32), jax.ShapeDtypeStruct((t, D), BF16),
                             jax.ShapeDtypeStruct((t, D), BF16), vec, vec, vec],
                  compiler_params=_cp("arbitrary"))(dy, r, xb, pb, wgate, wproj, bgate, g)


def loss_head(y, tgt, *, name):
    t = y.shape[0]
    tm = min(256, t)

    def body(y_ref, t_ref, dy_ref, sq_ref):
        @pl.when(pl.program_id(0) == 0)
        def _():
            sq_ref[...] = jnp.zeros_like(sq_ref)

        e = y_ref[...] - t_ref[...]
        dy_ref[...] = e / float(D)
        sq_ref[...] += jnp.sum(e * e, axis=0, keepdims=True)

    return _pcall(body, name=name, grid=(t // tm,),
                  in_specs=[_rows(tm, D), _rows(tm, D)],
                  out_specs=[_rows(tm, D), _fix((1, D))],
                  out_shape=[jax.ShapeDtypeStruct((t, D), F32), jax.ShapeDtypeStruct((1, D), F32)],
                  compiler_params=_cp("arbitrary"))(y, tgt)


def _lru_gates(xc, wa_ref, wx_ref, ba_ref, bx_ref, lam_ref):
    xcb = xc.astype(BF16)
    r = _sigmoid(_dot(xcb, wa_ref[...]) + ba_ref[...])
    ig = _sigmoid(_dot(xcb, wx_ref[...]) + bx_ref[...])
    sp = _softplus(-lam_ref[...])
    la = -LRU_C * r * sp
    a = jnp.exp(la)
    mult = jnp.sqrt(-_expm1(2.0 * la))
    return r, ig, sp, la, a, mult


def lru_fwd(z, cw, cb, wa, wx, ba, bx, lam, *, name):
    t = z.shape[0]
    tm = min(256, t)
    hb = tm // 8

    def body(ax_ref, prev_ref, ay_ref, cw_ref, cb_ref, wa_ref, wx_ref, ba_ref, bx_ref, lam_ref,
             xc_ref, xcb_ref, h_ref, ya_ref, xs, a_s, b_s, hc):
        i = pl.program_id(0)

        @pl.when(i == 0)
        def _():
            hc[...] = jnp.zeros_like(hc)

        xs[0:8, :] = jnp.where(i == 0, 0.0, prev_ref[...])
        xs[8:, :] = ax_ref[...]
        xc = cb_ref[...] + cw_ref[0:1, :] * xs[5:5 + tm, :]
        for k in range(1, 4):
            xc = xc + cw_ref[k:k + 1, :] * xs[5 + k:5 + k + tm, :]
        r, ig, sp, la, a, mult = _lru_gates(xc, wa_ref, wx_ref, ba_ref, bx_ref, lam_ref)
        a_s[...] = a
        b_s[...] = mult * (ig * xc)
        xc_ref[...] = xc
        xcb_ref[...] = xc.astype(BF16)

        def step(g, h):
            base = pl.multiple_of(g * 8, 8)
            a8 = a_s[pl.ds(base, 8), :]
            b8 = b_s[pl.ds(base, 8), :]
            for j in range(8):
                h = a8[j:j + 1, :] * h + b8[j:j + 1, :]
                h_ref[pl.ds(base + j, 1), :] = h
            return h

        hc[...] = lax.fori_loop(0, tm // 8, step, hc[...])
        ya_ref[...] = (_gelu(ay_ref[...]) * h_ref[...]).astype(BF16)

    vec = _fix((1, BW))
    return _pcall(body, name=name, grid=(t // tm,),
                  in_specs=[_rows(tm, BW, AX // BW),
                            pl.BlockSpec((8, BW), lambda i: (jnp.maximum(i * hb - 1, 0), AX // BW)),
                            _rows(tm, BW, AY // BW), _fix((4, BW)), vec, _fix((BW, BW)), _fix((BW, BW)),
                            vec, vec, vec],
                  out_specs=[_rows(tm, BW)] * 4,
                  out_shape=[jax.ShapeDtypeStruct((t, BW), F32), jax.ShapeDtypeStruct((t, BW), BF16),
                             jax.ShapeDtypeStruct((t, BW), F32), jax.ShapeDtypeStruct((t, BW), BF16)],
                  scratch_shapes=[pltpu.VMEM((tm + 8, BW), F32), pltpu.VMEM((tm, BW), F32),
                                  pltpu.VMEM((tm, BW), F32), pltpu.VMEM((1, BW), F32)],
                  compiler_params=_cp("arbitrary"))(z, z, z, cw, cb, wa, wx, ba, bx, lam)


def lru_bwd(dya, z, h, xc, wa, wx, ba, bx, lam, *, name):
    t = dya.shape[0]
    tm = min(256, t)
    nb = t // tm
    hb = tm // 8

    def body(dya_ref, ay_ref, h_ref, hprev_ref, xc_ref, wa_ref, wx_ref, ba_ref, bx_ref,
             lam_ref, day_ref, dxc_ref, dpr_ref, dpi_ref, dba_ref, dbx_ref, dlam_ref,
             hs, a_s, g_s, d_s, cc):
        i = pl.program_id(0)

        @pl.when(i == 0)
        def _():
            cc[...] = jnp.zeros_like(cc)
            dba_ref[...] = jnp.zeros_like(dba_ref)
            dbx_ref[...] = jnp.zeros_like(dbx_ref)
            dlam_ref[...] = jnp.zeros_like(dlam_ref)

        xc = xc_ref[...]
        r, ig, sp, la, a, mult = _lru_gates(xc, wa_ref, wx_ref, ba_ref, bx_ref, lam_ref)
        ay = ay_ref[...]
        dya = dya_ref[...]
        hcur = h_ref[...]
        day_ref[...] = (dya * hcur * _gelu_grad(ay)).astype(BF16)
        a_s[...] = a
        g_s[...] = dya * _gelu(ay)

        def step(gg, cin):
            g = tm // 8 - 1 - gg
            base = pl.multiple_of(g * 8, 8)
            a8 = a_s[pl.ds(base, 8), :]
            g8 = g_s[pl.ds(base, 8), :]
            for j in range(7, -1, -1):
                d = g8[j:j + 1, :] + cin
                d_s[pl.ds(base + j, 1), :] = d
                cin = a8[j:j + 1, :] * d
            return cin

        cc[...] = lax.fori_loop(0, tm // 8, step, cc[...])
        dht = d_s[...]
        hs[0:8, :] = jnp.where(i == nb - 1, 0.0, hprev_ref[...])
        hs[8:, :] = hcur
        da = dht * hs[7:7 + tm, :]
        dmult = dht * ig * xc
        dig = dht * mult * xc
        dla = da * a - dmult * a * a / mult
        dpr = dla * (-LRU_C * sp) * r * (1.0 - r)
        dpi = dig * ig * (1.0 - ig)
        dprb = dpr.astype(BF16)
        dpib = dpi.astype(BF16)
        dxc_ref[...] = dht * mult * ig + _dot_nt(dprb, wa_ref[...]) + _dot_nt(dpib, wx_ref[...])
        dpr_ref[...] = dprb
        dpi_ref[...] = dpib
        dba_ref[...] += jnp.sum(dpr, axis=0, keepdims=True)
        dbx_ref[...] += jnp.sum(dpi, axis=0, keepdims=True)
        dlam_ref[...] += jnp.sum(dla * (-LRU_C * r), axis=0, keepdims=True) * (-_sigmoid(-lam_ref[...]))

    vec = _fix((1, BW))
    mat = _fix((BW, BW))
    rev = lambda col: pl.BlockSpec((tm, BW), lambda i: (nb - 1 - i, col))
    vshape = jax.ShapeDtypeStruct((1, BW), F32)
    return _pcall(body, name=name, grid=(nb,),
                  in_specs=[rev(0), rev(AY // BW), rev(0),
                            pl.BlockSpec((8, BW), lambda i: (jnp.maximum((nb - 1 - i) * hb - 1, 0), 0)),
                            rev(0), mat, mat, vec, vec, vec],
                  out_specs=[rev(0), rev(0), rev(0), rev(0), vec, vec, vec],
                  out_shape=[jax.ShapeDtypeStruct((t, BW), BF16), jax.ShapeDtypeStruct((t, BW), F32),
                             jax.ShapeDtypeStruct((t, BW), BF16), jax.ShapeDtypeStruct((t, BW), BF16),
                             vshape, vshape, vshape],
                  scratch_shapes=[pltpu.VMEM((tm + 8, BW), F32), pltpu.VMEM((tm, BW), F32),
                                  pltpu.VMEM((tm, BW), F32), pltpu.VMEM((tm, BW), F32),
                                  pltpu.VMEM((1, BW), F32)],
                  compiler_params=_cp("arbitrary"))(dya, z, h, h, xc, wa, wx, ba, bx, lam)


def conv_bwd(dxc, z, cw, *, name):
    t = dxc.shape[0]
    tm = min(256, t)
    nb = t // tm
    hb = tm // 8

    def body(d_ref, dnext_ref, ax_ref, prev_ref, cw_ref, dax_ref, dcw_ref, dcb_ref, ds, xs):
        i = pl.program_id(0)

        @pl.when(i == 0)
        def _():
            dcw_ref[...] = jnp.zeros_like(dcw_ref)
            dcb_ref[...] = jnp.zeros_like(dcb_ref)

        d = d_ref[...]
        ds[0:tm, :] = d
        ds[tm:, :] = jnp.where(i == nb - 1, 0.0, dnext_ref[...])
        xs[0:8, :] = jnp.where(i == 0, 0.0, prev_ref[...])
        xs[8:, :] = ax_ref[...]
        dax = cw_ref[3:4, :] * d
        for k in range(3):
            dax = dax + cw_ref[k:k + 1, :] * ds[3 - k:3 - k + tm, :]
        dax_ref[...] = dax.astype(BF16)
        for k in range(4):
            dcw_ref[k:k + 1, :] += jnp.sum(d * xs[5 + k:5 + k + tm, :], axis=0, keepdims=True)
        dcb_ref[...] += jnp.sum(d, axis=0, keepdims=True)

    return _pcall(body, name=name, grid=(nb,),
                  in_specs=[_rows(tm, BW),
                            pl.BlockSpec((8, BW), lambda i: (jnp.minimum((i + 1) * hb, nb * hb - 1), 0)),
                            _rows(tm, BW, AX // BW),
                            pl.BlockSpec((8, BW), lambda i: (jnp.maximum(i * hb - 1, 0), AX // BW)),
                            _fix((4, BW))],
                  out_specs=[_rows(tm, BW), _fix((4, BW)), _fix((1, BW))],
                  out_shape=[jax.ShapeDtypeStruct((t, BW), BF16), jax.ShapeDtypeStruct((4, BW), F32),
                             jax.ShapeDtypeStruct((1, BW), F32)],
                  scratch_shapes=[pltpu.VMEM((tm + 8, BW), F32), pltpu.VMEM((tm + 8, BW), F32)],
                  compiler_params=_cp("arbitrary"))(dxc, dxc, z, z, cw)


GLA_CB = 4


def _gla_consts():
    tri = (jnp.arange(CHUNK)[:, None] >= jnp.arange(CHUNK)[None, :]).astype(F32)
    mask = ((jnp.arange(BW)[:, None] // 128) == (jnp.arange(256)[None, :] // 64)).astype(F32)
    return tri, mask


def gla_fwd(z, zb, wg2p, bg, ng, *, name):
    t = z.shape[0]
    tm = GLA_CB * CHUNK
    nc = t // CHUNK
    tri, mask = _gla_consts()

    def body(q_ref, k_ref, v_ref, misc_ref, br_ref, w_ref, bg_ref, ng_ref, tri_ref, mask_ref,
             yb_ref, st_ref, st):
        @pl.when(pl.program_id(0) == 0)
        def _():
            st[...] = jnp.zeros_like(st)

        for c in range(GLA_CB):
            rows = slice(c * CHUNK, (c + 1) * CHUNK)
            pre = _dot(misc_ref[rows, :], w_ref[...]) + bg_ref[...]
            la = _log_sigmoid(pre) / GLA_TAU
            gc = _dot_hi(tri_ref[...], la)
            gt = gc[CHUNK - 1:CHUNK, :]
            kdec = k_ref[rows, :] * jnp.exp(gt - gc)
            delta = _dot_tn(v_ref[rows, :], kdec.astype(BF16))
            s_new = st[...] * jnp.exp(gt) + delta * mask_ref[...]
            st[...] = s_new
            st_ref[c] = s_new
            o = _dot_nt(q_ref[rows, :], s_new.astype(BF16)) * (64.0 ** -0.5)
            br = br_ref[rows, :]
            for hd in range(4):
                cols = slice(hd * 128, (hd + 1) * 128)
                oh = o[:, cols]
                rs = lax.rsqrt(jnp.mean(oh * oh, axis=-1, keepdims=True) + RMS_EPS)
                brh = br[:, cols]
                yb_ref[rows, cols] = (oh * rs * ng_ref[:, cols] * (brh * _sigmoid(brh))).astype(BF16)

    return _pcall(body, name=name, grid=(t // tm,),
                  in_specs=[_rows(tm, 256, BQ // 256), _rows(tm, 256, BK // 256), _rows(tm, BW, BV // BW),
                            _rows(tm, 128, MISC // 128), _rows(tm, BW, BR // BW), _fix((128, 256)),
                            _fix((1, 256)), _fix((1, BW)), _fix((CHUNK, CHUNK)), _fix((BW, 256))],
                  out_specs=[_rows(tm, BW), pl.BlockSpec((GLA_CB, BW, 256), lambda i: (i, 0, 0))],
                  out_shape=[jax.ShapeDtypeStruct((t, BW), BF16), jax.ShapeDtypeStruct((nc, BW, 256), F32)],
                  scratch_shapes=[pltpu.VMEM((BW, 256), F32)],
                  compiler_params=_cp("arbitrary"))(zb, z, zb, zb, z, wg2p, bg, ng, tri, mask)


def gla_bwd(dyb, z, zb, states, wg2p, bg, ng, *, name):
    t = z.shape[0]
    tm = GLA_CB * CHUNK
    nb = t // tm
    tri, mask = _gla_consts()
    triu = tri.T

    def body(dy_ref, q_ref, k_ref, v_ref, misc_ref, br_ref, st_ref, sp_ref, w_ref, bg_ref, ng_ref,
             tri_ref, triu_ref, mask_ref,
             dq_ref, dk_ref, dv_ref, dbr_ref, dmisc_ref, dpre_ref, dbg_ref, dng_ref, cc):
        i = pl.program_id(0)

        @pl.when(i == 0)
        def _():
            cc[...] = jnp.zeros_like(cc)
            dbg_ref[...] = jnp.zeros_like(dbg_ref)
            dng_ref[...] = jnp.zeros_like(dng_ref)

        last_row = lax.broadcasted_iota(jnp.int32, (CHUNK, 256), 0) == CHUNK - 1
        for c in range(GLA_CB - 1, -1, -1):
            rows = slice(c * CHUNK, (c + 1) * CHUNK)
            pre = _dot(misc_ref[rows, :], w_ref[...]) + bg_ref[...]
            la = _log_sigmoid(pre) / GLA_TAU
            gc = _dot_hi(tri_ref[...], la)
            gt = gc[CHUNK - 1:CHUNK, :]
            eg = jnp.exp(gt - gc)
            kdec = k_ref[rows, :] * eg
            e = jnp.exp(gt)
            s_n = st_ref[c]
            if c > 0:
                s_prev = st_ref[c - 1]
            else:
                s_prev = jnp.where(i == nb - 1, 0.0, sp_ref[0])
            sb = s_n.astype(BF16)
            qb = q_ref[rows, :]
            o = _dot_nt(qb, sb) * (64.0 ** -0.5)
            br = br_ref[rows, :]
            dy = dy_ref[rows, :]
            do_parts = []
            for hd in range(4):
                cols = slice(hd * 128, (hd + 1) * 128)
                oh = o[:, cols]
                rs = lax.rsqrt(jnp.mean(oh * oh, axis=-1, keepdims=True) + RMS_EPS)
                ohat = oh * rs
                brh = br[:, cols]
                sg = _sigmoid(brh)
                dyh = dy[:, cols]
                ngh = ng_ref[:, cols]
                don = dyh * (brh * sg)
                dbr_ref[rows, cols] = (dyh * (ohat * ngh) * sg * (1.0 + brh * (1.0 - sg))).astype(BF16)
                dng_ref[:, cols] += jnp.sum(don * ohat, axis=0, keepdims=True)
                doh = don * ngh
                do_parts.append(rs * (doh - ohat * jnp.mean(doh * ohat, axis=-1, keepdims=True)))
            dob = jnp.concatenate(do_parts, axis=1).astype(BF16)
            dq_ref[rows, :] = (_dot(dob, sb) * (64.0 ** -0.5)).astype(BF16)
            dst = cc[...] + _dot_tn(dob, qb) * (64.0 ** -0.5) * mask_ref[...]
            dsb = dst.astype(BF16)
            dkdec = _dot(v_ref[rows, :], dsb)
            dv_ref[rows, :] = _dot_nt(kdec.astype(BF16), dsb).astype(BF16)
            dgt = jnp.sum(dst * s_prev, axis=0, keepdims=True) * e
            dk_ref[rows, :] = (dkdec * eg).astype(BF16)
            dd = dkdec * kdec
            dgt = dgt + jnp.sum(dd, axis=0, keepdims=True)
            dgc = jnp.where(last_row, dgt - dd, -dd)
            dla = _dot_hi(triu_ref[...], dgc)
            dpre = dla * (1.0 / GLA_TAU) * _sigmoid(-pre)
            dpb = dpre.astype(BF16)
            dpre_ref[rows, :] = dpb
            dmisc_ref[rows, :] = _dot_nt(dpb, w_ref[...])
            dbg_ref[...] += jnp.sum(dpre, axis=0, keepdims=True)
            cc[...] = dst * e

    rev = lambda w, col: pl.BlockSpec((tm, w), lambda i: (nb - 1 - i, col))
    return _pcall(body, name=name, grid=(nb,),
                  in_specs=[rev(BW, 0), rev(256, BQ // 256), rev(256, BK // 256), rev(BW, BV // BW),
                            rev(128, MISC // 128), rev(BW, BR // BW),
                            pl.BlockSpec((GLA_CB, BW, 256), lambda i: (nb - 1 - i, 0, 0)),
                            pl.BlockSpec((1, BW, 256), lambda i: (jnp.maximum((nb - 1 - i) * GLA_CB - 1, 0), 0, 0)),
                            _fix((128, 256)), _fix((1, 256)), _fix((1, BW)),
                            _fix((CHUNK, CHUNK)), _fix((CHUNK, CHUNK)), _fix((BW, 256))],
                  out_specs=[rev(256, 0), rev(256, 0), rev(BW, 0), rev(BW, 0), rev(128, 0), rev(256, 0),
                             _fix((1, 256)), _fix((1, BW))],
                  out_shape=[jax.ShapeDtypeStruct((t, 256), BF16), jax.ShapeDtypeStruct((t, 256), BF16),
                             jax.ShapeDtypeStruct((t, BW), BF16), jax.ShapeDtypeStruct((t, BW), BF16),
                             jax.ShapeDtypeStruct((t, 128), F32), jax.ShapeDtypeStruct((t, 256), BF16),
                             jax.ShapeDtypeStruct((1, 256), F32), jax.ShapeDtypeStruct((1, BW), F32)],
                  scratch_shapes=[pltpu.VMEM((BW, 256), F32)],
                  compiler_params=_cp("arbitrary"))(dyb, zb, z, zb, zb, z, states, states, wg2p, bg, ng,
                                                    tri, triu, mask)


FOX_SCALE = 64.0 ** -0.5
NEG = -1e30


def fox_fcum(z, bfp, *, name):
    t = z.shape[0]
    tm = min(256, t)
    tri = (jnp.arange(tm)[:, None] >= jnp.arange(tm)[None, :]).astype(F32)

    def body(m_ref, b_ref, tri_ref, o_ref, cc):
        @pl.when(pl.program_id(0) == 0)
        def _():
            cc[...] = jnp.zeros_like(cc)

        lf = _log_sigmoid(m_ref[...] + b_ref[...])
        cs = _dot_hi(tri_ref[...], lf) + cc[...]
        o_ref[...] = cs
        cc[...] = cs[tm - 1:tm, :]

    return _pcall(body, name=name, grid=(t // tm,),
                  in_specs=[_rows(tm, 128, MISC // 128), _fix((1, 128)), _fix((tm, tm))],
                  out_specs=_rows(tm, 128), out_shape=jax.ShapeDtypeStruct((t, 128), F32),
                  scratch_shapes=[pltpu.VMEM((1, 128), F32)],
                  compiler_params=_cp("arbitrary"))(z, bfp, tri)


def fox_dcf(dfc, z, bfp, dmisc_g, *, name):
    t = z.shape[0]
    tm = min(256, t)
    nb = t // tm
    triu = (jnp.arange(tm)[:, None] <= jnp.arange(tm)[None, :]).astype(F32)

    def body(d_ref, m_ref, b_ref, g_ref, tri_ref, o_ref, dbf_ref, cc):
        @pl.when(pl.program_id(0) == 0)
        def _():
            cc[...] = jnp.zeros_like(cc)
            dbf_ref[...] = jnp.zeros_like(dbf_ref)

        rc = _dot_hi(tri_ref[...], d_ref[...]) + cc[...]
        cc[...] = rc[0:1, :]
        dcf = rc * _sigmoid(-(m_ref[...] + b_ref[...]))
        o_ref[...] = (dcf + g_ref[...]).astype(BF16)
        dbf_ref[...] += jnp.sum(dcf, axis=0, keepdims=True)

    rev = lambda col: pl.BlockSpec((tm, 128), lambda i: (nb - 1 - i, col))
    return _pcall(body, name=name, grid=(nb,),
                  in_specs=[rev(0), rev(MISC // 128), _fix((1, 128)), rev(0), _fix((tm, tm))],
                  out_specs=[rev(0), _fix((1, 128))],
                  out_shape=[jax.ShapeDtypeStruct((t, 128), BF16), jax.ShapeDtypeStruct((1, 128), F32)],
                  scratch_shapes=[pltpu.VMEM((1, 128), F32)],
                  compiler_params=_cp("arbitrary"))(dfc, z, bfp, dmisc_g, triu)


def fox_fwd(zb, fq, fk, *, name):
    t = zb.shape[0]
    tq = min(512, t)
    nq = t // tq

    def body(q_ref, k_ref, v_ref, fq_ref, fk_ref, y_ref, yf_ref, lse_ref, m_s, l_s, acc):
        i, j = pl.program_id(1), pl.program_id(2)

        @pl.when(j == 0)
        def _():
            m_s[...] = jnp.full_like(m_s, NEG)
            l_s[...] = jnp.zeros_like(l_s)
            acc[...] = jnp.zeros_like(acc)

        lo = lax.broadcasted_iota(jnp.int32, (tq, 128), 1) < 64

        def work(diagonal):
            q = q_ref[...]
            k = k_ref[...]
            v = v_ref[...]
            if diagonal:
                row = lax.broadcasted_iota(jnp.int32, (tq, tq), 0)
                col = lax.broadcasted_iota(jnp.int32, (tq, tq), 1)
                keep = col <= row
            upd = []
            for hh in range(2):
                sel = lo if hh == 0 else jnp.logical_not(lo)
                qh = jnp.where(sel, q, jnp.zeros_like(q))
                s = _dot_nt(qh, k) * FOX_SCALE + fq_ref[hh] - fk_ref[hh]
                if diagonal:
                    s = jnp.where(keep, s, NEG)
                m_old = m_s[hh]
                m_new = jnp.maximum(m_old, jnp.max(s, axis=-1, keepdims=True))
                p = jnp.exp(s - m_new)
                corr = jnp.exp(m_old - m_new)
                l_s[hh] = l_s[hh] * corr + jnp.sum(p, axis=-1, keepdims=True)
                m_s[hh] = m_new
                upd.append(acc[...] * corr + _dot(p.astype(BF16), v))
            acc[...] = jnp.where(lo, upd[0], upd[1])

        @pl.when(j < i)
        def _():
            work(False)

        @pl.when(j == i)
        def _():
            work(True)

        @pl.when(j == nq - 1)
        def _():
            out = acc[...] * jnp.where(lo, 1.0 / l_s[0], 1.0 / l_s[1])
            y_ref[...] = out.astype(BF16)
            yf_ref[...] = out
            lse_ref[...] = m_s[...] + jnp.log(l_s[...])

    kv = lambda off: pl.BlockSpec((tq, 128), lambda h, i, j: (jnp.minimum(j, i), off // 128 + h))
    return _pcall(body, name=name, grid=(4, nq, nq),
                  in_specs=[pl.BlockSpec((tq, 128), lambda h, i, j: (i, CQ // 128 + h)), kv(CK), kv(CV),
                            pl.BlockSpec((2, tq, 1), lambda h, i, j: (h, i, 0)),
                            pl.BlockSpec((2, 1, tq), lambda h, i, j: (h, 0, jnp.minimum(j, i)))],
                  out_specs=[pl.BlockSpec((tq, 128), lambda h, i, j: (i, h)),
                             pl.BlockSpec((tq, 128), lambda h, i, j: (i, h)),
                             pl.BlockSpec((2, tq, 1), lambda h, i, j: (h, i, 0))],
                  out_shape=[jax.ShapeDtypeStruct((t, BW), BF16), jax.ShapeDtypeStruct((t, BW), F32),
                             jax.ShapeDtypeStruct((FOX_H, t, 1), F32)],
                  scratch_shapes=[pltpu.VMEM((2, tq, 1), F32), pltpu.VMEM((2, tq, 1), F32),
                                  pltpu.VMEM((tq, 128), F32)],
                  compiler_params=_cp("parallel", "parallel", "arbitrary"))(zb, zb, zb, fq, fk)


def fox_delta(dyc, ycf, *, name):
    t = dyc.shape[0]
    tm = min(256, t)
    seg = ((jnp.arange(BW)[:, None] // 64) == jnp.arange(128)[None, :]).astype(F32)

    def body(d_ref, o_ref, s_ref, out_ref):
        out_ref[...] = _dot_hi(d_ref[...] * o_ref[...], s_ref[...])

    return _pcall(body, name=name, grid=(t // tm,),
                  in_specs=[_rows(tm, BW), _rows(tm, BW), _fix((BW, 128))],
                  out_specs=_rows(tm, 128), out_shape=jax.ShapeDtypeStruct((t, 128), F32),
                  compiler_params=_cp("parallel"))(dyc, ycf, seg)


def fox_bwd(zb, dyc, fq, fk, lse, dl, *, name):
    t = zb.shape[0]
    tq = min(512, t)
    nq = t // tq

    def body(q_ref, k_ref, v_ref, do_ref, fq_ref, fk_ref, lse_ref, dl_ref,
             dq_ref, dk_ref, dv_ref, dfk_ref, dfq_ref, dk_s, dv_s, df_s):
        j, i = pl.program_id(1), pl.program_id(2)

        @pl.when(jnp.logical_and(j == 0, i == 0))
        def _():
            dq_ref[...] = jnp.zeros_like(dq_ref)
            dfq_ref[...] = jnp.zeros_like(dfq_ref)

        @pl.when(i == 0)
        def _():
            dk_s[...] = jnp.zeros_like(dk_s)
            dv_s[...] = jnp.zeros_like(dv_s)
            df_s[...] = jnp.zeros_like(df_s)

        lo = lax.broadcasted_iota(jnp.int32, (tq, 128), 1) < 64

        def work(diagonal):
            q = q_ref[...]
            k = k_ref[...]
            v = v_ref[...]
            dob = do_ref[...].astype(BF16)
            if diagonal:
                row = lax.broadcasted_iota(jnp.int32, (tq, tq), 0)
                col = lax.broadcasted_iota(jnp.int32, (tq, tq), 1)
                keep = col <= row
            dvs, dks, dqs, rsum = [], [], [], []
            for hh in range(2):
                sel = lo if hh == 0 else jnp.logical_not(lo)
                qh = jnp.where(sel, q, jnp.zeros_like(q))
                doh = jnp.where(sel, dob, jnp.zeros_like(dob))
                s = _dot_nt(qh, k) * FOX_SCALE + fq_ref[hh] - fk_ref[hh]
                p = jnp.exp(s - lse_ref[hh])
                if diagonal:
                    p = jnp.where(keep, p, 0.0)
                dp = _dot_nt(doh, v)
                ds = p * (dp - dl_ref[hh])
                dsb = (ds * FOX_SCALE).astype(BF16)
                dvs.append(_dot_tn(p.astype(BF16), dob))
                dks.append(_dot_tn(dsb, q))
                dqs.append(_dot(dsb, k))
                df_s[hh] += jnp.sum(ds, axis=0, keepdims=True)
                rsum.append(jnp.sum(ds, axis=-1, keepdims=True))
            dv_s[...] += jnp.where(lo, dvs[0], dvs[1])
            dk_s[...] += jnp.where(lo, dks[0], dks[1])
            r0 = pl.multiple_of(i * tq, tq)
            dq_ref[pl.ds(r0, tq), :] += jnp.where(lo, dqs[0], dqs[1])
            dfq_ref[pl.ds(r0, tq), :] += jnp.where(lo, rsum[0], rsum[1])

        @pl.when(i > j)
        def _():
            work(False)

        @pl.when(i == j)
        def _():
            work(True)

        @pl.when(i == nq - 1)
        def _():
            dk_ref[...] = dk_s[...].astype(BF16)
            dv_ref[...] = dv_s[...].astype(BF16)
            dfk_ref[...] = -df_s[...]

    qi = lambda j, i: jnp.maximum(i, j)
    return _pcall(body, name=name, grid=(4, nq, nq),
                  in_specs=[pl.BlockSpec((tq, 128), lambda h, j, i: (qi(j, i), CQ // 128 + h)),
                            pl.BlockSpec((tq, 128), lambda h, j, i: (j, CK // 128 + h)),
                            pl.BlockSpec((tq, 128), lambda h, j, i: (j, CV // 128 + h)),
                            pl.BlockSpec((tq, 128), lambda h, j, i: (qi(j, i), h)),
                            pl.BlockSpec((2, tq, 1), lambda h, j, i: (h, qi(j, i), 0)),
                            pl.BlockSpec((2, 1, tq), lambda h, j, i: (h, 0, j)),
                            pl.BlockSpec((2, tq, 1), lambda h, j, i: (h, qi(j, i), 0)),
                            pl.BlockSpec((2, tq, 1), lambda h, j, i: (h, qi(j, i), 0))],
                  out_specs=[pl.BlockSpec((t, 128), lambda h, j, i: (0, h)),
                             pl.BlockSpec((tq, 128), lambda h, j, i: (j, h)),
                             pl.BlockSpec((tq, 128), lambda h, j, i: (j, h)),
                             pl.BlockSpec((2, 1, tq), lambda h, j, i: (h, 0, j)),
                             pl.BlockSpec((t, 128), lambda h, j, i: (0, h))],
                  out_shape=[jax.ShapeDtypeStruct((t, BW), F32), jax.ShapeDtypeStruct((t, BW), BF16),
                             jax.ShapeDtypeStruct((t, BW), BF16), jax.ShapeDtypeStruct((FOX_H, 1, t), F32),
                             jax.ShapeDtypeStruct((t, BW), F32)],
                  scratch_shapes=[pltpu.VMEM((tq, 128), F32), pltpu.VMEM((tq, 128), F32),
                                  pltpu.VMEM((2, 1, tq), F32)],
                  compiler_params=_cp("parallel", "arbitrary", "arbitrary"))(zb, zb, zb, dyc, fq, fk, lse, dl)


def fox_fwd_t(zb, frow, fkb, *, name):
    t = zb.shape[0]
    tq = min(512, t)
    nq = t // tq
    rep = tq // 128

    def body(q_ref, k_ref, v_ref, fq_ref, fk_ref, y_ref, yf_ref, lse_ref, m_s, l_s, acc):
        i, j = pl.program_id(1), pl.program_id(2)

        @pl.when(j == 0)
        def _():
            m_s[...] = jnp.full_like(m_s, NEG)
            l_s[...] = jnp.zeros_like(l_s)
            acc[...] = jnp.zeros_like(acc)

        lo = lax.broadcasted_iota(jnp.int32, (tq, 128), 1) < 64

        def work(diagonal):
            q = q_ref[...]
            k = k_ref[...]
            v = v_ref[...]
            if diagonal:
                key = lax.broadcasted_iota(jnp.int32, (tq, tq), 0)
                qry = lax.broadcasted_iota(jnp.int32, (tq, tq), 1)
                keep = key <= qry
            for hh in range(2):
                sel = lo if hh == 0 else jnp.logical_not(lo)
                qh = jnp.where(sel, q, jnp.zeros_like(q))
                s = _dot_nt(k, qh) * FOX_SCALE + fq_ref[hh] - jnp.tile(fk_ref[hh], (1, rep))
                if diagonal:
                    s = jnp.where(keep, s, NEG)
                m_old = m_s[hh]
                m_new = jnp.maximum(m_old, jnp.max(s, axis=0, keepdims=True))
                p = jnp.exp(s - m_new)
                corr = jnp.exp(m_old - m_new)
                l_s[hh] = l_s[hh] * corr + jnp.sum(p, axis=0, keepdims=True)
                m_s[hh] = m_new
                pv = _dot_tn(v, p.astype(BF16))
                rows = slice(64 * hh, 64 * hh + 64)
                acc[rows, :] = acc[rows, :] * corr + pv[rows, :]

        @pl.when(j < i)
        def _():
            work(False)

        @pl.when(j == i)
        def _():
            work(True)

        @pl.when(j == nq - 1)
        def _():
            first = lax.broadcasted_iota(jnp.int32, (128, tq), 0) < 64
            out = (acc[...] * jnp.where(first, 1.0 / l_s[0], 1.0 / l_s[1])).T
            y_ref[...] = out.astype(BF16)
            yf_ref[...] = out
            lse_ref[...] = m_s[...] + jnp.log(l_s[...])

    kv = lambda off: pl.BlockSpec((tq, 128), lambda h, i, j: (jnp.minimum(j, i), off // 128 + h))
    return _pcall(body, name=name, grid=(4, nq, nq),
                  in_specs=[pl.BlockSpec((tq, 128), lambda h, i, j: (i, CQ // 128 + h)), kv(CK), kv(CV),
                            pl.BlockSpec((2, 1, tq), lambda h, i, j: (h, 0, i)),
                            pl.BlockSpec((2, tq, 128), lambda h, i, j: (h, jnp.minimum(j, i), 0))],
                  out_specs=[pl.BlockSpec((tq, 128), lambda h, i, j: (i, h)),
                             pl.BlockSpec((tq, 128), lambda h, i, j: (i, h)),
                             pl.BlockSpec((2, 1, tq), lambda h, i, j: (h, 0, i))],
                  out_shape=[jax.ShapeDtypeStruct((t, BW), BF16), jax.ShapeDtypeStruct((t, BW), F32),
                             jax.ShapeDtypeStruct((FOX_H, 1, t), F32)],
                  scratch_shapes=[pltpu.VMEM((2, 1, tq), F32), pltpu.VMEM((2, 1, tq), F32),
                                  pltpu.VMEM((128, tq), F32)],
                  compiler_params=_cp("parallel", "parallel", "arbitrary"))(zb, zb, zb, frow, fkb)


def fox_bwd_t(zb, dyc, frow, fkb, lse, dl, *, name):
    t = zb.shape[0]
    tq = min(512, t)
    nq = t // tq
    rep = tq // 128

    def body(q_ref, k_ref, v_ref, do_ref, fq_ref, fk_ref, lse_ref, dl_ref,
             dq_ref, dk_ref, dv_ref, dfk_ref, dfq_ref, dk_s, dv_s, df_s):
        j, i = pl.program_id(1), pl.program_id(2)

        @pl.when(jnp.logical_and(j == 0, i == 0))
        def _():
            dq_ref[...] = jnp.zeros_like(dq_ref)
            dfq_ref[...] = jnp.zeros_like(dfq_ref)

        @pl.when(i == 0)
        def _():
            dk_s[...] = jnp.zeros_like(dk_s)
            dv_s[...] = jnp.zeros_like(dv_s)
            df_s[...] = jnp.zeros_like(df_s)

        lo = lax.broadcasted_iota(jnp.int32, (tq, 128), 1) < 64

        def work(diagonal):
            q = q_ref[...]
            k = k_ref[...]
            v = v_ref[...]
            dob = do_ref[...].astype(BF16)
            if diagonal:
                key = lax.broadcasted_iota(jnp.int32, (tq, tq), 0)
                qry = lax.broadcasted_iota(jnp.int32, (tq, tq), 1)
                keep = key <= qry
            dvs, dks = [], []
            for hh in range(2):
                sel = lo if hh == 0 else jnp.logical_not(lo)
                qh = jnp.where(sel, q, jnp.zeros_like(q))
                doh = jnp.where(sel, dob, jnp.zeros_like(dob))
                s = _dot_nt(k, qh) * FOX_SCALE + fq_ref[hh] - jnp.tile(fk_ref[hh], (1, rep))
                p = jnp.exp(s - lse_ref[hh])
                if diagonal:
                    p = jnp.where(keep, p, 0.0)
                ds = p * (_dot_nt(v, doh) - dl_ref[hh])
                dsb = (ds * FOX_SCALE).astype(BF16)
                dvs.append(_dot(p.astype(BF16), dob))
                dks.append(_dot(dsb, q))
                rows = slice(64 * hh, 64 * hh + 64)
                dq_ref[i, rows, :] += _dot_tn(k, dsb)[rows, :]
                part = ds[:, 0:128]
                for r in range(1, rep):
                    part = part + ds[:, 128 * r:128 * (r + 1)]
                df_s[hh] += part
                dfq_ref[hh, i] += jnp.sum(ds, axis=0, keepdims=True)
            dv_s[...] += jnp.where(lo, dvs[0], dvs[1])
            dk_s[...] += jnp.where(lo, dks[0], dks[1])

        @pl.when(i > j)
        def _():
            work(False)

        @pl.when(i == j)
        def _():
            work(True)

        @pl.when(i == nq - 1)
        def _():
            dk_ref[...] = dk_s[...].astype(BF16)
            dv_ref[...] = dv_s[...].astype(BF16)
            dfk_ref[...] = -jnp.sum(df_s[...], axis=-1, keepdims=True)

    qi = lambda j, i: jnp.maximum(i, j)
    return _pcall(body, name=name, grid=(4, nq, nq),
                  in_specs=[pl.BlockSpec((tq, 128), lambda h, j, i: (qi(j, i), CQ // 128 + h)),
                            pl.BlockSpec((tq, 128), lambda h, j, i: (j, CK // 128 + h)),
                            pl.BlockSpec((tq, 128), lambda h, j, i: (j, CV // 128 + h)),
                            pl.BlockSpec((tq, 128), lambda h, j, i: (qi(j, i), h)),
                            pl.BlockSpec((2, 1, tq), lambda h, j, i: (h, 0, qi(j, i))),
                            pl.BlockSpec((2, tq, 128), lambda h, j, i: (h, j, 0)),
                            pl.BlockSpec((2, 1, tq), lambda h, j, i: (h, 0, qi(j, i))),
                            pl.BlockSpec((2, 1, tq), lambda h, j, i: (h, 0, qi(j, i)))],
                  out_specs=[pl.BlockSpec((None, nq, 128, tq), lambda h, j, i: (h, 0, 0, 0)),
                             pl.BlockSpec((tq, 128), lambda h, j, i: (j, h)),
                             pl.BlockSpec((tq, 128), lambda h, j, i: (j, h)),
                             pl.BlockSpec((2, tq, 1), lambda h, j, i: (h, j, 0)),
                             pl.BlockSpec((2, nq, 1, tq), lambda h, j, i: (h, 0, 0, 0))],
                  out_shape=[jax.ShapeDtypeStruct((4, nq, 128, tq), F32), jax.ShapeDtypeStruct((t, BW), BF16),
                             jax.ShapeDtypeStruct((t, BW), BF16), jax.ShapeDtypeStruct((FOX_H, t, 1), F32),
                             jax.ShapeDtypeStruct((FOX_H, nq, 1, tq), F32)],
                  scratch_shapes=[pltpu.VMEM((tq, 128), F32), pltpu.VMEM((tq, 128), F32),
                                  pltpu.VMEM((2, tq, 128), F32)],
                  compiler_params=_cp("parallel", "arbitrary", "arbitrary"))(zb, zb, zb, dyc, frow, fkb, lse, dl)


def merge_fwd(ya, yb, yc, wbr, z, *, name):
    t = ya.shape[0]
    tm = min(256, t)

    def body(ya_ref, yb_ref, yc_ref, w_ref, g0_ref, g1_ref, g2_ref, o_ref):
        m = _sigmoid(g0_ref[...]) * _dot(ya_ref[...], w_ref[0])
        m = m + _sigmoid(g1_ref[...]) * _dot(yb_ref[...], w_ref[1])
        m = m + _sigmoid(g2_ref[...]) * _dot(yc_ref[...], w_ref[2])
        o_ref[...] = m.astype(BF16)

    return _pcall(body, name=name, grid=(t // tm,),
                  in_specs=[_rows(tm, BW)] * 3 + [_fix((3, BW, D))]
                  + [_rows(tm, D, G0 // D + j) for j in range(3)],
                  out_specs=_rows(tm, D), out_shape=jax.ShapeDtypeStruct((t, D), BF16),
                  compiler_params=_cp("parallel"))(ya, yb, yc, wbr, z, z, z)


def merge_bwd(doutb, wo, l, ya, yb, yc, wbr, z, *, name):
    t = ya.shape[0]
    tm = min(256, t)

    def body(do_ref, wo_ref, ya_ref, yb_ref, yc_ref, w_ref, g0_ref, g1_ref, g2_ref,
             dya_ref, dyb_ref, dyc_ref, dp0_ref, dp1_ref, dp2_ref, dg0_ref, dg1_ref, dg2_ref):
        dm = _dot_nt(do_ref[...], wo_ref[...])
        ys = (ya_ref, yb_ref, yc_ref)
        gs = (g0_ref, g1_ref, g2_ref)
        dys = (dya_ref, dyb_ref, dyc_ref)
        dps = (dp0_ref, dp1_ref, dp2_ref)
        dgs = (dg0_ref, dg1_ref, dg2_ref)
        for j in range(3):
            s = _sigmoid(gs[j][...])
            pj = _dot(ys[j][...], w_ref[j])
            dpb = (dm * s).astype(BF16)
            dps[j][...] = dpb
            dgs[j][...] = (dm * pj * s * (1.0 - s)).astype(BF16)
            dys[j][...] = _dot_nt(dpb, w_ref[j])

    yshape = jax.ShapeDtypeStruct((t, BW), F32)
    dshape = jax.ShapeDtypeStruct((t, D), BF16)
    return _pcall(body, name=name, grid=(t // tm,),
                  in_specs=[_rows(tm, D), _layer(l, (D, D))] + [_rows(tm, BW)] * 3
                  + [_fix((3, BW, D))] + [_rows(tm, D, G0 // D + j) for j in range(3)],
                  out_specs=[_rows(tm, BW)] * 3 + [_rows(tm, D)] * 6,
                  out_shape=[yshape] * 3 + [dshape] * 6,
                  compiler_params=_cp("parallel"))(doutb, wo, ya, yb, yc, wbr, z, z, z)


def adamw(w, g, m, v, *, name):
    r, c = w.shape
    tm = _pick(r, (128, 64, 32, 16, 8))

    def body(w_ref, g_ref, m_ref, v_ref, d_ref, mo_ref, vo_ref):
        gg = g_ref[...]
        mn = ADAM_B1 * m_ref[...] + (1.0 - ADAM_B1) * gg
        vn = ADAM_B2 * v_ref[...] + (1.0 - ADAM_B2) * (gg * gg)
        m_hat = mn / (1.0 - ADAM_B1 ** ADAM_STEP)
        v_hat = vn / (1.0 - ADAM_B2 ** ADAM_STEP)
        d_ref[...] = -ADAM_LR * (m_hat / (jnp.sqrt(v_hat) + ADAM_EPS) + ADAM_WD * w_ref[...])
        mo_ref[...] = mn
        vo_ref[...] = vn

    shp = jax.ShapeDtypeStruct((r, c), F32)
    return _pcall(body, name=name, grid=(r // tm,), in_specs=[_rows(tm, c)] * 4, out_specs=[_rows(tm, c)] * 3,
                  out_shape=[shp] * 3, compiler_params=_cp("parallel"))(w, g, m, v)


def _place():
    return lax.axis_index("x"), lax.axis_index("y"), lax.axis_index("c")


def _remote(src, dst, send_sems, recv_sems, k, to):
    return pltpu.make_async_remote_copy(src_ref=src, dst_ref=dst, send_sem=send_sems.at[k],
                                        recv_sem=recv_sems.at[k], device_id=to, device_id_type=MESH)


def gather_weights(shards):
    n = len(shards)

    def body(*refs):
        ins, outs = refs[:n], refs[n:2 * n]
        send_sems, recv_sems, own_send, own_recv = refs[2 * n:]
        x, y, c = _place()
        sib = (x, y, 1 - c)
        chips = [(1 - x, y), (x, 1 - y), (1 - x, 1 - y)]
        k_me = 2 * x + y
        mine, first, passed = [], [], []
        for t in range(n):
            for l in range(DEPTH):
                mine.append(_remote(ins[t].at[l], outs[t].at[l, k_me], own_send, own_recv, 2 * t + l, sib))
            for j, chip in enumerate(chips):
                first.append(_remote(ins[t].at[c], outs[t].at[c, k_me], send_sems, recv_sems, 6 * t + j, (*chip, c)))
        for cp in mine + first:
            cp.start()
        for t in range(n):
            for j, chip in enumerate(chips):
                blk = outs[t].at[c, 2 * chip[0] + chip[1]]
                _remote(blk, blk, send_sems, recv_sems, 6 * t + j, (*chip, c)).wait_recv()
                cp = _remote(blk, blk, send_sems, recv_sems, 6 * t + 3 + j, sib)
                cp.start()
                passed.append(cp)
        for t in range(n):
            for j, chip in enumerate(chips):
                blk = outs[t].at[1 - c, 2 * chip[0] + chip[1]]
                _remote(blk, blk, send_sems, recv_sems, 6 * t + 3 + j, sib).wait_recv()
        for cp in first + passed:
            cp.wait_send()
        for cp in mine:
            cp.wait()

    return _pcall(body, name="gather_weights", in_specs=[ANY] * n, out_specs=[ANY] * n,
                  out_shape=[jax.ShapeDtypeStruct((DEPTH, 4) + s.shape[1:], s.dtype) for s in shards],
                  scratch_shapes=[pltpu.SemaphoreType.DMA((6 * n,)), pltpu.SemaphoreType.DMA((6 * n,)),
                                  pltpu.SemaphoreType.DMA((2 * n,)), pltpu.SemaphoreType.DMA((2 * n,))])(*shards)


def pair_exchange(g0, g1):
    n = len(g0)

    def body(*refs):
        a0, a1, outs = refs[:n], refs[n:2 * n], refs[2 * n:3 * n]
        send_sems, recv_sems = refs[3 * n:]
        x, y, c = _place()
        sib = (x, y, 1 - c)

        @pl.when(c == 0)
        def _():
            for t in range(n):
                _remote(a1[t], outs[t], send_sems, recv_sems, t, sib).start()

        @pl.when(c == 1)
        def _():
            for t in range(n):
                _remote(a0[t], outs[t], send_sems, recv_sems, t, sib).start()

        for t in range(n):
            _remote(a0[t], outs[t], send_sems, recv_sems, t, sib).wait()

    return _pcall(body, name="pair_exchange", in_specs=[ANY] * (2 * n), out_specs=[ANY] * n,
                  out_shape=[jax.ShapeDtypeStruct(a.shape, a.dtype) for a in g0],
                  scratch_shapes=[pltpu.SemaphoreType.DMA((n,)), pltpu.SemaphoreType.DMA((n,))])(*g0, *g1)


def chip_exchange(s1):
    n = len(s1)

    def body(*refs):
        ins, outs = refs[:n], refs[n:2 * n]
        send_sems, recv_sems = refs[2 * n:]
        x, y, c = _place()
        chips = [(1 - x, y), (x, 1 - y), (1 - x, 1 - y)]
        cps = [_remote(ins[t].at[2 * chip[0] + chip[1]], outs[t].at[j], send_sems, recv_sems, 3 * t + j, (*chip, c))
               for t in range(n) for j, chip in enumerate(chips)]
        for cp in cps:
            cp.start()
        for cp in cps:
            cp.wait()

    return _pcall(body, name="chip_exchange", in_specs=[ANY] * n, out_specs=[ANY] * n,
                  out_shape=[jax.ShapeDtypeStruct((3,) + a.shape[1:], a.dtype) for a in s1],
                  scratch_shapes=[pltpu.SemaphoreType.DMA((3 * n,)), pltpu.SemaphoreType.DMA((3 * n,))])(*s1)


def pair_share(s2):
    n = len(s2)

    def body(*refs):
        ins, outs = refs[:n], refs[n:2 * n]
        send_sems, recv_sems = refs[2 * n:]
        x, y, c = _place()
        sib = (x, y, 1 - c)
        cps = [_remote(ins[t].at[c], outs[t].at[c], send_sems, recv_sems, t, sib) for t in range(n)]
        for cp in cps:
            cp.start()
        for t in range(n):
            cps[t].wait_send()
            _remote(ins[t].at[c], outs[t].at[1 - c], send_sems, recv_sems, t, sib).wait_recv()

    return _pcall(body, name="pair_share", in_specs=[ANY] * n, out_specs=[ANY] * n,
                  out_shape=[jax.ShapeDtypeStruct(a.shape, a.dtype) for a in s2],
                  input_output_aliases={t: t for t in range(n)},
                  scratch_shapes=[pltpu.SemaphoreType.DMA((n,)), pltpu.SemaphoreType.DMA((n,))])(*s2)


def small_exchange(gs):
    rows, width = gs.shape

    def body(g_ref, o_ref, send_sems, recv_sems):
        x, y, c = _place()
        cps = []
        for r in range(1, 8):
            dx, dy, dc = (r >> 2) & 1, (r >> 1) & 1, r & 1
            to = (x if dx == 0 else 1 - x, y if dy == 0 else 1 - y, c if dc == 0 else 1 - c)
            cps.append(_remote(g_ref, o_ref.at[r - 1], send_sems, recv_sems, r - 1, to))
        for cp in cps:
            cp.start()
        for cp in cps:
            cp.wait()

    return _pcall(body, name="small_exchange", in_specs=[ANY], out_specs=ANY,
                  out_shape=jax.ShapeDtypeStruct((7, rows, width), gs.dtype),
                  scratch_shapes=[pltpu.SemaphoreType.DMA((7,)), pltpu.SemaphoreType.DMA((7,))])(gs)


def _row_tile(rows):
    return _pick(rows, (256, 352, 128, 64, 32, 16))


def pair_add(g0, g1, rb, core, *, name):
    _, rows, width = g0.shape
    tr = _row_tile(rows)

    def body(c_ref, g0_ref, g1_ref, r_ref, o_ref, ob_ref):
        s = jnp.where(c_ref[0] == 0, g0_ref[...], g1_ref[...]) + r_ref[...]
        o_ref[...] = s
        ob_ref[...] = s.astype(BF16)

    blk = pl.BlockSpec((None, tr, width), lambda k, i, c_ref: (k, i, 0))
    gs = pltpu.PrefetchScalarGridSpec(
        num_scalar_prefetch=1, grid=(4, rows // tr),
        in_specs=[pl.BlockSpec((None, tr, width), lambda k, i, c_ref: (k * (1 - c_ref[0]), i * (1 - c_ref[0]), 0)),
                  pl.BlockSpec((None, tr, width), lambda k, i, c_ref: (k * c_ref[0], i * c_ref[0], 0)), blk],
        out_specs=[blk, blk])
    return _pcall(body, name=name, grid_spec=gs,
                  out_shape=[jax.ShapeDtypeStruct(g0.shape, F32), jax.ShapeDtypeStruct(g0.shape, BF16)],
                  compiler_params=_cp("parallel", "parallel"))(core, g0, g1, rb)


def chip_add(s1, rb2, chip, core, *, name):
    _, rows, width = s1.shape
    tr = _row_tile(rows)

    def body(k_ref, c_ref, s_ref, r_ref, o_ref):
        o_ref[...] = ((s_ref[...] + r_ref[0].astype(F32)) + r_ref[1].astype(F32)) + r_ref[2].astype(F32)

    gs = pltpu.PrefetchScalarGridSpec(
        num_scalar_prefetch=2, grid=(rows // tr,),
        in_specs=[pl.BlockSpec((None, tr, width), lambda i, k_ref, c_ref: (k_ref[0], i, 0)),
                  pl.BlockSpec((3, tr, width), lambda i, k_ref, c_ref: (0, i, 0))],
        out_specs=pl.BlockSpec((None, tr, width), lambda i, k_ref, c_ref: (c_ref[0], i, 0)))
    return _pcall(body, name=name, grid_spec=gs, out_shape=jax.ShapeDtypeStruct((DEPTH, rows, width), F32),
                  compiler_params=_cp("parallel"))(chip, core, s1, rb2)


def small_add(gs_own, slots, me):
    rows, width = gs_own.shape
    tr = _pick(rows, (64, 32, 16, 8))

    def body(me_ref, g_ref, s_ref, o_ref):
        me_v = me_ref[0]
        total = None
        for d in range(8):
            rel = jnp.bitwise_xor(me_v, d)
            val = jnp.where(rel == 0, g_ref[...], s_ref[jnp.maximum(rel - 1, 0)])
            total = val if total is None else total + val
        o_ref[...] = total

    gs = pltpu.PrefetchScalarGridSpec(
        num_scalar_prefetch=1, grid=(rows // tr,),
        in_specs=[pl.BlockSpec((tr, width), lambda i, m_ref: (i, 0)),
                  pl.BlockSpec((7, tr, width), lambda i, m_ref: (0, i, 0))],
        out_specs=pl.BlockSpec((tr, width), lambda i, m_ref: (i, 0)))
    return _pcall(body, name="small_add", grid_spec=gs, out_shape=jax.ShapeDtypeStruct((rows, width), F32),
                  compiler_params=_cp("parallel"))(me, gs_own, slots)


SHARDED = (("ffn1_w_up", (D, UPW)), ("ffn1_w_down", (DFF // 4, D)), ("w_in", (D, D_IN // 4)),
           ("conv_w", (4, BW // 4)), ("gla_w_g2", (LOW_W, 64)), ("w_branch", (3 * BW, D // 4)),
           ("w_out", (D // 4, D)), ("ffn2_w_up", (D, UPW)), ("ffn2_w_down", (DFF // 4, D)),
           ("ple_w_proj", (PLE, D // 4)), ("ple_w_gate", (D // 4, D)))
SMALL = ("ln1_g", "ln1_b", "conv_b", "lru_wa", "lru_ba", "lru_wx", "lru_bx", "lru_lambda", "gla_b_g",
         "gla_norm_g", "fox_b_f", "ln2_g", "ln2_b", "ln3_g", "ln3_b", "ple_b_gate", "ln4_g", "ln4_b")
WEIGHTS = ('ffn1_w_up', 'ffn1_w_down', 'ln1_g', 'ln1_b', 'w_in', 'conv_w', 'conv_b', 'lru_wa', 'lru_ba',
           'lru_wx', 'lru_bx', 'lru_lambda', 'gla_w_g2', 'gla_b_g', 'gla_norm_g', 'fox_b_f', 'w_branch',
           'w_out', 'ln2_g', 'ln2_b', 'ffn2_w_up', 'ffn2_w_down', 'ln3_g', 'ln3_b', 'ple_w_proj',
           'ple_w_gate', 'ple_b_gate', 'ln4_g', 'ln4_b')


def _pack(parts, rows):
    flat = jnp.concatenate([p.reshape(-1) for p in parts])
    flat = jnp.pad(flat, (0, rows * PACK_W - flat.shape[0]))
    return flat.reshape(rows, PACK_W)


def _cols_join(parts):
    return jnp.concatenate([parts[k] for k in range(4)], axis=-1)


def _cols_split(full):
    r, c4 = full.shape
    return full.reshape(r, 4, c4 // 4).transpose(1, 0, 2)


def _regroup_in(w):
    pad = jnp.zeros(w.shape[:-1] + (ZW - D_IN,), w.dtype)
    return jnp.concatenate([w[..., 0:2048], w[..., 2064:4112], w[..., 4120:7192], w[..., 2048:2064],
                            w[..., 4112:4120], pad], axis=-1)


def _regroup_out(g):
    return jnp.concatenate([g[..., 0:2048], g[..., 7168:7184], g[..., 2048:4096], g[..., 7184:7192],
                            g[..., 4096:7168]], axis=-1)


def _block_diag(w):
    eye = jnp.eye(8, dtype=w.dtype)
    return (eye[:, None, :, None] * w[:, :, None, :]).reshape(BW, BW)


def _diag_blocks(dense):
    return jnp.stack([dense[64 * n:64 * (n + 1), 64 * n:64 * (n + 1)] for n in range(8)])


def _layer_weights(gw, small, l):
    w = {"up1": gw["ffn1_w_up"], "up2": gw["ffn2_w_up"],
         "dn1": gw["ffn1_w_down"].reshape(DEPTH, DFF, D), "dn2": gw["ffn2_w_down"].reshape(DEPTH, DFF, D),
         "wo": gw["w_out"].reshape(DEPTH, D, D), "wgt": gw["ple_w_gate"].reshape(DEPTH, D, D)}
    w["win"] = _regroup_in(_cols_join(gw["w_in"][l]))
    w["cw"] = _cols_join(gw["conv_w"][l])
    w["wa"] = _block_diag(small["lru_wa"][l]).astype(BF16)
    w["wx"] = _block_diag(small["lru_wx"][l]).astype(BF16)
    w["wg2p"] = jnp.pad(_cols_join(gw["gla_w_g2"][l]), ((0, 128 - LOW_W), (0, 0)))
    w["wbr"] = _cols_join(gw["w_branch"][l].reshape(4, 3, BW, D // 4))
    w["wp"] = _cols_join(gw["ple_w_proj"][l])
    for n in ("ln1_g", "ln1_b", "ln2_g", "ln2_b", "ln3_g", "ln3_b", "ln4_g", "ln4_b", "conv_b", "lru_ba",
              "lru_bx", "lru_lambda", "gla_b_g", "gla_norm_g", "ple_b_gate"):
        w[n] = small[n][l][None, :]
    w["bfp"] = jnp.pad(small["fox_b_f"][l], (LOW_W, 128 - LOW_W - FOX_H))[None, :]
    return w


def _heads_t(a):
    ht = a[:, LOW_W:LOW_W + FOX_H].T
    return ht[:, None, :], jnp.broadcast_to(ht[:, :, None], ht.shape + (128,))


def _layer_fwd(x, xb, pb, w, l):
    s = {"x0": x, "x0b": xb}
    tag = "l%d_" % l
    gate, up, act = ffn_up(xb, w["up1"], l, name=tag + "ffn1_up")
    r1, x1, x1b = matmul_res_ln(act, w["dn1"], l, x, w["ln1_g"], w["ln1_b"], mm_scale=0.5, name=tag + "ffn1_down")
    s.update(gate1=gate, up1=up, act1=act, r1=r1, x1=x1, x1b=x1b)
    z, zb = matmul(x1b, w["win"], also_bf16=True, tn=_pick(ZW, (2432,)), name=tag + "mix_in")
    xc, xcb, h, ya = lru_fwd(z, w["cw"], w["conv_b"], w["wa"], w["wx"], w["lru_ba"], w["lru_bx"],
                             w["lru_lambda"], name=tag + "lru_fwd")
    yb, states = gla_fwd(z, zb, w["wg2p"], w["gla_b_g"], w["gla_norm_g"], name=tag + "gla_fwd")
    fcum = fox_fcum(z, w["bfp"], name=tag + "fox_fcum")
    fq, fk = _heads_t(fcum)
    yc, ycf, lse = fox_fwd_t(zb, fq, fk, name=tag + "fox_fwd")
    merged = merge_fwd(ya, yb, yc, w["wbr"], z, name=tag + "merge_fwd")
    r2, x2, x2b = matmul_res_ln(merged, w["wo"], l, x1, w["ln2_g"], w["ln2_b"], mm_scale=1.0, name=tag + "mix_out")
    s.update(z=z, zb=zb, xc=xc, xcb=xcb, h=h, ya=ya, yb=yb, states=states, fq=fq, fk=fk, yc=yc, ycf=ycf,
             lse=lse, merged=merged, r2=r2, x2=x2, x2b=x2b)
    gate, up, act = ffn_up(x2b, w["up2"], l, name=tag + "ffn2_up")
    r3, x3, x3b = matmul_res_ln(act, w["dn2"], l, x2, w["ln3_g"], w["ln3_b"], mm_scale=0.5, name=tag + "ffn2_down")
    s.update(gate2=gate, up2=up, act2=act, r3=r3, x3=x3, x3b=x3b)
    r4, x4, x4b = ple_fwd(x3b, x3, pb, w["wgt"], l, w["wp"], w["ple_b_gate"], w["ln4_g"], w["ln4_b"],
                          name=tag + "ple_fwd")
    s.update(r4=r4, pb=pb)
    return x4, x4b, s


def _ffn_bwd(dy, s, w, n, xin_b, l, tag):
    k = {"1": ("r1", "ln1_g", "gate1", "up1", "act1"), "2": ("r3", "ln3_g", "gate2", "up2", "act2")}[n]
    dr, dfb, dg, db = ln_bwd(dy, s[k[0]], w[k[1]], out_scale=0.5, name=tag + "ln_bwd")
    dgate, dup = ffn_down_bwd(dfb, w["dn" + n], l, s[k[2]], s[k[3]], name=tag + "down_bwd")
    dx = ffn_dx(dgate, dup, w["up" + n], l, dr, name=tag + "dx")
    dwup = matmul_tn_up(xin_b, dgate, dup, name=tag + "dw_up")
    dwdn = matmul_tn(s[k[4]], dfb, name=tag + "dw_down").reshape(4, DFF // 4, D)
    return dx, dwup, dwdn, dg[0], db[0]


def _layer_bwd(dy, s, w, l):
    g = {}
    tag = "l%d_" % l
    dr4, dglb, dpeb, dg4, db4, dbg = ple_bwd(dy, s["r4"], s["x3b"], s["pb"], w["wgt"], l, w["wp"], w["ple_b_gate"],
                                             w["ln4_g"], name=tag + "ple_bwd")
    dx3 = matmul(dglb, w["wgt"], nt=True, b_lead=(l,), res=dr4, res_scale=ALPHA, tm=1024, tn=1024,
                 name=tag + "ple_dx")
    g["ple_w_gate"] = matmul_tn(s["x3b"], dglb, name=tag + "ple_dw_gate").reshape(4, D // 4, D)
    g["ple_w_proj"] = _cols_split(matmul_tn(s["pb"], dpeb, name=tag + "ple_dw_proj"))
    g["ln4_g"], g["ln4_b"], g["ple_b_gate"] = dg4[0], db4[0], dbg[0]
    dx2, g["ffn2_w_up"], g["ffn2_w_down"], g["ln3_g"], g["ln3_b"] = _ffn_bwd(dx3, s, w, "2", s["x2b"], l,
                                                                             tag + "ffn2_")
    dr2, doutb, dg2, db2 = ln_bwd(dx2, s["r2"], w["ln2_g"], out_scale=1.0, name=tag + "mix_ln_bwd")
    g["ln2_g"], g["ln2_b"] = dg2[0], db2[0]
    g["w_out"] = matmul_tn(s["merged"], doutb, name=tag + "dw_out").reshape(4, D // 4, D)
    z, zb = s["z"], s["zb"]
    (dya, dyb, dyc, dp0, dp1, dp2, dgl0, dgl1, dgl2) = merge_bwd(
        doutb, w["wo"], l, s["ya"], s["yb"], s["yc"], w["wbr"], z, name=tag + "merge_bwd")
    dwbr = jnp.stack([matmul_tn(s["ya"], dp0, name=tag + "dw_br0"), matmul_tn(s["yb"], dp1, name=tag + "dw_br1"),
                      matmul_tn(s["yc"], dp2, name=tag + "dw_br2")])
    g["w_branch"] = _cols_split(dwbr.reshape(3 * BW, D))
    day, dxc, dprb, dpib, dba, dbx, dlam = lru_bwd(dya, z, s["h"], s["xc"], w["wa"], w["wx"],
                                                   w["lru_ba"], w["lru_bx"], w["lru_lambda"], name=tag + "lru_bwd")
    dax, dcw, dcb = conv_bwd(dxc, z, w["cw"], name=tag + "conv_bwd")
    g["lru_wa"] = _diag_blocks(matmul_tn(s["xcb"], dprb, name=tag + "dw_lru_a"))
    g["lru_wx"] = _diag_blocks(matmul_tn(s["xcb"], dpib, name=tag + "dw_lru_x"))
    g["lru_ba"], g["lru_bx"], g["lru_lambda"] = dba[0], dbx[0], dlam[0]
    g["conv_w"], g["conv_b"] = _cols_split(dcw), dcb[0]
    dbq, dbk, dbv, dbr, dmisc_g, dpreb, dbgg, dng = gla_bwd(dyb, z, zb, s["states"], w["wg2p"], w["gla_b_g"],
                                                            w["gla_norm_g"], name=tag + "gla_bwd")
    miscb = zb[:, MISC:]
    g["gla_w_g2"] = _cols_split(matmul_tn(miscb, dpreb, name=tag + "dw_g2")[:LOW_W])
    g["gla_b_g"], g["gla_norm_g"] = dbgg[0], dng[0]
    dl = fox_delta(dyc, s["ycf"], name=tag + "fox_delta")
    t = z.shape[0]
    dlq = dl[:, :FOX_H].T[:, None, :]
    dcq, dck, dcv, dfk, dfq = fox_bwd_t(zb, dyc, s["fq"], s["fk"], s["lse"], dlq, name=tag + "fox_bwd")
    dcq = dcq.transpose(1, 3, 0, 2).reshape(t, BW)
    dfc = jnp.pad((dfk[:, :, 0] + dfq.reshape(FOX_H, t)).T, ((0, 0), (LOW_W, 128 - LOW_W - FOX_H)))
    dmiscb, dbf = fox_dcf(dfc, z, w["bfp"], dmisc_g, name=tag + "fox_dcf")
    g["fox_b_f"] = dbf[0, LOW_W:LOW_W + FOX_H]
    dz = jnp.concatenate([dax, day, dbq, dbk, dbv, dbr, dcq.astype(BF16), dck, dcv, dgl0, dgl1, dgl2, dmiscb],
                         axis=1)
    dx1 = matmul(dz, w["win"], nt=True, res=dr2, res_scale=ALPHA, tm=1024, tn=1024, tk=_pick(ZW, (2432,)),
                 name=tag + "mix_dx")
    g["w_in"] = _cols_split(_regroup_out(matmul_tn(s["x1b"], dz, name=tag + "dw_in")))
    dx0, g["ffn1_w_up"], g["ffn1_w_down"], g["ln1_g"], g["ln1_b"] = _ffn_bwd(dx1, s, w, "1", s["x0b"], l,
                                                                             tag + "ffn1_")
    return dx0, g


def _local_step(x, p, target, gw, small):
    xcur = x
    xb = xcur.astype(BF16)
    layer_w, saved = [], []
    for l in range(DEPTH):
        w = _layer_weights(gw, small, l)
        xcur, xb, s = _layer_fwd(xcur, xb, p[l].astype(BF16), w, l)
        layer_w.append(w)
        saved.append(s)
    dy, sq = loss_head(xcur, target, name="loss_head")
    grads = [None] * DEPTH
    for l in reversed(range(DEPTH)):
        dy, grads[l] = _layer_bwd(dy, saved[l], layer_w[l], l)
    return 0.5 * jnp.sum(sq) / float(D), dy, grads


def kernel(x, p, ffn1_w_up, ffn1_w_down, ln1_g, ln1_b, w_in, conv_w, conv_b, lru_wa, lru_ba, lru_wx, lru_bx, lru_lambda, gla_w_g2, gla_b_g, gla_norm_g, fox_b_f, w_branch, w_out, ln2_g, ln2_b, ffn2_w_up, ffn2_w_down, ln3_g, ln3_b, ple_w_proj, ple_w_gate, ple_b_gate, ln4_g, ln4_b, loss_target, m_ffn1_w_up, m_ffn1_w_down, m_ln1_g, m_ln1_b, m_w_in, m_conv_w, m_conv_b, m_lru_wa, m_lru_ba, m_lru_wx, m_lru_bx, m_lru_lambda, m_gla_w_g2, m_gla_b_g, m_gla_norm_g, m_fox_b_f, m_w_branch, m_w_out, m_ln2_g, m_ln2_b, m_ffn2_w_up, m_ffn2_w_down, m_ln3_g, m_ln3_b, m_ple_w_proj, m_ple_w_gate, m_ple_b_gate, m_ln4_g, m_ln4_b, v_ffn1_w_up, v_ffn1_w_down, v_ln1_g, v_ln1_b, v_w_in, v_conv_w, v_conv_b, v_lru_wa, v_lru_ba, v_lru_wx, v_lru_bx, v_lru_lambda, v_gla_w_g2, v_gla_b_g, v_gla_norm_g, v_fox_b_f, v_w_branch, v_w_out, v_ln2_g, v_ln2_b, v_ffn2_w_up, v_ffn2_w_down, v_ln3_g, v_ln3_b, v_ple_w_proj, v_ple_w_gate, v_ple_b_gate, v_ln4_g, v_ln4_b):
    args = dict(locals())
    wts = {n: args[n] for n in WEIGHTS}
    mom = {n: args["m_" + n] for n in WEIGHTS}
    var = {n: args["v_" + n] for n in WEIGHTS}
    cx, cy, cc = lax.axis_index("x"), lax.axis_index("y"), lax.axis_index("c")

    shards = [wts[n].reshape((DEPTH,) + rc).astype(F32 if n == "conv_w" else BF16) for n, rc in SHARDED]
    gw = dict(zip([n for n, _ in SHARDED], gather_weights(shards)))
    small = {n: wts[n] for n in SMALL}

    loss_local, dx, grads = _local_step(x[0], p[:, 0], loss_target[0], gw, small)
    loss = lax.psum(loss_local, ("x", "y", "c"))
    grad_x = dx[None]

    core = jnp.reshape(cc, (1,)).astype(jnp.int32)
    chip = jnp.reshape(2 * cx + cy, (1,)).astype(jnp.int32)
    g0 = [grads[0][n] for n, _ in SHARDED]
    g1 = [grads[1][n] for n, _ in SHARDED]
    rb = pair_exchange(g0, g1)
    s1 = [pair_add(a0, a1, r, core, name="pair_add_" + n) for (n, _), a0, a1, r in zip(SHARDED, g0, g1, rb)]
    rb2 = chip_exchange([sb for _, sb in s1])
    s2 = [chip_add(sf, r, chip, core, name="chip_add_" + n) for (n, _), (sf, _), r in zip(SHARDED, s1, rb2)]
    gout = {n: gsh.reshape(wts[n].shape) for (n, _), gsh in zip(SHARDED, pair_share(s2))}

    small_sizes = [wts[n].size for n in SMALL]
    srows = -(-sum(small_sizes) // (8 * PACK_W)) * 8
    gs = _pack([jnp.stack([grads[l][n] for l in range(DEPTH)]) for n in SMALL], srows)
    me = jnp.reshape(4 * cx + 2 * cy + cc, (1,)).astype(jnp.int32)
    gsum = small_add(gs, small_exchange(gs), me)

    delta, new_m, new_v = {}, {}, {}
    for n, (_, cols) in SHARDED:
        shp = wts[n].shape
        v2 = lambda a: a.reshape(-1, cols)
        d, mn, vn = adamw(v2(wts[n]), v2(gout[n]), v2(mom[n]), v2(var[n]), name="adamw_" + n)
        delta[n], new_m[n], new_v[n] = d.reshape(shp), mn.reshape(shp), vn.reshape(shp)
    d, mn, vn = adamw(_pack([wts[n] for n in SMALL], srows), gsum, _pack([mom[n] for n in SMALL], srows),
                      _pack([var[n] for n in SMALL], srows), name="adamw_small")
    off = 0
    for n, size in zip(SMALL, small_sizes):
        shp = wts[n].shape
        take = lambda a: a.reshape(-1)[off:off + size].reshape(shp)
        gout[n], delta[n], new_m[n], new_v[n] = take(gsum), take(d), take(mn), take(vn)
        off += size

    return (loss, grad_x, *[gout[n] for n in WEIGHTS], *[delta[n] for n in WEIGHTS],
            *[new_m[n] for n in WEIGHTS], *[new_v[n] for n in WEIGHTS])
```

```python
import functools
import math

import jax
import jax.numpy as jnp
from jax import lax
from jax.experimental import pallas as pl
from jax.experimental.pallas import tpu as pltpu

F32 = jnp.float32
BF16 = jnp.bfloat16

D = 1024
DFF = 2816
BW = 512
PLE = 256
DEPTH = 2
ALPHA = (2 * DEPTH) ** 0.25
LN_EPS = 1e-5
RMS_EPS = 1e-6
LRU_C = 8.0
GLA_TAU = 16.0
CHUNK = 64
D_IN = 7192
ZW = 7296
AX, AY, BQ, BK, BV, BR, CQ, CK, CV, G0, MISC = 0, 512, 1024, 1280, 1536, 2048, 2560, 3072, 3584, 4096, 7168
LOW_W, FOX_H = 16, 8
ADAM_LR, ADAM_B1, ADAM_B2, ADAM_EPS, ADAM_WD, ADAM_STEP = 0.001, 0.9, 0.999, 1e-08, 0.01, 10
PACK_W = 1024
VMEM_LIMIT = 56 << 20

MESH = pl.DeviceIdType.MESH
ANY = pl.BlockSpec(memory_space=pl.ANY)


def _pcall(body, **kw):
    return pl.pallas_call(body, **kw)


def _cp(*dims):
    return pltpu.CompilerParams(dimension_semantics=dims, vmem_limit_bytes=VMEM_LIMIT)


def _dot(a, b):
    return jnp.dot(a, b, preferred_element_type=F32)


def _dot_nt(a, b):
    return lax.dot_general(a, b, (((1,), (1,)), ((), ())), preferred_element_type=F32)


def _dot_tn(a, b):
    return lax.dot_general(a, b, (((0,), (0,)), ((), ())), preferred_element_type=F32)


def _dot_hi(a, b):
    return jnp.dot(a, b, preferred_element_type=F32, precision=lax.Precision.HIGHEST)


def _sigmoid(x):
    return 1.0 / (1.0 + jnp.exp(-x))


def _softplus(x):
    return jnp.maximum(x, 0.0) + jnp.log(1.0 + jnp.exp(-jnp.abs(x)))


def _log_sigmoid(x):
    return -_softplus(-x)


def _expm1(x):
    poly = x * (1.0 + x * (0.5 + x * (1.0 / 6.0 + x * (1.0 / 24.0 + x * (1.0 / 120.0 + x * (1.0 / 720.0))))))
    return jnp.where(jnp.abs(x) < 0.1, poly, jnp.exp(x) - 1.0)


_GELU_C = math.sqrt(2.0 / math.pi)


def _gelu(x):
    return 0.5 * x * (1.0 + jnp.tanh(_GELU_C * (x + 0.044715 * x * x * x)))


def _gelu_grad(x):
    t = jnp.tanh(_GELU_C * (x + 0.044715 * x * x * x))
    return 0.5 * (1.0 + t) + 0.5 * x * (1.0 - t * t) * _GELU_C * (1.0 + 3.0 * 0.044715 * x * x)


def _ln_stats(r):
    mu = jnp.mean(r, axis=-1, keepdims=True)
    xc = r - mu
    var = jnp.mean(xc * xc, axis=-1, keepdims=True)
    return xc, lax.rsqrt(var + LN_EPS)


def _pick(n, cands):
    for c in cands:
        if n % c == 0:
            return c
    return n


def _rows(tm, w, col=0):
    return pl.BlockSpec((tm, w), lambda i: (i, col))


def _fix(shape):
    nd = len(shape)
    return pl.BlockSpec(shape, lambda i: (0,) * nd)


def _layer(l, shape):
    nd = len(shape)
    return pl.BlockSpec((None,) + tuple(shape), lambda i: (l,) + (0,) * nd)


def matmul(a, b, *, name, nt=False, b_lead=(), res=None, res_scale=1.0, also_bf16=False, tm=512, tn=512,
           tk=None):
    m, k = a.shape
    n = b.shape[-2] if nt else b.shape[-1]
    tm, tn = min(tm, m), min(tn, n)
    tk = k if tk is None else tk
    nk = k // tk
    has_res = res is not None
    lead = tuple(b_lead)
    dot = _dot_nt if nt else _dot

    def body(*refs):
        a_ref, b_ref = refs[0], refs[1]
        pos = 2
        r_ref = None
        if has_res:
            r_ref = refs[pos]
            pos += 1
        o_ref = refs[pos]
        pos += 1
        ob_ref = None
        if also_bf16:
            ob_ref = refs[pos]
            pos += 1
        acc = refs[pos]
        kk = pl.program_id(2)

        @pl.when(kk == 0)
        def _():
            acc[...] = jnp.zeros_like(acc)

        acc[...] += dot(a_ref[...], b_ref[...])

        @pl.when(kk == nk - 1)
        def _():
            v = acc[...]
            if has_res:
                v = v + res_scale * r_ref[...]
            o_ref[...] = v
            if also_bf16:
                ob_ref[...] = v.astype(BF16)

    none = (None,) * len(lead)
    if nt:
        b_spec = pl.BlockSpec(none + (tn, tk), lambda j, i, kk: lead + (j, kk))
    else:
        b_spec = pl.BlockSpec(none + (tk, tn), lambda j, i, kk: lead + (kk, j))
    in_specs = [pl.BlockSpec((tm, tk), lambda j, i, kk: (i, kk)), b_spec]
    args = [a, b]
    if has_res:
        in_specs.append(pl.BlockSpec((tm, tn), lambda j, i, kk: (i, j)))
        args.append(res)
    out_shape = [jax.ShapeDtypeStruct((m, n), F32)]
    out_specs = [pl.BlockSpec((tm, tn), lambda j, i, kk: (i, j))]
    if also_bf16:
        out_shape.append(jax.ShapeDtypeStruct((m, n), BF16))
        out_specs.append(pl.BlockSpec((tm, tn), lambda j, i, kk: (i, j)))
    out = _pcall(body, name=name, grid=(n // tn, m // tm, nk), in_specs=in_specs, out_specs=out_specs,
                 out_shape=out_shape, scratch_shapes=[pltpu.VMEM((tm, tn), F32)],
                 compiler_params=_cp("parallel", "parallel", "arbitrary"))(*args)
    return out if also_bf16 else out[0]


def matmul_tn(a, b, *, name):
    t, k = a.shape
    n = b.shape[1]
    tt = min(2048, t)
    tk = _pick(k, (512, 1408, 256, 128))
    tn = _pick(n, (1024, 1408, 2432, 512, 256, 128))
    nt = t // tt

    def body(a_ref, b_ref, o_ref):
        @pl.when(pl.program_id(2) == 0)
        def _():
            o_ref[...] = jnp.zeros_like(o_ref)

        o_ref[...] += _dot_tn(a_ref[...], b_ref[...])

    return _pcall(body, name=name, grid=(k // tk, n // tn, nt),
                  in_specs=[pl.BlockSpec((tt, tk), lambda i, j, s: (s, i)),
                            pl.BlockSpec((tt, tn), lambda i, j, s: (s, j))],
                  out_specs=pl.BlockSpec((tk, tn), lambda i, j, s: (i, j)),
                  out_shape=jax.ShapeDtypeStruct((k, n), F32),
                  compiler_params=_cp("parallel", "parallel", "arbitrary"))(a, b)


UPW = 1408


def matmul_tn_up(a, dgate, dup, *, name):
    t, k = a.shape
    tt = min(2048, t)
    tk = 512

    def body(a_ref, g_ref, u_ref, o_ref):
        j = pl.program_id(1)

        @pl.when(pl.program_id(2) == 0)
        def _():
            o_ref[...] = jnp.zeros_like(o_ref)

        @pl.when(j < 2)
        def _():
            o_ref[...] += _dot_tn(a_ref[...], g_ref[...])

        @pl.when(j >= 2)
        def _():
            o_ref[...] += _dot_tn(a_ref[...], u_ref[...])

    return _pcall(body, name=name, grid=(k // tk, 4, t // tt),
                  in_specs=[pl.BlockSpec((tt, tk), lambda i, j, s: (s, i)),
                            pl.BlockSpec((tt, UPW), lambda i, j, s: (jnp.where(j < 2, s, 0), jnp.minimum(j, 1))),
                            pl.BlockSpec((tt, UPW), lambda i, j, s: (jnp.where(j >= 2, s, 0), jnp.maximum(j - 2, 0)))],
                  out_specs=pl.BlockSpec((None, tk, UPW), lambda i, j, s: (j, i, 0)),
                  out_shape=jax.ShapeDtypeStruct((4, k, UPW), F32),
                  compiler_params=_cp("parallel", "parallel", "arbitrary"))(a, dgate, dup)


def ffn_dx(dgate, dup, wup, l, res, *, name):
    t = dgate.shape[0]
    tm, tn = min(1024, t), 1024

    def body(g_ref, u_ref, w_ref, r_ref, o_ref, acc):
        kk = pl.program_id(2)

        @pl.when(kk == 0)
        def _():
            acc[...] = jnp.zeros_like(acc)

        @pl.when(kk < 2)
        def _():
            acc[...] += _dot_nt(g_ref[...], w_ref[...])

        @pl.when(kk >= 2)
        def _():
            acc[...] += _dot_nt(u_ref[...], w_ref[...])

        @pl.when(kk == 3)
        def _():
            o_ref[...] = acc[...] + ALPHA * r_ref[...]

    return _pcall(body, name=name, grid=(D // tn, t // tm, 4),
                  in_specs=[pl.BlockSpec((tm, UPW), lambda j, i, kk: (i, jnp.minimum(kk, 1))),
                            pl.BlockSpec((tm, UPW), lambda j, i, kk: (i, jnp.maximum(kk - 2, 0))),
                            pl.BlockSpec((None, None, tn, UPW), lambda j, i, kk: (l, kk, j, 0)),
                            pl.BlockSpec((tm, tn), lambda j, i, kk: (i, j))],
                  out_specs=pl.BlockSpec((tm, tn), lambda j, i, kk: (i, j)),
                  out_shape=jax.ShapeDtypeStruct((t, D), F32),
                  scratch_shapes=[pltpu.VMEM((tm, tn), F32)],
                  compiler_params=_cp("parallel", "parallel", "arbitrary"))(dgate, dup, wup, res)


def ffn_up(xb, wup, l, *, name):
    t = xb.shape[0]
    tm, tn = min(512, t), UPW

    def body(x_ref, wg_ref, wu_ref, g_ref, u_ref, a_ref):
        x = x_ref[...]
        g = _dot(x, wg_ref[...])
        u = _dot(x, wu_ref[...])
        g_ref[...] = g
        u_ref[...] = u
        a_ref[...] = (g * _sigmoid(g) * u).astype(BF16)

    blk = pl.BlockSpec((tm, tn), lambda j, i: (i, j))
    return _pcall(body, name=name, grid=(DFF // tn, t // tm),
                  in_specs=[pl.BlockSpec((tm, D), lambda j, i: (i, 0)),
                            pl.BlockSpec((None, None, D, tn), lambda j, i: (l, j, 0, 0)),
                            pl.BlockSpec((None, None, D, tn), lambda j, i: (l, 2 + j, 0, 0))],
                  out_specs=[blk, blk, blk],
                  out_shape=[jax.ShapeDtypeStruct((t, DFF), F32), jax.ShapeDtypeStruct((t, DFF), F32),
                             jax.ShapeDtypeStruct((t, DFF), BF16)],
                  compiler_params=_cp("parallel", "parallel"))(xb, wup, wup)


def matmul_res_ln(a, w, l, res, g, b, *, mm_scale, name):
    t, k = a.shape
    tm = min(256, t)

    def body(a_ref, w_ref, res_ref, g_ref, b_ref, r_ref, y_ref, yb_ref):
        f = _dot(a_ref[...], w_ref[...])
        r = ALPHA * res_ref[...] + mm_scale * f
        xc, rstd = _ln_stats(r)
        y = xc * rstd * g_ref[...] + b_ref[...]
        r_ref[...] = r
        y_ref[...] = y
        yb_ref[...] = y.astype(BF16)

    return _pcall(body, name=name, grid=(t // tm,),
                  in_specs=[_rows(tm, k), _layer(l, (k, D)), _rows(tm, D), _fix((1, D)), _fix((1, D))],
                  out_specs=[_rows(tm, D)] * 3,
                  out_shape=[jax.ShapeDtypeStruct((t, D), F32), jax.ShapeDtypeStruct((t, D), F32),
                             jax.ShapeDtypeStruct((t, D), BF16)],
                  compiler_params=_cp("parallel"))(a, w, res, g, b)


def ln_bwd(dy, r, g, *, out_scale, name):
    t = dy.shape[0]
    tm = min(256, t)

    def body(dy_ref, r_ref, g_ref, dr_ref, drb_ref, dg_ref, db_ref):
        @pl.when(pl.program_id(0) == 0)
        def _():
            dg_ref[...] = jnp.zeros_like(dg_ref)
            db_ref[...] = jnp.zeros_like(db_ref)

        xc, rstd = _ln_stats(r_ref[...])
        xhat = xc * rstd
        d = dy_ref[...]
        dxh = d * g_ref[...]
        dr = rstd * (dxh - jnp.mean(dxh, axis=-1, keepdims=True)
                     - xhat * jnp.mean(dxh * xhat, axis=-1, keepdims=True))
        dr_ref[...] = dr
        drb_ref[...] = (out_scale * dr).astype(BF16)
        dg_ref[...] += jnp.sum(d * xhat, axis=0, keepdims=True)
        db_ref[...] += jnp.sum(d, axis=0, keepdims=True)

    return _pcall(body, name=name, grid=(t // tm,),
                  in_specs=[_rows(tm, D), _rows(tm, D), _fix((1, D))],
                  out_specs=[_rows(tm, D), _rows(tm, D), _fix((1, D)), _fix((1, D))],
                  out_shape=[jax.ShapeDtypeStruct((t, D), F32), jax.ShapeDtypeStruct((t, D), BF16),
                             jax.ShapeDtypeStruct((1, D), F32), jax.ShapeDtypeStruct((1, D), F32)],
                  compiler_params=_cp("arbitrary"))(dy, r, g)


def ffn_down_bwd(dfb, wd, l, gate, up, *, name):
    t = dfb.shape[0]
    tm, tn = min(512, t), UPW
    nj = DFF // tn

    def body(df_ref, w_ref, g_ref, u_ref, dg_ref, du_ref):
        da = _dot_nt(df_ref[...], w_ref[...])
        g = g_ref[...]
        s = _sigmoid(g)
        dg_ref[...] = (da * u_ref[...] * s * (1.0 + g * (1.0 - s))).astype(BF16)
        du_ref[...] = (da * g * s).astype(BF16)

    blk = pl.BlockSpec((tm, tn), lambda j, i: (i, j))
    return _pcall(body, name=name, grid=(nj, t // tm),
                  in_specs=[pl.BlockSpec((tm, D), lambda j, i: (i, 0)),
                            pl.BlockSpec((None, tn, D), lambda j, i: (l, j, 0)), blk, blk],
                  out_specs=[blk, blk],
                  out_shape=[jax.ShapeDtypeStruct((t, DFF), BF16), jax.ShapeDtypeStruct((t, DFF), BF16)],
                  compiler_params=_cp("parallel", "parallel"))(dfb, wd, gate, up)


def ple_fwd(xb, x, pb, wgate, l, wproj, bgate, g, b, *, name):
    t = x.shape[0]
    tm = min(256, t)

    def body(xb_ref, x_ref, p_ref, wg_ref, wp_ref, bg_ref, g_ref, b_ref, r_ref, y_ref, yb_ref):
        gl = _dot(xb_ref[...], wg_ref[...]) + bg_ref[...]
        pe = _dot(p_ref[...], wp_ref[...])
        r = ALPHA * x_ref[...] + _sigmoid(gl) * pe
        xc, rstd = _ln_stats(r)
        y = xc * rstd * g_ref[...] + b_ref[...]
        r_ref[...] = r
        y_ref[...] = y
        yb_ref[...] = y.astype(BF16)

    return _pcall(body, name=name, grid=(t // tm,),
                  in_specs=[_rows(tm, D), _rows(tm, D), _rows(tm, PLE), _layer(l, (D, D)), _fix((PLE, D)),
                            _fix((1, D)), _fix((1, D)), _fix((1, D))],
                  out_specs=[_rows(tm, D)] * 3,
                  out_shape=[jax.ShapeDtypeStruct((t, D), F32), jax.ShapeDtypeStruct((t, D), F32),
                             jax.ShapeDtypeStruct((t, D), BF16)],
                  compiler_params=_cp("parallel"))(xb, x, pb, wgate, wproj, bgate, g, b)


def ple_bwd(dy, r, xb, pb, wgate, l, wproj, bgate, g, *, name):
    t = dy.shape[0]
    tm = min(256, t)

    def body(dy_ref, r_ref, xb_ref, p_ref, wg_ref, wp_ref, bg_ref, g_ref,
             dr_ref, dgl_ref, dpe_ref, dg_ref, db_ref, dbg_ref):
        @pl.when(pl.program_id(0) == 0)
        def _():
            dg_ref[...] = jnp.zeros_like(dg_ref)
            db_ref[...] = jnp.zeros_like(db_ref)
            dbg_ref[...] = jnp.zeros_like(dbg_ref)

        xc, rstd = _ln_stats(r_ref[...])
        xhat = xc * rstd
        d = dy_ref[...]
        dxh = d * g_ref[...]
        dr = rstd * (dxh - jnp.mean(dxh, axis=-1, keepdims=True)
                     - xhat * jnp.mean(dxh * xhat, axis=-1, keepdims=True))
        s = _sigmoid(_dot(xb_ref[...], wg_ref[...]) + bg_ref[...])
        pe = _dot(p_ref[...], wp_ref[...])
        dgl = dr * pe * s * (1.0 - s)
        dr_ref[...] = dr
        dgl_ref[...] = dgl.astype(BF16)
        dpe_ref[...] = (dr * s).astype(BF16)
        dg_ref[...] += jnp.sum(d * xhat, axis=0, keepdims=True)
        db_ref[...] += jnp.sum(d, axis=0, keepdims=True)
        dbg_ref[...] += jnp.sum(dgl, axis=0, keepdims=True)

    vec = jax.ShapeDtypeStruct((1, D), F32)
    return _pcall(body, name=name, grid=(t // tm,),
                  in_specs=[_rows(tm, D), _rows(tm, D), _rows(tm, D), _rows(tm, PLE), _layer(l, (D, D)),
                            _fix((PLE, D)), _fix((1, D)), _fix((1, D))],
                  out_specs=[_rows(tm, D), _rows(tm, D), _rows(tm, D), _fix((1, D)), _fix((1, D)), _fix((1, D))],
                  out_shape=[jax.ShapeDtypeStruct((t, D), F32), jax.ShapeDtypeStruct((t, D), BF16),
                             jax.ShapeDtypeStruct((t, D), BF16), vec, vec, vec],
                  compiler_params=_cp("arbitrary"))(dy, r, xb, pb, wgate, wproj, bgate, g)


def loss_head(y, tgt, *, name):
    t = y.shape[0]
    tm = min(256, t)

    def body(y_ref, t_ref, dy_ref, sq_ref):
        @pl.when(pl.program_id(0) == 0)
        def _():
            sq_ref[...] = jnp.zeros_like(sq_ref)

        e = y_ref[...] - t_ref[...]
        dy_ref[...] = e / float(D)
        sq_ref[...] += jnp.sum(e * e, axis=0, keepdims=True)

    return _pcall(body, name=name, grid=(t // tm,),
                  in_specs=[_rows(tm, D), _rows(tm, D)],
                  out_specs=[_rows(tm, D), _fix((1, D))],
                  out_shape=[jax.ShapeDtypeStruct((t, D), F32), jax.ShapeDtypeStruct((1, D), F32)],
                  compiler_params=_cp("arbitrary"))(y, tgt)


def _lru_gates(xc, wa_ref, wx_ref, ba_ref, bx_ref, lam_ref):
    xcb = xc.astype(BF16)
    r = _sigmoid(_dot(xcb, wa_ref[...]) + ba_ref[...])
    ig = _sigmoid(_dot(xcb, wx_ref[...]) + bx_ref[...])
    sp = _softplus(-lam_ref[...])
    la = -LRU_C * r * sp
    a = jnp.exp(la)
    mult = jnp.sqrt(-_expm1(2.0 * la))
    return r, ig, sp, la, a, mult


def lru_fwd(z, cw, cb, wa, wx, ba, bx, lam, *, name):
    t = z.shape[0]
    tm = min(256, t)
    hb = tm // 8

    def body(ax_ref, prev_ref, ay_ref, cw_ref, cb_ref, wa_ref, wx_ref, ba_ref, bx_ref, lam_ref,
             xc_ref, xcb_ref, h_ref, ya_ref, xs, a_s, b_s, hc):
        i = pl.program_id(0)

        @pl.when(i == 0)
        def _():
            hc[...] = jnp.zeros_like(hc)

        xs[0:8, :] = jnp.where(i == 0, 0.0, prev_ref[...])
        xs[8:, :] = ax_ref[...]
        xc = cb_ref[...] + cw_ref[0:1, :] * xs[5:5 + tm, :]
        for k in range(1, 4):
            xc = xc + cw_ref[k:k + 1, :] * xs[5 + k:5 + k + tm, :]
        r, ig, sp, la, a, mult = _lru_gates(xc, wa_ref, wx_ref, ba_ref, bx_ref, lam_ref)
        a_s[...] = a
        b_s[...] = mult * (ig * xc)
        xc_ref[...] = xc
        xcb_ref[...] = xc.astype(BF16)

        def step(g, h):
            base = pl.multiple_of(g * 8, 8)
            a8 = a_s[pl.ds(base, 8), :]
            b8 = b_s[pl.ds(base, 8), :]
            for j in range(8):
                h = a8[j:j + 1, :] * h + b8[j:j + 1, :]
                h_ref[pl.ds(base + j, 1), :] = h
            return h

        hc[...] = lax.fori_loop(0, tm // 8, step, hc[...])
        ya_ref[...] = (_gelu(ay_ref[...]) * h_ref[...]).astype(BF16)

    vec = _fix((1, BW))
    return _pcall(body, name=name, grid=(t // tm,),
                  in_specs=[_rows(tm, BW, AX // BW),
                            pl.BlockSpec((8, BW), lambda i: (jnp.maximum(i * hb - 1, 0), AX // BW)),
                            _rows(tm, BW, AY // BW), _fix((4, BW)), vec, _fix((BW, BW)), _fix((BW, BW)),
                            vec, vec, vec],
                  out_specs=[_rows(tm, BW)] * 4,
                  out_shape=[jax.ShapeDtypeStruct((t, BW), F32), jax.ShapeDtypeStruct((t, BW), BF16),
                             jax.ShapeDtypeStruct((t, BW), F32), jax.ShapeDtypeStruct((t, BW), BF16)],
                  scratch_shapes=[pltpu.VMEM((tm + 8, BW), F32), pltpu.VMEM((tm, BW), F32),
                                  pltpu.VMEM((tm, BW), F32), pltpu.VMEM((1, BW), F32)],
                  compiler_params=_cp("arbitrary"))(z, z, z, cw, cb, wa, wx, ba, bx, lam)


def lru_bwd(dya, z, h, xc, wa, wx, ba, bx, lam, *, name):
    t = dya.shape[0]
    tm = min(256, t)
    nb = t // tm
    hb = tm // 8

    def body(dya_ref, ay_ref, h_ref, hprev_ref, xc_ref, wa_ref, wx_ref, ba_ref, bx_ref,
             lam_ref, day_ref, dxc_ref, dpr_ref, dpi_ref, dba_ref, dbx_ref, dlam_ref,
             hs, a_s, g_s, d_s, cc):
        i = pl.program_id(0)

        @pl.when(i == 0)
        def _():
            cc[...] = jnp.zeros_like(cc)
            dba_ref[...] = jnp.zeros_like(dba_ref)
            dbx_ref[...] = jnp.zeros_like(dbx_ref)
            dlam_ref[...] = jnp.zeros_like(dlam_ref)

        xc = xc_ref[...]
        r, ig, sp, la, a, mult = _lru_gates(xc, wa_ref, wx_ref, ba_ref, bx_ref, lam_ref)
        ay = ay_ref[...]
        dya = dya_ref[...]
        hcur = h_ref[...]
        day_ref[...] = (dya * hcur * _gelu_grad(ay)).astype(BF16)
        a_s[...] = a
        g_s[...] = dya * _gelu(ay)

        def step(gg, cin):
            g = tm // 8 - 1 - gg
            base = pl.multiple_of(g * 8, 8)
            a8 = a_s[pl.ds(base, 8), :]
            g8 = g_s[pl.ds(base, 8), :]
            for j in range(7, -1, -1):
                d = g8[j:j + 1, :] + cin
                d_s[pl.ds(base + j, 1), :] = d
                cin = a8[j:j + 1, :] * d
            return cin

        cc[...] = lax.fori_loop(0, tm // 8, step, cc[...])
        dht = d_s[...]
        hs[0:8, :] = jnp.where(i == nb - 1, 0.0, hprev_ref[...])
        hs[8:, :] = hcur
        da = dht * hs[7:7 + tm, :]
        dmult = dht * ig * xc
        dig = dht * mult * xc
        dla = da * a - dmult * a * a / mult
        dpr = dla * (-LRU_C * sp) * r * (1.0 - r)
        dpi = dig * ig * (1.0 - ig)
        dprb = dpr.astype(BF16)
        dpib = dpi.astype(BF16)
        dxc_ref[...] = dht * mult * ig + _dot_nt(dprb, wa_ref[...]) + _dot_nt(dpib, wx_ref[...])
        dpr_ref[...] = dprb
        dpi_ref[...] = dpib
        dba_ref[...] += jnp.sum(dpr, axis=0, keepdims=True)
        dbx_ref[...] += jnp.sum(dpi, axis=0, keepdims=True)
        dlam_ref[...] += jnp.sum(dla * (-LRU_C * r), axis=0, keepdims=True) * (-_sigmoid(-lam_ref[...]))

    vec = _fix((1, BW))
    mat = _fix((BW, BW))
    rev = lambda col: pl.BlockSpec((tm, BW), lambda i: (nb - 1 - i, col))
    vshape = jax.ShapeDtypeStruct((1, BW), F32)
    return _pcall(body, name=name, grid=(nb,),
                  in_specs=[rev(0), rev(AY // BW), rev(0),
                            pl.BlockSpec((8, BW), lambda i: (jnp.maximum((nb - 1 - i) * hb - 1, 0), 0)),
                            rev(0), mat, mat, vec, vec, vec],
                  out_specs=[rev(0), rev(0), rev(0), rev(0), vec, vec, vec],
                  out_shape=[jax.ShapeDtypeStruct((t, BW), BF16), jax.ShapeDtypeStruct((t, BW), F32),
                             jax.ShapeDtypeStruct((t, BW), BF16), jax.ShapeDtypeStruct((t, BW), BF16),
                             vshape, vshape, vshape],
                  scratch_shapes=[pltpu.VMEM((tm + 8, BW), F32), pltpu.VMEM((tm, BW), F32),
                                  pltpu.VMEM((tm, BW), F32), pltpu.VMEM((tm, BW), F32),
                                  pltpu.VMEM((1, BW), F32)],
                  compiler_params=_cp("arbitrary"))(dya, z, h, h, xc, wa, wx, ba, bx, lam)


def conv_bwd(dxc, z, cw, *, name):
    t = dxc.shape[0]
    tm = min(256, t)
    nb = t // tm
    hb = tm // 8

    def body(d_ref, dnext_ref, ax_ref, prev_ref, cw_ref, dax_ref, dcw_ref, dcb_ref, ds, xs):
        i = pl.program_id(0)

        @pl.when(i == 0)
        def _():
            dcw_ref[...] = jnp.zeros_like(dcw_ref)
            dcb_ref[...] = jnp.zeros_like(dcb_ref)

        d = d_ref[...]
        ds[0:tm, :] = d
        ds[tm:, :] = jnp.where(i == nb - 1, 0.0, dnext_ref[...])
        xs[0:8, :] = jnp.where(i == 0, 0.0, prev_ref[...])
        xs[8:, :] = ax_ref[...]
        dax = cw_ref[3:4, :] * d
        for k in range(3):
            dax = dax + cw_ref[k:k + 1, :] * ds[3 - k:3 - k + tm, :]
        dax_ref[...] = dax.astype(BF16)
        for k in range(4):
            dcw_ref[k:k + 1, :] += jnp.sum(d * xs[5 + k:5 + k + tm, :], axis=0, keepdims=True)
        dcb_ref[...] += jnp.sum(d, axis=0, keepdims=True)

    return _pcall(body, name=name, grid=(nb,),
                  in_specs=[_rows(tm, BW),
                            pl.BlockSpec((8, BW), lambda i: (jnp.minimum((i + 1) * hb, nb * hb - 1), 0)),
                            _rows(tm, BW, AX // BW),
                            pl.BlockSpec((8, BW), lambda i: (jnp.maximum(i * hb - 1, 0), AX // BW)),
                            _fix((4, BW))],
                  out_specs=[_rows(tm, BW), _fix((4, BW)), _fix((1, BW))],
                  out_shape=[jax.ShapeDtypeStruct((t, BW), BF16), jax.ShapeDtypeStruct((4, BW), F32),
                             jax.ShapeDtypeStruct((1, BW), F32)],
                  scratch_shapes=[pltpu.VMEM((tm + 8, BW), F32), pltpu.VMEM((tm + 8, BW), F32)],
                  compiler_params=_cp("arbitrary"))(dxc, dxc, z, z, cw)


GLA_CB = 4


def _gla_consts():
    tri = (jnp.arange(CHUNK)[:, None] >= jnp.arange(CHUNK)[None, :]).astype(F32)
    mask = ((jnp.arange(BW)[:, None] // 128) == (jnp.arange(256)[None, :] // 64)).astype(F32)
    return tri, mask


def gla_fwd(z, zb, wg2p, bg, ng, *, name):
    t = z.shape[0]
    tm = GLA_CB * CHUNK
    nc = t // CHUNK
    tri, mask = _gla_consts()

    def body(q_ref, k_ref, v_ref, misc_ref, br_ref, w_ref, bg_ref, ng_ref, tri_ref, mask_ref,
             yb_ref, st_ref, st):
        @pl.when(pl.program_id(0) == 0)
        def _():
            st[...] = jnp.zeros_like(st)

        for c in range(GLA_CB):
            rows = slice(c * CHUNK, (c + 1) * CHUNK)
            pre = _dot(misc_ref[rows, :], w_ref[...]) + bg_ref[...]
            la = _log_sigmoid(pre) / GLA_TAU
            gc = _dot_hi(tri_ref[...], la)
            gt = gc[CHUNK - 1:CHUNK, :]
            kdec = k_ref[rows, :] * jnp.exp(gt - gc)
            delta = _dot_tn(v_ref[rows, :], kdec.astype(BF16))
            s_new = st[...] * jnp.exp(gt) + delta * mask_ref[...]
            st[...] = s_new
            st_ref[c] = s_new
            o = _dot_nt(q_ref[rows, :], s_new.astype(BF16)) * (64.0 ** -0.5)
            br = br_ref[rows, :]
            for hd in range(4):
                cols = slice(hd * 128, (hd + 1) * 128)
                oh = o[:, cols]
                rs = lax.rsqrt(jnp.mean(oh * oh, axis=-1, keepdims=True) + RMS_EPS)
                brh = br[:, cols]
                yb_ref[rows, cols] = (oh * rs * ng_ref[:, cols] * (brh * _sigmoid(brh))).astype(BF16)

    return _pcall(body, name=name, grid=(t // tm,),
                  in_specs=[_rows(tm, 256, BQ // 256), _rows(tm, 256, BK // 256), _rows(tm, BW, BV // BW),
                            _rows(tm, 128, MISC // 128), _rows(tm, BW, BR // BW), _fix((128, 256)),
                            _fix((1, 256)), _fix((1, BW)), _fix((CHUNK, CHUNK)), _fix((BW, 256))],
                  out_specs=[_rows(tm, BW), pl.BlockSpec((GLA_CB, BW, 256), lambda i: (i, 0, 0))],
                  out_shape=[jax.ShapeDtypeStruct((t, BW), BF16), jax.ShapeDtypeStruct((nc, BW, 256), F32)],
                  scratch_shapes=[pltpu.VMEM((BW, 256), F32)],
                  compiler_params=_cp("arbitrary"))(zb, z, zb, zb, z, wg2p, bg, ng, tri, mask)


def gla_bwd(dyb, z, zb, states, wg2p, bg, ng, *, name):
    t = z.shape[0]
    tm = GLA_CB * CHUNK
    nb = t // tm
    tri, mask = _gla_consts()
    triu = tri.T

    def body(dy_ref, q_ref, k_ref, v_ref, misc_ref, br_ref, st_ref, sp_ref, w_ref, bg_ref, ng_ref,
             tri_ref, triu_ref, mask_ref,
             dq_ref, dk_ref, dv_ref, dbr_ref, dmisc_ref, dpre_ref, dbg_ref, dng_ref, cc):
        i = pl.program_id(0)

        @pl.when(i == 0)
        def _():
            cc[...] = jnp.zeros_like(cc)
            dbg_ref[...] = jnp.zeros_like(dbg_ref)
            dng_ref[...] = jnp.zeros_like(dng_ref)

        last_row = lax.broadcasted_iota(jnp.int32, (CHUNK, 256), 0) == CHUNK - 1
        for c in range(GLA_CB - 1, -1, -1):
            rows = slice(c * CHUNK, (c + 1) * CHUNK)
            pre = _dot(misc_ref[rows, :], w_ref[...]) + bg_ref[...]
            la = _log_sigmoid(pre) / GLA_TAU
            gc = _dot_hi(tri_ref[...], la)
            gt = gc[CHUNK - 1:CHUNK, :]
            eg = jnp.exp(gt - gc)
            kdec = k_ref[rows, :] * eg
            e = jnp.exp(gt)
            s_n = st_ref[c]
            if c > 0:
                s_prev = st_ref[c - 1]
            else:
                s_prev = jnp.where(i == nb - 1, 0.0, sp_ref[0])
            sb = s_n.astype(BF16)
            qb = q_ref[rows, :]
            o = _dot_nt(qb, sb) * (64.0 ** -0.5)
            br = br_ref[rows, :]
            dy = dy_ref[rows, :]
            do_parts = []
            for hd in range(4):
                cols = slice(hd * 128, (hd + 1) * 128)
                oh = o[:, cols]
                rs = lax.rsqrt(jnp.mean(oh * oh, axis=-1, keepdims=True) + RMS_EPS)
                ohat = oh * rs
                brh = br[:, cols]
                sg = _sigmoid(brh)
                dyh = dy[:, cols]
                ngh = ng_ref[:, cols]
                don = dyh * (brh * sg)
                dbr_ref[rows, cols] = (dyh * (ohat * ngh) * sg * (1.0 + brh * (1.0 - sg))).astype(BF16)
                dng_ref[:, cols] += jnp.sum(don * ohat, axis=0, keepdims=True)
                doh = don * ngh
                do_parts.append(rs * (doh - ohat * jnp.mean(doh * ohat, axis=-1, keepdims=True)))
            dob = jnp.concatenate(do_parts, axis=1).astype(BF16)
            dq_ref[rows, :] = (_dot(dob, sb) * (64.0 ** -0.5)).astype(BF16)
            dst = cc[...] + _dot_tn(dob, qb) * (64.0 ** -0.5) * mask_ref[...]
            dsb = dst.astype(BF16)
            dkdec = _dot(v_ref[rows, :], dsb)
            dv_ref[rows, :] = _dot_nt(kdec.astype(BF16), dsb).astype(BF16)
            dgt = jnp.sum(dst * s_prev, axis=0, keepdims=True) * e
            dk_ref[rows, :] = (dkdec * eg).astype(BF16)
            dd = dkdec * kdec
            dgt = dgt + jnp.sum(dd, axis=0, keepdims=True)
            dgc = jnp.where(last_row, dgt - dd, -dd)
            dla = _dot_hi(triu_ref[...], dgc)
            dpre = dla * (1.0 / GLA_TAU) * _sigmoid(-pre)
            dpb = dpre.astype(BF16)
            dpre_ref[rows, :] = dpb
            dmisc_ref[rows, :] = _dot_nt(dpb, w_ref[...])
            dbg_ref[...] += jnp.sum(dpre, axis=0, keepdims=True)
            cc[...] = dst * e

    rev = lambda w, col: pl.BlockSpec((tm, w), lambda i: (nb - 1 - i, col))
    return _pcall(body, name=name, grid=(nb,),
                  in_specs=[rev(BW, 0), rev(256, BQ // 256), rev(256, BK // 256), rev(BW, BV // BW),
                            rev(128, MISC // 128), rev(BW, BR // BW),
                            pl.BlockSpec((GLA_CB, BW, 256), lambda i: (nb - 1 - i, 0, 0)),
                            pl.BlockSpec((1, BW, 256), lambda i: (jnp.maximum((nb - 1 - i) * GLA_CB - 1, 0), 0, 0)),
                            _fix((128, 256)), _fix((1, 256)), _fix((1, BW)),
                            _fix((CHUNK, CHUNK)), _fix((CHUNK, CHUNK)), _fix((BW, 256))],
                  out_specs=[rev(256, 0), rev(256, 0), rev(BW, 0), rev(BW, 0), rev(128, 0), rev(256, 0),
                             _fix((1, 256)), _fix((1, BW))],
                  out_shape=[jax.ShapeDtypeStruct((t, 256), BF16), jax.ShapeDtypeStruct((t, 256), BF16),
                             jax.ShapeDtypeStruct((t, BW), BF16), jax.ShapeDtypeStruct((t, BW), BF16),
                             jax.ShapeDtypeStruct((t, 128), F32), jax.ShapeDtypeStruct((t, 256), BF16),
                             jax.ShapeDtypeStruct((1, 256), F32), jax.ShapeDtypeStruct((1, BW), F32)],
                  scratch_shapes=[pltpu.VMEM((BW, 256), F32)],
                  compiler_params=_cp("arbitrary"))(dyb, zb, z, zb, zb, z, states, states, wg2p, bg, ng,
                                                    tri, triu, mask)


FOX_SCALE = 64.0 ** -0.5
NEG = -1e30


def fox_fcum(z, bfp, *, name):
    t = z.shape[0]
    tm = min(256, t)
    tri = (jnp.arange(tm)[:, None] >= jnp.arange(tm)[None, :]).astype(F32)

    def body(m_ref, b_ref, tri_ref, o_ref, cc):
        @pl.when(pl.program_id(0) == 0)
        def _():
            cc[...] = jnp.zeros_like(cc)

        lf = _log_sigmoid(m_ref[...] + b_ref[...])
        cs = _dot_hi(tri_ref[...], lf) + cc[...]
        o_ref[...] = cs
        cc[...] = cs[tm - 1:tm, :]

    return _pcall(body, name=name, grid=(t // tm,),
                  in_specs=[_rows(tm, 128, MISC // 128), _fix((1, 128)), _fix((tm, tm))],
                  out_specs=_rows(tm, 128), out_shape=jax.ShapeDtypeStruct((t, 128), F32),
                  scratch_shapes=[pltpu.VMEM((1, 128), F32)],
                  compiler_params=_cp("arbitrary"))(z, bfp, tri)


def fox_dcf(dfc, z, bfp, dmisc_g, *, name):
    t = z.shape[0]
    tm = min(256, t)
    nb = t // tm
    triu = (jnp.arange(tm)[:, None] <= jnp.arange(tm)[None, :]).astype(F32)

    def body(d_ref, m_ref, b_ref, g_ref, tri_ref, o_ref, dbf_ref, cc):
        @pl.when(pl.program_id(0) == 0)
        def _():
            cc[...] = jnp.zeros_like(cc)
            dbf_ref[...] = jnp.zeros_like(dbf_ref)

        rc = _dot_hi(tri_ref[...], d_ref[...]) + cc[...]
        cc[...] = rc[0:1, :]
        dcf = rc * _sigmoid(-(m_ref[...] + b_ref[...]))
        o_ref[...] = (dcf + g_ref[...]).astype(BF16)
        dbf_ref[...] += jnp.sum(dcf, axis=0, keepdims=True)

    rev = lambda col: pl.BlockSpec((tm, 128), lambda i: (nb - 1 - i, col))
    return _pcall(body, name=name, grid=(nb,),
                  in_specs=[rev(0), rev(MISC // 128), _fix((1, 128)), rev(0), _fix((tm, tm))],
                  out_specs=[rev(0), _fix((1, 128))],
                  out_shape=[jax.ShapeDtypeStruct((t, 128), BF16), jax.ShapeDtypeStruct((1, 128), F32)],
                  scratch_shapes=[pltpu.VMEM((1, 128), F32)],
                  compiler_params=_cp("arbitrary"))(dfc, z, bfp, dmisc_g, triu)


def fox_delta(dyc, ycf, *, name):
    t = dyc.shape[0]
    tm = min(256, t)
    seg = ((jnp.arange(BW)[:, None] // 64) == jnp.arange(128)[None, :]).astype(F32)

    def body(d_ref, o_ref, s_ref, out_ref):
        out_ref[...] = _dot_hi(d_ref[...] * o_ref[...], s_ref[...])

    return _pcall(body, name=name, grid=(t // tm,),
                  in_specs=[_rows(tm, BW), _rows(tm, BW), _fix((BW, 128))],
                  out_specs=_rows(tm, 128), out_shape=jax.ShapeDtypeStruct((t, 128), F32),
                  compiler_params=_cp("parallel"))(dyc, ycf, seg)


def fox_fwd_t(zb, frow, fkb, *, name):
    t = zb.shape[0]
    tq = min(512, t)
    nq = t // tq
    rep = tq // 128

    pairs = [(i, j) for i in range(nq) for j in range(i + 1)]
    qi_tab = jnp.asarray([p[0] for p in pairs], jnp.int32)
    kj_tab = jnp.asarray([p[1] for p in pairs], jnp.int32)

    def body(qi_ref, kj_ref, q_ref, k_ref, v_ref, fq_ref, fk_ref, y_ref, yf_ref, lse_ref, m_s, l_s, acc):
        step = pl.program_id(1)
        i, j = qi_ref[step], kj_ref[step]

        @pl.when(j == 0)
        def _():
            m_s[...] = jnp.full_like(m_s, NEG)
            l_s[...] = jnp.zeros_like(l_s)
            acc[...] = jnp.zeros_like(acc)

        lo = lax.broadcasted_iota(jnp.int32, (tq, 128), 1) < 64

        def work(diagonal):
            q = q_ref[...]
            k = k_ref[...]
            v = v_ref[...]
            if diagonal:
                key = lax.broadcasted_iota(jnp.int32, (tq, tq), 0)
                qry = lax.broadcasted_iota(jnp.int32, (tq, tq), 1)
                keep = key <= qry
            for hh in range(2):
                sel = lo if hh == 0 else jnp.logical_not(lo)
                qh = jnp.where(sel, q, jnp.zeros_like(q))
                s = _dot_nt(k, qh) + fq_ref[hh] - jnp.tile(fk_ref[hh], (1, rep))
                if diagonal:
                    s = jnp.where(keep, s, NEG)
                m_old = m_s[hh]
                m_new = jnp.maximum(m_old, jnp.max(s, axis=0, keepdims=True))
                p = jnp.exp(s - m_new)
                corr = jnp.exp(m_old - m_new)
                l_s[hh] = l_s[hh] * corr + jnp.sum(p, axis=0, keepdims=True)
                m_s[hh] = m_new
                pv = _dot_tn(v, p.astype(BF16))
                rows = slice(64 * hh, 64 * hh + 64)
                acc[rows, :] = acc[rows, :] * corr + pv[rows, :]

        @pl.when(j < i)
        def _():
            work(False)

        @pl.when(j == i)
        def _():
            work(True)
            first = lax.broadcasted_iota(jnp.int32, (128, tq), 0) < 64
            out = (acc[...] * jnp.where(first, 1.0 / l_s[0], 1.0 / l_s[1])).T
            y_ref[...] = out.astype(BF16)
            yf_ref[...] = out
            lse_ref[...] = m_s[...] + jnp.log(l_s[...])

    kv = lambda off: pl.BlockSpec((tq, 128), lambda h, s, qi, kj: (kj[s], off // 128 + h))
    gs = pltpu.PrefetchScalarGridSpec(
        num_scalar_prefetch=2, grid=(4, len(pairs)),
        in_specs=[pl.BlockSpec((tq, 128), lambda h, s, qi, kj: (qi[s], CQ // 128 + h)), kv(CK), kv(CV),
                  pl.BlockSpec((2, 1, tq), lambda h, s, qi, kj: (h, 0, qi[s])),
                  pl.BlockSpec((2, tq, 128), lambda h, s, qi, kj: (h, kj[s], 0))],
        out_specs=[pl.BlockSpec((tq, 128), lambda h, s, qi, kj: (qi[s], h)),
                   pl.BlockSpec((tq, 128), lambda h, s, qi, kj: (qi[s], h)),
                   pl.BlockSpec((2, 1, tq), lambda h, s, qi, kj: (h, 0, qi[s]))],
        scratch_shapes=[pltpu.VMEM((2, 1, tq), F32), pltpu.VMEM((2, 1, tq), F32), pltpu.VMEM((128, tq), F32)])
    return _pcall(body, name=name, grid_spec=gs,
                  out_shape=[jax.ShapeDtypeStruct((t, BW), BF16), jax.ShapeDtypeStruct((t, BW), F32),
                             jax.ShapeDtypeStruct((FOX_H, 1, t), F32)],
                  compiler_params=_cp("parallel", "arbitrary"))(qi_tab, kj_tab, zb, zb, zb, frow, fkb)


def fox_bwd_t(zb, dyc, frow, fkb, lse, dl, *, name):
    t = zb.shape[0]
    tq = min(512, t)
    nq = t // tq
    rep = tq // 128

    pairs = [(j, i) for j in range(nq) for i in range(j, nq)]
    kj_tab = jnp.asarray([p[0] for p in pairs], jnp.int32)
    qi_tab = jnp.asarray([p[1] for p in pairs], jnp.int32)

    def body(kj_ref, qi_ref, q_ref, k_ref, v_ref, do_ref, fq_ref, fk_ref, lse_ref, dl_ref,
             dq_ref, dk_ref, dv_ref, dfk_ref, dfq_ref, dk_s, dv_s, df_s):
        step = pl.program_id(1)
        j, i = kj_ref[step], qi_ref[step]

        @pl.when(step == 0)
        def _():
            dq_ref[...] = jnp.zeros_like(dq_ref)
            dfq_ref[...] = jnp.zeros_like(dfq_ref)

        @pl.when(i == j)
        def _():
            dk_s[...] = jnp.zeros_like(dk_s)
            dv_s[...] = jnp.zeros_like(dv_s)
            df_s[...] = jnp.zeros_like(df_s)

        lo = lax.broadcasted_iota(jnp.int32, (tq, 128), 1) < 64

        def work(diagonal):
            q = q_ref[...]
            k = k_ref[...]
            v = v_ref[...]
            dob = do_ref[...].astype(BF16)
            if diagonal:
                key = lax.broadcasted_iota(jnp.int32, (tq, tq), 0)
                qry = lax.broadcasted_iota(jnp.int32, (tq, tq), 1)
                keep = key <= qry
            dvs, dks = [], []
            for hh in range(2):
                sel = lo if hh == 0 else jnp.logical_not(lo)
                qh = jnp.where(sel, q, jnp.zeros_like(q))
                doh = jnp.where(sel, dob, jnp.zeros_like(dob))
                p = jnp.exp(_dot_nt(k, qh) + (fq_ref[hh] - lse_ref[hh]) - jnp.tile(fk_ref[hh], (1, rep)))
                if diagonal:
                    p = jnp.where(keep, p, 0.0)
                ds = p * (_dot_nt(v, doh) - dl_ref[hh])
                dsb = ds.astype(BF16)
                dvs.append(_dot(p.astype(BF16), dob))
                dks.append(_dot(dsb, q))
                rows = slice(64 * hh, 64 * hh + 64)
                dq_ref[i, rows, :] += _dot_tn(k, dsb)[rows, :]
                part = ds[:, 0:128]
                for r in range(1, rep):
                    part = part + ds[:, 128 * r:128 * (r + 1)]
                df_s[hh] += part
                dfq_ref[hh, i] += jnp.sum(ds, axis=0, keepdims=True)
            dv_s[...] += jnp.where(lo, dvs[0], dvs[1])
            dk_s[...] += jnp.where(lo, dks[0], dks[1])

        @pl.when(i > j)
        def _():
            work(False)

        @pl.when(i == j)
        def _():
            work(True)

        @pl.when(i == nq - 1)
        def _():
            dk_ref[...] = dk_s[...].astype(BF16)
            dv_ref[...] = dv_s[...].astype(BF16)
            dfk_ref[...] = -jnp.sum(df_s[...], axis=-1, keepdims=True)

    row = lambda: pl.BlockSpec((2, 1, tq), lambda h, s, kj, qi: (h, 0, qi[s]))
    gs = pltpu.PrefetchScalarGridSpec(
        num_scalar_prefetch=2, grid=(4, len(pairs)),
        in_specs=[pl.BlockSpec((tq, 128), lambda h, s, kj, qi: (qi[s], CQ // 128 + h)),
                  pl.BlockSpec((tq, 128), lambda h, s, kj, qi: (kj[s], CK // 128 + h)),
                  pl.BlockSpec((tq, 128), lambda h, s, kj, qi: (kj[s], CV // 128 + h)),
                  pl.BlockSpec((tq, 128), lambda h, s, kj, qi: (qi[s], h)),
                  row(), pl.BlockSpec((2, tq, 128), lambda h, s, kj, qi: (h, kj[s], 0)), row(), row()],
        out_specs=[pl.BlockSpec((None, nq, 128, tq), lambda h, s, kj, qi: (h, 0, 0, 0)),
                   pl.BlockSpec((tq, 128), lambda h, s, kj, qi: (kj[s], h)),
                   pl.BlockSpec((tq, 128), lambda h, s, kj, qi: (kj[s], h)),
                   pl.BlockSpec((2, tq, 1), lambda h, s, kj, qi: (h, kj[s], 0)),
                   pl.BlockSpec((2, nq, 1, tq), lambda h, s, kj, qi: (h, 0, 0, 0))],
        scratch_shapes=[pltpu.VMEM((tq, 128), F32), pltpu.VMEM((tq, 128), F32), pltpu.VMEM((2, tq, 128), F32)])
    return _pcall(body, name=name, grid_spec=gs,
                  out_shape=[jax.ShapeDtypeStruct((4, nq, 128, tq), F32), jax.ShapeDtypeStruct((t, BW), BF16),
                             jax.ShapeDtypeStruct((t, BW), BF16), jax.ShapeDtypeStruct((FOX_H, t, 1), F32),
                             jax.ShapeDtypeStruct((FOX_H, nq, 1, tq), F32)],
                  compiler_params=_cp("parallel", "arbitrary"))(kj_tab, qi_tab, zb, zb, zb, dyc, frow, fkb, lse, dl)


def merge_fwd(ya, yb, yc, wbr, z, *, name):
    t = ya.shape[0]
    tm = min(256, t)

    def body(ya_ref, yb_ref, yc_ref, w_ref, g0_ref, g1_ref, g2_ref, o_ref):
        m = _sigmoid(g0_ref[...]) * _dot(ya_ref[...], w_ref[0])
        m = m + _sigmoid(g1_ref[...]) * _dot(yb_ref[...], w_ref[1])
        m = m + _sigmoid(g2_ref[...]) * _dot(yc_ref[...], w_ref[2])
        o_ref[...] = m.astype(BF16)

    return _pcall(body, name=name, grid=(t // tm,),
                  in_specs=[_rows(tm, BW)] * 3 + [_fix((3, BW, D))]
                  + [_rows(tm, D, G0 // D + j) for j in range(3)],
                  out_specs=_rows(tm, D), out_shape=jax.ShapeDtypeStruct((t, D), BF16),
                  compiler_params=_cp("parallel"))(ya, yb, yc, wbr, z, z, z)


def merge_bwd(doutb, wo, l, ya, yb, yc, wbr, z, *, name):
    t = ya.shape[0]
    tm = min(256, t)

    def body(do_ref, wo_ref, ya_ref, yb_ref, yc_ref, w_ref, g0_ref, g1_ref, g2_ref,
             dya_ref, dyb_ref, dyc_ref, dp0_ref, dp1_ref, dp2_ref, dg0_ref, dg1_ref, dg2_ref):
        dm = _dot_nt(do_ref[...], wo_ref[...])
        ys = (ya_ref, yb_ref, yc_ref)
        gs = (g0_ref, g1_ref, g2_ref)
        dys = (dya_ref, dyb_ref, dyc_ref)
        dps = (dp0_ref, dp1_ref, dp2_ref)
        dgs = (dg0_ref, dg1_ref, dg2_ref)
        for j in range(3):
            s = _sigmoid(gs[j][...])
            pj = _dot(ys[j][...], w_ref[j])
            dpb = (dm * s).astype(BF16)
            dps[j][...] = dpb
            dgs[j][...] = (dm * pj * s * (1.0 - s)).astype(BF16)
            dys[j][...] = _dot_nt(dpb, w_ref[j])

    yshape = jax.ShapeDtypeStruct((t, BW), F32)
    dshape = jax.ShapeDtypeStruct((t, D), BF16)
    return _pcall(body, name=name, grid=(t // tm,),
                  in_specs=[_rows(tm, D), _layer(l, (D, D))] + [_rows(tm, BW)] * 3
                  + [_fix((3, BW, D))] + [_rows(tm, D, G0 // D + j) for j in range(3)],
                  out_specs=[_rows(tm, BW)] * 3 + [_rows(tm, D)] * 6,
                  out_shape=[yshape] * 3 + [dshape] * 6,
                  compiler_params=_cp("parallel"))(doutb, wo, ya, yb, yc, wbr, z, z, z)


def adamw(w, g, m, v, *, name):
    r, c = w.shape
    tm = _pick(r, (128, 64, 32, 16, 8))

    def body(w_ref, g_ref, m_ref, v_ref, d_ref, mo_ref, vo_ref):
        gg = g_ref[...]
        mn = ADAM_B1 * m_ref[...] + (1.0 - ADAM_B1) * gg
        vn = ADAM_B2 * v_ref[...] + (1.0 - ADAM_B2) * (gg * gg)
        m_hat = mn / (1.0 - ADAM_B1 ** ADAM_STEP)
        v_hat = vn / (1.0 - ADAM_B2 ** ADAM_STEP)
        d_ref[...] = -ADAM_LR * (m_hat / (jnp.sqrt(v_hat) + ADAM_EPS) + ADAM_WD * w_ref[...])
        mo_ref[...] = mn
        vo_ref[...] = vn

    shp = jax.ShapeDtypeStruct((r, c), F32)
    return _pcall(body, name=name, grid=(r // tm,), in_specs=[_rows(tm, c)] * 4, out_specs=[_rows(tm, c)] * 3,
                  out_shape=[shp] * 3, compiler_params=_cp("parallel"))(w, g, m, v)


def _place():
    return lax.axis_index("x"), lax.axis_index("y"), lax.axis_index("c")


def _remote(src, dst, send_sems, recv_sems, k, to):
    return pltpu.make_async_remote_copy(src_ref=src, dst_ref=dst, send_sem=send_sems.at[k],
                                        recv_sem=recv_sems.at[k], device_id=to, device_id_type=MESH)


def gather_weights(shards):
    n = len(shards)

    def body(*refs):
        ins, outs = refs[:n], refs[n:2 * n]
        send_sems, recv_sems, own_send, own_recv = refs[2 * n:]
        x, y, c = _place()
        sib = (x, y, 1 - c)
        chips = [(1 - x, y), (x, 1 - y), (1 - x, 1 - y)]
        k_me = 2 * x + y
        mine, first, passed = [], [], []
        for t in range(n):
            for l in range(DEPTH):
                mine.append(_remote(ins[t].at[l], outs[t].at[l, k_me], own_send, own_recv, 2 * t + l, sib))
            for j, chip in enumerate(chips):
                first.append(_remote(ins[t].at[c], outs[t].at[c, k_me], send_sems, recv_sems, 6 * t + j, (*chip, c)))
        for cp in mine + first:
            cp.start()
        for t in range(n):
            for j, chip in enumerate(chips):
                blk = outs[t].at[c, 2 * chip[0] + chip[1]]
                _remote(blk, blk, send_sems, recv_sems, 6 * t + j, (*chip, c)).wait_recv()
                cp = _remote(blk, blk, send_sems, recv_sems, 6 * t + 3 + j, sib)
                cp.start()
                passed.append(cp)
        for t in range(n):
            for j, chip in enumerate(chips):
                blk = outs[t].at[1 - c, 2 * chip[0] + chip[1]]
                _remote(blk, blk, send_sems, recv_sems, 6 * t + 3 + j, sib).wait_recv()
        for cp in first + passed:
            cp.wait_send()
        for cp in mine:
            cp.wait()

    return _pcall(body, name="gather_weights", in_specs=[ANY] * n, out_specs=[ANY] * n,
                  out_shape=[jax.ShapeDtypeStruct((DEPTH, 4) + s.shape[1:], s.dtype) for s in shards],
                  scratch_shapes=[pltpu.SemaphoreType.DMA((6 * n,)), pltpu.SemaphoreType.DMA((6 * n,)),
                                  pltpu.SemaphoreType.DMA((2 * n,)), pltpu.SemaphoreType.DMA((2 * n,))])(*shards)


def pair_exchange(g0, g1):
    n = len(g0)

    def body(*refs):
        a0, a1, outs = refs[:n], refs[n:2 * n], refs[2 * n:3 * n]
        send_sems, recv_sems = refs[3 * n:]
        x, y, c = _place()
        sib = (x, y, 1 - c)

        @pl.when(c == 0)
        def _():
            for t in range(n):
                _remote(a1[t], outs[t], send_sems, recv_sems, t, sib).start()

        @pl.when(c == 1)
        def _():
            for t in range(n):
                _remote(a0[t], outs[t], send_sems, recv_sems, t, sib).start()

        for t in range(n):
            _remote(a0[t], outs[t], send_sems, recv_sems, t, sib).wait()

    return _pcall(body, name="pair_exchange", in_specs=[ANY] * (2 * n), out_specs=[ANY] * n,
                  out_shape=[jax.ShapeDtypeStruct(a.shape, a.dtype) for a in g0],
                  scratch_shapes=[pltpu.SemaphoreType.DMA((n,)), pltpu.SemaphoreType.DMA((n,))])(*g0, *g1)


def chip_exchange(s1):
    n = len(s1)

    def body(*refs):
        ins, outs = refs[:n], refs[n:2 * n]
        send_sems, recv_sems = refs[2 * n:]
        x, y, c = _place()
        chips = [(1 - x, y), (x, 1 - y), (1 - x, 1 - y)]
        cps = [_remote(ins[t].at[2 * chip[0] + chip[1]], outs[t].at[j], send_sems, recv_sems, 3 * t + j, (*chip, c))
               for t in range(n) for j, chip in enumerate(chips)]
        for cp in cps:
            cp.start()
        for cp in cps:
            cp.wait()

    return _pcall(body, name="chip_exchange", in_specs=[ANY] * n, out_specs=[ANY] * n,
                  out_shape=[jax.ShapeDtypeStruct((3,) + a.shape[1:], a.dtype) for a in s1],
                  scratch_shapes=[pltpu.SemaphoreType.DMA((3 * n,)), pltpu.SemaphoreType.DMA((3 * n,))])(*s1)


def pair_share(s2):
    n = len(s2)

    def body(*refs):
        ins, outs = refs[:n], refs[n:2 * n]
        send_sems, recv_sems = refs[2 * n:]
        x, y, c = _place()
        sib = (x, y, 1 - c)
        cps = [_remote(ins[t].at[c], outs[t].at[c], send_sems, recv_sems, t, sib) for t in range(n)]
        for cp in cps:
            cp.start()
        for t in range(n):
            cps[t].wait_send()
            _remote(ins[t].at[c], outs[t].at[1 - c], send_sems, recv_sems, t, sib).wait_recv()

    return _pcall(body, name="pair_share", in_specs=[ANY] * n, out_specs=[ANY] * n,
                  out_shape=[jax.ShapeDtypeStruct(a.shape, a.dtype) for a in s2],
                  input_output_aliases={t: t for t in range(n)},
                  scratch_shapes=[pltpu.SemaphoreType.DMA((n,)), pltpu.SemaphoreType.DMA((n,))])(*s2)


def small_exchange(gs):
    rows, width = gs.shape

    def body(g_ref, o_ref, send_sems, recv_sems):
        x, y, c = _place()
        cps = []
        for r in range(1, 8):
            dx, dy, dc = (r >> 2) & 1, (r >> 1) & 1, r & 1
            to = (x if dx == 0 else 1 - x, y if dy == 0 else 1 - y, c if dc == 0 else 1 - c)
            cps.append(_remote(g_ref, o_ref.at[r - 1], send_sems, recv_sems, r - 1, to))
        for cp in cps:
            cp.start()
        for cp in cps:
            cp.wait()

    return _pcall(body, name="small_exchange", in_specs=[ANY], out_specs=ANY,
                  out_shape=jax.ShapeDtypeStruct((7, rows, width), gs.dtype),
                  scratch_shapes=[pltpu.SemaphoreType.DMA((7,)), pltpu.SemaphoreType.DMA((7,))])(gs)


def _row_tile(rows):
    return _pick(rows, (256, 352, 128, 64, 32, 16))


def pair_add(g0, g1, rb, core, *, name):
    _, rows, width = g0.shape
    tr = _row_tile(rows)

    def body(c_ref, g0_ref, g1_ref, r_ref, o_ref, ob_ref):
        s = jnp.where(c_ref[0] == 0, g0_ref[...], g1_ref[...]) + r_ref[...]
        o_ref[...] = s
        ob_ref[...] = s.astype(BF16)

    blk = pl.BlockSpec((None, tr, width), lambda k, i, c_ref: (k, i, 0))
    gs = pltpu.PrefetchScalarGridSpec(
        num_scalar_prefetch=1, grid=(4, rows // tr),
        in_specs=[pl.BlockSpec((None, tr, width), lambda k, i, c_ref: (k * (1 - c_ref[0]), i * (1 - c_ref[0]), 0)),
                  pl.BlockSpec((None, tr, width), lambda k, i, c_ref: (k * c_ref[0], i * c_ref[0], 0)), blk],
        out_specs=[blk, blk])
    return _pcall(body, name=name, grid_spec=gs,
                  out_shape=[jax.ShapeDtypeStruct(g0.shape, F32), jax.ShapeDtypeStruct(g0.shape, BF16)],
                  compiler_params=_cp("parallel", "parallel"))(core, g0, g1, rb)


def chip_add(s1, rb2, chip, core, *, name):
    _, rows, width = s1.shape
    tr = _row_tile(rows)

    def body(k_ref, c_ref, s_ref, r_ref, o_ref):
        o_ref[...] = ((s_ref[...] + r_ref[0].astype(F32)) + r_ref[1].astype(F32)) + r_ref[2].astype(F32)

    gs = pltpu.PrefetchScalarGridSpec(
        num_scalar_prefetch=2, grid=(rows // tr,),
        in_specs=[pl.BlockSpec((None, tr, width), lambda i, k_ref, c_ref: (k_ref[0], i, 0)),
                  pl.BlockSpec((3, tr, width), lambda i, k_ref, c_ref: (0, i, 0))],
        out_specs=pl.BlockSpec((None, tr, width), lambda i, k_ref, c_ref: (c_ref[0], i, 0)))
    return _pcall(body, name=name, grid_spec=gs, out_shape=jax.ShapeDtypeStruct((DEPTH, rows, width), F32),
                  compiler_params=_cp("parallel"))(chip, core, s1, rb2)


def small_add(gs_own, slots, me):
    rows, width = gs_own.shape
    tr = _pick(rows, (64, 32, 16, 8))

    def body(me_ref, g_ref, s_ref, o_ref):
        me_v = me_ref[0]
        total = None
        for d in range(8):
            rel = jnp.bitwise_xor(me_v, d)
            val = jnp.where(rel == 0, g_ref[...], s_ref[jnp.maximum(rel - 1, 0)])
            total = val if total is None else total + val
        o_ref[...] = total

    gs = pltpu.PrefetchScalarGridSpec(
        num_scalar_prefetch=1, grid=(rows // tr,),
        in_specs=[pl.BlockSpec((tr, width), lambda i, m_ref: (i, 0)),
                  pl.BlockSpec((7, tr, width), lambda i, m_ref: (0, i, 0))],
        out_specs=pl.BlockSpec((tr, width), lambda i, m_ref: (i, 0)))
    return _pcall(body, name="small_add", grid_spec=gs, out_shape=jax.ShapeDtypeStruct((rows, width), F32),
                  compiler_params=_cp("parallel"))(me, gs_own, slots)


SHARDED = (("ffn1_w_up", (D, UPW)), ("ffn1_w_down", (DFF // 4, D)), ("w_in", (D, D_IN // 4)),
           ("conv_w", (4, BW // 4)), ("gla_w_g2", (LOW_W, 64)), ("w_branch", (3 * BW, D // 4)),
           ("w_out", (D // 4, D)), ("ffn2_w_up", (D, UPW)), ("ffn2_w_down", (DFF // 4, D)),
           ("ple_w_proj", (PLE, D // 4)), ("ple_w_gate", (D // 4, D)))
SMALL = ("ln1_g", "ln1_b", "conv_b", "lru_wa", "lru_ba", "lru_wx", "lru_bx", "lru_lambda", "gla_b_g",
         "gla_norm_g", "fox_b_f", "ln2_g", "ln2_b", "ln3_g", "ln3_b", "ple_b_gate", "ln4_g", "ln4_b")
WEIGHTS = ('ffn1_w_up', 'ffn1_w_down', 'ln1_g', 'ln1_b', 'w_in', 'conv_w', 'conv_b', 'lru_wa', 'lru_ba',
           'lru_wx', 'lru_bx', 'lru_lambda', 'gla_w_g2', 'gla_b_g', 'gla_norm_g', 'fox_b_f', 'w_branch',
           'w_out', 'ln2_g', 'ln2_b', 'ffn2_w_up', 'ffn2_w_down', 'ln3_g', 'ln3_b', 'ple_w_proj',
           'ple_w_gate', 'ple_b_gate', 'ln4_g', 'ln4_b')


def _pack(parts, rows):
    flat = jnp.concatenate([p.reshape(-1) for p in parts])
    flat = jnp.pad(flat, (0, rows * PACK_W - flat.shape[0]))
    return flat.reshape(rows, PACK_W)


def _cols_join(parts):
    return jnp.concatenate([parts[k] for k in range(4)], axis=-1)


def _cols_split(full):
    r, c4 = full.shape
    return full.reshape(r, 4, c4 // 4).transpose(1, 0, 2)


def _regroup_in(w):
    pad = jnp.zeros(w.shape[:-1] + (ZW - D_IN,), w.dtype)
    fox_q = (w[..., 2576:3088] * FOX_SCALE).astype(w.dtype)
    return jnp.concatenate([w[..., 0:2048], w[..., 2064:2576], fox_q, w[..., 3088:4112], w[..., 4120:7192],
                            w[..., 2048:2064], w[..., 4112:4120], pad], axis=-1)


def _regroup_out(g):
    return jnp.concatenate([g[..., 0:2048], g[..., 7168:7184], g[..., 2048:CQ], g[..., CQ:CK] * FOX_SCALE,
                            g[..., CK:4096], g[..., 7184:7192], g[..., 4096:7168]], axis=-1)


def _block_diag(w):
    eye = jnp.eye(8, dtype=w.dtype)
    return (eye[:, None, :, None] * w[:, :, None, :]).reshape(BW, BW)


def _diag_blocks(dense):
    return jnp.stack([dense[64 * n:64 * (n + 1), 64 * n:64 * (n + 1)] for n in range(8)])


def _layer_weights(gw, small, l):
    w = {"up1": gw["ffn1_w_up"], "up2": gw["ffn2_w_up"],
         "dn1": gw["ffn1_w_down"].reshape(DEPTH, DFF, D), "dn2": gw["ffn2_w_down"].reshape(DEPTH, DFF, D),
         "wo": gw["w_out"].reshape(DEPTH, D, D), "wgt": gw["ple_w_gate"].reshape(DEPTH, D, D)}
    w["win"] = _regroup_in(_cols_join(gw["w_in"][l]))
    w["cw"] = _cols_join(gw["conv_w"][l])
    w["wa"] = _block_diag(small["lru_wa"][l]).astype(BF16)
    w["wx"] = _block_diag(small["lru_wx"][l]).astype(BF16)
    w["wg2p"] = jnp.pad(_cols_join(gw["gla_w_g2"][l]), ((0, 128 - LOW_W), (0, 0)))
    w["wbr"] = _cols_join(gw["w_branch"][l].reshape(4, 3, BW, D // 4))
    w["wp"] = _cols_join(gw["ple_w_proj"][l])
    for n in ("ln1_g", "ln1_b", "ln2_g", "ln2_b", "ln3_g", "ln3_b", "ln4_g", "ln4_b", "conv_b", "lru_ba",
              "lru_bx", "lru_lambda", "gla_b_g", "gla_norm_g", "ple_b_gate"):
        w[n] = small[n][l][None, :]
    w["bfp"] = jnp.pad(small["fox_b_f"][l], (LOW_W, 128 - LOW_W - FOX_H))[None, :]
    return w


def _heads_t(a):
    ht = a[:, LOW_W:LOW_W + FOX_H].T
    return ht[:, None, :], jnp.broadcast_to(ht[:, :, None], ht.shape + (128,))


def _layer_fwd(x, xb, pb, w, l):
    s = {"x0": x, "x0b": xb}
    tag = "l%d_" % l
    gate, up, act = ffn_up(xb, w["up1"], l, name=tag + "ffn1_up")
    r1, x1, x1b = matmul_res_ln(act, w["dn1"], l, x, w["ln1_g"], w["ln1_b"], mm_scale=0.5, name=tag + "ffn1_down")
    s.update(gate1=gate, up1=up, act1=act, r1=r1, x1=x1, x1b=x1b)
    z, zb = matmul(x1b, w["win"], also_bf16=True, tn=_pick(ZW, (2432,)), name=tag + "mix_in")
    xc, xcb, h, ya = lru_fwd(z, w["cw"], w["conv_b"], w["wa"], w["wx"], w["lru_ba"], w["lru_bx"],
                             w["lru_lambda"], name=tag + "lru_fwd")
    yb, states = gla_fwd(z, zb, w["wg2p"], w["gla_b_g"], w["gla_norm_g"], name=tag + "gla_fwd")
    fcum = fox_fcum(z, w["bfp"], name=tag + "fox_fcum")
    fq, fk = _heads_t(fcum)
    yc, ycf, lse = fox_fwd_t(zb, fq, fk, name=tag + "fox_fwd")
    merged = merge_fwd(ya, yb, yc, w["wbr"], z, name=tag + "merge_fwd")
    r2, x2, x2b = matmul_res_ln(merged, w["wo"], l, x1, w["ln2_g"], w["ln2_b"], mm_scale=1.0, name=tag + "mix_out")
    s.update(z=z, zb=zb, xc=xc, xcb=xcb, h=h, ya=ya, yb=yb, states=states, fq=fq, fk=fk, yc=yc, ycf=ycf,
             lse=lse, merged=merged, r2=r2, x2=x2, x2b=x2b)
    gate, up, act = ffn_up(x2b, w["up2"], l, name=tag + "ffn2_up")
    r3, x3, x3b = matmul_res_ln(act, w["dn2"], l, x2, w["ln3_g"], w["ln3_b"], mm_scale=0.5, name=tag + "ffn2_down")
    s.update(gate2=gate, up2=up, act2=act, r3=r3, x3=x3, x3b=x3b)
    r4, x4, x4b = ple_fwd(x3b, x3, pb, w["wgt"], l, w["wp"], w["ple_b_gate"], w["ln4_g"], w["ln4_b"],
                          name=tag + "ple_fwd")
    s.update(r4=r4, pb=pb)
    return x4, x4b, s


def _ffn_bwd(dy, s, w, n, xin_b, l, tag):
    k = {"1": ("r1", "ln1_g", "gate1", "up1", "act1"), "2": ("r3", "ln3_g", "gate2", "up2", "act2")}[n]
    dr, dfb, dg, db = ln_bwd(dy, s[k[0]], w[k[1]], out_scale=0.5, name=tag + "ln_bwd")
    dgate, dup = ffn_down_bwd(dfb, w["dn" + n], l, s[k[2]], s[k[3]], name=tag + "down_bwd")
    dx = ffn_dx(dgate, dup, w["up" + n], l, dr, name=tag + "dx")
    dwup = matmul_tn_up(xin_b, dgate, dup, name=tag + "dw_up")
    dwdn = matmul_tn(s[k[4]], dfb, name=tag + "dw_down").reshape(4, DFF // 4, D)
    return dx, dwup, dwdn, dg[0], db[0]


def _layer_bwd(dy, s, w, l):
    g = {}
    tag = "l%d_" % l
    dr4, dglb, dpeb, dg4, db4, dbg = ple_bwd(dy, s["r4"], s["x3b"], s["pb"], w["wgt"], l, w["wp"], w["ple_b_gate"],
                                             w["ln4_g"], name=tag + "ple_bwd")
    dx3 = matmul(dglb, w["wgt"], nt=True, b_lead=(l,), res=dr4, res_scale=ALPHA, tm=1024, tn=1024,
                 name=tag + "ple_dx")
    g["ple_w_gate"] = matmul_tn(s["x3b"], dglb, name=tag + "ple_dw_gate").reshape(4, D // 4, D)
    g["ple_w_proj"] = _cols_split(matmul_tn(s["pb"], dpeb, name=tag + "ple_dw_proj"))
    g["ln4_g"], g["ln4_b"], g["ple_b_gate"] = dg4[0], db4[0], dbg[0]
    dx2, g["ffn2_w_up"], g["ffn2_w_down"], g["ln3_g"], g["ln3_b"] = _ffn_bwd(dx3, s, w, "2", s["x2b"], l,
                                                                             tag + "ffn2_")
    dr2, doutb, dg2, db2 = ln_bwd(dx2, s["r2"], w["ln2_g"], out_scale=1.0, name=tag + "mix_ln_bwd")
    g["ln2_g"], g["ln2_b"] = dg2[0], db2[0]
    g["w_out"] = matmul_tn(s["merged"], doutb, name=tag + "dw_out").reshape(4, D // 4, D)
    z, zb = s["z"], s["zb"]
    (dya, dyb, dyc, dp0, dp1, dp2, dgl0, dgl1, dgl2) = merge_bwd(
        doutb, w["wo"], l, s["ya"], s["yb"], s["yc"], w["wbr"], z, name=tag + "merge_bwd")
    dwbr = jnp.stack([matmul_tn(s["ya"], dp0, name=tag + "dw_br0"), matmul_tn(s["yb"], dp1, name=tag + "dw_br1"),
                      matmul_tn(s["yc"], dp2, name=tag + "dw_br2")])
    g["w_branch"] = _cols_split(dwbr.reshape(3 * BW, D))
    day, dxc, dprb, dpib, dba, dbx, dlam = lru_bwd(dya, z, s["h"], s["xc"], w["wa"], w["wx"],
                                                   w["lru_ba"], w["lru_bx"], w["lru_lambda"], name=tag + "lru_bwd")
    dax, dcw, dcb = conv_bwd(dxc, z, w["cw"], name=tag + "conv_bwd")
    g["lru_wa"] = _diag_blocks(matmul_tn(s["xcb"], dprb, name=tag + "dw_lru_a"))
    g["lru_wx"] = _diag_blocks(matmul_tn(s["xcb"], dpib, name=tag + "dw_lru_x"))
    g["lru_ba"], g["lru_bx"], g["lru_lambda"] = dba[0], dbx[0], dlam[0]
    g["conv_w"], g["conv_b"] = _cols_split(dcw), dcb[0]
    dbq, dbk, dbv, dbr, dmisc_g, dpreb, dbgg, dng = gla_bwd(dyb, z, zb, s["states"], w["wg2p"], w["gla_b_g"],
                                                            w["gla_norm_g"], name=tag + "gla_bwd")
    miscb = zb[:, MISC:]
    g["gla_w_g2"] = _cols_split(matmul_tn(miscb, dpreb, name=tag + "dw_g2")[:LOW_W])
    g["gla_b_g"], g["gla_norm_g"] = dbgg[0], dng[0]
    dl = fox_delta(dyc, s["ycf"], name=tag + "fox_delta")
    t = z.shape[0]
    dlq = dl[:, :FOX_H].T[:, None, :]
    dcq, dck, dcv, dfk, dfq = fox_bwd_t(zb, dyc, s["fq"], s["fk"], s["lse"], dlq, name=tag + "fox_bwd")
    dcq = dcq.transpose(1, 3, 0, 2).reshape(t, BW)
    dfc = jnp.pad((dfk[:, :, 0] + dfq.reshape(FOX_H, t)).T, ((0, 0), (LOW_W, 128 - LOW_W - FOX_H)))
    dmiscb, dbf = fox_dcf(dfc, z, w["bfp"], dmisc_g, name=tag + "fox_dcf")
    g["fox_b_f"] = dbf[0, LOW_W:LOW_W + FOX_H]
    dz = jnp.concatenate([dax, day, dbq, dbk, dbv, dbr, dcq.astype(BF16), dck, dcv, dgl0, dgl1, dgl2, dmiscb],
                         axis=1)
    dx1 = matmul(dz, w["win"], nt=True, res=dr2, res_scale=ALPHA, tm=1024, tn=1024, tk=_pick(ZW, (2432,)),
                 name=tag + "mix_dx")
    g["w_in"] = _cols_split(_regroup_out(matmul_tn(s["x1b"], dz, name=tag + "dw_in")))
    dx0, g["ffn1_w_up"], g["ffn1_w_down"], g["ln1_g"], g["ln1_b"] = _ffn_bwd(dx1, s, w, "1", s["x0b"], l,
                                                                             tag + "ffn1_")
    return dx0, g


def _local_step(x, p, target, gw, small):
    xcur = x
    xb = xcur.astype(BF16)
    layer_w, saved = [], []
    for l in range(DEPTH):
        w = _layer_weights(gw, small, l)
        xcur, xb, s = _layer_fwd(xcur, xb, p[l].astype(BF16), w, l)
        layer_w.append(w)
        saved.append(s)
    dy, sq = loss_head(xcur, target, name="loss_head")
    grads = [None] * DEPTH
    for l in reversed(range(DEPTH)):
        dy, grads[l] = _layer_bwd(dy, saved[l], layer_w[l], l)
    return 0.5 * jnp.sum(sq) / float(D), dy, grads


def kernel(x, p, ffn1_w_up, ffn1_w_down, ln1_g, ln1_b, w_in, conv_w, conv_b, lru_wa, lru_ba, lru_wx, lru_bx, lru_lambda, gla_w_g2, gla_b_g, gla_norm_g, fox_b_f, w_branch, w_out, ln2_g, ln2_b, ffn2_w_up, ffn2_w_down, ln3_g, ln3_b, ple_w_proj, ple_w_gate, ple_b_gate, ln4_g, ln4_b, loss_target, m_ffn1_w_up, m_ffn1_w_down, m_ln1_g, m_ln1_b, m_w_in, m_conv_w, m_conv_b, m_lru_wa, m_lru_ba, m_lru_wx, m_lru_bx, m_lru_lambda, m_gla_w_g2, m_gla_b_g, m_gla_norm_g, m_fox_b_f, m_w_branch, m_w_out, m_ln2_g, m_ln2_b, m_ffn2_w_up, m_ffn2_w_down, m_ln3_g, m_ln3_b, m_ple_w_proj, m_ple_w_gate, m_ple_b_gate, m_ln4_g, m_ln4_b, v_ffn1_w_up, v_ffn1_w_down, v_ln1_g, v_ln1_b, v_w_in, v_conv_w, v_conv_b, v_lru_wa, v_lru_ba, v_lru_wx, v_lru_bx, v_lru_lambda, v_gla_w_g2, v_gla_b_g, v_gla_norm_g, v_fox_b_f, v_w_branch, v_w_out, v_ln2_g, v_ln2_b, v_ffn2_w_up, v_ffn2_w_down, v_ln3_g, v_ln3_b, v_ple_w_proj, v_ple_w_gate, v_ple_b_gate, v_ln4_g, v_ln4_b):
    args = dict(locals())
    wts = {n: args[n] for n in WEIGHTS}
    mom = {n: args["m_" + n] for n in WEIGHTS}
    var = {n: args["v_" + n] for n in WEIGHTS}
    cx, cy, cc = lax.axis_index("x"), lax.axis_index("y"), lax.axis_index("c")

    shards = [wts[n].reshape((DEPTH,) + rc).astype(F32 if n == "conv_w" else BF16) for n, rc in SHARDED]
    gw = dict(zip([n for n, _ in SHARDED], gather_weights(shards)))
    small = {n: wts[n] for n in SMALL}

    loss_local, dx, grads = _local_step(x[0], p[:, 0], loss_target[0], gw, small)
    loss = lax.psum(loss_local, ("x", "y", "c"))
    grad_x = dx[None]

    core = jnp.reshape(cc, (1,)).astype(jnp.int32)
    chip = jnp.reshape(2 * cx + cy, (1,)).astype(jnp.int32)
    g0 = [grads[0][n] for n, _ in SHARDED]
    g1 = [grads[1][n] for n, _ in SHARDED]
    rb = pair_exchange(g0, g1)
    s1 = [pair_add(a0, a1, r, core, name="pair_add_" + n) for (n, _), a0, a1, r in zip(SHARDED, g0, g1, rb)]
    rb2 = chip_exchange([sb for _, sb in s1])
    s2 = [chip_add(sf, r, chip, core, name="chip_add_" + n) for (n, _), (sf, _), r in zip(SHARDED, s1, rb2)]
    gout = {n: gsh.reshape(wts[n].shape) for (n, _), gsh in zip(SHARDED, pair_share(s2))}

    small_sizes = [wts[n].size for n in SMALL]
    srows = -(-sum(small_sizes) // (8 * PACK_W)) * 8
    gs = _pack([jnp.stack([grads[l][n] for l in range(DEPTH)]) for n in SMALL], srows)
    me = jnp.reshape(4 * cx + 2 * cy + cc, (1,)).astype(jnp.int32)
    gsum = small_add(gs, small_exchange(gs), me)

    delta, new_m, new_v = {}, {}, {}
    for n, (_, cols) in SHARDED:
        shp = wts[n].shape
        v2 = lambda a: a.reshape(-1, cols)
        d, mn, vn = adamw(v2(wts[n]), v2(gout[n]), v2(mom[n]), v2(var[n]), name="adamw_" + n)
        delta[n], new_m[n], new_v[n] = d.reshape(shp), mn.reshape(shp), vn.reshape(shp)
    d, mn, vn = adamw(_pack([wts[n] for n in SMALL], srows), gsum, _pack([mom[n] for n in SMALL], srows),
                      _pack([var[n] for n in SMALL], srows), name="adamw_small")
    off = 0
    for n, size in zip(SMALL, small_sizes):
        shp = wts[n].shape
        take = lambda a: a.reshape(-1)[off:off + size].reshape(shp)
        gout[n], delta[n], new_m[n], new_v[n] = take(gsum), take(d), take(mn), take(vn)
        off += size

    return (loss, grad_x, *[gout[n] for n in WEIGHTS], *[delta[n] for n in WEIGHTS],
            *[new_m[n] for n in WEIGHTS], *[new_v[n] for n in WEIGHTS])
```

```python
import functools
import math

import jax
import jax.numpy as jnp
from jax import lax
from jax.experimental import pallas as pl
from jax.experimental.pallas import tpu as pltpu

F32 = jnp.float32
BF16 = jnp.bfloat16

D = 1024
DFF = 2816
BW = 512
PLE = 256
DEPTH = 2
ALPHA = (2 * DEPTH) ** 0.25
LN_EPS = 1e-5
RMS_EPS = 1e-6
LRU_C = 8.0
GLA_TAU = 16.0
CHUNK = 64
D_IN = 7192
ZW = 7296
AX, AY, BQ, BK, BV, BR, CQ, CK, CV, G0, MISC = 0, 512, 1024, 1280, 1536, 2048, 2560, 3072, 3584, 4096, 7168
LOW_W, FOX_H = 16, 8
ADAM_LR, ADAM_B1, ADAM_B2, ADAM_EPS, ADAM_WD, ADAM_STEP = 0.001, 0.9, 0.999, 1e-08, 0.01, 10
PACK_W = 1024
VMEM_LIMIT = 56 << 20

MESH = pl.DeviceIdType.MESH
ANY = pl.BlockSpec(memory_space=pl.ANY)


def _pcall(body, **kw):
    return pl.pallas_call(body, **kw)


def _cp(*dims):
    return pltpu.CompilerParams(dimension_semantics=dims, vmem_limit_bytes=VMEM_LIMIT)


def _dot(a, b):
    return jnp.dot(a, b, preferred_element_type=F32)


def _dot_nt(a, b):
    return lax.dot_general(a, b, (((1,), (1,)), ((), ())), preferred_element_type=F32)


def _dot_tn(a, b):
    return lax.dot_general(a, b, (((0,), (0,)), ((), ())), preferred_element_type=F32)


def _dot_hi(a, b):
    return jnp.dot(a, b, preferred_element_type=F32, precision=lax.Precision.HIGHEST)


def _sigmoid(x):
    return 1.0 / (1.0 + jnp.exp(-x))


def _softplus(x):
    return jnp.maximum(x, 0.0) + jnp.log(1.0 + jnp.exp(-jnp.abs(x)))


def _log_sigmoid(x):
    return -_softplus(-x)


def _expm1(x):
    poly = x * (1.0 + x * (0.5 + x * (1.0 / 6.0 + x * (1.0 / 24.0 + x * (1.0 / 120.0 + x * (1.0 / 720.0))))))
    return jnp.where(jnp.abs(x) < 0.1, poly, jnp.exp(x) - 1.0)


_GELU_C = math.sqrt(2.0 / math.pi)


def _gelu(x):
    return 0.5 * x * (1.0 + jnp.tanh(_GELU_C * (x + 0.044715 * x * x * x)))


def _gelu_grad(x):
    t = jnp.tanh(_GELU_C * (x + 0.044715 * x * x * x))
    return 0.5 * (1.0 + t) + 0.5 * x * (1.0 - t * t) * _GELU_C * (1.0 + 3.0 * 0.044715 * x * x)


def _ln_stats(r):
    mu = jnp.mean(r, axis=-1, keepdims=True)
    xc = r - mu
    var = jnp.mean(xc * xc, axis=-1, keepdims=True)
    return xc, lax.rsqrt(var + LN_EPS)


def _pick(n, cands):
    for c in cands:
        if n % c == 0:
            return c
    return n


def _rows(tm, w, col=0):
    return pl.BlockSpec((tm, w), lambda i: (i, col))


def _fix(shape):
    nd = len(shape)
    return pl.BlockSpec(shape, lambda i: (0,) * nd)


def _col_chunks(n, width=256):
    return [slice(c, min(c + width, n)) for c in range(0, n, width)]


def _layer(l, shape):
    nd = len(shape)
    return pl.BlockSpec((None,) + tuple(shape), lambda i: (l,) + (0,) * nd)


def matmul(a, b, *, name, nt=False, b_lead=(), res=None, res_scale=1.0, also_bf16=False, tm=512, tn=512,
           tk=None):
    m, k = a.shape
    n = b.shape[-2] if nt else b.shape[-1]
    tm, tn = min(tm, m), min(tn, n)
    tk = k if tk is None else tk
    nk = k // tk
    has_res = res is not None
    lead = tuple(b_lead)
    dot = _dot_nt if nt else _dot

    def body(*refs):
        a_ref, b_ref = refs[0], refs[1]
        pos = 2
        r_ref = None
        if has_res:
            r_ref = refs[pos]
            pos += 1
        o_ref = refs[pos]
        pos += 1
        ob_ref = None
        if also_bf16:
            ob_ref = refs[pos]
            pos += 1
        acc = refs[pos]
        kk = pl.program_id(2)

        @pl.when(kk == 0)
        def _():
            acc[...] = jnp.zeros_like(acc)

        acc[...] += dot(a_ref[...], b_ref[...])

        @pl.when(kk == nk - 1)
        def _():
            v = acc[...]
            if has_res:
                v = v + res_scale * r_ref[...]
            o_ref[...] = v
            if also_bf16:
                ob_ref[...] = v.astype(BF16)

    none = (None,) * len(lead)
    if nt:
        b_spec = pl.BlockSpec(none + (tn, tk), lambda j, i, kk: lead + (j, kk))
    else:
        b_spec = pl.BlockSpec(none + (tk, tn), lambda j, i, kk: lead + (kk, j))
    in_specs = [pl.BlockSpec((tm, tk), lambda j, i, kk: (i, kk)), b_spec]
    args = [a, b]
    if has_res:
        in_specs.append(pl.BlockSpec((tm, tn), lambda j, i, kk: (i, j)))
        args.append(res)
    out_shape = [jax.ShapeDtypeStruct((m, n), F32)]
    out_specs = [pl.BlockSpec((tm, tn), lambda j, i, kk: (i, j))]
    if also_bf16:
        out_shape.append(jax.ShapeDtypeStruct((m, n), BF16))
        out_specs.append(pl.BlockSpec((tm, tn), lambda j, i, kk: (i, j)))
    out = _pcall(body, name=name, grid=(n // tn, m // tm, nk), in_specs=in_specs, out_specs=out_specs,
                 out_shape=out_shape, scratch_shapes=[pltpu.VMEM((tm, tn), F32)],
                 compiler_params=_cp("parallel", "parallel", "arbitrary"))(*args)
    return out if also_bf16 else out[0]


def matmul_tn(a, b, *, name):
    t, k = a.shape
    n = b.shape[1]
    tt = min(2048, t)
    tk = _pick(k, (512, 1408, 256, 128))
    tn = _pick(n, (1024, 1408, 2432, 512, 256, 128))
    nt = t // tt

    def body(a_ref, b_ref, o_ref):
        @pl.when(pl.program_id(2) == 0)
        def _():
            o_ref[...] = jnp.zeros_like(o_ref)

        o_ref[...] += _dot_tn(a_ref[...], b_ref[...])

    return _pcall(body, name=name, grid=(k // tk, n // tn, nt),
                  in_specs=[pl.BlockSpec((tt, tk), lambda i, j, s: (s, i)),
                            pl.BlockSpec((tt, tn), lambda i, j, s: (s, j))],
                  out_specs=pl.BlockSpec((tk, tn), lambda i, j, s: (i, j)),
                  out_shape=jax.ShapeDtypeStruct((k, n), F32),
                  compiler_params=_cp("parallel", "parallel", "arbitrary"))(a, b)


UPW = 1408


def matmul_tn_up(a, dgate, dup, *, name):
    t, k = a.shape
    tt = min(2048, t)
    tk = 512

    def body(a_ref, g_ref, u_ref, o_ref):
        j = pl.program_id(1)

        @pl.when(pl.program_id(2) == 0)
        def _():
            o_ref[...] = jnp.zeros_like(o_ref)

        @pl.when(j < 2)
        def _():
            o_ref[...] += _dot_tn(a_ref[...], g_ref[...])

        @pl.when(j >= 2)
        def _():
            o_ref[...] += _dot_tn(a_ref[...], u_ref[...])

    return _pcall(body, name=name, grid=(k // tk, 4, t // tt),
                  in_specs=[pl.BlockSpec((tt, tk), lambda i, j, s: (s, i)),
                            pl.BlockSpec((tt, UPW), lambda i, j, s: (jnp.where(j < 2, s, 0), jnp.minimum(j, 1))),
                            pl.BlockSpec((tt, UPW), lambda i, j, s: (jnp.where(j >= 2, s, 0), jnp.maximum(j - 2, 0)))],
                  out_specs=pl.BlockSpec((None, tk, UPW), lambda i, j, s: (j, i, 0)),
                  out_shape=jax.ShapeDtypeStruct((4, k, UPW), F32),
                  compiler_params=_cp("parallel", "parallel", "arbitrary"))(a, dgate, dup)


def ffn_dx(dgate, dup, wup, l, res, *, name):
    t = dgate.shape[0]
    tm, tn = min(1024, t), 1024

    def body(g_ref, u_ref, w_ref, r_ref, o_ref, acc):
        kk = pl.program_id(2)

        @pl.when(kk == 0)
        def _():
            acc[...] = jnp.zeros_like(acc)

        @pl.when(kk < 2)
        def _():
            acc[...] += _dot_nt(g_ref[...], w_ref[...])

        @pl.when(kk >= 2)
        def _():
            acc[...] += _dot_nt(u_ref[...], w_ref[...])

        @pl.when(kk == 3)
        def _():
            o_ref[...] = acc[...] + ALPHA * r_ref[...]

    return _pcall(body, name=name, grid=(D // tn, t // tm, 4),
                  in_specs=[pl.BlockSpec((tm, UPW), lambda j, i, kk: (i, jnp.minimum(kk, 1))),
                            pl.BlockSpec((tm, UPW), lambda j, i, kk: (i, jnp.maximum(kk - 2, 0))),
                            pl.BlockSpec((None, None, tn, UPW), lambda j, i, kk: (l, kk, j, 0)),
                            pl.BlockSpec((tm, tn), lambda j, i, kk: (i, j))],
                  out_specs=pl.BlockSpec((tm, tn), lambda j, i, kk: (i, j)),
                  out_shape=jax.ShapeDtypeStruct((t, D), F32),
                  scratch_shapes=[pltpu.VMEM((tm, tn), F32)],
                  compiler_params=_cp("parallel", "parallel", "arbitrary"))(dgate, dup, wup, res)


def ffn_up(xb, wup, l, *, name):
    t = xb.shape[0]
    tm, tn = min(512, t), UPW

    def body(x_ref, wg_ref, wu_ref, g_ref, u_ref, a_ref):
        x = x_ref[...]
        for cols in _col_chunks(tn):
            g = _dot(x, wg_ref[:, cols])
            u = _dot(x, wu_ref[:, cols])
            g_ref[:, cols] = g.astype(BF16)
            u_ref[:, cols] = u.astype(BF16)
            a_ref[:, cols] = (g * _sigmoid(g) * u).astype(BF16)

    blk = pl.BlockSpec((tm, tn), lambda j, i: (i, j))
    return _pcall(body, name=name, grid=(DFF // tn, t // tm),
                  in_specs=[pl.BlockSpec((tm, D), lambda j, i: (i, 0)),
                            pl.BlockSpec((None, None, D, tn), lambda j, i: (l, j, 0, 0)),
                            pl.BlockSpec((None, None, D, tn), lambda j, i: (l, 2 + j, 0, 0))],
                  out_specs=[blk, blk, blk],
                  out_shape=[jax.ShapeDtypeStruct((t, DFF), BF16)] * 3,
                  compiler_params=_cp("parallel", "parallel"))(xb, wup, wup)


def matmul_res_ln(a, w, l, res, g, b, *, mm_scale, name):
    t, k = a.shape
    tm = min(256, t)

    def body(a_ref, w_ref, res_ref, g_ref, b_ref, r_ref, y_ref, yb_ref):
        f = _dot(a_ref[...], w_ref[...])
        r = ALPHA * res_ref[...] + mm_scale * f
        xc, rstd = _ln_stats(r)
        y = xc * rstd * g_ref[...] + b_ref[...]
        r_ref[...] = r
        y_ref[...] = y
        yb_ref[...] = y.astype(BF16)

    return _pcall(body, name=name, grid=(t // tm,),
                  in_specs=[_rows(tm, k), _layer(l, (k, D)), _rows(tm, D), _fix((1, D)), _fix((1, D))],
                  out_specs=[_rows(tm, D)] * 3,
                  out_shape=[jax.ShapeDtypeStruct((t, D), F32), jax.ShapeDtypeStruct((t, D), F32),
                             jax.ShapeDtypeStruct((t, D), BF16)],
                  compiler_params=_cp("parallel"))(a, w, res, g, b)


def ln_bwd(dy, r, g, *, out_scale, name):
    t = dy.shape[0]
    tm = min(256, t)

    def body(dy_ref, r_ref, g_ref, dr_ref, drb_ref, dg_ref, db_ref):
        @pl.when(pl.program_id(0) == 0)
        def _():
            dg_ref[...] = jnp.zeros_like(dg_ref)
            db_ref[...] = jnp.zeros_like(db_ref)

        xc, rstd = _ln_stats(r_ref[...])
        xhat = xc * rstd
        d = dy_ref[...]
        dxh = d * g_ref[...]
        dr = rstd * (dxh - jnp.mean(dxh, axis=-1, keepdims=True)
                     - xhat * jnp.mean(dxh * xhat, axis=-1, keepdims=True))
        dr_ref[...] = dr
        drb_ref[...] = (out_scale * dr).astype(BF16)
        dg_ref[...] += jnp.sum(d * xhat, axis=0, keepdims=True)
        db_ref[...] += jnp.sum(d, axis=0, keepdims=True)

    return _pcall(body, name=name, grid=(t // tm,),
                  in_specs=[_rows(tm, D), _rows(tm, D), _fix((1, D))],
                  out_specs=[_rows(tm, D), _rows(tm, D), _fix((1, D)), _fix((1, D))],
                  out_shape=[jax.ShapeDtypeStruct((t, D), F32), jax.ShapeDtypeStruct((t, D), BF16),
                             jax.ShapeDtypeStruct((1, D), F32), jax.ShapeDtypeStruct((1, D), F32)],
                  compiler_params=_cp("arbitrary"))(dy, r, g)


def ffn_down_bwd(dfb, wd, l, gate, up, *, name):
    t = dfb.shape[0]
    tm, tn = min(512, t), UPW
    nj = DFF // tn

    def body(df_ref, w_ref, g_ref, u_ref, dg_ref, du_ref):
        df = df_ref[...]
        for cols in _col_chunks(tn):
            da = _dot_nt(df, w_ref[cols, :])
            g = g_ref[:, cols].astype(F32)
            s = _sigmoid(g)
            gs = g * s
            dg_ref[:, cols] = (da * u_ref[:, cols].astype(F32) * (s + gs * (1.0 - s))).astype(BF16)
            du_ref[:, cols] = (da * gs).astype(BF16)

    blk = pl.BlockSpec((tm, tn), lambda j, i: (i, j))
    return _pcall(body, name=name, grid=(nj, t // tm),
                  in_specs=[pl.BlockSpec((tm, D), lambda j, i: (i, 0)),
                            pl.BlockSpec((None, tn, D), lambda j, i: (l, j, 0)), blk, blk],
                  out_specs=[blk, blk],
                  out_shape=[jax.ShapeDtypeStruct((t, DFF), BF16), jax.ShapeDtypeStruct((t, DFF), BF16)],
                  compiler_params=_cp("parallel", "parallel"))(dfb, wd, gate, up)


def ple_fwd(xb, x, pb, wgate, l, wproj, bgate, g, b, *, name):
    t = x.shape[0]
    tm = min(256, t)

    def body(xb_ref, x_ref, p_ref, wg_ref, wp_ref, bg_ref, g_ref, b_ref, r_ref, y_ref, yb_ref):
        gl = _dot(xb_ref[...], wg_ref[...]) + bg_ref[...]
        pe = _dot(p_ref[...], wp_ref[...])
        r = ALPHA * x_ref[...] + _sigmoid(gl) * pe
        xc, rstd = _ln_stats(r)
        y = xc * rstd * g_ref[...] + b_ref[...]
        r_ref[...] = r
        y_ref[...] = y
        yb_ref[...] = y.astype(BF16)

    return _pcall(body, name=name, grid=(t // tm,),
                  in_specs=[_rows(tm, D), _rows(tm, D), _rows(tm, PLE), _layer(l, (D, D)), _fix((PLE, D)),
                            _fix((1, D)), _fix((1, D)), _fix((1, D))],
                  out_specs=[_rows(tm, D)] * 3,
                  out_shape=[jax.ShapeDtypeStruct((t, D), F32), jax.ShapeDtypeStruct((t, D), F32),
                             jax.ShapeDtypeStruct((t, D), BF16)],
                  compiler_params=_cp("parallel"))(xb, x, pb, wgate, wproj, bgate, g, b)


def ple_bwd(dy, r, xb, pb, wgate, l, wproj, bgate, g, *, name):
    t = dy.shape[0]
    tm = min(256, t)

    def body(dy_ref, r_ref, xb_ref, p_ref, wg_ref, wp_ref, bg_ref, g_ref,
             dr_ref, dgl_ref, dpe_ref, dg_ref, db_ref, dbg_ref):
        @pl.when(pl.program_id(0) == 0)
        def _():
            dg_ref[...] = jnp.zeros_like(dg_ref)
            db_ref[...] = jnp.zeros_like(db_ref)
            dbg_ref[...] = jnp.zeros_like(dbg_ref)

        xc, rstd = _ln_stats(r_ref[...])
        xhat = xc * rstd
        d = dy_ref[...]
        dxh = d * g_ref[...]
        dr = rstd * (dxh - jnp.mean(dxh, axis=-1, keepdims=True)
                     - xhat * jnp.mean(dxh * xhat, axis=-1, keepdims=True))
        s = _sigmoid(_dot(xb_ref[...], wg_ref[...]) + bg_ref[...])
        pe = _dot(p_ref[...], wp_ref[...])
        dgl = dr * pe * s * (1.0 - s)
        dr_ref[...] = dr
        dgl_ref[...] = dgl.astype(BF16)
        dpe_ref[...] = (dr * s).astype(BF16)
        dg_ref[...] += jnp.sum(d * xhat, axis=0, keepdims=True)
        db_ref[...] += jnp.sum(d, axis=0, keepdims=True)
        dbg_ref[...] += jnp.sum(dgl, axis=0, keepdims=True)

    vec = jax.ShapeDtypeStruct((1, D), F32)
    return _pcall(body, name=name, grid=(t // tm,),
                  in_specs=[_rows(tm, D), _rows(tm, D), _rows(tm, D), _rows(tm, PLE), _layer(l, (D, D)),
                            _fix((PLE, D)), _fix((1, D)), _fix((1, D))],
                  out_specs=[_rows(tm, D), _rows(tm, D), _rows(tm, D), _fix((1, D)), _fix((1, D)), _fix((1, D))],
                  out_shape=[jax.ShapeDtypeStruct((t, D), F32), jax.ShapeDtypeStruct((t, D), BF16),
                             jax.ShapeDtypeStruct((t, D), BF16), vec, vec, vec],
                  compiler_params=_cp("arbitrary"))(dy, r, xb, pb, wgate, wproj, bgate, g)


def loss_head(y, tgt, *, name):
    t = y.shape[0]
    tm = min(256, t)

    def body(y_ref, t_ref, dy_ref, sq_ref):
        @pl.when(pl.program_id(0) == 0)
        def _():
            sq_ref[...] = jnp.zeros_like(sq_ref)

        e = y_ref[...] - t_ref[...]
        dy_ref[...] = e / float(D)
        sq_ref[...] += jnp.sum(e * e, axis=0, keepdims=True)

    return _pcall(body, name=name, grid=(t // tm,),
                  in_specs=[_rows(tm, D), _rows(tm, D)],
                  out_specs=[_rows(tm, D), _fix((1, D))],
                  out_shape=[jax.ShapeDtypeStruct((t, D), F32), jax.ShapeDtypeStruct((1, D), F32)],
                  compiler_params=_cp("arbitrary"))(y, tgt)


def _lru_gates(xc, wa_ref, wx_ref, ba_ref, bx_ref, lam_ref):
    xcb = xc.astype(BF16)
    r = _sigmoid(_dot(xcb, wa_ref[...]) + ba_ref[...])
    ig = _sigmoid(_dot(xcb, wx_ref[...]) + bx_ref[...])
    sp = _softplus(-lam_ref[...])
    la = -LRU_C * r * sp
    a = jnp.exp(la)
    mult = jnp.sqrt(-_expm1(2.0 * la))
    return r, ig, sp, la, a, mult


def lru_fwd(z, cw, cb, wa, wx, ba, bx, lam, *, name):
    t = z.shape[0]
    tm = min(256, t)
    hb = tm // 8

    def body(ax_ref, prev_ref, ay_ref, cw_ref, cb_ref, wa_ref, wx_ref, ba_ref, bx_ref, lam_ref,
             xc_ref, xcb_ref, h_ref, ya_ref, xs, a_s, b_s, hc):
        i = pl.program_id(0)

        @pl.when(i == 0)
        def _():
            hc[...] = jnp.zeros_like(hc)

        xs[0:8, :] = jnp.where(i == 0, 0.0, prev_ref[...])
        xs[8:, :] = ax_ref[...]
        xc = cb_ref[...] + cw_ref[0:1, :] * xs[5:5 + tm, :]
        for k in range(1, 4):
            xc = xc + cw_ref[k:k + 1, :] * xs[5 + k:5 + k + tm, :]
        r, ig, sp, la, a, mult = _lru_gates(xc, wa_ref, wx_ref, ba_ref, bx_ref, lam_ref)
        a_s[...] = a
        b_s[...] = mult * (ig * xc)
        xc_ref[...] = xc
        xcb_ref[...] = xc.astype(BF16)

        def step(g, h):
            base = pl.multiple_of(g * 8, 8)
            a8 = a_s[pl.ds(base, 8), :]
            b8 = b_s[pl.ds(base, 8), :]
            for j in range(8):
                h = a8[j:j + 1, :] * h + b8[j:j + 1, :]
                h_ref[pl.ds(base + j, 1), :] = h
            return h

        hc[...] = lax.fori_loop(0, tm // 8, step, hc[...])
        ya_ref[...] = (_gelu(ay_ref[...]) * h_ref[...]).astype(BF16)

    vec = _fix((1, BW))
    return _pcall(body, name=name, grid=(t // tm,),
                  in_specs=[_rows(tm, BW, AX // BW),
                            pl.BlockSpec((8, BW), lambda i: (jnp.maximum(i * hb - 1, 0), AX // BW)),
                            _rows(tm, BW, AY // BW), _fix((4, BW)), vec, _fix((BW, BW)), _fix((BW, BW)),
                            vec, vec, vec],
                  out_specs=[_rows(tm, BW)] * 4,
                  out_shape=[jax.ShapeDtypeStruct((t, BW), F32), jax.ShapeDtypeStruct((t, BW), BF16),
                             jax.ShapeDtypeStruct((t, BW), F32), jax.ShapeDtypeStruct((t, BW), BF16)],
                  scratch_shapes=[pltpu.VMEM((tm + 8, BW), F32), pltpu.VMEM((tm, BW), F32),
                                  pltpu.VMEM((tm, BW), F32), pltpu.VMEM((1, BW), F32)],
                  compiler_params=_cp("arbitrary"))(z, z, z, cw, cb, wa, wx, ba, bx, lam)


def lru_bwd(dya, z, h, xc, wa, wx, ba, bx, lam, *, name):
    t = dya.shape[0]
    tm = min(256, t)
    nb = t // tm
    hb = tm // 8

    def body(dya_ref, ay_ref, h_ref, hprev_ref, xc_ref, wa_ref, wx_ref, ba_ref, bx_ref,
             lam_ref, day_ref, dxc_ref, dpr_ref, dpi_ref, dba_ref, dbx_ref, dlam_ref,
             hs, a_s, g_s, d_s, cc):
        i = pl.program_id(0)

        @pl.when(i == 0)
        def _():
            cc[...] = jnp.zeros_like(cc)
            dba_ref[...] = jnp.zeros_like(dba_ref)
            dbx_ref[...] = jnp.zeros_like(dbx_ref)
            dlam_ref[...] = jnp.zeros_like(dlam_ref)

        xc = xc_ref[...]
        r, ig, sp, la, a, mult = _lru_gates(xc, wa_ref, wx_ref, ba_ref, bx_ref, lam_ref)
        ay = ay_ref[...]
        dya = dya_ref[...]
        hcur = h_ref[...]
        day_ref[...] = (dya * hcur * _gelu_grad(ay)).astype(BF16)
        a_s[...] = a
        g_s[...] = dya * _gelu(ay)

        def step(gg, cin):
            g = tm // 8 - 1 - gg
            base = pl.multiple_of(g * 8, 8)
            a8 = a_s[pl.ds(base, 8), :]
            g8 = g_s[pl.ds(base, 8), :]
            for j in range(7, -1, -1):
                d = g8[j:j + 1, :] + cin
                d_s[pl.ds(base + j, 1), :] = d
                cin = a8[j:j + 1, :] * d
            return cin

        cc[...] = lax.fori_loop(0, tm // 8, step, cc[...])
        dht = d_s[...]
        hs[0:8, :] = jnp.where(i == nb - 1, 0.0, hprev_ref[...])
        hs[8:, :] = hcur
        da = dht * hs[7:7 + tm, :]
        dmult = dht * ig * xc
        dig = dht * mult * xc
        dla = da * a - dmult * a * a / mult
        dpr = dla * (-LRU_C * sp) * r * (1.0 - r)
        dpi = dig * ig * (1.0 - ig)
        dprb = dpr.astype(BF16)
        dpib = dpi.astype(BF16)
        dxc_ref[...] = dht * mult * ig + _dot_nt(dprb, wa_ref[...]) + _dot_nt(dpib, wx_ref[...])
        dpr_ref[...] = dprb
        dpi_ref[...] = dpib
        dba_ref[...] += jnp.sum(dpr, axis=0, keepdims=True)
        dbx_ref[...] += jnp.sum(dpi, axis=0, keepdims=True)
        dlam_ref[...] += jnp.sum(dla * (-LRU_C * r), axis=0, keepdims=True) * (-_sigmoid(-lam_ref[...]))

    vec = _fix((1, BW))
    mat = _fix((BW, BW))
    rev = lambda col: pl.BlockSpec((tm, BW), lambda i: (nb - 1 - i, col))
    vshape = jax.ShapeDtypeStruct((1, BW), F32)
    return _pcall(body, name=name, grid=(nb,),
                  in_specs=[rev(0), rev(AY // BW), rev(0),
                            pl.BlockSpec((8, BW), lambda i: (jnp.maximum((nb - 1 - i) * hb - 1, 0), 0)),
                            rev(0), mat, mat, vec, vec, vec],
                  out_specs=[rev(0), rev(0), rev(0), rev(0), vec, vec, vec],
                  out_shape=[jax.ShapeDtypeStruct((t, BW), BF16), jax.ShapeDtypeStruct((t, BW), F32),
                             jax.ShapeDtypeStruct((t, BW), BF16), jax.ShapeDtypeStruct((t, BW), BF16),
                             vshape, vshape, vshape],
                  scratch_shapes=[pltpu.VMEM((tm + 8, BW), F32), pltpu.VMEM((tm, BW), F32),
                                  pltpu.VMEM((tm, BW), F32), pltpu.VMEM((tm, BW), F32),
                                  pltpu.VMEM((1, BW), F32)],
                  compiler_params=_cp("arbitrary"))(dya, z, h, h, xc, wa, wx, ba, bx, lam)


def conv_bwd(dxc, z, cw, *, name):
    t = dxc.shape[0]
    tm = min(256, t)
    nb = t // tm
    hb = tm // 8

    def body(d_ref, dnext_ref, ax_ref, prev_ref, cw_ref, dax_ref, dcw_ref, dcb_ref, ds, xs):
        i = pl.program_id(0)

        @pl.when(i == 0)
        def _():
            dcw_ref[...] = jnp.zeros_like(dcw_ref)
            dcb_ref[...] = jnp.zeros_like(dcb_ref)

        d = d_ref[...]
        ds[0:tm, :] = d
        ds[tm:, :] = jnp.where(i == nb - 1, 0.0, dnext_ref[...])
        xs[0:8, :] = jnp.where(i == 0, 0.0, prev_ref[...])
        xs[8:, :] = ax_ref[...]
        dax = cw_ref[3:4, :] * d
        for k in range(3):
            dax = dax + cw_ref[k:k + 1, :] * ds[3 - k:3 - k + tm, :]
        dax_ref[...] = dax.astype(BF16)
        for k in range(4):
            dcw_ref[k:k + 1, :] += jnp.sum(d * xs[5 + k:5 + k + tm, :], axis=0, keepdims=True)
        dcb_ref[...] += jnp.sum(d, axis=0, keepdims=True)

    return _pcall(body, name=name, grid=(nb,),
                  in_specs=[_rows(tm, BW),
                            pl.BlockSpec((8, BW), lambda i: (jnp.minimum((i + 1) * hb, nb * hb - 1), 0)),
                            _rows(tm, BW, AX // BW),
                            pl.BlockSpec((8, BW), lambda i: (jnp.maximum(i * hb - 1, 0), AX // BW)),
                            _fix((4, BW))],
                  out_specs=[_rows(tm, BW), _fix((4, BW)), _fix((1, BW))],
                  out_shape=[jax.ShapeDtypeStruct((t, BW), BF16), jax.ShapeDtypeStruct((4, BW), F32),
                             jax.ShapeDtypeStruct((1, BW), F32)],
                  scratch_shapes=[pltpu.VMEM((tm + 8, BW), F32), pltpu.VMEM((tm + 8, BW), F32)],
                  compiler_params=_cp("arbitrary"))(dxc, dxc, z, z, cw)


GLA_CB = 4


def _gla_consts():
    tri = (jnp.arange(CHUNK)[:, None] >= jnp.arange(CHUNK)[None, :]).astype(F32)
    mask = ((jnp.arange(BW)[:, None] // 128) == (jnp.arange(256)[None, :] // 64)).astype(F32)
    return tri, mask


def gla_fwd(z, zb, wg2p, bg, ng, *, name):
    t = z.shape[0]
    tm = GLA_CB * CHUNK
    nc = t // CHUNK
    tri, mask = _gla_consts()

    def body(q_ref, k_ref, v_ref, misc_ref, br_ref, w_ref, bg_ref, ng_ref, tri_ref, mask_ref,
             yb_ref, st_ref, st):
        @pl.when(pl.program_id(0) == 0)
        def _():
            st[...] = jnp.zeros_like(st)

        for c in range(GLA_CB):
            rows = slice(c * CHUNK, (c + 1) * CHUNK)
            pre = _dot(misc_ref[rows, :], w_ref[...]) + bg_ref[...]
            la = _log_sigmoid(pre) / GLA_TAU
            gc = _dot_hi(tri_ref[...], la)
            gt = gc[CHUNK - 1:CHUNK, :]
            kdec = k_ref[rows, :] * jnp.exp(gt - gc)
            delta = _dot_tn(v_ref[rows, :], kdec.astype(BF16))
            s_new = st[...] * jnp.exp(gt) + delta * mask_ref[...]
            st[...] = s_new
            st_ref[c] = s_new
            o = _dot_nt(q_ref[rows, :], s_new.astype(BF16)) * (64.0 ** -0.5)
            br = br_ref[rows, :]
            for hd in range(4):
                cols = slice(hd * 128, (hd + 1) * 128)
                oh = o[:, cols]
                rs = lax.rsqrt(jnp.mean(oh * oh, axis=-1, keepdims=True) + RMS_EPS)
                brh = br[:, cols]
                yb_ref[rows, cols] = (oh * rs * ng_ref[:, cols] * (brh * _sigmoid(brh))).astype(BF16)

    return _pcall(body, name=name, grid=(t // tm,),
                  in_specs=[_rows(tm, 256, BQ // 256), _rows(tm, 256, BK // 256), _rows(tm, BW, BV // BW),
                            _rows(tm, 128, MISC // 128), _rows(tm, BW, BR // BW), _fix((128, 256)),
                            _fix((1, 256)), _fix((1, BW)), _fix((CHUNK, CHUNK)), _fix((BW, 256))],
                  out_specs=[_rows(tm, BW), pl.BlockSpec((GLA_CB, BW, 256), lambda i: (i, 0, 0))],
                  out_shape=[jax.ShapeDtypeStruct((t, BW), BF16), jax.ShapeDtypeStruct((nc, BW, 256), F32)],
                  scratch_shapes=[pltpu.VMEM((BW, 256), F32)],
                  compiler_params=_cp("arbitrary"))(zb, z, zb, zb, z, wg2p, bg, ng, tri, mask)


def gla_bwd(dyb, z, zb, states, wg2p, bg, ng, *, name):
    t = z.shape[0]
    tm = GLA_CB * CHUNK
    nb = t // tm
    tri, mask = _gla_consts()
    triu = tri.T

    def body(dy_ref, q_ref, k_ref, v_ref, misc_ref, br_ref, st_ref, sp_ref, w_ref, bg_ref, ng_ref,
             tri_ref, triu_ref, mask_ref,
             dq_ref, dk_ref, dv_ref, dbr_ref, dmisc_ref, dpre_ref, dbg_ref, dng_ref, cc):
        i = pl.program_id(0)

        @pl.when(i == 0)
        def _():
            cc[...] = jnp.zeros_like(cc)
            dbg_ref[...] = jnp.zeros_like(dbg_ref)
            dng_ref[...] = jnp.zeros_like(dng_ref)

        last_row = lax.broadcasted_iota(jnp.int32, (CHUNK, 256), 0) == CHUNK - 1
        for c in range(GLA_CB - 1, -1, -1):
            rows = slice(c * CHUNK, (c + 1) * CHUNK)
            pre = _dot(misc_ref[rows, :], w_ref[...]) + bg_ref[...]
            la = _log_sigmoid(pre) / GLA_TAU
            gc = _dot_hi(tri_ref[...], la)
            gt = gc[CHUNK - 1:CHUNK, :]
            eg = jnp.exp(gt - gc)
            kdec = k_ref[rows, :] * eg
            e = jnp.exp(gt)
            s_n = st_ref[c]
            if c > 0:
                s_prev = st_ref[c - 1]
            else:
                s_prev = jnp.where(i == nb - 1, 0.0, sp_ref[0])
            sb = s_n.astype(BF16)
            qb = q_ref[rows, :]
            o = _dot_nt(qb, sb) * (64.0 ** -0.5)
            br = br_ref[rows, :]
            dy = dy_ref[rows, :]
            do_parts = []
            for hd in range(4):
                cols = slice(hd * 128, (hd + 1) * 128)
                oh = o[:, cols]
                rs = lax.rsqrt(jnp.mean(oh * oh, axis=-1, keepdims=True) + RMS_EPS)
                ohat = oh * rs
                brh = br[:, cols]
                sg = _sigmoid(brh)
                dyh = dy[:, cols]
                ngh = ng_ref[:, cols]
                don = dyh * (brh * sg)
                dbr_ref[rows, cols] = (dyh * (ohat * ngh) * sg * (1.0 + brh * (1.0 - sg))).astype(BF16)
                dng_ref[:, cols] += jnp.sum(don * ohat, axis=0, keepdims=True)
                doh = don * ngh
                do_parts.append(rs * (doh - ohat * jnp.mean(doh * ohat, axis=-1, keepdims=True)))
            dob = jnp.concatenate(do_parts, axis=1).astype(BF16)
            dq_ref[rows, :] = (_dot(dob, sb) * (64.0 ** -0.5)).astype(BF16)
            dst = cc[...] + _dot_tn(dob, qb) * (64.0 ** -0.5) * mask_ref[...]
            dsb = dst.astype(BF16)
            dkdec = _dot(v_ref[rows, :], dsb)
            dv_ref[rows, :] = _dot_nt(kdec.astype(BF16), dsb).astype(BF16)
            dgt = jnp.sum(dst * s_prev, axis=0, keepdims=True) * e
            dk_ref[rows, :] = (dkdec * eg).astype(BF16)
            dd = dkdec * kdec
            dgt = dgt + jnp.sum(dd, axis=0, keepdims=True)
            dgc = jnp.where(last_row, dgt - dd, -dd)
            dla = _dot_hi(triu_ref[...], dgc)
            dpre = dla * (1.0 / GLA_TAU) * _sigmoid(-pre)
            dpb = dpre.astype(BF16)
            dpre_ref[rows, :] = dpb
            dmisc_ref[rows, :] = _dot_nt(dpb, w_ref[...])
            dbg_ref[...] += jnp.sum(dpre, axis=0, keepdims=True)
            cc[...] = dst * e

    rev = lambda w, col: pl.BlockSpec((tm, w), lambda i: (nb - 1 - i, col))
    return _pcall(body, name=name, grid=(nb,),
                  in_specs=[rev(BW, 0), rev(256, BQ // 256), rev(256, BK // 256), rev(BW, BV // BW),
                            rev(128, MISC // 128), rev(BW, BR // BW),
                            pl.BlockSpec((GLA_CB, BW, 256), lambda i: (nb - 1 - i, 0, 0)),
                            pl.BlockSpec((1, BW, 256), lambda i: (jnp.maximum((nb - 1 - i) * GLA_CB - 1, 0), 0, 0)),
                            _fix((128, 256)), _fix((1, 256)), _fix((1, BW)),
                            _fix((CHUNK, CHUNK)), _fix((CHUNK, CHUNK)), _fix((BW, 256))],
                  out_specs=[rev(256, 0), rev(256, 0), rev(BW, 0), rev(BW, 0), rev(128, 0), rev(256, 0),
                             _fix((1, 256)), _fix((1, BW))],
                  out_shape=[jax.ShapeDtypeStruct((t, 256), BF16), jax.ShapeDtypeStruct((t, 256), BF16),
                             jax.ShapeDtypeStruct((t, BW), BF16), jax.ShapeDtypeStruct((t, BW), BF16),
                             jax.ShapeDtypeStruct((t, 128), F32), jax.ShapeDtypeStruct((t, 256), BF16),
                             jax.ShapeDtypeStruct((1, 256), F32), jax.ShapeDtypeStruct((1, BW), F32)],
                  scratch_shapes=[pltpu.VMEM((BW, 256), F32)],
                  compiler_params=_cp("arbitrary"))(dyb, zb, z, zb, zb, z, states, states, wg2p, bg, ng,
                                                    tri, triu, mask)


FOX_SCALE = 64.0 ** -0.5
NEG = -1e30


def fox_fcum(z, bfp, *, name):
    t = z.shape[0]
    tm = min(256, t)
    tri = (jnp.arange(tm)[:, None] >= jnp.arange(tm)[None, :]).astype(F32)

    def body(m_ref, b_ref, tri_ref, o_ref, cc):
        @pl.when(pl.program_id(0) == 0)
        def _():
            cc[...] = jnp.zeros_like(cc)

        lf = _log_sigmoid(m_ref[...] + b_ref[...])
        cs = _dot_hi(tri_ref[...], lf) + cc[...]
        o_ref[...] = cs
        cc[...] = cs[tm - 1:tm, :]

    return _pcall(body, name=name, grid=(t // tm,),
                  in_specs=[_rows(tm, 128, MISC // 128), _fix((1, 128)), _fix((tm, tm))],
                  out_specs=_rows(tm, 128), out_shape=jax.ShapeDtypeStruct((t, 128), F32),
                  scratch_shapes=[pltpu.VMEM((1, 128), F32)],
                  compiler_params=_cp("arbitrary"))(z, bfp, tri)


def fox_dcf(dfc, z, bfp, dmisc_g, *, name):
    t = z.shape[0]
    tm = min(256, t)
    nb = t // tm
    triu = (jnp.arange(tm)[:, None] <= jnp.arange(tm)[None, :]).astype(F32)

    def body(d_ref, m_ref, b_ref, g_ref, tri_ref, o_ref, dbf_ref, cc):
        @pl.when(pl.program_id(0) == 0)
        def _():
            cc[...] = jnp.zeros_like(cc)
            dbf_ref[...] = jnp.zeros_like(dbf_ref)

        rc = _dot_hi(tri_ref[...], d_ref[...]) + cc[...]
        cc[...] = rc[0:1, :]
        dcf = rc * _sigmoid(-(m_ref[...] + b_ref[...]))
        o_ref[...] = (dcf + g_ref[...]).astype(BF16)
        dbf_ref[...] += jnp.sum(dcf, axis=0, keepdims=True)

    rev = lambda col: pl.BlockSpec((tm, 128), lambda i: (nb - 1 - i, col))
    return _pcall(body, name=name, grid=(nb,),
                  in_specs=[rev(0), rev(MISC // 128), _fix((1, 128)), rev(0), _fix((tm, tm))],
                  out_specs=[rev(0), _fix((1, 128))],
                  out_shape=[jax.ShapeDtypeStruct((t, 128), BF16), jax.ShapeDtypeStruct((1, 128), F32)],
                  scratch_shapes=[pltpu.VMEM((1, 128), F32)],
                  compiler_params=_cp("arbitrary"))(dfc, z, bfp, dmisc_g, triu)


def fox_delta(dyc, ycf, *, name):
    t = dyc.shape[0]
    tm = min(256, t)
    seg = ((jnp.arange(BW)[:, None] // 64) == jnp.arange(128)[None, :]).astype(F32)

    def body(d_ref, o_ref, s_ref, out_ref):
        out_ref[...] = _dot_hi(d_ref[...] * o_ref[...], s_ref[...])

    return _pcall(body, name=name, grid=(t // tm,),
                  in_specs=[_rows(tm, BW), _rows(tm, BW), _fix((BW, 128))],
                  out_specs=_rows(tm, 128), out_shape=jax.ShapeDtypeStruct((t, 128), F32),
                  compiler_params=_cp("parallel"))(dyc, ycf, seg)


def fox_fwd_t(zb, frow, fkb, *, name):
    t = zb.shape[0]
    tq = min(512, t)
    nq = t // tq
    rep = tq // 128

    pairs = [(i, j) for i in range(nq) for j in range(i + 1)]
    qi_tab = jnp.asarray([p[0] for p in pairs], jnp.int32)
    kj_tab = jnp.asarray([p[1] for p in pairs], jnp.int32)

    def body(qi_ref, kj_ref, q_ref, k_ref, v_ref, fq_ref, fk_ref, y_ref, yf_ref, lse_ref, m_s, l_s, acc):
        step = pl.program_id(1)
        i, j = qi_ref[step], kj_ref[step]

        @pl.when(j == 0)
        def _():
            m_s[...] = jnp.full_like(m_s, NEG)
            l_s[...] = jnp.zeros_like(l_s)
            acc[...] = jnp.zeros_like(acc)

        lo = lax.broadcasted_iota(jnp.int32, (tq, 128), 1) < 64

        def work(diagonal):
            q = q_ref[...]
            k = k_ref[...]
            v = v_ref[...]
            if diagonal:
                key = lax.broadcasted_iota(jnp.int32, (tq, tq), 0)
                qry = lax.broadcasted_iota(jnp.int32, (tq, tq), 1)
                keep = key <= qry
            for hh in range(2):
                sel = lo if hh == 0 else jnp.logical_not(lo)
                qh = jnp.where(sel, q, jnp.zeros_like(q))
                s = _dot_nt(k, qh) + fq_ref[hh] - jnp.tile(fk_ref[hh], (1, rep))
                if diagonal:
                    s = jnp.where(keep, s, NEG)
                m_old = m_s[hh]
                m_new = jnp.maximum(m_old, jnp.max(s, axis=0, keepdims=True))
                p = jnp.exp(s - m_new)
                corr = jnp.exp(m_old - m_new)
                l_s[hh] = l_s[hh] * corr + jnp.sum(p, axis=0, keepdims=True)
                m_s[hh] = m_new
                pv = _dot_tn(v, p.astype(BF16))
                rows = slice(64 * hh, 64 * hh + 64)
                acc[rows, :] = acc[rows, :] * corr + pv[rows, :]

        @pl.when(j < i)
        def _():
            work(False)

        @pl.when(j == i)
        def _():
            work(True)
            first = lax.broadcasted_iota(jnp.int32, (128, tq), 0) < 64
            out = (acc[...] * jnp.where(first, 1.0 / l_s[0], 1.0 / l_s[1])).T
            y_ref[...] = out.astype(BF16)
            yf_ref[...] = out
            lse_ref[...] = m_s[...] + jnp.log(l_s[...])

    kv = lambda off: pl.BlockSpec((tq, 128), lambda h, s, qi, kj: (kj[s], off // 128 + h))
    gs = pltpu.PrefetchScalarGridSpec(
        num_scalar_prefetch=2, grid=(4, len(pairs)),
        in_specs=[pl.BlockSpec((tq, 128), lambda h, s, qi, kj: (qi[s], CQ // 128 + h)), kv(CK), kv(CV),
                  pl.BlockSpec((2, 1, tq), lambda h, s, qi, kj: (h, 0, qi[s])),
                  pl.BlockSpec((2, tq, 128), lambda h, s, qi, kj: (h, kj[s], 0))],
        out_specs=[pl.BlockSpec((tq, 128), lambda h, s, qi, kj: (qi[s], h)),
                   pl.BlockSpec((tq, 128), lambda h, s, qi, kj: (qi[s], h)),
                   pl.BlockSpec((2, 1, tq), lambda h, s, qi, kj: (h, 0, qi[s]))],
        scratch_shapes=[pltpu.VMEM((2, 1, tq), F32), pltpu.VMEM((2, 1, tq), F32), pltpu.VMEM((128, tq), F32)])
    return _pcall(body, name=name, grid_spec=gs,
                  out_shape=[jax.ShapeDtypeStruct((t, BW), BF16), jax.ShapeDtypeStruct((t, BW), F32),
                             jax.ShapeDtypeStruct((FOX_H, 1, t), F32)],
                  compiler_params=_cp("parallel", "arbitrary"))(qi_tab, kj_tab, zb, zb, zb, frow, fkb)


def fox_bwd_t(zb, dyc, frow, fkb, lse, dl, *, name):
    t = zb.shape[0]
    tq = min(512, t)
    nq = t // tq
    rep = tq // 128

    pairs = [(j, i) for j in range(nq) for i in range(j, nq)]
    kj_tab = jnp.asarray([p[0] for p in pairs], jnp.int32)
    qi_tab = jnp.asarray([p[1] for p in pairs], jnp.int32)

    def body(kj_ref, qi_ref, q_ref, k_ref, v_ref, do_ref, fq_ref, fk_ref, lse_ref, dl_ref,
             dq_ref, dk_ref, dv_ref, dfk_ref, dfq_ref, dk_s, dv_s, df_s, dq_s):
        step = pl.program_id(1)
        j, i = kj_ref[step], qi_ref[step]

        @pl.when(step == 0)
        def _():
            dq_s[...] = jnp.zeros_like(dq_s)
            dfq_ref[...] = jnp.zeros_like(dfq_ref)

        @pl.when(i == j)
        def _():
            dk_s[...] = jnp.zeros_like(dk_s)
            dv_s[...] = jnp.zeros_like(dv_s)
            df_s[...] = jnp.zeros_like(df_s)

        lo = lax.broadcasted_iota(jnp.int32, (tq, 128), 1) < 64

        def work(diagonal):
            q = q_ref[...]
            k = k_ref[...]
            v = v_ref[...]
            dob = do_ref[...].astype(BF16)
            if diagonal:
                key = lax.broadcasted_iota(jnp.int32, (tq, tq), 0)
                qry = lax.broadcasted_iota(jnp.int32, (tq, tq), 1)
                keep = key <= qry
            dvs, dks = [], []
            for hh in range(2):
                sel = lo if hh == 0 else jnp.logical_not(lo)
                qh = jnp.where(sel, q, jnp.zeros_like(q))
                doh = jnp.where(sel, dob, jnp.zeros_like(dob))
                p = jnp.exp(_dot_nt(k, qh) + (fq_ref[hh] - lse_ref[hh]) - jnp.tile(fk_ref[hh], (1, rep)))
                if diagonal:
                    p = jnp.where(keep, p, 0.0)
                ds = p * (_dot_nt(v, doh) - dl_ref[hh])
                dsb = ds.astype(BF16)
                dvs.append(_dot(p.astype(BF16), dob))
                dks.append(_dot(dsb, q))
                rows = slice(64 * hh, 64 * hh + 64)
                dq_s[i, rows, :] += _dot_tn(k, dsb)[rows, :]
                part = ds[:, 0:128]
                for r in range(1, rep):
                    part = part + ds[:, 128 * r:128 * (r + 1)]
                df_s[hh] += part
                dfq_ref[hh, i] += jnp.sum(ds, axis=0, keepdims=True)
            dv_s[...] += jnp.where(lo, dvs[0], dvs[1])
            dk_s[...] += jnp.where(lo, dks[0], dks[1])

        @pl.when(i > j)
        def _():
            work(False)

        @pl.when(i == j)
        def _():
            work(True)
            dq_ref[...] = dq_s[i].T.astype(BF16)

        @pl.when(i == nq - 1)
        def _():
            dk_ref[...] = dk_s[...].astype(BF16)
            dv_ref[...] = dv_s[...].astype(BF16)
            for hh in range(2):
                dfk_ref[hh] = -jnp.sum(df_s[hh].T, axis=0, keepdims=True)

    row = lambda: pl.BlockSpec((2, 1, tq), lambda h, s, kj, qi: (h, 0, qi[s]))
    gs = pltpu.PrefetchScalarGridSpec(
        num_scalar_prefetch=2, grid=(4, len(pairs)),
        in_specs=[pl.BlockSpec((tq, 128), lambda h, s, kj, qi: (qi[s], CQ // 128 + h)),
                  pl.BlockSpec((tq, 128), lambda h, s, kj, qi: (kj[s], CK // 128 + h)),
                  pl.BlockSpec((tq, 128), lambda h, s, kj, qi: (kj[s], CV // 128 + h)),
                  pl.BlockSpec((tq, 128), lambda h, s, kj, qi: (qi[s], h)),
                  row(), pl.BlockSpec((2, tq, 128), lambda h, s, kj, qi: (h, kj[s], 0)), row(), row()],
        out_specs=[pl.BlockSpec((tq, 128), lambda h, s, kj, qi: (kj[s], h)),
                   pl.BlockSpec((tq, 128), lambda h, s, kj, qi: (kj[s], h)),
                   pl.BlockSpec((tq, 128), lambda h, s, kj, qi: (kj[s], h)),
                   pl.BlockSpec((2, 1, tq), lambda h, s, kj, qi: (h, 0, kj[s])),
                   pl.BlockSpec((2, nq, 1, tq), lambda h, s, kj, qi: (h, 0, 0, 0))],
        scratch_shapes=[pltpu.VMEM((tq, 128), F32), pltpu.VMEM((tq, 128), F32), pltpu.VMEM((2, tq, 128), F32),
                        pltpu.VMEM((nq, 128, tq), F32)])
    return _pcall(body, name=name, grid_spec=gs,
                  out_shape=[jax.ShapeDtypeStruct((t, BW), BF16), jax.ShapeDtypeStruct((t, BW), BF16),
                             jax.ShapeDtypeStruct((t, BW), BF16), jax.ShapeDtypeStruct((FOX_H, 1, t), F32),
                             jax.ShapeDtypeStruct((FOX_H, nq, 1, tq), F32)],
                  compiler_params=_cp("parallel", "arbitrary"))(kj_tab, qi_tab, zb, zb, zb, dyc, frow, fkb, lse, dl)


def merge_fwd(ya, yb, yc, wbr, z, *, name):
    t = ya.shape[0]
    tm = min(256, t)

    def body(ya_ref, yb_ref, yc_ref, w_ref, g0_ref, g1_ref, g2_ref, o_ref):
        m = _sigmoid(g0_ref[...]) * _dot(ya_ref[...], w_ref[0])
        m = m + _sigmoid(g1_ref[...]) * _dot(yb_ref[...], w_ref[1])
        m = m + _sigmoid(g2_ref[...]) * _dot(yc_ref[...], w_ref[2])
        o_ref[...] = m.astype(BF16)

    return _pcall(body, name=name, grid=(t // tm,),
                  in_specs=[_rows(tm, BW)] * 3 + [_fix((3, BW, D))]
                  + [_rows(tm, D, G0 // D + j) for j in range(3)],
                  out_specs=_rows(tm, D), out_shape=jax.ShapeDtypeStruct((t, D), BF16),
                  compiler_params=_cp("parallel"))(ya, yb, yc, wbr, z, z, z)


def merge_bwd(doutb, wo, l, ya, yb, yc, wbr, z, *, name):
    t = ya.shape[0]
    tm = min(256, t)

    def body(do_ref, wo_ref, ya_ref, yb_ref, yc_ref, w_ref, g0_ref, g1_ref, g2_ref,
             dya_ref, dyb_ref, dyc_ref, dp0_ref, dp1_ref, dp2_ref, dg0_ref, dg1_ref, dg2_ref):
        dm = _dot_nt(do_ref[...], wo_ref[...])
        ys = (ya_ref, yb_ref, yc_ref)
        gs = (g0_ref, g1_ref, g2_ref)
        dys = (dya_ref, dyb_ref, dyc_ref)
        dps = (dp0_ref, dp1_ref, dp2_ref)
        dgs = (dg0_ref, dg1_ref, dg2_ref)
        for j in range(3):
            s = _sigmoid(gs[j][...])
            pj = _dot(ys[j][...], w_ref[j])
            dpb = (dm * s).astype(BF16)
            dps[j][...] = dpb
            dgs[j][...] = (dm * pj * s * (1.0 - s)).astype(BF16)
            dys[j][...] = _dot_nt(dpb, w_ref[j])

    yshape = jax.ShapeDtypeStruct((t, BW), F32)
    dshape = jax.ShapeDtypeStruct((t, D), BF16)
    return _pcall(body, name=name, grid=(t // tm,),
                  in_specs=[_rows(tm, D), _layer(l, (D, D))] + [_rows(tm, BW)] * 3
                  + [_fix((3, BW, D))] + [_rows(tm, D, G0 // D + j) for j in range(3)],
                  out_specs=[_rows(tm, BW)] * 3 + [_rows(tm, D)] * 6,
                  out_shape=[yshape] * 3 + [dshape] * 6,
                  compiler_params=_cp("parallel"))(doutb, wo, ya, yb, yc, wbr, z, z, z)


def adamw(w, g, m, v, *, name):
    nl, r, c = w.shape
    tm = _row_tile(r)

    def body(w_ref, g_ref, m_ref, v_ref, d_ref, mo_ref, vo_ref):
        gg = g_ref[...]
        mn = ADAM_B1 * m_ref[...] + (1.0 - ADAM_B1) * gg
        vn = ADAM_B2 * v_ref[...] + (1.0 - ADAM_B2) * (gg * gg)
        m_hat = mn / (1.0 - ADAM_B1 ** ADAM_STEP)
        v_hat = vn / (1.0 - ADAM_B2 ** ADAM_STEP)
        d_ref[...] = -ADAM_LR * (m_hat / (jnp.sqrt(v_hat) + ADAM_EPS) + ADAM_WD * w_ref[...])
        mo_ref[...] = mn
        vo_ref[...] = vn

    shp = jax.ShapeDtypeStruct((nl, r, c), F32)
    blk = pl.BlockSpec((None, tm, c), lambda l, i: (l, i, 0))
    return _pcall(body, name=name, grid=(nl, r // tm), in_specs=[blk] * 4, out_specs=[blk] * 3,
                  out_shape=[shp] * 3, compiler_params=_cp("parallel", "parallel"))(w, g, m, v)


def _place():
    return lax.axis_index("x"), lax.axis_index("y"), lax.axis_index("c")


def _remote(src, dst, send_sems, recv_sems, k, to):
    return pltpu.make_async_remote_copy(src_ref=src, dst_ref=dst, send_sem=send_sems.at[k],
                                        recv_sem=recv_sems.at[k], device_id=to, device_id_type=MESH)


def gather_weights(shards):
    n = len(shards)

    def body(*refs):
        ins, outs = refs[:n], refs[n:2 * n]
        send_sems, recv_sems, own_send, own_recv = refs[2 * n:]
        x, y, c = _place()
        sib = (x, y, 1 - c)
        chips = [(1 - x, y), (x, 1 - y), (1 - x, 1 - y)]
        k_me = 2 * x + y
        mine, first, passed = [], [], []
        for t in range(n):
            for l in range(DEPTH):
                mine.append(_remote(ins[t].at[l], outs[t].at[l, k_me], own_send, own_recv, 2 * t + l, sib))
            for j, chip in enumerate(chips):
                first.append(_remote(ins[t].at[c], outs[t].at[c, k_me], send_sems, recv_sems, 6 * t + j, (*chip, c)))
        for cp in mine + first:
            cp.start()
        for t in range(n):
            for j, chip in enumerate(chips):
                blk = outs[t].at[c, 2 * chip[0] + chip[1]]
                _remote(blk, blk, send_sems, recv_sems, 6 * t + j, (*chip, c)).wait_recv()
                cp = _remote(blk, blk, send_sems, recv_sems, 6 * t + 3 + j, sib)
                cp.start()
                passed.append(cp)
        for t in range(n):
            for j, chip in enumerate(chips):
                blk = outs[t].at[1 - c, 2 * chip[0] + chip[1]]
                _remote(blk, blk, send_sems, recv_sems, 6 * t + 3 + j, sib).wait_recv()
        for cp in first + passed:
            cp.wait_send()
        for cp in mine:
            cp.wait()

    return _pcall(body, name="gather_weights", in_specs=[ANY] * n, out_specs=[ANY] * n,
                  out_shape=[jax.ShapeDtypeStruct((DEPTH, 4) + s.shape[1:], s.dtype) for s in shards],
                  scratch_shapes=[pltpu.SemaphoreType.DMA((6 * n,)), pltpu.SemaphoreType.DMA((6 * n,)),
                                  pltpu.SemaphoreType.DMA((2 * n,)), pltpu.SemaphoreType.DMA((2 * n,))])(*shards)


def pair_exchange(g0, g1):
    n = len(g0)

    def body(*refs):
        a0, a1, outs = refs[:n], refs[n:2 * n], refs[2 * n:3 * n]
        send_sems, recv_sems = refs[3 * n:]
        x, y, c = _place()
        sib = (x, y, 1 - c)

        @pl.when(c == 0)
        def _():
            for t in range(n):
                _remote(a1[t], outs[t], send_sems, recv_sems, t, sib).start()

        @pl.when(c == 1)
        def _():
            for t in range(n):
                _remote(a0[t], outs[t], send_sems, recv_sems, t, sib).start()

        for t in range(n):
            _remote(a0[t], outs[t], send_sems, recv_sems, t, sib).wait()

    return _pcall(body, name="pair_exchange", in_specs=[ANY] * (2 * n), out_specs=[ANY] * n,
                  out_shape=[jax.ShapeDtypeStruct(a.shape, a.dtype) for a in g0],
                  scratch_shapes=[pltpu.SemaphoreType.DMA((n,)), pltpu.SemaphoreType.DMA((n,))])(*g0, *g1)


def chip_exchange(s1):
    n = len(s1)

    def body(*refs):
        ins, outs = refs[:n], refs[n:2 * n]
        send_sems, recv_sems = refs[2 * n:]
        x, y, c = _place()
        chips = [(1 - x, y), (x, 1 - y), (1 - x, 1 - y)]
        cps = [_remote(ins[t].at[2 * chip[0] + chip[1]], outs[t].at[j], send_sems, recv_sems, 3 * t + j, (*chip, c))
               for t in range(n) for j, chip in enumerate(chips)]
        for cp in cps:
            cp.start()
        for cp in cps:
            cp.wait()

    return _pcall(body, name="chip_exchange", in_specs=[ANY] * n, out_specs=[ANY] * n,
                  out_shape=[jax.ShapeDtypeStruct((3,) + a.shape[1:], a.dtype) for a in s1],
                  scratch_shapes=[pltpu.SemaphoreType.DMA((3 * n,)), pltpu.SemaphoreType.DMA((3 * n,))])(*s1)


def pair_share(s2):
    n = len(s2)

    def body(*refs):
        ins, outs = refs[:n], refs[n:2 * n]
        send_sems, recv_sems = refs[2 * n:]
        x, y, c = _place()
        sib = (x, y, 1 - c)
        cps = [_remote(ins[t].at[c], outs[t].at[c], send_sems, recv_sems, t, sib) for t in range(n)]
        for cp in cps:
            cp.start()
        for t in range(n):
            cps[t].wait_send()
            _remote(ins[t].at[c], outs[t].at[1 - c], send_sems, recv_sems, t, sib).wait_recv()

    return _pcall(body, name="pair_share", in_specs=[ANY] * n, out_specs=[ANY] * n,
                  out_shape=[jax.ShapeDtypeStruct(a.shape, a.dtype) for a in s2],
                  input_output_aliases={t: t for t in range(n)},
                  scratch_shapes=[pltpu.SemaphoreType.DMA((n,)), pltpu.SemaphoreType.DMA((n,))])(*s2)


def small_exchange(gs):
    rows, width = gs.shape

    def body(g_ref, o_ref, send_sems, recv_sems):
        x, y, c = _place()
        cps = []
        for r in range(1, 8):
            dx, dy, dc = (r >> 2) & 1, (r >> 1) & 1, r & 1
            to = (x if dx == 0 else 1 - x, y if dy == 0 else 1 - y, c if dc == 0 else 1 - c)
            cps.append(_remote(g_ref, o_ref.at[r - 1], send_sems, recv_sems, r - 1, to))
        for cp in cps:
            cp.start()
        for cp in cps:
            cp.wait()

    return _pcall(body, name="small_exchange", in_specs=[ANY], out_specs=ANY,
                  out_shape=jax.ShapeDtypeStruct((7, rows, width), gs.dtype),
                  scratch_shapes=[pltpu.SemaphoreType.DMA((7,)), pltpu.SemaphoreType.DMA((7,))])(gs)


def _row_tile(rows):
    return _pick(rows, (256, 352, 128, 64, 32, 16))


def pair_add(g0, g1, rb, core, *, name):
    _, rows, width = g0.shape
    tr = _row_tile(rows)

    def body(c_ref, g0_ref, g1_ref, r_ref, o_ref, ob_ref):
        s = jnp.where(c_ref[0] == 0, g0_ref[...], g1_ref[...]) + r_ref[...]
        o_ref[...] = s
        ob_ref[...] = s.astype(BF16)

    blk = pl.BlockSpec((None, tr, width), lambda k, i, c_ref: (k, i, 0))
    gs = pltpu.PrefetchScalarGridSpec(
        num_scalar_prefetch=1, grid=(4, rows // tr),
        in_specs=[pl.BlockSpec((None, tr, width), lambda k, i, c_ref: (k * (1 - c_ref[0]), i * (1 - c_ref[0]), 0)),
                  pl.BlockSpec((None, tr, width), lambda k, i, c_ref: (k * c_ref[0], i * c_ref[0], 0)), blk],
        out_specs=[blk, blk])
    return _pcall(body, name=name, grid_spec=gs,
                  out_shape=[jax.ShapeDtypeStruct(g0.shape, F32), jax.ShapeDtypeStruct(g0.shape, BF16)],
                  compiler_params=_cp("parallel", "parallel"))(core, g0, g1, rb)


def chip_add(s1, rb2, chip, core, *, name):
    _, rows, width = s1.shape
    tr = _row_tile(rows)

    def body(k_ref, c_ref, s_ref, r_ref, o_ref):
        o_ref[...] = ((s_ref[...] + r_ref[0].astype(F32)) + r_ref[1].astype(F32)) + r_ref[2].astype(F32)

    gs = pltpu.PrefetchScalarGridSpec(
        num_scalar_prefetch=2, grid=(rows // tr,),
        in_specs=[pl.BlockSpec((None, tr, width), lambda i, k_ref, c_ref: (k_ref[0], i, 0)),
                  pl.BlockSpec((3, tr, width), lambda i, k_ref, c_ref: (0, i, 0))],
        out_specs=pl.BlockSpec((None, tr, width), lambda i, k_ref, c_ref: (c_ref[0], i, 0)))
    return _pcall(body, name=name, grid_spec=gs, out_shape=jax.ShapeDtypeStruct((DEPTH, rows, width), F32),
                  compiler_params=_cp("parallel"))(chip, core, s1, rb2)


def small_add(gs_own, slots, me):
    rows, width = gs_own.shape
    tr = _pick(rows, (64, 32, 16, 8))

    def body(me_ref, g_ref, s_ref, o_ref):
        me_v = me_ref[0]
        total = None
        for d in range(8):
            rel = jnp.bitwise_xor(me_v, d)
            val = jnp.where(rel == 0, g_ref[...], s_ref[jnp.maximum(rel - 1, 0)])
            total = val if total is None else total + val
        o_ref[...] = total

    gs = pltpu.PrefetchScalarGridSpec(
        num_scalar_prefetch=1, grid=(rows // tr,),
        in_specs=[pl.BlockSpec((tr, width), lambda i, m_ref: (i, 0)),
                  pl.BlockSpec((7, tr, width), lambda i, m_ref: (0, i, 0))],
        out_specs=pl.BlockSpec((tr, width), lambda i, m_ref: (i, 0)))
    return _pcall(body, name="small_add", grid_spec=gs, out_shape=jax.ShapeDtypeStruct((rows, width), F32),
                  compiler_params=_cp("parallel"))(me, gs_own, slots)


SHARDED = (("ffn1_w_up", (D, UPW)), ("ffn1_w_down", (DFF // 4, D)), ("w_in", (D, D_IN // 4)),
           ("conv_w", (4, BW // 4)), ("gla_w_g2", (LOW_W, 64)), ("w_branch", (3 * BW, D // 4)),
           ("w_out", (D // 4, D)), ("ffn2_w_up", (D, UPW)), ("ffn2_w_down", (DFF // 4, D)),
           ("ple_w_proj", (PLE, D // 4)), ("ple_w_gate", (D // 4, D)))
SMALL = ("ln1_g", "ln1_b", "conv_b", "lru_wa", "lru_ba", "lru_wx", "lru_bx", "lru_lambda", "gla_b_g",
         "gla_norm_g", "fox_b_f", "ln2_g", "ln2_b", "ln3_g", "ln3_b", "ple_b_gate", "ln4_g", "ln4_b")
WEIGHTS = ('ffn1_w_up', 'ffn1_w_down', 'ln1_g', 'ln1_b', 'w_in', 'conv_w', 'conv_b', 'lru_wa', 'lru_ba',
           'lru_wx', 'lru_bx', 'lru_lambda', 'gla_w_g2', 'gla_b_g', 'gla_norm_g', 'fox_b_f', 'w_branch',
           'w_out', 'ln2_g', 'ln2_b', 'ffn2_w_up', 'ffn2_w_down', 'ln3_g', 'ln3_b', 'ple_w_proj',
           'ple_w_gate', 'ple_b_gate', 'ln4_g', 'ln4_b')


def _cols_join(parts):
    return jnp.concatenate([parts[k] for k in range(4)], axis=-1)


def _cols_split(full):
    r, c4 = full.shape
    return full.reshape(r, 4, c4 // 4).transpose(1, 0, 2)


def _regroup_in(w):
    pad = jnp.zeros(w.shape[:-1] + (ZW - D_IN,), w.dtype)
    fox_q = (w[..., 2576:3088] * FOX_SCALE).astype(w.dtype)
    return jnp.concatenate([w[..., 0:2048], w[..., 2064:2576], fox_q, w[..., 3088:4112], w[..., 4120:7192],
                            w[..., 2048:2064], w[..., 4112:4120], pad], axis=-1)


_IN_RUNS = ((0, 2048, 0, 1.0), (2048, 2064, 7168, 1.0), (2064, 2576, 2048, 1.0), (2576, 3088, CQ, FOX_SCALE),
            (3088, 4112, CK, 1.0), (4112, 4120, 7184, 1.0), (4120, D_IN, 4096, 1.0))


def _regroup_out_shards(g):
    w = D_IN // 4
    shards = []
    for k in range(4):
        pieces = []
        for a, b, new, f in _IN_RUNS:
            lo, hi = max(a, k * w), min(b, (k + 1) * w)
            if lo < hi:
                piece = g[:, new + lo - a:new + hi - a]
                pieces.append(piece if f == 1.0 else piece * f)
        shards.append(jnp.concatenate(pieces, axis=1))
    return jnp.stack(shards)


def _block_diag(w):
    eye = jnp.eye(8, dtype=w.dtype)
    return (eye[:, None, :, None] * w[:, :, None, :]).reshape(BW, BW)


def _diag_blocks(dense):
    return jnp.stack([dense[64 * n:64 * (n + 1), 64 * n:64 * (n + 1)] for n in range(8)])


def _layer_weights(gw, small, l):
    w = {"up1": gw["ffn1_w_up"], "up2": gw["ffn2_w_up"],
         "dn1": gw["ffn1_w_down"].reshape(DEPTH, DFF, D), "dn2": gw["ffn2_w_down"].reshape(DEPTH, DFF, D),
         "wo": gw["w_out"].reshape(DEPTH, D, D), "wgt": gw["ple_w_gate"].reshape(DEPTH, D, D)}
    w["win"] = _regroup_in(_cols_join(gw["w_in"][l]))
    w["cw"] = _cols_join(gw["conv_w"][l])
    w["wa"] = _block_diag(small["lru_wa"][l]).astype(BF16)
    w["wx"] = _block_diag(small["lru_wx"][l]).astype(BF16)
    w["wg2p"] = jnp.pad(_cols_join(gw["gla_w_g2"][l]), ((0, 128 - LOW_W), (0, 0)))
    w["wbr"] = _cols_join(gw["w_branch"][l].reshape(4, 3, BW, D // 4))
    w["wp"] = _cols_join(gw["ple_w_proj"][l])
    for n in ("ln1_g", "ln1_b", "ln2_g", "ln2_b", "ln3_g", "ln3_b", "ln4_g", "ln4_b", "conv_b", "lru_ba",
              "lru_bx", "lru_lambda", "gla_b_g", "gla_norm_g", "ple_b_gate"):
        w[n] = small[n][l][None, :]
    w["bfp"] = jnp.pad(small["fox_b_f"][l], (LOW_W, 128 - LOW_W - FOX_H))[None, :]
    return w


def _heads_t(a):
    ht = a[:, LOW_W:LOW_W + FOX_H].T
    return ht[:, None, :], jnp.broadcast_to(ht[:, :, None], ht.shape + (128,))


def _layer_fwd(x, xb, pb, w, l):
    s = {"x0": x, "x0b": xb}
    tag = "l%d_" % l
    gate, up, act = ffn_up(xb, w["up1"], l, name=tag + "ffn1_up")
    r1, x1, x1b = matmul_res_ln(act, w["dn1"], l, x, w["ln1_g"], w["ln1_b"], mm_scale=0.5, name=tag + "ffn1_down")
    s.update(gate1=gate, up1=up, act1=act, r1=r1, x1=x1, x1b=x1b)
    z, zb = matmul(x1b, w["win"], also_bf16=True, tn=_pick(ZW, (2432,)), name=tag + "mix_in")
    xc, xcb, h, ya = lru_fwd(z, w["cw"], w["conv_b"], w["wa"], w["wx"], w["lru_ba"], w["lru_bx"],
                             w["lru_lambda"], name=tag + "lru_fwd")
    yb, states = gla_fwd(z, zb, w["wg2p"], w["gla_b_g"], w["gla_norm_g"], name=tag + "gla_fwd")
    fcum = fox_fcum(z, w["bfp"], name=tag + "fox_fcum")
    fq, fk = _heads_t(fcum)
    yc, ycf, lse = fox_fwd_t(zb, fq, fk, name=tag + "fox_fwd")
    merged = merge_fwd(ya, yb, yc, w["wbr"], z, name=tag + "merge_fwd")
    r2, x2, x2b = matmul_res_ln(merged, w["wo"], l, x1, w["ln2_g"], w["ln2_b"], mm_scale=1.0, name=tag + "mix_out")
    s.update(z=z, zb=zb, xc=xc, xcb=xcb, h=h, ya=ya, yb=yb, states=states, fq=fq, fk=fk, yc=yc, ycf=ycf,
             lse=lse, merged=merged, r2=r2, x2=x2, x2b=x2b)
    gate, up, act = ffn_up(x2b, w["up2"], l, name=tag + "ffn2_up")
    r3, x3, x3b = matmul_res_ln(act, w["dn2"], l, x2, w["ln3_g"], w["ln3_b"], mm_scale=0.5, name=tag + "ffn2_down")
    s.update(gate2=gate, up2=up, act2=act, r3=r3, x3=x3, x3b=x3b)
    r4, x4, x4b = ple_fwd(x3b, x3, pb, w["wgt"], l, w["wp"], w["ple_b_gate"], w["ln4_g"], w["ln4_b"],
                          name=tag + "ple_fwd")
    s.update(r4=r4, pb=pb)
    return x4, x4b, s


def _ffn_bwd(dy, s, w, n, xin_b, l, tag):
    k = {"1": ("r1", "ln1_g", "gate1", "up1", "act1"), "2": ("r3", "ln3_g", "gate2", "up2", "act2")}[n]
    dr, dfb, dg, db = ln_bwd(dy, s[k[0]], w[k[1]], out_scale=0.5, name=tag + "ln_bwd")
    dgate, dup = ffn_down_bwd(dfb, w["dn" + n], l, s[k[2]], s[k[3]], name=tag + "down_bwd")
    dx = ffn_dx(dgate, dup, w["up" + n], l, dr, name=tag + "dx")
    dwup = matmul_tn_up(xin_b, dgate, dup, name=tag + "dw_up")
    dwdn = matmul_tn(s[k[4]], dfb, name=tag + "dw_down").reshape(4, DFF // 4, D)
    return dx, dwup, dwdn, dg[0], db[0]


def _layer_bwd(dy, s, w, l):
    g = {}
    tag = "l%d_" % l
    dr4, dglb, dpeb, dg4, db4, dbg = ple_bwd(dy, s["r4"], s["x3b"], s["pb"], w["wgt"], l, w["wp"], w["ple_b_gate"],
                                             w["ln4_g"], name=tag + "ple_bwd")
    dx3 = matmul(dglb, w["wgt"], nt=True, b_lead=(l,), res=dr4, res_scale=ALPHA, tm=1024, tn=1024,
                 name=tag + "ple_dx")
    g["ple_w_gate"] = matmul_tn(s["x3b"], dglb, name=tag + "ple_dw_gate").reshape(4, D // 4, D)
    g["ple_w_proj"] = _cols_split(matmul_tn(s["pb"], dpeb, name=tag + "ple_dw_proj"))
    g["ln4_g"], g["ln4_b"], g["ple_b_gate"] = dg4[0], db4[0], dbg[0]
    dx2, g["ffn2_w_up"], g["ffn2_w_down"], g["ln3_g"], g["ln3_b"] = _ffn_bwd(dx3, s, w, "2", s["x2b"], l,
                                                                             tag + "ffn2_")
    dr2, doutb, dg2, db2 = ln_bwd(dx2, s["r2"], w["ln2_g"], out_scale=1.0, name=tag + "mix_ln_bwd")
    g["ln2_g"], g["ln2_b"] = dg2[0], db2[0]
    g["w_out"] = matmul_tn(s["merged"], doutb, name=tag + "dw_out").reshape(4, D // 4, D)
    z, zb = s["z"], s["zb"]
    (dya, dyb, dyc, dp0, dp1, dp2, dgl0, dgl1, dgl2) = merge_bwd(
        doutb, w["wo"], l, s["ya"], s["yb"], s["yc"], w["wbr"], z, name=tag + "merge_bwd")
    dwbr = jnp.stack([matmul_tn(s["ya"], dp0, name=tag + "dw_br0"), matmul_tn(s["yb"], dp1, name=tag + "dw_br1"),
                      matmul_tn(s["yc"], dp2, name=tag + "dw_br2")])
    g["w_branch"] = _cols_split(dwbr.reshape(3 * BW, D))
    day, dxc, dprb, dpib, dba, dbx, dlam = lru_bwd(dya, z, s["h"], s["xc"], w["wa"], w["wx"],
                                                   w["lru_ba"], w["lru_bx"], w["lru_lambda"], name=tag + "lru_bwd")
    dax, dcw, dcb = conv_bwd(dxc, z, w["cw"], name=tag + "conv_bwd")
    g["lru_wa"] = _diag_blocks(matmul_tn(s["xcb"], dprb, name=tag + "dw_lru_a"))
    g["lru_wx"] = _diag_blocks(matmul_tn(s["xcb"], dpib, name=tag + "dw_lru_x"))
    g["lru_ba"], g["lru_bx"], g["lru_lambda"] = dba[0], dbx[0], dlam[0]
    g["conv_w"], g["conv_b"] = _cols_split(dcw), dcb[0]
    dbq, dbk, dbv, dbr, dmisc_g, dpreb, dbgg, dng = gla_bwd(dyb, z, zb, s["states"], w["wg2p"], w["gla_b_g"],
                                                            w["gla_norm_g"], name=tag + "gla_bwd")
    miscb = zb[:, MISC:]
    g["gla_w_g2"] = _cols_split(matmul_tn(miscb, dpreb, name=tag + "dw_g2")[:LOW_W])
    g["gla_b_g"], g["gla_norm_g"] = dbgg[0], dng[0]
    dl = fox_delta(dyc, s["ycf"], name=tag + "fox_delta")
    t = z.shape[0]
    dlq = dl[:, :FOX_H].T[:, None, :]
    dcq, dck, dcv, dfk, dfq = fox_bwd_t(zb, dyc, s["fq"], s["fk"], s["lse"], dlq, name=tag + "fox_bwd")
    dfc = jnp.pad((dfk[:, 0, :] + dfq.reshape(FOX_H, t)).T, ((0, 0), (LOW_W, 128 - LOW_W - FOX_H)))
    dmiscb, dbf = fox_dcf(dfc, z, w["bfp"], dmisc_g, name=tag + "fox_dcf")
    g["fox_b_f"] = dbf[0, LOW_W:LOW_W + FOX_H]
    dz = jnp.concatenate([dax, day, dbq, dbk, dbv, dbr, dcq, dck, dcv, dgl0, dgl1, dgl2, dmiscb], axis=1)
    dx1 = matmul(dz, w["win"], nt=True, res=dr2, res_scale=ALPHA, tm=1024, tn=1024, tk=_pick(ZW, (2432,)),
                 name=tag + "mix_dx")
    g["w_in"] = _regroup_out_shards(matmul_tn(s["x1b"], dz, name=tag + "dw_in"))
    dx0, g["ffn1_w_up"], g["ffn1_w_down"], g["ln1_g"], g["ln1_b"] = _ffn_bwd(dx1, s, w, "1", s["x0b"], l,
                                                                             tag + "ffn1_")
    return dx0, g


def _local_step(x, p, target, gw, small):
    xcur = x
    xb = xcur.astype(BF16)
    layer_w, saved = [], []
    for l in range(DEPTH):
        w = _layer_weights(gw, small, l)
        xcur, xb, s = _layer_fwd(xcur, xb, p[l].astype(BF16), w, l)
        layer_w.append(w)
        saved.append(s)
    dy, sq = loss_head(xcur, target, name="loss_head")
    grads = [None] * DEPTH
    for l in reversed(range(DEPTH)):
        dy, grads[l] = _layer_bwd(dy, saved[l], layer_w[l], l)
    return 0.5 * jnp.sum(sq) / float(D), dy, grads


def kernel(x, p, ffn1_w_up, ffn1_w_down, ln1_g, ln1_b, w_in, conv_w, conv_b, lru_wa, lru_ba, lru_wx, lru_bx, lru_lambda, gla_w_g2, gla_b_g, gla_norm_g, fox_b_f, w_branch, w_out, ln2_g, ln2_b, ffn2_w_up, ffn2_w_down, ln3_g, ln3_b, ple_w_proj, ple_w_gate, ple_b_gate, ln4_g, ln4_b, loss_target, m_ffn1_w_up, m_ffn1_w_down, m_ln1_g, m_ln1_b, m_w_in, m_conv_w, m_conv_b, m_lru_wa, m_lru_ba, m_lru_wx, m_lru_bx, m_lru_lambda, m_gla_w_g2, m_gla_b_g, m_gla_norm_g, m_fox_b_f, m_w_branch, m_w_out, m_ln2_g, m_ln2_b, m_ffn2_w_up, m_ffn2_w_down, m_ln3_g, m_ln3_b, m_ple_w_proj, m_ple_w_gate, m_ple_b_gate, m_ln4_g, m_ln4_b, v_ffn1_w_up, v_ffn1_w_down, v_ln1_g, v_ln1_b, v_w_in, v_conv_w, v_conv_b, v_lru_wa, v_lru_ba, v_lru_wx, v_lru_bx, v_lru_lambda, v_gla_w_g2, v_gla_b_g, v_gla_norm_g, v_fox_b_f, v_w_branch, v_w_out, v_ln2_g, v_ln2_b, v_ffn2_w_up, v_ffn2_w_down, v_ln3_g, v_ln3_b, v_ple_w_proj, v_ple_w_gate, v_ple_b_gate, v_ln4_g, v_ln4_b):
    args = dict(locals())
    wts = {n: args[n] for n in WEIGHTS}
    mom = {n: args["m_" + n] for n in WEIGHTS}
    var = {n: args["v_" + n] for n in WEIGHTS}
    cx, cy, cc = lax.axis_index("x"), lax.axis_index("y"), lax.axis_index("c")

    shards = [wts[n].reshape((DEPTH,) + rc).astype(F32 if n == "conv_w" else BF16) for n, rc in SHARDED]
    gw = dict(zip([n for n, _ in SHARDED], gather_weights(shards)))
    small = {n: wts[n] for n in SMALL}

    loss_local, dx, grads = _local_step(x[0], p[:, 0], loss_target[0], gw, small)
    loss = lax.psum(loss_local, ("x", "y", "c"))
    grad_x = dx[None]

    core = jnp.reshape(cc, (1,)).astype(jnp.int32)
    chip = jnp.reshape(2 * cx + cy, (1,)).astype(jnp.int32)
    g0 = [grads[0][n] for n, _ in SHARDED]
    g1 = [grads[1][n] for n, _ in SHARDED]
    rb = pair_exchange(g0, g1)
    s1 = [pair_add(a0, a1, r, core, name="pair_add_" + n) for (n, _), a0, a1, r in zip(SHARDED, g0, g1, rb)]
    rb2 = chip_exchange([sb for _, sb in s1])
    s2 = [chip_add(sf, r, chip, core, name="chip_add_" + n) for (n, _), (sf, _), r in zip(SHARDED, s1, rb2)]
    gsh = dict(zip([n for n, _ in SHARDED], pair_share(s2)))

    pieces, spans, row = [], {}, 0
    for n in SMALL:
        flat = jnp.stack([grads[l][n] for l in range(DEPTH)]).reshape(-1)
        rows = -(-flat.shape[0] // (8 * PACK_W)) * 8
        pieces.append(jnp.pad(flat, (0, rows * PACK_W - flat.shape[0])).reshape(rows, PACK_W))
        spans[n] = (row, rows)
        row += rows
    gs = jnp.concatenate(pieces, axis=0)
    me = jnp.reshape(4 * cx + 2 * cy + cc, (1,)).astype(jnp.int32)
    gsum = small_add(gs, small_exchange(gs), me)

    gout, delta, new_m, new_v = {}, {}, {}, {}
    for n in WEIGHTS:
        shp = wts[n].shape
        if n in gsh:
            view = gsh[n].shape
            g = gsh[n]
        else:
            view = (1, DEPTH, wts[n].size // DEPTH)
            r0, rows = spans[n]
            g = gsum[r0:r0 + rows].reshape(-1)[:wts[n].size].reshape(view)
        d, mn, vn = adamw(wts[n].reshape(view), g, mom[n].reshape(view), var[n].reshape(view), name="adamw_" + n)
        gout[n], delta[n], new_m[n], new_v[n] = g.reshape(shp), d.reshape(shp), mn.reshape(shp), vn.reshape(shp)

    return (loss, grad_x, *[gout[n] for n in WEIGHTS], *[delta[n] for n in WEIGHTS],
            *[new_m[n] for n in WEIGHTS], *[new_v[n] for n in WEIGHTS])
```

```python
import functools
import math

import jax
import jax.numpy as jnp
from jax import lax
from jax.experimental import pallas as pl
from jax.experimental.pallas import tpu as pltpu

F32 = jnp.float32
BF16 = jnp.bfloat16

D = 1024
DFF = 2816
BW = 512
PLE = 256
DEPTH = 2
ALPHA = (2 * DEPTH) ** 0.25
LN_EPS = 1e-5
RMS_EPS = 1e-6
LRU_C = 8.0
GLA_TAU = 16.0
CHUNK = 64
D_IN = 7192
ZW = 7296
AX, AY, BQ, BK, BV, BR, CQ, CK, CV, G0, MISC = 0, 512, 1024, 1280, 1536, 2048, 2560, 3072, 3584, 4096, 7168
LOW_W, FOX_H = 16, 8
ADAM_LR, ADAM_B1, ADAM_B2, ADAM_EPS, ADAM_WD, ADAM_STEP = 0.001, 0.9, 0.999, 1e-08, 0.01, 10
PACK_W = 1024
VMEM_LIMIT = 56 << 20

MESH = pl.DeviceIdType.MESH
ANY = pl.BlockSpec(memory_space=pl.ANY)


def _pcall(body, **kw):
    return pl.pallas_call(body, **kw)


def _cp(*dims):
    return pltpu.CompilerParams(dimension_semantics=dims, vmem_limit_bytes=VMEM_LIMIT)


def _dot(a, b):
    return jnp.dot(a, b, preferred_element_type=F32)


def _dot_nt(a, b):
    return lax.dot_general(a, b, (((1,), (1,)), ((), ())), preferred_element_type=F32)


def _dot_tn(a, b):
    return lax.dot_general(a, b, (((0,), (0,)), ((), ())), preferred_element_type=F32)


def _dot_hi(a, b):
    return jnp.dot(a, b, preferred_element_type=F32, precision=lax.Precision.HIGHEST)


def _sigmoid(x):
    return 1.0 / (1.0 + jnp.exp(-x))


def _softplus(x):
    return jnp.maximum(x, 0.0) + jnp.log(1.0 + jnp.exp(-jnp.abs(x)))


def _log_sigmoid(x):
    return -_softplus(-x)


def _expm1(x):
    poly = x * (1.0 + x * (0.5 + x * (1.0 / 6.0 + x * (1.0 / 24.0 + x * (1.0 / 120.0 + x * (1.0 / 720.0))))))
    return jnp.where(jnp.abs(x) < 0.1, poly, jnp.exp(x) - 1.0)


_GELU_C = math.sqrt(2.0 / math.pi)


def _gelu(x):
    return 0.5 * x * (1.0 + jnp.tanh(_GELU_C * (x + 0.044715 * x * x * x)))


def _gelu_grad(x):
    t = jnp.tanh(_GELU_C * (x + 0.044715 * x * x * x))
    return 0.5 * (1.0 + t) + 0.5 * x * (1.0 - t * t) * _GELU_C * (1.0 + 3.0 * 0.044715 * x * x)


def _ln_stats(r):
    mu = jnp.mean(r, axis=-1, keepdims=True)
    xc = r - mu
    var = jnp.mean(xc * xc, axis=-1, keepdims=True)
    return xc, lax.rsqrt(var + LN_EPS)


def _pick(n, cands):
    for c in cands:
        if n % c == 0:
            return c
    return n


def _rows(tm, w, col=0):
    return pl.BlockSpec((tm, w), lambda i: (i, col))


def _fix(shape):
    nd = len(shape)
    return pl.BlockSpec(shape, lambda i: (0,) * nd)


def _col_chunks(n, width=256):
    return [slice(c, min(c + width, n)) for c in range(0, n, width)]


def _layer(l, shape):
    nd = len(shape)
    return pl.BlockSpec((None,) + tuple(shape), lambda i: (l,) + (0,) * nd)


def matmul(a, b, *, name, nt=False, b_lead=(), res=None, res_scale=1.0, also_bf16=False, tm=512, tn=512,
           tk=None):
    m, k = a.shape
    n = b.shape[-2] if nt else b.shape[-1]
    tm, tn = min(tm, m), min(tn, n)
    tk = k if tk is None else tk
    nk = k // tk
    has_res = res is not None
    lead = tuple(b_lead)
    dot = _dot_nt if nt else _dot

    def body(*refs):
        a_ref, b_ref = refs[0], refs[1]
        pos = 2
        r_ref = None
        if has_res:
            r_ref = refs[pos]
            pos += 1
        o_ref = refs[pos]
        pos += 1
        ob_ref = None
        if also_bf16:
            ob_ref = refs[pos]
            pos += 1

        def finish(v):
            if has_res:
                v = v + res_scale * r_ref[...]
            o_ref[...] = v
            if also_bf16:
                ob_ref[...] = v.astype(BF16)

        if nk == 1:
            finish(dot(a_ref[...], b_ref[...]))
            return
        acc = refs[pos]
        kk = pl.program_id(2)

        @pl.when(kk == 0)
        def _():
            acc[...] = jnp.zeros_like(acc)

        acc[...] += dot(a_ref[...], b_ref[...])

        @pl.when(kk == nk - 1)
        def _():
            finish(acc[...])

    none = (None,) * len(lead)
    if nt:
        b_spec = pl.BlockSpec(none + (tn, tk), lambda j, i, kk: lead + (j, kk))
    else:
        b_spec = pl.BlockSpec(none + (tk, tn), lambda j, i, kk: lead + (kk, j))
    in_specs = [pl.BlockSpec((tm, tk), lambda j, i, kk: (i, kk)), b_spec]
    args = [a, b]
    if has_res:
        in_specs.append(pl.BlockSpec((tm, tn), lambda j, i, kk: (i, j)))
        args.append(res)
    out_shape = [jax.ShapeDtypeStruct((m, n), F32)]
    out_specs = [pl.BlockSpec((tm, tn), lambda j, i, kk: (i, j))]
    if also_bf16:
        out_shape.append(jax.ShapeDtypeStruct((m, n), BF16))
        out_specs.append(pl.BlockSpec((tm, tn), lambda j, i, kk: (i, j)))
    out = _pcall(body, name=name, grid=(n // tn, m // tm, nk), in_specs=in_specs, out_specs=out_specs,
                 out_shape=out_shape, scratch_shapes=[pltpu.VMEM((tm, tn), F32)] if nk > 1 else [],
                 compiler_params=_cp("parallel", "parallel", "arbitrary"))(*args)
    return out if also_bf16 else out[0]


def matmul_tn(a, b, *, name):
    t, k = a.shape
    n = b.shape[1]
    tk = _pick(k, (1024, 1408, 512, 256, 128))
    tn = _pick(n, (1024, 1408, 2432, 512, 256, 128))
    tt = min(1024 if tk * tn > (1 << 20) else 2048, t)
    nt = t // tt

    def body(a_ref, b_ref, o_ref):
        @pl.when(pl.program_id(2) == 0)
        def _():
            o_ref[...] = jnp.zeros_like(o_ref)

        o_ref[...] += _dot_tn(a_ref[...], b_ref[...])

    return _pcall(body, name=name, grid=(k // tk, n // tn, nt),
                  in_specs=[pl.BlockSpec((tt, tk), lambda i, j, s: (s, i)),
                            pl.BlockSpec((tt, tn), lambda i, j, s: (s, j))],
                  out_specs=pl.BlockSpec((tk, tn), lambda i, j, s: (i, j)),
                  out_shape=jax.ShapeDtypeStruct((k, n), F32),
                  compiler_params=_cp("parallel", "parallel", "arbitrary"))(a, b)


UPW = 1408


def matmul_tn_up(a, dgate, dup, *, name):
    t, k = a.shape
    tt = min(1024, t)
    tk = 1024

    def body(a_ref, g_ref, u_ref, o_ref):
        j = pl.program_id(1)

        @pl.when(pl.program_id(2) == 0)
        def _():
            o_ref[...] = jnp.zeros_like(o_ref)

        @pl.when(j < 2)
        def _():
            o_ref[...] += _dot_tn(a_ref[...], g_ref[...])

        @pl.when(j >= 2)
        def _():
            o_ref[...] += _dot_tn(a_ref[...], u_ref[...])

    return _pcall(body, name=name, grid=(k // tk, 4, t // tt),
                  in_specs=[pl.BlockSpec((tt, tk), lambda i, j, s: (s, i)),
                            pl.BlockSpec((tt, UPW), lambda i, j, s: (jnp.where(j < 2, s, 0), jnp.minimum(j, 1))),
                            pl.BlockSpec((tt, UPW), lambda i, j, s: (jnp.where(j >= 2, s, 0), jnp.maximum(j - 2, 0)))],
                  out_specs=pl.BlockSpec((None, tk, UPW), lambda i, j, s: (j, i, 0)),
                  out_shape=jax.ShapeDtypeStruct((4, k, UPW), F32),
                  compiler_params=_cp("parallel", "parallel", "arbitrary"))(a, dgate, dup)


def ffn_dx(dgate, dup, wup, l, res, *, name):
    t = dgate.shape[0]
    tm, tn = min(1024, t), 1024

    def body(g_ref, u_ref, w_ref, r_ref, o_ref, acc):
        kk = pl.program_id(2)

        @pl.when(kk == 0)
        def _():
            acc[...] = jnp.zeros_like(acc)

        @pl.when(kk < 2)
        def _():
            acc[...] += _dot_nt(g_ref[...], w_ref[...])

        @pl.when(kk >= 2)
        def _():
            acc[...] += _dot_nt(u_ref[...], w_ref[...])

        @pl.when(kk == 3)
        def _():
            o_ref[...] = acc[...] + ALPHA * r_ref[...]

    return _pcall(body, name=name, grid=(D // tn, t // tm, 4),
                  in_specs=[pl.BlockSpec((tm, UPW), lambda j, i, kk: (i, jnp.minimum(kk, 1))),
                            pl.BlockSpec((tm, UPW), lambda j, i, kk: (i, jnp.maximum(kk - 2, 0))),
                            pl.BlockSpec((None, None, tn, UPW), lambda j, i, kk: (l, kk, j, 0)),
                            pl.BlockSpec((tm, tn), lambda j, i, kk: (i, j))],
                  out_specs=pl.BlockSpec((tm, tn), lambda j, i, kk: (i, j)),
                  out_shape=jax.ShapeDtypeStruct((t, D), F32),
                  scratch_shapes=[pltpu.VMEM((tm, tn), F32)],
                  compiler_params=_cp("parallel", "parallel", "arbitrary"))(dgate, dup, wup, res)


def ffn_up(xb, wup, l, *, name):
    t = xb.shape[0]
    tm, tn = min(1024, t), UPW

    def body(x_ref, wg_ref, wu_ref, g_ref, u_ref, a_ref):
        x = x_ref[...]
        for cols in _col_chunks(tn):
            g = _dot(x, wg_ref[:, cols])
            u = _dot(x, wu_ref[:, cols])
            g_ref[:, cols] = g.astype(BF16)
            u_ref[:, cols] = u.astype(BF16)
            a_ref[:, cols] = (g * _sigmoid(g) * u).astype(BF16)

    blk = pl.BlockSpec((tm, tn), lambda j, i: (i, j))
    return _pcall(body, name=name, grid=(DFF // tn, t // tm),
                  in_specs=[pl.BlockSpec((tm, D), lambda j, i: (i, 0)),
                            pl.BlockSpec((None, None, D, tn), lambda j, i: (l, j, 0, 0)),
                            pl.BlockSpec((None, None, D, tn), lambda j, i: (l, 2 + j, 0, 0))],
                  out_specs=[blk, blk, blk],
                  out_shape=[jax.ShapeDtypeStruct((t, DFF), BF16)] * 3,
                  compiler_params=_cp("parallel", "parallel"))(xb, wup, wup)


def matmul_res_ln(a, w, l, res, g, b, *, mm_scale, name):
    t, k = a.shape
    tm = min(512, t)

    def body(a_ref, w_ref, res_ref, g_ref, b_ref, r_ref, y_ref, yb_ref):
        f = _dot(a_ref[...], w_ref[...])
        r = ALPHA * res_ref[...] + mm_scale * f
        xc, rstd = _ln_stats(r)
        y = xc * rstd * g_ref[...] + b_ref[...]
        r_ref[...] = r
        y_ref[...] = y
        yb_ref[...] = y.astype(BF16)

    return _pcall(body, name=name, grid=(t // tm,),
                  in_specs=[_rows(tm, k), _layer(l, (k, D)), _rows(tm, D), _fix((1, D)), _fix((1, D))],
                  out_specs=[_rows(tm, D)] * 3,
                  out_shape=[jax.ShapeDtypeStruct((t, D), F32), jax.ShapeDtypeStruct((t, D), F32),
                             jax.ShapeDtypeStruct((t, D), BF16)],
                  compiler_params=_cp("parallel"))(a, w, res, g, b)


def ln_bwd(dy, r, g, *, out_scale, name):
    t = dy.shape[0]
    tm = min(256, t)

    def body(dy_ref, r_ref, g_ref, dr_ref, drb_ref, dg_ref, db_ref):
        @pl.when(pl.program_id(0) == 0)
        def _():
            dg_ref[...] = jnp.zeros_like(dg_ref)
            db_ref[...] = jnp.zeros_like(db_ref)

        xc, rstd = _ln_stats(r_ref[...])
        xhat = xc * rstd
        d = dy_ref[...]
        dxh = d * g_ref[...]
        dr = rstd * (dxh - jnp.mean(dxh, axis=-1, keepdims=True)
                     - xhat * jnp.mean(dxh * xhat, axis=-1, keepdims=True))
        dr_ref[...] = dr
        drb_ref[...] = (out_scale * dr).astype(BF16)
        dg_ref[...] += jnp.sum(d * xhat, axis=0, keepdims=True)
        db_ref[...] += jnp.sum(d, axis=0, keepdims=True)

    return _pcall(body, name=name, grid=(t // tm,),
                  in_specs=[_rows(tm, D), _rows(tm, D), _fix((1, D))],
                  out_specs=[_rows(tm, D), _rows(tm, D), _fix((1, D)), _fix((1, D))],
                  out_shape=[jax.ShapeDtypeStruct((t, D), F32), jax.ShapeDtypeStruct((t, D), BF16),
                             jax.ShapeDtypeStruct((1, D), F32), jax.ShapeDtypeStruct((1, D), F32)],
                  compiler_params=_cp("arbitrary"))(dy, r, g)


def ffn_down_bwd(dfb, wd, l, gate, up, *, name):
    t = dfb.shape[0]
    tm, tn = min(1024, t), UPW
    nj = DFF // tn

    def body(df_ref, w_ref, g_ref, u_ref, dg_ref, du_ref):
        df = df_ref[...]
        for cols in _col_chunks(tn):
            da = _dot_nt(df, w_ref[cols, :])
            g = g_ref[:, cols].astype(F32)
            s = _sigmoid(g)
            gs = g * s
            dg_ref[:, cols] = (da * u_ref[:, cols].astype(F32) * (s + gs * (1.0 - s))).astype(BF16)
            du_ref[:, cols] = (da * gs).astype(BF16)

    blk = pl.BlockSpec((tm, tn), lambda j, i: (i, j))
    return _pcall(body, name=name, grid=(nj, t // tm),
                  in_specs=[pl.BlockSpec((tm, D), lambda j, i: (i, 0)),
                            pl.BlockSpec((None, tn, D), lambda j, i: (l, j, 0)), blk, blk],
                  out_specs=[blk, blk],
                  out_shape=[jax.ShapeDtypeStruct((t, DFF), BF16), jax.ShapeDtypeStruct((t, DFF), BF16)],
                  compiler_params=_cp("parallel", "parallel"))(dfb, wd, gate, up)


def ple_fwd(xb, x, pb, wgate, l, wproj, bgate, g, b, *, name):
    t = x.shape[0]
    tm = min(512, t)

    def body(xb_ref, x_ref, p_ref, wg_ref, wp_ref, bg_ref, g_ref, b_ref, r_ref, y_ref, yb_ref):
        gl = _dot(xb_ref[...], wg_ref[...]) + bg_ref[...]
        pe = _dot(p_ref[...], wp_ref[...])
        r = ALPHA * x_ref[...] + _sigmoid(gl) * pe
        xc, rstd = _ln_stats(r)
        y = xc * rstd * g_ref[...] + b_ref[...]
        r_ref[...] = r
        y_ref[...] = y
        yb_ref[...] = y.astype(BF16)

    return _pcall(body, name=name, grid=(t // tm,),
                  in_specs=[_rows(tm, D), _rows(tm, D), _rows(tm, PLE), _layer(l, (D, D)), _fix((PLE, D)),
                            _fix((1, D)), _fix((1, D)), _fix((1, D))],
                  out_specs=[_rows(tm, D)] * 3,
                  out_shape=[jax.ShapeDtypeStruct((t, D), F32), jax.ShapeDtypeStruct((t, D), F32),
                             jax.ShapeDtypeStruct((t, D), BF16)],
                  compiler_params=_cp("parallel"))(xb, x, pb, wgate, wproj, bgate, g, b)


def ple_bwd(dy, r, xb, pb, wgate, l, wproj, bgate, g, *, name):
    t = dy.shape[0]
    tm = min(512, t)

    def body(dy_ref, r_ref, xb_ref, p_ref, wg_ref, wp_ref, bg_ref, g_ref,
             dr_ref, dgl_ref, dpe_ref, dg_ref, db_ref, dbg_ref):
        @pl.when(pl.program_id(0) == 0)
        def _():
            dg_ref[...] = jnp.zeros_like(dg_ref)
            db_ref[...] = jnp.zeros_like(db_ref)
            dbg_ref[...] = jnp.zeros_like(dbg_ref)

        xc, rstd = _ln_stats(r_ref[...])
        xhat = xc * rstd
        d = dy_ref[...]
        dxh = d * g_ref[...]
        dr = rstd * (dxh - jnp.mean(dxh, axis=-1, keepdims=True)
                     - xhat * jnp.mean(dxh * xhat, axis=-1, keepdims=True))
        s = _sigmoid(_dot(xb_ref[...], wg_ref[...]) + bg_ref[...])
        pe = _dot(p_ref[...], wp_ref[...])
        dgl = dr * pe * s * (1.0 - s)
        dr_ref[...] = dr
        dgl_ref[...] = dgl.astype(BF16)
        dpe_ref[...] = (dr * s).astype(BF16)
        dg_ref[...] += jnp.sum(d * xhat, axis=0, keepdims=True)
        db_ref[...] += jnp.sum(d, axis=0, keepdims=True)
        dbg_ref[...] += jnp.sum(dgl, axis=0, keepdims=True)

    vec = jax.ShapeDtypeStruct((1, D), F32)
    return _pcall(body, name=name, grid=(t // tm,),
                  in_specs=[_rows(tm, D), _rows(tm, D), _rows(tm, D), _rows(tm, PLE), _layer(l, (D, D)),
                            _fix((PLE, D)), _fix((1, D)), _fix((1, D))],
                  out_specs=[_rows(tm, D), _rows(tm, D), _rows(tm, D), _fix((1, D)), _fix((1, D)), _fix((1, D))],
                  out_shape=[jax.ShapeDtypeStruct((t, D), F32), jax.ShapeDtypeStruct((t, D), BF16),
                             jax.ShapeDtypeStruct((t, D), BF16), vec, vec, vec],
                  compiler_params=_cp("arbitrary"))(dy, r, xb, pb, wgate, wproj, bgate, g)


def loss_head(y, tgt, *, name):
    t = y.shape[0]
    tm = min(256, t)

    def body(y_ref, t_ref, dy_ref, sq_ref):
        @pl.when(pl.program_id(0) == 0)
        def _():
            sq_ref[...] = jnp.zeros_like(sq_ref)

        e = y_ref[...] - t_ref[...]
        dy_ref[...] = e / float(D)
        sq_ref[...] += jnp.sum(e * e, axis=0, keepdims=True)

    return _pcall(body, name=name, grid=(t // tm,),
                  in_specs=[_rows(tm, D), _rows(tm, D)],
                  out_specs=[_rows(tm, D), _fix((1, D))],
                  out_shape=[jax.ShapeDtypeStruct((t, D), F32), jax.ShapeDtypeStruct((1, D), F32)],
                  compiler_params=_cp("arbitrary"))(y, tgt)


def _lru_gates(xc, wa_ref, wx_ref, ba_ref, bx_ref, lam_ref):
    xcb = xc.astype(BF16)
    r = _sigmoid(_dot(xcb, wa_ref[...]) + ba_ref[...])
    ig = _sigmoid(_dot(xcb, wx_ref[...]) + bx_ref[...])
    sp = _softplus(-lam_ref[...])
    la = -LRU_C * r * sp
    a = jnp.exp(la)
    mult = jnp.sqrt(-_expm1(2.0 * la))
    return r, ig, sp, la, a, mult


def lru_fwd(z, cw, cb, wa, wx, ba, bx, lam, *, name):
    t = z.shape[0]
    tm = min(256, t)
    hb = tm // 8

    def body(ax_ref, prev_ref, ay_ref, cw_ref, cb_ref, wa_ref, wx_ref, ba_ref, bx_ref, lam_ref,
             xc_ref, xcb_ref, h_ref, ya_ref, xs, a_s, b_s, hc):
        i = pl.program_id(0)

        @pl.when(i == 0)
        def _():
            hc[...] = jnp.zeros_like(hc)

        xs[0:8, :] = jnp.where(i == 0, 0.0, prev_ref[...])
        xs[8:, :] = ax_ref[...]
        xc = cb_ref[...] + cw_ref[0:1, :] * xs[5:5 + tm, :]
        for k in range(1, 4):
            xc = xc + cw_ref[k:k + 1, :] * xs[5 + k:5 + k + tm, :]
        r, ig, sp, la, a, mult = _lru_gates(xc, wa_ref, wx_ref, ba_ref, bx_ref, lam_ref)
        a_s[...] = a
        b_s[...] = mult * (ig * xc)
        xc_ref[...] = xc
        xcb_ref[...] = xc.astype(BF16)

        def step(g, h):
            base = pl.multiple_of(g * 8, 8)
            a8 = a_s[pl.ds(base, 8), :]
            b8 = b_s[pl.ds(base, 8), :]
            for j in range(8):
                h = a8[j:j + 1, :] * h + b8[j:j + 1, :]
                h_ref[pl.ds(base + j, 1), :] = h
            return h

        hc[...] = lax.fori_loop(0, tm // 8, step, hc[...])
        ya_ref[...] = (_gelu(ay_ref[...]) * h_ref[...]).astype(BF16)

    vec = _fix((1, BW))
    return _pcall(body, name=name, grid=(t // tm,),
                  in_specs=[_rows(tm, BW, AX // BW),
                            pl.BlockSpec((8, BW), lambda i: (jnp.maximum(i * hb - 1, 0), AX // BW)),
                            _rows(tm, BW, AY // BW), _fix((4, BW)), vec, _fix((BW, BW)), _fix((BW, BW)),
                            vec, vec, vec],
                  out_specs=[_rows(tm, BW)] * 4,
                  out_shape=[jax.ShapeDtypeStruct((t, BW), F32), jax.ShapeDtypeStruct((t, BW), BF16),
                             jax.ShapeDtypeStruct((t, BW), F32), jax.ShapeDtypeStruct((t, BW), BF16)],
                  scratch_shapes=[pltpu.VMEM((tm + 8, BW), F32), pltpu.VMEM((tm, BW), F32),
                                  pltpu.VMEM((tm, BW), F32), pltpu.VMEM((1, BW), F32)],
                  compiler_params=_cp("arbitrary"))(z, z, z, cw, cb, wa, wx, ba, bx, lam)


def lru_bwd(dya, z, h, xc, wa, wx, ba, bx, lam, *, name):
    t = dya.shape[0]
    tm = min(256, t)
    nb = t // tm
    hb = tm // 8

    def body(dya_ref, ay_ref, h_ref, hprev_ref, xc_ref, wa_ref, wx_ref, ba_ref, bx_ref,
             lam_ref, day_ref, dxc_ref, dpr_ref, dpi_ref, dba_ref, dbx_ref, dlam_ref,
             hs, a_s, g_s, d_s, cc):
        i = pl.program_id(0)

        @pl.when(i == 0)
        def _():
            cc[...] = jnp.zeros_like(cc)
            dba_ref[...] = jnp.zeros_like(dba_ref)
            dbx_ref[...] = jnp.zeros_like(dbx_ref)
            dlam_ref[...] = jnp.zeros_like(dlam_ref)

        xc = xc_ref[...]
        r, ig, sp, la, a, mult = _lru_gates(xc, wa_ref, wx_ref, ba_ref, bx_ref, lam_ref)
        ay = ay_ref[...]
        dya = dya_ref[...]
        hcur = h_ref[...]
        day_ref[...] = (dya * hcur * _gelu_grad(ay)).astype(BF16)
        a_s[...] = a
        g_s[...] = dya * _gelu(ay)

        def step(gg, cin):
            g = tm // 8 - 1 - gg
            base = pl.multiple_of(g * 8, 8)
            a8 = a_s[pl.ds(base, 8), :]
            g8 = g_s[pl.ds(base, 8), :]
            for j in range(7, -1, -1):
                d = g8[j:j + 1, :] + cin
                d_s[pl.ds(base + j, 1), :] = d
                cin = a8[j:j + 1, :] * d
            return cin

        cc[...] = lax.fori_loop(0, tm // 8, step, cc[...])
        dht = d_s[...]
        hs[0:8, :] = jnp.where(i == nb - 1, 0.0, hprev_ref[...])
        hs[8:, :] = hcur
        da = dht * hs[7:7 + tm, :]
        dmult = dht * ig * xc
        dig = dht * mult * xc
        dla = da * a - dmult * a * a / mult
        dpr = dla * (-LRU_C * sp) * r * (1.0 - r)
        dpi = dig * ig * (1.0 - ig)
        dprb = dpr.astype(BF16)
        dpib = dpi.astype(BF16)
        dxc_ref[...] = dht * mult * ig + _dot_nt(dprb, wa_ref[...]) + _dot_nt(dpib, wx_ref[...])
        dpr_ref[...] = dprb
        dpi_ref[...] = dpib
        dba_ref[...] += jnp.sum(dpr, axis=0, keepdims=True)
        dbx_ref[...] += jnp.sum(dpi, axis=0, keepdims=True)
        dlam_ref[...] += jnp.sum(dla * (-LRU_C * r), axis=0, keepdims=True) * (-_sigmoid(-lam_ref[...]))

    vec = _fix((1, BW))
    mat = _fix((BW, BW))
    rev = lambda col: pl.BlockSpec((tm, BW), lambda i: (nb - 1 - i, col))
    vshape = jax.ShapeDtypeStruct((1, BW), F32)
    return _pcall(body, name=name, grid=(nb,),
                  in_specs=[rev(0), rev(AY // BW), rev(0),
                            pl.BlockSpec((8, BW), lambda i: (jnp.maximum((nb - 1 - i) * hb - 1, 0), 0)),
                            rev(0), mat, mat, vec, vec, vec],
                  out_specs=[rev(0), rev(0), rev(0), rev(0), vec, vec, vec],
                  out_shape=[jax.ShapeDtypeStruct((t, BW), BF16), jax.ShapeDtypeStruct((t, BW), F32),
                             jax.ShapeDtypeStruct((t, BW), BF16), jax.ShapeDtypeStruct((t, BW), BF16),
                             vshape, vshape, vshape],
                  scratch_shapes=[pltpu.VMEM((tm + 8, BW), F32), pltpu.VMEM((tm, BW), F32),
                                  pltpu.VMEM((tm, BW), F32), pltpu.VMEM((tm, BW), F32),
                                  pltpu.VMEM((1, BW), F32)],
                  compiler_params=_cp("arbitrary"))(dya, z, h, h, xc, wa, wx, ba, bx, lam)


def conv_bwd(dxc, z, cw, *, name):
    t = dxc.shape[0]
    tm = min(256, t)
    nb = t // tm
    hb = tm // 8

    def body(d_ref, dnext_ref, ax_ref, prev_ref, cw_ref, dax_ref, dcw_ref, dcb_ref, ds, xs):
        i = pl.program_id(0)

        @pl.when(i == 0)
        def _():
            dcw_ref[...] = jnp.zeros_like(dcw_ref)
            dcb_ref[...] = jnp.zeros_like(dcb_ref)

        d = d_ref[...]
        ds[0:tm, :] = d
        ds[tm:, :] = jnp.where(i == nb - 1, 0.0, dnext_ref[...])
        xs[0:8, :] = jnp.where(i == 0, 0.0, prev_ref[...])
        xs[8:, :] = ax_ref[...]
        dax = cw_ref[3:4, :] * d
        for k in range(3):
            dax = dax + cw_ref[k:k + 1, :] * ds[3 - k:3 - k + tm, :]
        dax_ref[...] = dax.astype(BF16)
        for k in range(4):
            dcw_ref[k:k + 1, :] += jnp.sum(d * xs[5 + k:5 + k + tm, :], axis=0, keepdims=True)
        dcb_ref[...] += jnp.sum(d, axis=0, keepdims=True)

    return _pcall(body, name=name, grid=(nb,),
                  in_specs=[_rows(tm, BW),
                            pl.BlockSpec((8, BW), lambda i: (jnp.minimum((i + 1) * hb, nb * hb - 1), 0)),
                            _rows(tm, BW, AX // BW),
                            pl.BlockSpec((8, BW), lambda i: (jnp.maximum(i * hb - 1, 0), AX // BW)),
                            _fix((4, BW))],
                  out_specs=[_rows(tm, BW), _fix((4, BW)), _fix((1, BW))],
                  out_shape=[jax.ShapeDtypeStruct((t, BW), BF16), jax.ShapeDtypeStruct((4, BW), F32),
                             jax.ShapeDtypeStruct((1, BW), F32)],
                  scratch_shapes=[pltpu.VMEM((tm + 8, BW), F32), pltpu.VMEM((tm + 8, BW), F32)],
                  compiler_params=_cp("arbitrary"))(dxc, dxc, z, z, cw)


GLA_CB = 4


def _gla_consts():
    tri = (jnp.arange(CHUNK)[:, None] >= jnp.arange(CHUNK)[None, :]).astype(F32)
    mask = ((jnp.arange(BW)[:, None] // 128) == (jnp.arange(256)[None, :] // 64)).astype(F32)
    return tri, mask


def gla_fwd(z, zb, wg2p, bg, ng, *, name):
    t = z.shape[0]
    tm = GLA_CB * CHUNK
    nc = t // CHUNK
    tri, mask = _gla_consts()

    def body(q_ref, k_ref, v_ref, misc_ref, br_ref, w_ref, bg_ref, ng_ref, tri_ref, mask_ref,
             yb_ref, st_ref, st):
        @pl.when(pl.program_id(0) == 0)
        def _():
            st[...] = jnp.zeros_like(st)

        for c in range(GLA_CB):
            rows = slice(c * CHUNK, (c + 1) * CHUNK)
            pre = _dot(misc_ref[rows, :], w_ref[...]) + bg_ref[...]
            la = _log_sigmoid(pre) / GLA_TAU
            gc = _dot_hi(tri_ref[...], la)
            gt = gc[CHUNK - 1:CHUNK, :]
            kdec = k_ref[rows, :] * jnp.exp(gt - gc)
            delta = _dot_tn(v_ref[rows, :], kdec.astype(BF16))
            s_new = st[...] * jnp.exp(gt) + delta * mask_ref[...]
            st[...] = s_new
            st_ref[c] = s_new
            o = _dot_nt(q_ref[rows, :], s_new.astype(BF16)) * (64.0 ** -0.5)
            br = br_ref[rows, :]
            for hd in range(4):
                cols = slice(hd * 128, (hd + 1) * 128)
                oh = o[:, cols]
                rs = lax.rsqrt(jnp.mean(oh * oh, axis=-1, keepdims=True) + RMS_EPS)
                brh = br[:, cols]
                yb_ref[rows, cols] = (oh * rs * ng_ref[:, cols] * (brh * _sigmoid(brh))).astype(BF16)

    return _pcall(body, name=name, grid=(t // tm,),
                  in_specs=[_rows(tm, 256, BQ // 256), _rows(tm, 256, BK // 256), _rows(tm, BW, BV // BW),
                            _rows(tm, 128, MISC // 128), _rows(tm, BW, BR // BW), _fix((128, 256)),
                            _fix((1, 256)), _fix((1, BW)), _fix((CHUNK, CHUNK)), _fix((BW, 256))],
                  out_specs=[_rows(tm, BW), pl.BlockSpec((GLA_CB, BW, 256), lambda i: (i, 0, 0))],
                  out_shape=[jax.ShapeDtypeStruct((t, BW), BF16), jax.ShapeDtypeStruct((nc, BW, 256), F32)],
                  scratch_shapes=[pltpu.VMEM((BW, 256), F32)],
                  compiler_params=_cp("arbitrary"))(zb, z, zb, zb, z, wg2p, bg, ng, tri, mask)


def gla_bwd(dyb, z, zb, states, wg2p, bg, ng, *, name):
    t = z.shape[0]
    tm = GLA_CB * CHUNK
    nb = t // tm
    tri, mask = _gla_consts()
    triu = tri.T

    def body(dy_ref, q_ref, k_ref, v_ref, misc_ref, br_ref, st_ref, sp_ref, w_ref, bg_ref, ng_ref,
             tri_ref, triu_ref, mask_ref,
             dq_ref, dk_ref, dv_ref, dbr_ref, dmisc_ref, dpre_ref, dbg_ref, dng_ref, cc):
        i = pl.program_id(0)

        @pl.when(i == 0)
        def _():
            cc[...] = jnp.zeros_like(cc)
            dbg_ref[...] = jnp.zeros_like(dbg_ref)
            dng_ref[...] = jnp.zeros_like(dng_ref)

        last_row = lax.broadcasted_iota(jnp.int32, (CHUNK, 256), 0) == CHUNK - 1
        for c in range(GLA_CB - 1, -1, -1):
            rows = slice(c * CHUNK, (c + 1) * CHUNK)
            pre = _dot(misc_ref[rows, :], w_ref[...]) + bg_ref[...]
            la = _log_sigmoid(pre) / GLA_TAU
            gc = _dot_hi(tri_ref[...], la)
            gt = gc[CHUNK - 1:CHUNK, :]
            eg = jnp.exp(gt - gc)
            kdec = k_ref[rows, :] * eg
            e = jnp.exp(gt)
            s_n = st_ref[c]
            if c > 0:
                s_prev = st_ref[c - 1]
            else:
                s_prev = jnp.where(i == nb - 1, 0.0, sp_ref[0])
            sb = s_n.astype(BF16)
            qb = q_ref[rows, :]
            o = _dot_nt(qb, sb) * (64.0 ** -0.5)
            br = br_ref[rows, :]
            dy = dy_ref[rows, :]
            do_parts = []
            for hd in range(4):
                cols = slice(hd * 128, (hd + 1) * 128)
                oh = o[:, cols]
                rs = lax.rsqrt(jnp.mean(oh * oh, axis=-1, keepdims=True) + RMS_EPS)
                ohat = oh * rs
                brh = br[:, cols]
                sg = _sigmoid(brh)
                dyh = dy[:, cols]
                ngh = ng_ref[:, cols]
                don = dyh * (brh * sg)
                dbr_ref[rows, cols] = (dyh * (ohat * ngh) * sg * (1.0 + brh * (1.0 - sg))).astype(BF16)
                dng_ref[:, cols] += jnp.sum(don * ohat, axis=0, keepdims=True)
                doh = don * ngh
                do_parts.append(rs * (doh - ohat * jnp.mean(doh * ohat, axis=-1, keepdims=True)))
            dob = jnp.concatenate(do_parts, axis=1).astype(BF16)
            dq_ref[rows, :] = (_dot(dob, sb) * (64.0 ** -0.5)).astype(BF16)
            dst = cc[...] + _dot_tn(dob, qb) * (64.0 ** -0.5) * mask_ref[...]
            dsb = dst.astype(BF16)
            dkdec = _dot(v_ref[rows, :], dsb)
            dv_ref[rows, :] = _dot_nt(kdec.astype(BF16), dsb).astype(BF16)
            dgt = jnp.sum(dst * s_prev, axis=0, keepdims=True) * e
            dk_ref[rows, :] = (dkdec * eg).astype(BF16)
            dd = dkdec * kdec
            dgt = dgt + jnp.sum(dd, axis=0, keepdims=True)
            dgc = jnp.where(last_row, dgt - dd, -dd)
            dla = _dot_hi(triu_ref[...], dgc)
            dpre = dla * (1.0 / GLA_TAU) * _sigmoid(-pre)
            dpb = dpre.astype(BF16)
            dpre_ref[rows, :] = dpb
            dmisc_ref[rows, :] = _dot_nt(dpb, w_ref[...])
            dbg_ref[...] += jnp.sum(dpre, axis=0, keepdims=True)
            cc[...] = dst * e

    rev = lambda w, col: pl.BlockSpec((tm, w), lambda i: (nb - 1 - i, col))
    return _pcall(body, name=name, grid=(nb,),
                  in_specs=[rev(BW, 0), rev(256, BQ // 256), rev(256, BK // 256), rev(BW, BV // BW),
                            rev(128, MISC // 128), rev(BW, BR // BW),
                            pl.BlockSpec((GLA_CB, BW, 256), lambda i: (nb - 1 - i, 0, 0)),
                            pl.BlockSpec((1, BW, 256), lambda i: (jnp.maximum((nb - 1 - i) * GLA_CB - 1, 0), 0, 0)),
                            _fix((128, 256)), _fix((1, 256)), _fix((1, BW)),
                            _fix((CHUNK, CHUNK)), _fix((CHUNK, CHUNK)), _fix((BW, 256))],
                  out_specs=[rev(256, 0), rev(256, 0), rev(BW, 0), rev(BW, 0), rev(128, 0), rev(256, 0),
                             _fix((1, 256)), _fix((1, BW))],
                  out_shape=[jax.ShapeDtypeStruct((t, 256), BF16), jax.ShapeDtypeStruct((t, 256), BF16),
                             jax.ShapeDtypeStruct((t, BW), BF16), jax.ShapeDtypeStruct((t, BW), BF16),
                             jax.ShapeDtypeStruct((t, 128), F32), jax.ShapeDtypeStruct((t, 256), BF16),
                             jax.ShapeDtypeStruct((1, 256), F32), jax.ShapeDtypeStruct((1, BW), F32)],
                  scratch_shapes=[pltpu.VMEM((BW, 256), F32)],
                  compiler_params=_cp("arbitrary"))(dyb, zb, z, zb, zb, z, states, states, wg2p, bg, ng,
                                                    tri, triu, mask)


FOX_SCALE = 64.0 ** -0.5
NEG = -1e30


def fox_fcum(z, bfp, *, name):
    t = z.shape[0]
    tm = min(256, t)
    tri = (jnp.arange(tm)[:, None] >= jnp.arange(tm)[None, :]).astype(F32)

    def body(m_ref, b_ref, tri_ref, o_ref, cc):
        @pl.when(pl.program_id(0) == 0)
        def _():
            cc[...] = jnp.zeros_like(cc)

        lf = _log_sigmoid(m_ref[...] + b_ref[...])
        cs = _dot_hi(tri_ref[...], lf) + cc[...]
        o_ref[...] = cs
        cc[...] = cs[tm - 1:tm, :]

    return _pcall(body, name=name, grid=(t // tm,),
                  in_specs=[_rows(tm, 128, MISC // 128), _fix((1, 128)), _fix((tm, tm))],
                  out_specs=_rows(tm, 128), out_shape=jax.ShapeDtypeStruct((t, 128), F32),
                  scratch_shapes=[pltpu.VMEM((1, 128), F32)],
                  compiler_params=_cp("arbitrary"))(z, bfp, tri)


def fox_dcf(dfc, z, bfp, dmisc_g, *, name):
    t = z.shape[0]
    tm = min(256, t)
    nb = t // tm
    triu = (jnp.arange(tm)[:, None] <= jnp.arange(tm)[None, :]).astype(F32)

    def body(d_ref, m_ref, b_ref, g_ref, tri_ref, o_ref, dbf_ref, cc):
        @pl.when(pl.program_id(0) == 0)
        def _():
            cc[...] = jnp.zeros_like(cc)
            dbf_ref[...] = jnp.zeros_like(dbf_ref)

        rc = _dot_hi(tri_ref[...], d_ref[...]) + cc[...]
        cc[...] = rc[0:1, :]
        dcf = rc * _sigmoid(-(m_ref[...] + b_ref[...]))
        o_ref[...] = (dcf + g_ref[...]).astype(BF16)
        dbf_ref[...] += jnp.sum(dcf, axis=0, keepdims=True)

    rev = lambda col: pl.BlockSpec((tm, 128), lambda i: (nb - 1 - i, col))
    return _pcall(body, name=name, grid=(nb,),
                  in_specs=[rev(0), rev(MISC // 128), _fix((1, 128)), rev(0), _fix((tm, tm))],
                  out_specs=[rev(0), _fix((1, 128))],
                  out_shape=[jax.ShapeDtypeStruct((t, 128), BF16), jax.ShapeDtypeStruct((1, 128), F32)],
                  scratch_shapes=[pltpu.VMEM((1, 128), F32)],
                  compiler_params=_cp("arbitrary"))(dfc, z, bfp, dmisc_g, triu)


def fox_delta(dyc, ycf, *, name):
    t = dyc.shape[0]
    tm = min(256, t)
    seg = ((jnp.arange(BW)[:, None] // 64) == jnp.arange(128)[None, :]).astype(F32)

    def body(d_ref, o_ref, s_ref, out_ref):
        out_ref[...] = _dot_hi(d_ref[...] * o_ref[...], s_ref[...])

    return _pcall(body, name=name, grid=(t // tm,),
                  in_specs=[_rows(tm, BW), _rows(tm, BW), _fix((BW, 128))],
                  out_specs=_rows(tm, 128), out_shape=jax.ShapeDtypeStruct((t, 128), F32),
                  compiler_params=_cp("parallel"))(dyc, ycf, seg)


def fox_fwd_t(zb, frow, fkb, *, name):
    t = zb.shape[0]
    tq = min(512, t)
    nq = t // tq
    rep = tq // 128

    pairs = [(i, j) for i in range(nq) for j in range(i + 1)]
    qi_tab = jnp.asarray([p[0] for p in pairs], jnp.int32)
    kj_tab = jnp.asarray([p[1] for p in pairs], jnp.int32)

    def body(qi_ref, kj_ref, q_ref, k_ref, v_ref, fq_ref, fk_ref, y_ref, yf_ref, lse_ref, m_s, l_s, acc):
        step = pl.program_id(1)
        i, j = qi_ref[step], kj_ref[step]

        @pl.when(j == 0)
        def _():
            m_s[...] = jnp.full_like(m_s, NEG)
            l_s[...] = jnp.zeros_like(l_s)
            acc[...] = jnp.zeros_like(acc)

        lo = lax.broadcasted_iota(jnp.int32, (tq, 128), 1) < 64

        def work(diagonal):
            q = q_ref[...]
            k = k_ref[...]
            v = v_ref[...]
            if diagonal:
                key = lax.broadcasted_iota(jnp.int32, (tq, tq), 0)
                qry = lax.broadcasted_iota(jnp.int32, (tq, tq), 1)
                keep = key <= qry
            for hh in range(2):
                sel = lo if hh == 0 else jnp.logical_not(lo)
                qh = jnp.where(sel, q, jnp.zeros_like(q))
                s = _dot_nt(k, qh) + fq_ref[hh] - jnp.tile(fk_ref[hh], (1, rep))
                if diagonal:
                    s = jnp.where(keep, s, NEG)
                m_old = m_s[hh]
                m_new = jnp.maximum(m_old, jnp.max(s, axis=0, keepdims=True))
                p = jnp.exp(s - m_new)
                corr = jnp.exp(m_old - m_new)
                l_s[hh] = l_s[hh] * corr + jnp.sum(p, axis=0, keepdims=True)
                m_s[hh] = m_new
                pv = _dot_tn(v, p.astype(BF16))
                rows = slice(64 * hh, 64 * hh + 64)
                acc[rows, :] = acc[rows, :] * corr + pv[rows, :]

        @pl.when(j < i)
        def _():
            work(False)

        @pl.when(j == i)
        def _():
            work(True)
            first = lax.broadcasted_iota(jnp.int32, (128, tq), 0) < 64
            out = (acc[...] * jnp.where(first, 1.0 / l_s[0], 1.0 / l_s[1])).T
            y_ref[...] = out.astype(BF16)
            yf_ref[...] = out
            lse_ref[...] = m_s[...] + jnp.log(l_s[...])

    kv = lambda off: pl.BlockSpec((tq, 128), lambda h, s, qi, kj: (kj[s], off // 128 + h))
    gs = pltpu.PrefetchScalarGridSpec(
        num_scalar_prefetch=2, grid=(4, len(pairs)),
        in_specs=[pl.BlockSpec((tq, 128), lambda h, s, qi, kj: (qi[s], CQ // 128 + h)), kv(CK), kv(CV),
                  pl.BlockSpec((2, 1, tq), lambda h, s, qi, kj: (h, 0, qi[s])),
                  pl.BlockSpec((2, tq, 128), lambda h, s, qi, kj: (h, kj[s], 0))],
        out_specs=[pl.BlockSpec((tq, 128), lambda h, s, qi, kj: (qi[s], h)),
                   pl.BlockSpec((tq, 128), lambda h, s, qi, kj: (qi[s], h)),
                   pl.BlockSpec((2, 1, tq), lambda h, s, qi, kj: (h, 0, qi[s]))],
        scratch_shapes=[pltpu.VMEM((2, 1, tq), F32), pltpu.VMEM((2, 1, tq), F32), pltpu.VMEM((128, tq), F32)])
    return _pcall(body, name=name, grid_spec=gs,
                  out_shape=[jax.ShapeDtypeStruct((t, BW), BF16), jax.ShapeDtypeStruct((t, BW), F32),
                             jax.ShapeDtypeStruct((FOX_H, 1, t), F32)],
                  compiler_params=_cp("parallel", "arbitrary"))(qi_tab, kj_tab, zb, zb, zb, frow, fkb)


def fox_bwd_t(zb, dyc, frow, fkb, lse, dl, *, name):
    t = zb.shape[0]
    tq = min(512, t)
    nq = t // tq
    rep = tq // 128

    pairs = [(j, i) for j in range(nq) for i in range(j, nq)]
    kj_tab = jnp.asarray([p[0] for p in pairs], jnp.int32)
    qi_tab = jnp.asarray([p[1] for p in pairs], jnp.int32)

    def body(kj_ref, qi_ref, q_ref, k_ref, v_ref, do_ref, fq_ref, fk_ref, lse_ref, dl_ref,
             dq_ref, dk_ref, dv_ref, dfk_ref, dfq_ref, dk_s, dv_s, df_s, dq_s):
        step = pl.program_id(1)
        j, i = kj_ref[step], qi_ref[step]

        @pl.when(step == 0)
        def _():
            dq_s[...] = jnp.zeros_like(dq_s)
            dfq_ref[...] = jnp.zeros_like(dfq_ref)

        @pl.when(i == j)
        def _():
            dk_s[...] = jnp.zeros_like(dk_s)
            dv_s[...] = jnp.zeros_like(dv_s)
            df_s[...] = jnp.zeros_like(df_s)

        lo = lax.broadcasted_iota(jnp.int32, (tq, 128), 1) < 64

        def work(diagonal):
            q = q_ref[...]
            k = k_ref[...]
            v = v_ref[...]
            dob = do_ref[...].astype(BF16)
            if diagonal:
                key = lax.broadcasted_iota(jnp.int32, (tq, tq), 0)
                qry = lax.broadcasted_iota(jnp.int32, (tq, tq), 1)
                keep = key <= qry
            dvs, dks = [], []
            for hh in range(2):
                sel = lo if hh == 0 else jnp.logical_not(lo)
                qh = jnp.where(sel, q, jnp.zeros_like(q))
                doh = jnp.where(sel, dob, jnp.zeros_like(dob))
                p = jnp.exp(_dot_nt(k, qh) + (fq_ref[hh] - lse_ref[hh]) - jnp.tile(fk_ref[hh], (1, rep)))
                if diagonal:
                    p = jnp.where(keep, p, 0.0)
                ds = p * (_dot_nt(v, doh) - dl_ref[hh])
                dsb = ds.astype(BF16)
                dvs.append(_dot(p.astype(BF16), dob))
                dks.append(_dot(dsb, q))
                rows = slice(64 * hh, 64 * hh + 64)
                dq_s[i, rows, :] += _dot_tn(k, dsb)[rows, :]
                part = ds[:, 0:128]
                for r in range(1, rep):
                    part = part + ds[:, 128 * r:128 * (r + 1)]
                df_s[hh] += part
                dfq_ref[hh, i] += jnp.sum(ds, axis=0, keepdims=True)
            dv_s[...] += jnp.where(lo, dvs[0], dvs[1])
            dk_s[...] += jnp.where(lo, dks[0], dks[1])

        @pl.when(i > j)
        def _():
            work(False)

        @pl.when(i == j)
        def _():
            work(True)
            dq_ref[...] = dq_s[i].T.astype(BF16)

        @pl.when(i == nq - 1)
        def _():
            dk_ref[...] = dk_s[...].astype(BF16)
            dv_ref[...] = dv_s[...].astype(BF16)
            for hh in range(2):
                dfk_ref[hh] = -jnp.sum(df_s[hh].T, axis=0, keepdims=True)

    row = lambda: pl.BlockSpec((2, 1, tq), lambda h, s, kj, qi: (h, 0, qi[s]))
    gs = pltpu.PrefetchScalarGridSpec(
        num_scalar_prefetch=2, grid=(4, len(pairs)),
        in_specs=[pl.BlockSpec((tq, 128), lambda h, s, kj, qi: (qi[s], CQ // 128 + h)),
                  pl.BlockSpec((tq, 128), lambda h, s, kj, qi: (kj[s], CK // 128 + h)),
                  pl.BlockSpec((tq, 128), lambda h, s, kj, qi: (kj[s], CV // 128 + h)),
                  pl.BlockSpec((tq, 128), lambda h, s, kj, qi: (qi[s], h)),
                  row(), pl.BlockSpec((2, tq, 128), lambda h, s, kj, qi: (h, kj[s], 0)), row(), row()],
        out_specs=[pl.BlockSpec((tq, 128), lambda h, s, kj, qi: (kj[s], h)),
                   pl.BlockSpec((tq, 128), lambda h, s, kj, qi: (kj[s], h)),
                   pl.BlockSpec((tq, 128), lambda h, s, kj, qi: (kj[s], h)),
                   pl.BlockSpec((2, 1, tq), lambda h, s, kj, qi: (h, 0, kj[s])),
                   pl.BlockSpec((2, nq, 1, tq), lambda h, s, kj, qi: (h, 0, 0, 0))],
        scratch_shapes=[pltpu.VMEM((tq, 128), F32), pltpu.VMEM((tq, 128), F32), pltpu.VMEM((2, tq, 128), F32),
                        pltpu.VMEM((nq, 128, tq), F32)])
    return _pcall(body, name=name, grid_spec=gs,
                  out_shape=[jax.ShapeDtypeStruct((t, BW), BF16), jax.ShapeDtypeStruct((t, BW), BF16),
                             jax.ShapeDtypeStruct((t, BW), BF16), jax.ShapeDtypeStruct((FOX_H, 1, t), F32),
                             jax.ShapeDtypeStruct((FOX_H, nq, 1, tq), F32)],
                  compiler_params=_cp("parallel", "arbitrary"))(kj_tab, qi_tab, zb, zb, zb, dyc, frow, fkb, lse, dl)


def merge_fwd(ya, yb, yc, wbr, z, *, name):
    t = ya.shape[0]
    tm = min(512, t)

    def body(ya_ref, yb_ref, yc_ref, w_ref, g0_ref, g1_ref, g2_ref, o_ref):
        m = _sigmoid(g0_ref[...]) * _dot(ya_ref[...], w_ref[0])
        m = m + _sigmoid(g1_ref[...]) * _dot(yb_ref[...], w_ref[1])
        m = m + _sigmoid(g2_ref[...]) * _dot(yc_ref[...], w_ref[2])
        o_ref[...] = m.astype(BF16)

    return _pcall(body, name=name, grid=(t // tm,),
                  in_specs=[_rows(tm, BW)] * 3 + [_fix((3, BW, D))]
                  + [_rows(tm, D, G0 // D + j) for j in range(3)],
                  out_specs=_rows(tm, D), out_shape=jax.ShapeDtypeStruct((t, D), BF16),
                  compiler_params=_cp("parallel"))(ya, yb, yc, wbr, z, z, z)


def merge_bwd(doutb, wo, l, ya, yb, yc, wbr, z, *, name):
    t = ya.shape[0]
    tm = min(256, t)

    def body(do_ref, wo_ref, ya_ref, yb_ref, yc_ref, w_ref, g0_ref, g1_ref, g2_ref,
             dya_ref, dyb_ref, dyc_ref, dp0_ref, dp1_ref, dp2_ref, dg0_ref, dg1_ref, dg2_ref):
        dm = _dot_nt(do_ref[...], wo_ref[...])
        ys = (ya_ref, yb_ref, yc_ref)
        gs = (g0_ref, g1_ref, g2_ref)
        dys = (dya_ref, dyb_ref, dyc_ref)
        dps = (dp0_ref, dp1_ref, dp2_ref)
        dgs = (dg0_ref, dg1_ref, dg2_ref)
        for j in range(3):
            s = _sigmoid(gs[j][...])
            pj = _dot(ys[j][...], w_ref[j])
            dpb = (dm * s).astype(BF16)
            dps[j][...] = dpb
            dgs[j][...] = (dm * pj * s * (1.0 - s)).astype(BF16)
            dys[j][...] = _dot_nt(dpb, w_ref[j])

    yshape = jax.ShapeDtypeStruct((t, BW), F32)
    dshape = jax.ShapeDtypeStruct((t, D), BF16)
    return _pcall(body, name=name, grid=(t // tm,),
                  in_specs=[_rows(tm, D), _layer(l, (D, D))] + [_rows(tm, BW)] * 3
                  + [_fix((3, BW, D))] + [_rows(tm, D, G0 // D + j) for j in range(3)],
                  out_specs=[_rows(tm, BW)] * 3 + [_rows(tm, D)] * 6,
                  out_shape=[yshape] * 3 + [dshape] * 6,
                  compiler_params=_cp("parallel"))(doutb, wo, ya, yb, yc, wbr, z, z, z)


def adamw(w, g, m, v, *, name):
    nl, r, c = w.shape
    tm = _row_tile(r)

    def body(w_ref, g_ref, m_ref, v_ref, d_ref, mo_ref, vo_ref):
        gg = g_ref[...]
        mn = ADAM_B1 * m_ref[...] + (1.0 - ADAM_B1) * gg
        vn = ADAM_B2 * v_ref[...] + (1.0 - ADAM_B2) * (gg * gg)
        m_hat = mn / (1.0 - ADAM_B1 ** ADAM_STEP)
        v_hat = vn / (1.0 - ADAM_B2 ** ADAM_STEP)
        d_ref[...] = -ADAM_LR * (m_hat / (jnp.sqrt(v_hat) + ADAM_EPS) + ADAM_WD * w_ref[...])
        mo_ref[...] = mn
        vo_ref[...] = vn

    shp = jax.ShapeDtypeStruct((nl, r, c), F32)
    blk = pl.BlockSpec((None, tm, c), lambda l, i: (l, i, 0))
    return _pcall(body, name=name, grid=(nl, r // tm), in_specs=[blk] * 4, out_specs=[blk] * 3,
                  out_shape=[shp] * 3, compiler_params=_cp("parallel", "parallel"))(w, g, m, v)


def _place():
    return lax.axis_index("x"), lax.axis_index("y"), lax.axis_index("c")


def _remote(src, dst, send_sems, recv_sems, k, to):
    return pltpu.make_async_remote_copy(src_ref=src, dst_ref=dst, send_sem=send_sems.at[k],
                                        recv_sem=recv_sems.at[k], device_id=to, device_id_type=MESH)


def gather_weights(shards):
    n = len(shards)

    def body(*refs):
        ins, outs = refs[:n], refs[n:2 * n]
        send_sems, recv_sems, own_send, own_recv = refs[2 * n:]
        x, y, c = _place()
        sib = (x, y, 1 - c)
        chips = [(1 - x, y), (x, 1 - y), (1 - x, 1 - y)]
        k_me = 2 * x + y
        mine, first, passed = [], [], []
        for t in range(n):
            for l in range(DEPTH):
                mine.append(_remote(ins[t].at[l], outs[t].at[l, k_me], own_send, own_recv, 2 * t + l, sib))
            for j, chip in enumerate(chips):
                first.append(_remote(ins[t].at[c], outs[t].at[c, k_me], send_sems, recv_sems, 6 * t + j, (*chip, c)))
        for cp in mine + first:
            cp.start()
        for t in range(n):
            for j, chip in enumerate(chips):
                blk = outs[t].at[c, 2 * chip[0] + chip[1]]
                _remote(blk, blk, send_sems, recv_sems, 6 * t + j, (*chip, c)).wait_recv()
                cp = _remote(blk, blk, send_sems, recv_sems, 6 * t + 3 + j, sib)
                cp.start()
                passed.append(cp)
        for t in range(n):
            for j, chip in enumerate(chips):
                blk = outs[t].at[1 - c, 2 * chip[0] + chip[1]]
                _remote(blk, blk, send_sems, recv_sems, 6 * t + 3 + j, sib).wait_recv()
        for cp in first + passed:
            cp.wait_send()
        for cp in mine:
            cp.wait()

    return _pcall(body, name="gather_weights", in_specs=[ANY] * n, out_specs=[ANY] * n,
                  out_shape=[jax.ShapeDtypeStruct((DEPTH, 4) + s.shape[1:], s.dtype) for s in shards],
                  scratch_shapes=[pltpu.SemaphoreType.DMA((6 * n,)), pltpu.SemaphoreType.DMA((6 * n,)),
                                  pltpu.SemaphoreType.DMA((2 * n,)), pltpu.SemaphoreType.DMA((2 * n,))])(*shards)


def pair_exchange(g0, g1):
    n = len(g0)

    def body(*refs):
        a0, a1, outs = refs[:n], refs[n:2 * n], refs[2 * n:3 * n]
        send_sems, recv_sems = refs[3 * n:]
        x, y, c = _place()
        sib = (x, y, 1 - c)

        @pl.when(c == 0)
        def _():
            for t in range(n):
                _remote(a1[t], outs[t], send_sems, recv_sems, t, sib).start()

        @pl.when(c == 1)
        def _():
            for t in range(n):
                _remote(a0[t], outs[t], send_sems, recv_sems, t, sib).start()

        for t in range(n):
            _remote(a0[t], outs[t], send_sems, recv_sems, t, sib).wait()

    return _pcall(body, name="pair_exchange", in_specs=[ANY] * (2 * n), out_specs=[ANY] * n,
                  out_shape=[jax.ShapeDtypeStruct(a.shape, a.dtype) for a in g0],
                  scratch_shapes=[pltpu.SemaphoreType.DMA((n,)), pltpu.SemaphoreType.DMA((n,))])(*g0, *g1)


def chip_exchange(s1):
    n = len(s1)

    def body(*refs):
        ins, outs = refs[:n], refs[n:2 * n]
        send_sems, recv_sems = refs[2 * n:]
        x, y, c = _place()
        chips = [(1 - x, y), (x, 1 - y), (1 - x, 1 - y)]
        cps = [_remote(ins[t].at[2 * chip[0] + chip[1]], outs[t].at[j], send_sems, recv_sems, 3 * t + j, (*chip, c))
               for t in range(n) for j, chip in enumerate(chips)]
        for cp in cps:
            cp.start()
        for cp in cps:
            cp.wait()

    return _pcall(body, name="chip_exchange", in_specs=[ANY] * n, out_specs=[ANY] * n,
                  out_shape=[jax.ShapeDtypeStruct((3,) + a.shape[1:], a.dtype) for a in s1],
                  scratch_shapes=[pltpu.SemaphoreType.DMA((3 * n,)), pltpu.SemaphoreType.DMA((3 * n,))])(*s1)


def pair_share(s2):
    n = len(s2)

    def body(*refs):
        ins, outs = refs[:n], refs[n:2 * n]
        send_sems, recv_sems = refs[2 * n:]
        x, y, c = _place()
        sib = (x, y, 1 - c)
        cps = [_remote(ins[t].at[c], outs[t].at[c], send_sems, recv_sems, t, sib) for t in range(n)]
        for cp in cps:
            cp.start()
        for t in range(n):
            cps[t].wait_send()
            _remote(ins[t].at[c], outs[t].at[1 - c], send_sems, recv_sems, t, sib).wait_recv()

    return _pcall(body, name="pair_share", in_specs=[ANY] * n, out_specs=[ANY] * n,
                  out_shape=[jax.ShapeDtypeStruct(a.shape, a.dtype) for a in s2],
                  input_output_aliases={t: t for t in range(n)},
                  scratch_shapes=[pltpu.SemaphoreType.DMA((n,)), pltpu.SemaphoreType.DMA((n,))])(*s2)


def small_exchange(gs):
    rows, width = gs.shape

    def body(g_ref, o_ref, send_sems, recv_sems):
        x, y, c = _place()
        cps = []
        for r in range(1, 8):
            dx, dy, dc = (r >> 2) & 1, (r >> 1) & 1, r & 1
            to = (x if dx == 0 else 1 - x, y if dy == 0 else 1 - y, c if dc == 0 else 1 - c)
            cps.append(_remote(g_ref, o_ref.at[r - 1], send_sems, recv_sems, r - 1, to))
        for cp in cps:
            cp.start()
        for cp in cps:
            cp.wait()

    return _pcall(body, name="small_exchange", in_specs=[ANY], out_specs=ANY,
                  out_shape=jax.ShapeDtypeStruct((7, rows, width), gs.dtype),
                  scratch_shapes=[pltpu.SemaphoreType.DMA((7,)), pltpu.SemaphoreType.DMA((7,))])(gs)


def _row_tile(rows):
    return _pick(rows, (256, 352, 128, 64, 32, 16))


def pair_add(g0, g1, rb, core, *, name):
    _, rows, width = g0.shape
    tr = _row_tile(rows)

    def body(c_ref, g0_ref, g1_ref, r_ref, o_ref, ob_ref):
        s = jnp.where(c_ref[0] == 0, g0_ref[...], g1_ref[...]) + r_ref[...]
        o_ref[...] = s
        ob_ref[...] = s.astype(BF16)

    blk = pl.BlockSpec((None, tr, width), lambda k, i, c_ref: (k, i, 0))
    gs = pltpu.PrefetchScalarGridSpec(
        num_scalar_prefetch=1, grid=(4, rows // tr),
        in_specs=[pl.BlockSpec((None, tr, width), lambda k, i, c_ref: (k * (1 - c_ref[0]), i * (1 - c_ref[0]), 0)),
                  pl.BlockSpec((None, tr, width), lambda k, i, c_ref: (k * c_ref[0], i * c_ref[0], 0)), blk],
        out_specs=[blk, blk])
    return _pcall(body, name=name, grid_spec=gs,
                  out_shape=[jax.ShapeDtypeStruct(g0.shape, F32), jax.ShapeDtypeStruct(g0.shape, BF16)],
                  compiler_params=_cp("parallel", "parallel"))(core, g0, g1, rb)


def chip_add(s1, rb2, chip, core, *, name):
    _, rows, width = s1.shape
    tr = _row_tile(rows)

    def body(k_ref, c_ref, s_ref, r_ref, o_ref):
        o_ref[...] = ((s_ref[...] + r_ref[0].astype(F32)) + r_ref[1].astype(F32)) + r_ref[2].astype(F32)

    gs = pltpu.PrefetchScalarGridSpec(
        num_scalar_prefetch=2, grid=(rows // tr,),
        in_specs=[pl.BlockSpec((None, tr, width), lambda i, k_ref, c_ref: (k_ref[0], i, 0)),
                  pl.BlockSpec((3, tr, width), lambda i, k_ref, c_ref: (0, i, 0))],
        out_specs=pl.BlockSpec((None, tr, width), lambda i, k_ref, c_ref: (c_ref[0], i, 0)))
    return _pcall(body, name=name, grid_spec=gs, out_shape=jax.ShapeDtypeStruct((DEPTH, rows, width), F32),
                  compiler_params=_cp("parallel"))(chip, core, s1, rb2)


def small_add(gs_own, slots, me):
    rows, width = gs_own.shape
    tr = _pick(rows, (64, 32, 16, 8))

    def body(me_ref, g_ref, s_ref, o_ref):
        me_v = me_ref[0]
        total = None
        for d in range(8):
            rel = jnp.bitwise_xor(me_v, d)
            val = jnp.where(rel == 0, g_ref[...], s_ref[jnp.maximum(rel - 1, 0)])
            total = val if total is None else total + val
        o_ref[...] = total

    gs = pltpu.PrefetchScalarGridSpec(
        num_scalar_prefetch=1, grid=(rows // tr,),
        in_specs=[pl.BlockSpec((tr, width), lambda i, m_ref: (i, 0)),
                  pl.BlockSpec((7, tr, width), lambda i, m_ref: (0, i, 0))],
        out_specs=pl.BlockSpec((tr, width), lambda i, m_ref: (i, 0)))
    return _pcall(body, name="small_add", grid_spec=gs, out_shape=jax.ShapeDtypeStruct((rows, width), F32),
                  compiler_params=_cp("parallel"))(me, gs_own, slots)


SHARDED = (("ffn1_w_up", (D, UPW)), ("ffn1_w_down", (DFF // 4, D)), ("w_in", (D, D_IN // 4)),
           ("conv_w", (4, BW // 4)), ("gla_w_g2", (LOW_W, 64)), ("w_branch", (3 * BW, D // 4)),
           ("w_out", (D // 4, D)), ("ffn2_w_up", (D, UPW)), ("ffn2_w_down", (DFF // 4, D)),
           ("ple_w_proj", (PLE, D // 4)), ("ple_w_gate", (D // 4, D)))
SMALL = ("ln1_g", "ln1_b", "conv_b", "lru_wa", "lru_ba", "lru_wx", "lru_bx", "lru_lambda", "gla_b_g",
         "gla_norm_g", "fox_b_f", "ln2_g", "ln2_b", "ln3_g", "ln3_b", "ple_b_gate", "ln4_g", "ln4_b")
WEIGHTS = ('ffn1_w_up', 'ffn1_w_down', 'ln1_g', 'ln1_b', 'w_in', 'conv_w', 'conv_b', 'lru_wa', 'lru_ba',
           'lru_wx', 'lru_bx', 'lru_lambda', 'gla_w_g2', 'gla_b_g', 'gla_norm_g', 'fox_b_f', 'w_branch',
           'w_out', 'ln2_g', 'ln2_b', 'ffn2_w_up', 'ffn2_w_down', 'ln3_g', 'ln3_b', 'ple_w_proj',
           'ple_w_gate', 'ple_b_gate', 'ln4_g', 'ln4_b')


def _cols_join(parts):
    return jnp.concatenate([parts[k] for k in range(4)], axis=-1)


def _cols_split(full):
    r, c4 = full.shape
    return full.reshape(r, 4, c4 // 4).transpose(1, 0, 2)


def _regroup_in(w):
    pad = jnp.zeros(w.shape[:-1] + (ZW - D_IN,), w.dtype)
    fox_q = (w[..., 2576:3088] * FOX_SCALE).astype(w.dtype)
    return jnp.concatenate([w[..., 0:2048], w[..., 2064:2576], fox_q, w[..., 3088:4112], w[..., 4120:7192],
                            w[..., 2048:2064], w[..., 4112:4120], pad], axis=-1)


_IN_RUNS = ((0, 2048, 0, 1.0), (2048, 2064, 7168, 1.0), (2064, 2576, 2048, 1.0), (2576, 3088, CQ, FOX_SCALE),
            (3088, 4112, CK, 1.0), (4112, 4120, 7184, 1.0), (4120, D_IN, 4096, 1.0))


def _regroup_out_shards(g):
    w = D_IN // 4
    shards = []
    for k in range(4):
        pieces = []
        for a, b, new, f in _IN_RUNS:
            lo, hi = max(a, k * w), min(b, (k + 1) * w)
            if lo < hi:
                piece = g[:, new + lo - a:new + hi - a]
                pieces.append(piece if f == 1.0 else piece * f)
        shards.append(jnp.concatenate(pieces, axis=1))
    return jnp.stack(shards)


def _block_diag(w):
    eye = jnp.eye(8, dtype=w.dtype)
    return (eye[:, None, :, None] * w[:, :, None, :]).reshape(BW, BW)


def _diag_blocks(dense):
    return jnp.stack([dense[64 * n:64 * (n + 1), 64 * n:64 * (n + 1)] for n in range(8)])


def _layer_weights(gw, small, l):
    w = {"up1": gw["ffn1_w_up"], "up2": gw["ffn2_w_up"],
         "dn1": gw["ffn1_w_down"].reshape(DEPTH, DFF, D), "dn2": gw["ffn2_w_down"].reshape(DEPTH, DFF, D),
         "wo": gw["w_out"].reshape(DEPTH, D, D), "wgt": gw["ple_w_gate"].reshape(DEPTH, D, D)}
    w["win"] = _regroup_in(_cols_join(gw["w_in"][l]))
    w["cw"] = _cols_join(gw["conv_w"][l])
    w["wa"] = _block_diag(small["lru_wa"][l]).astype(BF16)
    w["wx"] = _block_diag(small["lru_wx"][l]).astype(BF16)
    w["wg2p"] = jnp.pad(_cols_join(gw["gla_w_g2"][l]), ((0, 128 - LOW_W), (0, 0)))
    w["wbr"] = _cols_join(gw["w_branch"][l].reshape(4, 3, BW, D // 4))
    w["wp"] = _cols_join(gw["ple_w_proj"][l])
    for n in ("ln1_g", "ln1_b", "ln2_g", "ln2_b", "ln3_g", "ln3_b", "ln4_g", "ln4_b", "conv_b", "lru_ba",
              "lru_bx", "lru_lambda", "gla_b_g", "gla_norm_g", "ple_b_gate"):
        w[n] = small[n][l][None, :]
    w["bfp"] = jnp.pad(small["fox_b_f"][l], (LOW_W, 128 - LOW_W - FOX_H))[None, :]
    return w


def _heads_t(a):
    ht = a[:, LOW_W:LOW_W + FOX_H].T
    return ht[:, None, :], jnp.broadcast_to(ht[:, :, None], ht.shape + (128,))


def _layer_fwd(x, xb, pb, w, l):
    s = {"x0": x, "x0b": xb}
    tag = "l%d_" % l
    gate, up, act = ffn_up(xb, w["up1"], l, name=tag + "ffn1_up")
    r1, x1, x1b = matmul_res_ln(act, w["dn1"], l, x, w["ln1_g"], w["ln1_b"], mm_scale=0.5, name=tag + "ffn1_down")
    s.update(gate1=gate, up1=up, act1=act, r1=r1, x1=x1, x1b=x1b)
    z, zb = matmul(x1b, w["win"], also_bf16=True, tm=1024, tn=_pick(ZW, (2432,)), name=tag + "mix_in")
    xc, xcb, h, ya = lru_fwd(z, w["cw"], w["conv_b"], w["wa"], w["wx"], w["lru_ba"], w["lru_bx"],
                             w["lru_lambda"], name=tag + "lru_fwd")
    yb, states = gla_fwd(z, zb, w["wg2p"], w["gla_b_g"], w["gla_norm_g"], name=tag + "gla_fwd")
    fcum = fox_fcum(z, w["bfp"], name=tag + "fox_fcum")
    fq, fk = _heads_t(fcum)
    yc, ycf, lse = fox_fwd_t(zb, fq, fk, name=tag + "fox_fwd")
    merged = merge_fwd(ya, yb, yc, w["wbr"], z, name=tag + "merge_fwd")
    r2, x2, x2b = matmul_res_ln(merged, w["wo"], l, x1, w["ln2_g"], w["ln2_b"], mm_scale=1.0, name=tag + "mix_out")
    s.update(z=z, zb=zb, xc=xc, xcb=xcb, h=h, ya=ya, yb=yb, states=states, fq=fq, fk=fk, yc=yc, ycf=ycf,
             lse=lse, merged=merged, r2=r2, x2=x2, x2b=x2b)
    gate, up, act = ffn_up(x2b, w["up2"], l, name=tag + "ffn2_up")
    r3, x3, x3b = matmul_res_ln(act, w["dn2"], l, x2, w["ln3_g"], w["ln3_b"], mm_scale=0.5, name=tag + "ffn2_down")
    s.update(gate2=gate, up2=up, act2=act, r3=r3, x3=x3, x3b=x3b)
    r4, x4, x4b = ple_fwd(x3b, x3, pb, w["wgt"], l, w["wp"], w["ple_b_gate"], w["ln4_g"], w["ln4_b"],
                          name=tag + "ple_fwd")
    s.update(r4=r4, pb=pb)
    return x4, x4b, s


def _ffn_bwd(dy, s, w, n, xin_b, l, tag):
    k = {"1": ("r1", "ln1_g", "gate1", "up1", "act1"), "2": ("r3", "ln3_g", "gate2", "up2", "act2")}[n]
    dr, dfb, dg, db = ln_bwd(dy, s[k[0]], w[k[1]], out_scale=0.5, name=tag + "ln_bwd")
    dgate, dup = ffn_down_bwd(dfb, w["dn" + n], l, s[k[2]], s[k[3]], name=tag + "down_bwd")
    dx = ffn_dx(dgate, dup, w["up" + n], l, dr, name=tag + "dx")
    dwup = matmul_tn_up(xin_b, dgate, dup, name=tag + "dw_up")
    dwdn = matmul_tn(s[k[4]], dfb, name=tag + "dw_down").reshape(4, DFF // 4, D)
    return dx, dwup, dwdn, dg[0], db[0]


def _layer_bwd(dy, s, w, l):
    g = {}
    tag = "l%d_" % l
    dr4, dglb, dpeb, dg4, db4, dbg = ple_bwd(dy, s["r4"], s["x3b"], s["pb"], w["wgt"], l, w["wp"], w["ple_b_gate"],
                                             w["ln4_g"], name=tag + "ple_bwd")
    dx3 = matmul(dglb, w["wgt"], nt=True, b_lead=(l,), res=dr4, res_scale=ALPHA, tm=1024, tn=1024,
                 name=tag + "ple_dx")
    g["ple_w_gate"] = matmul_tn(s["x3b"], dglb, name=tag + "ple_dw_gate").reshape(4, D // 4, D)
    g["ple_w_proj"] = _cols_split(matmul_tn(s["pb"], dpeb, name=tag + "ple_dw_proj"))
    g["ln4_g"], g["ln4_b"], g["ple_b_gate"] = dg4[0], db4[0], dbg[0]
    dx2, g["ffn2_w_up"], g["ffn2_w_down"], g["ln3_g"], g["ln3_b"] = _ffn_bwd(dx3, s, w, "2", s["x2b"], l,
                                                                             tag + "ffn2_")
    dr2, doutb, dg2, db2 = ln_bwd(dx2, s["r2"], w["ln2_g"], out_scale=1.0, name=tag + "mix_ln_bwd")
    g["ln2_g"], g["ln2_b"] = dg2[0], db2[0]
    g["w_out"] = matmul_tn(s["merged"], doutb, name=tag + "dw_out").reshape(4, D // 4, D)
    z, zb = s["z"], s["zb"]
    (dya, dyb, dyc, dp0, dp1, dp2, dgl0, dgl1, dgl2) = merge_bwd(
        doutb, w["wo"], l, s["ya"], s["yb"], s["yc"], w["wbr"], z, name=tag + "merge_bwd")
    dwbr = jnp.stack([matmul_tn(s["ya"], dp0, name=tag + "dw_br0"), matmul_tn(s["yb"], dp1, name=tag + "dw_br1"),
                      matmul_tn(s["yc"], dp2, name=tag + "dw_br2")])
    g["w_branch"] = _cols_split(dwbr.reshape(3 * BW, D))
    day, dxc, dprb, dpib, dba, dbx, dlam = lru_bwd(dya, z, s["h"], s["xc"], w["wa"], w["wx"],
                                                   w["lru_ba"], w["lru_bx"], w["lru_lambda"], name=tag + "lru_bwd")
    dax, dcw, dcb = conv_bwd(dxc, z, w["cw"], name=tag + "conv_bwd")
    g["lru_wa"] = _diag_blocks(matmul_tn(s["xcb"], dprb, name=tag + "dw_lru_a"))
    g["lru_wx"] = _diag_blocks(matmul_tn(s["xcb"], dpib, name=tag + "dw_lru_x"))
    g["lru_ba"], g["lru_bx"], g["lru_lambda"] = dba[0], dbx[0], dlam[0]
    g["conv_w"], g["conv_b"] = _cols_split(dcw), dcb[0]
    dbq, dbk, dbv, dbr, dmisc_g, dpreb, dbgg, dng = gla_bwd(dyb, z, zb, s["states"], w["wg2p"], w["gla_b_g"],
                                                            w["gla_norm_g"], name=tag + "gla_bwd")
    miscb = zb[:, MISC:]
    g["gla_w_g2"] = _cols_split(matmul_tn(miscb, dpreb, name=tag + "dw_g2")[:LOW_W])
    g["gla_b_g"], g["gla_norm_g"] = dbgg[0], dng[0]
    dl = fox_delta(dyc, s["ycf"], name=tag + "fox_delta")
    t = z.shape[0]
    dlq = dl[:, :FOX_H].T[:, None, :]
    dcq, dck, dcv, dfk, dfq = fox_bwd_t(zb, dyc, s["fq"], s["fk"], s["lse"], dlq, name=tag + "fox_bwd")
    dfc = jnp.pad((dfk[:, 0, :] + dfq.reshape(FOX_H, t)).T, ((0, 0), (LOW_W, 128 - LOW_W - FOX_H)))
    dmiscb, dbf = fox_dcf(dfc, z, w["bfp"], dmisc_g, name=tag + "fox_dcf")
    g["fox_b_f"] = dbf[0, LOW_W:LOW_W + FOX_H]
    dz = jnp.concatenate([dax, day, dbq, dbk, dbv, dbr, dcq, dck, dcv, dgl0, dgl1, dgl2, dmiscb], axis=1)
    dx1 = matmul(dz, w["win"], nt=True, res=dr2, res_scale=ALPHA, tm=1024, tn=1024, tk=_pick(ZW, (2432,)),
                 name=tag + "mix_dx")
    g["w_in"] = _regroup_out_shards(matmul_tn(s["x1b"], dz, name=tag + "dw_in"))
    dx0, g["ffn1_w_up"], g["ffn1_w_down"], g["ln1_g"], g["ln1_b"] = _ffn_bwd(dx1, s, w, "1", s["x0b"], l,
                                                                             tag + "ffn1_")
    return dx0, g


def _local_step(x, p, target, gw, small):
    xcur = x
    xb = xcur.astype(BF16)
    layer_w, saved = [], []
    for l in range(DEPTH):
        w = _layer_weights(gw, small, l)
        xcur, xb, s = _layer_fwd(xcur, xb, p[l].astype(BF16), w, l)
        layer_w.append(w)
        saved.append(s)
    dy, sq = loss_head(xcur, target, name="loss_head")
    grads = [None] * DEPTH
    for l in reversed(range(DEPTH)):
        dy, grads[l] = _layer_bwd(dy, saved[l], layer_w[l], l)
    return 0.5 * jnp.sum(sq) / float(D), dy, grads


def kernel(x, p, ffn1_w_up, ffn1_w_down, ln1_g, ln1_b, w_in, conv_w, conv_b, lru_wa, lru_ba, lru_wx, lru_bx, lru_lambda, gla_w_g2, gla_b_g, gla_norm_g, fox_b_f, w_branch, w_out, ln2_g, ln2_b, ffn2_w_up, ffn2_w_down, ln3_g, ln3_b, ple_w_proj, ple_w_gate, ple_b_gate, ln4_g, ln4_b, loss_target, m_ffn1_w_up, m_ffn1_w_down, m_ln1_g, m_ln1_b, m_w_in, m_conv_w, m_conv_b, m_lru_wa, m_lru_ba, m_lru_wx, m_lru_bx, m_lru_lambda, m_gla_w_g2, m_gla_b_g, m_gla_norm_g, m_fox_b_f, m_w_branch, m_w_out, m_ln2_g, m_ln2_b, m_ffn2_w_up, m_ffn2_w_down, m_ln3_g, m_ln3_b, m_ple_w_proj, m_ple_w_gate, m_ple_b_gate, m_ln4_g, m_ln4_b, v_ffn1_w_up, v_ffn1_w_down, v_ln1_g, v_ln1_b, v_w_in, v_conv_w, v_conv_b, v_lru_wa, v_lru_ba, v_lru_wx, v_lru_bx, v_lru_lambda, v_gla_w_g2, v_gla_b_g, v_gla_norm_g, v_fox_b_f, v_w_branch, v_w_out, v_ln2_g, v_ln2_b, v_ffn2_w_up, v_ffn2_w_down, v_ln3_g, v_ln3_b, v_ple_w_proj, v_ple_w_gate, v_ple_b_gate, v_ln4_g, v_ln4_b):
    args = dict(locals())
    wts = {n: args[n] for n in WEIGHTS}
    mom = {n: args["m_" + n] for n in WEIGHTS}
    var = {n: args["v_" + n] for n in WEIGHTS}
    cx, cy, cc = lax.axis_index("x"), lax.axis_index("y"), lax.axis_index("c")

    shards = [wts[n].reshape((DEPTH,) + rc).astype(F32 if n == "conv_w" else BF16) for n, rc in SHARDED]
    gw = dict(zip([n for n, _ in SHARDED], gather_weights(shards)))
    small = {n: wts[n] for n in SMALL}

    loss_local, dx, grads = _local_step(x[0], p[:, 0], loss_target[0], gw, small)
    loss = lax.psum(loss_local, ("x", "y", "c"))
    grad_x = dx[None]

    core = jnp.reshape(cc, (1,)).astype(jnp.int32)
    chip = jnp.reshape(2 * cx + cy, (1,)).astype(jnp.int32)
    g0 = [grads[0][n] for n, _ in SHARDED]
    g1 = [grads[1][n] for n, _ in SHARDED]
    rb = pair_exchange(g0, g1)
    s1 = [pair_add(a0, a1, r, core, name="pair_add_" + n) for (n, _), a0, a1, r in zip(SHARDED, g0, g1, rb)]
    rb2 = chip_exchange([sb for _, sb in s1])
    s2 = [chip_add(sf, r, chip, core, name="chip_add_" + n) for (n, _), (sf, _), r in zip(SHARDED, s1, rb2)]
    gsh = dict(zip([n for n, _ in SHARDED], pair_share(s2)))

    pieces, spans, row = [], {}, 0
    for n in SMALL:
        flat = jnp.stack([grads[l][n] for l in range(DEPTH)]).reshape(-1)
        rows = -(-flat.shape[0] // (8 * PACK_W)) * 8
        pieces.append(jnp.pad(flat, (0, rows * PACK_W - flat.shape[0])).reshape(rows, PACK_W))
        spans[n] = (row, rows)
        row += rows
    gs = jnp.concatenate(pieces, axis=0)
    me = jnp.reshape(4 * cx + 2 * cy + cc, (1,)).astype(jnp.int32)
    gsum = small_add(gs, small_exchange(gs), me)

    gout, delta, new_m, new_v = {}, {}, {}, {}
    for n in WEIGHTS:
        shp = wts[n].shape
        if n in gsh:
            view = gsh[n].shape
            g = gsh[n]
        else:
            view = (1, DEPTH, wts[n].size // DEPTH)
            r0, rows = spans[n]
            g = gsum[r0:r0 + rows].reshape(-1)[:wts[n].size].reshape(view)
        d, mn, vn = adamw(wts[n].reshape(view), g, mom[n].reshape(view), var[n].reshape(view), name="adamw_" + n)
        gout[n], delta[n], new_m[n], new_v[n] = g.reshape(shp), d.reshape(shp), mn.reshape(shp), vn.reshape(shp)

    return (loss, grad_x, *[gout[n] for n in WEIGHTS], *[delta[n] for n in WEIGHTS],
            *[new_m[n] for n in WEIGHTS], *[new_v[n] for n in WEIGHTS])
```

```python
import functools
import math

import jax
import jax.numpy as jnp
from jax import lax
from jax.experimental import pallas as pl
from jax.experimental.pallas import tpu as pltpu

F32 = jnp.float32
BF16 = jnp.bfloat16

D = 1024
DFF = 2816
BW = 512
PLE = 256
DEPTH = 2
ALPHA = (2 * DEPTH) ** 0.25
LN_EPS = 1e-5
RMS_EPS = 1e-6
LRU_C = 8.0
GLA_TAU = 16.0
CHUNK = 64
D_IN = 7192
ZW = 7296
AX, AY, BQ, BK, BV, BR, CQ, CK, CV, G0, MISC = 0, 512, 1024, 1280, 1536, 2048, 2560, 3072, 3584, 4096, 7168
LOW_W, FOX_H = 16, 8
ADAM_LR, ADAM_B1, ADAM_B2, ADAM_EPS, ADAM_WD, ADAM_STEP = 0.001, 0.9, 0.999, 1e-08, 0.01, 10
PACK_W = 1024
VMEM_LIMIT = 56 << 20

MESH = pl.DeviceIdType.MESH
ANY = pl.BlockSpec(memory_space=pl.ANY)


def _pcall(body, **kw):
    return pl.pallas_call(body, **kw)


def _cp(*dims):
    return pltpu.CompilerParams(dimension_semantics=dims, vmem_limit_bytes=VMEM_LIMIT)


def _dot(a, b):
    return jnp.dot(a, b, preferred_element_type=F32)


def _dot_nt(a, b):
    return lax.dot_general(a, b, (((1,), (1,)), ((), ())), preferred_element_type=F32)


def _dot_tn(a, b):
    return lax.dot_general(a, b, (((0,), (0,)), ((), ())), preferred_element_type=F32)


def _dot_hi(a, b):
    return jnp.dot(a, b, preferred_element_type=F32, precision=lax.Precision.HIGHEST)


def _sigmoid(x):
    return 1.0 / (1.0 + jnp.exp(-x))


def _softplus(x):
    return jnp.maximum(x, 0.0) + jnp.log(1.0 + jnp.exp(-jnp.abs(x)))


def _log_sigmoid(x):
    return -_softplus(-x)


def _expm1(x):
    poly = x * (1.0 + x * (0.5 + x * (1.0 / 6.0 + x * (1.0 / 24.0 + x * (1.0 / 120.0 + x * (1.0 / 720.0))))))
    return jnp.where(jnp.abs(x) < 0.1, poly, jnp.exp(x) - 1.0)


_GELU_C = math.sqrt(2.0 / math.pi)


def _gelu(x):
    return 0.5 * x * (1.0 + jnp.tanh(_GELU_C * (x + 0.044715 * x * x * x)))


def _gelu_grad(x):
    t = jnp.tanh(_GELU_C * (x + 0.044715 * x * x * x))
    return 0.5 * (1.0 + t) + 0.5 * x * (1.0 - t * t) * _GELU_C * (1.0 + 3.0 * 0.044715 * x * x)


def _ln_stats(r):
    mu = jnp.mean(r, axis=-1, keepdims=True)
    xc = r - mu
    var = jnp.mean(xc * xc, axis=-1, keepdims=True)
    return xc, lax.rsqrt(var + LN_EPS)


def _pick(n, cands):
    for c in cands:
        if n % c == 0:
            return c
    return n


def _rows(tm, w, col=0):
    return pl.BlockSpec((tm, w), lambda i: (i, col))


def _fix(shape):
    nd = len(shape)
    return pl.BlockSpec(shape, lambda i: (0,) * nd)


def _col_chunks(n, width=256):
    return [slice(c, min(c + width, n)) for c in range(0, n, width)]


def _layer(l, shape):
    nd = len(shape)
    return pl.BlockSpec((None,) + tuple(shape), lambda i: (l,) + (0,) * nd)


def matmul(a, b, *, name, nt=False, b_lead=(), res=None, res_scale=1.0, also_bf16=False, tm=512, tn=512,
           tk=None):
    m, k = a.shape
    n = b.shape[-2] if nt else b.shape[-1]
    tm, tn = min(tm, m), min(tn, n)
    tk = k if tk is None else tk
    nk = k // tk
    has_res = res is not None
    lead = tuple(b_lead)
    dot = _dot_nt if nt else _dot

    def body(*refs):
        a_ref, b_ref = refs[0], refs[1]
        pos = 2
        r_ref = None
        if has_res:
            r_ref = refs[pos]
            pos += 1
        o_ref = refs[pos]
        pos += 1
        ob_ref = None
        if also_bf16:
            ob_ref = refs[pos]
            pos += 1

        def finish(v):
            if has_res:
                v = v + res_scale * r_ref[...]
            o_ref[...] = v
            if also_bf16:
                ob_ref[...] = v.astype(BF16)

        if nk == 1:
            finish(dot(a_ref[...], b_ref[...]))
            return
        acc = refs[pos]
        kk = pl.program_id(2)

        @pl.when(kk == 0)
        def _():
            acc[...] = jnp.zeros_like(acc)

        acc[...] += dot(a_ref[...], b_ref[...])

        @pl.when(kk == nk - 1)
        def _():
            finish(acc[...])

    none = (None,) * len(lead)
    if nt:
        b_spec = pl.BlockSpec(none + (tn, tk), lambda j, i, kk: lead + (j, kk))
    else:
        b_spec = pl.BlockSpec(none + (tk, tn), lambda j, i, kk: lead + (kk, j))
    in_specs = [pl.BlockSpec((tm, tk), lambda j, i, kk: (i, kk)), b_spec]
    args = [a, b]
    if has_res:
        in_specs.append(pl.BlockSpec((tm, tn), lambda j, i, kk: (i, j)))
        args.append(res)
    out_shape = [jax.ShapeDtypeStruct((m, n), F32)]
    out_specs = [pl.BlockSpec((tm, tn), lambda j, i, kk: (i, j))]
    if also_bf16:
        out_shape.append(jax.ShapeDtypeStruct((m, n), BF16))
        out_specs.append(pl.BlockSpec((tm, tn), lambda j, i, kk: (i, j)))
    out = _pcall(body, name=name, grid=(n // tn, m // tm, nk), in_specs=in_specs, out_specs=out_specs,
                 out_shape=out_shape, scratch_shapes=[pltpu.VMEM((tm, tn), F32)] if nk > 1 else [],
                 compiler_params=_cp("parallel", "parallel", "arbitrary"))(*args)
    return out if also_bf16 else out[0]


def matmul_tn(a, b, *, name):
    t, k = a.shape
    n = b.shape[1]
    tk = _pick(k, (1024, 1408, 512, 256, 128))
    tn = _pick(n, (1024, 1408, 2432, 512, 256, 128))
    tt = min(1024 if tk * tn > (1 << 20) else 2048, t)
    nt = t // tt

    def body(a_ref, b_ref, o_ref):
        @pl.when(pl.program_id(2) == 0)
        def _():
            o_ref[...] = jnp.zeros_like(o_ref)

        o_ref[...] += _dot_tn(a_ref[...], b_ref[...])

    return _pcall(body, name=name, grid=(k // tk, n // tn, nt),
                  in_specs=[pl.BlockSpec((tt, tk), lambda i, j, s: (s, i)),
                            pl.BlockSpec((tt, tn), lambda i, j, s: (s, j))],
                  out_specs=pl.BlockSpec((tk, tn), lambda i, j, s: (i, j)),
                  out_shape=jax.ShapeDtypeStruct((k, n), F32),
                  compiler_params=_cp("parallel", "parallel", "arbitrary"))(a, b)


UPW = 1408


def matmul_tn_up(a, dgate, dup, *, name):
    t, k = a.shape
    tt = min(1024, t)
    tk = 1024

    def body(a_ref, g_ref, u_ref, o_ref):
        j = pl.program_id(1)

        @pl.when(pl.program_id(2) == 0)
        def _():
            o_ref[...] = jnp.zeros_like(o_ref)

        @pl.when(j < 2)
        def _():
            o_ref[...] += _dot_tn(a_ref[...], g_ref[...])

        @pl.when(j >= 2)
        def _():
            o_ref[...] += _dot_tn(a_ref[...], u_ref[...])

    return _pcall(body, name=name, grid=(k // tk, 4, t // tt),
                  in_specs=[pl.BlockSpec((tt, tk), lambda i, j, s: (s, i)),
                            pl.BlockSpec((tt, UPW), lambda i, j, s: (jnp.where(j < 2, s, 0), jnp.minimum(j, 1))),
                            pl.BlockSpec((tt, UPW), lambda i, j, s: (jnp.where(j >= 2, s, 0), jnp.maximum(j - 2, 0)))],
                  out_specs=pl.BlockSpec((None, tk, UPW), lambda i, j, s: (j, i, 0)),
                  out_shape=jax.ShapeDtypeStruct((4, k, UPW), F32),
                  compiler_params=_cp("parallel", "parallel", "arbitrary"))(a, dgate, dup)


def ffn_dx(dgate, dup, wup, l, res, *, name):
    t = dgate.shape[0]
    tm, tn = min(1024, t), 1024

    def body(g_ref, u_ref, w_ref, r_ref, o_ref, acc):
        kk = pl.program_id(2)

        @pl.when(kk == 0)
        def _():
            acc[...] = jnp.zeros_like(acc)

        @pl.when(kk < 2)
        def _():
            acc[...] += _dot_nt(g_ref[...], w_ref[...])

        @pl.when(kk >= 2)
        def _():
            acc[...] += _dot_nt(u_ref[...], w_ref[...])

        @pl.when(kk == 3)
        def _():
            o_ref[...] = acc[...] + ALPHA * r_ref[...]

    return _pcall(body, name=name, grid=(D // tn, t // tm, 4),
                  in_specs=[pl.BlockSpec((tm, UPW), lambda j, i, kk: (i, jnp.minimum(kk, 1))),
                            pl.BlockSpec((tm, UPW), lambda j, i, kk: (i, jnp.maximum(kk - 2, 0))),
                            pl.BlockSpec((None, None, tn, UPW), lambda j, i, kk: (l, kk, j, 0)),
                            pl.BlockSpec((tm, tn), lambda j, i, kk: (i, j))],
                  out_specs=pl.BlockSpec((tm, tn), lambda j, i, kk: (i, j)),
                  out_shape=jax.ShapeDtypeStruct((t, D), F32),
                  scratch_shapes=[pltpu.VMEM((tm, tn), F32)],
                  compiler_params=_cp("parallel", "parallel", "arbitrary"))(dgate, dup, wup, res)


def ffn_up(xb, wup, l, *, name):
    t = xb.shape[0]
    tm, tn = min(1024, t), UPW

    def body(x_ref, wg_ref, wu_ref, g_ref, u_ref, a_ref):
        x = x_ref[...]
        for cols in _col_chunks(tn):
            g = _dot(x, wg_ref[:, cols])
            u = _dot(x, wu_ref[:, cols])
            g_ref[:, cols] = g.astype(BF16)
            u_ref[:, cols] = u.astype(BF16)
            a_ref[:, cols] = (g * _sigmoid(g) * u).astype(BF16)

    blk = pl.BlockSpec((tm, tn), lambda j, i: (i, j))
    return _pcall(body, name=name, grid=(DFF // tn, t // tm),
                  in_specs=[pl.BlockSpec((tm, D), lambda j, i: (i, 0)),
                            pl.BlockSpec((None, None, D, tn), lambda j, i: (l, j, 0, 0)),
                            pl.BlockSpec((None, None, D, tn), lambda j, i: (l, 2 + j, 0, 0))],
                  out_specs=[blk, blk, blk],
                  out_shape=[jax.ShapeDtypeStruct((t, DFF), BF16)] * 3,
                  compiler_params=_cp("parallel", "parallel"))(xb, wup, wup)


def matmul_res_ln(a, w, l, res, g, b, *, mm_scale, name):
    t, k = a.shape
    tm = min(512, t)

    def body(a_ref, w_ref, res_ref, g_ref, b_ref, r_ref, y_ref, yb_ref):
        f = _dot(a_ref[...], w_ref[...])
        r = ALPHA * res_ref[...] + mm_scale * f
        xc, rstd = _ln_stats(r)
        y = xc * rstd * g_ref[...] + b_ref[...]
        r_ref[...] = r
        y_ref[...] = y
        yb_ref[...] = y.astype(BF16)

    return _pcall(body, name=name, grid=(t // tm,),
                  in_specs=[_rows(tm, k), _layer(l, (k, D)), _rows(tm, D), _fix((1, D)), _fix((1, D))],
                  out_specs=[_rows(tm, D)] * 3,
                  out_shape=[jax.ShapeDtypeStruct((t, D), F32), jax.ShapeDtypeStruct((t, D), F32),
                             jax.ShapeDtypeStruct((t, D), BF16)],
                  compiler_params=_cp("parallel"))(a, w, res, g, b)


def ln_bwd(dy, r, g, *, out_scale, name):
    t = dy.shape[0]
    tm = min(256, t)

    def body(dy_ref, r_ref, g_ref, dr_ref, drb_ref, dg_ref, db_ref):
        @pl.when(pl.program_id(0) == 0)
        def _():
            dg_ref[...] = jnp.zeros_like(dg_ref)
            db_ref[...] = jnp.zeros_like(db_ref)

        xc, rstd = _ln_stats(r_ref[...])
        xhat = xc * rstd
        d = dy_ref[...]
        dxh = d * g_ref[...]
        dr = rstd * (dxh - jnp.mean(dxh, axis=-1, keepdims=True)
                     - xhat * jnp.mean(dxh * xhat, axis=-1, keepdims=True))
        dr_ref[...] = dr
        drb_ref[...] = (out_scale * dr).astype(BF16)
        dg_ref[...] += jnp.sum(d * xhat, axis=0, keepdims=True)
        db_ref[...] += jnp.sum(d, axis=0, keepdims=True)

    return _pcall(body, name=name, grid=(t // tm,),
                  in_specs=[_rows(tm, D), _rows(tm, D), _fix((1, D))],
                  out_specs=[_rows(tm, D), _rows(tm, D), _fix((1, D)), _fix((1, D))],
                  out_shape=[jax.ShapeDtypeStruct((t, D), F32), jax.ShapeDtypeStruct((t, D), BF16),
                             jax.ShapeDtypeStruct((1, D), F32), jax.ShapeDtypeStruct((1, D), F32)],
                  compiler_params=_cp("arbitrary"))(dy, r, g)


def ffn_down_bwd(dfb, wd, l, gate, up, *, name):
    t = dfb.shape[0]
    tm, tn = min(1024, t), UPW
    nj = DFF // tn

    def body(df_ref, w_ref, g_ref, u_ref, dg_ref, du_ref):
        df = df_ref[...]
        for cols in _col_chunks(tn):
            da = _dot_nt(df, w_ref[cols, :])
            g = g_ref[:, cols].astype(F32)
            s = _sigmoid(g)
            gs = g * s
            dg_ref[:, cols] = (da * u_ref[:, cols].astype(F32) * (s + gs * (1.0 - s))).astype(BF16)
            du_ref[:, cols] = (da * gs).astype(BF16)

    blk = pl.BlockSpec((tm, tn), lambda j, i: (i, j))
    return _pcall(body, name=name, grid=(nj, t // tm),
                  in_specs=[pl.BlockSpec((tm, D), lambda j, i: (i, 0)),
                            pl.BlockSpec((None, tn, D), lambda j, i: (l, j, 0)), blk, blk],
                  out_specs=[blk, blk],
                  out_shape=[jax.ShapeDtypeStruct((t, DFF), BF16), jax.ShapeDtypeStruct((t, DFF), BF16)],
                  compiler_params=_cp("parallel", "parallel"))(dfb, wd, gate, up)


def ple_fwd(xb, x, pb, wgate, l, wproj, bgate, g, b, *, name):
    t = x.shape[0]
    tm = min(512, t)

    def body(xb_ref, x_ref, p_ref, wg_ref, wp_ref, bg_ref, g_ref, b_ref, r_ref, y_ref, yb_ref):
        gl = _dot(xb_ref[...], wg_ref[...]) + bg_ref[...]
        pe = _dot(p_ref[...], wp_ref[...])
        r = ALPHA * x_ref[...] + _sigmoid(gl) * pe
        xc, rstd = _ln_stats(r)
        y = xc * rstd * g_ref[...] + b_ref[...]
        r_ref[...] = r
        y_ref[...] = y
        yb_ref[...] = y.astype(BF16)

    return _pcall(body, name=name, grid=(t // tm,),
                  in_specs=[_rows(tm, D), _rows(tm, D), _rows(tm, PLE), _layer(l, (D, D)), _fix((PLE, D)),
                            _fix((1, D)), _fix((1, D)), _fix((1, D))],
                  out_specs=[_rows(tm, D)] * 3,
                  out_shape=[jax.ShapeDtypeStruct((t, D), F32), jax.ShapeDtypeStruct((t, D), F32),
                             jax.ShapeDtypeStruct((t, D), BF16)],
                  compiler_params=_cp("parallel"))(xb, x, pb, wgate, wproj, bgate, g, b)


def ple_bwd(dy, r, xb, pb, wgate, l, wproj, bgate, g, *, name):
    t = dy.shape[0]
    tm = min(512, t)

    def body(dy_ref, r_ref, xb_ref, p_ref, wg_ref, wp_ref, bg_ref, g_ref,
             dr_ref, dgl_ref, dpe_ref, dg_ref, db_ref, dbg_ref):
        @pl.when(pl.program_id(0) == 0)
        def _():
            dg_ref[...] = jnp.zeros_like(dg_ref)
            db_ref[...] = jnp.zeros_like(db_ref)
            dbg_ref[...] = jnp.zeros_like(dbg_ref)

        xc, rstd = _ln_stats(r_ref[...])
        xhat = xc * rstd
        d = dy_ref[...]
        dxh = d * g_ref[...]
        dr = rstd * (dxh - jnp.mean(dxh, axis=-1, keepdims=True)
                     - xhat * jnp.mean(dxh * xhat, axis=-1, keepdims=True))
        s = _sigmoid(_dot(xb_ref[...], wg_ref[...]) + bg_ref[...])
        pe = _dot(p_ref[...], wp_ref[...])
        dgl = dr * pe * s * (1.0 - s)
        dr_ref[...] = dr
        dgl_ref[...] = dgl.astype(BF16)
        dpe_ref[...] = (dr * s).astype(BF16)
        dg_ref[...] += jnp.sum(d * xhat, axis=0, keepdims=True)
        db_ref[...] += jnp.sum(d, axis=0, keepdims=True)
        dbg_ref[...] += jnp.sum(dgl, axis=0, keepdims=True)

    vec = jax.ShapeDtypeStruct((1, D), F32)
    return _pcall(body, name=name, grid=(t // tm,),
                  in_specs=[_rows(tm, D), _rows(tm, D), _rows(tm, D), _rows(tm, PLE), _layer(l, (D, D)),
                            _fix((PLE, D)), _fix((1, D)), _fix((1, D))],
                  out_specs=[_rows(tm, D), _rows(tm, D), _rows(tm, D), _fix((1, D)), _fix((1, D)), _fix((1, D))],
                  out_shape=[jax.ShapeDtypeStruct((t, D), F32), jax.ShapeDtypeStruct((t, D), BF16),
                             jax.ShapeDtypeStruct((t, D), BF16), vec, vec, vec],
                  compiler_params=_cp("arbitrary"))(dy, r, xb, pb, wgate, wproj, bgate, g)


def loss_head(y, tgt, *, name):
    t = y.shape[0]
    tm = min(256, t)

    def body(y_ref, t_ref, dy_ref, sq_ref):
        @pl.when(pl.program_id(0) == 0)
        def _():
            sq_ref[...] = jnp.zeros_like(sq_ref)

        e = y_ref[...] - t_ref[...]
        dy_ref[...] = e / float(D)
        sq_ref[...] += jnp.sum(e * e, axis=0, keepdims=True)

    return _pcall(body, name=name, grid=(t // tm,),
                  in_specs=[_rows(tm, D), _rows(tm, D)],
                  out_specs=[_rows(tm, D), _fix((1, D))],
                  out_shape=[jax.ShapeDtypeStruct((t, D), F32), jax.ShapeDtypeStruct((1, D), F32)],
                  compiler_params=_cp("arbitrary"))(y, tgt)


def _lru_gates(xc, wa_ref, wx_ref, ba_ref, bx_ref, lam_ref):
    xcb = xc.astype(BF16)
    r = _sigmoid(_dot(xcb, wa_ref[...]) + ba_ref[...])
    ig = _sigmoid(_dot(xcb, wx_ref[...]) + bx_ref[...])
    sp = _softplus(-lam_ref[...])
    la = -LRU_C * r * sp
    a = jnp.exp(la)
    mult = jnp.sqrt(-_expm1(2.0 * la))
    return r, ig, sp, la, a, mult


def lru_fwd(z, cw, cb, wa, wx, ba, bx, lam, *, name):
    t = z.shape[0]
    tm = min(256, t)
    hb = tm // 8

    def body(ax_ref, prev_ref, ay_ref, cw_ref, cb_ref, wa_ref, wx_ref, ba_ref, bx_ref, lam_ref,
             xc_ref, xcb_ref, h_ref, ya_ref, xs, a_s, b_s, hc):
        i = pl.program_id(0)

        @pl.when(i == 0)
        def _():
            hc[...] = jnp.zeros_like(hc)

        xs[0:8, :] = jnp.where(i == 0, 0.0, prev_ref[...])
        xs[8:, :] = ax_ref[...]
        xc = cb_ref[...] + cw_ref[0:1, :] * xs[5:5 + tm, :]
        for k in range(1, 4):
            xc = xc + cw_ref[k:k + 1, :] * xs[5 + k:5 + k + tm, :]
        r, ig, sp, la, a, mult = _lru_gates(xc, wa_ref, wx_ref, ba_ref, bx_ref, lam_ref)
        a_s[...] = a
        b_s[...] = mult * (ig * xc)
        xc_ref[...] = xc
        xcb_ref[...] = xc.astype(BF16)

        def step(g, h):
            base = pl.multiple_of(g * 8, 8)
            a8 = a_s[pl.ds(base, 8), :]
            b8 = b_s[pl.ds(base, 8), :]
            for j in range(8):
                h = a8[j:j + 1, :] * h + b8[j:j + 1, :]
                h_ref[pl.ds(base + j, 1), :] = h
            return h

        hc[...] = lax.fori_loop(0, tm // 8, step, hc[...])
        ya_ref[...] = (_gelu(ay_ref[...]) * h_ref[...]).astype(BF16)

    vec = _fix((1, BW))
    return _pcall(body, name=name, grid=(t // tm,),
                  in_specs=[_rows(tm, BW, AX // BW),
                            pl.BlockSpec((8, BW), lambda i: (jnp.maximum(i * hb - 1, 0), AX // BW)),
                            _rows(tm, BW, AY // BW), _fix((4, BW)), vec, _fix((BW, BW)), _fix((BW, BW)),
                            vec, vec, vec],
                  out_specs=[_rows(tm, BW)] * 4,
                  out_shape=[jax.ShapeDtypeStruct((t, BW), F32), jax.ShapeDtypeStruct((t, BW), BF16),
                             jax.ShapeDtypeStruct((t, BW), F32), jax.ShapeDtypeStruct((t, BW), BF16)],
                  scratch_shapes=[pltpu.VMEM((tm + 8, BW), F32), pltpu.VMEM((tm, BW), F32),
                                  pltpu.VMEM((tm, BW), F32), pltpu.VMEM((1, BW), F32)],
                  compiler_params=_cp("arbitrary"))(z, z, z, cw, cb, wa, wx, ba, bx, lam)


def lru_bwd(dya, z, h, xc, wa, wx, ba, bx, lam, *, name):
    t = dya.shape[0]
    tm = min(256, t)
    nb = t // tm
    hb = tm // 8

    def body(dya_ref, ay_ref, h_ref, hprev_ref, xc_ref, wa_ref, wx_ref, ba_ref, bx_ref,
             lam_ref, day_ref, dxc_ref, dpr_ref, dpi_ref, dba_ref, dbx_ref, dlam_ref,
             hs, a_s, g_s, d_s, cc):
        i = pl.program_id(0)

        @pl.when(i == 0)
        def _():
            cc[...] = jnp.zeros_like(cc)
            dba_ref[...] = jnp.zeros_like(dba_ref)
            dbx_ref[...] = jnp.zeros_like(dbx_ref)
            dlam_ref[...] = jnp.zeros_like(dlam_ref)

        xc = xc_ref[...]
        r, ig, sp, la, a, mult = _lru_gates(xc, wa_ref, wx_ref, ba_ref, bx_ref, lam_ref)
        ay = ay_ref[...]
        dya = dya_ref[...]
        hcur = h_ref[...]
        day_ref[...] = (dya * hcur * _gelu_grad(ay)).astype(BF16)
        a_s[...] = a
        g_s[...] = dya * _gelu(ay)

        def step(gg, cin):
            g = tm // 8 - 1 - gg
            base = pl.multiple_of(g * 8, 8)
            a8 = a_s[pl.ds(base, 8), :]
            g8 = g_s[pl.ds(base, 8), :]
            for j in range(7, -1, -1):
                d = g8[j:j + 1, :] + cin
                d_s[pl.ds(base + j, 1), :] = d
                cin = a8[j:j + 1, :] * d
            return cin

        cc[...] = lax.fori_loop(0, tm // 8, step, cc[...])
        dht = d_s[...]
        hs[0:8, :] = jnp.where(i == nb - 1, 0.0, hprev_ref[...])
        hs[8:, :] = hcur
        da = dht * hs[7:7 + tm, :]
        dmult = dht * ig * xc
        dig = dht * mult * xc
        dla = da * a - dmult * a * a / mult
        dpr = dla * (-LRU_C * sp) * r * (1.0 - r)
        dpi = dig * ig * (1.0 - ig)
        dprb = dpr.astype(BF16)
        dpib = dpi.astype(BF16)
        dxc_ref[...] = dht * mult * ig + _dot_nt(dprb, wa_ref[...]) + _dot_nt(dpib, wx_ref[...])
        dpr_ref[...] = dprb
        dpi_ref[...] = dpib
        dba_ref[...] += jnp.sum(dpr, axis=0, keepdims=True)
        dbx_ref[...] += jnp.sum(dpi, axis=0, keepdims=True)
        dlam_ref[...] += jnp.sum(dla * (-LRU_C * r), axis=0, keepdims=True) * (-_sigmoid(-lam_ref[...]))

    vec = _fix((1, BW))
    mat = _fix((BW, BW))
    rev = lambda col: pl.BlockSpec((tm, BW), lambda i: (nb - 1 - i, col))
    vshape = jax.ShapeDtypeStruct((1, BW), F32)
    return _pcall(body, name=name, grid=(nb,),
                  in_specs=[rev(0), rev(AY // BW), rev(0),
                            pl.BlockSpec((8, BW), lambda i: (jnp.maximum((nb - 1 - i) * hb - 1, 0), 0)),
                            rev(0), mat, mat, vec, vec, vec],
                  out_specs=[rev(0), rev(0), rev(0), rev(0), vec, vec, vec],
                  out_shape=[jax.ShapeDtypeStruct((t, BW), BF16), jax.ShapeDtypeStruct((t, BW), F32),
                             jax.ShapeDtypeStruct((t, BW), BF16), jax.ShapeDtypeStruct((t, BW), BF16),
                             vshape, vshape, vshape],
                  scratch_shapes=[pltpu.VMEM((tm + 8, BW), F32), pltpu.VMEM((tm, BW), F32),
                                  pltpu.VMEM((tm, BW), F32), pltpu.VMEM((tm, BW), F32),
                                  pltpu.VMEM((1, BW), F32)],
                  compiler_params=_cp("arbitrary"))(dya, z, h, h, xc, wa, wx, ba, bx, lam)


def conv_bwd(dxc, z, cw, *, name):
    t = dxc.shape[0]
    tm = min(256, t)
    nb = t // tm
    hb = tm // 8

    def body(d_ref, dnext_ref, ax_ref, prev_ref, cw_ref, dax_ref, dcw_ref, dcb_ref, ds, xs):
        i = pl.program_id(0)

        @pl.when(i == 0)
        def _():
            dcw_ref[...] = jnp.zeros_like(dcw_ref)
            dcb_ref[...] = jnp.zeros_like(dcb_ref)

        d = d_ref[...]
        ds[0:tm, :] = d
        ds[tm:, :] = jnp.where(i == nb - 1, 0.0, dnext_ref[...])
        xs[0:8, :] = jnp.where(i == 0, 0.0, prev_ref[...])
        xs[8:, :] = ax_ref[...]
        dax = cw_ref[3:4, :] * d
        for k in range(3):
            dax = dax + cw_ref[k:k + 1, :] * ds[3 - k:3 - k + tm, :]
        dax_ref[...] = dax.astype(BF16)
        for k in range(4):
            dcw_ref[k:k + 1, :] += jnp.sum(d * xs[5 + k:5 + k + tm, :], axis=0, keepdims=True)
        dcb_ref[...] += jnp.sum(d, axis=0, keepdims=True)

    return _pcall(body, name=name, grid=(nb,),
                  in_specs=[_rows(tm, BW),
                            pl.BlockSpec((8, BW), lambda i: (jnp.minimum((i + 1) * hb, nb * hb - 1), 0)),
                            _rows(tm, BW, AX // BW),
                            pl.BlockSpec((8, BW), lambda i: (jnp.maximum(i * hb - 1, 0), AX // BW)),
                            _fix((4, BW))],
                  out_specs=[_rows(tm, BW), _fix((4, BW)), _fix((1, BW))],
                  out_shape=[jax.ShapeDtypeStruct((t, BW), BF16), jax.ShapeDtypeStruct((4, BW), F32),
                             jax.ShapeDtypeStruct((1, BW), F32)],
                  scratch_shapes=[pltpu.VMEM((tm + 8, BW), F32), pltpu.VMEM((tm + 8, BW), F32)],
                  compiler_params=_cp("arbitrary"))(dxc, dxc, z, z, cw)


GLA_CB = 4


def _gla_consts():
    tri = (jnp.arange(CHUNK)[:, None] >= jnp.arange(CHUNK)[None, :]).astype(F32)
    mask = ((jnp.arange(BW)[:, None] // 128) == (jnp.arange(256)[None, :] // 64)).astype(F32)
    return tri, mask


def gla_fwd(z, zb, wg2p, bg, ng, *, name):
    t = z.shape[0]
    tm = GLA_CB * CHUNK
    nc = t // CHUNK
    tri, mask = _gla_consts()

    def body(q_ref, k_ref, v_ref, misc_ref, br_ref, w_ref, bg_ref, ng_ref, tri_ref, mask_ref,
             yb_ref, st_ref, st):
        @pl.when(pl.program_id(0) == 0)
        def _():
            st[...] = jnp.zeros_like(st)

        for c in range(GLA_CB):
            rows = slice(c * CHUNK, (c + 1) * CHUNK)
            pre = _dot(misc_ref[rows, :], w_ref[...]) + bg_ref[...]
            la = _log_sigmoid(pre) / GLA_TAU
            gc = _dot_hi(tri_ref[...], la)
            gt = gc[CHUNK - 1:CHUNK, :]
            kdec = k_ref[rows, :] * jnp.exp(gt - gc)
            delta = _dot_tn(v_ref[rows, :], kdec.astype(BF16))
            s_new = st[...] * jnp.exp(gt) + delta * mask_ref[...]
            st[...] = s_new
            st_ref[c] = s_new
            o = _dot_nt(q_ref[rows, :], s_new.astype(BF16)) * (64.0 ** -0.5)
            br = br_ref[rows, :]
            for hd in range(4):
                cols = slice(hd * 128, (hd + 1) * 128)
                oh = o[:, cols]
                rs = lax.rsqrt(jnp.mean(oh * oh, axis=-1, keepdims=True) + RMS_EPS)
                brh = br[:, cols]
                yb_ref[rows, cols] = (oh * rs * ng_ref[:, cols] * (brh * _sigmoid(brh))).astype(BF16)

    return _pcall(body, name=name, grid=(t // tm,),
                  in_specs=[_rows(tm, 256, BQ // 256), _rows(tm, 256, BK // 256), _rows(tm, BW, BV // BW),
                            _rows(tm, 128, MISC // 128), _rows(tm, BW, BR // BW), _fix((128, 256)),
                            _fix((1, 256)), _fix((1, BW)), _fix((CHUNK, CHUNK)), _fix((BW, 256))],
                  out_specs=[_rows(tm, BW), pl.BlockSpec((GLA_CB, BW, 256), lambda i: (i, 0, 0))],
                  out_shape=[jax.ShapeDtypeStruct((t, BW), BF16), jax.ShapeDtypeStruct((nc, BW, 256), F32)],
                  scratch_shapes=[pltpu.VMEM((BW, 256), F32)],
                  compiler_params=_cp("arbitrary"))(zb, z, zb, zb, z, wg2p, bg, ng, tri, mask)


def gla_bwd(dyb, z, zb, states, wg2p, bg, ng, *, name):
    t = z.shape[0]
    tm = GLA_CB * CHUNK
    nb = t // tm
    tri, mask = _gla_consts()
    triu = tri.T

    def body(dy_ref, q_ref, k_ref, v_ref, misc_ref, br_ref, st_ref, sp_ref, w_ref, bg_ref, ng_ref,
             tri_ref, triu_ref, mask_ref,
             dq_ref, dk_ref, dv_ref, dbr_ref, dmisc_ref, dpre_ref, dbg_ref, dng_ref, cc):
        i = pl.program_id(0)

        @pl.when(i == 0)
        def _():
            cc[...] = jnp.zeros_like(cc)
            dbg_ref[...] = jnp.zeros_like(dbg_ref)
            dng_ref[...] = jnp.zeros_like(dng_ref)

        last_row = lax.broadcasted_iota(jnp.int32, (CHUNK, 256), 0) == CHUNK - 1
        for c in range(GLA_CB - 1, -1, -1):
            rows = slice(c * CHUNK, (c + 1) * CHUNK)
            pre = _dot(misc_ref[rows, :], w_ref[...]) + bg_ref[...]
            la = _log_sigmoid(pre) / GLA_TAU
            gc = _dot_hi(tri_ref[...], la)
            gt = gc[CHUNK - 1:CHUNK, :]
            eg = jnp.exp(gt - gc)
            kdec = k_ref[rows, :] * eg
            e = jnp.exp(gt)
            s_n = st_ref[c]
            if c > 0:
                s_prev = st_ref[c - 1]
            else:
                s_prev = jnp.where(i == nb - 1, 0.0, sp_ref[0])
            sb = s_n.astype(BF16)
            qb = q_ref[rows, :]
            o = _dot_nt(qb, sb) * (64.0 ** -0.5)
            br = br_ref[rows, :]
            dy = dy_ref[rows, :]
            do_parts = []
            for hd in range(4):
                cols = slice(hd * 128, (hd + 1) * 128)
                oh = o[:, cols]
                rs = lax.rsqrt(jnp.mean(oh * oh, axis=-1, keepdims=True) + RMS_EPS)
                ohat = oh * rs
                brh = br[:, cols]
                sg = _sigmoid(brh)
                dyh = dy[:, cols]
                ngh = ng_ref[:, cols]
                don = dyh * (brh * sg)
                dbr_ref[rows, cols] = (dyh * (ohat * ngh) * sg * (1.0 + brh * (1.0 - sg))).astype(BF16)
                dng_ref[:, cols] += jnp.sum(don * ohat, axis=0, keepdims=True)
                doh = don * ngh
                do_parts.append(rs * (doh - ohat * jnp.mean(doh * ohat, axis=-1, keepdims=True)))
            dob = jnp.concatenate(do_parts, axis=1).astype(BF16)
            dq_ref[rows, :] = (_dot(dob, sb) * (64.0 ** -0.5)).astype(BF16)
            dst = cc[...] + _dot_tn(dob, qb) * (64.0 ** -0.5) * mask_ref[...]
            dsb = dst.astype(BF16)
            dkdec = _dot(v_ref[rows, :], dsb)
            dv_ref[rows, :] = _dot_nt(kdec.astype(BF16), dsb).astype(BF16)
            dgt = jnp.sum(dst * s_prev, axis=0, keepdims=True) * e
            dk_ref[rows, :] = (dkdec * eg).astype(BF16)
            dd = dkdec * kdec
            dgt = dgt + jnp.sum(dd, axis=0, keepdims=True)
            dgc = jnp.where(last_row, dgt - dd, -dd)
            dla = _dot_hi(triu_ref[...], dgc)
            dpre = dla * (1.0 / GLA_TAU) * _sigmoid(-pre)
            dpb = dpre.astype(BF16)
            dpre_ref[rows, :] = dpb
            dmisc_ref[rows, :] = _dot_nt(dpb, w_ref[...])
            dbg_ref[...] += jnp.sum(dpre, axis=0, keepdims=True)
            cc[...] = dst * e

    rev = lambda w, col: pl.BlockSpec((tm, w), lambda i: (nb - 1 - i, col))
    return _pcall(body, name=name, grid=(nb,),
                  in_specs=[rev(BW, 0), rev(256, BQ // 256), rev(256, BK // 256), rev(BW, BV // BW),
                            rev(128, MISC // 128), rev(BW, BR // BW),
                            pl.BlockSpec((GLA_CB, BW, 256), lambda i: (nb - 1 - i, 0, 0)),
                            pl.BlockSpec((1, BW, 256), lambda i: (jnp.maximum((nb - 1 - i) * GLA_CB - 1, 0), 0, 0)),
                            _fix((128, 256)), _fix((1, 256)), _fix((1, BW)),
                            _fix((CHUNK, CHUNK)), _fix((CHUNK, CHUNK)), _fix((BW, 256))],
                  out_specs=[rev(256, 0), rev(256, 0), rev(BW, 0), rev(BW, 0), rev(128, 0), rev(256, 0),
                             _fix((1, 256)), _fix((1, BW))],
                  out_shape=[jax.ShapeDtypeStruct((t, 256), BF16), jax.ShapeDtypeStruct((t, 256), BF16),
                             jax.ShapeDtypeStruct((t, BW), BF16), jax.ShapeDtypeStruct((t, BW), BF16),
                             jax.ShapeDtypeStruct((t, 128), F32), jax.ShapeDtypeStruct((t, 256), BF16),
                             jax.ShapeDtypeStruct((1, 256), F32), jax.ShapeDtypeStruct((1, BW), F32)],
                  scratch_shapes=[pltpu.VMEM((BW, 256), F32)],
                  compiler_params=_cp("arbitrary"))(dyb, zb, z, zb, zb, z, states, states, wg2p, bg, ng,
                                                    tri, triu, mask)


FOX_SCALE = 64.0 ** -0.5
NEG = -1e30


def fox_fcum(z, bfp, *, name):
    t = z.shape[0]
    tm = min(256, t)
    tri = (jnp.arange(tm)[:, None] >= jnp.arange(tm)[None, :]).astype(F32)

    def body(m_ref, b_ref, tri_ref, o_ref, cc):
        @pl.when(pl.program_id(0) == 0)
        def _():
            cc[...] = jnp.zeros_like(cc)

        lf = _log_sigmoid(m_ref[...] + b_ref[...])
        cs = _dot_hi(tri_ref[...], lf) + cc[...]
        o_ref[...] = cs
        cc[...] = cs[tm - 1:tm, :]

    return _pcall(body, name=name, grid=(t // tm,),
                  in_specs=[_rows(tm, 128, MISC // 128), _fix((1, 128)), _fix((tm, tm))],
                  out_specs=_rows(tm, 128), out_shape=jax.ShapeDtypeStruct((t, 128), F32),
                  scratch_shapes=[pltpu.VMEM((1, 128), F32)],
                  compiler_params=_cp("arbitrary"))(z, bfp, tri)


def fox_dcf(dfc, z, bfp, dmisc_g, *, name):
    t = z.shape[0]
    tm = min(256, t)
    nb = t // tm
    triu = (jnp.arange(tm)[:, None] <= jnp.arange(tm)[None, :]).astype(F32)

    def body(d_ref, m_ref, b_ref, g_ref, tri_ref, o_ref, dbf_ref, cc):
        @pl.when(pl.program_id(0) == 0)
        def _():
            cc[...] = jnp.zeros_like(cc)
            dbf_ref[...] = jnp.zeros_like(dbf_ref)

        rc = _dot_hi(tri_ref[...], d_ref[...]) + cc[...]
        cc[...] = rc[0:1, :]
        dcf = rc * _sigmoid(-(m_ref[...] + b_ref[...]))
        o_ref[...] = (dcf + g_ref[...]).astype(BF16)
        dbf_ref[...] += jnp.sum(dcf, axis=0, keepdims=True)

    rev = lambda col: pl.BlockSpec((tm, 128), lambda i: (nb - 1 - i, col))
    return _pcall(body, name=name, grid=(nb,),
                  in_specs=[rev(0), rev(MISC // 128), _fix((1, 128)), rev(0), _fix((tm, tm))],
                  out_specs=[rev(0), _fix((1, 128))],
                  out_shape=[jax.ShapeDtypeStruct((t, 128), BF16), jax.ShapeDtypeStruct((1, 128), F32)],
                  scratch_shapes=[pltpu.VMEM((1, 128), F32)],
                  compiler_params=_cp("arbitrary"))(dfc, z, bfp, dmisc_g, triu)


def fox_delta(dyc, ycf, *, name):
    t = dyc.shape[0]
    tm = min(256, t)
    seg = ((jnp.arange(BW)[:, None] // 64) == jnp.arange(128)[None, :]).astype(F32)

    def body(d_ref, o_ref, s_ref, out_ref):
        out_ref[...] = _dot_hi(d_ref[...] * o_ref[...], s_ref[...])

    return _pcall(body, name=name, grid=(t // tm,),
                  in_specs=[_rows(tm, BW), _rows(tm, BW), _fix((BW, 128))],
                  out_specs=_rows(tm, 128), out_shape=jax.ShapeDtypeStruct((t, 128), F32),
                  compiler_params=_cp("parallel"))(dyc, ycf, seg)


def fox_fwd_t(zb, frow, fkb, *, name):
    t = zb.shape[0]
    tq = min(512, t)
    nq = t // tq
    rep = tq // 128

    pairs = [(i, j) for i in range(nq) for j in range(i + 1)]
    qi_tab = jnp.asarray([p[0] for p in pairs], jnp.int32)
    kj_tab = jnp.asarray([p[1] for p in pairs], jnp.int32)

    def body(qi_ref, kj_ref, q_ref, k_ref, v_ref, fq_ref, fk_ref, y_ref, yf_ref, lse_ref, m_s, l_s, acc):
        step = pl.program_id(1)
        i, j = qi_ref[step], kj_ref[step]

        @pl.when(j == 0)
        def _():
            m_s[...] = jnp.full_like(m_s, NEG)
            l_s[...] = jnp.zeros_like(l_s)
            acc[...] = jnp.zeros_like(acc)

        lo = lax.broadcasted_iota(jnp.int32, (tq, 128), 1) < 64

        def work(diagonal):
            q = q_ref[...]
            k = k_ref[...]
            v = v_ref[...]
            if diagonal:
                key = lax.broadcasted_iota(jnp.int32, (tq, tq), 0)
                qry = lax.broadcasted_iota(jnp.int32, (tq, tq), 1)
                keep = key <= qry
            for hh in range(2):
                sel = lo if hh == 0 else jnp.logical_not(lo)
                qh = jnp.where(sel, q, jnp.zeros_like(q))
                s = _dot_nt(k, qh) + fq_ref[hh] - jnp.tile(fk_ref[hh], (1, rep))
                if diagonal:
                    s = jnp.where(keep, s, NEG)
                m_old = m_s[hh]
                m_new = jnp.maximum(m_old, jnp.max(s, axis=0, keepdims=True))
                p = jnp.exp(s - m_new)
                corr = jnp.exp(m_old - m_new)
                l_s[hh] = l_s[hh] * corr + jnp.sum(p, axis=0, keepdims=True)
                m_s[hh] = m_new
                pv = _dot_tn(v, p.astype(BF16))
                rows = slice(64 * hh, 64 * hh + 64)
                acc[rows, :] = acc[rows, :] * corr + pv[rows, :]

        @pl.when(j < i)
        def _():
            work(False)

        @pl.when(j == i)
        def _():
            work(True)
            first = lax.broadcasted_iota(jnp.int32, (128, tq), 0) < 64
            out = (acc[...] * jnp.where(first, 1.0 / l_s[0], 1.0 / l_s[1])).T
            y_ref[...] = out.astype(BF16)
            yf_ref[...] = out
            lse_ref[...] = m_s[...] + jnp.log(l_s[...])

    kv = lambda off: pl.BlockSpec((tq, 128), lambda h, s, qi, kj: (kj[s], off // 128 + h))
    gs = pltpu.PrefetchScalarGridSpec(
        num_scalar_prefetch=2, grid=(4, len(pairs)),
        in_specs=[pl.BlockSpec((tq, 128), lambda h, s, qi, kj: (qi[s], CQ // 128 + h)), kv(CK), kv(CV),
                  pl.BlockSpec((2, 1, tq), lambda h, s, qi, kj: (h, 0, qi[s])),
                  pl.BlockSpec((2, tq, 128), lambda h, s, qi, kj: (h, kj[s], 0))],
        out_specs=[pl.BlockSpec((tq, 128), lambda h, s, qi, kj: (qi[s], h)),
                   pl.BlockSpec((tq, 128), lambda h, s, qi, kj: (qi[s], h)),
                   pl.BlockSpec((2, 1, tq), lambda h, s, qi, kj: (h, 0, qi[s]))],
        scratch_shapes=[pltpu.VMEM((2, 1, tq), F32), pltpu.VMEM((2, 1, tq), F32), pltpu.VMEM((128, tq), F32)])
    return _pcall(body, name=name, grid_spec=gs,
                  out_shape=[jax.ShapeDtypeStruct((t, BW), BF16), jax.ShapeDtypeStruct((t, BW), F32),
                             jax.ShapeDtypeStruct((FOX_H, 1, t), F32)],
                  compiler_params=_cp("parallel", "arbitrary"))(qi_tab, kj_tab, zb, zb, zb, frow, fkb)


def fox_bwd_t(zb, dyc, frow, fkb, lse, dl, *, name):
    t = zb.shape[0]
    tq = min(512, t)
    nq = t // tq
    rep = tq // 128

    pairs = [(j, i) for j in range(nq) for i in range(j, nq)]
    kj_tab = jnp.asarray([p[0] for p in pairs], jnp.int32)
    qi_tab = jnp.asarray([p[1] for p in pairs], jnp.int32)

    def body(kj_ref, qi_ref, q_ref, k_ref, v_ref, do_ref, fq_ref, fk_ref, lse_ref, dl_ref,
             dq_ref, dk_ref, dv_ref, dfk_ref, dfq_ref, dk_s, dv_s, df_s, dq_s):
        step = pl.program_id(1)
        j, i = kj_ref[step], qi_ref[step]

        @pl.when(step == 0)
        def _():
            dq_s[...] = jnp.zeros_like(dq_s)
            dfq_ref[...] = jnp.zeros_like(dfq_ref)

        @pl.when(i == j)
        def _():
            dk_s[...] = jnp.zeros_like(dk_s)
            dv_s[...] = jnp.zeros_like(dv_s)
            df_s[...] = jnp.zeros_like(df_s)

        lo = lax.broadcasted_iota(jnp.int32, (tq, 128), 1) < 64

        def work(diagonal):
            q = q_ref[...]
            k = k_ref[...]
            v = v_ref[...]
            dob = do_ref[...].astype(BF16)
            if diagonal:
                key = lax.broadcasted_iota(jnp.int32, (tq, tq), 0)
                qry = lax.broadcasted_iota(jnp.int32, (tq, tq), 1)
                keep = key <= qry
            dvs, dks = [], []
            for hh in range(2):
                sel = lo if hh == 0 else jnp.logical_not(lo)
                qh = jnp.where(sel, q, jnp.zeros_like(q))
                doh = jnp.where(sel, dob, jnp.zeros_like(dob))
                p = jnp.exp(_dot_nt(k, qh) + (fq_ref[hh] - lse_ref[hh]) - jnp.tile(fk_ref[hh], (1, rep)))
                if diagonal:
                    p = jnp.where(keep, p, 0.0)
                ds = p * (_dot_nt(v, doh) - dl_ref[hh])
                dsb = ds.astype(BF16)
                dvs.append(_dot(p.astype(BF16), dob))
                dks.append(_dot(dsb, q))
                rows = slice(64 * hh, 64 * hh + 64)
                dq_s[i, rows, :] += _dot_tn(k, dsb)[rows, :]
                part = ds[:, 0:128]
                for r in range(1, rep):
                    part = part + ds[:, 128 * r:128 * (r + 1)]
                df_s[hh] += part
                dfq_ref[hh, i] += jnp.sum(ds, axis=0, keepdims=True)
            dv_s[...] += jnp.where(lo, dvs[0], dvs[1])
            dk_s[...] += jnp.where(lo, dks[0], dks[1])

        @pl.when(i > j)
        def _():
            work(False)

        @pl.when(i == j)
        def _():
            work(True)
            dq_ref[...] = dq_s[i].T.astype(BF16)

        @pl.when(i == nq - 1)
        def _():
            dk_ref[...] = dk_s[...].astype(BF16)
            dv_ref[...] = dv_s[...].astype(BF16)
            for hh in range(2):
                dfk_ref[hh] = -jnp.sum(df_s[hh].T, axis=0, keepdims=True)

    row = lambda: pl.BlockSpec((2, 1, tq), lambda h, s, kj, qi: (h, 0, qi[s]))
    gs = pltpu.PrefetchScalarGridSpec(
        num_scalar_prefetch=2, grid=(4, len(pairs)),
        in_specs=[pl.BlockSpec((tq, 128), lambda h, s, kj, qi: (qi[s], CQ // 128 + h)),
                  pl.BlockSpec((tq, 128), lambda h, s, kj, qi: (kj[s], CK // 128 + h)),
                  pl.BlockSpec((tq, 128), lambda h, s, kj, qi: (kj[s], CV // 128 + h)),
                  pl.BlockSpec((tq, 128), lambda h, s, kj, qi: (qi[s], h)),
                  row(), pl.BlockSpec((2, tq, 128), lambda h, s, kj, qi: (h, kj[s], 0)), row(), row()],
        out_specs=[pl.BlockSpec((tq, 128), lambda h, s, kj, qi: (kj[s], h)),
                   pl.BlockSpec((tq, 128), lambda h, s, kj, qi: (kj[s], h)),
                   pl.BlockSpec((tq, 128), lambda h, s, kj, qi: (kj[s], h)),
                   pl.BlockSpec((2, 1, tq), lambda h, s, kj, qi: (h, 0, kj[s])),
                   pl.BlockSpec((2, nq, 1, tq), lambda h, s, kj, qi: (h, 0, 0, 0))],
        scratch_shapes=[pltpu.VMEM((tq, 128), F32), pltpu.VMEM((tq, 128), F32), pltpu.VMEM((2, tq, 128), F32),
                        pltpu.VMEM((nq, 128, tq), F32)])
    return _pcall(body, name=name, grid_spec=gs,
                  out_shape=[jax.ShapeDtypeStruct((t, BW), BF16), jax.ShapeDtypeStruct((t, BW), BF16),
                             jax.ShapeDtypeStruct((t, BW), BF16), jax.ShapeDtypeStruct((FOX_H, 1, t), F32),
                             jax.ShapeDtypeStruct((FOX_H, nq, 1, tq), F32)],
                  compiler_params=_cp("parallel", "arbitrary"))(kj_tab, qi_tab, zb, zb, zb, dyc, frow, fkb, lse, dl)


def merge_fwd(ya, yb, yc, wbr, z, *, name):
    t = ya.shape[0]
    tm = min(512, t)

    def body(ya_ref, yb_ref, yc_ref, w_ref, g0_ref, g1_ref, g2_ref, o_ref):
        m = _sigmoid(g0_ref[...]) * _dot(ya_ref[...], w_ref[0])
        m = m + _sigmoid(g1_ref[...]) * _dot(yb_ref[...], w_ref[1])
        m = m + _sigmoid(g2_ref[...]) * _dot(yc_ref[...], w_ref[2])
        o_ref[...] = m.astype(BF16)

    return _pcall(body, name=name, grid=(t // tm,),
                  in_specs=[_rows(tm, BW)] * 3 + [_fix((3, BW, D))]
                  + [_rows(tm, D, G0 // D + j) for j in range(3)],
                  out_specs=_rows(tm, D), out_shape=jax.ShapeDtypeStruct((t, D), BF16),
                  compiler_params=_cp("parallel"))(ya, yb, yc, wbr, z, z, z)


def merge_bwd(doutb, wo, l, ya, yb, yc, wbr, z, *, name):
    t = ya.shape[0]
    tm = min(256, t)

    def body(do_ref, wo_ref, ya_ref, yb_ref, yc_ref, w_ref, g0_ref, g1_ref, g2_ref,
             dya_ref, dyb_ref, dyc_ref, dp0_ref, dp1_ref, dp2_ref, dg0_ref, dg1_ref, dg2_ref):
        dm = _dot_nt(do_ref[...], wo_ref[...])
        ys = (ya_ref, yb_ref, yc_ref)
        gs = (g0_ref, g1_ref, g2_ref)
        dys = (dya_ref, dyb_ref, dyc_ref)
        dps = (dp0_ref, dp1_ref, dp2_ref)
        dgs = (dg0_ref, dg1_ref, dg2_ref)
        for j in range(3):
            s = _sigmoid(gs[j][...])
            pj = _dot(ys[j][...], w_ref[j])
            dpb = (dm * s).astype(BF16)
            dps[j][...] = dpb
            dgs[j][...] = (dm * pj * s * (1.0 - s)).astype(BF16)
            dys[j][...] = _dot_nt(dpb, w_ref[j])

    yshape = jax.ShapeDtypeStruct((t, BW), F32)
    dshape = jax.ShapeDtypeStruct((t, D), BF16)
    return _pcall(body, name=name, grid=(t // tm,),
                  in_specs=[_rows(tm, D), _layer(l, (D, D))] + [_rows(tm, BW)] * 3
                  + [_fix((3, BW, D))] + [_rows(tm, D, G0 // D + j) for j in range(3)],
                  out_specs=[_rows(tm, BW)] * 3 + [_rows(tm, D)] * 6,
                  out_shape=[yshape] * 3 + [dshape] * 6,
                  compiler_params=_cp("parallel"))(doutb, wo, ya, yb, yc, wbr, z, z, z)


def adamw(w, g, m, v, *, name):
    nl, r, c = w.shape
    tm = _row_tile(r)

    def body(w_ref, g_ref, m_ref, v_ref, d_ref, mo_ref, vo_ref):
        gg = g_ref[...]
        mn = ADAM_B1 * m_ref[...] + (1.0 - ADAM_B1) * gg
        vn = ADAM_B2 * v_ref[...] + (1.0 - ADAM_B2) * (gg * gg)
        m_hat = mn / (1.0 - ADAM_B1 ** ADAM_STEP)
        v_hat = vn / (1.0 - ADAM_B2 ** ADAM_STEP)
        d_ref[...] = -ADAM_LR * (m_hat / (jnp.sqrt(v_hat) + ADAM_EPS) + ADAM_WD * w_ref[...])
        mo_ref[...] = mn
        vo_ref[...] = vn

    shp = jax.ShapeDtypeStruct((nl, r, c), F32)
    blk = pl.BlockSpec((None, tm, c), lambda l, i: (l, i, 0))
    return _pcall(body, name=name, grid=(nl, r // tm), in_specs=[blk] * 4, out_specs=[blk] * 3,
                  out_shape=[shp] * 3, compiler_params=_cp("parallel", "parallel"))(w, g, m, v)


def _place():
    return lax.axis_index("x"), lax.axis_index("y"), lax.axis_index("c")


def _remote(src, dst, send_sems, recv_sems, k, to):
    return pltpu.make_async_remote_copy(src_ref=src, dst_ref=dst, send_sem=send_sems.at[k],
                                        recv_sem=recv_sems.at[k], device_id=to, device_id_type=MESH)


def gather_weights(shards):
    n = len(shards)

    def body(*refs):
        ins, outs = refs[:n], refs[n:2 * n]
        send_sems, recv_sems, own_send, own_recv = refs[2 * n:]
        x, y, c = _place()
        sib = (x, y, 1 - c)
        chips = [(1 - x, y), (x, 1 - y), (1 - x, 1 - y)]
        k_me = 2 * x + y
        mine, first, passed = [], [], []
        for t in range(n):
            for l in range(DEPTH):
                mine.append(_remote(ins[t].at[l], outs[t].at[l, k_me], own_send, own_recv, 2 * t + l, sib))
            for j, chip in enumerate(chips):
                first.append(_remote(ins[t].at[c], outs[t].at[c, k_me], send_sems, recv_sems, 6 * t + j, (*chip, c)))
        for cp in mine + first:
            cp.start()
        for t in range(n):
            for j, chip in enumerate(chips):
                blk = outs[t].at[c, 2 * chip[0] + chip[1]]
                _remote(blk, blk, send_sems, recv_sems, 6 * t + j, (*chip, c)).wait_recv()
                cp = _remote(blk, blk, send_sems, recv_sems, 6 * t + 3 + j, sib)
                cp.start()
                passed.append(cp)
        for t in range(n):
            for j, chip in enumerate(chips):
                blk = outs[t].at[1 - c, 2 * chip[0] + chip[1]]
                _remote(blk, blk, send_sems, recv_sems, 6 * t + 3 + j, sib).wait_recv()
        for cp in first + passed:
            cp.wait_send()
        for cp in mine:
            cp.wait()

    return _pcall(body, name="gather_weights", in_specs=[ANY] * n, out_specs=[ANY] * n,
                  out_shape=[jax.ShapeDtypeStruct((DEPTH, 4) + s.shape[1:], s.dtype) for s in shards],
                  scratch_shapes=[pltpu.SemaphoreType.DMA((6 * n,)), pltpu.SemaphoreType.DMA((6 * n,)),
                                  pltpu.SemaphoreType.DMA((2 * n,)), pltpu.SemaphoreType.DMA((2 * n,))])(*shards)


def pair_send(gl, layer):
    n = len(gl)

    def body(*refs):
        ins, outs = refs[:n], refs[n:2 * n]
        send_sems, recv_sems = refs[2 * n:]
        x, y, c = _place()
        sib = (x, y, 1 - c)
        cps = [_remote(ins[t], outs[t], send_sems, recv_sems, t, sib) for t in range(n)]

        @pl.when(c == 1 - layer)
        def _():
            for cp in cps:
                cp.start()
            for cp in cps:
                cp.wait_send()

        @pl.when(c == layer)
        def _():
            for cp in cps:
                cp.wait_recv()

    return _pcall(body, name="pair_send_l%d" % layer, in_specs=[ANY] * n, out_specs=[ANY] * n,
                  out_shape=[jax.ShapeDtypeStruct(a.shape, a.dtype) for a in gl],
                  scratch_shapes=[pltpu.SemaphoreType.DMA((n,)), pltpu.SemaphoreType.DMA((n,))])(*gl)


def _chip_copies(ins, outs, send_sems, recv_sems):
    x, y, c = _place()
    chips = [(1 - x, y), (x, 1 - y), (1 - x, 1 - y)]
    return [_remote(ins[t].at[2 * chip[0] + chip[1]], outs[t].at[j], send_sems, recv_sems, 3 * t + j, (*chip, c))
            for t in range(len(ins)) for j, chip in enumerate(chips)]


def chip_send(s1, layer):
    n = len(s1)

    def body(*refs):
        ins, outs = refs[:n], refs[n:2 * n]
        send_sems, recv_sems = refs[2 * n:]
        cps = _chip_copies(ins, outs, send_sems, recv_sems)

        @pl.when(lax.axis_index("c") == layer)
        def _():
            for cp in cps:
                cp.start()
            for cp in cps:
                cp.wait()

    return _pcall(body, name="chip_send_l%d" % layer, in_specs=[ANY] * n, out_specs=[ANY] * n,
                  out_shape=[jax.ShapeDtypeStruct((3,) + a.shape[1:], a.dtype) for a in s1],
                  scratch_shapes=[pltpu.SemaphoreType.DMA((3 * n,)), pltpu.SemaphoreType.DMA((3 * n,))])(*s1)


HBM = pl.BlockSpec(memory_space=pltpu.HBM)
SEM = pl.BlockSpec(memory_space=pltpu.SEMAPHORE)
EFFECT = pltpu.SideEffectType.DATAFLOW_SIDE_EFFECTING


def chip_send_start(s1, layer):
    n = len(s1)
    land = [lax.empty((3,) + a.shape[1:], a.dtype) for a in s1]

    def body(*refs):
        ins, lands = refs[:n], refs[n:2 * n]
        send_sems, recv_sems = refs[2 * n], refs[2 * n + 1]
        token = refs[-1]
        cps = _chip_copies(ins, lands, send_sems, recv_sems)

        @pl.when(lax.axis_index("c") == layer)
        def _():
            for cp in cps:
                cp.start()

        token[...] = jnp.zeros_like(token)

    hbm = lambda a: pltpu.with_memory_space_constraint(a, pltpu.HBM)
    out = _pcall(body, name="chip_send_start_l%d" % layer, in_specs=[HBM] * (2 * n),
                 out_specs=[SEM, SEM] + [HBM] * (2 * n) + [pl.BlockSpec(memory_space=pltpu.VMEM)],
                 out_shape=[pltpu.SemaphoreType.DMA((3 * n,)), pltpu.SemaphoreType.DMA((3 * n,))]
                 + [pltpu.HBM(a.shape, a.dtype) for a in s1] + [pltpu.HBM(a.shape, a.dtype) for a in land]
                 + [jax.ShapeDtypeStruct((8, 128), F32)],
                 input_output_aliases={i: 2 + i for i in range(2 * n)},
                 compiler_params=pltpu.CompilerParams(has_side_effects=EFFECT))(
                     *[hbm(a) for a in s1], *[hbm(a) for a in land])
    return out[0], out[1], out[2:2 + n], out[2 + n:2 + 2 * n], out[-1]


def chip_send_wait(send_sems, recv_sems, srcs, lands, after, layer):
    n = len(srcs)

    def body(*refs):
        ins, lds = refs[:n], refs[n:2 * n]
        s_sems, r_sems = refs[2 * n], refs[2 * n + 1]
        cps = _chip_copies(ins, lds, s_sems, r_sems)

        @pl.when(lax.axis_index("c") == layer)
        def _():
            for cp in cps:
                cp.wait_send()
                cp.wait_recv()

    out = _pcall(body, name="chip_send_wait_l%d" % layer, in_specs=[HBM] * (2 * n) + [SEM, SEM, ANY],
                 out_specs=[HBM] * (2 * n),
                 out_shape=[pltpu.HBM(a.shape, a.dtype) for a in srcs] + [pltpu.HBM(a.shape, a.dtype) for a in lands],
                 input_output_aliases={i: i for i in range(2 * n)},
                 compiler_params=pltpu.CompilerParams(has_side_effects=EFFECT))(
                     *srcs, *lands, send_sems, recv_sems, after)
    return out[n:]


def pair_share(s2):
    n = len(s2)

    def body(*refs):
        ins, outs = refs[:n], refs[n:2 * n]
        send_sems, recv_sems = refs[2 * n:]
        x, y, c = _place()
        sib = (x, y, 1 - c)
        cps = [_remote(ins[t].at[c], outs[t].at[c], send_sems, recv_sems, t, sib) for t in range(n)]
        for cp in cps:
            cp.start()
        for t in range(n):
            cps[t].wait_send()
            _remote(ins[t].at[c], outs[t].at[1 - c], send_sems, recv_sems, t, sib).wait_recv()

    return _pcall(body, name="pair_share", in_specs=[ANY] * n, out_specs=[ANY] * n,
                  out_shape=[jax.ShapeDtypeStruct(a.shape, a.dtype) for a in s2],
                  input_output_aliases={t: t for t in range(n)},
                  scratch_shapes=[pltpu.SemaphoreType.DMA((n,)), pltpu.SemaphoreType.DMA((n,))])(*s2)


def small_exchange(gs):
    rows, width = gs.shape

    def body(g_ref, o_ref, send_sems, recv_sems):
        x, y, c = _place()
        cps = []
        for r in range(1, 8):
            dx, dy, dc = (r >> 2) & 1, (r >> 1) & 1, r & 1
            to = (x if dx == 0 else 1 - x, y if dy == 0 else 1 - y, c if dc == 0 else 1 - c)
            cps.append(_remote(g_ref, o_ref.at[r - 1], send_sems, recv_sems, r - 1, to))
        for cp in cps:
            cp.start()
        for cp in cps:
            cp.wait()

    return _pcall(body, name="small_exchange", in_specs=[ANY], out_specs=ANY,
                  out_shape=jax.ShapeDtypeStruct((7, rows, width), gs.dtype),
                  scratch_shapes=[pltpu.SemaphoreType.DMA((7,)), pltpu.SemaphoreType.DMA((7,))])(gs)


def _row_tile(rows):
    return _pick(rows, (256, 352, 128, 64, 32, 16))


def pair_add_layer(g, rb, *, name):
    _, rows, width = g.shape
    tr = _row_tile(rows)

    def body(g_ref, r_ref, o_ref, ob_ref):
        s = g_ref[...] + r_ref[...]
        o_ref[...] = s
        ob_ref[...] = s.astype(BF16)

    blk = pl.BlockSpec((None, tr, width), lambda k, i: (k, i, 0))
    return _pcall(body, name=name, grid=(4, rows // tr), in_specs=[blk, blk], out_specs=[blk, blk],
                  out_shape=[jax.ShapeDtypeStruct(g.shape, F32), jax.ShapeDtypeStruct(g.shape, BF16)],
                  compiler_params=_cp("parallel", "parallel"))(g, rb)


def chip_add_layers(s1, rb2, chip, core, *, name):
    _, rows, width = s1[0].shape
    tr = _row_tile(rows)

    def body(k_ref, c_ref, s0_ref, s1_ref, r0_ref, r1_ref, o_ref):
        first = c_ref[0] == 0
        s = jnp.where(first, s0_ref[...], s1_ref[...])
        r = jnp.where(first, r0_ref[...], r1_ref[...]).astype(F32)
        o_ref[...] = ((s + r[0]) + r[1]) + r[2]

    def s_spec(layer):
        return pl.BlockSpec((None, tr, width),
                            lambda i, k_ref, c_ref: (jnp.where(c_ref[0] == layer, k_ref[0], 0),
                                                     jnp.where(c_ref[0] == layer, i, 0), 0))

    def r_spec(layer):
        return pl.BlockSpec((3, tr, width), lambda i, k_ref, c_ref: (0, jnp.where(c_ref[0] == layer, i, 0), 0))

    gs = pltpu.PrefetchScalarGridSpec(
        num_scalar_prefetch=2, grid=(rows // tr,),
        in_specs=[s_spec(0), s_spec(1), r_spec(0), r_spec(1)],
        out_specs=pl.BlockSpec((None, tr, width), lambda i, k_ref, c_ref: (c_ref[0], i, 0)))
    return _pcall(body, name=name, grid_spec=gs, out_shape=jax.ShapeDtypeStruct((DEPTH, rows, width), F32),
                  compiler_params=_cp("parallel"))(chip, core, s1[0], s1[1], rb2[0], rb2[1])


def small_add(gs_own, slots, me):
    rows, width = gs_own.shape
    tr = _pick(rows, (64, 32, 16, 8))

    def body(me_ref, g_ref, s_ref, o_ref):
        me_v = me_ref[0]
        total = None
        for d in range(8):
            rel = jnp.bitwise_xor(me_v, d)
            val = jnp.where(rel == 0, g_ref[...], s_ref[jnp.maximum(rel - 1, 0)])
            total = val if total is None else total + val
        o_ref[...] = total

    gs = pltpu.PrefetchScalarGridSpec(
        num_scalar_prefetch=1, grid=(rows // tr,),
        in_specs=[pl.BlockSpec((tr, width), lambda i, m_ref: (i, 0)),
                  pl.BlockSpec((7, tr, width), lambda i, m_ref: (0, i, 0))],
        out_specs=pl.BlockSpec((tr, width), lambda i, m_ref: (i, 0)))
    return _pcall(body, name="small_add", grid_spec=gs, out_shape=jax.ShapeDtypeStruct((rows, width), F32),
                  compiler_params=_cp("parallel"))(me, gs_own, slots)


SHARDED = (("ffn1_w_up", (D, UPW)), ("ffn1_w_down", (DFF // 4, D)), ("w_in", (D, D_IN // 4)),
           ("conv_w", (4, BW // 4)), ("gla_w_g2", (LOW_W, 64)), ("w_branch", (3 * BW, D // 4)),
           ("w_out", (D // 4, D)), ("ffn2_w_up", (D, UPW)), ("ffn2_w_down", (DFF // 4, D)),
           ("ple_w_proj", (PLE, D // 4)), ("ple_w_gate", (D // 4, D)))
SMALL = ("ln1_g", "ln1_b", "conv_b", "lru_wa", "lru_ba", "lru_wx", "lru_bx", "lru_lambda", "gla_b_g",
         "gla_norm_g", "fox_b_f", "ln2_g", "ln2_b", "ln3_g", "ln3_b", "ple_b_gate", "ln4_g", "ln4_b")
WEIGHTS = ('ffn1_w_up', 'ffn1_w_down', 'ln1_g', 'ln1_b', 'w_in', 'conv_w', 'conv_b', 'lru_wa', 'lru_ba',
           'lru_wx', 'lru_bx', 'lru_lambda', 'gla_w_g2', 'gla_b_g', 'gla_norm_g', 'fox_b_f', 'w_branch',
           'w_out', 'ln2_g', 'ln2_b', 'ffn2_w_up', 'ffn2_w_down', 'ln3_g', 'ln3_b', 'ple_w_proj',
           'ple_w_gate', 'ple_b_gate', 'ln4_g', 'ln4_b')


def _cols_join(parts):
    return jnp.concatenate([parts[k] for k in range(4)], axis=-1)


def _cols_split(full):
    r, c4 = full.shape
    return full.reshape(r, 4, c4 // 4).transpose(1, 0, 2)


def _regroup_in(w):
    pad = jnp.zeros(w.shape[:-1] + (ZW - D_IN,), w.dtype)
    fox_q = (w[..., 2576:3088] * FOX_SCALE).astype(w.dtype)
    return jnp.concatenate([w[..., 0:2048], w[..., 2064:2576], fox_q, w[..., 3088:4112], w[..., 4120:7192],
                            w[..., 2048:2064], w[..., 4112:4120], pad], axis=-1)


_IN_RUNS = ((0, 2048, 0, 1.0), (2048, 2064, 7168, 1.0), (2064, 2576, 2048, 1.0), (2576, 3088, CQ, FOX_SCALE),
            (3088, 4112, CK, 1.0), (4112, 4120, 7184, 1.0), (4120, D_IN, 4096, 1.0))


def _regroup_out_shards(g):
    w = D_IN // 4
    shards = []
    for k in range(4):
        pieces = []
        for a, b, new, f in _IN_RUNS:
            lo, hi = max(a, k * w), min(b, (k + 1) * w)
            if lo < hi:
                piece = g[:, new + lo - a:new + hi - a]
                pieces.append(piece if f == 1.0 else piece * f)
        shards.append(jnp.concatenate(pieces, axis=1))
    return jnp.stack(shards)


def _block_diag(w):
    eye = jnp.eye(8, dtype=w.dtype)
    return (eye[:, None, :, None] * w[:, :, None, :]).reshape(BW, BW)


def _diag_blocks(dense):
    return jnp.stack([dense[64 * n:64 * (n + 1), 64 * n:64 * (n + 1)] for n in range(8)])


def _layer_weights(gw, small, l):
    w = {"up1": gw["ffn1_w_up"], "up2": gw["ffn2_w_up"],
         "dn1": gw["ffn1_w_down"].reshape(DEPTH, DFF, D), "dn2": gw["ffn2_w_down"].reshape(DEPTH, DFF, D),
         "wo": gw["w_out"].reshape(DEPTH, D, D), "wgt": gw["ple_w_gate"].reshape(DEPTH, D, D)}
    w["win"] = _regroup_in(_cols_join(gw["w_in"][l]))
    w["cw"] = _cols_join(gw["conv_w"][l])
    w["wa"] = _block_diag(small["lru_wa"][l]).astype(BF16)
    w["wx"] = _block_diag(small["lru_wx"][l]).astype(BF16)
    w["wg2p"] = jnp.pad(_cols_join(gw["gla_w_g2"][l]), ((0, 128 - LOW_W), (0, 0)))
    w["wbr"] = _cols_join(gw["w_branch"][l].reshape(4, 3, BW, D // 4))
    w["wp"] = _cols_join(gw["ple_w_proj"][l])
    for n in ("ln1_g", "ln1_b", "ln2_g", "ln2_b", "ln3_g", "ln3_b", "ln4_g", "ln4_b", "conv_b", "lru_ba",
              "lru_bx", "lru_lambda", "gla_b_g", "gla_norm_g", "ple_b_gate"):
        w[n] = small[n][l][None, :]
    w["bfp"] = jnp.pad(small["fox_b_f"][l], (LOW_W, 128 - LOW_W - FOX_H))[None, :]
    return w


def _heads_t(a):
    ht = a[:, LOW_W:LOW_W + FOX_H].T
    return ht[:, None, :], jnp.broadcast_to(ht[:, :, None], ht.shape + (128,))


def _layer_fwd(x, xb, pb, w, l):
    s = {"x0": x, "x0b": xb}
    tag = "l%d_" % l
    gate, up, act = ffn_up(xb, w["up1"], l, name=tag + "ffn1_up")
    r1, x1, x1b = matmul_res_ln(act, w["dn1"], l, x, w["ln1_g"], w["ln1_b"], mm_scale=0.5, name=tag + "ffn1_down")
    s.update(gate1=gate, up1=up, act1=act, r1=r1, x1=x1, x1b=x1b)
    z, zb = matmul(x1b, w["win"], also_bf16=True, tm=1024, tn=_pick(ZW, (2432,)), name=tag + "mix_in")
    xc, xcb, h, ya = lru_fwd(z, w["cw"], w["conv_b"], w["wa"], w["wx"], w["lru_ba"], w["lru_bx"],
                             w["lru_lambda"], name=tag + "lru_fwd")
    yb, states = gla_fwd(z, zb, w["wg2p"], w["gla_b_g"], w["gla_norm_g"], name=tag + "gla_fwd")
    fcum = fox_fcum(z, w["bfp"], name=tag + "fox_fcum")
    fq, fk = _heads_t(fcum)
    yc, ycf, lse = fox_fwd_t(zb, fq, fk, name=tag + "fox_fwd")
    merged = merge_fwd(ya, yb, yc, w["wbr"], z, name=tag + "merge_fwd")
    r2, x2, x2b = matmul_res_ln(merged, w["wo"], l, x1, w["ln2_g"], w["ln2_b"], mm_scale=1.0, name=tag + "mix_out")
    s.update(z=z, zb=zb, xc=xc, xcb=xcb, h=h, ya=ya, yb=yb, states=states, fq=fq, fk=fk, yc=yc, ycf=ycf,
             lse=lse, merged=merged, r2=r2, x2=x2, x2b=x2b)
    gate, up, act = ffn_up(x2b, w["up2"], l, name=tag + "ffn2_up")
    r3, x3, x3b = matmul_res_ln(act, w["dn2"], l, x2, w["ln3_g"], w["ln3_b"], mm_scale=0.5, name=tag + "ffn2_down")
    s.update(gate2=gate, up2=up, act2=act, r3=r3, x3=x3, x3b=x3b)
    r4, x4, x4b = ple_fwd(x3b, x3, pb, w["wgt"], l, w["wp"], w["ple_b_gate"], w["ln4_g"], w["ln4_b"],
                          name=tag + "ple_fwd")
    s.update(r4=r4, pb=pb)
    return x4, x4b, s


def _ffn_bwd(dy, s, w, n, xin_b, l, tag):
    k = {"1": ("r1", "ln1_g", "gate1", "up1", "act1"), "2": ("r3", "ln3_g", "gate2", "up2", "act2")}[n]
    dr, dfb, dg, db = ln_bwd(dy, s[k[0]], w[k[1]], out_scale=0.5, name=tag + "ln_bwd")
    dgate, dup = ffn_down_bwd(dfb, w["dn" + n], l, s[k[2]], s[k[3]], name=tag + "down_bwd")
    dx = ffn_dx(dgate, dup, w["up" + n], l, dr, name=tag + "dx")
    dwup = matmul_tn_up(xin_b, dgate, dup, name=tag + "dw_up")
    dwdn = matmul_tn(s[k[4]], dfb, name=tag + "dw_down").reshape(4, DFF // 4, D)
    return dx, dwup, dwdn, dg[0], db[0]


def _layer_bwd(dy, s, w, l):
    g = {}
    tag = "l%d_" % l
    dr4, dglb, dpeb, dg4, db4, dbg = ple_bwd(dy, s["r4"], s["x3b"], s["pb"], w["wgt"], l, w["wp"], w["ple_b_gate"],
                                             w["ln4_g"], name=tag + "ple_bwd")
    dx3 = matmul(dglb, w["wgt"], nt=True, b_lead=(l,), res=dr4, res_scale=ALPHA, tm=1024, tn=1024,
                 name=tag + "ple_dx")
    g["ple_w_gate"] = matmul_tn(s["x3b"], dglb, name=tag + "ple_dw_gate").reshape(4, D // 4, D)
    g["ple_w_proj"] = _cols_split(matmul_tn(s["pb"], dpeb, name=tag + "ple_dw_proj"))
    g["ln4_g"], g["ln4_b"], g["ple_b_gate"] = dg4[0], db4[0], dbg[0]
    dx2, g["ffn2_w_up"], g["ffn2_w_down"], g["ln3_g"], g["ln3_b"] = _ffn_bwd(dx3, s, w, "2", s["x2b"], l,
                                                                             tag + "ffn2_")
    dr2, doutb, dg2, db2 = ln_bwd(dx2, s["r2"], w["ln2_g"], out_scale=1.0, name=tag + "mix_ln_bwd")
    g["ln2_g"], g["ln2_b"] = dg2[0], db2[0]
    g["w_out"] = matmul_tn(s["merged"], doutb, name=tag + "dw_out").reshape(4, D // 4, D)
    z, zb = s["z"], s["zb"]
    (dya, dyb, dyc, dp0, dp1, dp2, dgl0, dgl1, dgl2) = merge_bwd(
        doutb, w["wo"], l, s["ya"], s["yb"], s["yc"], w["wbr"], z, name=tag + "merge_bwd")
    dwbr = jnp.stack([matmul_tn(s["ya"], dp0, name=tag + "dw_br0"), matmul_tn(s["yb"], dp1, name=tag + "dw_br1"),
                      matmul_tn(s["yc"], dp2, name=tag + "dw_br2")])
    g["w_branch"] = _cols_split(dwbr.reshape(3 * BW, D))
    day, dxc, dprb, dpib, dba, dbx, dlam = lru_bwd(dya, z, s["h"], s["xc"], w["wa"], w["wx"],
                                                   w["lru_ba"], w["lru_bx"], w["lru_lambda"], name=tag + "lru_bwd")
    dax, dcw, dcb = conv_bwd(dxc, z, w["cw"], name=tag + "conv_bwd")
    g["lru_wa"] = _diag_blocks(matmul_tn(s["xcb"], dprb, name=tag + "dw_lru_a"))
    g["lru_wx"] = _diag_blocks(matmul_tn(s["xcb"], dpib, name=tag + "dw_lru_x"))
    g["lru_ba"], g["lru_bx"], g["lru_lambda"] = dba[0], dbx[0], dlam[0]
    g["conv_w"], g["conv_b"] = _cols_split(dcw), dcb[0]
    dbq, dbk, dbv, dbr, dmisc_g, dpreb, dbgg, dng = gla_bwd(dyb, z, zb, s["states"], w["wg2p"], w["gla_b_g"],
                                                            w["gla_norm_g"], name=tag + "gla_bwd")
    miscb = zb[:, MISC:]
    g["gla_w_g2"] = _cols_split(matmul_tn(miscb, dpreb, name=tag + "dw_g2")[:LOW_W])
    g["gla_b_g"], g["gla_norm_g"] = dbgg[0], dng[0]
    dl = fox_delta(dyc, s["ycf"], name=tag + "fox_delta")
    t = z.shape[0]
    dlq = dl[:, :FOX_H].T[:, None, :]
    dcq, dck, dcv, dfk, dfq = fox_bwd_t(zb, dyc, s["fq"], s["fk"], s["lse"], dlq, name=tag + "fox_bwd")
    dfc = jnp.pad((dfk[:, 0, :] + dfq.reshape(FOX_H, t)).T, ((0, 0), (LOW_W, 128 - LOW_W - FOX_H)))
    dmiscb, dbf = fox_dcf(dfc, z, w["bfp"], dmisc_g, name=tag + "fox_dcf")
    g["fox_b_f"] = dbf[0, LOW_W:LOW_W + FOX_H]
    dz = jnp.concatenate([dax, day, dbq, dbk, dbv, dbr, dcq, dck, dcv, dgl0, dgl1, dgl2, dmiscb], axis=1)
    dx1 = matmul(dz, w["win"], nt=True, res=dr2, res_scale=ALPHA, tm=1024, tn=1024, tk=_pick(ZW, (2432,)),
                 name=tag + "mix_dx")
    g["w_in"] = _regroup_out_shards(matmul_tn(s["x1b"], dz, name=tag + "dw_in"))
    dx0, g["ffn1_w_up"], g["ffn1_w_down"], g["ln1_g"], g["ln1_b"] = _ffn_bwd(dx1, s, w, "1", s["x0b"], l,
                                                                             tag + "ffn1_")
    return dx0, g


def _local_step(x, p, target, gw, small, after_last_layer=None):
    xcur = x
    xb = xcur.astype(BF16)
    layer_w, saved = [], []
    for l in range(DEPTH):
        w = _layer_weights(gw, small, l)
        xcur, xb, s = _layer_fwd(xcur, xb, p[l].astype(BF16), w, l)
        layer_w.append(w)
        saved.append(s)
    dy, sq = loss_head(xcur, target, name="loss_head")
    grads = [None] * DEPTH
    for l in reversed(range(DEPTH)):
        dy, grads[l] = _layer_bwd(dy, saved[l], layer_w[l], l)
        if l == DEPTH - 1 and after_last_layer is not None:
            layer_w[l - 1]["ln4_g"] = layer_w[l - 1]["ln4_g"] + after_last_layer(grads[l])
    return 0.5 * jnp.sum(sq) / float(D), dy, grads


def kernel(x, p, ffn1_w_up, ffn1_w_down, ln1_g, ln1_b, w_in, conv_w, conv_b, lru_wa, lru_ba, lru_wx, lru_bx, lru_lambda, gla_w_g2, gla_b_g, gla_norm_g, fox_b_f, w_branch, w_out, ln2_g, ln2_b, ffn2_w_up, ffn2_w_down, ln3_g, ln3_b, ple_w_proj, ple_w_gate, ple_b_gate, ln4_g, ln4_b, loss_target, m_ffn1_w_up, m_ffn1_w_down, m_ln1_g, m_ln1_b, m_w_in, m_conv_w, m_conv_b, m_lru_wa, m_lru_ba, m_lru_wx, m_lru_bx, m_lru_lambda, m_gla_w_g2, m_gla_b_g, m_gla_norm_g, m_fox_b_f, m_w_branch, m_w_out, m_ln2_g, m_ln2_b, m_ffn2_w_up, m_ffn2_w_down, m_ln3_g, m_ln3_b, m_ple_w_proj, m_ple_w_gate, m_ple_b_gate, m_ln4_g, m_ln4_b, v_ffn1_w_up, v_ffn1_w_down, v_ln1_g, v_ln1_b, v_w_in, v_conv_w, v_conv_b, v_lru_wa, v_lru_ba, v_lru_wx, v_lru_bx, v_lru_lambda, v_gla_w_g2, v_gla_b_g, v_gla_norm_g, v_fox_b_f, v_w_branch, v_w_out, v_ln2_g, v_ln2_b, v_ffn2_w_up, v_ffn2_w_down, v_ln3_g, v_ln3_b, v_ple_w_proj, v_ple_w_gate, v_ple_b_gate, v_ln4_g, v_ln4_b):
    args = dict(locals())
    wts = {n: args[n] for n in WEIGHTS}
    mom = {n: args["m_" + n] for n in WEIGHTS}
    var = {n: args["v_" + n] for n in WEIGHTS}
    cx, cy, cc = lax.axis_index("x"), lax.axis_index("y"), lax.axis_index("c")

    shards = [wts[n].reshape((DEPTH,) + rc).astype(F32 if n == "conv_w" else BF16) for n, rc in SHARDED]
    gw = dict(zip([n for n, _ in SHARDED], gather_weights(shards)))
    small = {n: wts[n] for n in SMALL}

    names = [n for n, _ in SHARDED]
    flight = {}

    def chip_sum(gl, layer):
        lst = [gl[n] for n in names]
        rb = pair_send(lst, layer)
        return [pair_add_layer(a, r, name="pair_add_l%d_%s" % (layer, n)) for n, a, r in zip(names, lst, rb)]

    def start_last_layer(gl):
        s1 = chip_sum(gl, DEPTH - 1)
        send_sems, recv_sems, srcs, lands, token = chip_send_start([sb for _, sb in s1], DEPTH - 1)
        flight.update(s1=[sf for sf, _ in s1], sems=(send_sems, recv_sems), srcs=srcs, lands=lands)
        return token[0, 0]

    loss_local, dx, grads = _local_step(x[0], p[:, 0], loss_target[0], gw, small, start_last_layer)
    loss = lax.psum(loss_local, ("x", "y", "c"))
    grad_x = dx[None]

    core = jnp.reshape(cc, (1,)).astype(jnp.int32)
    chip = jnp.reshape(2 * cx + cy, (1,)).astype(jnp.int32)
    s1_first = chip_sum(grads[0], 0)
    rb2_first = chip_send([sb for _, sb in s1_first], 0)
    rb2_last = chip_send_wait(*flight["sems"], flight["srcs"], flight["lands"], dx, DEPTH - 1)
    s2 = [chip_add_layers((sf0, sf1), (r0, r1), chip, core, name="chip_add_" + n)
          for n, (sf0, _), sf1, r0, r1 in zip(names, s1_first, flight["s1"], rb2_first, rb2_last)]
    gsh = dict(zip(names, pair_share(s2)))

    pieces, spans, row = [], {}, 0
    for n in SMALL:
        flat = jnp.stack([grads[l][n] for l in range(DEPTH)]).reshape(-1)
        rows = -(-flat.shape[0] // (8 * PACK_W)) * 8
        pieces.append(jnp.pad(flat, (0, rows * PACK_W - flat.shape[0])).reshape(rows, PACK_W))
        spans[n] = (row, rows)
        row += rows
    gs = jnp.concatenate(pieces, axis=0)
    me = jnp.reshape(4 * cx + 2 * cy + cc, (1,)).astype(jnp.int32)
    gsum = small_add(gs, small_exchange(gs), me)

    gout, delta, new_m, new_v = {}, {}, {}, {}
    for n in WEIGHTS:
        shp = wts[n].shape
        if n in gsh:
            view = gsh[n].shape
            g = gsh[n]
        else:
            view = (1, DEPTH, wts[n].size // DEPTH)
            r0, rows = spans[n]
            g = gsum[r0:r0 + rows].reshape(-1)[:wts[n].size].reshape(view)
        d, mn, vn = adamw(wts[n].reshape(view), g, mom[n].reshape(view), var[n].reshape(view), name="adamw_" + n)
        gout[n], delta[n], new_m[n], new_v[n] = g.reshape(shp), d.reshape(shp), mn.reshape(shp), vn.reshape(shp)

    return (loss, grad_x, *[gout[n] for n in WEIGHTS], *[delta[n] for n in WEIGHTS],
            *[new_m[n] for n in WEIGHTS], *[new_v[n] for n in WEIGHTS])
```

```python
import functools
import math

import jax
import jax.numpy as jnp
from jax import lax
from jax.experimental import pallas as pl
from jax.experimental.pallas import tpu as pltpu

F32 = jnp.float32
BF16 = jnp.bfloat16

D = 1024
DFF = 2816
BW = 512
PLE = 256
DEPTH = 2
ALPHA = (2 * DEPTH) ** 0.25
LN_EPS = 1e-5
RMS_EPS = 1e-6
LRU_C = 8.0
GLA_TAU = 16.0
CHUNK = 64
D_IN = 7192
ZW = 7296
AX, AY, BQ, BK, BV, BR, CQ, CK, CV, G0, MISC = 0, 512, 1024, 1280, 1536, 2048, 2560, 3072, 3584, 4096, 7168
LOW_W, FOX_H = 16, 8
ADAM_LR, ADAM_B1, ADAM_B2, ADAM_EPS, ADAM_WD, ADAM_STEP = 0.001, 0.9, 0.999, 1e-08, 0.01, 10
PACK_W = 1024
VMEM_LIMIT = 56 << 20

MESH = pl.DeviceIdType.MESH
ANY = pl.BlockSpec(memory_space=pl.ANY)


def _pcall(body, **kw):
    return pl.pallas_call(body, **kw)


def _cp(*dims):
    return pltpu.CompilerParams(dimension_semantics=dims, vmem_limit_bytes=VMEM_LIMIT)


def _dot(a, b):
    return jnp.dot(a, b, preferred_element_type=F32)


def _dot_nt(a, b):
    return lax.dot_general(a, b, (((1,), (1,)), ((), ())), preferred_element_type=F32)


def _dot_tn(a, b):
    return lax.dot_general(a, b, (((0,), (0,)), ((), ())), preferred_element_type=F32)


def _dot_hi(a, b):
    return jnp.dot(a, b, preferred_element_type=F32, precision=lax.Precision.HIGHEST)


def _sigmoid(x):
    return 1.0 / (1.0 + jnp.exp(-x))


def _softplus(x):
    return jnp.maximum(x, 0.0) + jnp.log(1.0 + jnp.exp(-jnp.abs(x)))


def _log_sigmoid(x):
    return -_softplus(-x)


def _expm1(x):
    poly = x * (1.0 + x * (0.5 + x * (1.0 / 6.0 + x * (1.0 / 24.0 + x * (1.0 / 120.0 + x * (1.0 / 720.0))))))
    return jnp.where(jnp.abs(x) < 0.1, poly, jnp.exp(x) - 1.0)


_GELU_C = math.sqrt(2.0 / math.pi)


def _gelu(x):
    return 0.5 * x * (1.0 + jnp.tanh(_GELU_C * (x + 0.044715 * x * x * x)))


def _gelu_grad(x):
    t = jnp.tanh(_GELU_C * (x + 0.044715 * x * x * x))
    return 0.5 * (1.0 + t) + 0.5 * x * (1.0 - t * t) * _GELU_C * (1.0 + 3.0 * 0.044715 * x * x)


def _ln_stats(r):
    mu = jnp.mean(r, axis=-1, keepdims=True)
    xc = r - mu
    var = jnp.mean(xc * xc, axis=-1, keepdims=True)
    return xc, lax.rsqrt(var + LN_EPS)


def _pick(n, cands):
    for c in cands:
        if n % c == 0:
            return c
    return n


def _rows(tm, w, col=0):
    return pl.BlockSpec((tm, w), lambda i: (i, col))


def _fix(shape):
    nd = len(shape)
    return pl.BlockSpec(shape, lambda i: (0,) * nd)


def _col_chunks(n, width=256):
    return [slice(c, min(c + width, n)) for c in range(0, n, width)]


def _layer(l, shape):
    nd = len(shape)
    return pl.BlockSpec((None,) + tuple(shape), lambda i: (l,) + (0,) * nd)


def matmul(a, b, *, name, nt=False, b_lead=(), res=None, res_scale=1.0, also_bf16=False, tm=512, tn=512,
           tk=None):
    m, k = a.shape
    n = b.shape[-2] if nt else b.shape[-1]
    tm, tn = min(tm, m), min(tn, n)
    tk = k if tk is None else tk
    nk = k // tk
    has_res = res is not None
    lead = tuple(b_lead)
    dot = _dot_nt if nt else _dot

    def body(*refs):
        a_ref, b_ref = refs[0], refs[1]
        pos = 2
        r_ref = None
        if has_res:
            r_ref = refs[pos]
            pos += 1
        o_ref = refs[pos]
        pos += 1
        ob_ref = None
        if also_bf16:
            ob_ref = refs[pos]
            pos += 1

        def finish(v):
            if has_res:
                v = v + res_scale * r_ref[...]
            o_ref[...] = v
            if also_bf16:
                ob_ref[...] = v.astype(BF16)

        if nk == 1:
            finish(dot(a_ref[...], b_ref[...]))
            return
        acc = refs[pos]
        kk = pl.program_id(2)

        @pl.when(kk == 0)
        def _():
            acc[...] = jnp.zeros_like(acc)

        acc[...] += dot(a_ref[...], b_ref[...])

        @pl.when(kk == nk - 1)
        def _():
            finish(acc[...])

    none = (None,) * len(lead)
    if nt:
        b_spec = pl.BlockSpec(none + (tn, tk), lambda j, i, kk: lead + (j, kk))
    else:
        b_spec = pl.BlockSpec(none + (tk, tn), lambda j, i, kk: lead + (kk, j))
    in_specs = [pl.BlockSpec((tm, tk), lambda j, i, kk: (i, kk)), b_spec]
    args = [a, b]
    if has_res:
        in_specs.append(pl.BlockSpec((tm, tn), lambda j, i, kk: (i, j)))
        args.append(res)
    out_shape = [jax.ShapeDtypeStruct((m, n), F32)]
    out_specs = [pl.BlockSpec((tm, tn), lambda j, i, kk: (i, j))]
    if also_bf16:
        out_shape.append(jax.ShapeDtypeStruct((m, n), BF16))
        out_specs.append(pl.BlockSpec((tm, tn), lambda j, i, kk: (i, j)))
    out = _pcall(body, name=name, grid=(n // tn, m // tm, nk), in_specs=in_specs, out_specs=out_specs,
                 out_shape=out_shape, scratch_shapes=[pltpu.VMEM((tm, tn), F32)] if nk > 1 else [],
                 compiler_params=_cp("parallel", "parallel", "arbitrary"))(*args)
    return out if also_bf16 else out[0]


def matmul_tn(a, b, *, name):
    t, k = a.shape
    n = b.shape[1]
    tk = _pick(k, (1024, 1408, 512, 256, 128))
    tn = _pick(n, (1024, 1408, 2432, 512, 256, 128))
    tt = min(1024 if tk * tn > (1 << 20) else 2048, t)
    nt = t // tt

    def body(a_ref, b_ref, o_ref):
        @pl.when(pl.program_id(2) == 0)
        def _():
            o_ref[...] = jnp.zeros_like(o_ref)

        o_ref[...] += _dot_tn(a_ref[...], b_ref[...])

    return _pcall(body, name=name, grid=(k // tk, n // tn, nt),
                  in_specs=[pl.BlockSpec((tt, tk), lambda i, j, s: (s, i)),
                            pl.BlockSpec((tt, tn), lambda i, j, s: (s, j))],
                  out_specs=pl.BlockSpec((tk, tn), lambda i, j, s: (i, j)),
                  out_shape=jax.ShapeDtypeStruct((k, n), F32),
                  compiler_params=_cp("parallel", "parallel", "arbitrary"))(a, b)


UPW = 1408


def matmul_tn_up(a, dgate, dup, *, name):
    t, k = a.shape
    tt = min(1024, t)
    tk = 1024

    def body(a_ref, g_ref, u_ref, o_ref):
        j = pl.program_id(1)

        @pl.when(pl.program_id(2) == 0)
        def _():
            o_ref[...] = jnp.zeros_like(o_ref)

        @pl.when(j < 2)
        def _():
            o_ref[...] += _dot_tn(a_ref[...], g_ref[...])

        @pl.when(j >= 2)
        def _():
            o_ref[...] += _dot_tn(a_ref[...], u_ref[...])

    return _pcall(body, name=name, grid=(k // tk, 4, t // tt),
                  in_specs=[pl.BlockSpec((tt, tk), lambda i, j, s: (s, i)),
                            pl.BlockSpec((tt, UPW), lambda i, j, s: (jnp.where(j < 2, s, 0), jnp.minimum(j, 1))),
                            pl.BlockSpec((tt, UPW), lambda i, j, s: (jnp.where(j >= 2, s, 0), jnp.maximum(j - 2, 0)))],
                  out_specs=pl.BlockSpec((None, tk, UPW), lambda i, j, s: (j, i, 0)),
                  out_shape=jax.ShapeDtypeStruct((4, k, UPW), F32),
                  compiler_params=_cp("parallel", "parallel", "arbitrary"))(a, dgate, dup)


def ffn_dx(dgate, dup, wup, l, res, *, name):
    t = dgate.shape[0]
    tm, tn = min(1024, t), 1024

    def body(g_ref, u_ref, w_ref, r_ref, o_ref, acc):
        kk = pl.program_id(2)

        @pl.when(kk == 0)
        def _():
            acc[...] = jnp.zeros_like(acc)

        @pl.when(kk < 2)
        def _():
            acc[...] += _dot_nt(g_ref[...], w_ref[...])

        @pl.when(kk >= 2)
        def _():
            acc[...] += _dot_nt(u_ref[...], w_ref[...])

        @pl.when(kk == 3)
        def _():
            o_ref[...] = acc[...] + ALPHA * r_ref[...]

    return _pcall(body, name=name, grid=(D // tn, t // tm, 4),
                  in_specs=[pl.BlockSpec((tm, UPW), lambda j, i, kk: (i, jnp.minimum(kk, 1))),
                            pl.BlockSpec((tm, UPW), lambda j, i, kk: (i, jnp.maximum(kk - 2, 0))),
                            pl.BlockSpec((None, None, tn, UPW), lambda j, i, kk: (l, kk, j, 0)),
                            pl.BlockSpec((tm, tn), lambda j, i, kk: (i, j))],
                  out_specs=pl.BlockSpec((tm, tn), lambda j, i, kk: (i, j)),
                  out_shape=jax.ShapeDtypeStruct((t, D), F32),
                  scratch_shapes=[pltpu.VMEM((tm, tn), F32)],
                  compiler_params=_cp("parallel", "parallel", "arbitrary"))(dgate, dup, wup, res)


def ffn_up(xb, wup, l, *, name):
    t = xb.shape[0]
    tm, tn = min(1024, t), UPW

    def body(x_ref, wg_ref, wu_ref, g_ref, u_ref, a_ref):
        x = x_ref[...]
        for cols in _col_chunks(tn):
            g = _dot(x, wg_ref[:, cols])
            u = _dot(x, wu_ref[:, cols])
            g_ref[:, cols] = g.astype(BF16)
            u_ref[:, cols] = u.astype(BF16)
            a_ref[:, cols] = (g * _sigmoid(g) * u).astype(BF16)

    blk = pl.BlockSpec((tm, tn), lambda j, i: (i, j))
    return _pcall(body, name=name, grid=(DFF // tn, t // tm),
                  in_specs=[pl.BlockSpec((tm, D), lambda j, i: (i, 0)),
                            pl.BlockSpec((None, None, D, tn), lambda j, i: (l, j, 0, 0)),
                            pl.BlockSpec((None, None, D, tn), lambda j, i: (l, 2 + j, 0, 0))],
                  out_specs=[blk, blk, blk],
                  out_shape=[jax.ShapeDtypeStruct((t, DFF), BF16)] * 3,
                  compiler_params=_cp("parallel", "parallel"))(xb, wup, wup)


def matmul_res_ln(a, w, l, res, g, b, *, mm_scale, name):
    t, k = a.shape
    tm = min(512, t)

    def body(a_ref, w_ref, res_ref, g_ref, b_ref, r_ref, y_ref, yb_ref):
        f = _dot(a_ref[...], w_ref[...])
        r = ALPHA * res_ref[...] + mm_scale * f
        xc, rstd = _ln_stats(r)
        y = xc * rstd * g_ref[...] + b_ref[...]
        r_ref[...] = r
        y_ref[...] = y
        yb_ref[...] = y.astype(BF16)

    return _pcall(body, name=name, grid=(t // tm,),
                  in_specs=[_rows(tm, k), _layer(l, (k, D)), _rows(tm, D), _fix((1, D)), _fix((1, D))],
                  out_specs=[_rows(tm, D)] * 3,
                  out_shape=[jax.ShapeDtypeStruct((t, D), F32), jax.ShapeDtypeStruct((t, D), F32),
                             jax.ShapeDtypeStruct((t, D), BF16)],
                  compiler_params=_cp("parallel"))(a, w, res, g, b)


def ln_bwd(dy, r, g, *, out_scale, name):
    t = dy.shape[0]
    tm = min(256, t)

    def body(dy_ref, r_ref, g_ref, dr_ref, drb_ref, dg_ref, db_ref):
        @pl.when(pl.program_id(0) == 0)
        def _():
            dg_ref[...] = jnp.zeros_like(dg_ref)
            db_ref[...] = jnp.zeros_like(db_ref)

        xc, rstd = _ln_stats(r_ref[...])
        xhat = xc * rstd
        d = dy_ref[...]
        dxh = d * g_ref[...]
        dr = rstd * (dxh - jnp.mean(dxh, axis=-1, keepdims=True)
                     - xhat * jnp.mean(dxh * xhat, axis=-1, keepdims=True))
        dr_ref[...] = dr
        drb_ref[...] = (out_scale * dr).astype(BF16)
        dg_ref[...] += jnp.sum(d * xhat, axis=0, keepdims=True)
        db_ref[...] += jnp.sum(d, axis=0, keepdims=True)

    return _pcall(body, name=name, grid=(t // tm,),
                  in_specs=[_rows(tm, D), _rows(tm, D), _fix((1, D))],
                  out_specs=[_rows(tm, D), _rows(tm, D), _fix((1, D)), _fix((1, D))],
                  out_shape=[jax.ShapeDtypeStruct((t, D), F32), jax.ShapeDtypeStruct((t, D), BF16),
                             jax.ShapeDtypeStruct((1, D), F32), jax.ShapeDtypeStruct((1, D), F32)],
                  compiler_params=_cp("arbitrary"))(dy, r, g)


def ffn_down_bwd(dfb, wd, l, gate, up, *, name):
    t = dfb.shape[0]
    tm, tn = min(1024, t), UPW
    nj = DFF // tn

    def body(df_ref, w_ref, g_ref, u_ref, dg_ref, du_ref):
        df = df_ref[...]
        for cols in _col_chunks(tn):
            da = _dot_nt(df, w_ref[cols, :])
            g = g_ref[:, cols].astype(F32)
            s = _sigmoid(g)
            gs = g * s
            dg_ref[:, cols] = (da * u_ref[:, cols].astype(F32) * (s + gs * (1.0 - s))).astype(BF16)
            du_ref[:, cols] = (da * gs).astype(BF16)

    blk = pl.BlockSpec((tm, tn), lambda j, i: (i, j))
    return _pcall(body, name=name, grid=(nj, t // tm),
                  in_specs=[pl.BlockSpec((tm, D), lambda j, i: (i, 0)),
                            pl.BlockSpec((None, tn, D), lambda j, i: (l, j, 0)), blk, blk],
                  out_specs=[blk, blk],
                  out_shape=[jax.ShapeDtypeStruct((t, DFF), BF16), jax.ShapeDtypeStruct((t, DFF), BF16)],
                  compiler_params=_cp("parallel", "parallel"))(dfb, wd, gate, up)


def ple_fwd(xb, x, pb, wgate, l, wproj, bgate, g, b, *, name):
    t = x.shape[0]
    tm = min(512, t)

    def body(xb_ref, x_ref, p_ref, wg_ref, wp_ref, bg_ref, g_ref, b_ref, r_ref, y_ref, yb_ref):
        gl = _dot(xb_ref[...], wg_ref[...]) + bg_ref[...]
        pe = _dot(p_ref[...], wp_ref[...])
        r = ALPHA * x_ref[...] + _sigmoid(gl) * pe
        xc, rstd = _ln_stats(r)
        y = xc * rstd * g_ref[...] + b_ref[...]
        r_ref[...] = r
        y_ref[...] = y
        yb_ref[...] = y.astype(BF16)

    return _pcall(body, name=name, grid=(t // tm,),
                  in_specs=[_rows(tm, D), _rows(tm, D), _rows(tm, PLE), _layer(l, (D, D)), _fix((PLE, D)),
                            _fix((1, D)), _fix((1, D)), _fix((1, D))],
                  out_specs=[_rows(tm, D)] * 3,
                  out_shape=[jax.ShapeDtypeStruct((t, D), F32), jax.ShapeDtypeStruct((t, D), F32),
                             jax.ShapeDtypeStruct((t, D), BF16)],
                  compiler_params=_cp("parallel"))(xb, x, pb, wgate, wproj, bgate, g, b)


def ple_bwd(dy, r, xb, pb, wgate, l, wproj, bgate, g, *, name):
    t = dy.shape[0]
    tm = min(512, t)

    def body(dy_ref, r_ref, xb_ref, p_ref, wg_ref, wp_ref, bg_ref, g_ref,
             dr_ref, dgl_ref, dpe_ref, dg_ref, db_ref, dbg_ref):
        @pl.when(pl.program_id(0) == 0)
        def _():
            dg_ref[...] = jnp.zeros_like(dg_ref)
            db_ref[...] = jnp.zeros_like(db_ref)
            dbg_ref[...] = jnp.zeros_like(dbg_ref)

        xc, rstd = _ln_stats(r_ref[...])
        xhat = xc * rstd
        d = dy_ref[...]
        dxh = d * g_ref[...]
        dr = rstd * (dxh - jnp.mean(dxh, axis=-1, keepdims=True)
                     - xhat * jnp.mean(dxh * xhat, axis=-1, keepdims=True))
        s = _sigmoid(_dot(xb_ref[...], wg_ref[...]) + bg_ref[...])
        pe = _dot(p_ref[...], wp_ref[...])
        dgl = dr * pe * s * (1.0 - s)
        dr_ref[...] = dr
        dgl_ref[...] = dgl.astype(BF16)
        dpe_ref[...] = (dr * s).astype(BF16)
        dg_ref[...] += jnp.sum(d * xhat, axis=0, keepdims=True)
        db_ref[...] += jnp.sum(d, axis=0, keepdims=True)
        dbg_ref[...] += jnp.sum(dgl, axis=0, keepdims=True)

    vec = jax.ShapeDtypeStruct((1, D), F32)
    return _pcall(body, name=name, grid=(t // tm,),
                  in_specs=[_rows(tm, D), _rows(tm, D), _rows(tm, D), _rows(tm, PLE), _layer(l, (D, D)),
                            _fix((PLE, D)), _fix((1, D)), _fix((1, D))],
                  out_specs=[_rows(tm, D), _rows(tm, D), _rows(tm, D), _fix((1, D)), _fix((1, D)), _fix((1, D))],
                  out_shape=[jax.ShapeDtypeStruct((t, D), F32), jax.ShapeDtypeStruct((t, D), BF16),
                             jax.ShapeDtypeStruct((t, D), BF16), vec, vec, vec],
                  compiler_params=_cp("arbitrary"))(dy, r, xb, pb, wgate, wproj, bgate, g)


def loss_head(y, tgt, *, name):
    t = y.shape[0]
    tm = min(256, t)

    def body(y_ref, t_ref, dy_ref, sq_ref):
        @pl.when(pl.program_id(0) == 0)
        def _():
            sq_ref[...] = jnp.zeros_like(sq_ref)

        e = y_ref[...] - t_ref[...]
        dy_ref[...] = e / float(D)
        sq_ref[...] += jnp.sum(e * e, axis=0, keepdims=True)

    return _pcall(body, name=name, grid=(t // tm,),
                  in_specs=[_rows(tm, D), _rows(tm, D)],
                  out_specs=[_rows(tm, D), _fix((1, D))],
                  out_shape=[jax.ShapeDtypeStruct((t, D), F32), jax.ShapeDtypeStruct((1, D), F32)],
                  compiler_params=_cp("arbitrary"))(y, tgt)


def _lru_gates(xc, wa_ref, wx_ref, ba_ref, bx_ref, lam_ref):
    xcb = xc.astype(BF16)
    r = _sigmoid(_dot(xcb, wa_ref[...]) + ba_ref[...])
    ig = _sigmoid(_dot(xcb, wx_ref[...]) + bx_ref[...])
    sp = _softplus(-lam_ref[...])
    la = -LRU_C * r * sp
    a = jnp.exp(la)
    mult = jnp.sqrt(-_expm1(2.0 * la))
    return r, ig, sp, la, a, mult


def lru_fwd(z, cw, cb, wa, wx, ba, bx, lam, *, name):
    t = z.shape[0]
    tm = min(256, t)
    hb = tm // 8

    def body(ax_ref, prev_ref, ay_ref, cw_ref, cb_ref, wa_ref, wx_ref, ba_ref, bx_ref, lam_ref,
             xc_ref, xcb_ref, h_ref, ya_ref, xs, a_s, b_s, hc):
        i = pl.program_id(0)

        @pl.when(i == 0)
        def _():
            hc[...] = jnp.zeros_like(hc)

        xs[0:8, :] = jnp.where(i == 0, 0.0, prev_ref[...])
        xs[8:, :] = ax_ref[...]
        xc = cb_ref[...] + cw_ref[0:1, :] * xs[5:5 + tm, :]
        for k in range(1, 4):
            xc = xc + cw_ref[k:k + 1, :] * xs[5 + k:5 + k + tm, :]
        r, ig, sp, la, a, mult = _lru_gates(xc, wa_ref, wx_ref, ba_ref, bx_ref, lam_ref)
        a_s[...] = a
        b_s[...] = mult * (ig * xc)
        xc_ref[...] = xc
        xcb_ref[...] = xc.astype(BF16)

        def step(g, h):
            base = pl.multiple_of(g * 8, 8)
            a8 = a_s[pl.ds(base, 8), :]
            b8 = b_s[pl.ds(base, 8), :]
            for j in range(8):
                h = a8[j:j + 1, :] * h + b8[j:j + 1, :]
                h_ref[pl.ds(base + j, 1), :] = h
            return h

        hc[...] = lax.fori_loop(0, tm // 8, step, hc[...])
        ya_ref[...] = (_gelu(ay_ref[...]) * h_ref[...]).astype(BF16)

    vec = _fix((1, BW))
    return _pcall(body, name=name, grid=(t // tm,),
                  in_specs=[_rows(tm, BW, AX // BW),
                            pl.BlockSpec((8, BW), lambda i: (jnp.maximum(i * hb - 1, 0), AX // BW)),
                            _rows(tm, BW, AY // BW), _fix((4, BW)), vec, _fix((BW, BW)), _fix((BW, BW)),
                            vec, vec, vec],
                  out_specs=[_rows(tm, BW)] * 4,
                  out_shape=[jax.ShapeDtypeStruct((t, BW), F32), jax.ShapeDtypeStruct((t, BW), BF16),
                             jax.ShapeDtypeStruct((t, BW), F32), jax.ShapeDtypeStruct((t, BW), BF16)],
                  scratch_shapes=[pltpu.VMEM((tm + 8, BW), F32), pltpu.VMEM((tm, BW), F32),
                                  pltpu.VMEM((tm, BW), F32), pltpu.VMEM((1, BW), F32)],
                  compiler_params=_cp("arbitrary"))(z, z, z, cw, cb, wa, wx, ba, bx, lam)


def lru_bwd(dya, z, h, xc, wa, wx, ba, bx, lam, *, name):
    t = dya.shape[0]
    tm = min(256, t)
    nb = t // tm
    hb = tm // 8

    def body(dya_ref, ay_ref, h_ref, hprev_ref, xc_ref, wa_ref, wx_ref, ba_ref, bx_ref,
             lam_ref, day_ref, dxc_ref, dpr_ref, dpi_ref, dba_ref, dbx_ref, dlam_ref,
             hs, a_s, g_s, d_s, cc):
        i = pl.program_id(0)

        @pl.when(i == 0)
        def _():
            cc[...] = jnp.zeros_like(cc)
            dba_ref[...] = jnp.zeros_like(dba_ref)
            dbx_ref[...] = jnp.zeros_like(dbx_ref)
            dlam_ref[...] = jnp.zeros_like(dlam_ref)

        xc = xc_ref[...]
        r, ig, sp, la, a, mult = _lru_gates(xc, wa_ref, wx_ref, ba_ref, bx_ref, lam_ref)
        ay = ay_ref[...]
        dya = dya_ref[...]
        hcur = h_ref[...]
        day_ref[...] = (dya * hcur * _gelu_grad(ay)).astype(BF16)
        a_s[...] = a
        g_s[...] = dya * _gelu(ay)

        def step(gg, cin):
            g = tm // 8 - 1 - gg
            base = pl.multiple_of(g * 8, 8)
            a8 = a_s[pl.ds(base, 8), :]
            g8 = g_s[pl.ds(base, 8), :]
            for j in range(7, -1, -1):
                d = g8[j:j + 1, :] + cin
                d_s[pl.ds(base + j, 1), :] = d
                cin = a8[j:j + 1, :] * d
            return cin

        cc[...] = lax.fori_loop(0, tm // 8, step, cc[...])
        dht = d_s[...]
        hs[0:8, :] = jnp.where(i == nb - 1, 0.0, hprev_ref[...])
        hs[8:, :] = hcur
        da = dht * hs[7:7 + tm, :]
        dmult = dht * ig * xc
        dig = dht * mult * xc
        dla = da * a - dmult * a * a / mult
        dpr = dla * (-LRU_C * sp) * r * (1.0 - r)
        dpi = dig * ig * (1.0 - ig)
        dprb = dpr.astype(BF16)
        dpib = dpi.astype(BF16)
        dxc_ref[...] = dht * mult * ig + _dot_nt(dprb, wa_ref[...]) + _dot_nt(dpib, wx_ref[...])
        dpr_ref[...] = dprb
        dpi_ref[...] = dpib
        dba_ref[...] += jnp.sum(dpr, axis=0, keepdims=True)
        dbx_ref[...] += jnp.sum(dpi, axis=0, keepdims=True)
        dlam_ref[...] += jnp.sum(dla * (-LRU_C * r), axis=0, keepdims=True) * (-_sigmoid(-lam_ref[...]))

    vec = _fix((1, BW))
    mat = _fix((BW, BW))
    rev = lambda col: pl.BlockSpec((tm, BW), lambda i: (nb - 1 - i, col))
    vshape = jax.ShapeDtypeStruct((1, BW), F32)
    return _pcall(body, name=name, grid=(nb,),
                  in_specs=[rev(0), rev(AY // BW), rev(0),
                            pl.BlockSpec((8, BW), lambda i: (jnp.maximum((nb - 1 - i) * hb - 1, 0), 0)),
                            rev(0), mat, mat, vec, vec, vec],
                  out_specs=[rev(0), rev(0), rev(0), rev(0), vec, vec, vec],
                  out_shape=[jax.ShapeDtypeStruct((t, BW), BF16), jax.ShapeDtypeStruct((t, BW), F32),
                             jax.ShapeDtypeStruct((t, BW), BF16), jax.ShapeDtypeStruct((t, BW), BF16),
                             vshape, vshape, vshape],
                  scratch_shapes=[pltpu.VMEM((tm + 8, BW), F32), pltpu.VMEM((tm, BW), F32),
                                  pltpu.VMEM((tm, BW), F32), pltpu.VMEM((tm, BW), F32),
                                  pltpu.VMEM((1, BW), F32)],
                  compiler_params=_cp("arbitrary"))(dya, z, h, h, xc, wa, wx, ba, bx, lam)


def conv_bwd(dxc, z, cw, *, name):
    t = dxc.shape[0]
    tm = min(256, t)
    nb = t // tm
    hb = tm // 8

    def body(d_ref, dnext_ref, ax_ref, prev_ref, cw_ref, dax_ref, dcw_ref, dcb_ref, ds, xs):
        i = pl.program_id(0)

        @pl.when(i == 0)
        def _():
            dcw_ref[...] = jnp.zeros_like(dcw_ref)
            dcb_ref[...] = jnp.zeros_like(dcb_ref)

        d = d_ref[...]
        ds[0:tm, :] = d
        ds[tm:, :] = jnp.where(i == nb - 1, 0.0, dnext_ref[...])
        xs[0:8, :] = jnp.where(i == 0, 0.0, prev_ref[...])
        xs[8:, :] = ax_ref[...]
        dax = cw_ref[3:4, :] * d
        for k in range(3):
            dax = dax + cw_ref[k:k + 1, :] * ds[3 - k:3 - k + tm, :]
        dax_ref[...] = dax.astype(BF16)
        for k in range(4):
            dcw_ref[k:k + 1, :] += jnp.sum(d * xs[5 + k:5 + k + tm, :], axis=0, keepdims=True)
        dcb_ref[...] += jnp.sum(d, axis=0, keepdims=True)

    return _pcall(body, name=name, grid=(nb,),
                  in_specs=[_rows(tm, BW),
                            pl.BlockSpec((8, BW), lambda i: (jnp.minimum((i + 1) * hb, nb * hb - 1), 0)),
                            _rows(tm, BW, AX // BW),
                            pl.BlockSpec((8, BW), lambda i: (jnp.maximum(i * hb - 1, 0), AX // BW)),
                            _fix((4, BW))],
                  out_specs=[_rows(tm, BW), _fix((4, BW)), _fix((1, BW))],
                  out_shape=[jax.ShapeDtypeStruct((t, BW), BF16), jax.ShapeDtypeStruct((4, BW), F32),
                             jax.ShapeDtypeStruct((1, BW), F32)],
                  scratch_shapes=[pltpu.VMEM((tm + 8, BW), F32), pltpu.VMEM((tm + 8, BW), F32)],
                  compiler_params=_cp("arbitrary"))(dxc, dxc, z, z, cw)


GLA_CB = 4


def _gla_consts():
    tri = (jnp.arange(CHUNK)[:, None] >= jnp.arange(CHUNK)[None, :]).astype(F32)
    mask = ((jnp.arange(BW)[:, None] // 128) == (jnp.arange(256)[None, :] // 64)).astype(F32)
    return tri, mask


def gla_fwd(z, zb, wg2p, bg, ng, *, name):
    t = z.shape[0]
    tm = GLA_CB * CHUNK
    nc = t // CHUNK
    tri, mask = _gla_consts()

    def body(q_ref, k_ref, v_ref, misc_ref, br_ref, w_ref, bg_ref, ng_ref, tri_ref, mask_ref,
             yb_ref, st_ref, st):
        @pl.when(pl.program_id(0) == 0)
        def _():
            st[...] = jnp.zeros_like(st)

        for c in range(GLA_CB):
            rows = slice(c * CHUNK, (c + 1) * CHUNK)
            pre = _dot(misc_ref[rows, :], w_ref[...]) + bg_ref[...]
            la = _log_sigmoid(pre) / GLA_TAU
            gc = _dot_hi(tri_ref[...], la)
            gt = gc[CHUNK - 1:CHUNK, :]
            kdec = k_ref[rows, :] * jnp.exp(gt - gc)
            delta = _dot_tn(v_ref[rows, :], kdec.astype(BF16))
            s_new = st[...] * jnp.exp(gt) + delta * mask_ref[...]
            st[...] = s_new
            st_ref[c] = s_new
            o = _dot_nt(q_ref[rows, :], s_new.astype(BF16)) * (64.0 ** -0.5)
            br = br_ref[rows, :]
            for hd in range(4):
                cols = slice(hd * 128, (hd + 1) * 128)
                oh = o[:, cols]
                rs = lax.rsqrt(jnp.mean(oh * oh, axis=-1, keepdims=True) + RMS_EPS)
                brh = br[:, cols]
                yb_ref[rows, cols] = (oh * rs * ng_ref[:, cols] * (brh * _sigmoid(brh))).astype(BF16)

    return _pcall(body, name=name, grid=(t // tm,),
                  in_specs=[_rows(tm, 256, BQ // 256), _rows(tm, 256, BK // 256), _rows(tm, BW, BV // BW),
                            _rows(tm, 128, MISC // 128), _rows(tm, BW, BR // BW), _fix((128, 256)),
                            _fix((1, 256)), _fix((1, BW)), _fix((CHUNK, CHUNK)), _fix((BW, 256))],
                  out_specs=[_rows(tm, BW), pl.BlockSpec((GLA_CB, BW, 256), lambda i: (i, 0, 0))],
                  out_shape=[jax.ShapeDtypeStruct((t, BW), BF16), jax.ShapeDtypeStruct((nc, BW, 256), F32)],
                  scratch_shapes=[pltpu.VMEM((BW, 256), F32)],
                  compiler_params=_cp("arbitrary"))(zb, z, zb, zb, z, wg2p, bg, ng, tri, mask)


def gla_bwd(dyb, z, zb, states, wg2p, bg, ng, *, name):
    t = z.shape[0]
    tm = GLA_CB * CHUNK
    nb = t // tm
    tri, mask = _gla_consts()
    triu = tri.T

    def body(dy_ref, q_ref, k_ref, v_ref, misc_ref, br_ref, st_ref, sp_ref, w_ref, bg_ref, ng_ref,
             tri_ref, triu_ref, mask_ref,
             dq_ref, dk_ref, dv_ref, dbr_ref, dmisc_ref, dpre_ref, dbg_ref, dng_ref, cc):
        i = pl.program_id(0)

        @pl.when(i == 0)
        def _():
            cc[...] = jnp.zeros_like(cc)
            dbg_ref[...] = jnp.zeros_like(dbg_ref)
            dng_ref[...] = jnp.zeros_like(dng_ref)

        last_row = lax.broadcasted_iota(jnp.int32, (CHUNK, 256), 0) == CHUNK - 1
        for c in range(GLA_CB - 1, -1, -1):
            rows = slice(c * CHUNK, (c + 1) * CHUNK)
            pre = _dot(misc_ref[rows, :], w_ref[...]) + bg_ref[...]
            la = _log_sigmoid(pre) / GLA_TAU
            gc = _dot_hi(tri_ref[...], la)
            gt = gc[CHUNK - 1:CHUNK, :]
            eg = jnp.exp(gt - gc)
            kdec = k_ref[rows, :] * eg
            e = jnp.exp(gt)
            s_n = st_ref[c]
            if c > 0:
                s_prev = st_ref[c - 1]
            else:
                s_prev = jnp.where(i == nb - 1, 0.0, sp_ref[0])
            sb = s_n.astype(BF16)
            qb = q_ref[rows, :]
            o = _dot_nt(qb, sb) * (64.0 ** -0.5)
            br = br_ref[rows, :]
            dy = dy_ref[rows, :]
            do_parts = []
            for hd in range(4):
                cols = slice(hd * 128, (hd + 1) * 128)
                oh = o[:, cols]
                rs = lax.rsqrt(jnp.mean(oh * oh, axis=-1, keepdims=True) + RMS_EPS)
                ohat = oh * rs
                brh = br[:, cols]
                sg = _sigmoid(brh)
                dyh = dy[:, cols]
                ngh = ng_ref[:, cols]
                don = dyh * (brh * sg)
                dbr_ref[rows, cols] = (dyh * (ohat * ngh) * sg * (1.0 + brh * (1.0 - sg))).astype(BF16)
                dng_ref[:, cols] += jnp.sum(don * ohat, axis=0, keepdims=True)
                doh = don * ngh
                do_parts.append(rs * (doh - ohat * jnp.mean(doh * ohat, axis=-1, keepdims=True)))
            dob = jnp.concatenate(do_parts, axis=1).astype(BF16)
            dq_ref[rows, :] = (_dot(dob, sb) * (64.0 ** -0.5)).astype(BF16)
            dst = cc[...] + _dot_tn(dob, qb) * (64.0 ** -0.5) * mask_ref[...]
            dsb = dst.astype(BF16)
            dkdec = _dot(v_ref[rows, :], dsb)
            dv_ref[rows, :] = _dot_nt(kdec.astype(BF16), dsb).astype(BF16)
            dgt = jnp.sum(dst * s_prev, axis=0, keepdims=True) * e
            dk_ref[rows, :] = (dkdec * eg).astype(BF16)
            dd = dkdec * kdec
            dgt = dgt + jnp.sum(dd, axis=0, keepdims=True)
            dgc = jnp.where(last_row, dgt - dd, -dd)
            dla = _dot_hi(triu_ref[...], dgc)
            dpre = dla * (1.0 / GLA_TAU) * _sigmoid(-pre)
            dpb = dpre.astype(BF16)
            dpre_ref[rows, :] = dpb
            dmisc_ref[rows, :] = _dot_nt(dpb, w_ref[...])
            dbg_ref[...] += jnp.sum(dpre, axis=0, keepdims=True)
            cc[...] = dst * e

    rev = lambda w, col: pl.BlockSpec((tm, w), lambda i: (nb - 1 - i, col))
    return _pcall(body, name=name, grid=(nb,),
                  in_specs=[rev(BW, 0), rev(256, BQ // 256), rev(256, BK // 256), rev(BW, BV // BW),
                            rev(128, MISC // 128), rev(BW, BR // BW),
                            pl.BlockSpec((GLA_CB, BW, 256), lambda i: (nb - 1 - i, 0, 0)),
                            pl.BlockSpec((1, BW, 256), lambda i: (jnp.maximum((nb - 1 - i) * GLA_CB - 1, 0), 0, 0)),
                            _fix((128, 256)), _fix((1, 256)), _fix((1, BW)),
                            _fix((CHUNK, CHUNK)), _fix((CHUNK, CHUNK)), _fix((BW, 256))],
                  out_specs=[rev(256, 0), rev(256, 0), rev(BW, 0), rev(BW, 0), rev(128, 0), rev(256, 0),
                             _fix((1, 256)), _fix((1, BW))],
                  out_shape=[jax.ShapeDtypeStruct((t, 256), BF16), jax.ShapeDtypeStruct((t, 256), BF16),
                             jax.ShapeDtypeStruct((t, BW), BF16), jax.ShapeDtypeStruct((t, BW), BF16),
                             jax.ShapeDtypeStruct((t, 128), F32), jax.ShapeDtypeStruct((t, 256), BF16),
                             jax.ShapeDtypeStruct((1, 256), F32), jax.ShapeDtypeStruct((1, BW), F32)],
                  scratch_shapes=[pltpu.VMEM((BW, 256), F32)],
                  compiler_params=_cp("arbitrary"))(dyb, zb, z, zb, zb, z, states, states, wg2p, bg, ng,
                                                    tri, triu, mask)


FOX_SCALE = 64.0 ** -0.5
NEG = -1e30


def fox_fcum(z, bfp, *, name):
    t = z.shape[0]
    tm = min(256, t)
    tri = (jnp.arange(tm)[:, None] >= jnp.arange(tm)[None, :]).astype(F32)

    def body(m_ref, b_ref, tri_ref, o_ref, cc):
        @pl.when(pl.program_id(0) == 0)
        def _():
            cc[...] = jnp.zeros_like(cc)

        lf = _log_sigmoid(m_ref[...] + b_ref[...])
        cs = _dot_hi(tri_ref[...], lf) + cc[...]
        o_ref[...] = cs
        cc[...] = cs[tm - 1:tm, :]

    return _pcall(body, name=name, grid=(t // tm,),
                  in_specs=[_rows(tm, 128, MISC // 128), _fix((1, 128)), _fix((tm, tm))],
                  out_specs=_rows(tm, 128), out_shape=jax.ShapeDtypeStruct((t, 128), F32),
                  scratch_shapes=[pltpu.VMEM((1, 128), F32)],
                  compiler_params=_cp("arbitrary"))(z, bfp, tri)


def fox_dcf(dfc, z, bfp, dmisc_g, *, name):
    t = z.shape[0]
    tm = min(256, t)
    nb = t // tm
    triu = (jnp.arange(tm)[:, None] <= jnp.arange(tm)[None, :]).astype(F32)

    def body(d_ref, m_ref, b_ref, g_ref, tri_ref, o_ref, dbf_ref, cc):
        @pl.when(pl.program_id(0) == 0)
        def _():
            cc[...] = jnp.zeros_like(cc)
            dbf_ref[...] = jnp.zeros_like(dbf_ref)

        rc = _dot_hi(tri_ref[...], d_ref[...]) + cc[...]
        cc[...] = rc[0:1, :]
        dcf = rc * _sigmoid(-(m_ref[...] + b_ref[...]))
        o_ref[...] = (dcf + g_ref[...]).astype(BF16)
        dbf_ref[...] += jnp.sum(dcf, axis=0, keepdims=True)

    rev = lambda col: pl.BlockSpec((tm, 128), lambda i: (nb - 1 - i, col))
    return _pcall(body, name=name, grid=(nb,),
                  in_specs=[rev(0), rev(MISC // 128), _fix((1, 128)), rev(0), _fix((tm, tm))],
                  out_specs=[rev(0), _fix((1, 128))],
                  out_shape=[jax.ShapeDtypeStruct((t, 128), BF16), jax.ShapeDtypeStruct((1, 128), F32)],
                  scratch_shapes=[pltpu.VMEM((1, 128), F32)],
                  compiler_params=_cp("arbitrary"))(dfc, z, bfp, dmisc_g, triu)


def fox_delta(dyc, ycf, *, name):
    t = dyc.shape[0]
    tm = min(256, t)
    seg = ((jnp.arange(BW)[:, None] // 64) == jnp.arange(128)[None, :]).astype(F32)

    def body(d_ref, o_ref, s_ref, out_ref):
        out_ref[...] = _dot_hi(d_ref[...] * o_ref[...], s_ref[...])

    return _pcall(body, name=name, grid=(t // tm,),
                  in_specs=[_rows(tm, BW), _rows(tm, BW), _fix((BW, 128))],
                  out_specs=_rows(tm, 128), out_shape=jax.ShapeDtypeStruct((t, 128), F32),
                  compiler_params=_cp("parallel"))(dyc, ycf, seg)


def fox_fwd_t(zb, frow, fkb, *, name):
    t = zb.shape[0]
    tq = min(512, t)
    nq = t // tq
    rep = tq // 128

    pairs = [(i, j) for i in range(nq) for j in range(i + 1)]
    qi_tab = jnp.asarray([p[0] for p in pairs], jnp.int32)
    kj_tab = jnp.asarray([p[1] for p in pairs], jnp.int32)

    def body(qi_ref, kj_ref, q_ref, k_ref, v_ref, fq_ref, fk_ref, y_ref, yf_ref, lse_ref, m_s, l_s, acc):
        step = pl.program_id(1)
        i, j = qi_ref[step], kj_ref[step]

        @pl.when(j == 0)
        def _():
            m_s[...] = jnp.full_like(m_s, NEG)
            l_s[...] = jnp.zeros_like(l_s)
            acc[...] = jnp.zeros_like(acc)

        lo = lax.broadcasted_iota(jnp.int32, (tq, 128), 1) < 64

        def work(diagonal):
            q = q_ref[...]
            k = k_ref[...]
            v = v_ref[...]
            if diagonal:
                key = lax.broadcasted_iota(jnp.int32, (tq, tq), 0)
                qry = lax.broadcasted_iota(jnp.int32, (tq, tq), 1)
                keep = key <= qry
            for hh in range(2):
                sel = lo if hh == 0 else jnp.logical_not(lo)
                qh = jnp.where(sel, q, jnp.zeros_like(q))
                s = _dot_nt(k, qh) + fq_ref[hh] - jnp.tile(fk_ref[hh], (1, rep))
                if diagonal:
                    s = jnp.where(keep, s, NEG)
                m_old = m_s[hh]
                m_new = jnp.maximum(m_old, jnp.max(s, axis=0, keepdims=True))
                p = jnp.exp(s - m_new)
                corr = jnp.exp(m_old - m_new)
                l_s[hh] = l_s[hh] * corr + jnp.sum(p, axis=0, keepdims=True)
                m_s[hh] = m_new
                pv = _dot_tn(v, p.astype(BF16))
                rows = slice(64 * hh, 64 * hh + 64)
                acc[rows, :] = acc[rows, :] * corr + pv[rows, :]

        @pl.when(j < i)
        def _():
            work(False)

        @pl.when(j == i)
        def _():
            work(True)
            first = lax.broadcasted_iota(jnp.int32, (128, tq), 0) < 64
            out = (acc[...] * jnp.where(first, 1.0 / l_s[0], 1.0 / l_s[1])).T
            y_ref[...] = out.astype(BF16)
            yf_ref[...] = out
            lse_ref[...] = m_s[...] + jnp.log(l_s[...])

    kv = lambda off: pl.BlockSpec((tq, 128), lambda h, s, qi, kj: (kj[s], off // 128 + h))
    gs = pltpu.PrefetchScalarGridSpec(
        num_scalar_prefetch=2, grid=(4, len(pairs)),
        in_specs=[pl.BlockSpec((tq, 128), lambda h, s, qi, kj: (qi[s], CQ // 128 + h)), kv(CK), kv(CV),
                  pl.BlockSpec((2, 1, tq), lambda h, s, qi, kj: (h, 0, qi[s])),
                  pl.BlockSpec((2, tq, 128), lambda h, s, qi, kj: (h, kj[s], 0))],
        out_specs=[pl.BlockSpec((tq, 128), lambda h, s, qi, kj: (qi[s], h)),
                   pl.BlockSpec((tq, 128), lambda h, s, qi, kj: (qi[s], h)),
                   pl.BlockSpec((2, 1, tq), lambda h, s, qi, kj: (h, 0, qi[s]))],
        scratch_shapes=[pltpu.VMEM((2, 1, tq), F32), pltpu.VMEM((2, 1, tq), F32), pltpu.VMEM((128, tq), F32)])
    return _pcall(body, name=name, grid_spec=gs,
                  out_shape=[jax.ShapeDtypeStruct((t, BW), BF16), jax.ShapeDtypeStruct((t, BW), F32),
                             jax.ShapeDtypeStruct((FOX_H, 1, t), F32)],
                  compiler_params=_cp("parallel", "arbitrary"))(qi_tab, kj_tab, zb, zb, zb, frow, fkb)


def fox_bwd_t(zb, dyc, frow, fkb, lse, dl, *, name):
    t = zb.shape[0]
    tq = min(512, t)
    nq = t // tq
    rep = tq // 128

    pairs = [(j, i) for j in range(nq) for i in range(j, nq)]
    kj_tab = jnp.asarray([p[0] for p in pairs], jnp.int32)
    qi_tab = jnp.asarray([p[1] for p in pairs], jnp.int32)

    def body(kj_ref, qi_ref, q_ref, k_ref, v_ref, do_ref, fq_ref, fk_ref, lse_ref, dl_ref,
             dq_ref, dk_ref, dv_ref, dfk_ref, dfq_ref, dk_s, dv_s, df_s, dq_s):
        step = pl.program_id(1)
        j, i = kj_ref[step], qi_ref[step]

        @pl.when(step == 0)
        def _():
            dq_s[...] = jnp.zeros_like(dq_s)
            dfq_ref[...] = jnp.zeros_like(dfq_ref)

        @pl.when(i == j)
        def _():
            dk_s[...] = jnp.zeros_like(dk_s)
            dv_s[...] = jnp.zeros_like(dv_s)
            df_s[...] = jnp.zeros_like(df_s)

        lo = lax.broadcasted_iota(jnp.int32, (tq, 128), 1) < 64

        def work(diagonal):
            q = q_ref[...]
            k = k_ref[...]
            v = v_ref[...]
            dob = do_ref[...].astype(BF16)
            if diagonal:
                key = lax.broadcasted_iota(jnp.int32, (tq, tq), 0)
                qry = lax.broadcasted_iota(jnp.int32, (tq, tq), 1)
                keep = key <= qry
            dvs, dks = [], []
            for hh in range(2):
                sel = lo if hh == 0 else jnp.logical_not(lo)
                qh = jnp.where(sel, q, jnp.zeros_like(q))
                doh = jnp.where(sel, dob, jnp.zeros_like(dob))
                p = jnp.exp(_dot_nt(k, qh) + (fq_ref[hh] - lse_ref[hh]) - jnp.tile(fk_ref[hh], (1, rep)))
                if diagonal:
                    p = jnp.where(keep, p, 0.0)
                ds = p * (_dot_nt(v, doh) - dl_ref[hh])
                dsb = ds.astype(BF16)
                dvs.append(_dot(p.astype(BF16), dob))
                dks.append(_dot(dsb, q))
                rows = slice(64 * hh, 64 * hh + 64)
                dq_s[i, rows, :] += _dot_tn(k, dsb)[rows, :]
                part = ds[:, 0:128]
                for r in range(1, rep):
                    part = part + ds[:, 128 * r:128 * (r + 1)]
                df_s[hh] += part
                dfq_ref[hh, i] += jnp.sum(ds, axis=0, keepdims=True)
            dv_s[...] += jnp.where(lo, dvs[0], dvs[1])
            dk_s[...] += jnp.where(lo, dks[0], dks[1])

        @pl.when(i > j)
        def _():
            work(False)

        @pl.when(i == j)
        def _():
            work(True)
            dq_ref[...] = dq_s[i].T.astype(BF16)

        @pl.when(i == nq - 1)
        def _():
            dk_ref[...] = dk_s[...].astype(BF16)
            dv_ref[...] = dv_s[...].astype(BF16)
            for hh in range(2):
                dfk_ref[hh] = -jnp.sum(df_s[hh].T, axis=0, keepdims=True)

    row = lambda: pl.BlockSpec((2, 1, tq), lambda h, s, kj, qi: (h, 0, qi[s]))
    gs = pltpu.PrefetchScalarGridSpec(
        num_scalar_prefetch=2, grid=(4, len(pairs)),
        in_specs=[pl.BlockSpec((tq, 128), lambda h, s, kj, qi: (qi[s], CQ // 128 + h)),
                  pl.BlockSpec((tq, 128), lambda h, s, kj, qi: (kj[s], CK // 128 + h)),
                  pl.BlockSpec((tq, 128), lambda h, s, kj, qi: (kj[s], CV // 128 + h)),
                  pl.BlockSpec((tq, 128), lambda h, s, kj, qi: (qi[s], h)),
                  row(), pl.BlockSpec((2, tq, 128), lambda h, s, kj, qi: (h, kj[s], 0)), row(), row()],
        out_specs=[pl.BlockSpec((tq, 128), lambda h, s, kj, qi: (kj[s], h)),
                   pl.BlockSpec((tq, 128), lambda h, s, kj, qi: (kj[s], h)),
                   pl.BlockSpec((tq, 128), lambda h, s, kj, qi: (kj[s], h)),
                   pl.BlockSpec((2, 1, tq), lambda h, s, kj, qi: (h, 0, kj[s])),
                   pl.BlockSpec((2, nq, 1, tq), lambda h, s, kj, qi: (h, 0, 0, 0))],
        scratch_shapes=[pltpu.VMEM((tq, 128), F32), pltpu.VMEM((tq, 128), F32), pltpu.VMEM((2, tq, 128), F32),
                        pltpu.VMEM((nq, 128, tq), F32)])
    return _pcall(body, name=name, grid_spec=gs,
                  out_shape=[jax.ShapeDtypeStruct((t, BW), BF16), jax.ShapeDtypeStruct((t, BW), BF16),
                             jax.ShapeDtypeStruct((t, BW), BF16), jax.ShapeDtypeStruct((FOX_H, 1, t), F32),
                             jax.ShapeDtypeStruct((FOX_H, nq, 1, tq), F32)],
                  compiler_params=_cp("parallel", "arbitrary"))(kj_tab, qi_tab, zb, zb, zb, dyc, frow, fkb, lse, dl)


def merge_fwd(ya, yb, yc, wbr, z, *, name):
    t = ya.shape[0]
    tm = min(512, t)

    def body(ya_ref, yb_ref, yc_ref, w_ref, g0_ref, g1_ref, g2_ref, o_ref):
        m = _sigmoid(g0_ref[...]) * _dot(ya_ref[...], w_ref[0])
        m = m + _sigmoid(g1_ref[...]) * _dot(yb_ref[...], w_ref[1])
        m = m + _sigmoid(g2_ref[...]) * _dot(yc_ref[...], w_ref[2])
        o_ref[...] = m.astype(BF16)

    return _pcall(body, name=name, grid=(t // tm,),
                  in_specs=[_rows(tm, BW)] * 3 + [_fix((3, BW, D))]
                  + [_rows(tm, D, G0 // D + j) for j in range(3)],
                  out_specs=_rows(tm, D), out_shape=jax.ShapeDtypeStruct((t, D), BF16),
                  compiler_params=_cp("parallel"))(ya, yb, yc, wbr, z, z, z)


def merge_bwd(doutb, wo, l, ya, yb, yc, wbr, z, *, name):
    t = ya.shape[0]
    tm = min(256, t)

    def body(do_ref, wo_ref, ya_ref, yb_ref, yc_ref, w_ref, g0_ref, g1_ref, g2_ref,
             dya_ref, dyb_ref, dyc_ref, dp0_ref, dp1_ref, dp2_ref, dg0_ref, dg1_ref, dg2_ref):
        dm = _dot_nt(do_ref[...], wo_ref[...])
        ys = (ya_ref, yb_ref, yc_ref)
        gs = (g0_ref, g1_ref, g2_ref)
        dys = (dya_ref, dyb_ref, dyc_ref)
        dps = (dp0_ref, dp1_ref, dp2_ref)
        dgs = (dg0_ref, dg1_ref, dg2_ref)
        for j in range(3):
            s = _sigmoid(gs[j][...])
            pj = _dot(ys[j][...], w_ref[j])
            dpb = (dm * s).astype(BF16)
            dps[j][...] = dpb
            dgs[j][...] = (dm * pj * s * (1.0 - s)).astype(BF16)
            dys[j][...] = _dot_nt(dpb, w_ref[j])

    yshape = jax.ShapeDtypeStruct((t, BW), F32)
    dshape = jax.ShapeDtypeStruct((t, D), BF16)
    return _pcall(body, name=name, grid=(t // tm,),
                  in_specs=[_rows(tm, D), _layer(l, (D, D))] + [_rows(tm, BW)] * 3
                  + [_fix((3, BW, D))] + [_rows(tm, D, G0 // D + j) for j in range(3)],
                  out_specs=[_rows(tm, BW)] * 3 + [_rows(tm, D)] * 6,
                  out_shape=[yshape] * 3 + [dshape] * 6,
                  compiler_params=_cp("parallel"))(doutb, wo, ya, yb, yc, wbr, z, z, z)


def adamw(w, g, m, v, *, name):
    nl, r, c = w.shape
    tm = _row_tile(r)

    def body(w_ref, g_ref, m_ref, v_ref, d_ref, mo_ref, vo_ref):
        gg = g_ref[...]
        mn = ADAM_B1 * m_ref[...] + (1.0 - ADAM_B1) * gg
        vn = ADAM_B2 * v_ref[...] + (1.0 - ADAM_B2) * (gg * gg)
        m_hat = mn / (1.0 - ADAM_B1 ** ADAM_STEP)
        v_hat = vn / (1.0 - ADAM_B2 ** ADAM_STEP)
        d_ref[...] = -ADAM_LR * (m_hat / (jnp.sqrt(v_hat) + ADAM_EPS) + ADAM_WD * w_ref[...])
        mo_ref[...] = mn
        vo_ref[...] = vn

    shp = jax.ShapeDtypeStruct((nl, r, c), F32)
    blk = pl.BlockSpec((None, tm, c), lambda l, i: (l, i, 0))
    return _pcall(body, name=name, grid=(nl, r // tm), in_specs=[blk] * 4, out_specs=[blk] * 3,
                  out_shape=[shp] * 3, compiler_params=_cp("parallel", "parallel"))(w, g, m, v)


def _place():
    return lax.axis_index("x"), lax.axis_index("y"), lax.axis_index("c")


def _remote(src, dst, send_sems, recv_sems, k, to):
    return pltpu.make_async_remote_copy(src_ref=src, dst_ref=dst, send_sem=send_sems.at[k],
                                        recv_sem=recv_sems.at[k], device_id=to, device_id_type=MESH)


HBM = pl.BlockSpec(memory_space=pltpu.HBM)
SEM = pl.BlockSpec(memory_space=pltpu.SEMAPHORE)
EFFECT = pltpu.SideEffectType.DATAFLOW_SIDE_EFFECTING


def gather_first(shards):
    n = len(shards)

    def body(*refs):
        ins, outs, lands = refs[:n], refs[n:2 * n], refs[2 * n:3 * n]
        send_sems, recv_sems, own_send, own_recv = refs[3 * n:]
        x, y, c = _place()
        sib = (x, y, 1 - c)
        chips = [(1 - x, y), (x, 1 - y), (1 - x, 1 - y)]
        k_me = 2 * x + y
        mine = []
        for t in range(n):
            mine.append(_remote(ins[t].at[0], outs[t].at[0, k_me], own_send, own_recv, 2 * t, sib))
            mine.append(_remote(ins[t].at[1], lands[t].at[0, k_me], own_send, own_recv, 2 * t + 1, sib))
        for cp in mine:
            cp.start()

        def slot(t, chip):
            return outs[t].at[0, 2 * chip[0] + chip[1]]

        @pl.when(c == 0)
        def _():
            first = [_remote(ins[t].at[0], outs[t].at[0, k_me], send_sems, recv_sems, 6 * t + j, (*chip, 0))
                     for t in range(n) for j, chip in enumerate(chips)]
            for cp in first:
                cp.start()
            passed = []
            for t in range(n):
                for j, chip in enumerate(chips):
                    _remote(slot(t, chip), slot(t, chip), send_sems, recv_sems, 6 * t + j, (*chip, 0)).wait_recv()
                    cp = _remote(slot(t, chip), slot(t, chip), send_sems, recv_sems, 6 * t + 3 + j, sib)
                    cp.start()
                    passed.append(cp)
            for cp in first + passed:
                cp.wait_send()

        @pl.when(c == 1)
        def _():
            for t in range(n):
                for j, chip in enumerate(chips):
                    _remote(slot(t, chip), slot(t, chip), send_sems, recv_sems, 6 * t + 3 + j, sib).wait_recv()

        for cp in mine:
            cp.wait()

    shape = [jax.ShapeDtypeStruct((1, 4) + s.shape[1:], s.dtype) for s in shards]
    out = _pcall(body, name="gather_first", in_specs=[ANY] * n, out_specs=[ANY] * (2 * n), out_shape=shape + shape,
                 scratch_shapes=[pltpu.SemaphoreType.DMA((6 * n,)), pltpu.SemaphoreType.DMA((6 * n,)),
                                 pltpu.SemaphoreType.DMA((2 * n,)), pltpu.SemaphoreType.DMA((2 * n,))])(*shards)
    return out[:n], out[n:]


def _rest_copies(ins, lands, send_sems, recv_sems):
    x, y, c = _place()
    chips = [(1 - x, y), (x, 1 - y), (1 - x, 1 - y)]
    copies, arrivals = [], []
    for t in range(len(ins)):
        for j, chip in enumerate(chips):
            for to in range(2):
                copies.append(pltpu.make_async_remote_copy(
                    src_ref=ins[t].at[1], dst_ref=lands[t].at[0, 2 * x + y], send_sem=send_sems.at[6 * t + 2 * j + to],
                    recv_sem=recv_sems.at[3 * t + j], device_id=(*chip, to), device_id_type=MESH))
            blk = lands[t].at[0, 2 * chip[0] + chip[1]]
            arrivals.append(pltpu.make_async_remote_copy(
                src_ref=blk, dst_ref=blk, send_sem=send_sems.at[6 * t + 2 * j], recv_sem=recv_sems.at[3 * t + j],
                device_id=(*chip, 1), device_id_type=MESH))
    return copies, arrivals


def gather_rest_start(shards, lands):
    n = len(shards)

    def body(*refs):
        ins, lds = refs[:n], refs[n:2 * n]
        send_sems, recv_sems = refs[2 * n], refs[2 * n + 1]
        token = refs[-1]
        copies, _ = _rest_copies(ins, lds, send_sems, recv_sems)

        @pl.when(lax.axis_index("c") == 1)
        def _():
            for cp in copies:
                cp.start()

        token[...] = jnp.zeros_like(token)

    hbm = lambda a: pltpu.with_memory_space_constraint(a, pltpu.HBM)
    out = _pcall(body, name="gather_rest_start", in_specs=[HBM] * (2 * n),
                 out_specs=[SEM, SEM] + [HBM] * (2 * n) + [pl.BlockSpec(memory_space=pltpu.VMEM)],
                 out_shape=[pltpu.SemaphoreType.DMA((6 * n,)), pltpu.SemaphoreType.DMA((3 * n,))]
                 + [pltpu.HBM(a.shape, a.dtype) for a in shards] + [pltpu.HBM(a.shape, a.dtype) for a in lands]
                 + [jax.ShapeDtypeStruct((8, 128), F32)],
                 input_output_aliases={i: 2 + i for i in range(2 * n)},
                 compiler_params=pltpu.CompilerParams(has_side_effects=EFFECT))(
                     *[hbm(a) for a in shards], *[hbm(a) for a in lands])
    return out[0], out[1], out[2:2 + n], out[2 + n:2 + 2 * n], out[-1]


def gather_rest_wait(send_sems, recv_sems, srcs, lands, after):
    n = len(srcs)

    def body(*refs):
        ins, lds = refs[:n], refs[n:2 * n]
        s_sems, r_sems = refs[2 * n], refs[2 * n + 1]
        copies, arrivals = _rest_copies(ins, lds, s_sems, r_sems)

        @pl.when(lax.axis_index("c") == 1)
        def _():
            for cp in copies:
                cp.wait_send()

        for cp in arrivals:
            cp.wait_recv()

    out = _pcall(body, name="gather_rest_wait", in_specs=[HBM] * (2 * n) + [SEM, SEM, ANY],
                 out_specs=[HBM] * (2 * n),
                 out_shape=[pltpu.HBM(a.shape, a.dtype) for a in srcs] + [pltpu.HBM(a.shape, a.dtype) for a in lands],
                 input_output_aliases={i: i for i in range(2 * n)},
                 compiler_params=pltpu.CompilerParams(has_side_effects=EFFECT))(
                     *srcs, *lands, send_sems, recv_sems, after)
    return out[n:]


def pair_send(gl, layer):
    n = len(gl)

    def body(*refs):
        ins, outs = refs[:n], refs[n:2 * n]
        send_sems, recv_sems = refs[2 * n:]
        x, y, c = _place()
        sib = (x, y, 1 - c)
        cps = [_remote(ins[t], outs[t], send_sems, recv_sems, t, sib) for t in range(n)]

        @pl.when(c == 1 - layer)
        def _():
            for cp in cps:
                cp.start()
            for cp in cps:
                cp.wait_send()

        @pl.when(c == layer)
        def _():
            for cp in cps:
                cp.wait_recv()

    return _pcall(body, name="pair_send_l%d" % layer, in_specs=[ANY] * n, out_specs=[ANY] * n,
                  out_shape=[jax.ShapeDtypeStruct(a.shape, a.dtype) for a in gl],
                  scratch_shapes=[pltpu.SemaphoreType.DMA((n,)), pltpu.SemaphoreType.DMA((n,))])(*gl)


def _chip_copies(ins, outs, send_sems, recv_sems):
    x, y, c = _place()
    chips = [(1 - x, y), (x, 1 - y), (1 - x, 1 - y)]
    return [_remote(ins[t].at[2 * chip[0] + chip[1]], outs[t].at[j], send_sems, recv_sems, 3 * t + j, (*chip, c))
            for t in range(len(ins)) for j, chip in enumerate(chips)]


def chip_send(s1, layer):
    n = len(s1)

    def body(*refs):
        ins, outs = refs[:n], refs[n:2 * n]
        send_sems, recv_sems = refs[2 * n:]
        cps = _chip_copies(ins, outs, send_sems, recv_sems)

        @pl.when(lax.axis_index("c") == layer)
        def _():
            for cp in cps:
                cp.start()
            for cp in cps:
                cp.wait()

    return _pcall(body, name="chip_send_l%d" % layer, in_specs=[ANY] * n, out_specs=[ANY] * n,
                  out_shape=[jax.ShapeDtypeStruct((3,) + a.shape[1:], a.dtype) for a in s1],
                  scratch_shapes=[pltpu.SemaphoreType.DMA((3 * n,)), pltpu.SemaphoreType.DMA((3 * n,))])(*s1)


def chip_send_start(s1, layer):
    n = len(s1)
    land = [lax.empty((3,) + a.shape[1:], a.dtype) for a in s1]

    def body(*refs):
        ins, lands = refs[:n], refs[n:2 * n]
        send_sems, recv_sems = refs[2 * n], refs[2 * n + 1]
        token = refs[-1]
        cps = _chip_copies(ins, lands, send_sems, recv_sems)

        @pl.when(lax.axis_index("c") == layer)
        def _():
            for cp in cps:
                cp.start()

        token[...] = jnp.zeros_like(token)

    hbm = lambda a: pltpu.with_memory_space_constraint(a, pltpu.HBM)
    out = _pcall(body, name="chip_send_start_l%d" % layer, in_specs=[HBM] * (2 * n),
                 out_specs=[SEM, SEM] + [HBM] * (2 * n) + [pl.BlockSpec(memory_space=pltpu.VMEM)],
                 out_shape=[pltpu.SemaphoreType.DMA((3 * n,)), pltpu.SemaphoreType.DMA((3 * n,))]
                 + [pltpu.HBM(a.shape, a.dtype) for a in s1] + [pltpu.HBM(a.shape, a.dtype) for a in land]
                 + [jax.ShapeDtypeStruct((8, 128), F32)],
                 input_output_aliases={i: 2 + i for i in range(2 * n)},
                 compiler_params=pltpu.CompilerParams(has_side_effects=EFFECT))(
                     *[hbm(a) for a in s1], *[hbm(a) for a in land])
    return out[0], out[1], out[2:2 + n], out[2 + n:2 + 2 * n], out[-1]


def chip_send_wait(send_sems, recv_sems, srcs, lands, after, layer):
    n = len(srcs)

    def body(*refs):
        ins, lds = refs[:n], refs[n:2 * n]
        s_sems, r_sems = refs[2 * n], refs[2 * n + 1]
        cps = _chip_copies(ins, lds, s_sems, r_sems)

        @pl.when(lax.axis_index("c") == layer)
        def _():
            for cp in cps:
                cp.wait_send()
                cp.wait_recv()

    out = _pcall(body, name="chip_send_wait_l%d" % layer, in_specs=[HBM] * (2 * n) + [SEM, SEM, ANY],
                 out_specs=[HBM] * (2 * n),
                 out_shape=[pltpu.HBM(a.shape, a.dtype) for a in srcs] + [pltpu.HBM(a.shape, a.dtype) for a in lands],
                 input_output_aliases={i: i for i in range(2 * n)},
                 compiler_params=pltpu.CompilerParams(has_side_effects=EFFECT))(
                     *srcs, *lands, send_sems, recv_sems, after)
    return out[n:]


def pair_share(s2):
    n = len(s2)

    def body(*refs):
        ins, outs = refs[:n], refs[n:2 * n]
        send_sems, recv_sems = refs[2 * n:]
        x, y, c = _place()
        sib = (x, y, 1 - c)
        cps = [_remote(ins[t].at[c], outs[t].at[c], send_sems, recv_sems, t, sib) for t in range(n)]
        for cp in cps:
            cp.start()
        for t in range(n):
            cps[t].wait_send()
            _remote(ins[t].at[c], outs[t].at[1 - c], send_sems, recv_sems, t, sib).wait_recv()

    return _pcall(body, name="pair_share", in_specs=[ANY] * n, out_specs=[ANY] * n,
                  out_shape=[jax.ShapeDtypeStruct(a.shape, a.dtype) for a in s2],
                  input_output_aliases={t: t for t in range(n)},
                  scratch_shapes=[pltpu.SemaphoreType.DMA((n,)), pltpu.SemaphoreType.DMA((n,))])(*s2)


def small_exchange(gs):
    rows, width = gs.shape

    def body(g_ref, o_ref, send_sems, recv_sems):
        x, y, c = _place()
        cps = []
        for r in range(1, 8):
            dx, dy, dc = (r >> 2) & 1, (r >> 1) & 1, r & 1
            to = (x if dx == 0 else 1 - x, y if dy == 0 else 1 - y, c if dc == 0 else 1 - c)
            cps.append(_remote(g_ref, o_ref.at[r - 1], send_sems, recv_sems, r - 1, to))
        for cp in cps:
            cp.start()
        for cp in cps:
            cp.wait()

    return _pcall(body, name="small_exchange", in_specs=[ANY], out_specs=ANY,
                  out_shape=jax.ShapeDtypeStruct((7, rows, width), gs.dtype),
                  scratch_shapes=[pltpu.SemaphoreType.DMA((7,)), pltpu.SemaphoreType.DMA((7,))])(gs)


def _row_tile(rows):
    return _pick(rows, (256, 352, 128, 64, 32, 16))


def pair_add_layer(g, rb, *, name):
    _, rows, width = g.shape
    tr = _row_tile(rows)

    def body(g_ref, r_ref, o_ref, ob_ref):
        s = g_ref[...] + r_ref[...]
        o_ref[...] = s
        ob_ref[...] = s.astype(BF16)

    blk = pl.BlockSpec((None, tr, width), lambda k, i: (k, i, 0))
    return _pcall(body, name=name, grid=(4, rows // tr), in_specs=[blk, blk], out_specs=[blk, blk],
                  out_shape=[jax.ShapeDtypeStruct(g.shape, F32), jax.ShapeDtypeStruct(g.shape, BF16)],
                  compiler_params=_cp("parallel", "parallel"))(g, rb)


def chip_add_layers(s1, rb2, chip, core, *, name):
    _, rows, width = s1[0].shape
    tr = _row_tile(rows)

    def body(k_ref, c_ref, s0_ref, s1_ref, r0_ref, r1_ref, o_ref):
        first = c_ref[0] == 0
        s = jnp.where(first, s0_ref[...], s1_ref[...])
        r = jnp.where(first, r0_ref[...], r1_ref[...]).astype(F32)
        o_ref[...] = ((s + r[0]) + r[1]) + r[2]

    def s_spec(layer):
        return pl.BlockSpec((None, tr, width),
                            lambda i, k_ref, c_ref: (jnp.where(c_ref[0] == layer, k_ref[0], 0),
                                                     jnp.where(c_ref[0] == layer, i, 0), 0))

    def r_spec(layer):
        return pl.BlockSpec((3, tr, width), lambda i, k_ref, c_ref: (0, jnp.where(c_ref[0] == layer, i, 0), 0))

    gs = pltpu.PrefetchScalarGridSpec(
        num_scalar_prefetch=2, grid=(rows // tr,),
        in_specs=[s_spec(0), s_spec(1), r_spec(0), r_spec(1)],
        out_specs=pl.BlockSpec((None, tr, width), lambda i, k_ref, c_ref: (c_ref[0], i, 0)))
    return _pcall(body, name=name, grid_spec=gs, out_shape=jax.ShapeDtypeStruct((DEPTH, rows, width), F32),
                  compiler_params=_cp("parallel"))(chip, core, s1[0], s1[1], rb2[0], rb2[1])


def small_add(gs_own, slots, me):
    rows, width = gs_own.shape
    tr = _pick(rows, (64, 32, 16, 8))

    def body(me_ref, g_ref, s_ref, o_ref):
        me_v = me_ref[0]
        total = None
        for d in range(8):
            rel = jnp.bitwise_xor(me_v, d)
            val = jnp.where(rel == 0, g_ref[...], s_ref[jnp.maximum(rel - 1, 0)])
            total = val if total is None else total + val
        o_ref[...] = total

    gs = pltpu.PrefetchScalarGridSpec(
        num_scalar_prefetch=1, grid=(rows // tr,),
        in_specs=[pl.BlockSpec((tr, width), lambda i, m_ref: (i, 0)),
                  pl.BlockSpec((7, tr, width), lambda i, m_ref: (0, i, 0))],
        out_specs=pl.BlockSpec((tr, width), lambda i, m_ref: (i, 0)))
    return _pcall(body, name="small_add", grid_spec=gs, out_shape=jax.ShapeDtypeStruct((rows, width), F32),
                  compiler_params=_cp("parallel"))(me, gs_own, slots)


SHARDED = (("ffn1_w_up", (D, UPW)), ("ffn1_w_down", (DFF // 4, D)), ("w_in", (D, D_IN // 4)),
           ("conv_w", (4, BW // 4)), ("gla_w_g2", (LOW_W, 64)), ("w_branch", (3 * BW, D // 4)),
           ("w_out", (D // 4, D)), ("ffn2_w_up", (D, UPW)), ("ffn2_w_down", (DFF // 4, D)),
           ("ple_w_proj", (PLE, D // 4)), ("ple_w_gate", (D // 4, D)))
SMALL = ("ln1_g", "ln1_b", "conv_b", "lru_wa", "lru_ba", "lru_wx", "lru_bx", "lru_lambda", "gla_b_g",
         "gla_norm_g", "fox_b_f", "ln2_g", "ln2_b", "ln3_g", "ln3_b", "ple_b_gate", "ln4_g", "ln4_b")
WEIGHTS = ('ffn1_w_up', 'ffn1_w_down', 'ln1_g', 'ln1_b', 'w_in', 'conv_w', 'conv_b', 'lru_wa', 'lru_ba',
           'lru_wx', 'lru_bx', 'lru_lambda', 'gla_w_g2', 'gla_b_g', 'gla_norm_g', 'fox_b_f', 'w_branch',
           'w_out', 'ln2_g', 'ln2_b', 'ffn2_w_up', 'ffn2_w_down', 'ln3_g', 'ln3_b', 'ple_w_proj',
           'ple_w_gate', 'ple_b_gate', 'ln4_g', 'ln4_b')


def _cols_join(parts):
    return jnp.concatenate([parts[k] for k in range(4)], axis=-1)


def _cols_split(full):
    r, c4 = full.shape
    return full.reshape(r, 4, c4 // 4).transpose(1, 0, 2)


def _regroup_in(w):
    pad = jnp.zeros(w.shape[:-1] + (ZW - D_IN,), w.dtype)
    fox_q = (w[..., 2576:3088] * FOX_SCALE).astype(w.dtype)
    return jnp.concatenate([w[..., 0:2048], w[..., 2064:2576], fox_q, w[..., 3088:4112], w[..., 4120:7192],
                            w[..., 2048:2064], w[..., 4112:4120], pad], axis=-1)


_IN_RUNS = ((0, 2048, 0, 1.0), (2048, 2064, 7168, 1.0), (2064, 2576, 2048, 1.0), (2576, 3088, CQ, FOX_SCALE),
            (3088, 4112, CK, 1.0), (4112, 4120, 7184, 1.0), (4120, D_IN, 4096, 1.0))


def _regroup_out_shards(g):
    w = D_IN // 4
    shards = []
    for k in range(4):
        pieces = []
        for a, b, new, f in _IN_RUNS:
            lo, hi = max(a, k * w), min(b, (k + 1) * w)
            if lo < hi:
                piece = g[:, new + lo - a:new + hi - a]
                pieces.append(piece if f == 1.0 else piece * f)
        shards.append(jnp.concatenate(pieces, axis=1))
    return jnp.stack(shards)


def _block_diag(w):
    eye = jnp.eye(8, dtype=w.dtype)
    return (eye[:, None, :, None] * w[:, :, None, :]).reshape(BW, BW)


def _diag_blocks(dense):
    return jnp.stack([dense[64 * n:64 * (n + 1), 64 * n:64 * (n + 1)] for n in range(8)])


def _layer_weights(gw, small, l):
    w = {"up1": gw["ffn1_w_up"], "up2": gw["ffn2_w_up"],
         "dn1": gw["ffn1_w_down"].reshape(1, DFF, D), "dn2": gw["ffn2_w_down"].reshape(1, DFF, D),
         "wo": gw["w_out"].reshape(1, D, D), "wgt": gw["ple_w_gate"].reshape(1, D, D)}
    w["win"] = _regroup_in(_cols_join(gw["w_in"][0]))
    w["cw"] = _cols_join(gw["conv_w"][0])
    w["wa"] = _block_diag(small["lru_wa"][l]).astype(BF16)
    w["wx"] = _block_diag(small["lru_wx"][l]).astype(BF16)
    w["wg2p"] = jnp.pad(_cols_join(gw["gla_w_g2"][0]), ((0, 128 - LOW_W), (0, 0)))
    w["wbr"] = _cols_join(gw["w_branch"][0].reshape(4, 3, BW, D // 4))
    w["wp"] = _cols_join(gw["ple_w_proj"][0])
    for n in ("ln1_g", "ln1_b", "ln2_g", "ln2_b", "ln3_g", "ln3_b", "ln4_g", "ln4_b", "conv_b", "lru_ba",
              "lru_bx", "lru_lambda", "gla_b_g", "gla_norm_g", "ple_b_gate"):
        w[n] = small[n][l][None, :]
    w["bfp"] = jnp.pad(small["fox_b_f"][l], (LOW_W, 128 - LOW_W - FOX_H))[None, :]
    return w


def _heads_t(a):
    ht = a[:, LOW_W:LOW_W + FOX_H].T
    return ht[:, None, :], jnp.broadcast_to(ht[:, :, None], ht.shape + (128,))


def _layer_fwd(x, xb, pb, w, l):
    s = {"x0": x, "x0b": xb}
    tag = "l%d_" % l
    gate, up, act = ffn_up(xb, w["up1"], 0, name=tag + "ffn1_up")
    r1, x1, x1b = matmul_res_ln(act, w["dn1"], 0, x, w["ln1_g"], w["ln1_b"], mm_scale=0.5, name=tag + "ffn1_down")
    s.update(gate1=gate, up1=up, act1=act, r1=r1, x1=x1, x1b=x1b)
    z, zb = matmul(x1b, w["win"], also_bf16=True, tm=1024, tn=_pick(ZW, (2432,)), name=tag + "mix_in")
    xc, xcb, h, ya = lru_fwd(z, w["cw"], w["conv_b"], w["wa"], w["wx"], w["lru_ba"], w["lru_bx"],
                             w["lru_lambda"], name=tag + "lru_fwd")
    yb, states = gla_fwd(z, zb, w["wg2p"], w["gla_b_g"], w["gla_norm_g"], name=tag + "gla_fwd")
    fcum = fox_fcum(z, w["bfp"], name=tag + "fox_fcum")
    fq, fk = _heads_t(fcum)
    yc, ycf, lse = fox_fwd_t(zb, fq, fk, name=tag + "fox_fwd")
    merged = merge_fwd(ya, yb, yc, w["wbr"], z, name=tag + "merge_fwd")
    r2, x2, x2b = matmul_res_ln(merged, w["wo"], 0, x1, w["ln2_g"], w["ln2_b"], mm_scale=1.0, name=tag + "mix_out")
    s.update(z=z, zb=zb, xc=xc, xcb=xcb, h=h, ya=ya, yb=yb, states=states, fq=fq, fk=fk, yc=yc, ycf=ycf,
             lse=lse, merged=merged, r2=r2, x2=x2, x2b=x2b)
    gate, up, act = ffn_up(x2b, w["up2"], 0, name=tag + "ffn2_up")
    r3, x3, x3b = matmul_res_ln(act, w["dn2"], 0, x2, w["ln3_g"], w["ln3_b"], mm_scale=0.5, name=tag + "ffn2_down")
    s.update(gate2=gate, up2=up, act2=act, r3=r3, x3=x3, x3b=x3b)
    r4, x4, x4b = ple_fwd(x3b, x3, pb, w["wgt"], 0, w["wp"], w["ple_b_gate"], w["ln4_g"], w["ln4_b"],
                          name=tag + "ple_fwd")
    s.update(r4=r4, pb=pb)
    return x4, x4b, s


def _ffn_bwd(dy, s, w, n, xin_b, l, tag):
    k = {"1": ("r1", "ln1_g", "gate1", "up1", "act1"), "2": ("r3", "ln3_g", "gate2", "up2", "act2")}[n]
    dr, dfb, dg, db = ln_bwd(dy, s[k[0]], w[k[1]], out_scale=0.5, name=tag + "ln_bwd")
    dgate, dup = ffn_down_bwd(dfb, w["dn" + n], 0, s[k[2]], s[k[3]], name=tag + "down_bwd")
    dx = ffn_dx(dgate, dup, w["up" + n], 0, dr, name=tag + "dx")
    dwup = matmul_tn_up(xin_b, dgate, dup, name=tag + "dw_up")
    dwdn = matmul_tn(s[k[4]], dfb, name=tag + "dw_down").reshape(4, DFF // 4, D)
    return dx, dwup, dwdn, dg[0], db[0]


def _layer_bwd(dy, s, w, l):
    g = {}
    tag = "l%d_" % l
    dr4, dglb, dpeb, dg4, db4, dbg = ple_bwd(dy, s["r4"], s["x3b"], s["pb"], w["wgt"], 0, w["wp"], w["ple_b_gate"],
                                             w["ln4_g"], name=tag + "ple_bwd")
    dx3 = matmul(dglb, w["wgt"], nt=True, b_lead=(0,), res=dr4, res_scale=ALPHA, tm=1024, tn=1024,
                 name=tag + "ple_dx")
    g["ple_w_gate"] = matmul_tn(s["x3b"], dglb, name=tag + "ple_dw_gate").reshape(4, D // 4, D)
    g["ple_w_proj"] = _cols_split(matmul_tn(s["pb"], dpeb, name=tag + "ple_dw_proj"))
    g["ln4_g"], g["ln4_b"], g["ple_b_gate"] = dg4[0], db4[0], dbg[0]
    dx2, g["ffn2_w_up"], g["ffn2_w_down"], g["ln3_g"], g["ln3_b"] = _ffn_bwd(dx3, s, w, "2", s["x2b"], l,
                                                                             tag + "ffn2_")
    dr2, doutb, dg2, db2 = ln_bwd(dx2, s["r2"], w["ln2_g"], out_scale=1.0, name=tag + "mix_ln_bwd")
    g["ln2_g"], g["ln2_b"] = dg2[0], db2[0]
    g["w_out"] = matmul_tn(s["merged"], doutb, name=tag + "dw_out").reshape(4, D // 4, D)
    z, zb = s["z"], s["zb"]
    (dya, dyb, dyc, dp0, dp1, dp2, dgl0, dgl1, dgl2) = merge_bwd(
        doutb, w["wo"], 0, s["ya"], s["yb"], s["yc"], w["wbr"], z, name=tag + "merge_bwd")
    dwbr = jnp.stack([matmul_tn(s["ya"], dp0, name=tag + "dw_br0"), matmul_tn(s["yb"], dp1, name=tag + "dw_br1"),
                      matmul_tn(s["yc"], dp2, name=tag + "dw_br2")])
    g["w_branch"] = _cols_split(dwbr.reshape(3 * BW, D))
    day, dxc, dprb, dpib, dba, dbx, dlam = lru_bwd(dya, z, s["h"], s["xc"], w["wa"], w["wx"],
                                                   w["lru_ba"], w["lru_bx"], w["lru_lambda"], name=tag + "lru_bwd")
    dax, dcw, dcb = conv_bwd(dxc, z, w["cw"], name=tag + "conv_bwd")
    g["lru_wa"] = _diag_blocks(matmul_tn(s["xcb"], dprb, name=tag + "dw_lru_a"))
    g["lru_wx"] = _diag_blocks(matmul_tn(s["xcb"], dpib, name=tag + "dw_lru_x"))
    g["lru_ba"], g["lru_bx"], g["lru_lambda"] = dba[0], dbx[0], dlam[0]
    g["conv_w"], g["conv_b"] = _cols_split(dcw), dcb[0]
    dbq, dbk, dbv, dbr, dmisc_g, dpreb, dbgg, dng = gla_bwd(dyb, z, zb, s["states"], w["wg2p"], w["gla_b_g"],
                                                            w["gla_norm_g"], name=tag + "gla_bwd")
    miscb = zb[:, MISC:]
    g["gla_w_g2"] = _cols_split(matmul_tn(miscb, dpreb, name=tag + "dw_g2")[:LOW_W])
    g["gla_b_g"], g["gla_norm_g"] = dbgg[0], dng[0]
    dl = fox_delta(dyc, s["ycf"], name=tag + "fox_delta")
    t = z.shape[0]
    dlq = dl[:, :FOX_H].T[:, None, :]
    dcq, dck, dcv, dfk, dfq = fox_bwd_t(zb, dyc, s["fq"], s["fk"], s["lse"], dlq, name=tag + "fox_bwd")
    dfc = jnp.pad((dfk[:, 0, :] + dfq.reshape(FOX_H, t)).T, ((0, 0), (LOW_W, 128 - LOW_W - FOX_H)))
    dmiscb, dbf = fox_dcf(dfc, z, w["bfp"], dmisc_g, name=tag + "fox_dcf")
    g["fox_b_f"] = dbf[0, LOW_W:LOW_W + FOX_H]
    dz = jnp.concatenate([dax, day, dbq, dbk, dbv, dbr, dcq, dck, dcv, dgl0, dgl1, dgl2, dmiscb], axis=1)
    dx1 = matmul(dz, w["win"], nt=True, res=dr2, res_scale=ALPHA, tm=1024, tn=1024, tk=_pick(ZW, (2432,)),
                 name=tag + "mix_dx")
    g["w_in"] = _regroup_out_shards(matmul_tn(s["x1b"], dz, name=tag + "dw_in"))
    dx0, g["ffn1_w_up"], g["ffn1_w_down"], g["ln1_g"], g["ln1_b"] = _ffn_bwd(dx1, s, w, "1", s["x0b"], l,
                                                                             tag + "ffn1_")
    return dx0, g


def _local_step(x, p, target, gathered, small, after_last_layer=None):
    xcur = x
    xb = xcur.astype(BF16)
    layer_w, saved = [], []
    for l in range(DEPTH):
        w = _layer_weights(gathered(l, xcur), small, l)
        xcur, xb, s = _layer_fwd(xcur, xb, p[l].astype(BF16), w, l)
        layer_w.append(w)
        saved.append(s)
    dy, sq = loss_head(xcur, target, name="loss_head")
    grads = [None] * DEPTH
    for l in reversed(range(DEPTH)):
        dy, grads[l] = _layer_bwd(dy, saved[l], layer_w[l], l)
        if l == DEPTH - 1 and after_last_layer is not None:
            layer_w[l - 1]["ln4_g"] = layer_w[l - 1]["ln4_g"] + after_last_layer(grads[l])
    return 0.5 * jnp.sum(sq) / float(D), dy, grads


def kernel(x, p, ffn1_w_up, ffn1_w_down, ln1_g, ln1_b, w_in, conv_w, conv_b, lru_wa, lru_ba, lru_wx, lru_bx, lru_lambda, gla_w_g2, gla_b_g, gla_norm_g, fox_b_f, w_branch, w_out, ln2_g, ln2_b, ffn2_w_up, ffn2_w_down, ln3_g, ln3_b, ple_w_proj, ple_w_gate, ple_b_gate, ln4_g, ln4_b, loss_target, m_ffn1_w_up, m_ffn1_w_down, m_ln1_g, m_ln1_b, m_w_in, m_conv_w, m_conv_b, m_lru_wa, m_lru_ba, m_lru_wx, m_lru_bx, m_lru_lambda, m_gla_w_g2, m_gla_b_g, m_gla_norm_g, m_fox_b_f, m_w_branch, m_w_out, m_ln2_g, m_ln2_b, m_ffn2_w_up, m_ffn2_w_down, m_ln3_g, m_ln3_b, m_ple_w_proj, m_ple_w_gate, m_ple_b_gate, m_ln4_g, m_ln4_b, v_ffn1_w_up, v_ffn1_w_down, v_ln1_g, v_ln1_b, v_w_in, v_conv_w, v_conv_b, v_lru_wa, v_lru_ba, v_lru_wx, v_lru_bx, v_lru_lambda, v_gla_w_g2, v_gla_b_g, v_gla_norm_g, v_fox_b_f, v_w_branch, v_w_out, v_ln2_g, v_ln2_b, v_ffn2_w_up, v_ffn2_w_down, v_ln3_g, v_ln3_b, v_ple_w_proj, v_ple_w_gate, v_ple_b_gate, v_ln4_g, v_ln4_b):
    args = dict(locals())
    wts = {n: args[n] for n in WEIGHTS}
    mom = {n: args["m_" + n] for n in WEIGHTS}
    var = {n: args["v_" + n] for n in WEIGHTS}
    cx, cy, cc = lax.axis_index("x"), lax.axis_index("y"), lax.axis_index("c")

    names = [n for n, _ in SHARDED]
    shards = [wts[n].reshape((DEPTH,) + rc).astype(F32 if n == "conv_w" else BF16) for n, rc in SHARDED]
    first, lands = gather_first(shards)
    rest_send, rest_recv, rest_srcs, rest_lands, rest_token = gather_rest_start(shards, lands)
    small = {n: wts[n] for n in SMALL}
    small["ln1_g"] = small["ln1_g"] + rest_token[0, 0]

    def gathered(l, after):
        if l == 0:
            return dict(zip(names, first))
        return dict(zip(names, gather_rest_wait(rest_send, rest_recv, rest_srcs, rest_lands, after)))

    flight = {}

    def chip_sum(gl, layer):
        lst = [gl[n] for n in names]
        rb = pair_send(lst, layer)
        return [pair_add_layer(a, r, name="pair_add_l%d_%s" % (layer, n)) for n, a, r in zip(names, lst, rb)]

    def start_last_layer(gl):
        s1 = chip_sum(gl, DEPTH - 1)
        send_sems, recv_sems, srcs, lands, token = chip_send_start([sb for _, sb in s1], DEPTH - 1)
        flight.update(s1=[sf for sf, _ in s1], sems=(send_sems, recv_sems), srcs=srcs, lands=lands)
        return token[0, 0]

    loss_local, dx, grads = _local_step(x[0], p[:, 0], loss_target[0], gathered, small, start_last_layer)
    loss = lax.psum(loss_local, ("x", "y", "c"))
    grad_x = dx[None]

    core = jnp.reshape(cc, (1,)).astype(jnp.int32)
    chip = jnp.reshape(2 * cx + cy, (1,)).astype(jnp.int32)
    s1_first = chip_sum(grads[0], 0)
    rb2_first = chip_send([sb for _, sb in s1_first], 0)
    rb2_last = chip_send_wait(*flight["sems"], flight["srcs"], flight["lands"], dx, DEPTH - 1)
    s2 = [chip_add_layers((sf0, sf1), (r0, r1), chip, core, name="chip_add_" + n)
          for n, (sf0, _), sf1, r0, r1 in zip(names, s1_first, flight["s1"], rb2_first, rb2_last)]
    gsh = dict(zip(names, pair_share(s2)))

    pieces, spans, row = [], {}, 0
    for n in SMALL:
        flat = jnp.stack([grads[l][n] for l in range(DEPTH)]).reshape(-1)
        rows = -(-flat.shape[0] // (8 * PACK_W)) * 8
        pieces.append(jnp.pad(flat, (0, rows * PACK_W - flat.shape[0])).reshape(rows, PACK_W))
        spans[n] = (row, rows)
        row += rows
    gs = jnp.concatenate(pieces, axis=0)
    me = jnp.reshape(4 * cx + 2 * cy + cc, (1,)).astype(jnp.int32)
    gsum = small_add(gs, small_exchange(gs), me)

    gout, delta, new_m, new_v = {}, {}, {}, {}
    for n in WEIGHTS:
        shp = wts[n].shape
        if n in gsh:
            view = gsh[n].shape
            g = gsh[n]
        else:
            view = (1, DEPTH, wts[n].size // DEPTH)
            r0, rows = spans[n]
            g = gsum[r0:r0 + rows].reshape(-1)[:wts[n].size].reshape(view)
        d, mn, vn = adamw(wts[n].reshape(view), g, mom[n].reshape(view), var[n].reshape(view), name="adamw_" + n)
        gout[n], delta[n], new_m[n], new_v[n] = g.reshape(shp), d.reshape(shp), mn.reshape(shp), vn.reshape(shp)

    return (loss, grad_x, *[gout[n] for n in WEIGHTS], *[delta[n] for n in WEIGHTS],
            *[new_m[n] for n in WEIGHTS], *[new_v[n] for n in WEIGHTS])
```

```python
import functools
import math

import jax
import jax.numpy as jnp
from jax import lax
from jax.experimental import pallas as pl
from jax.experimental.pallas import tpu as pltpu

F32 = jnp.float32
BF16 = jnp.bfloat16

D = 1024
DFF = 2816
BW = 512
PLE = 256
DEPTH = 2
ALPHA = (2 * DEPTH) ** 0.25
LN_EPS = 1e-5
RMS_EPS = 1e-6
LRU_C = 8.0
GLA_TAU = 16.0
CHUNK = 64
D_IN = 7192
ZW = 7296
AX, AY, BQ, BK, BV, BR, CQ, CK, CV, G0, MISC = 0, 512, 1024, 1280, 1536, 2048, 2560, 3072, 3584, 4096, 7168
LOW_W, FOX_H = 16, 8
ADAM_LR, ADAM_B1, ADAM_B2, ADAM_EPS, ADAM_WD, ADAM_STEP = 0.001, 0.9, 0.999, 1e-08, 0.01, 10
PACK_W = 1024
VMEM_LIMIT = 56 << 20

MESH = pl.DeviceIdType.MESH
ANY = pl.BlockSpec(memory_space=pl.ANY)


def _pcall(body, **kw):
    return pl.pallas_call(body, **kw)


def _cp(*dims):
    return pltpu.CompilerParams(dimension_semantics=dims, vmem_limit_bytes=VMEM_LIMIT)


def _dot(a, b):
    return jnp.dot(a, b, preferred_element_type=F32)


def _dot_nt(a, b):
    return lax.dot_general(a, b, (((1,), (1,)), ((), ())), preferred_element_type=F32)


def _dot_tn(a, b):
    return lax.dot_general(a, b, (((0,), (0,)), ((), ())), preferred_element_type=F32)


def _dot_hi(a, b):
    return jnp.dot(a, b, preferred_element_type=F32, precision=lax.Precision.HIGHEST)


def _sigmoid(x):
    return 1.0 / (1.0 + jnp.exp(-x))


def _softplus(x):
    return jnp.maximum(x, 0.0) + jnp.log(1.0 + jnp.exp(-jnp.abs(x)))


def _log_sigmoid(x):
    return -_softplus(-x)


def _expm1(x):
    poly = x * (1.0 + x * (0.5 + x * (1.0 / 6.0 + x * (1.0 / 24.0 + x * (1.0 / 120.0 + x * (1.0 / 720.0))))))
    return jnp.where(jnp.abs(x) < 0.1, poly, jnp.exp(x) - 1.0)


_GELU_C = math.sqrt(2.0 / math.pi)


def _gelu(x):
    return 0.5 * x * (1.0 + jnp.tanh(_GELU_C * (x + 0.044715 * x * x * x)))


def _gelu_grad(x):
    t = jnp.tanh(_GELU_C * (x + 0.044715 * x * x * x))
    return 0.5 * (1.0 + t) + 0.5 * x * (1.0 - t * t) * _GELU_C * (1.0 + 3.0 * 0.044715 * x * x)


def _ln_stats(r):
    mu = jnp.mean(r, axis=-1, keepdims=True)
    xc = r - mu
    var = jnp.mean(xc * xc, axis=-1, keepdims=True)
    return xc, lax.rsqrt(var + LN_EPS)


def _pick(n, cands):
    for c in cands:
        if n % c == 0:
            return c
    return n


def _rows(tm, w, col=0):
    return pl.BlockSpec((tm, w), lambda i: (i, col))


def _fix(shape):
    nd = len(shape)
    return pl.BlockSpec(shape, lambda i: (0,) * nd)


def _col_chunks(n, width=256):
    return [slice(c, min(c + width, n)) for c in range(0, n, width)]


def _layer(l, shape):
    nd = len(shape)
    return pl.BlockSpec((None,) + tuple(shape), lambda i: (l,) + (0,) * nd)


def matmul(a, b, *, name, nt=False, b_lead=(), res=None, res_scale=1.0, also_bf16=False, tm=512, tn=512,
           tk=None):
    m, k = a.shape
    n = b.shape[-2] if nt else b.shape[-1]
    tm, tn = min(tm, m), min(tn, n)
    tk = k if tk is None else tk
    nk = k // tk
    has_res = res is not None
    lead = tuple(b_lead)
    dot = _dot_nt if nt else _dot

    def body(*refs):
        a_ref, b_ref = refs[0], refs[1]
        pos = 2
        r_ref = None
        if has_res:
            r_ref = refs[pos]
            pos += 1
        o_ref = refs[pos]
        pos += 1
        ob_ref = None
        if also_bf16:
            ob_ref = refs[pos]
            pos += 1

        def finish(v):
            if has_res:
                v = v + res_scale * r_ref[...]
            o_ref[...] = v
            if also_bf16:
                ob_ref[...] = v.astype(BF16)

        if nk == 1:
            finish(dot(a_ref[...], b_ref[...]))
            return
        acc = refs[pos]
        kk = pl.program_id(2)

        @pl.when(kk == 0)
        def _():
            acc[...] = jnp.zeros_like(acc)

        acc[...] += dot(a_ref[...], b_ref[...])

        @pl.when(kk == nk - 1)
        def _():
            finish(acc[...])

    none = (None,) * len(lead)
    if nt:
        b_spec = pl.BlockSpec(none + (tn, tk), lambda j, i, kk: lead + (j, kk))
    else:
        b_spec = pl.BlockSpec(none + (tk, tn), lambda j, i, kk: lead + (kk, j))
    in_specs = [pl.BlockSpec((tm, tk), lambda j, i, kk: (i, kk)), b_spec]
    args = [a, b]
    if has_res:
        in_specs.append(pl.BlockSpec((tm, tn), lambda j, i, kk: (i, j)))
        args.append(res)
    out_shape = [jax.ShapeDtypeStruct((m, n), F32)]
    out_specs = [pl.BlockSpec((tm, tn), lambda j, i, kk: (i, j))]
    if also_bf16:
        out_shape.append(jax.ShapeDtypeStruct((m, n), BF16))
        out_specs.append(pl.BlockSpec((tm, tn), lambda j, i, kk: (i, j)))
    out = _pcall(body, name=name, grid=(n // tn, m // tm, nk), in_specs=in_specs, out_specs=out_specs,
                 out_shape=out_shape, scratch_shapes=[pltpu.VMEM((tm, tn), F32)] if nk > 1 else [],
                 compiler_params=_cp("parallel", "parallel", "arbitrary"))(*args)
    return out if also_bf16 else out[0]


def matmul_tn(a, b, *, name):
    t, k = a.shape
    n = b.shape[1]
    tk = _pick(k, (1024, 1408, 512, 256, 128))
    tn = _pick(n, (1024, 1408, 2432, 512, 256, 128))
    tt = min(1024 if tk * tn > (1 << 20) else 2048, t)
    nt = t // tt

    def body(a_ref, b_ref, o_ref):
        @pl.when(pl.program_id(2) == 0)
        def _():
            o_ref[...] = jnp.zeros_like(o_ref)

        o_ref[...] += _dot_tn(a_ref[...], b_ref[...])

    return _pcall(body, name=name, grid=(k // tk, n // tn, nt),
                  in_specs=[pl.BlockSpec((tt, tk), lambda i, j, s: (s, i)),
                            pl.BlockSpec((tt, tn), lambda i, j, s: (s, j))],
                  out_specs=pl.BlockSpec((tk, tn), lambda i, j, s: (i, j)),
                  out_shape=jax.ShapeDtypeStruct((k, n), F32),
                  compiler_params=_cp("parallel", "parallel", "arbitrary"))(a, b)


UPW = 1408


def matmul_tn_up(a, dgate, dup, *, name):
    t, k = a.shape
    tt = min(1024, t)
    tk = 1024

    def body(a_ref, g_ref, u_ref, o_ref):
        j = pl.program_id(1)

        @pl.when(pl.program_id(2) == 0)
        def _():
            o_ref[...] = jnp.zeros_like(o_ref)

        @pl.when(j < 2)
        def _():
            o_ref[...] += _dot_tn(a_ref[...], g_ref[...])

        @pl.when(j >= 2)
        def _():
            o_ref[...] += _dot_tn(a_ref[...], u_ref[...])

    return _pcall(body, name=name, grid=(k // tk, 4, t // tt),
                  in_specs=[pl.BlockSpec((tt, tk), lambda i, j, s: (s, i)),
                            pl.BlockSpec((tt, UPW), lambda i, j, s: (jnp.where(j < 2, s, 0), jnp.minimum(j, 1))),
                            pl.BlockSpec((tt, UPW), lambda i, j, s: (jnp.where(j >= 2, s, 0), jnp.maximum(j - 2, 0)))],
                  out_specs=pl.BlockSpec((None, tk, UPW), lambda i, j, s: (j, i, 0)),
                  out_shape=jax.ShapeDtypeStruct((4, k, UPW), F32),
                  compiler_params=_cp("parallel", "parallel", "arbitrary"))(a, dgate, dup)


def ffn_dx(dgate, dup, wup, l, res, *, name):
    t = dgate.shape[0]
    tm, tn = min(1024, t), 1024

    def body(g_ref, u_ref, w_ref, r_ref, o_ref, acc):
        kk = pl.program_id(2)

        @pl.when(kk == 0)
        def _():
            acc[...] = jnp.zeros_like(acc)

        @pl.when(kk < 2)
        def _():
            acc[...] += _dot_nt(g_ref[...], w_ref[...])

        @pl.when(kk >= 2)
        def _():
            acc[...] += _dot_nt(u_ref[...], w_ref[...])

        @pl.when(kk == 3)
        def _():
            o_ref[...] = acc[...] + ALPHA * r_ref[...]

    return _pcall(body, name=name, grid=(D // tn, t // tm, 4),
                  in_specs=[pl.BlockSpec((tm, UPW), lambda j, i, kk: (i, jnp.minimum(kk, 1))),
                            pl.BlockSpec((tm, UPW), lambda j, i, kk: (i, jnp.maximum(kk - 2, 0))),
                            pl.BlockSpec((None, None, tn, UPW), lambda j, i, kk: (l, kk, j, 0)),
                            pl.BlockSpec((tm, tn), lambda j, i, kk: (i, j))],
                  out_specs=pl.BlockSpec((tm, tn), lambda j, i, kk: (i, j)),
                  out_shape=jax.ShapeDtypeStruct((t, D), F32),
                  scratch_shapes=[pltpu.VMEM((tm, tn), F32)],
                  compiler_params=_cp("parallel", "parallel", "arbitrary"))(dgate, dup, wup, res)


def ffn_up(xb, wup, l, *, name):
    t = xb.shape[0]
    tm, tn = min(1024, t), UPW

    def body(x_ref, wg_ref, wu_ref, g_ref, u_ref, a_ref):
        x = x_ref[...]
        for cols in _col_chunks(tn):
            g = _dot(x, wg_ref[:, cols])
            u = _dot(x, wu_ref[:, cols])
            g_ref[:, cols] = g.astype(BF16)
            u_ref[:, cols] = u.astype(BF16)
            a_ref[:, cols] = (g * _sigmoid(g) * u).astype(BF16)

    blk = pl.BlockSpec((tm, tn), lambda j, i: (i, j))
    return _pcall(body, name=name, grid=(DFF // tn, t // tm),
                  in_specs=[pl.BlockSpec((tm, D), lambda j, i: (i, 0)),
                            pl.BlockSpec((None, None, D, tn), lambda j, i: (l, j, 0, 0)),
                            pl.BlockSpec((None, None, D, tn), lambda j, i: (l, 2 + j, 0, 0))],
                  out_specs=[blk, blk, blk],
                  out_shape=[jax.ShapeDtypeStruct((t, DFF), BF16)] * 3,
                  compiler_params=_cp("parallel", "parallel"))(xb, wup, wup)


def matmul_res_ln(a, w, l, res, g, b, *, mm_scale, name):
    t, k = a.shape
    tm = min(512, t)

    def body(a_ref, w_ref, res_ref, g_ref, b_ref, r_ref, y_ref, yb_ref):
        f = _dot(a_ref[...], w_ref[...])
        r = ALPHA * res_ref[...] + mm_scale * f
        xc, rstd = _ln_stats(r)
        y = xc * rstd * g_ref[...] + b_ref[...]
        r_ref[...] = r
        y_ref[...] = y
        yb_ref[...] = y.astype(BF16)

    return _pcall(body, name=name, grid=(t // tm,),
                  in_specs=[_rows(tm, k), _layer(l, (k, D)), _rows(tm, D), _fix((1, D)), _fix((1, D))],
                  out_specs=[_rows(tm, D)] * 3,
                  out_shape=[jax.ShapeDtypeStruct((t, D), F32), jax.ShapeDtypeStruct((t, D), F32),
                             jax.ShapeDtypeStruct((t, D), BF16)],
                  compiler_params=_cp("parallel"))(a, w, res, g, b)


def ln_bwd(dy, r, g, *, out_scale, name):
    t = dy.shape[0]
    tm = min(256, t)

    def body(dy_ref, r_ref, g_ref, dr_ref, drb_ref, dg_ref, db_ref):
        @pl.when(pl.program_id(0) == 0)
        def _():
            dg_ref[...] = jnp.zeros_like(dg_ref)
            db_ref[...] = jnp.zeros_like(db_ref)

        xc, rstd = _ln_stats(r_ref[...])
        xhat = xc * rstd
        d = dy_ref[...]
        dxh = d * g_ref[...]
        dr = rstd * (dxh - jnp.mean(dxh, axis=-1, keepdims=True)
                     - xhat * jnp.mean(dxh * xhat, axis=-1, keepdims=True))
        dr_ref[...] = dr
        drb_ref[...] = (out_scale * dr).astype(BF16)
        dg_ref[...] += jnp.sum(d * xhat, axis=0, keepdims=True)
        db_ref[...] += jnp.sum(d, axis=0, keepdims=True)

    return _pcall(body, name=name, grid=(t // tm,),
                  in_specs=[_rows(tm, D), _rows(tm, D), _fix((1, D))],
                  out_specs=[_rows(tm, D), _rows(tm, D), _fix((1, D)), _fix((1, D))],
                  out_shape=[jax.ShapeDtypeStruct((t, D), F32), jax.ShapeDtypeStruct((t, D), BF16),
                             jax.ShapeDtypeStruct((1, D), F32), jax.ShapeDtypeStruct((1, D), F32)],
                  compiler_params=_cp("arbitrary"))(dy, r, g)


def ffn_down_bwd(dfb, wd, l, gate, up, *, name):
    t = dfb.shape[0]
    tm, tn = min(1024, t), UPW
    nj = DFF // tn

    def body(df_ref, w_ref, g_ref, u_ref, dg_ref, du_ref):
        df = df_ref[...]
        for cols in _col_chunks(tn):
            da = _dot_nt(df, w_ref[cols, :])
            g = g_ref[:, cols].astype(F32)
            s = _sigmoid(g)
            gs = g * s
            dg_ref[:, cols] = (da * u_ref[:, cols].astype(F32) * (s + gs * (1.0 - s))).astype(BF16)
            du_ref[:, cols] = (da * gs).astype(BF16)

    blk = pl.BlockSpec((tm, tn), lambda j, i: (i, j))
    return _pcall(body, name=name, grid=(nj, t // tm),
                  in_specs=[pl.BlockSpec((tm, D), lambda j, i: (i, 0)),
                            pl.BlockSpec((None, tn, D), lambda j, i: (l, j, 0)), blk, blk],
                  out_specs=[blk, blk],
                  out_shape=[jax.ShapeDtypeStruct((t, DFF), BF16), jax.ShapeDtypeStruct((t, DFF), BF16)],
                  compiler_params=_cp("parallel", "parallel"))(dfb, wd, gate, up)


def ple_fwd(xb, x, pb, wgate, l, wproj, bgate, g, b, *, name):
    t = x.shape[0]
    tm = min(512, t)

    def body(xb_ref, x_ref, p_ref, wg_ref, wp_ref, bg_ref, g_ref, b_ref, r_ref, y_ref, yb_ref):
        gl = _dot(xb_ref[...], wg_ref[...]) + bg_ref[...]
        pe = _dot(p_ref[...], wp_ref[...])
        r = ALPHA * x_ref[...] + _sigmoid(gl) * pe
        xc, rstd = _ln_stats(r)
        y = xc * rstd * g_ref[...] + b_ref[...]
        r_ref[...] = r
        y_ref[...] = y
        yb_ref[...] = y.astype(BF16)

    return _pcall(body, name=name, grid=(t // tm,),
                  in_specs=[_rows(tm, D), _rows(tm, D), _rows(tm, PLE), _layer(l, (D, D)), _fix((PLE, D)),
                            _fix((1, D)), _fix((1, D)), _fix((1, D))],
                  out_specs=[_rows(tm, D)] * 3,
                  out_shape=[jax.ShapeDtypeStruct((t, D), F32), jax.ShapeDtypeStruct((t, D), F32),
                             jax.ShapeDtypeStruct((t, D), BF16)],
                  compiler_params=_cp("parallel"))(xb, x, pb, wgate, wproj, bgate, g, b)


def ple_bwd(dy, r, xb, pb, wgate, l, wproj, bgate, g, *, name):
    t = dy.shape[0]
    tm = min(512, t)

    def body(dy_ref, r_ref, xb_ref, p_ref, wg_ref, wp_ref, bg_ref, g_ref,
             dr_ref, dgl_ref, dpe_ref, dg_ref, db_ref, dbg_ref):
        @pl.when(pl.program_id(0) == 0)
        def _():
            dg_ref[...] = jnp.zeros_like(dg_ref)
            db_ref[...] = jnp.zeros_like(db_ref)
            dbg_ref[...] = jnp.zeros_like(dbg_ref)

        xc, rstd = _ln_stats(r_ref[...])
        xhat = xc * rstd
        d = dy_ref[...]
        dxh = d * g_ref[...]
        dr = rstd * (dxh - jnp.mean(dxh, axis=-1, keepdims=True)
                     - xhat * jnp.mean(dxh * xhat, axis=-1, keepdims=True))
        s = _sigmoid(_dot(xb_ref[...], wg_ref[...]) + bg_ref[...])
        pe = _dot(p_ref[...], wp_ref[...])
        dgl = dr * pe * s * (1.0 - s)
        dr_ref[...] = dr
        dgl_ref[...] = dgl.astype(BF16)
        dpe_ref[...] = (dr * s).astype(BF16)
        dg_ref[...] += jnp.sum(d * xhat, axis=0, keepdims=True)
        db_ref[...] += jnp.sum(d, axis=0, keepdims=True)
        dbg_ref[...] += jnp.sum(dgl, axis=0, keepdims=True)

    vec = jax.ShapeDtypeStruct((1, D), F32)
    return _pcall(body, name=name, grid=(t // tm,),
                  in_specs=[_rows(tm, D), _rows(tm, D), _rows(tm, D), _rows(tm, PLE), _layer(l, (D, D)),
                            _fix((PLE, D)), _fix((1, D)), _fix((1, D))],
                  out_specs=[_rows(tm, D), _rows(tm, D), _rows(tm, D), _fix((1, D)), _fix((1, D)), _fix((1, D))],
                  out_shape=[jax.ShapeDtypeStruct((t, D), F32), jax.ShapeDtypeStruct((t, D), BF16),
                             jax.ShapeDtypeStruct((t, D), BF16), vec, vec, vec],
                  compiler_params=_cp("arbitrary"))(dy, r, xb, pb, wgate, wproj, bgate, g)


def loss_head(y, tgt, *, name):
    t = y.shape[0]
    tm = min(256, t)

    def body(y_ref, t_ref, dy_ref, sq_ref):
        @pl.when(pl.program_id(0) == 0)
        def _():
            sq_ref[...] = jnp.zeros_like(sq_ref)

        e = y_ref[...] - t_ref[...]
        dy_ref[...] = e / float(D)
        sq_ref[...] += jnp.sum(e * e, axis=0, keepdims=True)

    return _pcall(body, name=name, grid=(t // tm,),
                  in_specs=[_rows(tm, D), _rows(tm, D)],
                  out_specs=[_rows(tm, D), _fix((1, D))],
                  out_shape=[jax.ShapeDtypeStruct((t, D), F32), jax.ShapeDtypeStruct((1, D), F32)],
                  compiler_params=_cp("arbitrary"))(y, tgt)


def _lru_gates(xc, wa_ref, wx_ref, ba_ref, bx_ref, lam_ref):
    xcb = xc.astype(BF16)
    r = _sigmoid(_dot(xcb, wa_ref[...]) + ba_ref[...])
    ig = _sigmoid(_dot(xcb, wx_ref[...]) + bx_ref[...])
    sp = _softplus(-lam_ref[...])
    la = -LRU_C * r * sp
    a = jnp.exp(la)
    mult = jnp.sqrt(-_expm1(2.0 * la))
    return r, ig, sp, la, a, mult


def lru_fwd(z, cw, cb, wa, wx, ba, bx, lam, *, name):
    t = z.shape[0]
    tm = min(256, t)
    hb = tm // 8

    def body(ax_ref, prev_ref, ay_ref, cw_ref, cb_ref, wa_ref, wx_ref, ba_ref, bx_ref, lam_ref,
             xc_ref, xcb_ref, h_ref, ya_ref, xs, a_s, b_s, hc):
        i = pl.program_id(0)

        @pl.when(i == 0)
        def _():
            hc[...] = jnp.zeros_like(hc)

        xs[0:8, :] = jnp.where(i == 0, 0.0, prev_ref[...])
        xs[8:, :] = ax_ref[...]
        xc = cb_ref[...] + cw_ref[0:1, :] * xs[5:5 + tm, :]
        for k in range(1, 4):
            xc = xc + cw_ref[k:k + 1, :] * xs[5 + k:5 + k + tm, :]
        r, ig, sp, la, a, mult = _lru_gates(xc, wa_ref, wx_ref, ba_ref, bx_ref, lam_ref)
        a_s[...] = a
        b_s[...] = mult * (ig * xc)
        xc_ref[...] = xc
        xcb_ref[...] = xc.astype(BF16)

        def step(g, h):
            base = pl.multiple_of(g * 8, 8)
            a8 = a_s[pl.ds(base, 8), :]
            b8 = b_s[pl.ds(base, 8), :]
            for j in range(8):
                h = a8[j:j + 1, :] * h + b8[j:j + 1, :]
                h_ref[pl.ds(base + j, 1), :] = h
            return h

        hc[...] = lax.fori_loop(0, tm // 8, step, hc[...])
        ya_ref[...] = (_gelu(ay_ref[...]) * h_ref[...]).astype(BF16)

    vec = _fix((1, BW))
    return _pcall(body, name=name, grid=(t // tm,),
                  in_specs=[_rows(tm, BW, AX // BW),
                            pl.BlockSpec((8, BW), lambda i: (jnp.maximum(i * hb - 1, 0), AX // BW)),
                            _rows(tm, BW, AY // BW), _fix((4, BW)), vec, _fix((BW, BW)), _fix((BW, BW)),
                            vec, vec, vec],
                  out_specs=[_rows(tm, BW)] * 4,
                  out_shape=[jax.ShapeDtypeStruct((t, BW), F32), jax.ShapeDtypeStruct((t, BW), BF16),
                             jax.ShapeDtypeStruct((t, BW), F32), jax.ShapeDtypeStruct((t, BW), BF16)],
                  scratch_shapes=[pltpu.VMEM((tm + 8, BW), F32), pltpu.VMEM((tm, BW), F32),
                                  pltpu.VMEM((tm, BW), F32), pltpu.VMEM((1, BW), F32)],
                  compiler_params=_cp("arbitrary"))(z, z, z, cw, cb, wa, wx, ba, bx, lam)


def lru_bwd(dya, z, h, xc, wa, wx, ba, bx, lam, *, name):
    t = dya.shape[0]
    tm = min(256, t)
    nb = t // tm
    hb = tm // 8

    def body(dya_ref, ay_ref, h_ref, hprev_ref, xc_ref, wa_ref, wx_ref, ba_ref, bx_ref,
             lam_ref, day_ref, dxc_ref, dpr_ref, dpi_ref, dba_ref, dbx_ref, dlam_ref,
             hs, a_s, g_s, d_s, cc):
        i = pl.program_id(0)

        @pl.when(i == 0)
        def _():
            cc[...] = jnp.zeros_like(cc)
            dba_ref[...] = jnp.zeros_like(dba_ref)
            dbx_ref[...] = jnp.zeros_like(dbx_ref)
            dlam_ref[...] = jnp.zeros_like(dlam_ref)

        xc = xc_ref[...]
        r, ig, sp, la, a, mult = _lru_gates(xc, wa_ref, wx_ref, ba_ref, bx_ref, lam_ref)
        ay = ay_ref[...]
        dya = dya_ref[...]
        hcur = h_ref[...]
        day_ref[...] = (dya * hcur * _gelu_grad(ay)).astype(BF16)
        a_s[...] = a
        g_s[...] = dya * _gelu(ay)

        def step(gg, cin):
            g = tm // 8 - 1 - gg
            base = pl.multiple_of(g * 8, 8)
            a8 = a_s[pl.ds(base, 8), :]
            g8 = g_s[pl.ds(base, 8), :]
            for j in range(7, -1, -1):
                d = g8[j:j + 1, :] + cin
                d_s[pl.ds(base + j, 1), :] = d
                cin = a8[j:j + 1, :] * d
            return cin

        cc[...] = lax.fori_loop(0, tm // 8, step, cc[...])
        dht = d_s[...]
        hs[0:8, :] = jnp.where(i == nb - 1, 0.0, hprev_ref[...])
        hs[8:, :] = hcur
        da = dht * hs[7:7 + tm, :]
        dmult = dht * ig * xc
        dig = dht * mult * xc
        dla = da * a - dmult * a * a / mult
        dpr = dla * (-LRU_C * sp) * r * (1.0 - r)
        dpi = dig * ig * (1.0 - ig)
        dprb = dpr.astype(BF16)
        dpib = dpi.astype(BF16)
        dxc_ref[...] = dht * mult * ig + _dot_nt(dprb, wa_ref[...]) + _dot_nt(dpib, wx_ref[...])
        dpr_ref[...] = dprb
        dpi_ref[...] = dpib
        dba_ref[...] += jnp.sum(dpr, axis=0, keepdims=True)
        dbx_ref[...] += jnp.sum(dpi, axis=0, keepdims=True)
        dlam_ref[...] += jnp.sum(dla * (-LRU_C * r), axis=0, keepdims=True) * (-_sigmoid(-lam_ref[...]))

    vec = _fix((1, BW))
    mat = _fix((BW, BW))
    rev = lambda col: pl.BlockSpec((tm, BW), lambda i: (nb - 1 - i, col))
    vshape = jax.ShapeDtypeStruct((1, BW), F32)
    return _pcall(body, name=name, grid=(nb,),
                  in_specs=[rev(0), rev(AY // BW), rev(0),
                            pl.BlockSpec((8, BW), lambda i: (jnp.maximum((nb - 1 - i) * hb - 1, 0), 0)),
                            rev(0), mat, mat, vec, vec, vec],
                  out_specs=[rev(0), rev(0), rev(0), rev(0), vec, vec, vec],
                  out_shape=[jax.ShapeDtypeStruct((t, BW), BF16), jax.ShapeDtypeStruct((t, BW), F32),
                             jax.ShapeDtypeStruct((t, BW), BF16), jax.ShapeDtypeStruct((t, BW), BF16),
                             vshape, vshape, vshape],
                  scratch_shapes=[pltpu.VMEM((tm + 8, BW), F32), pltpu.VMEM((tm, BW), F32),
                                  pltpu.VMEM((tm, BW), F32), pltpu.VMEM((tm, BW), F32),
                                  pltpu.VMEM((1, BW), F32)],
                  compiler_params=_cp("arbitrary"))(dya, z, h, h, xc, wa, wx, ba, bx, lam)


def conv_bwd(dxc, z, cw, *, name):
    t = dxc.shape[0]
    tm = min(256, t)
    nb = t // tm
    hb = tm // 8

    def body(d_ref, dnext_ref, ax_ref, prev_ref, cw_ref, dax_ref, dcw_ref, dcb_ref, ds, xs):
        i = pl.program_id(0)

        @pl.when(i == 0)
        def _():
            dcw_ref[...] = jnp.zeros_like(dcw_ref)
            dcb_ref[...] = jnp.zeros_like(dcb_ref)

        d = d_ref[...]
        ds[0:tm, :] = d
        ds[tm:, :] = jnp.where(i == nb - 1, 0.0, dnext_ref[...])
        xs[0:8, :] = jnp.where(i == 0, 0.0, prev_ref[...])
        xs[8:, :] = ax_ref[...]
        dax = cw_ref[3:4, :] * d
        for k in range(3):
            dax = dax + cw_ref[k:k + 1, :] * ds[3 - k:3 - k + tm, :]
        dax_ref[...] = dax.astype(BF16)
        for k in range(4):
            dcw_ref[k:k + 1, :] += jnp.sum(d * xs[5 + k:5 + k + tm, :], axis=0, keepdims=True)
        dcb_ref[...] += jnp.sum(d, axis=0, keepdims=True)

    return _pcall(body, name=name, grid=(nb,),
                  in_specs=[_rows(tm, BW),
                            pl.BlockSpec((8, BW), lambda i: (jnp.minimum((i + 1) * hb, nb * hb - 1), 0)),
                            _rows(tm, BW, AX // BW),
                            pl.BlockSpec((8, BW), lambda i: (jnp.maximum(i * hb - 1, 0), AX // BW)),
                            _fix((4, BW))],
                  out_specs=[_rows(tm, BW), _fix((4, BW)), _fix((1, BW))],
                  out_shape=[jax.ShapeDtypeStruct((t, BW), BF16), jax.ShapeDtypeStruct((4, BW), F32),
                             jax.ShapeDtypeStruct((1, BW), F32)],
                  scratch_shapes=[pltpu.VMEM((tm + 8, BW), F32), pltpu.VMEM((tm + 8, BW), F32)],
                  compiler_params=_cp("arbitrary"))(dxc, dxc, z, z, cw)


GLA_CB = 4


def _gla_consts():
    tri = (jnp.arange(CHUNK)[:, None] >= jnp.arange(CHUNK)[None, :]).astype(F32)
    mask = ((jnp.arange(BW)[:, None] // 128) == (jnp.arange(256)[None, :] // 64)).astype(F32)
    return tri, mask


def gla_fwd(z, zb, wg2p, bg, ng, *, name):
    t = z.shape[0]
    tm = GLA_CB * CHUNK
    nc = t // CHUNK
    tri, mask = _gla_consts()

    def body(q_ref, k_ref, v_ref, misc_ref, br_ref, w_ref, bg_ref, ng_ref, tri_ref, mask_ref,
             yb_ref, st_ref, st):
        @pl.when(pl.program_id(0) == 0)
        def _():
            st[...] = jnp.zeros_like(st)

        for c in range(GLA_CB):
            rows = slice(c * CHUNK, (c + 1) * CHUNK)
            pre = _dot(misc_ref[rows, :], w_ref[...]) + bg_ref[...]
            la = _log_sigmoid(pre) / GLA_TAU
            gc = _dot_hi(tri_ref[...], la)
            gt = gc[CHUNK - 1:CHUNK, :]
            kdec = k_ref[rows, :] * jnp.exp(gt - gc)
            delta = _dot_tn(v_ref[rows, :], kdec.astype(BF16))
            s_new = st[...] * jnp.exp(gt) + delta * mask_ref[...]
            st[...] = s_new
            st_ref[c] = s_new
            o = _dot_nt(q_ref[rows, :], s_new.astype(BF16)) * (64.0 ** -0.5)
            br = br_ref[rows, :]
            for hd in range(4):
                cols = slice(hd * 128, (hd + 1) * 128)
                oh = o[:, cols]
                rs = lax.rsqrt(jnp.mean(oh * oh, axis=-1, keepdims=True) + RMS_EPS)
                brh = br[:, cols]
                yb_ref[rows, cols] = (oh * rs * ng_ref[:, cols] * (brh * _sigmoid(brh))).astype(BF16)

    return _pcall(body, name=name, grid=(t // tm,),
                  in_specs=[_rows(tm, 256, BQ // 256), _rows(tm, 256, BK // 256), _rows(tm, BW, BV // BW),
                            _rows(tm, 128, MISC // 128), _rows(tm, BW, BR // BW), _fix((128, 256)),
                            _fix((1, 256)), _fix((1, BW)), _fix((CHUNK, CHUNK)), _fix((BW, 256))],
                  out_specs=[_rows(tm, BW), pl.BlockSpec((GLA_CB, BW, 256), lambda i: (i, 0, 0))],
                  out_shape=[jax.ShapeDtypeStruct((t, BW), BF16), jax.ShapeDtypeStruct((nc, BW, 256), F32)],
                  scratch_shapes=[pltpu.VMEM((BW, 256), F32)],
                  compiler_params=_cp("arbitrary"))(zb, z, zb, zb, z, wg2p, bg, ng, tri, mask)


def gla_bwd(dyb, z, zb, states, wg2p, bg, ng, *, name):
    t = z.shape[0]
    tm = GLA_CB * CHUNK
    nb = t // tm
    tri, mask = _gla_consts()
    triu = tri.T

    def body(dy_ref, q_ref, k_ref, v_ref, misc_ref, br_ref, st_ref, sp_ref, w_ref, bg_ref, ng_ref,
             tri_ref, triu_ref, mask_ref,
             dq_ref, dk_ref, dv_ref, dbr_ref, dmisc_ref, dpre_ref, dbg_ref, dng_ref, cc):
        i = pl.program_id(0)

        @pl.when(i == 0)
        def _():
            cc[...] = jnp.zeros_like(cc)
            dbg_ref[...] = jnp.zeros_like(dbg_ref)
            dng_ref[...] = jnp.zeros_like(dng_ref)

        last_row = lax.broadcasted_iota(jnp.int32, (CHUNK, 256), 0) == CHUNK - 1
        for c in range(GLA_CB - 1, -1, -1):
            rows = slice(c * CHUNK, (c + 1) * CHUNK)
            pre = _dot(misc_ref[rows, :], w_ref[...]) + bg_ref[...]
            la = _log_sigmoid(pre) / GLA_TAU
            gc = _dot_hi(tri_ref[...], la)
            gt = gc[CHUNK - 1:CHUNK, :]
            eg = jnp.exp(gt - gc)
            kdec = k_ref[rows, :] * eg
            e = jnp.exp(gt)
            s_n = st_ref[c]
            if c > 0:
                s_prev = st_ref[c - 1]
            else:
                s_prev = jnp.where(i == nb - 1, 0.0, sp_ref[0])
            sb = s_n.astype(BF16)
            qb = q_ref[rows, :]
            o = _dot_nt(qb, sb) * (64.0 ** -0.5)
            br = br_ref[rows, :]
            dy = dy_ref[rows, :]
            do_parts = []
            for hd in range(4):
                cols = slice(hd * 128, (hd + 1) * 128)
                oh = o[:, cols]
                rs = lax.rsqrt(jnp.mean(oh * oh, axis=-1, keepdims=True) + RMS_EPS)
                ohat = oh * rs
                brh = br[:, cols]
                sg = _sigmoid(brh)
                dyh = dy[:, cols]
                ngh = ng_ref[:, cols]
                don = dyh * (brh * sg)
                dbr_ref[rows, cols] = (dyh * (ohat * ngh) * sg * (1.0 + brh * (1.0 - sg))).astype(BF16)
                dng_ref[:, cols] += jnp.sum(don * ohat, axis=0, keepdims=True)
                doh = don * ngh
                do_parts.append(rs * (doh - ohat * jnp.mean(doh * ohat, axis=-1, keepdims=True)))
            dob = jnp.concatenate(do_parts, axis=1).astype(BF16)
            dq_ref[rows, :] = (_dot(dob, sb) * (64.0 ** -0.5)).astype(BF16)
            dst = cc[...] + _dot_tn(dob, qb) * (64.0 ** -0.5) * mask_ref[...]
            dsb = dst.astype(BF16)
            dkdec = _dot(v_ref[rows, :], dsb)
            dv_ref[rows, :] = _dot_nt(kdec.astype(BF16), dsb).astype(BF16)
            dgt = jnp.sum(dst * s_prev, axis=0, keepdims=True) * e
            dk_ref[rows, :] = (dkdec * eg).astype(BF16)
            dd = dkdec * kdec
            dgt = dgt + jnp.sum(dd, axis=0, keepdims=True)
            dgc = jnp.where(last_row, dgt - dd, -dd)
            dla = _dot_hi(triu_ref[...], dgc)
            dpre = dla * (1.0 / GLA_TAU) * _sigmoid(-pre)
            dpb = dpre.astype(BF16)
            dpre_ref[rows, :] = dpb
            dmisc_ref[rows, :] = _dot_nt(dpb, w_ref[...])
            dbg_ref[...] += jnp.sum(dpre, axis=0, keepdims=True)
            cc[...] = dst * e

    rev = lambda w, col: pl.BlockSpec((tm, w), lambda i: (nb - 1 - i, col))
    return _pcall(body, name=name, grid=(nb,),
                  in_specs=[rev(BW, 0), rev(256, BQ // 256), rev(256, BK // 256), rev(BW, BV // BW),
                            rev(128, MISC // 128), rev(BW, BR // BW),
                            pl.BlockSpec((GLA_CB, BW, 256), lambda i: (nb - 1 - i, 0, 0)),
                            pl.BlockSpec((1, BW, 256), lambda i: (jnp.maximum((nb - 1 - i) * GLA_CB - 1, 0), 0, 0)),
                            _fix((128, 256)), _fix((1, 256)), _fix((1, BW)),
                            _fix((CHUNK, CHUNK)), _fix((CHUNK, CHUNK)), _fix((BW, 256))],
                  out_specs=[rev(256, 0), rev(256, 0), rev(BW, 0), rev(BW, 0), rev(128, 0), rev(256, 0),
                             _fix((1, 256)), _fix((1, BW))],
                  out_shape=[jax.ShapeDtypeStruct((t, 256), BF16), jax.ShapeDtypeStruct((t, 256), BF16),
                             jax.ShapeDtypeStruct((t, BW), BF16), jax.ShapeDtypeStruct((t, BW), BF16),
                             jax.ShapeDtypeStruct((t, 128), F32), jax.ShapeDtypeStruct((t, 256), BF16),
                             jax.ShapeDtypeStruct((1, 256), F32), jax.ShapeDtypeStruct((1, BW), F32)],
                  scratch_shapes=[pltpu.VMEM((BW, 256), F32)],
                  compiler_params=_cp("arbitrary"))(dyb, zb, z, zb, zb, z, states, states, wg2p, bg, ng,
                                                    tri, triu, mask)


FOX_SCALE = 64.0 ** -0.5
NEG = -1e30


def fox_fcum(z, bfp, *, name):
    t = z.shape[0]
    tm = min(256, t)
    tri = (jnp.arange(tm)[:, None] >= jnp.arange(tm)[None, :]).astype(F32)

    def body(m_ref, b_ref, tri_ref, o_ref, cc):
        @pl.when(pl.program_id(0) == 0)
        def _():
            cc[...] = jnp.zeros_like(cc)

        lf = _log_sigmoid(m_ref[...] + b_ref[...])
        cs = _dot_hi(tri_ref[...], lf) + cc[...]
        o_ref[...] = cs
        cc[...] = cs[tm - 1:tm, :]

    return _pcall(body, name=name, grid=(t // tm,),
                  in_specs=[_rows(tm, 128, MISC // 128), _fix((1, 128)), _fix((tm, tm))],
                  out_specs=_rows(tm, 128), out_shape=jax.ShapeDtypeStruct((t, 128), F32),
                  scratch_shapes=[pltpu.VMEM((1, 128), F32)],
                  compiler_params=_cp("arbitrary"))(z, bfp, tri)


def fox_dcf(dfc, z, bfp, dmisc_g, *, name):
    t = z.shape[0]
    tm = min(256, t)
    nb = t // tm
    triu = (jnp.arange(tm)[:, None] <= jnp.arange(tm)[None, :]).astype(F32)

    def body(d_ref, m_ref, b_ref, g_ref, tri_ref, o_ref, dbf_ref, cc):
        @pl.when(pl.program_id(0) == 0)
        def _():
            cc[...] = jnp.zeros_like(cc)
            dbf_ref[...] = jnp.zeros_like(dbf_ref)

        rc = _dot_hi(tri_ref[...], d_ref[...]) + cc[...]
        cc[...] = rc[0:1, :]
        dcf = rc * _sigmoid(-(m_ref[...] + b_ref[...]))
        o_ref[...] = (dcf + g_ref[...]).astype(BF16)
        dbf_ref[...] += jnp.sum(dcf, axis=0, keepdims=True)

    rev = lambda col: pl.BlockSpec((tm, 128), lambda i: (nb - 1 - i, col))
    return _pcall(body, name=name, grid=(nb,),
                  in_specs=[rev(0), rev(MISC // 128), _fix((1, 128)), rev(0), _fix((tm, tm))],
                  out_specs=[rev(0), _fix((1, 128))],
                  out_shape=[jax.ShapeDtypeStruct((t, 128), BF16), jax.ShapeDtypeStruct((1, 128), F32)],
                  scratch_shapes=[pltpu.VMEM((1, 128), F32)],
                  compiler_params=_cp("arbitrary"))(dfc, z, bfp, dmisc_g, triu)


def fox_delta(dyc, ycf, *, name):
    t = dyc.shape[0]
    tm = min(256, t)
    seg = ((jnp.arange(BW)[:, None] // 64) == jnp.arange(128)[None, :]).astype(F32)

    def body(d_ref, o_ref, s_ref, out_ref):
        out_ref[...] = _dot_hi(d_ref[...] * o_ref[...], s_ref[...])

    return _pcall(body, name=name, grid=(t // tm,),
                  in_specs=[_rows(tm, BW), _rows(tm, BW), _fix((BW, 128))],
                  out_specs=_rows(tm, 128), out_shape=jax.ShapeDtypeStruct((t, 128), F32),
                  compiler_params=_cp("parallel"))(dyc, ycf, seg)


def fox_fwd_t(zb, frow, fkb, *, name):
    t = zb.shape[0]
    tq = min(512, t)
    nq = t // tq
    rep = tq // 128

    pairs = [(i, j) for i in range(nq) for j in range(i + 1)]
    qi_tab = jnp.asarray([p[0] for p in pairs], jnp.int32)
    kj_tab = jnp.asarray([p[1] for p in pairs], jnp.int32)

    def body(qi_ref, kj_ref, q_ref, k_ref, v_ref, fq_ref, fk_ref, y_ref, yf_ref, lse_ref, m_s, l_s, acc):
        step = pl.program_id(1)
        i, j = qi_ref[step], kj_ref[step]

        @pl.when(j == 0)
        def _():
            m_s[...] = jnp.full_like(m_s, NEG)
            l_s[...] = jnp.zeros_like(l_s)
            acc[...] = jnp.zeros_like(acc)

        lo = lax.broadcasted_iota(jnp.int32, (tq, 128), 1) < 64

        def work(diagonal):
            q = q_ref[...]
            k = k_ref[...]
            v = v_ref[...]
            if diagonal:
                key = lax.broadcasted_iota(jnp.int32, (tq, tq), 0)
                qry = lax.broadcasted_iota(jnp.int32, (tq, tq), 1)
                keep = key <= qry
            for hh in range(2):
                sel = lo if hh == 0 else jnp.logical_not(lo)
                qh = jnp.where(sel, q, jnp.zeros_like(q))
                s = _dot_nt(k, qh) + fq_ref[hh] - jnp.tile(fk_ref[hh], (1, rep))
                if diagonal:
                    s = jnp.where(keep, s, NEG)
                m_old = m_s[hh]
                m_new = jnp.maximum(m_old, jnp.max(s, axis=0, keepdims=True))
                p = jnp.exp(s - m_new)
                corr = jnp.exp(m_old - m_new)
                l_s[hh] = l_s[hh] * corr + jnp.sum(p, axis=0, keepdims=True)
                m_s[hh] = m_new
                pv = _dot_tn(v, p.astype(BF16))
                rows = slice(64 * hh, 64 * hh + 64)
                acc[rows, :] = acc[rows, :] * corr + pv[rows, :]

        @pl.when(j < i)
        def _():
            work(False)

        @pl.when(j == i)
        def _():
            work(True)
            first = lax.broadcasted_iota(jnp.int32, (128, tq), 0) < 64
            out = (acc[...] * jnp.where(first, 1.0 / l_s[0], 1.0 / l_s[1])).T
            y_ref[...] = out.astype(BF16)
            yf_ref[...] = out
            lse_ref[...] = m_s[...] + jnp.log(l_s[...])

    kv = lambda off: pl.BlockSpec((tq, 128), lambda h, s, qi, kj: (kj[s], off // 128 + h))
    gs = pltpu.PrefetchScalarGridSpec(
        num_scalar_prefetch=2, grid=(4, len(pairs)),
        in_specs=[pl.BlockSpec((tq, 128), lambda h, s, qi, kj: (qi[s], CQ // 128 + h)), kv(CK), kv(CV),
                  pl.BlockSpec((2, 1, tq), lambda h, s, qi, kj: (h, 0, qi[s])),
                  pl.BlockSpec((2, tq, 128), lambda h, s, qi, kj: (h, kj[s], 0))],
        out_specs=[pl.BlockSpec((tq, 128), lambda h, s, qi, kj: (qi[s], h)),
                   pl.BlockSpec((tq, 128), lambda h, s, qi, kj: (qi[s], h)),
                   pl.BlockSpec((2, 1, tq), lambda h, s, qi, kj: (h, 0, qi[s]))],
        scratch_shapes=[pltpu.VMEM((2, 1, tq), F32), pltpu.VMEM((2, 1, tq), F32), pltpu.VMEM((128, tq), F32)])
    return _pcall(body, name=name, grid_spec=gs,
                  out_shape=[jax.ShapeDtypeStruct((t, BW), BF16), jax.ShapeDtypeStruct((t, BW), F32),
                             jax.ShapeDtypeStruct((FOX_H, 1, t), F32)],
                  compiler_params=_cp("parallel", "arbitrary"))(qi_tab, kj_tab, zb, zb, zb, frow, fkb)


def fox_bwd_t(zb, dyc, frow, fkb, lse, dl, *, name):
    t = zb.shape[0]
    tq = min(512, t)
    nq = t // tq
    rep = tq // 128

    pairs = [(j, i) for j in range(nq) for i in range(j, nq)]
    kj_tab = jnp.asarray([p[0] for p in pairs], jnp.int32)
    qi_tab = jnp.asarray([p[1] for p in pairs], jnp.int32)

    def body(kj_ref, qi_ref, q_ref, k_ref, v_ref, do_ref, fq_ref, fk_ref, lse_ref, dl_ref,
             dq_ref, dk_ref, dv_ref, dfk_ref, dfq_ref, dk_s, dv_s, df_s, dq_s):
        step = pl.program_id(1)
        j, i = kj_ref[step], qi_ref[step]

        @pl.when(step == 0)
        def _():
            dq_s[...] = jnp.zeros_like(dq_s)
            dfq_ref[...] = jnp.zeros_like(dfq_ref)

        @pl.when(i == j)
        def _():
            dk_s[...] = jnp.zeros_like(dk_s)
            dv_s[...] = jnp.zeros_like(dv_s)
            df_s[...] = jnp.zeros_like(df_s)

        lo = lax.broadcasted_iota(jnp.int32, (tq, 128), 1) < 64

        def work(diagonal):
            q = q_ref[...]
            k = k_ref[...]
            v = v_ref[...]
            dob = do_ref[...].astype(BF16)
            if diagonal:
                key = lax.broadcasted_iota(jnp.int32, (tq, tq), 0)
                qry = lax.broadcasted_iota(jnp.int32, (tq, tq), 1)
                keep = key <= qry
            dvs, dks = [], []
            for hh in range(2):
                sel = lo if hh == 0 else jnp.logical_not(lo)
                qh = jnp.where(sel, q, jnp.zeros_like(q))
                doh = jnp.where(sel, dob, jnp.zeros_like(dob))
                p = jnp.exp(_dot_nt(k, qh) + (fq_ref[hh] - lse_ref[hh]) - jnp.tile(fk_ref[hh], (1, rep)))
                if diagonal:
                    p = jnp.where(keep, p, 0.0)
                ds = p * (_dot_nt(v, doh) - dl_ref[hh])
                dsb = ds.astype(BF16)
                dvs.append(_dot(p.astype(BF16), dob))
                dks.append(_dot(dsb, q))
                rows = slice(64 * hh, 64 * hh + 64)
                dq_s[i, rows, :] += _dot_tn(k, dsb)[rows, :]
                part = ds[:, 0:128]
                for r in range(1, rep):
                    part = part + ds[:, 128 * r:128 * (r + 1)]
                df_s[hh] += part
                dfq_ref[hh, i] += jnp.sum(ds, axis=0, keepdims=True)
            dv_s[...] += jnp.where(lo, dvs[0], dvs[1])
            dk_s[...] += jnp.where(lo, dks[0], dks[1])

        @pl.when(i > j)
        def _():
            work(False)

        @pl.when(i == j)
        def _():
            work(True)
            dq_ref[...] = dq_s[i].T.astype(BF16)

        @pl.when(i == nq - 1)
        def _():
            dk_ref[...] = dk_s[...].astype(BF16)
            dv_ref[...] = dv_s[...].astype(BF16)
            for hh in range(2):
                dfk_ref[hh] = -jnp.sum(df_s[hh].T, axis=0, keepdims=True)

    row = lambda: pl.BlockSpec((2, 1, tq), lambda h, s, kj, qi: (h, 0, qi[s]))
    gs = pltpu.PrefetchScalarGridSpec(
        num_scalar_prefetch=2, grid=(4, len(pairs)),
        in_specs=[pl.BlockSpec((tq, 128), lambda h, s, kj, qi: (qi[s], CQ // 128 + h)),
                  pl.BlockSpec((tq, 128), lambda h, s, kj, qi: (kj[s], CK // 128 + h)),
                  pl.BlockSpec((tq, 128), lambda h, s, kj, qi: (kj[s], CV // 128 + h)),
                  pl.BlockSpec((tq, 128), lambda h, s, kj, qi: (qi[s], h)),
                  row(), pl.BlockSpec((2, tq, 128), lambda h, s, kj, qi: (h, kj[s], 0)), row(), row()],
        out_specs=[pl.BlockSpec((tq, 128), lambda h, s, kj, qi: (kj[s], h)),
                   pl.BlockSpec((tq, 128), lambda h, s, kj, qi: (kj[s], h)),
                   pl.BlockSpec((tq, 128), lambda h, s, kj, qi: (kj[s], h)),
                   pl.BlockSpec((2, 1, tq), lambda h, s, kj, qi: (h, 0, kj[s])),
                   pl.BlockSpec((2, nq, 1, tq), lambda h, s, kj, qi: (h, 0, 0, 0))],
        scratch_shapes=[pltpu.VMEM((tq, 128), F32), pltpu.VMEM((tq, 128), F32), pltpu.VMEM((2, tq, 128), F32),
                        pltpu.VMEM((nq, 128, tq), F32)])
    return _pcall(body, name=name, grid_spec=gs,
                  out_shape=[jax.ShapeDtypeStruct((t, BW), BF16), jax.ShapeDtypeStruct((t, BW), BF16),
                             jax.ShapeDtypeStruct((t, BW), BF16), jax.ShapeDtypeStruct((FOX_H, 1, t), F32),
                             jax.ShapeDtypeStruct((FOX_H, nq, 1, tq), F32)],
                  compiler_params=_cp("parallel", "arbitrary"))(kj_tab, qi_tab, zb, zb, zb, dyc, frow, fkb, lse, dl)


def merge_fwd(ya, yb, yc, wbr, z, *, name):
    t = ya.shape[0]
    tm = min(512, t)

    def body(ya_ref, yb_ref, yc_ref, w_ref, g0_ref, g1_ref, g2_ref, o_ref):
        m = _sigmoid(g0_ref[...]) * _dot(ya_ref[...], w_ref[0])
        m = m + _sigmoid(g1_ref[...]) * _dot(yb_ref[...], w_ref[1])
        m = m + _sigmoid(g2_ref[...]) * _dot(yc_ref[...], w_ref[2])
        o_ref[...] = m.astype(BF16)

    return _pcall(body, name=name, grid=(t // tm,),
                  in_specs=[_rows(tm, BW)] * 3 + [_fix((3, BW, D))]
                  + [_rows(tm, D, G0 // D + j) for j in range(3)],
                  out_specs=_rows(tm, D), out_shape=jax.ShapeDtypeStruct((t, D), BF16),
                  compiler_params=_cp("parallel"))(ya, yb, yc, wbr, z, z, z)


def merge_bwd(doutb, wo, l, ya, yb, yc, wbr, z, *, name):
    t = ya.shape[0]
    tm = min(256, t)

    def body(do_ref, wo_ref, ya_ref, yb_ref, yc_ref, w_ref, g0_ref, g1_ref, g2_ref,
             dya_ref, dyb_ref, dyc_ref, dp0_ref, dp1_ref, dp2_ref, dg0_ref, dg1_ref, dg2_ref):
        dm = _dot_nt(do_ref[...], wo_ref[...])
        ys = (ya_ref, yb_ref, yc_ref)
        gs = (g0_ref, g1_ref, g2_ref)
        dys = (dya_ref, dyb_ref, dyc_ref)
        dps = (dp0_ref, dp1_ref, dp2_ref)
        dgs = (dg0_ref, dg1_ref, dg2_ref)
        for j in range(3):
            s = _sigmoid(gs[j][...])
            pj = _dot(ys[j][...], w_ref[j])
            dpb = (dm * s).astype(BF16)
            dps[j][...] = dpb
            dgs[j][...] = (dm * pj * s * (1.0 - s)).astype(BF16)
            dys[j][...] = _dot_nt(dpb, w_ref[j])

    yshape = jax.ShapeDtypeStruct((t, BW), F32)
    dshape = jax.ShapeDtypeStruct((t, D), BF16)
    return _pcall(body, name=name, grid=(t // tm,),
                  in_specs=[_rows(tm, D), _layer(l, (D, D))] + [_rows(tm, BW)] * 3
                  + [_fix((3, BW, D))] + [_rows(tm, D, G0 // D + j) for j in range(3)],
                  out_specs=[_rows(tm, BW)] * 3 + [_rows(tm, D)] * 6,
                  out_shape=[yshape] * 3 + [dshape] * 6,
                  compiler_params=_cp("parallel"))(doutb, wo, ya, yb, yc, wbr, z, z, z)


def adamw(w, g, m, v, *, name):
    nl, r, c = w.shape
    tm = _row_tile(r)

    def body(w_ref, g_ref, m_ref, v_ref, d_ref, mo_ref, vo_ref):
        gg = g_ref[...]
        mn = ADAM_B1 * m_ref[...] + (1.0 - ADAM_B1) * gg
        vn = ADAM_B2 * v_ref[...] + (1.0 - ADAM_B2) * (gg * gg)
        m_hat = mn / (1.0 - ADAM_B1 ** ADAM_STEP)
        v_hat = vn / (1.0 - ADAM_B2 ** ADAM_STEP)
        d_ref[...] = -ADAM_LR * (m_hat / (jnp.sqrt(v_hat) + ADAM_EPS) + ADAM_WD * w_ref[...])
        mo_ref[...] = mn
        vo_ref[...] = vn

    shp = jax.ShapeDtypeStruct((nl, r, c), F32)
    blk = pl.BlockSpec((None, tm, c), lambda l, i: (l, i, 0))
    return _pcall(body, name=name, grid=(nl, r // tm), in_specs=[blk] * 4, out_specs=[blk] * 3,
                  out_shape=[shp] * 3, compiler_params=_cp("parallel", "parallel"))(w, g, m, v)


def _place():
    return lax.axis_index("x"), lax.axis_index("y"), lax.axis_index("c")


def _remote(src, dst, send_sems, recv_sems, k, to):
    return pltpu.make_async_remote_copy(src_ref=src, dst_ref=dst, send_sem=send_sems.at[k],
                                        recv_sem=recv_sems.at[k], device_id=to, device_id_type=MESH)


HBM = pl.BlockSpec(memory_space=pltpu.HBM)
SEM = pl.BlockSpec(memory_space=pltpu.SEMAPHORE)
EFFECT = pltpu.SideEffectType.DATAFLOW_SIDE_EFFECTING


def gather_first(shards):
    n = len(shards)

    def body(*refs):
        ins, outs, lands = refs[:n], refs[n:2 * n], refs[2 * n:3 * n]
        send_sems, recv_sems, own_send, own_recv = refs[3 * n:]
        x, y, c = _place()
        sib = (x, y, 1 - c)
        chips = [(1 - x, y), (x, 1 - y), (1 - x, 1 - y)]
        k_me = 2 * x + y
        mine = []
        for t in range(n):
            mine.append(_remote(ins[t].at[0], outs[t].at[0, k_me], own_send, own_recv, 2 * t, sib))
            mine.append(_remote(ins[t].at[1], lands[t].at[0, k_me], own_send, own_recv, 2 * t + 1, sib))
        for cp in mine:
            cp.start()

        def slot(t, chip):
            return outs[t].at[0, 2 * chip[0] + chip[1]]

        @pl.when(c == 0)
        def _():
            first = [_remote(ins[t].at[0], outs[t].at[0, k_me], send_sems, recv_sems, 6 * t + j, (*chip, 0))
                     for t in range(n) for j, chip in enumerate(chips)]
            for cp in first:
                cp.start()
            passed = []
            for t in range(n):
                for j, chip in enumerate(chips):
                    _remote(slot(t, chip), slot(t, chip), send_sems, recv_sems, 6 * t + j, (*chip, 0)).wait_recv()
                    cp = _remote(slot(t, chip), slot(t, chip), send_sems, recv_sems, 6 * t + 3 + j, sib)
                    cp.start()
                    passed.append(cp)
            for cp in first + passed:
                cp.wait_send()

        @pl.when(c == 1)
        def _():
            for t in range(n):
                for j, chip in enumerate(chips):
                    _remote(slot(t, chip), slot(t, chip), send_sems, recv_sems, 6 * t + 3 + j, sib).wait_recv()

        for cp in mine:
            cp.wait()

    shape = [jax.ShapeDtypeStruct((1, 4) + s.shape[1:], s.dtype) for s in shards]
    out = _pcall(body, name="gather_first", in_specs=[ANY] * n, out_specs=[ANY] * (2 * n), out_shape=shape + shape,
                 scratch_shapes=[pltpu.SemaphoreType.DMA((6 * n,)), pltpu.SemaphoreType.DMA((6 * n,)),
                                 pltpu.SemaphoreType.DMA((2 * n,)), pltpu.SemaphoreType.DMA((2 * n,))])(*shards)
    return out[:n], out[n:]


def _rest_copies(ins, lands, send_sems, recv_sems):
    x, y, c = _place()
    chips = [(1 - x, y), (x, 1 - y), (1 - x, 1 - y)]
    copies, arrivals = [], []
    for t in range(len(ins)):
        for j, chip in enumerate(chips):
            for to in range(2):
                copies.append(pltpu.make_async_remote_copy(
                    src_ref=ins[t].at[1], dst_ref=lands[t].at[0, 2 * x + y], send_sem=send_sems.at[6 * t + 2 * j + to],
                    recv_sem=recv_sems.at[3 * t + j], device_id=(*chip, to), device_id_type=MESH))
            blk = lands[t].at[0, 2 * chip[0] + chip[1]]
            arrivals.append(pltpu.make_async_remote_copy(
                src_ref=blk, dst_ref=blk, send_sem=send_sems.at[6 * t + 2 * j], recv_sem=recv_sems.at[3 * t + j],
                device_id=(*chip, 1), device_id_type=MESH))
    return copies, arrivals


def gather_rest_start(shards, lands):
    n = len(shards)

    def body(*refs):
        ins, lds = refs[:n], refs[n:2 * n]
        send_sems, recv_sems = refs[2 * n], refs[2 * n + 1]
        token = refs[-1]
        copies, _ = _rest_copies(ins, lds, send_sems, recv_sems)

        @pl.when(lax.axis_index("c") == 1)
        def _():
            for cp in copies:
                cp.start()

        token[...] = jnp.zeros_like(token)

    hbm = lambda a: pltpu.with_memory_space_constraint(a, pltpu.HBM)
    out = _pcall(body, name="gather_rest_start", in_specs=[HBM] * (2 * n),
                 out_specs=[SEM, SEM] + [HBM] * (2 * n) + [pl.BlockSpec(memory_space=pltpu.VMEM)],
                 out_shape=[pltpu.SemaphoreType.DMA((6 * n,)), pltpu.SemaphoreType.DMA((3 * n,))]
                 + [pltpu.HBM(a.shape, a.dtype) for a in shards] + [pltpu.HBM(a.shape, a.dtype) for a in lands]
                 + [jax.ShapeDtypeStruct((8, 128), F32)],
                 input_output_aliases={i: 2 + i for i in range(2 * n)},
                 compiler_params=pltpu.CompilerParams(has_side_effects=EFFECT))(
                     *[hbm(a) for a in shards], *[hbm(a) for a in lands])
    return out[0], out[1], out[2:2 + n], out[2 + n:2 + 2 * n], out[-1]


def gather_rest_wait(send_sems, recv_sems, srcs, lands, after):
    n = len(srcs)

    def body(*refs):
        ins, lds = refs[:n], refs[n:2 * n]
        s_sems, r_sems = refs[2 * n], refs[2 * n + 1]
        copies, arrivals = _rest_copies(ins, lds, s_sems, r_sems)

        @pl.when(lax.axis_index("c") == 1)
        def _():
            for cp in copies:
                cp.wait_send()

        for cp in arrivals:
            cp.wait_recv()

    out = _pcall(body, name="gather_rest_wait", in_specs=[HBM] * (2 * n) + [SEM, SEM, ANY],
                 out_specs=[HBM] * (2 * n),
                 out_shape=[pltpu.HBM(a.shape, a.dtype) for a in srcs] + [pltpu.HBM(a.shape, a.dtype) for a in lands],
                 input_output_aliases={i: i for i in range(2 * n)},
                 compiler_params=pltpu.CompilerParams(has_side_effects=EFFECT))(
                     *srcs, *lands, send_sems, recv_sems, after)
    return out[n:]


def pair_send(gl, owner, layer):
    n = len(gl)

    def body(*refs):
        ins, outs = refs[:n], refs[n:2 * n]
        send_sems, recv_sems = refs[2 * n:]
        x, y, c = _place()
        sib = (x, y, 1 - c)
        cps = [_remote(ins[t], outs[t], send_sems, recv_sems, t, sib) for t in range(n)]
        for core in range(2):
            @pl.when(c == core)
            def _():
                for cp in _owned(cps, owner, 1 - core, per=1):
                    cp.start()
                for cp in _owned(cps, owner, 1 - core, per=1):
                    cp.wait_send()
                for cp in _owned(cps, owner, core, per=1):
                    cp.wait_recv()

    return _pcall(body, name="pair_send_l%d" % layer, in_specs=[ANY] * n, out_specs=[ANY] * n,
                  out_shape=[jax.ShapeDtypeStruct(a.shape, a.dtype) for a in gl],
                  scratch_shapes=[pltpu.SemaphoreType.DMA((n,)), pltpu.SemaphoreType.DMA((n,))])(*gl)


def _chip_copies(ins, outs, send_sems, recv_sems):
    x, y, c = _place()
    chips = [(1 - x, y), (x, 1 - y), (1 - x, 1 - y)]
    return [_remote(ins[t].at[2 * chip[0] + chip[1]], outs[t].at[j], send_sems, recv_sems, 3 * t + j, (*chip, c))
            for t in range(len(ins)) for j, chip in enumerate(chips)]


def _owned(cps, owner, core, per=3):
    return [cp for k, cp in enumerate(cps) if owner[k // per] == core]


def chip_send(s1, owner, layer):
    n = len(s1)

    def body(*refs):
        ins, outs = refs[:n], refs[n:2 * n]
        send_sems, recv_sems = refs[2 * n:]
        cps = _chip_copies(ins, outs, send_sems, recv_sems)
        for core in range(2):
            @pl.when(lax.axis_index("c") == core)
            def _():
                for cp in _owned(cps, owner, core):
                    cp.start()
                for cp in _owned(cps, owner, core):
                    cp.wait()

    return _pcall(body, name="chip_send_l%d" % layer, in_specs=[ANY] * n, out_specs=[ANY] * n,
                  out_shape=[jax.ShapeDtypeStruct((3,) + a.shape[1:], a.dtype) for a in s1],
                  scratch_shapes=[pltpu.SemaphoreType.DMA((3 * n,)), pltpu.SemaphoreType.DMA((3 * n,))])(*s1)


def chip_send_start(s1, owner, layer):
    n = len(s1)
    land = [lax.empty((3,) + a.shape[1:], a.dtype) for a in s1]

    def body(*refs):
        ins, lands = refs[:n], refs[n:2 * n]
        send_sems, recv_sems = refs[2 * n], refs[2 * n + 1]
        token = refs[-1]
        cps = _chip_copies(ins, lands, send_sems, recv_sems)
        for core in range(2):
            @pl.when(lax.axis_index("c") == core)
            def _():
                for cp in _owned(cps, owner, core):
                    cp.start()

        token[...] = jnp.zeros_like(token)

    hbm = lambda a: pltpu.with_memory_space_constraint(a, pltpu.HBM)
    out = _pcall(body, name="chip_send_start_l%d" % layer, in_specs=[HBM] * (2 * n),
                 out_specs=[SEM, SEM] + [HBM] * (2 * n) + [pl.BlockSpec(memory_space=pltpu.VMEM)],
                 out_shape=[pltpu.SemaphoreType.DMA((3 * n,)), pltpu.SemaphoreType.DMA((3 * n,))]
                 + [pltpu.HBM(a.shape, a.dtype) for a in s1] + [pltpu.HBM(a.shape, a.dtype) for a in land]
                 + [jax.ShapeDtypeStruct((8, 128), F32)],
                 input_output_aliases={i: 2 + i for i in range(2 * n)},
                 compiler_params=pltpu.CompilerParams(has_side_effects=EFFECT))(
                     *[hbm(a) for a in s1], *[hbm(a) for a in land])
    return out[0], out[1], out[2:2 + n], out[2 + n:2 + 2 * n], out[-1]


def chip_send_wait(send_sems, recv_sems, srcs, lands, after, owner, layer):
    n = len(srcs)

    def body(*refs):
        ins, lds = refs[:n], refs[n:2 * n]
        s_sems, r_sems = refs[2 * n], refs[2 * n + 1]
        cps = _chip_copies(ins, lds, s_sems, r_sems)
        for core in range(2):
            @pl.when(lax.axis_index("c") == core)
            def _():
                for cp in _owned(cps, owner, core):
                    cp.wait_send()
                    cp.wait_recv()

    out = _pcall(body, name="chip_send_wait_l%d" % layer, in_specs=[HBM] * (2 * n) + [SEM, SEM, ANY],
                 out_specs=[HBM] * (2 * n),
                 out_shape=[pltpu.HBM(a.shape, a.dtype) for a in srcs] + [pltpu.HBM(a.shape, a.dtype) for a in lands],
                 input_output_aliases={i: i for i in range(2 * n)},
                 compiler_params=pltpu.CompilerParams(has_side_effects=EFFECT))(
                     *srcs, *lands, send_sems, recv_sems, after)
    return out[n:]


def pair_share(s2, owner):
    n = len(s2)

    def body(*refs):
        ins, outs = refs[:n], refs[n:2 * n]
        send_sems, recv_sems = refs[2 * n:]
        x, y, c = _place()
        sib = (x, y, 1 - c)
        cps = [_remote(ins[t], outs[t], send_sems, recv_sems, t, sib) for t in range(n)]
        for core in range(2):
            @pl.when(c == core)
            def _():
                for cp in _owned(cps, owner, core, per=1):
                    cp.start()
                for cp in _owned(cps, owner, core, per=1):
                    cp.wait_send()
                for cp in _owned(cps, owner, 1 - core, per=1):
                    cp.wait_recv()

    return _pcall(body, name="pair_share", in_specs=[ANY] * n, out_specs=[ANY] * n,
                  out_shape=[jax.ShapeDtypeStruct(a.shape, a.dtype) for a in s2],
                  input_output_aliases={t: t for t in range(n)},
                  scratch_shapes=[pltpu.SemaphoreType.DMA((n,)), pltpu.SemaphoreType.DMA((n,))])(*s2)


def small_exchange(gs):
    rows, width = gs.shape

    def body(g_ref, o_ref, send_sems, recv_sems):
        x, y, c = _place()
        cps = []
        for r in range(1, 8):
            dx, dy, dc = (r >> 2) & 1, (r >> 1) & 1, r & 1
            to = (x if dx == 0 else 1 - x, y if dy == 0 else 1 - y, c if dc == 0 else 1 - c)
            cps.append(_remote(g_ref, o_ref.at[r - 1], send_sems, recv_sems, r - 1, to))
        for cp in cps:
            cp.start()
        for cp in cps:
            cp.wait()

    return _pcall(body, name="small_exchange", in_specs=[ANY], out_specs=ANY,
                  out_shape=jax.ShapeDtypeStruct((7, rows, width), gs.dtype),
                  scratch_shapes=[pltpu.SemaphoreType.DMA((7,)), pltpu.SemaphoreType.DMA((7,))])(gs)


def _row_tile(rows):
    return _pick(rows, (256, 352, 128, 64, 32, 16))


def pair_add_layer(g, rb, *, name):
    _, rows, width = g.shape
    tr = _row_tile(rows)

    def body(g_ref, r_ref, o_ref, ob_ref):
        s = g_ref[...] + r_ref[...]
        o_ref[...] = s
        ob_ref[...] = s.astype(BF16)

    blk = pl.BlockSpec((None, tr, width), lambda k, i: (k, i, 0))
    return _pcall(body, name=name, grid=(4, rows // tr), in_specs=[blk, blk], out_specs=[blk, blk],
                  out_shape=[jax.ShapeDtypeStruct(g.shape, F32), jax.ShapeDtypeStruct(g.shape, BF16)],
                  compiler_params=_cp("parallel", "parallel"))(g, rb)


def chip_add_layers(s1, rb2, chip, *, name):
    _, rows, width = s1[0].shape
    tr = _row_tile(rows)

    def body(k_ref, s0_ref, s1_ref, r0_ref, r1_ref, o_ref):
        first = pl.program_id(0) == 0
        s = jnp.where(first, s0_ref[...], s1_ref[...])
        r = jnp.where(first, r0_ref[...], r1_ref[...]).astype(F32)
        o_ref[...] = ((s + r[0]) + r[1]) + r[2]

    def s_spec(layer):
        return pl.BlockSpec((None, tr, width),
                            lambda l, i, k_ref: (jnp.where(l == layer, k_ref[0], 0), jnp.where(l == layer, i, 0), 0))

    def r_spec(layer):
        return pl.BlockSpec((3, tr, width), lambda l, i, k_ref: (0, jnp.where(l == layer, i, 0), 0))

    gs = pltpu.PrefetchScalarGridSpec(
        num_scalar_prefetch=1, grid=(DEPTH, rows // tr),
        in_specs=[s_spec(0), s_spec(1), r_spec(0), r_spec(1)],
        out_specs=pl.BlockSpec((None, tr, width), lambda l, i, k_ref: (l, i, 0)))
    return _pcall(body, name=name, grid_spec=gs, out_shape=jax.ShapeDtypeStruct((DEPTH, rows, width), F32),
                  compiler_params=_cp("parallel", "parallel"))(chip, s1[0], s1[1], rb2[0], rb2[1])


def small_add(gs_own, slots, me):
    rows, width = gs_own.shape
    tr = _pick(rows, (64, 32, 16, 8))

    def body(me_ref, g_ref, s_ref, o_ref):
        me_v = me_ref[0]
        total = None
        for d in range(8):
            rel = jnp.bitwise_xor(me_v, d)
            val = jnp.where(rel == 0, g_ref[...], s_ref[jnp.maximum(rel - 1, 0)])
            total = val if total is None else total + val
        o_ref[...] = total

    gs = pltpu.PrefetchScalarGridSpec(
        num_scalar_prefetch=1, grid=(rows // tr,),
        in_specs=[pl.BlockSpec((tr, width), lambda i, m_ref: (i, 0)),
                  pl.BlockSpec((7, tr, width), lambda i, m_ref: (0, i, 0))],
        out_specs=pl.BlockSpec((tr, width), lambda i, m_ref: (i, 0)))
    return _pcall(body, name="small_add", grid_spec=gs, out_shape=jax.ShapeDtypeStruct((rows, width), F32),
                  compiler_params=_cp("parallel"))(me, gs_own, slots)


SHARDED = (("ffn1_w_up", (D, UPW)), ("ffn1_w_down", (DFF // 4, D)), ("w_in", (D, D_IN // 4)),
           ("conv_w", (4, BW // 4)), ("gla_w_g2", (LOW_W, 64)), ("w_branch", (3 * BW, D // 4)),
           ("w_out", (D // 4, D)), ("ffn2_w_up", (D, UPW)), ("ffn2_w_down", (DFF // 4, D)),
           ("ple_w_proj", (PLE, D // 4)), ("ple_w_gate", (D // 4, D)))
OWNER = tuple(0 if n in ("ffn1_w_up", "w_in", "w_out") else 1 for n, _ in SHARDED)
SMALL = ("ln1_g", "ln1_b", "conv_b", "lru_wa", "lru_ba", "lru_wx", "lru_bx", "lru_lambda", "gla_b_g",
         "gla_norm_g", "fox_b_f", "ln2_g", "ln2_b", "ln3_g", "ln3_b", "ple_b_gate", "ln4_g", "ln4_b")
WEIGHTS = ('ffn1_w_up', 'ffn1_w_down', 'ln1_g', 'ln1_b', 'w_in', 'conv_w', 'conv_b', 'lru_wa', 'lru_ba',
           'lru_wx', 'lru_bx', 'lru_lambda', 'gla_w_g2', 'gla_b_g', 'gla_norm_g', 'fox_b_f', 'w_branch',
           'w_out', 'ln2_g', 'ln2_b', 'ffn2_w_up', 'ffn2_w_down', 'ln3_g', 'ln3_b', 'ple_w_proj',
           'ple_w_gate', 'ple_b_gate', 'ln4_g', 'ln4_b')


def _cols_join(parts):
    return jnp.concatenate([parts[k] for k in range(4)], axis=-1)


def _cols_split(full):
    r, c4 = full.shape
    return full.reshape(r, 4, c4 // 4).transpose(1, 0, 2)


def _regroup_in(w):
    pad = jnp.zeros(w.shape[:-1] + (ZW - D_IN,), w.dtype)
    fox_q = (w[..., 2576:3088] * FOX_SCALE).astype(w.dtype)
    return jnp.concatenate([w[..., 0:2048], w[..., 2064:2576], fox_q, w[..., 3088:4112], w[..., 4120:7192],
                            w[..., 2048:2064], w[..., 4112:4120], pad], axis=-1)


_IN_RUNS = ((0, 2048, 0, 1.0), (2048, 2064, 7168, 1.0), (2064, 2576, 2048, 1.0), (2576, 3088, CQ, FOX_SCALE),
            (3088, 4112, CK, 1.0), (4112, 4120, 7184, 1.0), (4120, D_IN, 4096, 1.0))


def _regroup_out_shards(g):
    w = D_IN // 4
    shards = []
    for k in range(4):
        pieces = []
        for a, b, new, f in _IN_RUNS:
            lo, hi = max(a, k * w), min(b, (k + 1) * w)
            if lo < hi:
                piece = g[:, new + lo - a:new + hi - a]
                pieces.append(piece if f == 1.0 else piece * f)
        shards.append(jnp.concatenate(pieces, axis=1))
    return jnp.stack(shards)


def _block_diag(w):
    eye = jnp.eye(8, dtype=w.dtype)
    return (eye[:, None, :, None] * w[:, :, None, :]).reshape(BW, BW)


def _diag_blocks(dense):
    return jnp.stack([dense[64 * n:64 * (n + 1), 64 * n:64 * (n + 1)] for n in range(8)])


def _layer_weights(gw, small, l):
    w = {"up1": gw["ffn1_w_up"], "up2": gw["ffn2_w_up"],
         "dn1": gw["ffn1_w_down"].reshape(1, DFF, D), "dn2": gw["ffn2_w_down"].reshape(1, DFF, D),
         "wo": gw["w_out"].reshape(1, D, D), "wgt": gw["ple_w_gate"].reshape(1, D, D)}
    w["win"] = _regroup_in(_cols_join(gw["w_in"][0]))
    w["cw"] = _cols_join(gw["conv_w"][0])
    w["wa"] = _block_diag(small["lru_wa"][l]).astype(BF16)
    w["wx"] = _block_diag(small["lru_wx"][l]).astype(BF16)
    w["wg2p"] = jnp.pad(_cols_join(gw["gla_w_g2"][0]), ((0, 128 - LOW_W), (0, 0)))
    w["wbr"] = _cols_join(gw["w_branch"][0].reshape(4, 3, BW, D // 4))
    w["wp"] = _cols_join(gw["ple_w_proj"][0])
    for n in ("ln1_g", "ln1_b", "ln2_g", "ln2_b", "ln3_g", "ln3_b", "ln4_g", "ln4_b", "conv_b", "lru_ba",
              "lru_bx", "lru_lambda", "gla_b_g", "gla_norm_g", "ple_b_gate"):
        w[n] = small[n][l][None, :]
    w["bfp"] = jnp.pad(small["fox_b_f"][l], (LOW_W, 128 - LOW_W - FOX_H))[None, :]
    return w


def _heads_t(a):
    ht = a[:, LOW_W:LOW_W + FOX_H].T
    return ht[:, None, :], jnp.broadcast_to(ht[:, :, None], ht.shape + (128,))


def _layer_fwd(x, xb, pb, w, l):
    s = {"x0": x, "x0b": xb}
    tag = "l%d_" % l
    gate, up, act = ffn_up(xb, w["up1"], 0, name=tag + "ffn1_up")
    r1, x1, x1b = matmul_res_ln(act, w["dn1"], 0, x, w["ln1_g"], w["ln1_b"], mm_scale=0.5, name=tag + "ffn1_down")
    s.update(gate1=gate, up1=up, act1=act, r1=r1, x1=x1, x1b=x1b)
    z, zb = matmul(x1b, w["win"], also_bf16=True, tm=1024, tn=_pick(ZW, (2432,)), name=tag + "mix_in")
    xc, xcb, h, ya = lru_fwd(z, w["cw"], w["conv_b"], w["wa"], w["wx"], w["lru_ba"], w["lru_bx"],
                             w["lru_lambda"], name=tag + "lru_fwd")
    yb, states = gla_fwd(z, zb, w["wg2p"], w["gla_b_g"], w["gla_norm_g"], name=tag + "gla_fwd")
    fcum = fox_fcum(z, w["bfp"], name=tag + "fox_fcum")
    fq, fk = _heads_t(fcum)
    yc, ycf, lse = fox_fwd_t(zb, fq, fk, name=tag + "fox_fwd")
    merged = merge_fwd(ya, yb, yc, w["wbr"], z, name=tag + "merge_fwd")
    r2, x2, x2b = matmul_res_ln(merged, w["wo"], 0, x1, w["ln2_g"], w["ln2_b"], mm_scale=1.0, name=tag + "mix_out")
    s.update(z=z, zb=zb, xc=xc, xcb=xcb, h=h, ya=ya, yb=yb, states=states, fq=fq, fk=fk, yc=yc, ycf=ycf,
             lse=lse, merged=merged, r2=r2, x2=x2, x2b=x2b)
    gate, up, act = ffn_up(x2b, w["up2"], 0, name=tag + "ffn2_up")
    r3, x3, x3b = matmul_res_ln(act, w["dn2"], 0, x2, w["ln3_g"], w["ln3_b"], mm_scale=0.5, name=tag + "ffn2_down")
    s.update(gate2=gate, up2=up, act2=act, r3=r3, x3=x3, x3b=x3b)
    r4, x4, x4b = ple_fwd(x3b, x3, pb, w["wgt"], 0, w["wp"], w["ple_b_gate"], w["ln4_g"], w["ln4_b"],
                          name=tag + "ple_fwd")
    s.update(r4=r4, pb=pb)
    return x4, x4b, s


def _ffn_bwd(dy, s, w, n, xin_b, l, tag):
    k = {"1": ("r1", "ln1_g", "gate1", "up1", "act1"), "2": ("r3", "ln3_g", "gate2", "up2", "act2")}[n]
    dr, dfb, dg, db = ln_bwd(dy, s[k[0]], w[k[1]], out_scale=0.5, name=tag + "ln_bwd")
    dgate, dup = ffn_down_bwd(dfb, w["dn" + n], 0, s[k[2]], s[k[3]], name=tag + "down_bwd")
    dx = ffn_dx(dgate, dup, w["up" + n], 0, dr, name=tag + "dx")
    dwup = matmul_tn_up(xin_b, dgate, dup, name=tag + "dw_up")
    dwdn = matmul_tn(s[k[4]], dfb, name=tag + "dw_down").reshape(4, DFF // 4, D)
    return dx, dwup, dwdn, dg[0], db[0]


def _layer_bwd(dy, s, w, l):
    g = {}
    tag = "l%d_" % l
    dr4, dglb, dpeb, dg4, db4, dbg = ple_bwd(dy, s["r4"], s["x3b"], s["pb"], w["wgt"], 0, w["wp"], w["ple_b_gate"],
                                             w["ln4_g"], name=tag + "ple_bwd")
    dx3 = matmul(dglb, w["wgt"], nt=True, b_lead=(0,), res=dr4, res_scale=ALPHA, tm=1024, tn=1024,
                 name=tag + "ple_dx")
    g["ple_w_gate"] = matmul_tn(s["x3b"], dglb, name=tag + "ple_dw_gate").reshape(4, D // 4, D)
    g["ple_w_proj"] = _cols_split(matmul_tn(s["pb"], dpeb, name=tag + "ple_dw_proj"))
    g["ln4_g"], g["ln4_b"], g["ple_b_gate"] = dg4[0], db4[0], dbg[0]
    dx2, g["ffn2_w_up"], g["ffn2_w_down"], g["ln3_g"], g["ln3_b"] = _ffn_bwd(dx3, s, w, "2", s["x2b"], l,
                                                                             tag + "ffn2_")
    dr2, doutb, dg2, db2 = ln_bwd(dx2, s["r2"], w["ln2_g"], out_scale=1.0, name=tag + "mix_ln_bwd")
    g["ln2_g"], g["ln2_b"] = dg2[0], db2[0]
    g["w_out"] = matmul_tn(s["merged"], doutb, name=tag + "dw_out").reshape(4, D // 4, D)
    z, zb = s["z"], s["zb"]
    (dya, dyb, dyc, dp0, dp1, dp2, dgl0, dgl1, dgl2) = merge_bwd(
        doutb, w["wo"], 0, s["ya"], s["yb"], s["yc"], w["wbr"], z, name=tag + "merge_bwd")
    dwbr = jnp.stack([matmul_tn(s["ya"], dp0, name=tag + "dw_br0"), matmul_tn(s["yb"], dp1, name=tag + "dw_br1"),
                      matmul_tn(s["yc"], dp2, name=tag + "dw_br2")])
    g["w_branch"] = _cols_split(dwbr.reshape(3 * BW, D))
    day, dxc, dprb, dpib, dba, dbx, dlam = lru_bwd(dya, z, s["h"], s["xc"], w["wa"], w["wx"],
                                                   w["lru_ba"], w["lru_bx"], w["lru_lambda"], name=tag + "lru_bwd")
    dax, dcw, dcb = conv_bwd(dxc, z, w["cw"], name=tag + "conv_bwd")
    g["lru_wa"] = _diag_blocks(matmul_tn(s["xcb"], dprb, name=tag + "dw_lru_a"))
    g["lru_wx"] = _diag_blocks(matmul_tn(s["xcb"], dpib, name=tag + "dw_lru_x"))
    g["lru_ba"], g["lru_bx"], g["lru_lambda"] = dba[0], dbx[0], dlam[0]
    g["conv_w"], g["conv_b"] = _cols_split(dcw), dcb[0]
    dbq, dbk, dbv, dbr, dmisc_g, dpreb, dbgg, dng = gla_bwd(dyb, z, zb, s["states"], w["wg2p"], w["gla_b_g"],
                                                            w["gla_norm_g"], name=tag + "gla_bwd")
    miscb = zb[:, MISC:]
    g["gla_w_g2"] = _cols_split(matmul_tn(miscb, dpreb, name=tag + "dw_g2")[:LOW_W])
    g["gla_b_g"], g["gla_norm_g"] = dbgg[0], dng[0]
    dl = fox_delta(dyc, s["ycf"], name=tag + "fox_delta")
    t = z.shape[0]
    dlq = dl[:, :FOX_H].T[:, None, :]
    dcq, dck, dcv, dfk, dfq = fox_bwd_t(zb, dyc, s["fq"], s["fk"], s["lse"], dlq, name=tag + "fox_bwd")
    dfc = jnp.pad((dfk[:, 0, :] + dfq.reshape(FOX_H, t)).T, ((0, 0), (LOW_W, 128 - LOW_W - FOX_H)))
    dmiscb, dbf = fox_dcf(dfc, z, w["bfp"], dmisc_g, name=tag + "fox_dcf")
    g["fox_b_f"] = dbf[0, LOW_W:LOW_W + FOX_H]
    dz = jnp.concatenate([dax, day, dbq, dbk, dbv, dbr, dcq, dck, dcv, dgl0, dgl1, dgl2, dmiscb], axis=1)
    dx1 = matmul(dz, w["win"], nt=True, res=dr2, res_scale=ALPHA, tm=1024, tn=1024, tk=_pick(ZW, (2432,)),
                 name=tag + "mix_dx")
    g["w_in"] = _regroup_out_shards(matmul_tn(s["x1b"], dz, name=tag + "dw_in"))
    dx0, g["ffn1_w_up"], g["ffn1_w_down"], g["ln1_g"], g["ln1_b"] = _ffn_bwd(dx1, s, w, "1", s["x0b"], l,
                                                                             tag + "ffn1_")
    return dx0, g


def _local_step(x, p, target, gathered, small, after_last_layer=None):
    xcur = x
    xb = xcur.astype(BF16)
    layer_w, saved = [], []
    for l in range(DEPTH):
        w = _layer_weights(gathered(l, xcur), small, l)
        xcur, xb, s = _layer_fwd(xcur, xb, p[l].astype(BF16), w, l)
        layer_w.append(w)
        saved.append(s)
    dy, sq = loss_head(xcur, target, name="loss_head")
    grads = [None] * DEPTH
    for l in reversed(range(DEPTH)):
        dy, grads[l] = _layer_bwd(dy, saved[l], layer_w[l], l)
        if l == DEPTH - 1 and after_last_layer is not None:
            layer_w[l - 1]["ln4_g"] = layer_w[l - 1]["ln4_g"] + after_last_layer(grads[l])
    return 0.5 * jnp.sum(sq) / float(D), dy, grads


def kernel(x, p, ffn1_w_up, ffn1_w_down, ln1_g, ln1_b, w_in, conv_w, conv_b, lru_wa, lru_ba, lru_wx, lru_bx, lru_lambda, gla_w_g2, gla_b_g, gla_norm_g, fox_b_f, w_branch, w_out, ln2_g, ln2_b, ffn2_w_up, ffn2_w_down, ln3_g, ln3_b, ple_w_proj, ple_w_gate, ple_b_gate, ln4_g, ln4_b, loss_target, m_ffn1_w_up, m_ffn1_w_down, m_ln1_g, m_ln1_b, m_w_in, m_conv_w, m_conv_b, m_lru_wa, m_lru_ba, m_lru_wx, m_lru_bx, m_lru_lambda, m_gla_w_g2, m_gla_b_g, m_gla_norm_g, m_fox_b_f, m_w_branch, m_w_out, m_ln2_g, m_ln2_b, m_ffn2_w_up, m_ffn2_w_down, m_ln3_g, m_ln3_b, m_ple_w_proj, m_ple_w_gate, m_ple_b_gate, m_ln4_g, m_ln4_b, v_ffn1_w_up, v_ffn1_w_down, v_ln1_g, v_ln1_b, v_w_in, v_conv_w, v_conv_b, v_lru_wa, v_lru_ba, v_lru_wx, v_lru_bx, v_lru_lambda, v_gla_w_g2, v_gla_b_g, v_gla_norm_g, v_fox_b_f, v_w_branch, v_w_out, v_ln2_g, v_ln2_b, v_ffn2_w_up, v_ffn2_w_down, v_ln3_g, v_ln3_b, v_ple_w_proj, v_ple_w_gate, v_ple_b_gate, v_ln4_g, v_ln4_b):
    args = dict(locals())
    wts = {n: args[n] for n in WEIGHTS}
    mom = {n: args["m_" + n] for n in WEIGHTS}
    var = {n: args["v_" + n] for n in WEIGHTS}
    cx, cy, cc = lax.axis_index("x"), lax.axis_index("y"), lax.axis_index("c")

    names = [n for n, _ in SHARDED]
    shards = [wts[n].reshape((DEPTH,) + rc).astype(F32 if n == "conv_w" else BF16) for n, rc in SHARDED]
    first, lands = gather_first(shards)
    rest_send, rest_recv, rest_srcs, rest_lands, rest_token = gather_rest_start(shards, lands)
    small = {n: wts[n] for n in SMALL}
    small["ln1_g"] = small["ln1_g"] + rest_token[0, 0]

    def gathered(l, after):
        if l == 0:
            return dict(zip(names, first))
        return dict(zip(names, gather_rest_wait(rest_send, rest_recv, rest_srcs, rest_lands, after)))

    flight = {}

    def chip_sum(gl, layer):
        lst = [gl[n] for n in names]
        rb = pair_send(lst, OWNER, layer)
        return [pair_add_layer(a, r, name="pair_add_l%d_%s" % (layer, n)) for n, a, r in zip(names, lst, rb)]

    def start_last_layer(gl):
        s1 = chip_sum(gl, DEPTH - 1)
        send_sems, recv_sems, srcs, lands, token = chip_send_start([sb for _, sb in s1], OWNER, DEPTH - 1)
        flight.update(s1=[sf for sf, _ in s1], sems=(send_sems, recv_sems), srcs=srcs, lands=lands)
        return token[0, 0]

    loss_local, dx, grads = _local_step(x[0], p[:, 0], loss_target[0], gathered, small, start_last_layer)
    loss = lax.psum(loss_local, ("x", "y", "c"))
    grad_x = dx[None]

    chip = jnp.reshape(2 * cx + cy, (1,)).astype(jnp.int32)
    s1_first = chip_sum(grads[0], 0)
    rb2_first = chip_send([sb for _, sb in s1_first], OWNER, 0)
    rb2_last = chip_send_wait(*flight["sems"], flight["srcs"], flight["lands"], dx, OWNER, DEPTH - 1)
    s2 = [chip_add_layers((sf0, sf1), (r0, r1), chip, name="chip_add_" + n)
          for n, (sf0, _), sf1, r0, r1 in zip(names, s1_first, flight["s1"], rb2_first, rb2_last)]
    gsh = dict(zip(names, pair_share(s2, OWNER)))

    pieces, spans, row = [], {}, 0
    for n in SMALL:
        flat = jnp.stack([grads[l][n] for l in range(DEPTH)]).reshape(-1)
        rows = -(-flat.shape[0] // (8 * PACK_W)) * 8
        pieces.append(jnp.pad(flat, (0, rows * PACK_W - flat.shape[0])).reshape(rows, PACK_W))
        spans[n] = (row, rows)
        row += rows
    gs = jnp.concatenate(pieces, axis=0)
    me = jnp.reshape(4 * cx + 2 * cy + cc, (1,)).astype(jnp.int32)
    gsum = small_add(gs, small_exchange(gs), me)

    gout, delta, new_m, new_v = {}, {}, {}, {}
    for n in WEIGHTS:
        shp = wts[n].shape
        if n in gsh:
            view = gsh[n].shape
            g = gsh[n]
        else:
            view = (1, DEPTH, wts[n].size // DEPTH)
            r0, rows = spans[n]
            g = gsum[r0:r0 + rows].reshape(-1)[:wts[n].size].reshape(view)
        d, mn, vn = adamw(wts[n].reshape(view), g, mom[n].reshape(view), var[n].reshape(view), name="adamw_" + n)
        gout[n], delta[n], new_m[n], new_v[n] = g.reshape(shp), d.reshape(shp), mn.reshape(shp), vn.reshape(shp)

    return (loss, grad_x, *[gout[n] for n in WEIGHTS], *[delta[n] for n in WEIGHTS],
            *[new_m[n] for n in WEIGHTS], *[new_v[n] for n in WEIGHTS])
```

```python
import functools
import math

import jax
import jax.numpy as jnp
from jax import lax
from jax.experimental import pallas as pl
from jax.experimental.pallas import tpu as pltpu

F32 = jnp.float32
BF16 = jnp.bfloat16

D = 1024
DFF = 2816
BW = 512
PLE = 256
DEPTH = 2
ALPHA = (2 * DEPTH) ** 0.25
LN_EPS = 1e-5
RMS_EPS = 1e-6
LRU_C = 8.0
GLA_TAU = 16.0
CHUNK = 64
D_IN = 7192
ZW = 7296
AX, AY, BQ, BK, BV, BR, CQ, CK, CV, G0, MISC = 0, 512, 1024, 1280, 1536, 2048, 2560, 3072, 3584, 4096, 7168
LOW_W, FOX_H = 16, 8
ADAM_LR, ADAM_B1, ADAM_B2, ADAM_EPS, ADAM_WD, ADAM_STEP = 0.001, 0.9, 0.999, 1e-08, 0.01, 10
PACK_W = 1024
VMEM_LIMIT = 56 << 20

MESH = pl.DeviceIdType.MESH
ANY = pl.BlockSpec(memory_space=pl.ANY)


def _pcall(body, **kw):
    return pl.pallas_call(body, **kw)


def _cp(*dims):
    return pltpu.CompilerParams(dimension_semantics=dims, vmem_limit_bytes=VMEM_LIMIT)


def _dot(a, b):
    return jnp.dot(a, b, preferred_element_type=F32)


def _dot_nt(a, b):
    return lax.dot_general(a, b, (((1,), (1,)), ((), ())), preferred_element_type=F32)


def _dot_tn(a, b):
    return lax.dot_general(a, b, (((0,), (0,)), ((), ())), preferred_element_type=F32)


def _dot_hi(a, b):
    return jnp.dot(a, b, preferred_element_type=F32, precision=lax.Precision.HIGHEST)


def _sigmoid(x):
    return 1.0 / (1.0 + jnp.exp(-x))


def _softplus(x):
    return jnp.maximum(x, 0.0) + jnp.log(1.0 + jnp.exp(-jnp.abs(x)))


def _log_sigmoid(x):
    return -_softplus(-x)


def _expm1(x):
    poly = x * (1.0 + x * (0.5 + x * (1.0 / 6.0 + x * (1.0 / 24.0 + x * (1.0 / 120.0 + x * (1.0 / 720.0))))))
    return jnp.where(jnp.abs(x) < 0.1, poly, jnp.exp(x) - 1.0)


_GELU_C = math.sqrt(2.0 / math.pi)


def _gelu(x):
    return 0.5 * x * (1.0 + jnp.tanh(_GELU_C * (x + 0.044715 * x * x * x)))


def _gelu_grad(x):
    t = jnp.tanh(_GELU_C * (x + 0.044715 * x * x * x))
    return 0.5 * (1.0 + t) + 0.5 * x * (1.0 - t * t) * _GELU_C * (1.0 + 3.0 * 0.044715 * x * x)


def _ln_stats(r):
    mu = jnp.mean(r, axis=-1, keepdims=True)
    xc = r - mu
    var = jnp.mean(xc * xc, axis=-1, keepdims=True)
    return xc, lax.rsqrt(var + LN_EPS)


def _pick(n, cands):
    for c in cands:
        if n % c == 0:
            return c
    return n


def _rows(tm, w, col=0):
    return pl.BlockSpec((tm, w), lambda i: (i, col))


def _fix(shape):
    nd = len(shape)
    return pl.BlockSpec(shape, lambda i: (0,) * nd)


def _col_chunks(n, width=256):
    return [slice(c, min(c + width, n)) for c in range(0, n, width)]


def _layer(l, shape):
    nd = len(shape)
    return pl.BlockSpec((None,) + tuple(shape), lambda i: (l,) + (0,) * nd)


def matmul(a, b, *, name, nt=False, b_lead=(), res=None, res_scale=1.0, also_bf16=False, tm=512, tn=512,
           tk=None):
    m, k = a.shape
    n = b.shape[-2] if nt else b.shape[-1]
    tm, tn = min(tm, m), min(tn, n)
    tk = k if tk is None else tk
    nk = k // tk
    has_res = res is not None
    lead = tuple(b_lead)
    dot = _dot_nt if nt else _dot

    def body(*refs):
        a_ref, b_ref = refs[0], refs[1]
        pos = 2
        r_ref = None
        if has_res:
            r_ref = refs[pos]
            pos += 1
        o_ref = refs[pos]
        pos += 1
        ob_ref = None
        if also_bf16:
            ob_ref = refs[pos]
            pos += 1

        def finish(v):
            if has_res:
                v = v + res_scale * r_ref[...]
            o_ref[...] = v
            if also_bf16:
                ob_ref[...] = v.astype(BF16)

        if nk == 1:
            finish(dot(a_ref[...], b_ref[...]))
            return
        acc = refs[pos]
        kk = pl.program_id(2)

        @pl.when(kk == 0)
        def _():
            acc[...] = jnp.zeros_like(acc)

        acc[...] += dot(a_ref[...], b_ref[...])

        @pl.when(kk == nk - 1)
        def _():
            finish(acc[...])

    none = (None,) * len(lead)
    if nt:
        b_spec = pl.BlockSpec(none + (tn, tk), lambda j, i, kk: lead + (j, kk))
    else:
        b_spec = pl.BlockSpec(none + (tk, tn), lambda j, i, kk: lead + (kk, j))
    in_specs = [pl.BlockSpec((tm, tk), lambda j, i, kk: (i, kk)), b_spec]
    args = [a, b]
    if has_res:
        in_specs.append(pl.BlockSpec((tm, tn), lambda j, i, kk: (i, j)))
        args.append(res)
    out_shape = [jax.ShapeDtypeStruct((m, n), F32)]
    out_specs = [pl.BlockSpec((tm, tn), lambda j, i, kk: (i, j))]
    if also_bf16:
        out_shape.append(jax.ShapeDtypeStruct((m, n), BF16))
        out_specs.append(pl.BlockSpec((tm, tn), lambda j, i, kk: (i, j)))
    out = _pcall(body, name=name, grid=(n // tn, m // tm, nk), in_specs=in_specs, out_specs=out_specs,
                 out_shape=out_shape, scratch_shapes=[pltpu.VMEM((tm, tn), F32)] if nk > 1 else [],
                 compiler_params=_cp("parallel", "parallel", "arbitrary"))(*args)
    return out if also_bf16 else out[0]


def matmul_tn(a, b, *, name):
    t, k = a.shape
    n = b.shape[1]
    tk = _pick(k, (1024, 1408, 512, 256, 128))
    tn = _pick(n, (1024, 1408, 2432, 512, 256, 128))
    tt = min(1024 if tk * tn > (1 << 20) else 2048, t)
    nt = t // tt

    def body(a_ref, b_ref, o_ref):
        @pl.when(pl.program_id(2) == 0)
        def _():
            o_ref[...] = jnp.zeros_like(o_ref)

        o_ref[...] += _dot_tn(a_ref[...], b_ref[...])

    return _pcall(body, name=name, grid=(k // tk, n // tn, nt),
                  in_specs=[pl.BlockSpec((tt, tk), lambda i, j, s: (s, i)),
                            pl.BlockSpec((tt, tn), lambda i, j, s: (s, j))],
                  out_specs=pl.BlockSpec((tk, tn), lambda i, j, s: (i, j)),
                  out_shape=jax.ShapeDtypeStruct((k, n), F32),
                  compiler_params=_cp("parallel", "parallel", "arbitrary"))(a, b)


UPW = 1408


def matmul_tn_up(a, dgate, dup, *, name):
    t, k = a.shape
    tt = min(1024, t)
    tk = 1024

    def body(a_ref, g_ref, u_ref, o_ref):
        j = pl.program_id(1)

        @pl.when(pl.program_id(2) == 0)
        def _():
            o_ref[...] = jnp.zeros_like(o_ref)

        @pl.when(j < 2)
        def _():
            o_ref[...] += _dot_tn(a_ref[...], g_ref[...])

        @pl.when(j >= 2)
        def _():
            o_ref[...] += _dot_tn(a_ref[...], u_ref[...])

    return _pcall(body, name=name, grid=(k // tk, 4, t // tt),
                  in_specs=[pl.BlockSpec((tt, tk), lambda i, j, s: (s, i)),
                            pl.BlockSpec((tt, UPW), lambda i, j, s: (jnp.where(j < 2, s, 0), jnp.minimum(j, 1))),
                            pl.BlockSpec((tt, UPW), lambda i, j, s: (jnp.where(j >= 2, s, 0), jnp.maximum(j - 2, 0)))],
                  out_specs=pl.BlockSpec((None, tk, UPW), lambda i, j, s: (j, i, 0)),
                  out_shape=jax.ShapeDtypeStruct((4, k, UPW), F32),
                  compiler_params=_cp("parallel", "parallel", "arbitrary"))(a, dgate, dup)


def ffn_dx(dgate, dup, wup, l, res, *, name):
    t = dgate.shape[0]
    tm, tn = min(1024, t), 1024

    def body(g_ref, u_ref, w_ref, r_ref, o_ref, acc):
        kk = pl.program_id(2)

        @pl.when(kk == 0)
        def _():
            acc[...] = jnp.zeros_like(acc)

        @pl.when(kk < 2)
        def _():
            acc[...] += _dot_nt(g_ref[...], w_ref[...])

        @pl.when(kk >= 2)
        def _():
            acc[...] += _dot_nt(u_ref[...], w_ref[...])

        @pl.when(kk == 3)
        def _():
            o_ref[...] = acc[...] + ALPHA * r_ref[...]

    return _pcall(body, name=name, grid=(D // tn, t // tm, 4),
                  in_specs=[pl.BlockSpec((tm, UPW), lambda j, i, kk: (i, jnp.minimum(kk, 1))),
                            pl.BlockSpec((tm, UPW), lambda j, i, kk: (i, jnp.maximum(kk - 2, 0))),
                            pl.BlockSpec((None, None, tn, UPW), lambda j, i, kk: (l, kk, j, 0)),
                            pl.BlockSpec((tm, tn), lambda j, i, kk: (i, j))],
                  out_specs=pl.BlockSpec((tm, tn), lambda j, i, kk: (i, j)),
                  out_shape=jax.ShapeDtypeStruct((t, D), F32),
                  scratch_shapes=[pltpu.VMEM((tm, tn), F32)],
                  compiler_params=_cp("parallel", "parallel", "arbitrary"))(dgate, dup, wup, res)


def ffn_up(xb, wup, l, *, name):
    t = xb.shape[0]
    tm, tn = min(1024, t), UPW

    def body(x_ref, wg_ref, wu_ref, g_ref, u_ref, a_ref):
        x = x_ref[...]
        for cols in _col_chunks(tn):
            g = _dot(x, wg_ref[:, cols])
            u = _dot(x, wu_ref[:, cols])
            g_ref[:, cols] = g.astype(BF16)
            u_ref[:, cols] = u.astype(BF16)
            a_ref[:, cols] = (g * _sigmoid(g) * u).astype(BF16)

    blk = pl.BlockSpec((tm, tn), lambda j, i: (i, j))
    return _pcall(body, name=name, grid=(DFF // tn, t // tm),
                  in_specs=[pl.BlockSpec((tm, D), lambda j, i: (i, 0)),
                            pl.BlockSpec((None, None, D, tn), lambda j, i: (l, j, 0, 0)),
                            pl.BlockSpec((None, None, D, tn), lambda j, i: (l, 2 + j, 0, 0))],
                  out_specs=[blk, blk, blk],
                  out_shape=[jax.ShapeDtypeStruct((t, DFF), BF16)] * 3,
                  compiler_params=_cp("parallel", "parallel"))(xb, wup, wup)


def matmul_res_ln(a, w, l, res, g, b, *, mm_scale, name):
    t, k = a.shape
    tm = min(512, t)

    def body(a_ref, w_ref, res_ref, g_ref, b_ref, r_ref, y_ref, yb_ref):
        f = _dot(a_ref[...], w_ref[...])
        r = ALPHA * res_ref[...] + mm_scale * f
        xc, rstd = _ln_stats(r)
        y = xc * rstd * g_ref[...] + b_ref[...]
        r_ref[...] = r
        y_ref[...] = y
        yb_ref[...] = y.astype(BF16)

    return _pcall(body, name=name, grid=(t // tm,),
                  in_specs=[_rows(tm, k), _layer(l, (k, D)), _rows(tm, D), _fix((1, D)), _fix((1, D))],
                  out_specs=[_rows(tm, D)] * 3,
                  out_shape=[jax.ShapeDtypeStruct((t, D), F32), jax.ShapeDtypeStruct((t, D), F32),
                             jax.ShapeDtypeStruct((t, D), BF16)],
                  compiler_params=_cp("parallel"))(a, w, res, g, b)


def ln_bwd(dy, r, g, *, out_scale, name):
    t = dy.shape[0]
    tm = min(256, t)

    def body(dy_ref, r_ref, g_ref, dr_ref, drb_ref, dg_ref, db_ref):
        @pl.when(pl.program_id(0) == 0)
        def _():
            dg_ref[...] = jnp.zeros_like(dg_ref)
            db_ref[...] = jnp.zeros_like(db_ref)

        xc, rstd = _ln_stats(r_ref[...])
        xhat = xc * rstd
        d = dy_ref[...]
        dxh = d * g_ref[...]
        dr = rstd * (dxh - jnp.mean(dxh, axis=-1, keepdims=True)
                     - xhat * jnp.mean(dxh * xhat, axis=-1, keepdims=True))
        dr_ref[...] = dr
        drb_ref[...] = (out_scale * dr).astype(BF16)
        dg_ref[...] += jnp.sum(d * xhat, axis=0, keepdims=True)
        db_ref[...] += jnp.sum(d, axis=0, keepdims=True)

    return _pcall(body, name=name, grid=(t // tm,),
                  in_specs=[_rows(tm, D), _rows(tm, D), _fix((1, D))],
                  out_specs=[_rows(tm, D), _rows(tm, D), _fix((1, D)), _fix((1, D))],
                  out_shape=[jax.ShapeDtypeStruct((t, D), F32), jax.ShapeDtypeStruct((t, D), BF16),
                             jax.ShapeDtypeStruct((1, D), F32), jax.ShapeDtypeStruct((1, D), F32)],
                  compiler_params=_cp("arbitrary"))(dy, r, g)


def ffn_down_bwd(dfb, wd, l, gate, up, *, name):
    t = dfb.shape[0]
    tm, tn = min(1024, t), UPW
    nj = DFF // tn

    def body(df_ref, w_ref, g_ref, u_ref, dg_ref, du_ref):
        df = df_ref[...]
        for cols in _col_chunks(tn):
            da = _dot_nt(df, w_ref[cols, :])
            g = g_ref[:, cols].astype(F32)
            s = _sigmoid(g)
            gs = g * s
            dg_ref[:, cols] = (da * u_ref[:, cols].astype(F32) * (s + gs * (1.0 - s))).astype(BF16)
            du_ref[:, cols] = (da * gs).astype(BF16)

    blk = pl.BlockSpec((tm, tn), lambda j, i: (i, j))
    return _pcall(body, name=name, grid=(nj, t // tm),
                  in_specs=[pl.BlockSpec((tm, D), lambda j, i: (i, 0)),
                            pl.BlockSpec((None, tn, D), lambda j, i: (l, j, 0)), blk, blk],
                  out_specs=[blk, blk],
                  out_shape=[jax.ShapeDtypeStruct((t, DFF), BF16), jax.ShapeDtypeStruct((t, DFF), BF16)],
                  compiler_params=_cp("parallel", "parallel"))(dfb, wd, gate, up)


def ple_fwd(xb, x, pb, wgate, l, wproj, bgate, g, b, *, name):
    t = x.shape[0]
    tm = min(512, t)

    def body(xb_ref, x_ref, p_ref, wg_ref, wp_ref, bg_ref, g_ref, b_ref, r_ref, y_ref, yb_ref):
        gl = _dot(xb_ref[...], wg_ref[...]) + bg_ref[...]
        pe = _dot(p_ref[...], wp_ref[...])
        r = ALPHA * x_ref[...] + _sigmoid(gl) * pe
        xc, rstd = _ln_stats(r)
        y = xc * rstd * g_ref[...] + b_ref[...]
        r_ref[...] = r
        y_ref[...] = y
        yb_ref[...] = y.astype(BF16)

    return _pcall(body, name=name, grid=(t // tm,),
                  in_specs=[_rows(tm, D), _rows(tm, D), _rows(tm, PLE), _layer(l, (D, D)), _fix((PLE, D)),
                            _fix((1, D)), _fix((1, D)), _fix((1, D))],
                  out_specs=[_rows(tm, D)] * 3,
                  out_shape=[jax.ShapeDtypeStruct((t, D), F32), jax.ShapeDtypeStruct((t, D), F32),
                             jax.ShapeDtypeStruct((t, D), BF16)],
                  compiler_params=_cp("parallel"))(xb, x, pb, wgate, wproj, bgate, g, b)


def ple_bwd(dy, r, xb, pb, wgate, l, wproj, bgate, g, *, name):
    t = dy.shape[0]
    tm = min(512, t)

    def body(dy_ref, r_ref, xb_ref, p_ref, wg_ref, wp_ref, bg_ref, g_ref,
             dr_ref, dgl_ref, dpe_ref, dg_ref, db_ref, dbg_ref):
        @pl.when(pl.program_id(0) == 0)
        def _():
            dg_ref[...] = jnp.zeros_like(dg_ref)
            db_ref[...] = jnp.zeros_like(db_ref)
            dbg_ref[...] = jnp.zeros_like(dbg_ref)

        xc, rstd = _ln_stats(r_ref[...])
        xhat = xc * rstd
        d = dy_ref[...]
        dxh = d * g_ref[...]
        dr = rstd * (dxh - jnp.mean(dxh, axis=-1, keepdims=True)
                     - xhat * jnp.mean(dxh * xhat, axis=-1, keepdims=True))
        s = _sigmoid(_dot(xb_ref[...], wg_ref[...]) + bg_ref[...])
        pe = _dot(p_ref[...], wp_ref[...])
        dgl = dr * pe * s * (1.0 - s)
        dr_ref[...] = dr
        dgl_ref[...] = dgl.astype(BF16)
        dpe_ref[...] = (dr * s).astype(BF16)
        dg_ref[...] += jnp.sum(d * xhat, axis=0, keepdims=True)
        db_ref[...] += jnp.sum(d, axis=0, keepdims=True)
        dbg_ref[...] += jnp.sum(dgl, axis=0, keepdims=True)

    vec = jax.ShapeDtypeStruct((1, D), F32)
    return _pcall(body, name=name, grid=(t // tm,),
                  in_specs=[_rows(tm, D), _rows(tm, D), _rows(tm, D), _rows(tm, PLE), _layer(l, (D, D)),
                            _fix((PLE, D)), _fix((1, D)), _fix((1, D))],
                  out_specs=[_rows(tm, D), _rows(tm, D), _rows(tm, D), _fix((1, D)), _fix((1, D)), _fix((1, D))],
                  out_shape=[jax.ShapeDtypeStruct((t, D), F32), jax.ShapeDtypeStruct((t, D), BF16),
                             jax.ShapeDtypeStruct((t, D), BF16), vec, vec, vec],
                  compiler_params=_cp("arbitrary"))(dy, r, xb, pb, wgate, wproj, bgate, g)


def loss_head(y, tgt, *, name):
    t = y.shape[0]
    tm = min(256, t)

    def body(y_ref, t_ref, dy_ref, sq_ref):
        @pl.when(pl.program_id(0) == 0)
        def _():
            sq_ref[...] = jnp.zeros_like(sq_ref)

        e = y_ref[...] - t_ref[...]
        dy_ref[...] = e / float(D)
        sq_ref[...] += jnp.sum(e * e, axis=0, keepdims=True)

    return _pcall(body, name=name, grid=(t // tm,),
                  in_specs=[_rows(tm, D), _rows(tm, D)],
                  out_specs=[_rows(tm, D), _fix((1, D))],
                  out_shape=[jax.ShapeDtypeStruct((t, D), F32), jax.ShapeDtypeStruct((1, D), F32)],
                  compiler_params=_cp("arbitrary"))(y, tgt)


def _lru_gates(xc, wa_ref, wx_ref, ba_ref, bx_ref, lam_ref):
    xcb = xc.astype(BF16)
    r = _sigmoid(_dot(xcb, wa_ref[...]) + ba_ref[...])
    ig = _sigmoid(_dot(xcb, wx_ref[...]) + bx_ref[...])
    sp = _softplus(-lam_ref[...])
    la = -LRU_C * r * sp
    a = jnp.exp(la)
    mult = jnp.sqrt(-_expm1(2.0 * la))
    return r, ig, sp, la, a, mult


def lru_fwd(z, cw, cb, wa, wx, ba, bx, lam, *, name):
    t = z.shape[0]
    tm = min(256, t)
    hb = tm // 8

    def body(ax_ref, prev_ref, ay_ref, cw_ref, cb_ref, wa_ref, wx_ref, ba_ref, bx_ref, lam_ref,
             xc_ref, xcb_ref, h_ref, ya_ref, xs, a_s, b_s, hc):
        i = pl.program_id(0)

        @pl.when(i == 0)
        def _():
            hc[...] = jnp.zeros_like(hc)

        xs[0:8, :] = jnp.where(i == 0, 0.0, prev_ref[...])
        xs[8:, :] = ax_ref[...]
        xc = cb_ref[...] + cw_ref[0:1, :] * xs[5:5 + tm, :]
        for k in range(1, 4):
            xc = xc + cw_ref[k:k + 1, :] * xs[5 + k:5 + k + tm, :]
        r, ig, sp, la, a, mult = _lru_gates(xc, wa_ref, wx_ref, ba_ref, bx_ref, lam_ref)
        a_s[...] = a
        b_s[...] = mult * (ig * xc)
        xc_ref[...] = xc
        xcb_ref[...] = xc.astype(BF16)

        def step(g, h):
            base = pl.multiple_of(g * 8, 8)
            a8 = a_s[pl.ds(base, 8), :]
            b8 = b_s[pl.ds(base, 8), :]
            for j in range(8):
                h = a8[j:j + 1, :] * h + b8[j:j + 1, :]
                h_ref[pl.ds(base + j, 1), :] = h
            return h

        hc[...] = lax.fori_loop(0, tm // 8, step, hc[...])
        ya_ref[...] = (_gelu(ay_ref[...]) * h_ref[...]).astype(BF16)

    vec = _fix((1, BW))
    return _pcall(body, name=name, grid=(t // tm,),
                  in_specs=[_rows(tm, BW, AX // BW),
                            pl.BlockSpec((8, BW), lambda i: (jnp.maximum(i * hb - 1, 0), AX // BW)),
                            _rows(tm, BW, AY // BW), _fix((4, BW)), vec, _fix((BW, BW)), _fix((BW, BW)),
                            vec, vec, vec],
                  out_specs=[_rows(tm, BW)] * 4,
                  out_shape=[jax.ShapeDtypeStruct((t, BW), F32), jax.ShapeDtypeStruct((t, BW), BF16),
                             jax.ShapeDtypeStruct((t, BW), F32), jax.ShapeDtypeStruct((t, BW), BF16)],
                  scratch_shapes=[pltpu.VMEM((tm + 8, BW), F32), pltpu.VMEM((tm, BW), F32),
                                  pltpu.VMEM((tm, BW), F32), pltpu.VMEM((1, BW), F32)],
                  compiler_params=_cp("arbitrary"))(z, z, z, cw, cb, wa, wx, ba, bx, lam)


def lru_bwd(dya, z, h, xc, wa, wx, ba, bx, lam, *, name):
    t = dya.shape[0]
    tm = min(256, t)
    nb = t // tm
    hb = tm // 8

    def body(dya_ref, ay_ref, h_ref, hprev_ref, xc_ref, wa_ref, wx_ref, ba_ref, bx_ref,
             lam_ref, day_ref, dxc_ref, dpr_ref, dpi_ref, dba_ref, dbx_ref, dlam_ref,
             hs, a_s, g_s, d_s, cc):
        i = pl.program_id(0)

        @pl.when(i == 0)
        def _():
            cc[...] = jnp.zeros_like(cc)
            dba_ref[...] = jnp.zeros_like(dba_ref)
            dbx_ref[...] = jnp.zeros_like(dbx_ref)
            dlam_ref[...] = jnp.zeros_like(dlam_ref)

        xc = xc_ref[...]
        r, ig, sp, la, a, mult = _lru_gates(xc, wa_ref, wx_ref, ba_ref, bx_ref, lam_ref)
        ay = ay_ref[...]
        dya = dya_ref[...]
        hcur = h_ref[...]
        day_ref[...] = (dya * hcur * _gelu_grad(ay)).astype(BF16)
        a_s[...] = a
        g_s[...] = dya * _gelu(ay)

        def step(gg, cin):
            g = tm // 8 - 1 - gg
            base = pl.multiple_of(g * 8, 8)
            a8 = a_s[pl.ds(base, 8), :]
            g8 = g_s[pl.ds(base, 8), :]
            for j in range(7, -1, -1):
                d = g8[j:j + 1, :] + cin
                d_s[pl.ds(base + j, 1), :] = d
                cin = a8[j:j + 1, :] * d
            return cin

        cc[...] = lax.fori_loop(0, tm // 8, step, cc[...])
        dht = d_s[...]
        hs[0:8, :] = jnp.where(i == nb - 1, 0.0, hprev_ref[...])
        hs[8:, :] = hcur
        da = dht * hs[7:7 + tm, :]
        dmult = dht * ig * xc
        dig = dht * mult * xc
        dla = da * a - dmult * a * a / mult
        dpr = dla * (-LRU_C * sp) * r * (1.0 - r)
        dpi = dig * ig * (1.0 - ig)
        dprb = dpr.astype(BF16)
        dpib = dpi.astype(BF16)
        dxc_ref[...] = dht * mult * ig + _dot_nt(dprb, wa_ref[...]) + _dot_nt(dpib, wx_ref[...])
        dpr_ref[...] = dprb
        dpi_ref[...] = dpib
        dba_ref[...] += jnp.sum(dpr, axis=0, keepdims=True)
        dbx_ref[...] += jnp.sum(dpi, axis=0, keepdims=True)
        dlam_ref[...] += jnp.sum(dla * (-LRU_C * r), axis=0, keepdims=True) * (-_sigmoid(-lam_ref[...]))

    vec = _fix((1, BW))
    mat = _fix((BW, BW))
    rev = lambda col: pl.BlockSpec((tm, BW), lambda i: (nb - 1 - i, col))
    vshape = jax.ShapeDtypeStruct((1, BW), F32)
    return _pcall(body, name=name, grid=(nb,),
                  in_specs=[rev(0), rev(AY // BW), rev(0),
                            pl.BlockSpec((8, BW), lambda i: (jnp.maximum((nb - 1 - i) * hb - 1, 0), 0)),
                            rev(0), mat, mat, vec, vec, vec],
                  out_specs=[rev(0), rev(0), rev(0), rev(0), vec, vec, vec],
                  out_shape=[jax.ShapeDtypeStruct((t, BW), BF16), jax.ShapeDtypeStruct((t, BW), F32),
                             jax.ShapeDtypeStruct((t, BW), BF16), jax.ShapeDtypeStruct((t, BW), BF16),
                             vshape, vshape, vshape],
                  scratch_shapes=[pltpu.VMEM((tm + 8, BW), F32), pltpu.VMEM((tm, BW), F32),
                                  pltpu.VMEM((tm, BW), F32), pltpu.VMEM((tm, BW), F32),
                                  pltpu.VMEM((1, BW), F32)],
                  compiler_params=_cp("arbitrary"))(dya, z, h, h, xc, wa, wx, ba, bx, lam)


def conv_bwd(dxc, z, cw, *, name):
    t = dxc.shape[0]
    tm = min(256, t)
    nb = t // tm
    hb = tm // 8

    def body(d_ref, dnext_ref, ax_ref, prev_ref, cw_ref, dax_ref, dcw_ref, dcb_ref, ds, xs):
        i = pl.program_id(0)

        @pl.when(i == 0)
        def _():
            dcw_ref[...] = jnp.zeros_like(dcw_ref)
            dcb_ref[...] = jnp.zeros_like(dcb_ref)

        d = d_ref[...]
        ds[0:tm, :] = d
        ds[tm:, :] = jnp.where(i == nb - 1, 0.0, dnext_ref[...])
        xs[0:8, :] = jnp.where(i == 0, 0.0, prev_ref[...])
        xs[8:, :] = ax_ref[...]
        dax = cw_ref[3:4, :] * d
        for k in range(3):
            dax = dax + cw_ref[k:k + 1, :] * ds[3 - k:3 - k + tm, :]
        dax_ref[...] = dax.astype(BF16)
        for k in range(4):
            dcw_ref[k:k + 1, :] += jnp.sum(d * xs[5 + k:5 + k + tm, :], axis=0, keepdims=True)
        dcb_ref[...] += jnp.sum(d, axis=0, keepdims=True)

    return _pcall(body, name=name, grid=(nb,),
                  in_specs=[_rows(tm, BW),
                            pl.BlockSpec((8, BW), lambda i: (jnp.minimum((i + 1) * hb, nb * hb - 1), 0)),
                            _rows(tm, BW, AX // BW),
                            pl.BlockSpec((8, BW), lambda i: (jnp.maximum(i * hb - 1, 0), AX // BW)),
                            _fix((4, BW))],
                  out_specs=[_rows(tm, BW), _fix((4, BW)), _fix((1, BW))],
                  out_shape=[jax.ShapeDtypeStruct((t, BW), BF16), jax.ShapeDtypeStruct((4, BW), F32),
                             jax.ShapeDtypeStruct((1, BW), F32)],
                  scratch_shapes=[pltpu.VMEM((tm + 8, BW), F32), pltpu.VMEM((tm + 8, BW), F32)],
                  compiler_params=_cp("arbitrary"))(dxc, dxc, z, z, cw)


GLA_CB = 4


def _gla_consts():
    tri = (jnp.arange(CHUNK)[:, None] >= jnp.arange(CHUNK)[None, :]).astype(F32)
    mask = ((jnp.arange(BW)[:, None] // 128) == (jnp.arange(256)[None, :] // 64)).astype(F32)
    return tri, mask


def gla_fwd(z, zb, wg2p, bg, ng, *, name):
    t = z.shape[0]
    tm = GLA_CB * CHUNK
    nc = t // CHUNK
    tri, mask = _gla_consts()

    def body(q_ref, k_ref, v_ref, misc_ref, br_ref, w_ref, bg_ref, ng_ref, tri_ref, mask_ref,
             yb_ref, st_ref, st):
        @pl.when(pl.program_id(0) == 0)
        def _():
            st[...] = jnp.zeros_like(st)

        for c in range(GLA_CB):
            rows = slice(c * CHUNK, (c + 1) * CHUNK)
            pre = _dot(misc_ref[rows, :], w_ref[...]) + bg_ref[...]
            la = _log_sigmoid(pre) / GLA_TAU
            gc = _dot_hi(tri_ref[...], la)
            gt = gc[CHUNK - 1:CHUNK, :]
            kdec = k_ref[rows, :] * jnp.exp(gt - gc)
            delta = _dot_tn(v_ref[rows, :], kdec.astype(BF16))
            s_new = st[...] * jnp.exp(gt) + delta * mask_ref[...]
            st[...] = s_new
            st_ref[c] = s_new
            o = _dot_nt(q_ref[rows, :], s_new.astype(BF16)) * (64.0 ** -0.5)
            br = br_ref[rows, :]
            for hd in range(4):
                cols = slice(hd * 128, (hd + 1) * 128)
                oh = o[:, cols]
                rs = lax.rsqrt(jnp.mean(oh * oh, axis=-1, keepdims=True) + RMS_EPS)
                brh = br[:, cols]
                yb_ref[rows, cols] = (oh * rs * ng_ref[:, cols] * (brh * _sigmoid(brh))).astype(BF16)

    return _pcall(body, name=name, grid=(t // tm,),
                  in_specs=[_rows(tm, 256, BQ // 256), _rows(tm, 256, BK // 256), _rows(tm, BW, BV // BW),
                            _rows(tm, 128, MISC // 128), _rows(tm, BW, BR // BW), _fix((128, 256)),
                            _fix((1, 256)), _fix((1, BW)), _fix((CHUNK, CHUNK)), _fix((BW, 256))],
                  out_specs=[_rows(tm, BW), pl.BlockSpec((GLA_CB, BW, 256), lambda i: (i, 0, 0))],
                  out_shape=[jax.ShapeDtypeStruct((t, BW), BF16), jax.ShapeDtypeStruct((nc, BW, 256), F32)],
                  scratch_shapes=[pltpu.VMEM((BW, 256), F32)],
                  compiler_params=_cp("arbitrary"))(zb, z, zb, zb, z, wg2p, bg, ng, tri, mask)


def gla_bwd(dyb, z, zb, states, wg2p, bg, ng, *, name):
    t = z.shape[0]
    tm = GLA_CB * CHUNK
    nb = t // tm
    tri, mask = _gla_consts()
    triu = tri.T

    def body(dy_ref, q_ref, k_ref, v_ref, misc_ref, br_ref, st_ref, sp_ref, w_ref, bg_ref, ng_ref,
             tri_ref, triu_ref, mask_ref,
             dq_ref, dk_ref, dv_ref, dbr_ref, dmisc_ref, dpre_ref, dbg_ref, dng_ref, cc):
        i = pl.program_id(0)

        @pl.when(i == 0)
        def _():
            cc[...] = jnp.zeros_like(cc)
            dbg_ref[...] = jnp.zeros_like(dbg_ref)
            dng_ref[...] = jnp.zeros_like(dng_ref)

        last_row = lax.broadcasted_iota(jnp.int32, (CHUNK, 256), 0) == CHUNK - 1
        for c in range(GLA_CB - 1, -1, -1):
            rows = slice(c * CHUNK, (c + 1) * CHUNK)
            pre = _dot(misc_ref[rows, :], w_ref[...]) + bg_ref[...]
            la = _log_sigmoid(pre) / GLA_TAU
            gc = _dot_hi(tri_ref[...], la)
            gt = gc[CHUNK - 1:CHUNK, :]
            eg = jnp.exp(gt - gc)
            kdec = k_ref[rows, :] * eg
            e = jnp.exp(gt)
            s_n = st_ref[c]
            if c > 0:
                s_prev = st_ref[c - 1]
            else:
                s_prev = jnp.where(i == nb - 1, 0.0, sp_ref[0])
            sb = s_n.astype(BF16)
            qb = q_ref[rows, :]
            o = _dot_nt(qb, sb) * (64.0 ** -0.5)
            br = br_ref[rows, :]
            dy = dy_ref[rows, :]
            do_parts = []
            for hd in range(4):
                cols = slice(hd * 128, (hd + 1) * 128)
                oh = o[:, cols]
                rs = lax.rsqrt(jnp.mean(oh * oh, axis=-1, keepdims=True) + RMS_EPS)
                ohat = oh * rs
                brh = br[:, cols]
                sg = _sigmoid(brh)
                dyh = dy[:, cols]
                ngh = ng_ref[:, cols]
                don = dyh * (brh * sg)
                dbr_ref[rows, cols] = (dyh * (ohat * ngh) * sg * (1.0 + brh * (1.0 - sg))).astype(BF16)
                dng_ref[:, cols] += jnp.sum(don * ohat, axis=0, keepdims=True)
                doh = don * ngh
                do_parts.append(rs * (doh - ohat * jnp.mean(doh * ohat, axis=-1, keepdims=True)))
            dob = jnp.concatenate(do_parts, axis=1).astype(BF16)
            dq_ref[rows, :] = (_dot(dob, sb) * (64.0 ** -0.5)).astype(BF16)
            dst = cc[...] + _dot_tn(dob, qb) * (64.0 ** -0.5) * mask_ref[...]
            dsb = dst.astype(BF16)
            dkdec = _dot(v_ref[rows, :], dsb)
            dv_ref[rows, :] = _dot_nt(kdec.astype(BF16), dsb).astype(BF16)
            dgt = jnp.sum(dst * s_prev, axis=0, keepdims=True) * e
            dk_ref[rows, :] = (dkdec * eg).astype(BF16)
            dd = dkdec * kdec
            dgt = dgt + jnp.sum(dd, axis=0, keepdims=True)
            dgc = jnp.where(last_row, dgt - dd, -dd)
            dla = _dot_hi(triu_ref[...], dgc)
            dpre = dla * (1.0 / GLA_TAU) * _sigmoid(-pre)
            dpb = dpre.astype(BF16)
            dpre_ref[rows, :] = dpb
            dmisc_ref[rows, :] = _dot_nt(dpb, w_ref[...])
            dbg_ref[...] += jnp.sum(dpre, axis=0, keepdims=True)
            cc[...] = dst * e

    rev = lambda w, col: pl.BlockSpec((tm, w), lambda i: (nb - 1 - i, col))
    return _pcall(body, name=name, grid=(nb,),
                  in_specs=[rev(BW, 0), rev(256, BQ // 256), rev(256, BK // 256), rev(BW, BV // BW),
                            rev(128, MISC // 128), rev(BW, BR // BW),
                            pl.BlockSpec((GLA_CB, BW, 256), lambda i: (nb - 1 - i, 0, 0)),
                            pl.BlockSpec((1, BW, 256), lambda i: (jnp.maximum((nb - 1 - i) * GLA_CB - 1, 0), 0, 0)),
                            _fix((128, 256)), _fix((1, 256)), _fix((1, BW)),
                            _fix((CHUNK, CHUNK)), _fix((CHUNK, CHUNK)), _fix((BW, 256))],
                  out_specs=[rev(256, 0), rev(256, 0), rev(BW, 0), rev(BW, 0), rev(128, 0), rev(256, 0),
                             _fix((1, 256)), _fix((1, BW))],
                  out_shape=[jax.ShapeDtypeStruct((t, 256), BF16), jax.ShapeDtypeStruct((t, 256), BF16),
                             jax.ShapeDtypeStruct((t, BW), BF16), jax.ShapeDtypeStruct((t, BW), BF16),
                             jax.ShapeDtypeStruct((t, 128), F32), jax.ShapeDtypeStruct((t, 256), BF16),
                             jax.ShapeDtypeStruct((1, 256), F32), jax.ShapeDtypeStruct((1, BW), F32)],
                  scratch_shapes=[pltpu.VMEM((BW, 256), F32)],
                  compiler_params=_cp("arbitrary"))(dyb, zb, z, zb, zb, z, states, states, wg2p, bg, ng,
                                                    tri, triu, mask)


FOX_SCALE = 64.0 ** -0.5
NEG = -1e30


def fox_fcum(z, bfp, *, name):
    t = z.shape[0]
    tm = min(256, t)
    tri = (jnp.arange(tm)[:, None] >= jnp.arange(tm)[None, :]).astype(F32)

    def body(m_ref, b_ref, tri_ref, o_ref, cc):
        @pl.when(pl.program_id(0) == 0)
        def _():
            cc[...] = jnp.zeros_like(cc)

        lf = _log_sigmoid(m_ref[...] + b_ref[...])
        cs = _dot_hi(tri_ref[...], lf) + cc[...]
        o_ref[...] = cs
        cc[...] = cs[tm - 1:tm, :]

    return _pcall(body, name=name, grid=(t // tm,),
                  in_specs=[_rows(tm, 128, MISC // 128), _fix((1, 128)), _fix((tm, tm))],
                  out_specs=_rows(tm, 128), out_shape=jax.ShapeDtypeStruct((t, 128), F32),
                  scratch_shapes=[pltpu.VMEM((1, 128), F32)],
                  compiler_params=_cp("arbitrary"))(z, bfp, tri)


def fox_dcf(dfc, z, bfp, dmisc_g, *, name):
    t = z.shape[0]
    tm = min(256, t)
    nb = t // tm
    triu = (jnp.arange(tm)[:, None] <= jnp.arange(tm)[None, :]).astype(F32)

    def body(d_ref, m_ref, b_ref, g_ref, tri_ref, o_ref, dbf_ref, cc):
        @pl.when(pl.program_id(0) == 0)
        def _():
            cc[...] = jnp.zeros_like(cc)
            dbf_ref[...] = jnp.zeros_like(dbf_ref)

        rc = _dot_hi(tri_ref[...], d_ref[...]) + cc[...]
        cc[...] = rc[0:1, :]
        dcf = rc * _sigmoid(-(m_ref[...] + b_ref[...]))
        o_ref[...] = (dcf + g_ref[...]).astype(BF16)
        dbf_ref[...] += jnp.sum(dcf, axis=0, keepdims=True)

    rev = lambda col: pl.BlockSpec((tm, 128), lambda i: (nb - 1 - i, col))
    return _pcall(body, name=name, grid=(nb,),
                  in_specs=[rev(0), rev(MISC // 128), _fix((1, 128)), rev(0), _fix((tm, tm))],
                  out_specs=[rev(0), _fix((1, 128))],
                  out_shape=[jax.ShapeDtypeStruct((t, 128), BF16), jax.ShapeDtypeStruct((1, 128), F32)],
                  scratch_shapes=[pltpu.VMEM((1, 128), F32)],
                  compiler_params=_cp("arbitrary"))(dfc, z, bfp, dmisc_g, triu)


def fox_delta(dyc, ycf, *, name):
    t = dyc.shape[0]
    tm = min(256, t)
    seg = ((jnp.arange(BW)[:, None] // 64) == jnp.arange(128)[None, :]).astype(F32)

    def body(d_ref, o_ref, s_ref, out_ref):
        out_ref[...] = _dot_hi(d_ref[...] * o_ref[...], s_ref[...])

    return _pcall(body, name=name, grid=(t // tm,),
                  in_specs=[_rows(tm, BW), _rows(tm, BW), _fix((BW, 128))],
                  out_specs=_rows(tm, 128), out_shape=jax.ShapeDtypeStruct((t, 128), F32),
                  compiler_params=_cp("parallel"))(dyc, ycf, seg)


def fox_fwd_t(zb, frow, fkb, *, name):
    t = zb.shape[0]
    tq = min(512, t)
    nq = t // tq
    rep = tq // 128

    pairs = [(i, j) for i in range(nq) for j in range(i + 1)]
    qi_tab = jnp.asarray([p[0] for p in pairs], jnp.int32)
    kj_tab = jnp.asarray([p[1] for p in pairs], jnp.int32)

    def body(qi_ref, kj_ref, q_ref, k_ref, v_ref, fq_ref, fk_ref, y_ref, yf_ref, lse_ref, m_s, l_s, acc):
        step = pl.program_id(1)
        i, j = qi_ref[step], kj_ref[step]

        @pl.when(j == 0)
        def _():
            m_s[...] = jnp.full_like(m_s, NEG)
            l_s[...] = jnp.zeros_like(l_s)
            acc[...] = jnp.zeros_like(acc)

        lo = lax.broadcasted_iota(jnp.int32, (tq, 128), 1) < 64

        def work(diagonal):
            q = q_ref[...]
            k = k_ref[...]
            v = v_ref[...]
            if diagonal:
                key = lax.broadcasted_iota(jnp.int32, (tq, tq), 0)
                qry = lax.broadcasted_iota(jnp.int32, (tq, tq), 1)
                keep = key <= qry
            for hh in range(2):
                sel = lo if hh == 0 else jnp.logical_not(lo)
                qh = jnp.where(sel, q, jnp.zeros_like(q))
                s = _dot_nt(k, qh) + fq_ref[hh] - jnp.tile(fk_ref[hh], (1, rep))
                if diagonal:
                    s = jnp.where(keep, s, NEG)
                m_old = m_s[hh]
                m_new = jnp.maximum(m_old, jnp.max(s, axis=0, keepdims=True))
                p = jnp.exp(s - m_new)
                corr = jnp.exp(m_old - m_new)
                l_s[hh] = l_s[hh] * corr + jnp.sum(p, axis=0, keepdims=True)
                m_s[hh] = m_new
                pv = _dot_tn(v, p.astype(BF16))
                rows = slice(64 * hh, 64 * hh + 64)
                acc[rows, :] = acc[rows, :] * corr + pv[rows, :]

        @pl.when(j < i)
        def _():
            work(False)

        @pl.when(j == i)
        def _():
            work(True)
            first = lax.broadcasted_iota(jnp.int32, (128, tq), 0) < 64
            out = (acc[...] * jnp.where(first, 1.0 / l_s[0], 1.0 / l_s[1])).T
            y_ref[...] = out.astype(BF16)
            yf_ref[...] = out
            lse_ref[...] = m_s[...] + jnp.log(l_s[...])

    kv = lambda off: pl.BlockSpec((tq, 128), lambda h, s, qi, kj: (kj[s], off // 128 + h))
    gs = pltpu.PrefetchScalarGridSpec(
        num_scalar_prefetch=2, grid=(4, len(pairs)),
        in_specs=[pl.BlockSpec((tq, 128), lambda h, s, qi, kj: (qi[s], CQ // 128 + h)), kv(CK), kv(CV),
                  pl.BlockSpec((2, 1, tq), lambda h, s, qi, kj: (h, 0, qi[s])),
                  pl.BlockSpec((2, tq, 128), lambda h, s, qi, kj: (h, kj[s], 0))],
        out_specs=[pl.BlockSpec((tq, 128), lambda h, s, qi, kj: (qi[s], h)),
                   pl.BlockSpec((tq, 128), lambda h, s, qi, kj: (qi[s], h)),
                   pl.BlockSpec((2, 1, tq), lambda h, s, qi, kj: (h, 0, qi[s]))],
        scratch_shapes=[pltpu.VMEM((2, 1, tq), F32), pltpu.VMEM((2, 1, tq), F32), pltpu.VMEM((128, tq), F32)])
    return _pcall(body, name=name, grid_spec=gs,
                  out_shape=[jax.ShapeDtypeStruct((t, BW), BF16), jax.ShapeDtypeStruct((t, BW), F32),
                             jax.ShapeDtypeStruct((FOX_H, 1, t), F32)],
                  compiler_params=_cp("parallel", "arbitrary"))(qi_tab, kj_tab, zb, zb, zb, frow, fkb)


def fox_bwd_t(zb, dyc, frow, fkb, lse, dl, *, name):
    t = zb.shape[0]
    tq = min(512, t)
    nq = t // tq
    rep = tq // 128

    pairs = [(j, i) for j in range(nq) for i in range(j, nq)]
    kj_tab = jnp.asarray([p[0] for p in pairs], jnp.int32)
    qi_tab = jnp.asarray([p[1] for p in pairs], jnp.int32)

    def body(kj_ref, qi_ref, q_ref, k_ref, v_ref, do_ref, fq_ref, fk_ref, lse_ref, dl_ref,
             dq_ref, dk_ref, dv_ref, dfk_ref, dfq_ref, dk_s, dv_s, df_s, dq_s):
        step = pl.program_id(1)
        j, i = kj_ref[step], qi_ref[step]

        @pl.when(step == 0)
        def _():
            dq_s[...] = jnp.zeros_like(dq_s)
            dfq_ref[...] = jnp.zeros_like(dfq_ref)

        @pl.when(i == j)
        def _():
            dk_s[...] = jnp.zeros_like(dk_s)
            dv_s[...] = jnp.zeros_like(dv_s)
            df_s[...] = jnp.zeros_like(df_s)

        lo = lax.broadcasted_iota(jnp.int32, (tq, 128), 1) < 64

        def work(diagonal):
            q = q_ref[...]
            k = k_ref[...]
            v = v_ref[...]
            dob = do_ref[...].astype(BF16)
            if diagonal:
                key = lax.broadcasted_iota(jnp.int32, (tq, tq), 0)
                qry = lax.broadcasted_iota(jnp.int32, (tq, tq), 1)
                keep = key <= qry
            dvs, dks = [], []
            for hh in range(2):
                sel = lo if hh == 0 else jnp.logical_not(lo)
                qh = jnp.where(sel, q, jnp.zeros_like(q))
                doh = jnp.where(sel, dob, jnp.zeros_like(dob))
                p = jnp.exp(_dot_nt(k, qh) + (fq_ref[hh] - lse_ref[hh]) - jnp.tile(fk_ref[hh], (1, rep)))
                if diagonal:
                    p = jnp.where(keep, p, 0.0)
                ds = p * (_dot_nt(v, doh) - dl_ref[hh])
                dsb = ds.astype(BF16)
                dvs.append(_dot(p.astype(BF16), dob))
                dks.append(_dot(dsb, q))
                rows = slice(64 * hh, 64 * hh + 64)
                dq_s[i, rows, :] += _dot_tn(k, dsb)[rows, :]
                part = ds[:, 0:128]
                for r in range(1, rep):
                    part = part + ds[:, 128 * r:128 * (r + 1)]
                df_s[hh] += part
                dfq_ref[hh, i] += jnp.sum(ds, axis=0, keepdims=True)
            dv_s[...] += jnp.where(lo, dvs[0], dvs[1])
            dk_s[...] += jnp.where(lo, dks[0], dks[1])

        @pl.when(i > j)
        def _():
            work(False)

        @pl.when(i == j)
        def _():
            work(True)
            dq_ref[...] = dq_s[i].T.astype(BF16)

        @pl.when(i == nq - 1)
        def _():
            dk_ref[...] = dk_s[...].astype(BF16)
            dv_ref[...] = dv_s[...].astype(BF16)
            for hh in range(2):
                dfk_ref[hh] = -jnp.sum(df_s[hh].T, axis=0, keepdims=True)

    row = lambda: pl.BlockSpec((2, 1, tq), lambda h, s, kj, qi: (h, 0, qi[s]))
    gs = pltpu.PrefetchScalarGridSpec(
        num_scalar_prefetch=2, grid=(4, len(pairs)),
        in_specs=[pl.BlockSpec((tq, 128), lambda h, s, kj, qi: (qi[s], CQ // 128 + h)),
                  pl.BlockSpec((tq, 128), lambda h, s, kj, qi: (kj[s], CK // 128 + h)),
                  pl.BlockSpec((tq, 128), lambda h, s, kj, qi: (kj[s], CV // 128 + h)),
                  pl.BlockSpec((tq, 128), lambda h, s, kj, qi: (qi[s], h)),
                  row(), pl.BlockSpec((2, tq, 128), lambda h, s, kj, qi: (h, kj[s], 0)), row(), row()],
        out_specs=[pl.BlockSpec((tq, 128), lambda h, s, kj, qi: (kj[s], h)),
                   pl.BlockSpec((tq, 128), lambda h, s, kj, qi: (kj[s], h)),
                   pl.BlockSpec((tq, 128), lambda h, s, kj, qi: (kj[s], h)),
                   pl.BlockSpec((2, 1, tq), lambda h, s, kj, qi: (h, 0, kj[s])),
                   pl.BlockSpec((2, nq, 1, tq), lambda h, s, kj, qi: (h, 0, 0, 0))],
        scratch_shapes=[pltpu.VMEM((tq, 128), F32), pltpu.VMEM((tq, 128), F32), pltpu.VMEM((2, tq, 128), F32),
                        pltpu.VMEM((nq, 128, tq), F32)])
    return _pcall(body, name=name, grid_spec=gs,
                  out_shape=[jax.ShapeDtypeStruct((t, BW), BF16), jax.ShapeDtypeStruct((t, BW), BF16),
                             jax.ShapeDtypeStruct((t, BW), BF16), jax.ShapeDtypeStruct((FOX_H, 1, t), F32),
                             jax.ShapeDtypeStruct((FOX_H, nq, 1, tq), F32)],
                  compiler_params=_cp("parallel", "arbitrary"))(kj_tab, qi_tab, zb, zb, zb, dyc, frow, fkb, lse, dl)


def merge_fwd(ya, yb, yc, wbr, z, *, name):
    t = ya.shape[0]
    tm = min(512, t)

    def body(ya_ref, yb_ref, yc_ref, w_ref, g0_ref, g1_ref, g2_ref, o_ref):
        m = _sigmoid(g0_ref[...]) * _dot(ya_ref[...], w_ref[0])
        m = m + _sigmoid(g1_ref[...]) * _dot(yb_ref[...], w_ref[1])
        m = m + _sigmoid(g2_ref[...]) * _dot(yc_ref[...], w_ref[2])
        o_ref[...] = m.astype(BF16)

    return _pcall(body, name=name, grid=(t // tm,),
                  in_specs=[_rows(tm, BW)] * 3 + [_fix((3, BW, D))]
                  + [_rows(tm, D, G0 // D + j) for j in range(3)],
                  out_specs=_rows(tm, D), out_shape=jax.ShapeDtypeStruct((t, D), BF16),
                  compiler_params=_cp("parallel"))(ya, yb, yc, wbr, z, z, z)


def merge_bwd(doutb, wo, l, ya, yb, yc, wbr, z, *, name):
    t = ya.shape[0]
    tm = min(256, t)

    def body(do_ref, wo_ref, ya_ref, yb_ref, yc_ref, w_ref, g0_ref, g1_ref, g2_ref,
             dya_ref, dyb_ref, dyc_ref, dp0_ref, dp1_ref, dp2_ref, dg0_ref, dg1_ref, dg2_ref):
        dm = _dot_nt(do_ref[...], wo_ref[...])
        ys = (ya_ref, yb_ref, yc_ref)
        gs = (g0_ref, g1_ref, g2_ref)
        dys = (dya_ref, dyb_ref, dyc_ref)
        dps = (dp0_ref, dp1_ref, dp2_ref)
        dgs = (dg0_ref, dg1_ref, dg2_ref)
        for j in range(3):
            s = _sigmoid(gs[j][...])
            pj = _dot(ys[j][...], w_ref[j])
            dpb = (dm * s).astype(BF16)
            dps[j][...] = dpb
            dgs[j][...] = (dm * pj * s * (1.0 - s)).astype(BF16)
            dys[j][...] = _dot_nt(dpb, w_ref[j])

    yshape = jax.ShapeDtypeStruct((t, BW), F32)
    dshape = jax.ShapeDtypeStruct((t, D), BF16)
    return _pcall(body, name=name, grid=(t // tm,),
                  in_specs=[_rows(tm, D), _layer(l, (D, D))] + [_rows(tm, BW)] * 3
                  + [_fix((3, BW, D))] + [_rows(tm, D, G0 // D + j) for j in range(3)],
                  out_specs=[_rows(tm, BW)] * 3 + [_rows(tm, D)] * 6,
                  out_shape=[yshape] * 3 + [dshape] * 6,
                  compiler_params=_cp("parallel"))(doutb, wo, ya, yb, yc, wbr, z, z, z)


def adamw(w, g, m, v, *, name):
    nl, r, c = w.shape
    tm = _row_tile(r)

    def body(w_ref, g_ref, m_ref, v_ref, d_ref, mo_ref, vo_ref):
        gg = g_ref[...]
        mn = ADAM_B1 * m_ref[...] + (1.0 - ADAM_B1) * gg
        vn = ADAM_B2 * v_ref[...] + (1.0 - ADAM_B2) * (gg * gg)
        m_hat = mn / (1.0 - ADAM_B1 ** ADAM_STEP)
        v_hat = vn / (1.0 - ADAM_B2 ** ADAM_STEP)
        d_ref[...] = -ADAM_LR * (m_hat / (jnp.sqrt(v_hat) + ADAM_EPS) + ADAM_WD * w_ref[...])
        mo_ref[...] = mn
        vo_ref[...] = vn

    shp = jax.ShapeDtypeStruct((nl, r, c), F32)
    blk = pl.BlockSpec((None, tm, c), lambda l, i: (l, i, 0))
    return _pcall(body, name=name, grid=(nl, r // tm), in_specs=[blk] * 4, out_specs=[blk] * 3,
                  out_shape=[shp] * 3, compiler_params=_cp("parallel", "parallel"))(w, g, m, v)


def _place():
    return lax.axis_index("x"), lax.axis_index("y"), lax.axis_index("c")


def _remote(src, dst, send_sems, recv_sems, k, to):
    return pltpu.make_async_remote_copy(src_ref=src, dst_ref=dst, send_sem=send_sems.at[k],
                                        recv_sem=recv_sems.at[k], device_id=to, device_id_type=MESH)


HBM = pl.BlockSpec(memory_space=pltpu.HBM)
SEM = pl.BlockSpec(memory_space=pltpu.SEMAPHORE)
EFFECT = pltpu.SideEffectType.DATAFLOW_SIDE_EFFECTING


def gather_first(shards):
    n = len(shards)

    def body(*refs):
        ins, outs, lands = refs[:n], refs[n:2 * n], refs[2 * n:3 * n]
        send_sems, recv_sems, own_send, own_recv = refs[3 * n:]
        x, y, c = _place()
        sib = (x, y, 1 - c)
        chips = [(1 - x, y), (x, 1 - y), (1 - x, 1 - y)]
        k_me = 2 * x + y
        mine = []
        for t in range(n):
            mine.append(_remote(ins[t].at[0], outs[t].at[0, k_me], own_send, own_recv, 2 * t, sib))
            mine.append(_remote(ins[t].at[1], lands[t].at[0, k_me], own_send, own_recv, 2 * t + 1, sib))
        for cp in mine:
            cp.start()

        def slot(t, chip):
            return outs[t].at[0, 2 * chip[0] + chip[1]]

        @pl.when(c == 0)
        def _():
            first = [_remote(ins[t].at[0], outs[t].at[0, k_me], send_sems, recv_sems, 6 * t + j, (*chip, 0))
                     for t in range(n) for j, chip in enumerate(chips)]
            for cp in first:
                cp.start()
            passed = []
            for t in range(n):
                for j, chip in enumerate(chips):
                    _remote(slot(t, chip), slot(t, chip), send_sems, recv_sems, 6 * t + j, (*chip, 0)).wait_recv()
                    cp = _remote(slot(t, chip), slot(t, chip), send_sems, recv_sems, 6 * t + 3 + j, sib)
                    cp.start()
                    passed.append(cp)
            for cp in first + passed:
                cp.wait_send()

        @pl.when(c == 1)
        def _():
            for t in range(n):
                for j, chip in enumerate(chips):
                    _remote(slot(t, chip), slot(t, chip), send_sems, recv_sems, 6 * t + 3 + j, sib).wait_recv()

        for cp in mine:
            cp.wait()

    shape = [jax.ShapeDtypeStruct((1, 4) + s.shape[1:], s.dtype) for s in shards]
    out = _pcall(body, name="gather_first", in_specs=[ANY] * n, out_specs=[ANY] * (2 * n), out_shape=shape + shape,
                 scratch_shapes=[pltpu.SemaphoreType.DMA((6 * n,)), pltpu.SemaphoreType.DMA((6 * n,)),
                                 pltpu.SemaphoreType.DMA((2 * n,)), pltpu.SemaphoreType.DMA((2 * n,))])(*shards)
    return out[:n], out[n:]


def _rest_copies(ins, lands, send_sems, recv_sems):
    x, y, c = _place()
    chips = [(1 - x, y), (x, 1 - y), (1 - x, 1 - y)]
    copies, arrivals = [], []
    for t in range(len(ins)):
        for j, chip in enumerate(chips):
            for to in range(2):
                copies.append(pltpu.make_async_remote_copy(
                    src_ref=ins[t].at[1], dst_ref=lands[t].at[0, 2 * x + y], send_sem=send_sems.at[6 * t + 2 * j + to],
                    recv_sem=recv_sems.at[3 * t + j], device_id=(*chip, to), device_id_type=MESH))
            blk = lands[t].at[0, 2 * chip[0] + chip[1]]
            arrivals.append(pltpu.make_async_remote_copy(
                src_ref=blk, dst_ref=blk, send_sem=send_sems.at[6 * t + 2 * j], recv_sem=recv_sems.at[3 * t + j],
                device_id=(*chip, 1), device_id_type=MESH))
    return copies, arrivals


def gather_rest_start(shards, lands):
    n = len(shards)

    def body(*refs):
        ins, lds = refs[:n], refs[n:2 * n]
        send_sems, recv_sems = refs[2 * n], refs[2 * n + 1]
        token = refs[-1]
        copies, _ = _rest_copies(ins, lds, send_sems, recv_sems)

        @pl.when(lax.axis_index("c") == 1)
        def _():
            for cp in copies:
                cp.start()

        token[...] = jnp.zeros_like(token)

    hbm = lambda a: pltpu.with_memory_space_constraint(a, pltpu.HBM)
    out = _pcall(body, name="gather_rest_start", in_specs=[HBM] * (2 * n),
                 out_specs=[SEM, SEM] + [HBM] * (2 * n) + [pl.BlockSpec(memory_space=pltpu.VMEM)],
                 out_shape=[pltpu.SemaphoreType.DMA((6 * n,)), pltpu.SemaphoreType.DMA((3 * n,))]
                 + [pltpu.HBM(a.shape, a.dtype) for a in shards] + [pltpu.HBM(a.shape, a.dtype) for a in lands]
                 + [jax.ShapeDtypeStruct((8, 128), F32)],
                 input_output_aliases={i: 2 + i for i in range(2 * n)},
                 compiler_params=pltpu.CompilerParams(has_side_effects=EFFECT))(
                     *[hbm(a) for a in shards], *[hbm(a) for a in lands])
    return out[0], out[1], out[2:2 + n], out[2 + n:2 + 2 * n], out[-1]


def gather_rest_wait(send_sems, recv_sems, srcs, lands, after):
    n = len(srcs)

    def body(*refs):
        ins, lds = refs[:n], refs[n:2 * n]
        s_sems, r_sems = refs[2 * n], refs[2 * n + 1]
        copies, arrivals = _rest_copies(ins, lds, s_sems, r_sems)

        @pl.when(lax.axis_index("c") == 1)
        def _():
            for cp in copies:
                cp.wait_send()

        for cp in arrivals:
            cp.wait_recv()

    out = _pcall(body, name="gather_rest_wait", in_specs=[HBM] * (2 * n) + [SEM, SEM, ANY],
                 out_specs=[HBM] * (2 * n),
                 out_shape=[pltpu.HBM(a.shape, a.dtype) for a in srcs] + [pltpu.HBM(a.shape, a.dtype) for a in lands],
                 input_output_aliases={i: i for i in range(2 * n)},
                 compiler_params=pltpu.CompilerParams(has_side_effects=EFFECT))(
                     *srcs, *lands, send_sems, recv_sems, after)
    return out[n:]


def pair_send(gl, owner, layer):
    n = len(gl)

    def body(*refs):
        ins, outs = refs[:n], refs[n:2 * n]
        send_sems, recv_sems = refs[2 * n:]
        x, y, c = _place()
        sib = (x, y, 1 - c)
        cps = [_remote(ins[t], outs[t], send_sems, recv_sems, t, sib) for t in range(n)]
        for core in range(2):
            @pl.when(c == core)
            def _():
                for cp in _owned(cps, owner, 1 - core, per=1):
                    cp.start()
                for cp in _owned(cps, owner, 1 - core, per=1):
                    cp.wait_send()
                for cp in _owned(cps, owner, core, per=1):
                    cp.wait_recv()

    return _pcall(body, name="pair_send_l%d" % layer, in_specs=[ANY] * n, out_specs=[ANY] * n,
                  out_shape=[jax.ShapeDtypeStruct(a.shape, a.dtype) for a in gl],
                  scratch_shapes=[pltpu.SemaphoreType.DMA((n,)), pltpu.SemaphoreType.DMA((n,))])(*gl)


def _chip_copies(ins, outs, send_sems, recv_sems):
    x, y, c = _place()
    chips = [(1 - x, y), (x, 1 - y), (1 - x, 1 - y)]
    return [_remote(ins[t].at[2 * chip[0] + chip[1]], outs[t].at[j], send_sems, recv_sems, 3 * t + j, (*chip, c))
            for t in range(len(ins)) for j, chip in enumerate(chips)]


def _owned(cps, owner, core, per=3):
    return [cp for k, cp in enumerate(cps) if owner[k // per] == core]


def chip_send(s1, owner, layer):
    n = len(s1)

    def body(*refs):
        ins, outs = refs[:n], refs[n:2 * n]
        send_sems, recv_sems = refs[2 * n:]
        cps = _chip_copies(ins, outs, send_sems, recv_sems)
        for core in range(2):
            @pl.when(lax.axis_index("c") == core)
            def _():
                for cp in _owned(cps, owner, core):
                    cp.start()
                for cp in _owned(cps, owner, core):
                    cp.wait()

    return _pcall(body, name="chip_send_l%d" % layer, in_specs=[ANY] * n, out_specs=[ANY] * n,
                  out_shape=[jax.ShapeDtypeStruct((3,) + a.shape[1:], a.dtype) for a in s1],
                  scratch_shapes=[pltpu.SemaphoreType.DMA((3 * n,)), pltpu.SemaphoreType.DMA((3 * n,))])(*s1)


def chip_send_start(s1, owner, layer):
    n = len(s1)
    land = [lax.empty((3,) + a.shape[1:], a.dtype) for a in s1]

    def body(*refs):
        ins, lands = refs[:n], refs[n:2 * n]
        send_sems, recv_sems = refs[2 * n], refs[2 * n + 1]
        token = refs[-1]
        cps = _chip_copies(ins, lands, send_sems, recv_sems)
        for core in range(2):
            @pl.when(lax.axis_index("c") == core)
            def _():
                for cp in _owned(cps, owner, core):
                    cp.start()

        token[...] = jnp.zeros_like(token)

    hbm = lambda a: pltpu.with_memory_space_constraint(a, pltpu.HBM)
    out = _pcall(body, name="chip_send_start_l%d" % layer, in_specs=[HBM] * (2 * n),
                 out_specs=[SEM, SEM] + [HBM] * (2 * n) + [pl.BlockSpec(memory_space=pltpu.VMEM)],
                 out_shape=[pltpu.SemaphoreType.DMA((3 * n,)), pltpu.SemaphoreType.DMA((3 * n,))]
                 + [pltpu.HBM(a.shape, a.dtype) for a in s1] + [pltpu.HBM(a.shape, a.dtype) for a in land]
                 + [jax.ShapeDtypeStruct((8, 128), F32)],
                 input_output_aliases={i: 2 + i for i in range(2 * n)},
                 compiler_params=pltpu.CompilerParams(has_side_effects=EFFECT))(
                     *[hbm(a) for a in s1], *[hbm(a) for a in land])
    return out[0], out[1], out[2:2 + n], out[2 + n:2 + 2 * n], out[-1]


def chip_send_wait(send_sems, recv_sems, srcs, lands, after, owner, layer):
    n = len(srcs)

    def body(*refs):
        ins, lds = refs[:n], refs[n:2 * n]
        s_sems, r_sems = refs[2 * n], refs[2 * n + 1]
        cps = _chip_copies(ins, lds, s_sems, r_sems)
        for core in range(2):
            @pl.when(lax.axis_index("c") == core)
            def _():
                for cp in _owned(cps, owner, core):
                    cp.wait_send()
                    cp.wait_recv()

    out = _pcall(body, name="chip_send_wait_l%d" % layer, in_specs=[HBM] * (2 * n) + [SEM, SEM, ANY],
                 out_specs=[HBM] * (2 * n),
                 out_shape=[pltpu.HBM(a.shape, a.dtype) for a in srcs] + [pltpu.HBM(a.shape, a.dtype) for a in lands],
                 input_output_aliases={i: i for i in range(2 * n)},
                 compiler_params=pltpu.CompilerParams(has_side_effects=EFFECT))(
                     *srcs, *lands, send_sems, recv_sems, after)
    return out[n:]


def pair_share(s2, owner):
    n = len(s2)

    def body(*refs):
        ins, outs = refs[:n], refs[n:2 * n]
        send_sems, recv_sems = refs[2 * n:]
        x, y, c = _place()
        sib = (x, y, 1 - c)
        cps = [_remote(ins[t], outs[t], send_sems, recv_sems, t, sib) for t in range(n)]
        for core in range(2):
            @pl.when(c == core)
            def _():
                for cp in _owned(cps, owner, core, per=1):
                    cp.start()
                for cp in _owned(cps, owner, core, per=1):
                    cp.wait_send()
                for cp in _owned(cps, owner, 1 - core, per=1):
                    cp.wait_recv()

    return _pcall(body, name="pair_share", in_specs=[ANY] * n, out_specs=[ANY] * n,
                  out_shape=[jax.ShapeDtypeStruct(a.shape, a.dtype) for a in s2],
                  input_output_aliases={t: t for t in range(n)},
                  scratch_shapes=[pltpu.SemaphoreType.DMA((n,)), pltpu.SemaphoreType.DMA((n,))])(*s2)


def small_exchange(gs):
    rows, width = gs.shape

    def body(g_ref, o_ref, send_sems, recv_sems):
        x, y, c = _place()
        cps = []
        for r in range(1, 8):
            dx, dy, dc = (r >> 2) & 1, (r >> 1) & 1, r & 1
            to = (x if dx == 0 else 1 - x, y if dy == 0 else 1 - y, c if dc == 0 else 1 - c)
            cps.append(_remote(g_ref, o_ref.at[r - 1], send_sems, recv_sems, r - 1, to))
        for cp in cps:
            cp.start()
        for cp in cps:
            cp.wait()

    return _pcall(body, name="small_exchange", in_specs=[ANY], out_specs=ANY,
                  out_shape=jax.ShapeDtypeStruct((7, rows, width), gs.dtype),
                  scratch_shapes=[pltpu.SemaphoreType.DMA((7,)), pltpu.SemaphoreType.DMA((7,))])(gs)


def _row_tile(rows):
    return _pick(rows, (256, 352, 128, 64, 32, 16))


def pair_add_layer(g, rb, core, owner, *, name):
    _, rows, width = g.shape
    tr = _row_tile(rows)

    def body(c_ref, g_ref, r_ref, o_ref, ob_ref):
        @pl.when(c_ref[0] == owner)
        def _():
            s = g_ref[...] + r_ref[...]
            o_ref[...] = s
            ob_ref[...] = s.astype(BF16)

    def at(k, i, c_ref):
        mine = c_ref[0] == owner
        return (jnp.where(mine, k, 0), jnp.where(mine, i, 0), 0)

    blk = pl.BlockSpec((None, tr, width), at)
    gs = pltpu.PrefetchScalarGridSpec(num_scalar_prefetch=1, grid=(4, rows // tr), in_specs=[blk, blk],
                                      out_specs=[blk, blk])
    return _pcall(body, name=name, grid_spec=gs,
                  out_shape=[jax.ShapeDtypeStruct(g.shape, F32), jax.ShapeDtypeStruct(g.shape, BF16)],
                  compiler_params=_cp("parallel", "parallel"))(core, g, rb)


def chip_add_layers(s1, rb2, chip, core, owner, *, name):
    _, rows, width = s1[0].shape
    tr = _row_tile(rows)

    def body(k_ref, c_ref, s0_ref, s1_ref, r0_ref, r1_ref, o_ref):
        @pl.when(c_ref[0] == owner)
        def _():
            first = pl.program_id(0) == 0
            s = jnp.where(first, s0_ref[...], s1_ref[...])
            r = jnp.where(first, r0_ref[...], r1_ref[...]).astype(F32)
            o_ref[...] = ((s + r[0]) + r[1]) + r[2]

    def s_spec(layer):
        def at(l, i, k_ref, c_ref):
            use = jnp.logical_and(l == layer, c_ref[0] == owner)
            return (jnp.where(use, k_ref[0], 0), jnp.where(use, i, 0), 0)
        return pl.BlockSpec((None, tr, width), at)

    def r_spec(layer):
        def at(l, i, k_ref, c_ref):
            return (0, jnp.where(jnp.logical_and(l == layer, c_ref[0] == owner), i, 0), 0)
        return pl.BlockSpec((3, tr, width), at)

    def out_at(l, i, k_ref, c_ref):
        mine = c_ref[0] == owner
        return (jnp.where(mine, l, 0), jnp.where(mine, i, 0), 0)

    gs = pltpu.PrefetchScalarGridSpec(
        num_scalar_prefetch=2, grid=(DEPTH, rows // tr),
        in_specs=[s_spec(0), s_spec(1), r_spec(0), r_spec(1)],
        out_specs=pl.BlockSpec((None, tr, width), out_at))
    return _pcall(body, name=name, grid_spec=gs, out_shape=jax.ShapeDtypeStruct((DEPTH, rows, width), F32),
                  compiler_params=_cp("parallel", "parallel"))(chip, core, s1[0], s1[1], rb2[0], rb2[1])


def small_add(gs_own, slots, me):
    rows, width = gs_own.shape
    tr = _pick(rows, (64, 32, 16, 8))

    def body(me_ref, g_ref, s_ref, o_ref):
        me_v = me_ref[0]
        total = None
        for d in range(8):
            rel = jnp.bitwise_xor(me_v, d)
            val = jnp.where(rel == 0, g_ref[...], s_ref[jnp.maximum(rel - 1, 0)])
            total = val if total is None else total + val
        o_ref[...] = total

    gs = pltpu.PrefetchScalarGridSpec(
        num_scalar_prefetch=1, grid=(rows // tr,),
        in_specs=[pl.BlockSpec((tr, width), lambda i, m_ref: (i, 0)),
                  pl.BlockSpec((7, tr, width), lambda i, m_ref: (0, i, 0))],
        out_specs=pl.BlockSpec((tr, width), lambda i, m_ref: (i, 0)))
    return _pcall(body, name="small_add", grid_spec=gs, out_shape=jax.ShapeDtypeStruct((rows, width), F32),
                  compiler_params=_cp("parallel"))(me, gs_own, slots)


SHARDED = (("ffn1_w_up", (D, UPW)), ("ffn1_w_down", (DFF // 4, D)), ("w_in", (D, D_IN // 4)),
           ("conv_w", (4, BW // 4)), ("gla_w_g2", (LOW_W, 64)), ("w_branch", (3 * BW, D // 4)),
           ("w_out", (D // 4, D)), ("ffn2_w_up", (D, UPW)), ("ffn2_w_down", (DFF // 4, D)),
           ("ple_w_proj", (PLE, D // 4)), ("ple_w_gate", (D // 4, D)))
OWNER = tuple(0 if n in ("ffn1_w_up", "w_in", "w_out") else 1 for n, _ in SHARDED)
SMALL = ("ln1_g", "ln1_b", "conv_b", "lru_wa", "lru_ba", "lru_wx", "lru_bx", "lru_lambda", "gla_b_g",
         "gla_norm_g", "fox_b_f", "ln2_g", "ln2_b", "ln3_g", "ln3_b", "ple_b_gate", "ln4_g", "ln4_b")
WEIGHTS = ('ffn1_w_up', 'ffn1_w_down', 'ln1_g', 'ln1_b', 'w_in', 'conv_w', 'conv_b', 'lru_wa', 'lru_ba',
           'lru_wx', 'lru_bx', 'lru_lambda', 'gla_w_g2', 'gla_b_g', 'gla_norm_g', 'fox_b_f', 'w_branch',
           'w_out', 'ln2_g', 'ln2_b', 'ffn2_w_up', 'ffn2_w_down', 'ln3_g', 'ln3_b', 'ple_w_proj',
           'ple_w_gate', 'ple_b_gate', 'ln4_g', 'ln4_b')


def _cols_join(parts):
    return jnp.concatenate([parts[k] for k in range(4)], axis=-1)


def _cols_split(full):
    r, c4 = full.shape
    return full.reshape(r, 4, c4 // 4).transpose(1, 0, 2)


def _regroup_in(w):
    pad = jnp.zeros(w.shape[:-1] + (ZW - D_IN,), w.dtype)
    fox_q = (w[..., 2576:3088] * FOX_SCALE).astype(w.dtype)
    return jnp.concatenate([w[..., 0:2048], w[..., 2064:2576], fox_q, w[..., 3088:4112], w[..., 4120:7192],
                            w[..., 2048:2064], w[..., 4112:4120], pad], axis=-1)


_IN_RUNS = ((0, 2048, 0, 1.0), (2048, 2064, 7168, 1.0), (2064, 2576, 2048, 1.0), (2576, 3088, CQ, FOX_SCALE),
            (3088, 4112, CK, 1.0), (4112, 4120, 7184, 1.0), (4120, D_IN, 4096, 1.0))


def _regroup_out_shards(g):
    w = D_IN // 4
    shards = []
    for k in range(4):
        pieces = []
        for a, b, new, f in _IN_RUNS:
            lo, hi = max(a, k * w), min(b, (k + 1) * w)
            if lo < hi:
                piece = g[:, new + lo - a:new + hi - a]
                pieces.append(piece if f == 1.0 else piece * f)
        shards.append(jnp.concatenate(pieces, axis=1))
    return jnp.stack(shards)


def _block_diag(w):
    eye = jnp.eye(8, dtype=w.dtype)
    return (eye[:, None, :, None] * w[:, :, None, :]).reshape(BW, BW)


def _diag_blocks(dense):
    return jnp.stack([dense[64 * n:64 * (n + 1), 64 * n:64 * (n + 1)] for n in range(8)])


def _layer_weights(gw, small, l):
    w = {"up1": gw["ffn1_w_up"], "up2": gw["ffn2_w_up"],
         "dn1": gw["ffn1_w_down"].reshape(1, DFF, D), "dn2": gw["ffn2_w_down"].reshape(1, DFF, D),
         "wo": gw["w_out"].reshape(1, D, D), "wgt": gw["ple_w_gate"].reshape(1, D, D)}
    w["win"] = _regroup_in(_cols_join(gw["w_in"][0]))
    w["cw"] = _cols_join(gw["conv_w"][0])
    w["wa"] = _block_diag(small["lru_wa"][l]).astype(BF16)
    w["wx"] = _block_diag(small["lru_wx"][l]).astype(BF16)
    w["wg2p"] = jnp.pad(_cols_join(gw["gla_w_g2"][0]), ((0, 128 - LOW_W), (0, 0)))
    w["wbr"] = _cols_join(gw["w_branch"][0].reshape(4, 3, BW, D // 4))
    w["wp"] = _cols_join(gw["ple_w_proj"][0])
    for n in ("ln1_g", "ln1_b", "ln2_g", "ln2_b", "ln3_g", "ln3_b", "ln4_g", "ln4_b", "conv_b", "lru_ba",
              "lru_bx", "lru_lambda", "gla_b_g", "gla_norm_g", "ple_b_gate"):
        w[n] = small[n][l][None, :]
    w["bfp"] = jnp.pad(small["fox_b_f"][l], (LOW_W, 128 - LOW_W - FOX_H))[None, :]
    return w


def _heads_t(a):
    ht = a[:, LOW_W:LOW_W + FOX_H].T
    return ht[:, None, :], jnp.broadcast_to(ht[:, :, None], ht.shape + (128,))


def _layer_fwd(x, xb, pb, w, l):
    s = {"x0": x, "x0b": xb}
    tag = "l%d_" % l
    gate, up, act = ffn_up(xb, w["up1"], 0, name=tag + "ffn1_up")
    r1, x1, x1b = matmul_res_ln(act, w["dn1"], 0, x, w["ln1_g"], w["ln1_b"], mm_scale=0.5, name=tag + "ffn1_down")
    s.update(gate1=gate, up1=up, act1=act, r1=r1, x1=x1, x1b=x1b)
    z, zb = matmul(x1b, w["win"], also_bf16=True, tm=1024, tn=_pick(ZW, (2432,)), name=tag + "mix_in")
    xc, xcb, h, ya = lru_fwd(z, w["cw"], w["conv_b"], w["wa"], w["wx"], w["lru_ba"], w["lru_bx"],
                             w["lru_lambda"], name=tag + "lru_fwd")
    yb, states = gla_fwd(z, zb, w["wg2p"], w["gla_b_g"], w["gla_norm_g"], name=tag + "gla_fwd")
    fcum = fox_fcum(z, w["bfp"], name=tag + "fox_fcum")
    fq, fk = _heads_t(fcum)
    yc, ycf, lse = fox_fwd_t(zb, fq, fk, name=tag + "fox_fwd")
    merged = merge_fwd(ya, yb, yc, w["wbr"], z, name=tag + "merge_fwd")
    r2, x2, x2b = matmul_res_ln(merged, w["wo"], 0, x1, w["ln2_g"], w["ln2_b"], mm_scale=1.0, name=tag + "mix_out")
    s.update(z=z, zb=zb, xc=xc, xcb=xcb, h=h, ya=ya, yb=yb, states=states, fq=fq, fk=fk, yc=yc, ycf=ycf,
             lse=lse, merged=merged, r2=r2, x2=x2, x2b=x2b)
    gate, up, act = ffn_up(x2b, w["up2"], 0, name=tag + "ffn2_up")
    r3, x3, x3b = matmul_res_ln(act, w["dn2"], 0, x2, w["ln3_g"], w["ln3_b"], mm_scale=0.5, name=tag + "ffn2_down")
    s.update(gate2=gate, up2=up, act2=act, r3=r3, x3=x3, x3b=x3b)
    r4, x4, x4b = ple_fwd(x3b, x3, pb, w["wgt"], 0, w["wp"], w["ple_b_gate"], w["ln4_g"], w["ln4_b"],
                          name=tag + "ple_fwd")
    s.update(r4=r4, pb=pb)
    return x4, x4b, s


def _ffn_bwd(dy, s, w, n, xin_b, l, tag):
    k = {"1": ("r1", "ln1_g", "gate1", "up1", "act1"), "2": ("r3", "ln3_g", "gate2", "up2", "act2")}[n]
    dr, dfb, dg, db = ln_bwd(dy, s[k[0]], w[k[1]], out_scale=0.5, name=tag + "ln_bwd")
    dgate, dup = ffn_down_bwd(dfb, w["dn" + n], 0, s[k[2]], s[k[3]], name=tag + "down_bwd")
    dx = ffn_dx(dgate, dup, w["up" + n], 0, dr, name=tag + "dx")
    dwup = matmul_tn_up(xin_b, dgate, dup, name=tag + "dw_up")
    dwdn = matmul_tn(s[k[4]], dfb, name=tag + "dw_down").reshape(4, DFF // 4, D)
    return dx, dwup, dwdn, dg[0], db[0]


def _layer_bwd(dy, s, w, l):
    g = {}
    tag = "l%d_" % l
    dr4, dglb, dpeb, dg4, db4, dbg = ple_bwd(dy, s["r4"], s["x3b"], s["pb"], w["wgt"], 0, w["wp"], w["ple_b_gate"],
                                             w["ln4_g"], name=tag + "ple_bwd")
    dx3 = matmul(dglb, w["wgt"], nt=True, b_lead=(0,), res=dr4, res_scale=ALPHA, tm=1024, tn=1024,
                 name=tag + "ple_dx")
    g["ple_w_gate"] = matmul_tn(s["x3b"], dglb, name=tag + "ple_dw_gate").reshape(4, D // 4, D)
    g["ple_w_proj"] = _cols_split(matmul_tn(s["pb"], dpeb, name=tag + "ple_dw_proj"))
    g["ln4_g"], g["ln4_b"], g["ple_b_gate"] = dg4[0], db4[0], dbg[0]
    dx2, g["ffn2_w_up"], g["ffn2_w_down"], g["ln3_g"], g["ln3_b"] = _ffn_bwd(dx3, s, w, "2", s["x2b"], l,
                                                                             tag + "ffn2_")
    dr2, doutb, dg2, db2 = ln_bwd(dx2, s["r2"], w["ln2_g"], out_scale=1.0, name=tag + "mix_ln_bwd")
    g["ln2_g"], g["ln2_b"] = dg2[0], db2[0]
    g["w_out"] = matmul_tn(s["merged"], doutb, name=tag + "dw_out").reshape(4, D // 4, D)
    z, zb = s["z"], s["zb"]
    (dya, dyb, dyc, dp0, dp1, dp2, dgl0, dgl1, dgl2) = merge_bwd(
        doutb, w["wo"], 0, s["ya"], s["yb"], s["yc"], w["wbr"], z, name=tag + "merge_bwd")
    dwbr = jnp.stack([matmul_tn(s["ya"], dp0, name=tag + "dw_br0"), matmul_tn(s["yb"], dp1, name=tag + "dw_br1"),
                      matmul_tn(s["yc"], dp2, name=tag + "dw_br2")])
    g["w_branch"] = _cols_split(dwbr.reshape(3 * BW, D))
    day, dxc, dprb, dpib, dba, dbx, dlam = lru_bwd(dya, z, s["h"], s["xc"], w["wa"], w["wx"],
                                                   w["lru_ba"], w["lru_bx"], w["lru_lambda"], name=tag + "lru_bwd")
    dax, dcw, dcb = conv_bwd(dxc, z, w["cw"], name=tag + "conv_bwd")
    g["lru_wa"] = _diag_blocks(matmul_tn(s["xcb"], dprb, name=tag + "dw_lru_a"))
    g["lru_wx"] = _diag_blocks(matmul_tn(s["xcb"], dpib, name=tag + "dw_lru_x"))
    g["lru_ba"], g["lru_bx"], g["lru_lambda"] = dba[0], dbx[0], dlam[0]
    g["conv_w"], g["conv_b"] = _cols_split(dcw), dcb[0]
    dbq, dbk, dbv, dbr, dmisc_g, dpreb, dbgg, dng = gla_bwd(dyb, z, zb, s["states"], w["wg2p"], w["gla_b_g"],
                                                            w["gla_norm_g"], name=tag + "gla_bwd")
    miscb = zb[:, MISC:]
    g["gla_w_g2"] = _cols_split(matmul_tn(miscb, dpreb, name=tag + "dw_g2")[:LOW_W])
    g["gla_b_g"], g["gla_norm_g"] = dbgg[0], dng[0]
    dl = fox_delta(dyc, s["ycf"], name=tag + "fox_delta")
    t = z.shape[0]
    dlq = dl[:, :FOX_H].T[:, None, :]
    dcq, dck, dcv, dfk, dfq = fox_bwd_t(zb, dyc, s["fq"], s["fk"], s["lse"], dlq, name=tag + "fox_bwd")
    dfc = jnp.pad((dfk[:, 0, :] + dfq.reshape(FOX_H, t)).T, ((0, 0), (LOW_W, 128 - LOW_W - FOX_H)))
    dmiscb, dbf = fox_dcf(dfc, z, w["bfp"], dmisc_g, name=tag + "fox_dcf")
    g["fox_b_f"] = dbf[0, LOW_W:LOW_W + FOX_H]
    dz = jnp.concatenate([dax, day, dbq, dbk, dbv, dbr, dcq, dck, dcv, dgl0, dgl1, dgl2, dmiscb], axis=1)
    dx1 = matmul(dz, w["win"], nt=True, res=dr2, res_scale=ALPHA, tm=1024, tn=1024, tk=_pick(ZW, (2432,)),
                 name=tag + "mix_dx")
    g["w_in"] = _regroup_out_shards(matmul_tn(s["x1b"], dz, name=tag + "dw_in"))
    dx0, g["ffn1_w_up"], g["ffn1_w_down"], g["ln1_g"], g["ln1_b"] = _ffn_bwd(dx1, s, w, "1", s["x0b"], l,
                                                                             tag + "ffn1_")
    return dx0, g


def _local_step(x, p, target, gathered, small, after_last_layer=None):
    xcur = x
    xb = xcur.astype(BF16)
    layer_w, saved = [], []
    for l in range(DEPTH):
        w = _layer_weights(gathered(l, xcur), small, l)
        xcur, xb, s = _layer_fwd(xcur, xb, p[l].astype(BF16), w, l)
        layer_w.append(w)
        saved.append(s)
    dy, sq = loss_head(xcur, target, name="loss_head")
    grads = [None] * DEPTH
    for l in reversed(range(DEPTH)):
        dy, grads[l] = _layer_bwd(dy, saved[l], layer_w[l], l)
        if l == DEPTH - 1 and after_last_layer is not None:
            layer_w[l - 1]["ln4_g"] = layer_w[l - 1]["ln4_g"] + after_last_layer(grads[l])
    return 0.5 * jnp.sum(sq) / float(D), dy, grads


def kernel(x, p, ffn1_w_up, ffn1_w_down, ln1_g, ln1_b, w_in, conv_w, conv_b, lru_wa, lru_ba, lru_wx, lru_bx, lru_lambda, gla_w_g2, gla_b_g, gla_norm_g, fox_b_f, w_branch, w_out, ln2_g, ln2_b, ffn2_w_up, ffn2_w_down, ln3_g, ln3_b, ple_w_proj, ple_w_gate, ple_b_gate, ln4_g, ln4_b, loss_target, m_ffn1_w_up, m_ffn1_w_down, m_ln1_g, m_ln1_b, m_w_in, m_conv_w, m_conv_b, m_lru_wa, m_lru_ba, m_lru_wx, m_lru_bx, m_lru_lambda, m_gla_w_g2, m_gla_b_g, m_gla_norm_g, m_fox_b_f, m_w_branch, m_w_out, m_ln2_g, m_ln2_b, m_ffn2_w_up, m_ffn2_w_down, m_ln3_g, m_ln3_b, m_ple_w_proj, m_ple_w_gate, m_ple_b_gate, m_ln4_g, m_ln4_b, v_ffn1_w_up, v_ffn1_w_down, v_ln1_g, v_ln1_b, v_w_in, v_conv_w, v_conv_b, v_lru_wa, v_lru_ba, v_lru_wx, v_lru_bx, v_lru_lambda, v_gla_w_g2, v_gla_b_g, v_gla_norm_g, v_fox_b_f, v_w_branch, v_w_out, v_ln2_g, v_ln2_b, v_ffn2_w_up, v_ffn2_w_down, v_ln3_g, v_ln3_b, v_ple_w_proj, v_ple_w_gate, v_ple_b_gate, v_ln4_g, v_ln4_b):
    args = dict(locals())
    wts = {n: args[n] for n in WEIGHTS}
    mom = {n: args["m_" + n] for n in WEIGHTS}
    var = {n: args["v_" + n] for n in WEIGHTS}
    cx, cy, cc = lax.axis_index("x"), lax.axis_index("y"), lax.axis_index("c")

    names = [n for n, _ in SHARDED]
    shards = [wts[n].reshape((DEPTH,) + rc).astype(F32 if n == "conv_w" else BF16) for n, rc in SHARDED]
    first, lands = gather_first(shards)
    rest_send, rest_recv, rest_srcs, rest_lands, rest_token = gather_rest_start(shards, lands)
    small = {n: wts[n] for n in SMALL}
    small["ln1_g"] = small["ln1_g"] + rest_token[0, 0]

    def gathered(l, after):
        if l == 0:
            return dict(zip(names, first))
        return dict(zip(names, gather_rest_wait(rest_send, rest_recv, rest_srcs, rest_lands, after)))

    flight = {}
    core = jnp.reshape(cc, (1,)).astype(jnp.int32)
    chip = jnp.reshape(2 * cx + cy, (1,)).astype(jnp.int32)

    def chip_sum(gl, layer):
        lst = [gl[n] for n in names]
        rb = pair_send(lst, OWNER, layer)
        return [pair_add_layer(a, r, core, own, name="pair_add_l%d_%s" % (layer, n))
                for n, own, a, r in zip(names, OWNER, lst, rb)]

    def start_last_layer(gl):
        s1 = chip_sum(gl, DEPTH - 1)
        send_sems, recv_sems, srcs, lands, token = chip_send_start([sb for _, sb in s1], OWNER, DEPTH - 1)
        flight.update(s1=[sf for sf, _ in s1], sems=(send_sems, recv_sems), srcs=srcs, lands=lands)
        return token[0, 0]

    loss_local, dx, grads = _local_step(x[0], p[:, 0], loss_target[0], gathered, small, start_last_layer)
    loss = lax.psum(loss_local, ("x", "y", "c"))
    grad_x = dx[None]

    s1_first = chip_sum(grads[0], 0)
    rb2_first = chip_send([sb for _, sb in s1_first], OWNER, 0)
    rb2_last = chip_send_wait(*flight["sems"], flight["srcs"], flight["lands"], dx, OWNER, DEPTH - 1)
    s2 = [chip_add_layers((sf0, sf1), (r0, r1), chip, core, own, name="chip_add_" + n)
          for n, own, (sf0, _), sf1, r0, r1 in zip(names, OWNER, s1_first, flight["s1"], rb2_first, rb2_last)]
    gsh = dict(zip(names, pair_share(s2, OWNER)))

    pieces, spans, row = [], {}, 0
    for n in SMALL:
        flat = jnp.stack([grads[l][n] for l in range(DEPTH)]).reshape(-1)
        rows = -(-flat.shape[0] // (8 * PACK_W)) * 8
        pieces.append(jnp.pad(flat, (0, rows * PACK_W - flat.shape[0])).reshape(rows, PACK_W))
        spans[n] = (row, rows)
        row += rows
    gs = jnp.concatenate(pieces, axis=0)
    me = jnp.reshape(4 * cx + 2 * cy + cc, (1,)).astype(jnp.int32)
    gsum = small_add(gs, small_exchange(gs), me)

    gout, delta, new_m, new_v = {}, {}, {}, {}
    for n in WEIGHTS:
        shp = wts[n].shape
        if n in gsh:
            view = gsh[n].shape
            g = gsh[n]
        else:
            view = (1, DEPTH, wts[n].size // DEPTH)
            r0, rows = spans[n]
            g = gsum[r0:r0 + rows].reshape(-1)[:wts[n].size].reshape(view)
        d, mn, vn = adamw(wts[n].reshape(view), g, mom[n].reshape(view), var[n].reshape(view), name="adamw_" + n)
        gout[n], delta[n], new_m[n], new_v[n] = g.reshape(shp), d.reshape(shp), mn.reshape(shp), vn.reshape(shp)

    return (loss, grad_x, *[gout[n] for n in WEIGHTS], *[delta[n] for n in WEIGHTS],
            *[new_m[n] for n in WEIGHTS], *[new_v[n] for n in WEIGHTS])
```

```python
import functools
import math

import jax
import jax.numpy as jnp
from jax import lax
from jax.experimental import pallas as pl
from jax.experimental.pallas import tpu as pltpu

F32 = jnp.float32
BF16 = jnp.bfloat16

D = 1024
DFF = 2816
BW = 512
PLE = 256
DEPTH = 2
ALPHA = (2 * DEPTH) ** 0.25
LN_EPS = 1e-5
RMS_EPS = 1e-6
LRU_C = 8.0
GLA_TAU = 16.0
CHUNK = 64
D_IN = 7192
ZW = 7296
AX, AY, BQ, BK, BV, BR, CQ, CK, CV, G0, MISC = 0, 512, 1024, 1280, 1536, 2048, 2560, 3072, 3584, 4096, 7168
LOW_W, FOX_H = 16, 8
ADAM_LR, ADAM_B1, ADAM_B2, ADAM_EPS, ADAM_WD, ADAM_STEP = 0.001, 0.9, 0.999, 1e-08, 0.01, 10
PACK_W = 1024
VMEM_LIMIT = 56 << 20

MESH = pl.DeviceIdType.MESH
ANY = pl.BlockSpec(memory_space=pl.ANY)


def _pcall(body, **kw):
    return pl.pallas_call(body, **kw)


def _cp(*dims):
    return pltpu.CompilerParams(dimension_semantics=dims, vmem_limit_bytes=VMEM_LIMIT)


def _dot(a, b):
    return jnp.dot(a, b, preferred_element_type=F32)


def _dot_nt(a, b):
    return lax.dot_general(a, b, (((1,), (1,)), ((), ())), preferred_element_type=F32)


def _dot_tn(a, b):
    return lax.dot_general(a, b, (((0,), (0,)), ((), ())), preferred_element_type=F32)


def _dot_hi(a, b):
    return jnp.dot(a, b, preferred_element_type=F32, precision=lax.Precision.HIGHEST)


def _sigmoid(x):
    return 1.0 / (1.0 + jnp.exp(-x))


def _softplus(x):
    return jnp.maximum(x, 0.0) + jnp.log(1.0 + jnp.exp(-jnp.abs(x)))


def _log_sigmoid(x):
    return -_softplus(-x)


def _expm1(x):
    poly = x * (1.0 + x * (0.5 + x * (1.0 / 6.0 + x * (1.0 / 24.0 + x * (1.0 / 120.0 + x * (1.0 / 720.0))))))
    return jnp.where(jnp.abs(x) < 0.1, poly, jnp.exp(x) - 1.0)


_GELU_C = math.sqrt(2.0 / math.pi)


def _gelu(x):
    return 0.5 * x * (1.0 + jnp.tanh(_GELU_C * (x + 0.044715 * x * x * x)))


def _gelu_grad(x):
    t = jnp.tanh(_GELU_C * (x + 0.044715 * x * x * x))
    return 0.5 * (1.0 + t) + 0.5 * x * (1.0 - t * t) * _GELU_C * (1.0 + 3.0 * 0.044715 * x * x)


def _ln_stats(r):
    mu = jnp.mean(r, axis=-1, keepdims=True)
    xc = r - mu
    var = jnp.mean(xc * xc, axis=-1, keepdims=True)
    return xc, lax.rsqrt(var + LN_EPS)


def _pick(n, cands):
    for c in cands:
        if n % c == 0:
            return c
    return n


def _rows(tm, w, col=0):
    return pl.BlockSpec((tm, w), lambda i: (i, col))


def _fix(shape):
    nd = len(shape)
    return pl.BlockSpec(shape, lambda i: (0,) * nd)


def _col_chunks(n, width=256):
    return [slice(c, min(c + width, n)) for c in range(0, n, width)]


def _layer(l, shape):
    nd = len(shape)
    return pl.BlockSpec((None,) + tuple(shape), lambda i: (l,) + (0,) * nd)


def matmul(a, b, *, name, nt=False, b_lead=(), res=None, res_scale=1.0, also_bf16=False, tm=512, tn=512,
           tk=None):
    m, k = a.shape
    n = b.shape[-2] if nt else b.shape[-1]
    tm, tn = min(tm, m), min(tn, n)
    tk = k if tk is None else tk
    nk = k // tk
    has_res = res is not None
    lead = tuple(b_lead)
    dot = _dot_nt if nt else _dot

    def body(*refs):
        a_ref, b_ref = refs[0], refs[1]
        pos = 2
        r_ref = None
        if has_res:
            r_ref = refs[pos]
            pos += 1
        o_ref = refs[pos]
        pos += 1
        ob_ref = None
        if also_bf16:
            ob_ref = refs[pos]
            pos += 1

        def finish(v):
            if has_res:
                v = v + res_scale * r_ref[...]
            o_ref[...] = v
            if also_bf16:
                ob_ref[...] = v.astype(BF16)

        if nk == 1:
            finish(dot(a_ref[...], b_ref[...]))
            return
        acc = refs[pos]
        kk = pl.program_id(2)

        @pl.when(kk == 0)
        def _():
            acc[...] = jnp.zeros_like(acc)

        acc[...] += dot(a_ref[...], b_ref[...])

        @pl.when(kk == nk - 1)
        def _():
            finish(acc[...])

    none = (None,) * len(lead)
    if nt:
        b_spec = pl.BlockSpec(none + (tn, tk), lambda j, i, kk: lead + (j, kk))
    else:
        b_spec = pl.BlockSpec(none + (tk, tn), lambda j, i, kk: lead + (kk, j))
    in_specs = [pl.BlockSpec((tm, tk), lambda j, i, kk: (i, kk)), b_spec]
    args = [a, b]
    if has_res:
        in_specs.append(pl.BlockSpec((tm, tn), lambda j, i, kk: (i, j)))
        args.append(res)
    out_shape = [jax.ShapeDtypeStruct((m, n), F32)]
    out_specs = [pl.BlockSpec((tm, tn), lambda j, i, kk: (i, j))]
    if also_bf16:
        out_shape.append(jax.ShapeDtypeStruct((m, n), BF16))
        out_specs.append(pl.BlockSpec((tm, tn), lambda j, i, kk: (i, j)))
    out = _pcall(body, name=name, grid=(n // tn, m // tm, nk), in_specs=in_specs, out_specs=out_specs,
                 out_shape=out_shape, scratch_shapes=[pltpu.VMEM((tm, tn), F32)] if nk > 1 else [],
                 compiler_params=_cp("parallel", "parallel", "arbitrary"))(*args)
    return out if also_bf16 else out[0]


def matmul_tn(a, b, *, name):
    t, k = a.shape
    n = b.shape[1]
    tk = _pick(k, (1024, 1408, 512, 256, 128))
    tn = _pick(n, (1024, 1408, 2432, 512, 256, 128))
    tt = min(1024 if tk * tn > (1 << 20) else 2048, t)
    nt = t // tt

    def body(a_ref, b_ref, o_ref):
        @pl.when(pl.program_id(2) == 0)
        def _():
            o_ref[...] = jnp.zeros_like(o_ref)

        o_ref[...] += _dot_tn(a_ref[...], b_ref[...])

    return _pcall(body, name=name, grid=(k // tk, n // tn, nt),
                  in_specs=[pl.BlockSpec((tt, tk), lambda i, j, s: (s, i)),
                            pl.BlockSpec((tt, tn), lambda i, j, s: (s, j))],
                  out_specs=pl.BlockSpec((tk, tn), lambda i, j, s: (i, j)),
                  out_shape=jax.ShapeDtypeStruct((k, n), F32),
                  compiler_params=_cp("parallel", "parallel", "arbitrary"))(a, b)


UPW = 1408


def matmul_tn_up(a, dgate, dup, *, name):
    t, k = a.shape
    tt = min(1024, t)
    tk = 1024

    def body(a_ref, g_ref, u_ref, o_ref):
        j = pl.program_id(1)

        @pl.when(pl.program_id(2) == 0)
        def _():
            o_ref[...] = jnp.zeros_like(o_ref)

        @pl.when(j < 2)
        def _():
            o_ref[...] += _dot_tn(a_ref[...], g_ref[...])

        @pl.when(j >= 2)
        def _():
            o_ref[...] += _dot_tn(a_ref[...], u_ref[...])

    return _pcall(body, name=name, grid=(k // tk, 4, t // tt),
                  in_specs=[pl.BlockSpec((tt, tk), lambda i, j, s: (s, i)),
                            pl.BlockSpec((tt, UPW), lambda i, j, s: (jnp.where(j < 2, s, 0), jnp.minimum(j, 1))),
                            pl.BlockSpec((tt, UPW), lambda i, j, s: (jnp.where(j >= 2, s, 0), jnp.maximum(j - 2, 0)))],
                  out_specs=pl.BlockSpec((None, tk, UPW), lambda i, j, s: (j, i, 0)),
                  out_shape=jax.ShapeDtypeStruct((4, k, UPW), F32),
                  compiler_params=_cp("parallel", "parallel", "arbitrary"))(a, dgate, dup)


def ffn_dx(dgate, dup, wup, l, res, *, name):
    t = dgate.shape[0]
    tm, tn = min(1024, t), 1024

    def body(g_ref, u_ref, w_ref, r_ref, o_ref, acc):
        kk = pl.program_id(2)

        @pl.when(kk == 0)
        def _():
            acc[...] = jnp.zeros_like(acc)

        @pl.when(kk < 2)
        def _():
            acc[...] += _dot_nt(g_ref[...], w_ref[...])

        @pl.when(kk >= 2)
        def _():
            acc[...] += _dot_nt(u_ref[...], w_ref[...])

        @pl.when(kk == 3)
        def _():
            o_ref[...] = acc[...] + ALPHA * r_ref[...]

    return _pcall(body, name=name, grid=(D // tn, t // tm, 4),
                  in_specs=[pl.BlockSpec((tm, UPW), lambda j, i, kk: (i, jnp.minimum(kk, 1))),
                            pl.BlockSpec((tm, UPW), lambda j, i, kk: (i, jnp.maximum(kk - 2, 0))),
                            pl.BlockSpec((None, None, tn, UPW), lambda j, i, kk: (l, kk, j, 0)),
                            pl.BlockSpec((tm, tn), lambda j, i, kk: (i, j))],
                  out_specs=pl.BlockSpec((tm, tn), lambda j, i, kk: (i, j)),
                  out_shape=jax.ShapeDtypeStruct((t, D), F32),
                  scratch_shapes=[pltpu.VMEM((tm, tn), F32)],
                  compiler_params=_cp("parallel", "parallel", "arbitrary"))(dgate, dup, wup, res)


def ffn_up(xb, wup, l, *, name):
    t = xb.shape[0]
    tm, tn = min(1024, t), UPW

    def body(x_ref, wg_ref, wu_ref, g_ref, u_ref, a_ref):
        x = x_ref[...]
        for cols in _col_chunks(tn):
            g = _dot(x, wg_ref[:, cols])
            u = _dot(x, wu_ref[:, cols])
            g_ref[:, cols] = g.astype(BF16)
            u_ref[:, cols] = u.astype(BF16)
            a_ref[:, cols] = (g * _sigmoid(g) * u).astype(BF16)

    blk = pl.BlockSpec((tm, tn), lambda j, i: (i, j))
    return _pcall(body, name=name, grid=(DFF // tn, t // tm),
                  in_specs=[pl.BlockSpec((tm, D), lambda j, i: (i, 0)),
                            pl.BlockSpec((None, None, D, tn), lambda j, i: (l, j, 0, 0)),
                            pl.BlockSpec((None, None, D, tn), lambda j, i: (l, 2 + j, 0, 0))],
                  out_specs=[blk, blk, blk],
                  out_shape=[jax.ShapeDtypeStruct((t, DFF), BF16)] * 3,
                  compiler_params=_cp("parallel", "parallel"))(xb, wup, wup)


def matmul_res_ln(a, w, l, res, g, b, *, mm_scale, name):
    t, k = a.shape
    tm = min(512, t)

    def body(a_ref, w_ref, res_ref, g_ref, b_ref, r_ref, y_ref, yb_ref):
        f = _dot(a_ref[...], w_ref[...])
        r = ALPHA * res_ref[...] + mm_scale * f
        xc, rstd = _ln_stats(r)
        y = xc * rstd * g_ref[...] + b_ref[...]
        r_ref[...] = r
        y_ref[...] = y
        yb_ref[...] = y.astype(BF16)

    return _pcall(body, name=name, grid=(t // tm,),
                  in_specs=[_rows(tm, k), _layer(l, (k, D)), _rows(tm, D), _fix((1, D)), _fix((1, D))],
                  out_specs=[_rows(tm, D)] * 3,
                  out_shape=[jax.ShapeDtypeStruct((t, D), F32), jax.ShapeDtypeStruct((t, D), F32),
                             jax.ShapeDtypeStruct((t, D), BF16)],
                  compiler_params=_cp("parallel"))(a, w, res, g, b)


def ln_bwd(dy, r, g, *, out_scale, name):
    t = dy.shape[0]
    tm = min(256, t)

    def body(dy_ref, r_ref, g_ref, dr_ref, drb_ref, dg_ref, db_ref):
        @pl.when(pl.program_id(0) == 0)
        def _():
            dg_ref[...] = jnp.zeros_like(dg_ref)
            db_ref[...] = jnp.zeros_like(db_ref)

        xc, rstd = _ln_stats(r_ref[...])
        xhat = xc * rstd
        d = dy_ref[...]
        dxh = d * g_ref[...]
        dr = rstd * (dxh - jnp.mean(dxh, axis=-1, keepdims=True)
                     - xhat * jnp.mean(dxh * xhat, axis=-1, keepdims=True))
        dr_ref[...] = dr
        drb_ref[...] = (out_scale * dr).astype(BF16)
        dg_ref[...] += jnp.sum(d * xhat, axis=0, keepdims=True)
        db_ref[...] += jnp.sum(d, axis=0, keepdims=True)

    return _pcall(body, name=name, grid=(t // tm,),
                  in_specs=[_rows(tm, D), _rows(tm, D), _fix((1, D))],
                  out_specs=[_rows(tm, D), _rows(tm, D), _fix((1, D)), _fix((1, D))],
                  out_shape=[jax.ShapeDtypeStruct((t, D), F32), jax.ShapeDtypeStruct((t, D), BF16),
                             jax.ShapeDtypeStruct((1, D), F32), jax.ShapeDtypeStruct((1, D), F32)],
                  compiler_params=_cp("arbitrary"))(dy, r, g)


def ffn_down_bwd(dfb, wd, l, gate, up, *, name):
    t = dfb.shape[0]
    tm, tn = min(1024, t), UPW
    nj = DFF // tn

    def body(df_ref, w_ref, g_ref, u_ref, dg_ref, du_ref):
        df = df_ref[...]
        for cols in _col_chunks(tn):
            da = _dot_nt(df, w_ref[cols, :])
            g = g_ref[:, cols].astype(F32)
            s = _sigmoid(g)
            gs = g * s
            dg_ref[:, cols] = (da * u_ref[:, cols].astype(F32) * (s + gs * (1.0 - s))).astype(BF16)
            du_ref[:, cols] = (da * gs).astype(BF16)

    blk = pl.BlockSpec((tm, tn), lambda j, i: (i, j))
    return _pcall(body, name=name, grid=(nj, t // tm),
                  in_specs=[pl.BlockSpec((tm, D), lambda j, i: (i, 0)),
                            pl.BlockSpec((None, tn, D), lambda j, i: (l, j, 0)), blk, blk],
                  out_specs=[blk, blk],
                  out_shape=[jax.ShapeDtypeStruct((t, DFF), BF16), jax.ShapeDtypeStruct((t, DFF), BF16)],
                  compiler_params=_cp("parallel", "parallel"))(dfb, wd, gate, up)


def ple_fwd(xb, x, pb, wgate, l, wproj, bgate, g, b, *, name):
    t = x.shape[0]
    tm = min(512, t)

    def body(xb_ref, x_ref, p_ref, wg_ref, wp_ref, bg_ref, g_ref, b_ref, r_ref, y_ref, yb_ref):
        gl = _dot(xb_ref[...], wg_ref[...]) + bg_ref[...]
        pe = _dot(p_ref[...], wp_ref[...])
        r = ALPHA * x_ref[...] + _sigmoid(gl) * pe
        xc, rstd = _ln_stats(r)
        y = xc * rstd * g_ref[...] + b_ref[...]
        r_ref[...] = r
        y_ref[...] = y
        yb_ref[...] = y.astype(BF16)

    return _pcall(body, name=name, grid=(t // tm,),
                  in_specs=[_rows(tm, D), _rows(tm, D), _rows(tm, PLE), _layer(l, (D, D)), _fix((PLE, D)),
                            _fix((1, D)), _fix((1, D)), _fix((1, D))],
                  out_specs=[_rows(tm, D)] * 3,
                  out_shape=[jax.ShapeDtypeStruct((t, D), F32), jax.ShapeDtypeStruct((t, D), F32),
                             jax.ShapeDtypeStruct((t, D), BF16)],
                  compiler_params=_cp("parallel"))(xb, x, pb, wgate, wproj, bgate, g, b)


def ple_bwd(dy, r, xb, pb, wgate, l, wproj, bgate, g, *, name):
    t = dy.shape[0]
    tm = min(512, t)

    def body(dy_ref, r_ref, xb_ref, p_ref, wg_ref, wp_ref, bg_ref, g_ref,
             dr_ref, dgl_ref, dpe_ref, dg_ref, db_ref, dbg_ref):
        @pl.when(pl.program_id(0) == 0)
        def _():
            dg_ref[...] = jnp.zeros_like(dg_ref)
            db_ref[...] = jnp.zeros_like(db_ref)
            dbg_ref[...] = jnp.zeros_like(dbg_ref)

        xc, rstd = _ln_stats(r_ref[...])
        xhat = xc * rstd
        d = dy_ref[...]
        dxh = d * g_ref[...]
        dr = rstd * (dxh - jnp.mean(dxh, axis=-1, keepdims=True)
                     - xhat * jnp.mean(dxh * xhat, axis=-1, keepdims=True))
        s = _sigmoid(_dot(xb_ref[...], wg_ref[...]) + bg_ref[...])
        pe = _dot(p_ref[...], wp_ref[...])
        dgl = dr * pe * s * (1.0 - s)
        dr_ref[...] = dr
        dgl_ref[...] = dgl.astype(BF16)
        dpe_ref[...] = (dr * s).astype(BF16)
        dg_ref[...] += jnp.sum(d * xhat, axis=0, keepdims=True)
        db_ref[...] += jnp.sum(d, axis=0, keepdims=True)
        dbg_ref[...] += jnp.sum(dgl, axis=0, keepdims=True)

    vec = jax.ShapeDtypeStruct((1, D), F32)
    return _pcall(body, name=name, grid=(t // tm,),
                  in_specs=[_rows(tm, D), _rows(tm, D), _rows(tm, D), _rows(tm, PLE), _layer(l, (D, D)),
                            _fix((PLE, D)), _fix((1, D)), _fix((1, D))],
                  out_specs=[_rows(tm, D), _rows(tm, D), _rows(tm, D), _fix((1, D)), _fix((1, D)), _fix((1, D))],
                  out_shape=[jax.ShapeDtypeStruct((t, D), F32), jax.ShapeDtypeStruct((t, D), BF16),
                             jax.ShapeDtypeStruct((t, D), BF16), vec, vec, vec],
                  compiler_params=_cp("arbitrary"))(dy, r, xb, pb, wgate, wproj, bgate, g)


def loss_head(y, tgt, *, name):
    t = y.shape[0]
    tm = min(256, t)

    def body(y_ref, t_ref, dy_ref, sq_ref):
        @pl.when(pl.program_id(0) == 0)
        def _():
            sq_ref[...] = jnp.zeros_like(sq_ref)

        e = y_ref[...] - t_ref[...]
        dy_ref[...] = e / float(D)
        sq_ref[...] += jnp.sum(e * e, axis=0, keepdims=True)

    return _pcall(body, name=name, grid=(t // tm,),
                  in_specs=[_rows(tm, D), _rows(tm, D)],
                  out_specs=[_rows(tm, D), _fix((1, D))],
                  out_shape=[jax.ShapeDtypeStruct((t, D), F32), jax.ShapeDtypeStruct((1, D), F32)],
                  compiler_params=_cp("arbitrary"))(y, tgt)


def _lru_gates(xc, wa_ref, wx_ref, ba_ref, bx_ref, lam_ref):
    xcb = xc.astype(BF16)
    r = _sigmoid(_dot(xcb, wa_ref[...]) + ba_ref[...])
    ig = _sigmoid(_dot(xcb, wx_ref[...]) + bx_ref[...])
    sp = _softplus(-lam_ref[...])
    la = -LRU_C * r * sp
    a = jnp.exp(la)
    mult = jnp.sqrt(-_expm1(2.0 * la))
    return r, ig, sp, la, a, mult


def lru_fwd(z, cw, cb, wa, wx, ba, bx, lam, *, name):
    t = z.shape[0]
    tm = min(256, t)
    hb = tm // 8

    def body(ax_ref, prev_ref, ay_ref, cw_ref, cb_ref, wa_ref, wx_ref, ba_ref, bx_ref, lam_ref,
             xc_ref, xcb_ref, h_ref, ya_ref, xs, a_s, b_s, hc):
        i = pl.program_id(0)

        @pl.when(i == 0)
        def _():
            hc[...] = jnp.zeros_like(hc)

        xs[0:8, :] = jnp.where(i == 0, 0.0, prev_ref[...])
        xs[8:, :] = ax_ref[...]
        xc = cb_ref[...] + cw_ref[0:1, :] * xs[5:5 + tm, :]
        for k in range(1, 4):
            xc = xc + cw_ref[k:k + 1, :] * xs[5 + k:5 + k + tm, :]
        r, ig, sp, la, a, mult = _lru_gates(xc, wa_ref, wx_ref, ba_ref, bx_ref, lam_ref)
        a_s[...] = a
        b_s[...] = mult * (ig * xc)
        xc_ref[...] = xc
        xcb_ref[...] = xc.astype(BF16)

        def step(g, h):
            base = pl.multiple_of(g * 8, 8)
            a8 = a_s[pl.ds(base, 8), :]
            b8 = b_s[pl.ds(base, 8), :]
            for j in range(8):
                h = a8[j:j + 1, :] * h + b8[j:j + 1, :]
                h_ref[pl.ds(base + j, 1), :] = h
            return h

        hc[...] = lax.fori_loop(0, tm // 8, step, hc[...])
        ya_ref[...] = (_gelu(ay_ref[...]) * h_ref[...]).astype(BF16)

    vec = _fix((1, BW))
    return _pcall(body, name=name, grid=(t // tm,),
                  in_specs=[_rows(tm, BW, AX // BW),
                            pl.BlockSpec((8, BW), lambda i: (jnp.maximum(i * hb - 1, 0), AX // BW)),
                            _rows(tm, BW, AY // BW), _fix((4, BW)), vec, _fix((BW, BW)), _fix((BW, BW)),
                            vec, vec, vec],
                  out_specs=[_rows(tm, BW)] * 4,
                  out_shape=[jax.ShapeDtypeStruct((t, BW), F32), jax.ShapeDtypeStruct((t, BW), BF16),
                             jax.ShapeDtypeStruct((t, BW), F32), jax.ShapeDtypeStruct((t, BW), BF16)],
                  scratch_shapes=[pltpu.VMEM((tm + 8, BW), F32), pltpu.VMEM((tm, BW), F32),
                                  pltpu.VMEM((tm, BW), F32), pltpu.VMEM((1, BW), F32)],
                  compiler_params=_cp("arbitrary"))(z, z, z, cw, cb, wa, wx, ba, bx, lam)


def lru_bwd(dya, z, h, xc, wa, wx, ba, bx, lam, *, name):
    t = dya.shape[0]
    tm = min(256, t)
    nb = t // tm
    hb = tm // 8

    def body(dya_ref, ay_ref, h_ref, hprev_ref, xc_ref, wa_ref, wx_ref, ba_ref, bx_ref,
             lam_ref, day_ref, dxc_ref, dpr_ref, dpi_ref, dba_ref, dbx_ref, dlam_ref,
             hs, a_s, g_s, d_s, cc):
        i = pl.program_id(0)

        @pl.when(i == 0)
        def _():
            cc[...] = jnp.zeros_like(cc)
            dba_ref[...] = jnp.zeros_like(dba_ref)
            dbx_ref[...] = jnp.zeros_like(dbx_ref)
            dlam_ref[...] = jnp.zeros_like(dlam_ref)

        xc = xc_ref[...]
        r, ig, sp, la, a, mult = _lru_gates(xc, wa_ref, wx_ref, ba_ref, bx_ref, lam_ref)
        ay = ay_ref[...]
        dya = dya_ref[...]
        hcur = h_ref[...]
        day_ref[...] = (dya * hcur * _gelu_grad(ay)).astype(BF16)
        a_s[...] = a
        g_s[...] = dya * _gelu(ay)

        def step(gg, cin):
            g = tm // 8 - 1 - gg
            base = pl.multiple_of(g * 8, 8)
            a8 = a_s[pl.ds(base, 8), :]
            g8 = g_s[pl.ds(base, 8), :]
            for j in range(7, -1, -1):
                d = g8[j:j + 1, :] + cin
                d_s[pl.ds(base + j, 1), :] = d
                cin = a8[j:j + 1, :] * d
            return cin

        cc[...] = lax.fori_loop(0, tm // 8, step, cc[...])
        dht = d_s[...]
        hs[0:8, :] = jnp.where(i == nb - 1, 0.0, hprev_ref[...])
        hs[8:, :] = hcur
        da = dht * hs[7:7 + tm, :]
        dmult = dht * ig * xc
        dig = dht * mult * xc
        dla = da * a - dmult * a * a / mult
        dpr = dla * (-LRU_C * sp) * r * (1.0 - r)
        dpi = dig * ig * (1.0 - ig)
        dprb = dpr.astype(BF16)
        dpib = dpi.astype(BF16)
        dxc_ref[...] = dht * mult * ig + _dot_nt(dprb, wa_ref[...]) + _dot_nt(dpib, wx_ref[...])
        dpr_ref[...] = dprb
        dpi_ref[...] = dpib
        dba_ref[...] += jnp.sum(dpr, axis=0, keepdims=True)
        dbx_ref[...] += jnp.sum(dpi, axis=0, keepdims=True)
        dlam_ref[...] += jnp.sum(dla * (-LRU_C * r), axis=0, keepdims=True) * (-_sigmoid(-lam_ref[...]))

    vec = _fix((1, BW))
    mat = _fix((BW, BW))
    rev = lambda col: pl.BlockSpec((tm, BW), lambda i: (nb - 1 - i, col))
    vshape = jax.ShapeDtypeStruct((1, BW), F32)
    return _pcall(body, name=name, grid=(nb,),
                  in_specs=[rev(0), rev(AY // BW), rev(0),
                            pl.BlockSpec((8, BW), lambda i: (jnp.maximum((nb - 1 - i) * hb - 1, 0), 0)),
                            rev(0), mat, mat, vec, vec, vec],
                  out_specs=[rev(0), rev(0), rev(0), rev(0), vec, vec, vec],
                  out_shape=[jax.ShapeDtypeStruct((t, BW), BF16), jax.ShapeDtypeStruct((t, BW), F32),
                             jax.ShapeDtypeStruct((t, BW), BF16), jax.ShapeDtypeStruct((t, BW), BF16),
                             vshape, vshape, vshape],
                  scratch_shapes=[pltpu.VMEM((tm + 8, BW), F32), pltpu.VMEM((tm, BW), F32),
                                  pltpu.VMEM((tm, BW), F32), pltpu.VMEM((tm, BW), F32),
                                  pltpu.VMEM((1, BW), F32)],
                  compiler_params=_cp("arbitrary"))(dya, z, h, h, xc, wa, wx, ba, bx, lam)


def conv_bwd(dxc, z, cw, *, name):
    t = dxc.shape[0]
    tm = min(256, t)
    nb = t // tm
    hb = tm // 8

    def body(d_ref, dnext_ref, ax_ref, prev_ref, cw_ref, dax_ref, dcw_ref, dcb_ref, ds, xs):
        i = pl.program_id(0)

        @pl.when(i == 0)
        def _():
            dcw_ref[...] = jnp.zeros_like(dcw_ref)
            dcb_ref[...] = jnp.zeros_like(dcb_ref)

        d = d_ref[...]
        ds[0:tm, :] = d
        ds[tm:, :] = jnp.where(i == nb - 1, 0.0, dnext_ref[...])
        xs[0:8, :] = jnp.where(i == 0, 0.0, prev_ref[...])
        xs[8:, :] = ax_ref[...]
        dax = cw_ref[3:4, :] * d
        for k in range(3):
            dax = dax + cw_ref[k:k + 1, :] * ds[3 - k:3 - k + tm, :]
        dax_ref[...] = dax.astype(BF16)
        for k in range(4):
            dcw_ref[k:k + 1, :] += jnp.sum(d * xs[5 + k:5 + k + tm, :], axis=0, keepdims=True)
        dcb_ref[...] += jnp.sum(d, axis=0, keepdims=True)

    return _pcall(body, name=name, grid=(nb,),
                  in_specs=[_rows(tm, BW),
                            pl.BlockSpec((8, BW), lambda i: (jnp.minimum((i + 1) * hb, nb * hb - 1), 0)),
                            _rows(tm, BW, AX // BW),
                            pl.BlockSpec((8, BW), lambda i: (jnp.maximum(i * hb - 1, 0), AX // BW)),
                            _fix((4, BW))],
                  out_specs=[_rows(tm, BW), _fix((4, BW)), _fix((1, BW))],
                  out_shape=[jax.ShapeDtypeStruct((t, BW), BF16), jax.ShapeDtypeStruct((4, BW), F32),
                             jax.ShapeDtypeStruct((1, BW), F32)],
                  scratch_shapes=[pltpu.VMEM((tm + 8, BW), F32), pltpu.VMEM((tm + 8, BW), F32)],
                  compiler_params=_cp("arbitrary"))(dxc, dxc, z, z, cw)


GLA_CB = 4


def _gla_consts():
    tri = (jnp.arange(CHUNK)[:, None] >= jnp.arange(CHUNK)[None, :]).astype(F32)
    mask = ((jnp.arange(BW)[:, None] // 128) == (jnp.arange(256)[None, :] // 64)).astype(F32)
    return tri, mask


def gla_fwd(z, zb, wg2p, bg, ng, *, name):
    t = z.shape[0]
    tm = GLA_CB * CHUNK
    nc = t // CHUNK
    tri, mask = _gla_consts()

    def body(q_ref, k_ref, v_ref, misc_ref, br_ref, w_ref, bg_ref, ng_ref, tri_ref, mask_ref,
             yb_ref, st_ref, st):
        @pl.when(pl.program_id(0) == 0)
        def _():
            st[...] = jnp.zeros_like(st)

        for c in range(GLA_CB):
            rows = slice(c * CHUNK, (c + 1) * CHUNK)
            pre = _dot(misc_ref[rows, :], w_ref[...]) + bg_ref[...]
            la = _log_sigmoid(pre) / GLA_TAU
            gc = _dot_hi(tri_ref[...], la)
            gt = gc[CHUNK - 1:CHUNK, :]
            kdec = k_ref[rows, :] * jnp.exp(gt - gc)
            delta = _dot_tn(v_ref[rows, :], kdec.astype(BF16))
            s_new = st[...] * jnp.exp(gt) + delta * mask_ref[...]
            st[...] = s_new
            st_ref[c] = s_new
            o = _dot_nt(q_ref[rows, :], s_new.astype(BF16)) * (64.0 ** -0.5)
            br = br_ref[rows, :]
            for hd in range(4):
                cols = slice(hd * 128, (hd + 1) * 128)
                oh = o[:, cols]
                rs = lax.rsqrt(jnp.mean(oh * oh, axis=-1, keepdims=True) + RMS_EPS)
                brh = br[:, cols]
                yb_ref[rows, cols] = (oh * rs * ng_ref[:, cols] * (brh * _sigmoid(brh))).astype(BF16)

    return _pcall(body, name=name, grid=(t // tm,),
                  in_specs=[_rows(tm, 256, BQ // 256), _rows(tm, 256, BK // 256), _rows(tm, BW, BV // BW),
                            _rows(tm, 128, MISC // 128), _rows(tm, BW, BR // BW), _fix((128, 256)),
                            _fix((1, 256)), _fix((1, BW)), _fix((CHUNK, CHUNK)), _fix((BW, 256))],
                  out_specs=[_rows(tm, BW), pl.BlockSpec((GLA_CB, BW, 256), lambda i: (i, 0, 0))],
                  out_shape=[jax.ShapeDtypeStruct((t, BW), BF16), jax.ShapeDtypeStruct((nc, BW, 256), F32)],
                  scratch_shapes=[pltpu.VMEM((BW, 256), F32)],
                  compiler_params=_cp("arbitrary"))(zb, z, zb, zb, z, wg2p, bg, ng, tri, mask)


def gla_bwd(dyb, z, zb, states, wg2p, bg, ng, *, name):
    t = z.shape[0]
    tm = GLA_CB * CHUNK
    nb = t // tm
    tri, mask = _gla_consts()
    triu = tri.T

    def body(dy_ref, q_ref, k_ref, v_ref, misc_ref, br_ref, st_ref, sp_ref, w_ref, bg_ref, ng_ref,
             tri_ref, triu_ref, mask_ref,
             dq_ref, dk_ref, dv_ref, dbr_ref, dmisc_ref, dpre_ref, dbg_ref, dng_ref, cc):
        i = pl.program_id(0)

        @pl.when(i == 0)
        def _():
            cc[...] = jnp.zeros_like(cc)
            dbg_ref[...] = jnp.zeros_like(dbg_ref)
            dng_ref[...] = jnp.zeros_like(dng_ref)

        last_row = lax.broadcasted_iota(jnp.int32, (CHUNK, 256), 0) == CHUNK - 1
        for c in range(GLA_CB - 1, -1, -1):
            rows = slice(c * CHUNK, (c + 1) * CHUNK)
            pre = _dot(misc_ref[rows, :], w_ref[...]) + bg_ref[...]
            la = _log_sigmoid(pre) / GLA_TAU
            gc = _dot_hi(tri_ref[...], la)
            gt = gc[CHUNK - 1:CHUNK, :]
            eg = jnp.exp(gt - gc)
            kdec = k_ref[rows, :] * eg
            e = jnp.exp(gt)
            s_n = st_ref[c]
            if c > 0:
                s_prev = st_ref[c - 1]
            else:
                s_prev = jnp.where(i == nb - 1, 0.0, sp_ref[0])
            sb = s_n.astype(BF16)
            qb = q_ref[rows, :]
            o = _dot_nt(qb, sb) * (64.0 ** -0.5)
            br = br_ref[rows, :]
            dy = dy_ref[rows, :]
            do_parts = []
            for hd in range(4):
                cols = slice(hd * 128, (hd + 1) * 128)
                oh = o[:, cols]
                rs = lax.rsqrt(jnp.mean(oh * oh, axis=-1, keepdims=True) + RMS_EPS)
                ohat = oh * rs
                brh = br[:, cols]
                sg = _sigmoid(brh)
                dyh = dy[:, cols]
                ngh = ng_ref[:, cols]
                don = dyh * (brh * sg)
                dbr_ref[rows, cols] = (dyh * (ohat * ngh) * sg * (1.0 + brh * (1.0 - sg))).astype(BF16)
                dng_ref[:, cols] += jnp.sum(don * ohat, axis=0, keepdims=True)
                doh = don * ngh
                do_parts.append(rs * (doh - ohat * jnp.mean(doh * ohat, axis=-1, keepdims=True)))
            dob = jnp.concatenate(do_parts, axis=1).astype(BF16)
            dq_ref[rows, :] = (_dot(dob, sb) * (64.0 ** -0.5)).astype(BF16)
            dst = cc[...] + _dot_tn(dob, qb) * (64.0 ** -0.5) * mask_ref[...]
            dsb = dst.astype(BF16)
            dkdec = _dot(v_ref[rows, :], dsb)
            dv_ref[rows, :] = _dot_nt(kdec.astype(BF16), dsb).astype(BF16)
            dgt = jnp.sum(dst * s_prev, axis=0, keepdims=True) * e
            dk_ref[rows, :] = (dkdec * eg).astype(BF16)
            dd = dkdec * kdec
            dgt = dgt + jnp.sum(dd, axis=0, keepdims=True)
            dgc = jnp.where(last_row, dgt - dd, -dd)
            dla = _dot_hi(triu_ref[...], dgc)
            dpre = dla * (1.0 / GLA_TAU) * _sigmoid(-pre)
            dpb = dpre.astype(BF16)
            dpre_ref[rows, :] = dpb
            dmisc_ref[rows, :] = _dot_nt(dpb, w_ref[...])
            dbg_ref[...] += jnp.sum(dpre, axis=0, keepdims=True)
            cc[...] = dst * e

    rev = lambda w, col: pl.BlockSpec((tm, w), lambda i: (nb - 1 - i, col))
    return _pcall(body, name=name, grid=(nb,),
                  in_specs=[rev(BW, 0), rev(256, BQ // 256), rev(256, BK // 256), rev(BW, BV // BW),
                            rev(128, MISC // 128), rev(BW, BR // BW),
                            pl.BlockSpec((GLA_CB, BW, 256), lambda i: (nb - 1 - i, 0, 0)),
                            pl.BlockSpec((1, BW, 256), lambda i: (jnp.maximum((nb - 1 - i) * GLA_CB - 1, 0), 0, 0)),
                            _fix((128, 256)), _fix((1, 256)), _fix((1, BW)),
                            _fix((CHUNK, CHUNK)), _fix((CHUNK, CHUNK)), _fix((BW, 256))],
                  out_specs=[rev(256, 0), rev(256, 0), rev(BW, 0), rev(BW, 0), rev(128, 0), rev(256, 0),
                             _fix((1, 256)), _fix((1, BW))],
                  out_shape=[jax.ShapeDtypeStruct((t, 256), BF16), jax.ShapeDtypeStruct((t, 256), BF16),
                             jax.ShapeDtypeStruct((t, BW), BF16), jax.ShapeDtypeStruct((t, BW), BF16),
                             jax.ShapeDtypeStruct((t, 128), F32), jax.ShapeDtypeStruct((t, 256), BF16),
                             jax.ShapeDtypeStruct((1, 256), F32), jax.ShapeDtypeStruct((1, BW), F32)],
                  scratch_shapes=[pltpu.VMEM((BW, 256), F32)],
                  compiler_params=_cp("arbitrary"))(dyb, zb, z, zb, zb, z, states, states, wg2p, bg, ng,
                                                    tri, triu, mask)


FOX_SCALE = 64.0 ** -0.5
NEG = -1e30


def fox_fcum(z, bfp, *, name):
    t = z.shape[0]
    tm = min(256, t)
    tri = (jnp.arange(tm)[:, None] >= jnp.arange(tm)[None, :]).astype(F32)

    def body(m_ref, b_ref, tri_ref, o_ref, cc):
        @pl.when(pl.program_id(0) == 0)
        def _():
            cc[...] = jnp.zeros_like(cc)

        lf = _log_sigmoid(m_ref[...] + b_ref[...])
        cs = _dot_hi(tri_ref[...], lf) + cc[...]
        o_ref[...] = cs
        cc[...] = cs[tm - 1:tm, :]

    return _pcall(body, name=name, grid=(t // tm,),
                  in_specs=[_rows(tm, 128, MISC // 128), _fix((1, 128)), _fix((tm, tm))],
                  out_specs=_rows(tm, 128), out_shape=jax.ShapeDtypeStruct((t, 128), F32),
                  scratch_shapes=[pltpu.VMEM((1, 128), F32)],
                  compiler_params=_cp("arbitrary"))(z, bfp, tri)


def fox_dcf(dfc, z, bfp, dmisc_g, *, name):
    t = z.shape[0]
    tm = min(256, t)
    nb = t // tm
    triu = (jnp.arange(tm)[:, None] <= jnp.arange(tm)[None, :]).astype(F32)

    def body(d_ref, m_ref, b_ref, g_ref, tri_ref, o_ref, dbf_ref, cc):
        @pl.when(pl.program_id(0) == 0)
        def _():
            cc[...] = jnp.zeros_like(cc)
            dbf_ref[...] = jnp.zeros_like(dbf_ref)

        rc = _dot_hi(tri_ref[...], d_ref[...]) + cc[...]
        cc[...] = rc[0:1, :]
        dcf = rc * _sigmoid(-(m_ref[...] + b_ref[...]))
        o_ref[...] = (dcf + g_ref[...]).astype(BF16)
        dbf_ref[...] += jnp.sum(dcf, axis=0, keepdims=True)

    rev = lambda col: pl.BlockSpec((tm, 128), lambda i: (nb - 1 - i, col))
    return _pcall(body, name=name, grid=(nb,),
                  in_specs=[rev(0), rev(MISC // 128), _fix((1, 128)), rev(0), _fix((tm, tm))],
                  out_specs=[rev(0), _fix((1, 128))],
                  out_shape=[jax.ShapeDtypeStruct((t, 128), BF16), jax.ShapeDtypeStruct((1, 128), F32)],
                  scratch_shapes=[pltpu.VMEM((1, 128), F32)],
                  compiler_params=_cp("arbitrary"))(dfc, z, bfp, dmisc_g, triu)


def fox_delta(dyc, ycf, *, name):
    t = dyc.shape[0]
    tm = min(256, t)
    seg = ((jnp.arange(BW)[:, None] // 64) == jnp.arange(128)[None, :]).astype(F32)

    def body(d_ref, o_ref, s_ref, out_ref):
        out_ref[...] = _dot_hi(d_ref[...] * o_ref[...], s_ref[...])

    return _pcall(body, name=name, grid=(t // tm,),
                  in_specs=[_rows(tm, BW), _rows(tm, BW), _fix((BW, 128))],
                  out_specs=_rows(tm, 128), out_shape=jax.ShapeDtypeStruct((t, 128), F32),
                  compiler_params=_cp("parallel"))(dyc, ycf, seg)


def fox_fwd_t(zb, frow, fkb, *, name):
    t = zb.shape[0]
    tq = min(512, t)
    nq = t // tq
    rep = tq // 128

    pairs = [(i, j) for i in range(nq) for j in range(i + 1)]
    qi_tab = jnp.asarray([p[0] for p in pairs], jnp.int32)
    kj_tab = jnp.asarray([p[1] for p in pairs], jnp.int32)

    def body(qi_ref, kj_ref, q_ref, k_ref, v_ref, fq_ref, fk_ref, y_ref, yf_ref, lse_ref, m_s, l_s, acc):
        step = pl.program_id(1)
        i, j = qi_ref[step], kj_ref[step]

        @pl.when(j == 0)
        def _():
            m_s[...] = jnp.full_like(m_s, NEG)
            l_s[...] = jnp.zeros_like(l_s)
            acc[...] = jnp.zeros_like(acc)

        lo = lax.broadcasted_iota(jnp.int32, (tq, 128), 1) < 64

        def work(diagonal):
            q = q_ref[...]
            k = k_ref[...]
            v = v_ref[...]
            if diagonal:
                key = lax.broadcasted_iota(jnp.int32, (tq, tq), 0)
                qry = lax.broadcasted_iota(jnp.int32, (tq, tq), 1)
                keep = key <= qry
            for hh in range(2):
                sel = lo if hh == 0 else jnp.logical_not(lo)
                qh = jnp.where(sel, q, jnp.zeros_like(q))
                s = _dot_nt(k, qh) + fq_ref[hh] - jnp.tile(fk_ref[hh], (1, rep))
                if diagonal:
                    s = jnp.where(keep, s, NEG)
                m_old = m_s[hh]
                m_new = jnp.maximum(m_old, jnp.max(s, axis=0, keepdims=True))
                p = jnp.exp(s - m_new)
                corr = jnp.exp(m_old - m_new)
                l_s[hh] = l_s[hh] * corr + jnp.sum(p, axis=0, keepdims=True)
                m_s[hh] = m_new
                pv = _dot_tn(v, p.astype(BF16))
                rows = slice(64 * hh, 64 * hh + 64)
                acc[rows, :] = acc[rows, :] * corr + pv[rows, :]

        @pl.when(j < i)
        def _():
            work(False)

        @pl.when(j == i)
        def _():
            work(True)
            first = lax.broadcasted_iota(jnp.int32, (128, tq), 0) < 64
            out = (acc[...] * jnp.where(first, 1.0 / l_s[0], 1.0 / l_s[1])).T
            y_ref[...] = out.astype(BF16)
            yf_ref[...] = out
            lse_ref[...] = m_s[...] + jnp.log(l_s[...])

    kv = lambda off: pl.BlockSpec((tq, 128), lambda h, s, qi, kj: (kj[s], off // 128 + h))
    gs = pltpu.PrefetchScalarGridSpec(
        num_scalar_prefetch=2, grid=(4, len(pairs)),
        in_specs=[pl.BlockSpec((tq, 128), lambda h, s, qi, kj: (qi[s], CQ // 128 + h)), kv(CK), kv(CV),
                  pl.BlockSpec((2, 1, tq), lambda h, s, qi, kj: (h, 0, qi[s])),
                  pl.BlockSpec((2, tq, 128), lambda h, s, qi, kj: (h, kj[s], 0))],
        out_specs=[pl.BlockSpec((tq, 128), lambda h, s, qi, kj: (qi[s], h)),
                   pl.BlockSpec((tq, 128), lambda h, s, qi, kj: (qi[s], h)),
                   pl.BlockSpec((2, 1, tq), lambda h, s, qi, kj: (h, 0, qi[s]))],
        scratch_shapes=[pltpu.VMEM((2, 1, tq), F32), pltpu.VMEM((2, 1, tq), F32), pltpu.VMEM((128, tq), F32)])
    return _pcall(body, name=name, grid_spec=gs,
                  out_shape=[jax.ShapeDtypeStruct((t, BW), BF16), jax.ShapeDtypeStruct((t, BW), F32),
                             jax.ShapeDtypeStruct((FOX_H, 1, t), F32)],
                  compiler_params=_cp("parallel", "arbitrary"))(qi_tab, kj_tab, zb, zb, zb, frow, fkb)


def fox_bwd_t(zb, dyc, frow, fkb, lse, dl, *, name):
    t = zb.shape[0]
    tq = min(512, t)
    nq = t // tq
    rep = tq // 128

    pairs = [(j, i) for j in range(nq) for i in range(j, nq)]
    kj_tab = jnp.asarray([p[0] for p in pairs], jnp.int32)
    qi_tab = jnp.asarray([p[1] for p in pairs], jnp.int32)

    def body(kj_ref, qi_ref, q_ref, k_ref, v_ref, do_ref, fq_ref, fk_ref, lse_ref, dl_ref,
             dq_ref, dk_ref, dv_ref, dfk_ref, dfq_ref, dk_s, dv_s, df_s, dq_s):
        step = pl.program_id(1)
        j, i = kj_ref[step], qi_ref[step]

        @pl.when(step == 0)
        def _():
            dq_s[...] = jnp.zeros_like(dq_s)
            dfq_ref[...] = jnp.zeros_like(dfq_ref)

        @pl.when(i == j)
        def _():
            dk_s[...] = jnp.zeros_like(dk_s)
            dv_s[...] = jnp.zeros_like(dv_s)
            df_s[...] = jnp.zeros_like(df_s)

        lo = lax.broadcasted_iota(jnp.int32, (tq, 128), 1) < 64

        def work(diagonal):
            q = q_ref[...]
            k = k_ref[...]
            v = v_ref[...]
            dob = do_ref[...].astype(BF16)
            if diagonal:
                key = lax.broadcasted_iota(jnp.int32, (tq, tq), 0)
                qry = lax.broadcasted_iota(jnp.int32, (tq, tq), 1)
                keep = key <= qry
            dvs, dks = [], []
            for hh in range(2):
                sel = lo if hh == 0 else jnp.logical_not(lo)
                qh = jnp.where(sel, q, jnp.zeros_like(q))
                doh = jnp.where(sel, dob, jnp.zeros_like(dob))
                p = jnp.exp(_dot_nt(k, qh) + (fq_ref[hh] - lse_ref[hh]) - jnp.tile(fk_ref[hh], (1, rep)))
                if diagonal:
                    p = jnp.where(keep, p, 0.0)
                ds = p * (_dot_nt(v, doh) - dl_ref[hh])
                dsb = ds.astype(BF16)
                dvs.append(_dot(p.astype(BF16), dob))
                dks.append(_dot(dsb, q))
                rows = slice(64 * hh, 64 * hh + 64)
                dq_s[i, rows, :] += _dot_tn(k, dsb)[rows, :]
                part = ds[:, 0:128]
                for r in range(1, rep):
                    part = part + ds[:, 128 * r:128 * (r + 1)]
                df_s[hh] += part
                dfq_ref[hh, i] += jnp.sum(ds, axis=0, keepdims=True)
            dv_s[...] += jnp.where(lo, dvs[0], dvs[1])
            dk_s[...] += jnp.where(lo, dks[0], dks[1])

        @pl.when(i > j)
        def _():
            work(False)

        @pl.when(i == j)
        def _():
            work(True)
            dq_ref[...] = dq_s[i].T.astype(BF16)

        @pl.when(i == nq - 1)
        def _():
            dk_ref[...] = dk_s[...].astype(BF16)
            dv_ref[...] = dv_s[...].astype(BF16)
            for hh in range(2):
                dfk_ref[hh] = -jnp.sum(df_s[hh].T, axis=0, keepdims=True)

    row = lambda: pl.BlockSpec((2, 1, tq), lambda h, s, kj, qi: (h, 0, qi[s]))
    gs = pltpu.PrefetchScalarGridSpec(
        num_scalar_prefetch=2, grid=(4, len(pairs)),
        in_specs=[pl.BlockSpec((tq, 128), lambda h, s, kj, qi: (qi[s], CQ // 128 + h)),
                  pl.BlockSpec((tq, 128), lambda h, s, kj, qi: (kj[s], CK // 128 + h)),
                  pl.BlockSpec((tq, 128), lambda h, s, kj, qi: (kj[s], CV // 128 + h)),
                  pl.BlockSpec((tq, 128), lambda h, s, kj, qi: (qi[s], h)),
                  row(), pl.BlockSpec((2, tq, 128), lambda h, s, kj, qi: (h, kj[s], 0)), row(), row()],
        out_specs=[pl.BlockSpec((tq, 128), lambda h, s, kj, qi: (kj[s], h)),
                   pl.BlockSpec((tq, 128), lambda h, s, kj, qi: (kj[s], h)),
                   pl.BlockSpec((tq, 128), lambda h, s, kj, qi: (kj[s], h)),
                   pl.BlockSpec((2, 1, tq), lambda h, s, kj, qi: (h, 0, kj[s])),
                   pl.BlockSpec((2, nq, 1, tq), lambda h, s, kj, qi: (h, 0, 0, 0))],
        scratch_shapes=[pltpu.VMEM((tq, 128), F32), pltpu.VMEM((tq, 128), F32), pltpu.VMEM((2, tq, 128), F32),
                        pltpu.VMEM((nq, 128, tq), F32)])
    return _pcall(body, name=name, grid_spec=gs,
                  out_shape=[jax.ShapeDtypeStruct((t, BW), BF16), jax.ShapeDtypeStruct((t, BW), BF16),
                             jax.ShapeDtypeStruct((t, BW), BF16), jax.ShapeDtypeStruct((FOX_H, 1, t), F32),
                             jax.ShapeDtypeStruct((FOX_H, nq, 1, tq), F32)],
                  compiler_params=_cp("parallel", "arbitrary"))(kj_tab, qi_tab, zb, zb, zb, dyc, frow, fkb, lse, dl)


def merge_fwd(ya, yb, yc, wbr, z, *, name):
    t = ya.shape[0]
    tm = min(512, t)

    def body(ya_ref, yb_ref, yc_ref, w_ref, g0_ref, g1_ref, g2_ref, o_ref):
        m = _sigmoid(g0_ref[...]) * _dot(ya_ref[...], w_ref[0])
        m = m + _sigmoid(g1_ref[...]) * _dot(yb_ref[...], w_ref[1])
        m = m + _sigmoid(g2_ref[...]) * _dot(yc_ref[...], w_ref[2])
        o_ref[...] = m.astype(BF16)

    return _pcall(body, name=name, grid=(t // tm,),
                  in_specs=[_rows(tm, BW)] * 3 + [_fix((3, BW, D))]
                  + [_rows(tm, D, G0 // D + j) for j in range(3)],
                  out_specs=_rows(tm, D), out_shape=jax.ShapeDtypeStruct((t, D), BF16),
                  compiler_params=_cp("parallel"))(ya, yb, yc, wbr, z, z, z)


def merge_bwd(doutb, wo, l, ya, yb, yc, wbr, z, *, name):
    t = ya.shape[0]
    tm = min(256, t)

    def body(do_ref, wo_ref, ya_ref, yb_ref, yc_ref, w_ref, g0_ref, g1_ref, g2_ref,
             dya_ref, dyb_ref, dyc_ref, dp0_ref, dp1_ref, dp2_ref, dg0_ref, dg1_ref, dg2_ref):
        dm = _dot_nt(do_ref[...], wo_ref[...])
        ys = (ya_ref, yb_ref, yc_ref)
        gs = (g0_ref, g1_ref, g2_ref)
        dys = (dya_ref, dyb_ref, dyc_ref)
        dps = (dp0_ref, dp1_ref, dp2_ref)
        dgs = (dg0_ref, dg1_ref, dg2_ref)
        for j in range(3):
            s = _sigmoid(gs[j][...])
            pj = _dot(ys[j][...], w_ref[j])
            dpb = (dm * s).astype(BF16)
            dps[j][...] = dpb
            dgs[j][...] = (dm * pj * s * (1.0 - s)).astype(BF16)
            dys[j][...] = _dot_nt(dpb, w_ref[j])

    yshape = jax.ShapeDtypeStruct((t, BW), F32)
    dshape = jax.ShapeDtypeStruct((t, D), BF16)
    return _pcall(body, name=name, grid=(t // tm,),
                  in_specs=[_rows(tm, D), _layer(l, (D, D))] + [_rows(tm, BW)] * 3
                  + [_fix((3, BW, D))] + [_rows(tm, D, G0 // D + j) for j in range(3)],
                  out_specs=[_rows(tm, BW)] * 3 + [_rows(tm, D)] * 6,
                  out_shape=[yshape] * 3 + [dshape] * 6,
                  compiler_params=_cp("parallel"))(doutb, wo, ya, yb, yc, wbr, z, z, z)


def adamw(w, g, m, v, *, name):
    nl, r, c = w.shape
    tm = _row_tile(r)

    def body(w_ref, g_ref, m_ref, v_ref, d_ref, mo_ref, vo_ref):
        gg = g_ref[...]
        mn = ADAM_B1 * m_ref[...] + (1.0 - ADAM_B1) * gg
        vn = ADAM_B2 * v_ref[...] + (1.0 - ADAM_B2) * (gg * gg)
        m_hat = mn / (1.0 - ADAM_B1 ** ADAM_STEP)
        v_hat = vn / (1.0 - ADAM_B2 ** ADAM_STEP)
        d_ref[...] = -ADAM_LR * (m_hat / (jnp.sqrt(v_hat) + ADAM_EPS) + ADAM_WD * w_ref[...])
        mo_ref[...] = mn
        vo_ref[...] = vn

    shp = jax.ShapeDtypeStruct((nl, r, c), F32)
    blk = pl.BlockSpec((None, tm, c), lambda l, i: (l, i, 0))
    return _pcall(body, name=name, grid=(nl, r // tm), in_specs=[blk] * 4, out_specs=[blk] * 3,
                  out_shape=[shp] * 3, compiler_params=_cp("parallel", "parallel"))(w, g, m, v)


def _place():
    return lax.axis_index("x"), lax.axis_index("y"), lax.axis_index("c")


def _remote(src, dst, send_sems, recv_sems, k, to):
    return pltpu.make_async_remote_copy(src_ref=src, dst_ref=dst, send_sem=send_sems.at[k],
                                        recv_sem=recv_sems.at[k], device_id=to, device_id_type=MESH)


HBM = pl.BlockSpec(memory_space=pltpu.HBM)
SEM = pl.BlockSpec(memory_space=pltpu.SEMAPHORE)
EFFECT = pltpu.SideEffectType.DATAFLOW_SIDE_EFFECTING


def gather_first(shards):
    n = len(shards)

    def body(*refs):
        ins, outs, lands = refs[:n], refs[n:2 * n], refs[2 * n:3 * n]
        send_sems, recv_sems, own_send, own_recv = refs[3 * n:]
        x, y, c = _place()
        sib = (x, y, 1 - c)
        chips = [(1 - x, y), (x, 1 - y), (1 - x, 1 - y)]
        k_me = 2 * x + y
        mine = []
        for t in range(n):
            mine.append(_remote(ins[t].at[0], outs[t].at[0, k_me], own_send, own_recv, 2 * t, sib))
            mine.append(_remote(ins[t].at[1], lands[t].at[0, k_me], own_send, own_recv, 2 * t + 1, sib))
        for cp in mine:
            cp.start()

        def slot(t, chip):
            return outs[t].at[0, 2 * chip[0] + chip[1]]

        @pl.when(c == 0)
        def _():
            first = [_remote(ins[t].at[0], outs[t].at[0, k_me], send_sems, recv_sems, 6 * t + j, (*chip, 0))
                     for t in range(n) for j, chip in enumerate(chips)]
            for cp in first:
                cp.start()
            passed = []
            for t in range(n):
                for j, chip in enumerate(chips):
                    _remote(slot(t, chip), slot(t, chip), send_sems, recv_sems, 6 * t + j, (*chip, 0)).wait_recv()
                    cp = _remote(slot(t, chip), slot(t, chip), send_sems, recv_sems, 6 * t + 3 + j, sib)
                    cp.start()
                    passed.append(cp)
            for cp in first + passed:
                cp.wait_send()

        @pl.when(c == 1)
        def _():
            for t in range(n):
                for j, chip in enumerate(chips):
                    _remote(slot(t, chip), slot(t, chip), send_sems, recv_sems, 6 * t + 3 + j, sib).wait_recv()

        for cp in mine:
            cp.wait()

    shape = [jax.ShapeDtypeStruct((1, 4) + s.shape[1:], s.dtype) for s in shards]
    out = _pcall(body, name="gather_first", in_specs=[ANY] * n, out_specs=[ANY] * (2 * n), out_shape=shape + shape,
                 scratch_shapes=[pltpu.SemaphoreType.DMA((6 * n,)), pltpu.SemaphoreType.DMA((6 * n,)),
                                 pltpu.SemaphoreType.DMA((2 * n,)), pltpu.SemaphoreType.DMA((2 * n,))])(*shards)
    return out[:n], out[n:]


def _rest_copies(ins, lands, send_sems, recv_sems):
    x, y, c = _place()
    chips = [(1 - x, y), (x, 1 - y), (1 - x, 1 - y)]
    copies, arrivals = [], []
    for t in range(len(ins)):
        for j, chip in enumerate(chips):
            for to in range(2):
                copies.append(pltpu.make_async_remote_copy(
                    src_ref=ins[t].at[1], dst_ref=lands[t].at[0, 2 * x + y], send_sem=send_sems.at[6 * t + 2 * j + to],
                    recv_sem=recv_sems.at[3 * t + j], device_id=(*chip, to), device_id_type=MESH))
            blk = lands[t].at[0, 2 * chip[0] + chip[1]]
            arrivals.append(pltpu.make_async_remote_copy(
                src_ref=blk, dst_ref=blk, send_sem=send_sems.at[6 * t + 2 * j], recv_sem=recv_sems.at[3 * t + j],
                device_id=(*chip, 1), device_id_type=MESH))
    return copies, arrivals


def gather_rest_start(shards, lands):
    n = len(shards)

    def body(*refs):
        ins, lds = refs[:n], refs[n:2 * n]
        send_sems, recv_sems = refs[2 * n], refs[2 * n + 1]
        token = refs[-1]
        copies, _ = _rest_copies(ins, lds, send_sems, recv_sems)

        @pl.when(lax.axis_index("c") == 1)
        def _():
            for cp in copies:
                cp.start()

        token[...] = jnp.zeros_like(token)

    hbm = lambda a: pltpu.with_memory_space_constraint(a, pltpu.HBM)
    out = _pcall(body, name="gather_rest_start", in_specs=[HBM] * (2 * n),
                 out_specs=[SEM, SEM] + [HBM] * (2 * n) + [pl.BlockSpec(memory_space=pltpu.VMEM)],
                 out_shape=[pltpu.SemaphoreType.DMA((6 * n,)), pltpu.SemaphoreType.DMA((3 * n,))]
                 + [pltpu.HBM(a.shape, a.dtype) for a in shards] + [pltpu.HBM(a.shape, a.dtype) for a in lands]
                 + [jax.ShapeDtypeStruct((8, 128), F32)],
                 input_output_aliases={i: 2 + i for i in range(2 * n)},
                 compiler_params=pltpu.CompilerParams(has_side_effects=EFFECT))(
                     *[hbm(a) for a in shards], *[hbm(a) for a in lands])
    return out[0], out[1], out[2:2 + n], out[2 + n:2 + 2 * n], out[-1]


def gather_rest_wait(send_sems, recv_sems, srcs, lands, after):
    n = len(srcs)

    def body(*refs):
        ins, lds = refs[:n], refs[n:2 * n]
        s_sems, r_sems = refs[2 * n], refs[2 * n + 1]
        copies, arrivals = _rest_copies(ins, lds, s_sems, r_sems)

        @pl.when(lax.axis_index("c") == 1)
        def _():
            for cp in copies:
                cp.wait_send()

        for cp in arrivals:
            cp.wait_recv()

    out = _pcall(body, name="gather_rest_wait", in_specs=[HBM] * (2 * n) + [SEM, SEM, ANY],
                 out_specs=[HBM] * (2 * n),
                 out_shape=[pltpu.HBM(a.shape, a.dtype) for a in srcs] + [pltpu.HBM(a.shape, a.dtype) for a in lands],
                 input_output_aliases={i: i for i in range(2 * n)},
                 compiler_params=pltpu.CompilerParams(has_side_effects=EFFECT))(
                     *srcs, *lands, send_sems, recv_sems, after)
    return out[n:]


def pair_send(gl, owner, layer):
    n = len(gl)

    def body(*refs):
        ins, outs = refs[:n], refs[n:2 * n]
        send_sems, recv_sems = refs[2 * n:]
        x, y, c = _place()
        sib = (x, y, 1 - c)
        cps = [_remote(ins[t], outs[t], send_sems, recv_sems, t, sib) for t in range(n)]
        for core in range(2):
            @pl.when(c == core)
            def _():
                for cp in _owned(cps, owner, 1 - core, per=1):
                    cp.start()
                for cp in _owned(cps, owner, 1 - core, per=1):
                    cp.wait_send()
                for cp in _owned(cps, owner, core, per=1):
                    cp.wait_recv()

    return _pcall(body, name="pair_send_l%d" % layer, in_specs=[ANY] * n, out_specs=[ANY] * n,
                  out_shape=[jax.ShapeDtypeStruct(a.shape, a.dtype) for a in gl],
                  scratch_shapes=[pltpu.SemaphoreType.DMA((n,)), pltpu.SemaphoreType.DMA((n,))])(*gl)


def _chip_copies(ins, outs, send_sems, recv_sems):
    x, y, c = _place()
    chips = [(1 - x, y), (x, 1 - y), (1 - x, 1 - y)]
    return [_remote(ins[t].at[2 * chip[0] + chip[1]], outs[t].at[j], send_sems, recv_sems, 3 * t + j, (*chip, c))
            for t in range(len(ins)) for j, chip in enumerate(chips)]


def _owned(cps, owner, core, per=3):
    return [cp for k, cp in enumerate(cps) if owner[k // per] == core]


def _all_to_all(g_ref, o_ref, send_sems, recv_sems):
    x, y, c = _place()
    cps = []
    for r in range(1, 8):
        dx, dy, dc = (r >> 2) & 1, (r >> 1) & 1, r & 1
        to = (x if dx == 0 else 1 - x, y if dy == 0 else 1 - y, c if dc == 0 else 1 - c)
        cps.append(_remote(g_ref, o_ref.at[r - 1], send_sems, recv_sems, r - 1, to))
    return cps


def chip_send(s1, owner, layer, gs):
    n = len(s1)

    def body(*refs):
        ins, g_ref, outs, o_ref = refs[:n], refs[n], refs[n + 1:2 * n + 1], refs[2 * n + 1]
        send_sems, recv_sems, small_send, small_recv = refs[2 * n + 2:]
        small = _all_to_all(g_ref, o_ref, small_send, small_recv)
        for cp in small:
            cp.start()
        cps = _chip_copies(ins, outs, send_sems, recv_sems)
        for core in range(2):
            @pl.when(lax.axis_index("c") == core)
            def _():
                for cp in _owned(cps, owner, core):
                    cp.start()
                for cp in _owned(cps, owner, core):
                    cp.wait()
        for cp in small:
            cp.wait()

    out = _pcall(body, name="chip_send_l%d" % layer, in_specs=[ANY] * (n + 1), out_specs=[ANY] * (n + 1),
                 out_shape=[jax.ShapeDtypeStruct((3,) + a.shape[1:], a.dtype) for a in s1]
                 + [jax.ShapeDtypeStruct((7,) + gs.shape, gs.dtype)],
                 scratch_shapes=[pltpu.SemaphoreType.DMA((3 * n,)), pltpu.SemaphoreType.DMA((3 * n,)),
                                 pltpu.SemaphoreType.DMA((7,)), pltpu.SemaphoreType.DMA((7,))])(*s1, gs)
    return out[:n], out[n]


def chip_send_start(s1, owner, layer):
    n = len(s1)
    land = [lax.empty((3,) + a.shape[1:], a.dtype) for a in s1]

    def body(*refs):
        ins, lands = refs[:n], refs[n:2 * n]
        send_sems, recv_sems = refs[2 * n], refs[2 * n + 1]
        token = refs[-1]
        cps = _chip_copies(ins, lands, send_sems, recv_sems)
        for core in range(2):
            @pl.when(lax.axis_index("c") == core)
            def _():
                for cp in _owned(cps, owner, core):
                    cp.start()

        token[...] = jnp.zeros_like(token)

    hbm = lambda a: pltpu.with_memory_space_constraint(a, pltpu.HBM)
    out = _pcall(body, name="chip_send_start_l%d" % layer, in_specs=[HBM] * (2 * n),
                 out_specs=[SEM, SEM] + [HBM] * (2 * n) + [pl.BlockSpec(memory_space=pltpu.VMEM)],
                 out_shape=[pltpu.SemaphoreType.DMA((3 * n,)), pltpu.SemaphoreType.DMA((3 * n,))]
                 + [pltpu.HBM(a.shape, a.dtype) for a in s1] + [pltpu.HBM(a.shape, a.dtype) for a in land]
                 + [jax.ShapeDtypeStruct((8, 128), F32)],
                 input_output_aliases={i: 2 + i for i in range(2 * n)},
                 compiler_params=pltpu.CompilerParams(has_side_effects=EFFECT))(
                     *[hbm(a) for a in s1], *[hbm(a) for a in land])
    return out[0], out[1], out[2:2 + n], out[2 + n:2 + 2 * n], out[-1]


def chip_send_wait(send_sems, recv_sems, srcs, lands, after, owner, layer):
    n = len(srcs)

    def body(*refs):
        ins, lds = refs[:n], refs[n:2 * n]
        s_sems, r_sems = refs[2 * n], refs[2 * n + 1]
        cps = _chip_copies(ins, lds, s_sems, r_sems)
        for core in range(2):
            @pl.when(lax.axis_index("c") == core)
            def _():
                for cp in _owned(cps, owner, core):
                    cp.wait_send()
                    cp.wait_recv()

    out = _pcall(body, name="chip_send_wait_l%d" % layer, in_specs=[HBM] * (2 * n) + [SEM, SEM, ANY],
                 out_specs=[HBM] * (2 * n),
                 out_shape=[pltpu.HBM(a.shape, a.dtype) for a in srcs] + [pltpu.HBM(a.shape, a.dtype) for a in lands],
                 input_output_aliases={i: i for i in range(2 * n)},
                 compiler_params=pltpu.CompilerParams(has_side_effects=EFFECT))(
                     *srcs, *lands, send_sems, recv_sems, after)
    return out[n:]


def pair_share(s2, owner):
    n = len(s2)

    def body(*refs):
        ins, outs = refs[:n], refs[n:2 * n]
        send_sems, recv_sems = refs[2 * n:]
        x, y, c = _place()
        sib = (x, y, 1 - c)
        cps = [_remote(ins[t], outs[t], send_sems, recv_sems, t, sib) for t in range(n)]
        for core in range(2):
            @pl.when(c == core)
            def _():
                for cp in _owned(cps, owner, core, per=1):
                    cp.start()
                for cp in _owned(cps, owner, core, per=1):
                    cp.wait_send()
                for cp in _owned(cps, owner, 1 - core, per=1):
                    cp.wait_recv()

    return _pcall(body, name="pair_share", in_specs=[ANY] * n, out_specs=[ANY] * n,
                  out_shape=[jax.ShapeDtypeStruct(a.shape, a.dtype) for a in s2],
                  input_output_aliases={t: t for t in range(n)},
                  scratch_shapes=[pltpu.SemaphoreType.DMA((n,)), pltpu.SemaphoreType.DMA((n,))])(*s2)


def _row_tile(rows):
    return _pick(rows, (256, 352, 128, 64, 32, 16))


def pair_add_layer(g, rb, core, owner, *, name):
    _, rows, width = g.shape
    tr = _row_tile(rows)

    def body(c_ref, g_ref, r_ref, o_ref, ob_ref):
        @pl.when(c_ref[0] == owner)
        def _():
            s = g_ref[...] + r_ref[...]
            o_ref[...] = s
            ob_ref[...] = s.astype(BF16)

    def at(k, i, c_ref):
        mine = c_ref[0] == owner
        return (jnp.where(mine, k, 0), jnp.where(mine, i, 0), 0)

    blk = pl.BlockSpec((None, tr, width), at)
    gs = pltpu.PrefetchScalarGridSpec(num_scalar_prefetch=1, grid=(4, rows // tr), in_specs=[blk, blk],
                                      out_specs=[blk, blk])
    return _pcall(body, name=name, grid_spec=gs,
                  out_shape=[jax.ShapeDtypeStruct(g.shape, F32), jax.ShapeDtypeStruct(g.shape, BF16)],
                  compiler_params=_cp("parallel", "parallel"))(core, g, rb)


def chip_add_layers(s1, rb2, chip, core, owner, *, name):
    _, rows, width = s1[0].shape
    tr = _row_tile(rows)

    def body(k_ref, c_ref, s0_ref, s1_ref, r0_ref, r1_ref, o_ref):
        @pl.when(c_ref[0] == owner)
        def _():
            first = pl.program_id(0) == 0
            s = jnp.where(first, s0_ref[...], s1_ref[...])
            r = jnp.where(first, r0_ref[...], r1_ref[...]).astype(F32)
            o_ref[...] = ((s + r[0]) + r[1]) + r[2]

    def s_spec(layer):
        def at(l, i, k_ref, c_ref):
            use = jnp.logical_and(l == layer, c_ref[0] == owner)
            return (jnp.where(use, k_ref[0], 0), jnp.where(use, i, 0), 0)
        return pl.BlockSpec((None, tr, width), at)

    def r_spec(layer):
        def at(l, i, k_ref, c_ref):
            return (0, jnp.where(jnp.logical_and(l == layer, c_ref[0] == owner), i, 0), 0)
        return pl.BlockSpec((3, tr, width), at)

    def out_at(l, i, k_ref, c_ref):
        mine = c_ref[0] == owner
        return (jnp.where(mine, l, 0), jnp.where(mine, i, 0), 0)

    gs = pltpu.PrefetchScalarGridSpec(
        num_scalar_prefetch=2, grid=(DEPTH, rows // tr),
        in_specs=[s_spec(0), s_spec(1), r_spec(0), r_spec(1)],
        out_specs=pl.BlockSpec((None, tr, width), out_at))
    return _pcall(body, name=name, grid_spec=gs, out_shape=jax.ShapeDtypeStruct((DEPTH, rows, width), F32),
                  compiler_params=_cp("parallel", "parallel"))(chip, core, s1[0], s1[1], rb2[0], rb2[1])


def small_add(gs_own, slots, me):
    rows, width = gs_own.shape
    tr = _pick(rows, (64, 32, 16, 8))

    def body(me_ref, g_ref, s_ref, o_ref):
        me_v = me_ref[0]
        total = None
        for d in range(8):
            rel = jnp.bitwise_xor(me_v, d)
            val = jnp.where(rel == 0, g_ref[...], s_ref[jnp.maximum(rel - 1, 0)])
            total = val if total is None else total + val
        o_ref[...] = total

    gs = pltpu.PrefetchScalarGridSpec(
        num_scalar_prefetch=1, grid=(rows // tr,),
        in_specs=[pl.BlockSpec((tr, width), lambda i, m_ref: (i, 0)),
                  pl.BlockSpec((7, tr, width), lambda i, m_ref: (0, i, 0))],
        out_specs=pl.BlockSpec((tr, width), lambda i, m_ref: (i, 0)))
    return _pcall(body, name="small_add", grid_spec=gs, out_shape=jax.ShapeDtypeStruct((rows, width), F32),
                  compiler_params=_cp("parallel"))(me, gs_own, slots)


SHARDED = (("ffn1_w_up", (D, UPW)), ("ffn1_w_down", (DFF // 4, D)), ("w_in", (D, D_IN // 4)),
           ("conv_w", (4, BW // 4)), ("gla_w_g2", (LOW_W, 64)), ("w_branch", (3 * BW, D // 4)),
           ("w_out", (D // 4, D)), ("ffn2_w_up", (D, UPW)), ("ffn2_w_down", (DFF // 4, D)),
           ("ple_w_proj", (PLE, D // 4)), ("ple_w_gate", (D // 4, D)))
OWNER = tuple(0 if n in ("ffn1_w_up", "w_in", "w_out") else 1 for n, _ in SHARDED)
SMALL = ("ln1_g", "ln1_b", "conv_b", "lru_wa", "lru_ba", "lru_wx", "lru_bx", "lru_lambda", "gla_b_g",
         "gla_norm_g", "fox_b_f", "ln2_g", "ln2_b", "ln3_g", "ln3_b", "ple_b_gate", "ln4_g", "ln4_b")
WEIGHTS = ('ffn1_w_up', 'ffn1_w_down', 'ln1_g', 'ln1_b', 'w_in', 'conv_w', 'conv_b', 'lru_wa', 'lru_ba',
           'lru_wx', 'lru_bx', 'lru_lambda', 'gla_w_g2', 'gla_b_g', 'gla_norm_g', 'fox_b_f', 'w_branch',
           'w_out', 'ln2_g', 'ln2_b', 'ffn2_w_up', 'ffn2_w_down', 'ln3_g', 'ln3_b', 'ple_w_proj',
           'ple_w_gate', 'ple_b_gate', 'ln4_g', 'ln4_b')


def _cols_join(parts):
    return jnp.concatenate([parts[k] for k in range(4)], axis=-1)


def _cols_split(full):
    r, c4 = full.shape
    return full.reshape(r, 4, c4 // 4).transpose(1, 0, 2)


def _regroup_in(w):
    pad = jnp.zeros(w.shape[:-1] + (ZW - D_IN,), w.dtype)
    fox_q = (w[..., 2576:3088] * FOX_SCALE).astype(w.dtype)
    return jnp.concatenate([w[..., 0:2048], w[..., 2064:2576], fox_q, w[..., 3088:4112], w[..., 4120:7192],
                            w[..., 2048:2064], w[..., 4112:4120], pad], axis=-1)


_IN_RUNS = ((0, 2048, 0, 1.0), (2048, 2064, 7168, 1.0), (2064, 2576, 2048, 1.0), (2576, 3088, CQ, FOX_SCALE),
            (3088, 4112, CK, 1.0), (4112, 4120, 7184, 1.0), (4120, D_IN, 4096, 1.0))


def _regroup_out_shards(g):
    w = D_IN // 4
    shards = []
    for k in range(4):
        pieces = []
        for a, b, new, f in _IN_RUNS:
            lo, hi = max(a, k * w), min(b, (k + 1) * w)
            if lo < hi:
                piece = g[:, new + lo - a:new + hi - a]
                pieces.append(piece if f == 1.0 else piece * f)
        shards.append(jnp.concatenate(pieces, axis=1))
    return jnp.stack(shards)


def _block_diag(w):
    eye = jnp.eye(8, dtype=w.dtype)
    return (eye[:, None, :, None] * w[:, :, None, :]).reshape(BW, BW)


def _diag_blocks(dense):
    return jnp.stack([dense[64 * n:64 * (n + 1), 64 * n:64 * (n + 1)] for n in range(8)])


def _layer_weights(gw, small, l):
    w = {"up1": gw["ffn1_w_up"], "up2": gw["ffn2_w_up"],
         "dn1": gw["ffn1_w_down"].reshape(1, DFF, D), "dn2": gw["ffn2_w_down"].reshape(1, DFF, D),
         "wo": gw["w_out"].reshape(1, D, D), "wgt": gw["ple_w_gate"].reshape(1, D, D)}
    w["win"] = _regroup_in(_cols_join(gw["w_in"][0]))
    w["cw"] = _cols_join(gw["conv_w"][0])
    w["wa"] = _block_diag(small["lru_wa"][l]).astype(BF16)
    w["wx"] = _block_diag(small["lru_wx"][l]).astype(BF16)
    w["wg2p"] = jnp.pad(_cols_join(gw["gla_w_g2"][0]), ((0, 128 - LOW_W), (0, 0)))
    w["wbr"] = _cols_join(gw["w_branch"][0].reshape(4, 3, BW, D // 4))
    w["wp"] = _cols_join(gw["ple_w_proj"][0])
    for n in ("ln1_g", "ln1_b", "ln2_g", "ln2_b", "ln3_g", "ln3_b", "ln4_g", "ln4_b", "conv_b", "lru_ba",
              "lru_bx", "lru_lambda", "gla_b_g", "gla_norm_g", "ple_b_gate"):
        w[n] = small[n][l][None, :]
    w["bfp"] = jnp.pad(small["fox_b_f"][l], (LOW_W, 128 - LOW_W - FOX_H))[None, :]
    return w


def _heads_t(a):
    ht = a[:, LOW_W:LOW_W + FOX_H].T
    return ht[:, None, :], jnp.broadcast_to(ht[:, :, None], ht.shape + (128,))


def _layer_fwd(x, xb, pb, w, l):
    s = {"x0": x, "x0b": xb}
    tag = "l%d_" % l
    gate, up, act = ffn_up(xb, w["up1"], 0, name=tag + "ffn1_up")
    r1, x1, x1b = matmul_res_ln(act, w["dn1"], 0, x, w["ln1_g"], w["ln1_b"], mm_scale=0.5, name=tag + "ffn1_down")
    s.update(gate1=gate, up1=up, act1=act, r1=r1, x1=x1, x1b=x1b)
    z, zb = matmul(x1b, w["win"], also_bf16=True, tm=1024, tn=_pick(ZW, (2432,)), name=tag + "mix_in")
    xc, xcb, h, ya = lru_fwd(z, w["cw"], w["conv_b"], w["wa"], w["wx"], w["lru_ba"], w["lru_bx"],
                             w["lru_lambda"], name=tag + "lru_fwd")
    yb, states = gla_fwd(z, zb, w["wg2p"], w["gla_b_g"], w["gla_norm_g"], name=tag + "gla_fwd")
    fcum = fox_fcum(z, w["bfp"], name=tag + "fox_fcum")
    fq, fk = _heads_t(fcum)
    yc, ycf, lse = fox_fwd_t(zb, fq, fk, name=tag + "fox_fwd")
    merged = merge_fwd(ya, yb, yc, w["wbr"], z, name=tag + "merge_fwd")
    r2, x2, x2b = matmul_res_ln(merged, w["wo"], 0, x1, w["ln2_g"], w["ln2_b"], mm_scale=1.0, name=tag + "mix_out")
    s.update(z=z, zb=zb, xc=xc, xcb=xcb, h=h, ya=ya, yb=yb, states=states, fq=fq, fk=fk, yc=yc, ycf=ycf,
             lse=lse, merged=merged, r2=r2, x2=x2, x2b=x2b)
    gate, up, act = ffn_up(x2b, w["up2"], 0, name=tag + "ffn2_up")
    r3, x3, x3b = matmul_res_ln(act, w["dn2"], 0, x2, w["ln3_g"], w["ln3_b"], mm_scale=0.5, name=tag + "ffn2_down")
    s.update(gate2=gate, up2=up, act2=act, r3=r3, x3=x3, x3b=x3b)
    r4, x4, x4b = ple_fwd(x3b, x3, pb, w["wgt"], 0, w["wp"], w["ple_b_gate"], w["ln4_g"], w["ln4_b"],
                          name=tag + "ple_fwd")
    s.update(r4=r4, pb=pb)
    return x4, x4b, s


def _ffn_bwd(dy, s, w, n, xin_b, l, tag):
    k = {"1": ("r1", "ln1_g", "gate1", "up1", "act1"), "2": ("r3", "ln3_g", "gate2", "up2", "act2")}[n]
    dr, dfb, dg, db = ln_bwd(dy, s[k[0]], w[k[1]], out_scale=0.5, name=tag + "ln_bwd")
    dgate, dup = ffn_down_bwd(dfb, w["dn" + n], 0, s[k[2]], s[k[3]], name=tag + "down_bwd")
    dx = ffn_dx(dgate, dup, w["up" + n], 0, dr, name=tag + "dx")
    dwup = matmul_tn_up(xin_b, dgate, dup, name=tag + "dw_up")
    dwdn = matmul_tn(s[k[4]], dfb, name=tag + "dw_down").reshape(4, DFF // 4, D)
    return dx, dwup, dwdn, dg[0], db[0]


def _layer_bwd(dy, s, w, l):
    g = {}
    tag = "l%d_" % l
    dr4, dglb, dpeb, dg4, db4, dbg = ple_bwd(dy, s["r4"], s["x3b"], s["pb"], w["wgt"], 0, w["wp"], w["ple_b_gate"],
                                             w["ln4_g"], name=tag + "ple_bwd")
    dx3 = matmul(dglb, w["wgt"], nt=True, b_lead=(0,), res=dr4, res_scale=ALPHA, tm=1024, tn=1024,
                 name=tag + "ple_dx")
    g["ple_w_gate"] = matmul_tn(s["x3b"], dglb, name=tag + "ple_dw_gate").reshape(4, D // 4, D)
    g["ple_w_proj"] = _cols_split(matmul_tn(s["pb"], dpeb, name=tag + "ple_dw_proj"))
    g["ln4_g"], g["ln4_b"], g["ple_b_gate"] = dg4[0], db4[0], dbg[0]
    dx2, g["ffn2_w_up"], g["ffn2_w_down"], g["ln3_g"], g["ln3_b"] = _ffn_bwd(dx3, s, w, "2", s["x2b"], l,
                                                                             tag + "ffn2_")
    dr2, doutb, dg2, db2 = ln_bwd(dx2, s["r2"], w["ln2_g"], out_scale=1.0, name=tag + "mix_ln_bwd")
    g["ln2_g"], g["ln2_b"] = dg2[0], db2[0]
    g["w_out"] = matmul_tn(s["merged"], doutb, name=tag + "dw_out").reshape(4, D // 4, D)
    z, zb = s["z"], s["zb"]
    (dya, dyb, dyc, dp0, dp1, dp2, dgl0, dgl1, dgl2) = merge_bwd(
        doutb, w["wo"], 0, s["ya"], s["yb"], s["yc"], w["wbr"], z, name=tag + "merge_bwd")
    dwbr = jnp.stack([matmul_tn(s["ya"], dp0, name=tag + "dw_br0"), matmul_tn(s["yb"], dp1, name=tag + "dw_br1"),
                      matmul_tn(s["yc"], dp2, name=tag + "dw_br2")])
    g["w_branch"] = _cols_split(dwbr.reshape(3 * BW, D))
    day, dxc, dprb, dpib, dba, dbx, dlam = lru_bwd(dya, z, s["h"], s["xc"], w["wa"], w["wx"],
                                                   w["lru_ba"], w["lru_bx"], w["lru_lambda"], name=tag + "lru_bwd")
    dax, dcw, dcb = conv_bwd(dxc, z, w["cw"], name=tag + "conv_bwd")
    g["lru_wa"] = _diag_blocks(matmul_tn(s["xcb"], dprb, name=tag + "dw_lru_a"))
    g["lru_wx"] = _diag_blocks(matmul_tn(s["xcb"], dpib, name=tag + "dw_lru_x"))
    g["lru_ba"], g["lru_bx"], g["lru_lambda"] = dba[0], dbx[0], dlam[0]
    g["conv_w"], g["conv_b"] = _cols_split(dcw), dcb[0]
    dbq, dbk, dbv, dbr, dmisc_g, dpreb, dbgg, dng = gla_bwd(dyb, z, zb, s["states"], w["wg2p"], w["gla_b_g"],
                                                            w["gla_norm_g"], name=tag + "gla_bwd")
    miscb = zb[:, MISC:]
    g["gla_w_g2"] = _cols_split(matmul_tn(miscb, dpreb, name=tag + "dw_g2")[:LOW_W])
    g["gla_b_g"], g["gla_norm_g"] = dbgg[0], dng[0]
    dl = fox_delta(dyc, s["ycf"], name=tag + "fox_delta")
    t = z.shape[0]
    dlq = dl[:, :FOX_H].T[:, None, :]
    dcq, dck, dcv, dfk, dfq = fox_bwd_t(zb, dyc, s["fq"], s["fk"], s["lse"], dlq, name=tag + "fox_bwd")
    dfc = jnp.pad((dfk[:, 0, :] + dfq.reshape(FOX_H, t)).T, ((0, 0), (LOW_W, 128 - LOW_W - FOX_H)))
    dmiscb, dbf = fox_dcf(dfc, z, w["bfp"], dmisc_g, name=tag + "fox_dcf")
    g["fox_b_f"] = dbf[0, LOW_W:LOW_W + FOX_H]
    dz = jnp.concatenate([dax, day, dbq, dbk, dbv, dbr, dcq, dck, dcv, dgl0, dgl1, dgl2, dmiscb], axis=1)
    dx1 = matmul(dz, w["win"], nt=True, res=dr2, res_scale=ALPHA, tm=1024, tn=1024, tk=_pick(ZW, (2432,)),
                 name=tag + "mix_dx")
    g["w_in"] = _regroup_out_shards(matmul_tn(s["x1b"], dz, name=tag + "dw_in"))
    dx0, g["ffn1_w_up"], g["ffn1_w_down"], g["ln1_g"], g["ln1_b"] = _ffn_bwd(dx1, s, w, "1", s["x0b"], l,
                                                                             tag + "ffn1_")
    return dx0, g


def _local_step(x, p, target, gathered, small, after_last_layer=None):
    xcur = x
    xb = xcur.astype(BF16)
    layer_w, saved = [], []
    for l in range(DEPTH):
        w = _layer_weights(gathered(l, xcur), small, l)
        xcur, xb, s = _layer_fwd(xcur, xb, p[l].astype(BF16), w, l)
        layer_w.append(w)
        saved.append(s)
    dy, sq = loss_head(xcur, target, name="loss_head")
    grads = [None] * DEPTH
    for l in reversed(range(DEPTH)):
        dy, grads[l] = _layer_bwd(dy, saved[l], layer_w[l], l)
        if l == DEPTH - 1 and after_last_layer is not None:
            layer_w[l - 1]["ln4_g"] = layer_w[l - 1]["ln4_g"] + after_last_layer(grads[l])
    return 0.5 * jnp.sum(sq) / float(D), dy, grads


def kernel(x, p, ffn1_w_up, ffn1_w_down, ln1_g, ln1_b, w_in, conv_w, conv_b, lru_wa, lru_ba, lru_wx, lru_bx, lru_lambda, gla_w_g2, gla_b_g, gla_norm_g, fox_b_f, w_branch, w_out, ln2_g, ln2_b, ffn2_w_up, ffn2_w_down, ln3_g, ln3_b, ple_w_proj, ple_w_gate, ple_b_gate, ln4_g, ln4_b, loss_target, m_ffn1_w_up, m_ffn1_w_down, m_ln1_g, m_ln1_b, m_w_in, m_conv_w, m_conv_b, m_lru_wa, m_lru_ba, m_lru_wx, m_lru_bx, m_lru_lambda, m_gla_w_g2, m_gla_b_g, m_gla_norm_g, m_fox_b_f, m_w_branch, m_w_out, m_ln2_g, m_ln2_b, m_ffn2_w_up, m_ffn2_w_down, m_ln3_g, m_ln3_b, m_ple_w_proj, m_ple_w_gate, m_ple_b_gate, m_ln4_g, m_ln4_b, v_ffn1_w_up, v_ffn1_w_down, v_ln1_g, v_ln1_b, v_w_in, v_conv_w, v_conv_b, v_lru_wa, v_lru_ba, v_lru_wx, v_lru_bx, v_lru_lambda, v_gla_w_g2, v_gla_b_g, v_gla_norm_g, v_fox_b_f, v_w_branch, v_w_out, v_ln2_g, v_ln2_b, v_ffn2_w_up, v_ffn2_w_down, v_ln3_g, v_ln3_b, v_ple_w_proj, v_ple_w_gate, v_ple_b_gate, v_ln4_g, v_ln4_b):
    args = dict(locals())
    wts = {n: args[n] for n in WEIGHTS}
    mom = {n: args["m_" + n] for n in WEIGHTS}
    var = {n: args["v_" + n] for n in WEIGHTS}
    cx, cy, cc = lax.axis_index("x"), lax.axis_index("y"), lax.axis_index("c")

    names = [n for n, _ in SHARDED]
    shards = [wts[n].reshape((DEPTH,) + rc).astype(F32 if n == "conv_w" else BF16) for n, rc in SHARDED]
    first, lands = gather_first(shards)
    rest_send, rest_recv, rest_srcs, rest_lands, rest_token = gather_rest_start(shards, lands)
    small = {n: wts[n] for n in SMALL}
    small["ln1_g"] = small["ln1_g"] + rest_token[0, 0]

    def gathered(l, after):
        if l == 0:
            return dict(zip(names, first))
        return dict(zip(names, gather_rest_wait(rest_send, rest_recv, rest_srcs, rest_lands, after)))

    flight = {}
    core = jnp.reshape(cc, (1,)).astype(jnp.int32)
    chip = jnp.reshape(2 * cx + cy, (1,)).astype(jnp.int32)

    def chip_sum(gl, layer):
        lst = [gl[n] for n in names]
        rb = pair_send(lst, OWNER, layer)
        return [pair_add_layer(a, r, core, own, name="pair_add_l%d_%s" % (layer, n))
                for n, own, a, r in zip(names, OWNER, lst, rb)]

    def start_last_layer(gl):
        s1 = chip_sum(gl, DEPTH - 1)
        send_sems, recv_sems, srcs, lands, token = chip_send_start([sb for _, sb in s1], OWNER, DEPTH - 1)
        flight.update(s1=[sf for sf, _ in s1], sems=(send_sems, recv_sems), srcs=srcs, lands=lands)
        return token[0, 0]

    loss_local, dx, grads = _local_step(x[0], p[:, 0], loss_target[0], gathered, small, start_last_layer)
    loss = lax.psum(loss_local, ("x", "y", "c"))
    grad_x = dx[None]

    pieces, spans, row = [], {}, 0
    for n in SMALL:
        flat = jnp.stack([grads[l][n] for l in range(DEPTH)]).reshape(-1)
        rows = -(-flat.shape[0] // (8 * PACK_W)) * 8
        pieces.append(jnp.pad(flat, (0, rows * PACK_W - flat.shape[0])).reshape(rows, PACK_W))
        spans[n] = (row, rows)
        row += rows
    gs = jnp.concatenate(pieces, axis=0)

    s1_first = chip_sum(grads[0], 0)
    rb2_first, gs_all = chip_send([sb for _, sb in s1_first], OWNER, 0, gs)
    rb2_last = chip_send_wait(*flight["sems"], flight["srcs"], flight["lands"], dx, OWNER, DEPTH - 1)
    s2 = [chip_add_layers((sf0, sf1), (r0, r1), chip, core, own, name="chip_add_" + n)
          for n, own, (sf0, _), sf1, r0, r1 in zip(names, OWNER, s1_first, flight["s1"], rb2_first, rb2_last)]
    gsh = dict(zip(names, pair_share(s2, OWNER)))
    me = jnp.reshape(4 * cx + 2 * cy + cc, (1,)).astype(jnp.int32)
    gsum = small_add(gs, gs_all, me)

    gout, delta, new_m, new_v = {}, {}, {}, {}
    for n in WEIGHTS:
        shp = wts[n].shape
        if n in gsh:
            view = gsh[n].shape
            g = gsh[n]
        else:
            view = (1, DEPTH, wts[n].size // DEPTH)
            r0, rows = spans[n]
            g = gsum[r0:r0 + rows].reshape(-1)[:wts[n].size].reshape(view)
        d, mn, vn = adamw(wts[n].reshape(view), g, mom[n].reshape(view), var[n].reshape(view), name="adamw_" + n)
        gout[n], delta[n], new_m[n], new_v[n] = g.reshape(shp), d.reshape(shp), mn.reshape(shp), vn.reshape(shp)

    return (loss, grad_x, *[gout[n] for n in WEIGHTS], *[delta[n] for n in WEIGHTS],
            *[new_m[n] for n in WEIGHTS], *[new_v[n] for n in WEIGHTS])
```

```python
import functools
import math

import jax
import jax.numpy as jnp
from jax import lax
from jax.experimental import pallas as pl
from jax.experimental.pallas import tpu as pltpu

F32 = jnp.float32
BF16 = jnp.bfloat16

D = 1024
DFF = 2816
BW = 512
PLE = 256
DEPTH = 2
ALPHA = (2 * DEPTH) ** 0.25
LN_EPS = 1e-5
RMS_EPS = 1e-6
LRU_C = 8.0
GLA_TAU = 16.0
CHUNK = 64
D_IN = 7192
ZW = 7296
AX, AY, BQ, BK, BV, BR, CQ, CK, CV, G0, MISC = 0, 512, 1024, 1280, 1536, 2048, 2560, 3072, 3584, 4096, 7168
LOW_W, FOX_H = 16, 8
ADAM_LR, ADAM_B1, ADAM_B2, ADAM_EPS, ADAM_WD, ADAM_STEP = 0.001, 0.9, 0.999, 1e-08, 0.01, 10
PACK_W = 1024
VMEM_LIMIT = 56 << 20

MESH = pl.DeviceIdType.MESH
ANY = pl.BlockSpec(memory_space=pl.ANY)


def _pcall(body, **kw):
    return pl.pallas_call(body, **kw)


def _cp(*dims):
    return pltpu.CompilerParams(dimension_semantics=dims, vmem_limit_bytes=VMEM_LIMIT)


def _dot(a, b):
    return jnp.dot(a, b, preferred_element_type=F32)


def _dot_nt(a, b):
    return lax.dot_general(a, b, (((1,), (1,)), ((), ())), preferred_element_type=F32)


def _dot_tn(a, b):
    return lax.dot_general(a, b, (((0,), (0,)), ((), ())), preferred_element_type=F32)


def _dot_hi(a, b):
    return jnp.dot(a, b, preferred_element_type=F32, precision=lax.Precision.HIGHEST)


def _sigmoid(x):
    return 1.0 / (1.0 + jnp.exp(-x))


def _softplus(x):
    return jnp.maximum(x, 0.0) + jnp.log(1.0 + jnp.exp(-jnp.abs(x)))


def _log_sigmoid(x):
    return -_softplus(-x)


def _expm1(x):
    poly = x * (1.0 + x * (0.5 + x * (1.0 / 6.0 + x * (1.0 / 24.0 + x * (1.0 / 120.0 + x * (1.0 / 720.0))))))
    return jnp.where(jnp.abs(x) < 0.1, poly, jnp.exp(x) - 1.0)


_GELU_C = math.sqrt(2.0 / math.pi)


def _gelu(x):
    return 0.5 * x * (1.0 + jnp.tanh(_GELU_C * (x + 0.044715 * x * x * x)))


def _gelu_grad(x):
    t = jnp.tanh(_GELU_C * (x + 0.044715 * x * x * x))
    return 0.5 * (1.0 + t) + 0.5 * x * (1.0 - t * t) * _GELU_C * (1.0 + 3.0 * 0.044715 * x * x)


def _ln_stats(r):
    mu = jnp.mean(r, axis=-1, keepdims=True)
    xc = r - mu
    var = jnp.mean(xc * xc, axis=-1, keepdims=True)
    return xc, lax.rsqrt(var + LN_EPS)


def _pick(n, cands):
    for c in cands:
        if n % c == 0:
            return c
    return n


def _rows(tm, w, col=0):
    return pl.BlockSpec((tm, w), lambda i: (i, col))


def _fix(shape):
    nd = len(shape)
    return pl.BlockSpec(shape, lambda i: (0,) * nd)


def _col_chunks(n, width=256):
    return [slice(c, min(c + width, n)) for c in range(0, n, width)]


def _layer(l, shape):
    nd = len(shape)
    return pl.BlockSpec((None,) + tuple(shape), lambda i: (l,) + (0,) * nd)


def matmul(a, b, *, name, nt=False, b_lead=(), res=None, res_scale=1.0, also_bf16=False, tm=512, tn=512,
           tk=None):
    m, k = a.shape
    n = b.shape[-2] if nt else b.shape[-1]
    tm, tn = min(tm, m), min(tn, n)
    tk = k if tk is None else tk
    nk = k // tk
    has_res = res is not None
    lead = tuple(b_lead)
    dot = _dot_nt if nt else _dot

    def body(*refs):
        a_ref, b_ref = refs[0], refs[1]
        pos = 2
        r_ref = None
        if has_res:
            r_ref = refs[pos]
            pos += 1
        o_ref = refs[pos]
        pos += 1
        ob_ref = None
        if also_bf16:
            ob_ref = refs[pos]
            pos += 1

        def finish(v):
            if has_res:
                v = v + res_scale * r_ref[...]
            o_ref[...] = v
            if also_bf16:
                ob_ref[...] = v.astype(BF16)

        if nk == 1:
            finish(dot(a_ref[...], b_ref[...]))
            return
        acc = refs[pos]
        kk = pl.program_id(2)

        @pl.when(kk == 0)
        def _():
            acc[...] = jnp.zeros_like(acc)

        acc[...] += dot(a_ref[...], b_ref[...])

        @pl.when(kk == nk - 1)
        def _():
            finish(acc[...])

    none = (None,) * len(lead)
    if nt:
        b_spec = pl.BlockSpec(none + (tn, tk), lambda j, i, kk: lead + (j, kk))
    else:
        b_spec = pl.BlockSpec(none + (tk, tn), lambda j, i, kk: lead + (kk, j))
    in_specs = [pl.BlockSpec((tm, tk), lambda j, i, kk: (i, kk)), b_spec]
    args = [a, b]
    if has_res:
        in_specs.append(pl.BlockSpec((tm, tn), lambda j, i, kk: (i, j)))
        args.append(res)
    out_shape = [jax.ShapeDtypeStruct((m, n), F32)]
    out_specs = [pl.BlockSpec((tm, tn), lambda j, i, kk: (i, j))]
    if also_bf16:
        out_shape.append(jax.ShapeDtypeStruct((m, n), BF16))
        out_specs.append(pl.BlockSpec((tm, tn), lambda j, i, kk: (i, j)))
    out = _pcall(body, name=name, grid=(n // tn, m // tm, nk), in_specs=in_specs, out_specs=out_specs,
                 out_shape=out_shape, scratch_shapes=[pltpu.VMEM((tm, tn), F32)] if nk > 1 else [],
                 compiler_params=_cp("parallel", "parallel", "arbitrary"))(*args)
    return out if also_bf16 else out[0]


def matmul_tn(a, b, *, name):
    t, k = a.shape
    n = b.shape[1]
    tk = _pick(k, (1024, 1408, 512, 256, 128))
    tn = _pick(n, (1024, 1408, 2432, 512, 256, 128))
    tt = min(1024 if tk * tn > (1 << 20) else 2048, t)
    nt = t // tt

    def body(a_ref, b_ref, o_ref):
        @pl.when(pl.program_id(2) == 0)
        def _():
            o_ref[...] = jnp.zeros_like(o_ref)

        o_ref[...] += _dot_tn(a_ref[...], b_ref[...])

    return _pcall(body, name=name, grid=(k // tk, n // tn, nt),
                  in_specs=[pl.BlockSpec((tt, tk), lambda i, j, s: (s, i)),
                            pl.BlockSpec((tt, tn), lambda i, j, s: (s, j))],
                  out_specs=pl.BlockSpec((tk, tn), lambda i, j, s: (i, j)),
                  out_shape=jax.ShapeDtypeStruct((k, n), F32),
                  compiler_params=_cp("parallel", "parallel", "arbitrary"))(a, b)


UPW = 1408


def matmul_tn_up(a, dgate, dup, *, name):
    t, k = a.shape
    tt = min(1024, t)
    tk = 1024

    def body(a_ref, g_ref, u_ref, o_ref):
        j = pl.program_id(1)

        @pl.when(pl.program_id(2) == 0)
        def _():
            o_ref[...] = jnp.zeros_like(o_ref)

        @pl.when(j < 2)
        def _():
            o_ref[...] += _dot_tn(a_ref[...], g_ref[...])

        @pl.when(j >= 2)
        def _():
            o_ref[...] += _dot_tn(a_ref[...], u_ref[...])

    return _pcall(body, name=name, grid=(k // tk, 4, t // tt),
                  in_specs=[pl.BlockSpec((tt, tk), lambda i, j, s: (s, i)),
                            pl.BlockSpec((tt, UPW), lambda i, j, s: (jnp.where(j < 2, s, 0), jnp.minimum(j, 1))),
                            pl.BlockSpec((tt, UPW), lambda i, j, s: (jnp.where(j >= 2, s, 0), jnp.maximum(j - 2, 0)))],
                  out_specs=pl.BlockSpec((None, tk, UPW), lambda i, j, s: (j, i, 0)),
                  out_shape=jax.ShapeDtypeStruct((4, k, UPW), F32),
                  compiler_params=_cp("parallel", "parallel", "arbitrary"))(a, dgate, dup)


def ffn_dx(dgate, dup, wup, l, res, *, name):
    t = dgate.shape[0]
    tm, tn = min(1024, t), 1024

    def body(g_ref, u_ref, w_ref, r_ref, o_ref, acc):
        kk = pl.program_id(2)

        @pl.when(kk == 0)
        def _():
            acc[...] = jnp.zeros_like(acc)

        @pl.when(kk < 2)
        def _():
            acc[...] += _dot_nt(g_ref[...], w_ref[...])

        @pl.when(kk >= 2)
        def _():
            acc[...] += _dot_nt(u_ref[...], w_ref[...])

        @pl.when(kk == 3)
        def _():
            o_ref[...] = acc[...] + ALPHA * r_ref[...]

    return _pcall(body, name=name, grid=(D // tn, t // tm, 4),
                  in_specs=[pl.BlockSpec((tm, UPW), lambda j, i, kk: (i, jnp.minimum(kk, 1))),
                            pl.BlockSpec((tm, UPW), lambda j, i, kk: (i, jnp.maximum(kk - 2, 0))),
                            pl.BlockSpec((None, None, tn, UPW), lambda j, i, kk: (l, kk, j, 0)),
                            pl.BlockSpec((tm, tn), lambda j, i, kk: (i, j))],
                  out_specs=pl.BlockSpec((tm, tn), lambda j, i, kk: (i, j)),
                  out_shape=jax.ShapeDtypeStruct((t, D), F32),
                  scratch_shapes=[pltpu.VMEM((tm, tn), F32)],
                  compiler_params=_cp("parallel", "parallel", "arbitrary"))(dgate, dup, wup, res)


def ffn_up(xb, wup, l, *, name):
    t = xb.shape[0]
    tm, tn = min(1024, t), UPW

    def body(x_ref, wg_ref, wu_ref, g_ref, u_ref, a_ref):
        x = x_ref[...]
        for cols in _col_chunks(tn):
            g = _dot(x, wg_ref[:, cols])
            u = _dot(x, wu_ref[:, cols])
            g_ref[:, cols] = g.astype(BF16)
            u_ref[:, cols] = u.astype(BF16)
            a_ref[:, cols] = (g * _sigmoid(g) * u).astype(BF16)

    blk = pl.BlockSpec((tm, tn), lambda j, i: (i, j))
    return _pcall(body, name=name, grid=(DFF // tn, t // tm),
                  in_specs=[pl.BlockSpec((tm, D), lambda j, i: (i, 0)),
                            pl.BlockSpec((None, None, D, tn), lambda j, i: (l, j, 0, 0)),
                            pl.BlockSpec((None, None, D, tn), lambda j, i: (l, 2 + j, 0, 0))],
                  out_specs=[blk, blk, blk],
                  out_shape=[jax.ShapeDtypeStruct((t, DFF), BF16)] * 3,
                  compiler_params=_cp("parallel", "parallel"))(xb, wup, wup)


def matmul_res_ln(a, w, l, res, g, b, *, mm_scale, name):
    t, k = a.shape
    tm = min(512, t)

    def body(a_ref, w_ref, res_ref, g_ref, b_ref, r_ref, y_ref, yb_ref):
        f = _dot(a_ref[...], w_ref[...])
        r = ALPHA * res_ref[...] + mm_scale * f
        xc, rstd = _ln_stats(r)
        y = xc * rstd * g_ref[...] + b_ref[...]
        r_ref[...] = r
        y_ref[...] = y
        yb_ref[...] = y.astype(BF16)

    return _pcall(body, name=name, grid=(t // tm,),
                  in_specs=[_rows(tm, k), _layer(l, (k, D)), _rows(tm, D), _fix((1, D)), _fix((1, D))],
                  out_specs=[_rows(tm, D)] * 3,
                  out_shape=[jax.ShapeDtypeStruct((t, D), F32), jax.ShapeDtypeStruct((t, D), F32),
                             jax.ShapeDtypeStruct((t, D), BF16)],
                  compiler_params=_cp("parallel"))(a, w, res, g, b)


def ln_bwd(dy, r, g, *, out_scale, name):
    t = dy.shape[0]
    tm = min(512, t)

    def body(dy_ref, r_ref, g_ref, dr_ref, drb_ref, dg_ref, db_ref):
        @pl.when(pl.program_id(0) == 0)
        def _():
            dg_ref[...] = jnp.zeros_like(dg_ref)
            db_ref[...] = jnp.zeros_like(db_ref)

        xc, rstd = _ln_stats(r_ref[...])
        xhat = xc * rstd
        d = dy_ref[...]
        dxh = d * g_ref[...]
        dr = rstd * (dxh - jnp.mean(dxh, axis=-1, keepdims=True)
                     - xhat * jnp.mean(dxh * xhat, axis=-1, keepdims=True))
        dr_ref[...] = dr
        drb_ref[...] = (out_scale * dr).astype(BF16)
        dg_ref[...] += jnp.sum(d * xhat, axis=0, keepdims=True)
        db_ref[...] += jnp.sum(d, axis=0, keepdims=True)

    return _pcall(body, name=name, grid=(t // tm,),
                  in_specs=[_rows(tm, D), _rows(tm, D), _fix((1, D))],
                  out_specs=[_rows(tm, D), _rows(tm, D), _fix((1, D)), _fix((1, D))],
                  out_shape=[jax.ShapeDtypeStruct((t, D), F32), jax.ShapeDtypeStruct((t, D), BF16),
                             jax.ShapeDtypeStruct((1, D), F32), jax.ShapeDtypeStruct((1, D), F32)],
                  compiler_params=_cp("arbitrary"))(dy, r, g)


def ffn_down_bwd(dfb, wd, l, gate, up, *, name):
    t = dfb.shape[0]
    tm, tn = min(1024, t), UPW
    nj = DFF // tn

    def body(df_ref, w_ref, g_ref, u_ref, dg_ref, du_ref):
        df = df_ref[...]
        for cols in _col_chunks(tn):
            da = _dot_nt(df, w_ref[cols, :])
            g = g_ref[:, cols].astype(F32)
            s = _sigmoid(g)
            gs = g * s
            dg_ref[:, cols] = (da * u_ref[:, cols].astype(F32) * (s + gs * (1.0 - s))).astype(BF16)
            du_ref[:, cols] = (da * gs).astype(BF16)

    blk = pl.BlockSpec((tm, tn), lambda j, i: (i, j))
    return _pcall(body, name=name, grid=(nj, t // tm),
                  in_specs=[pl.BlockSpec((tm, D), lambda j, i: (i, 0)),
                            pl.BlockSpec((None, tn, D), lambda j, i: (l, j, 0)), blk, blk],
                  out_specs=[blk, blk],
                  out_shape=[jax.ShapeDtypeStruct((t, DFF), BF16), jax.ShapeDtypeStruct((t, DFF), BF16)],
                  compiler_params=_cp("parallel", "parallel"))(dfb, wd, gate, up)


def ple_fwd(xb, x, pb, wgate, l, wproj, bgate, g, b, *, name):
    t = x.shape[0]
    tm = min(512, t)

    def body(xb_ref, x_ref, p_ref, wg_ref, wp_ref, bg_ref, g_ref, b_ref, r_ref, y_ref, yb_ref):
        gl = _dot(xb_ref[...], wg_ref[...]) + bg_ref[...]
        pe = _dot(p_ref[...], wp_ref[...])
        r = ALPHA * x_ref[...] + _sigmoid(gl) * pe
        xc, rstd = _ln_stats(r)
        y = xc * rstd * g_ref[...] + b_ref[...]
        r_ref[...] = r
        y_ref[...] = y
        yb_ref[...] = y.astype(BF16)

    return _pcall(body, name=name, grid=(t // tm,),
                  in_specs=[_rows(tm, D), _rows(tm, D), _rows(tm, PLE), _layer(l, (D, D)), _fix((PLE, D)),
                            _fix((1, D)), _fix((1, D)), _fix((1, D))],
                  out_specs=[_rows(tm, D)] * 3,
                  out_shape=[jax.ShapeDtypeStruct((t, D), F32), jax.ShapeDtypeStruct((t, D), F32),
                             jax.ShapeDtypeStruct((t, D), BF16)],
                  compiler_params=_cp("parallel"))(xb, x, pb, wgate, wproj, bgate, g, b)


def ple_bwd(dy, r, xb, pb, wgate, l, wproj, bgate, g, *, name):
    t = dy.shape[0]
    tm = min(512, t)

    def body(dy_ref, r_ref, xb_ref, p_ref, wg_ref, wp_ref, bg_ref, g_ref,
             dr_ref, dgl_ref, dpe_ref, dg_ref, db_ref, dbg_ref):
        @pl.when(pl.program_id(0) == 0)
        def _():
            dg_ref[...] = jnp.zeros_like(dg_ref)
            db_ref[...] = jnp.zeros_like(db_ref)
            dbg_ref[...] = jnp.zeros_like(dbg_ref)

        xc, rstd = _ln_stats(r_ref[...])
        xhat = xc * rstd
        d = dy_ref[...]
        dxh = d * g_ref[...]
        dr = rstd * (dxh - jnp.mean(dxh, axis=-1, keepdims=True)
                     - xhat * jnp.mean(dxh * xhat, axis=-1, keepdims=True))
        s = _sigmoid(_dot(xb_ref[...], wg_ref[...]) + bg_ref[...])
        pe = _dot(p_ref[...], wp_ref[...])
        dgl = dr * pe * s * (1.0 - s)
        dr_ref[...] = dr
        dgl_ref[...] = dgl.astype(BF16)
        dpe_ref[...] = (dr * s).astype(BF16)
        dg_ref[...] += jnp.sum(d * xhat, axis=0, keepdims=True)
        db_ref[...] += jnp.sum(d, axis=0, keepdims=True)
        dbg_ref[...] += jnp.sum(dgl, axis=0, keepdims=True)

    vec = jax.ShapeDtypeStruct((1, D), F32)
    return _pcall(body, name=name, grid=(t // tm,),
                  in_specs=[_rows(tm, D), _rows(tm, D), _rows(tm, D), _rows(tm, PLE), _layer(l, (D, D)),
                            _fix((PLE, D)), _fix((1, D)), _fix((1, D))],
                  out_specs=[_rows(tm, D), _rows(tm, D), _rows(tm, D), _fix((1, D)), _fix((1, D)), _fix((1, D))],
                  out_shape=[jax.ShapeDtypeStruct((t, D), F32), jax.ShapeDtypeStruct((t, D), BF16),
                             jax.ShapeDtypeStruct((t, D), BF16), vec, vec, vec],
                  compiler_params=_cp("arbitrary"))(dy, r, xb, pb, wgate, wproj, bgate, g)


def loss_head(y, tgt, *, name):
    t = y.shape[0]
    tm = min(256, t)

    def body(y_ref, t_ref, dy_ref, sq_ref):
        @pl.when(pl.program_id(0) == 0)
        def _():
            sq_ref[...] = jnp.zeros_like(sq_ref)

        e = y_ref[...] - t_ref[...]
        dy_ref[...] = e / float(D)
        sq_ref[...] += jnp.sum(e * e, axis=0, keepdims=True)

    return _pcall(body, name=name, grid=(t // tm,),
                  in_specs=[_rows(tm, D), _rows(tm, D)],
                  out_specs=[_rows(tm, D), _fix((1, D))],
                  out_shape=[jax.ShapeDtypeStruct((t, D), F32), jax.ShapeDtypeStruct((1, D), F32)],
                  compiler_params=_cp("arbitrary"))(y, tgt)


def _lru_gates(xc, wa_ref, wx_ref, ba_ref, bx_ref, lam_ref):
    xcb = xc.astype(BF16)
    r = _sigmoid(_dot(xcb, wa_ref[...]) + ba_ref[...])
    ig = _sigmoid(_dot(xcb, wx_ref[...]) + bx_ref[...])
    sp = _softplus(-lam_ref[...])
    la = -LRU_C * r * sp
    a = jnp.exp(la)
    mult = jnp.sqrt(-_expm1(2.0 * la))
    return r, ig, sp, la, a, mult


def lru_fwd(z, cw, cb, wa, wx, ba, bx, lam, *, name):
    t = z.shape[0]
    tm = min(512, t)
    hb = tm // 8

    def body(ax_ref, prev_ref, ay_ref, cw_ref, cb_ref, wa_ref, wx_ref, ba_ref, bx_ref, lam_ref,
             xc_ref, xcb_ref, h_ref, ya_ref, xs, a_s, b_s, hc):
        i = pl.program_id(0)

        @pl.when(i == 0)
        def _():
            hc[...] = jnp.zeros_like(hc)

        xs[0:8, :] = jnp.where(i == 0, 0.0, prev_ref[...])
        xs[8:, :] = ax_ref[...]
        xc = cb_ref[...] + cw_ref[0:1, :] * xs[5:5 + tm, :]
        for k in range(1, 4):
            xc = xc + cw_ref[k:k + 1, :] * xs[5 + k:5 + k + tm, :]
        r, ig, sp, la, a, mult = _lru_gates(xc, wa_ref, wx_ref, ba_ref, bx_ref, lam_ref)
        a_s[...] = a
        b_s[...] = mult * (ig * xc)
        xc_ref[...] = xc
        xcb_ref[...] = xc.astype(BF16)

        def step(g, h):
            base = pl.multiple_of(g * 8, 8)
            a8 = a_s[pl.ds(base, 8), :]
            b8 = b_s[pl.ds(base, 8), :]
            for j in range(8):
                h = a8[j:j + 1, :] * h + b8[j:j + 1, :]
                h_ref[pl.ds(base + j, 1), :] = h
            return h

        hc[...] = lax.fori_loop(0, tm // 8, step, hc[...])
        ya_ref[...] = (_gelu(ay_ref[...]) * h_ref[...]).astype(BF16)

    vec = _fix((1, BW))
    return _pcall(body, name=name, grid=(t // tm,),
                  in_specs=[_rows(tm, BW, AX // BW),
                            pl.BlockSpec((8, BW), lambda i: (jnp.maximum(i * hb - 1, 0), AX // BW)),
                            _rows(tm, BW, AY // BW), _fix((4, BW)), vec, _fix((BW, BW)), _fix((BW, BW)),
                            vec, vec, vec],
                  out_specs=[_rows(tm, BW)] * 4,
                  out_shape=[jax.ShapeDtypeStruct((t, BW), F32), jax.ShapeDtypeStruct((t, BW), BF16),
                             jax.ShapeDtypeStruct((t, BW), F32), jax.ShapeDtypeStruct((t, BW), BF16)],
                  scratch_shapes=[pltpu.VMEM((tm + 8, BW), F32), pltpu.VMEM((tm, BW), F32),
                                  pltpu.VMEM((tm, BW), F32), pltpu.VMEM((1, BW), F32)],
                  compiler_params=_cp("arbitrary"))(z, z, z, cw, cb, wa, wx, ba, bx, lam)


def lru_bwd(dya, z, h, xc, wa, wx, ba, bx, lam, *, name):
    t = dya.shape[0]
    tm = min(512, t)
    nb = t // tm
    hb = tm // 8

    def body(dya_ref, ay_ref, h_ref, hprev_ref, xc_ref, wa_ref, wx_ref, ba_ref, bx_ref,
             lam_ref, day_ref, dxc_ref, dpr_ref, dpi_ref, dba_ref, dbx_ref, dlam_ref,
             hs, a_s, g_s, d_s, cc):
        i = pl.program_id(0)

        @pl.when(i == 0)
        def _():
            cc[...] = jnp.zeros_like(cc)
            dba_ref[...] = jnp.zeros_like(dba_ref)
            dbx_ref[...] = jnp.zeros_like(dbx_ref)
            dlam_ref[...] = jnp.zeros_like(dlam_ref)

        xc = xc_ref[...]
        r, ig, sp, la, a, mult = _lru_gates(xc, wa_ref, wx_ref, ba_ref, bx_ref, lam_ref)
        ay = ay_ref[...]
        dya = dya_ref[...]
        hcur = h_ref[...]
        day_ref[...] = (dya * hcur * _gelu_grad(ay)).astype(BF16)
        a_s[...] = a
        g_s[...] = dya * _gelu(ay)

        def step(gg, cin):
            g = tm // 8 - 1 - gg
            base = pl.multiple_of(g * 8, 8)
            a8 = a_s[pl.ds(base, 8), :]
            g8 = g_s[pl.ds(base, 8), :]
            for j in range(7, -1, -1):
                d = g8[j:j + 1, :] + cin
                d_s[pl.ds(base + j, 1), :] = d
                cin = a8[j:j + 1, :] * d
            return cin

        cc[...] = lax.fori_loop(0, tm // 8, step, cc[...])
        dht = d_s[...]
        hs[0:8, :] = jnp.where(i == nb - 1, 0.0, hprev_ref[...])
        hs[8:, :] = hcur
        da = dht * hs[7:7 + tm, :]
        dmult = dht * ig * xc
        dig = dht * mult * xc
        dla = da * a - dmult * a * a / mult
        dpr = dla * (-LRU_C * sp) * r * (1.0 - r)
        dpi = dig * ig * (1.0 - ig)
        dprb = dpr.astype(BF16)
        dpib = dpi.astype(BF16)
        dxc_ref[...] = dht * mult * ig + _dot_nt(dprb, wa_ref[...]) + _dot_nt(dpib, wx_ref[...])
        dpr_ref[...] = dprb
        dpi_ref[...] = dpib
        dba_ref[...] += jnp.sum(dpr, axis=0, keepdims=True)
        dbx_ref[...] += jnp.sum(dpi, axis=0, keepdims=True)
        dlam_ref[...] += jnp.sum(dla * (-LRU_C * r), axis=0, keepdims=True) * (-_sigmoid(-lam_ref[...]))

    vec = _fix((1, BW))
    mat = _fix((BW, BW))
    rev = lambda col: pl.BlockSpec((tm, BW), lambda i: (nb - 1 - i, col))
    vshape = jax.ShapeDtypeStruct((1, BW), F32)
    return _pcall(body, name=name, grid=(nb,),
                  in_specs=[rev(0), rev(AY // BW), rev(0),
                            pl.BlockSpec((8, BW), lambda i: (jnp.maximum((nb - 1 - i) * hb - 1, 0), 0)),
                            rev(0), mat, mat, vec, vec, vec],
                  out_specs=[rev(0), rev(0), rev(0), rev(0), vec, vec, vec],
                  out_shape=[jax.ShapeDtypeStruct((t, BW), BF16), jax.ShapeDtypeStruct((t, BW), F32),
                             jax.ShapeDtypeStruct((t, BW), BF16), jax.ShapeDtypeStruct((t, BW), BF16),
                             vshape, vshape, vshape],
                  scratch_shapes=[pltpu.VMEM((tm + 8, BW), F32), pltpu.VMEM((tm, BW), F32),
                                  pltpu.VMEM((tm, BW), F32), pltpu.VMEM((tm, BW), F32),
                                  pltpu.VMEM((1, BW), F32)],
                  compiler_params=_cp("arbitrary"))(dya, z, h, h, xc, wa, wx, ba, bx, lam)


def conv_bwd(dxc, z, cw, *, name):
    t = dxc.shape[0]
    tm = min(512, t)
    nb = t // tm
    hb = tm // 8

    def body(d_ref, dnext_ref, ax_ref, prev_ref, cw_ref, dax_ref, dcw_ref, dcb_ref, ds, xs):
        i = pl.program_id(0)

        @pl.when(i == 0)
        def _():
            dcw_ref[...] = jnp.zeros_like(dcw_ref)
            dcb_ref[...] = jnp.zeros_like(dcb_ref)

        d = d_ref[...]
        ds[0:tm, :] = d
        ds[tm:, :] = jnp.where(i == nb - 1, 0.0, dnext_ref[...])
        xs[0:8, :] = jnp.where(i == 0, 0.0, prev_ref[...])
        xs[8:, :] = ax_ref[...]
        dax = cw_ref[3:4, :] * d
        for k in range(3):
            dax = dax + cw_ref[k:k + 1, :] * ds[3 - k:3 - k + tm, :]
        dax_ref[...] = dax.astype(BF16)
        for k in range(4):
            dcw_ref[k:k + 1, :] += jnp.sum(d * xs[5 + k:5 + k + tm, :], axis=0, keepdims=True)
        dcb_ref[...] += jnp.sum(d, axis=0, keepdims=True)

    return _pcall(body, name=name, grid=(nb,),
                  in_specs=[_rows(tm, BW),
                            pl.BlockSpec((8, BW), lambda i: (jnp.minimum((i + 1) * hb, nb * hb - 1), 0)),
                            _rows(tm, BW, AX // BW),
                            pl.BlockSpec((8, BW), lambda i: (jnp.maximum(i * hb - 1, 0), AX // BW)),
                            _fix((4, BW))],
                  out_specs=[_rows(tm, BW), _fix((4, BW)), _fix((1, BW))],
                  out_shape=[jax.ShapeDtypeStruct((t, BW), BF16), jax.ShapeDtypeStruct((4, BW), F32),
                             jax.ShapeDtypeStruct((1, BW), F32)],
                  scratch_shapes=[pltpu.VMEM((tm + 8, BW), F32), pltpu.VMEM((tm + 8, BW), F32)],
                  compiler_params=_cp("arbitrary"))(dxc, dxc, z, z, cw)


GLA_CB = 4


def _gla_consts():
    tri = (jnp.arange(CHUNK)[:, None] >= jnp.arange(CHUNK)[None, :]).astype(F32)
    mask = ((jnp.arange(BW)[:, None] // 128) == (jnp.arange(256)[None, :] // 64)).astype(F32)
    return tri, mask


def gla_fwd(z, zb, wg2p, bg, ng, *, name):
    t = z.shape[0]
    tm = GLA_CB * CHUNK
    nc = t // CHUNK
    tri, mask = _gla_consts()

    def body(q_ref, k_ref, v_ref, misc_ref, br_ref, w_ref, bg_ref, ng_ref, tri_ref, mask_ref,
             yb_ref, st_ref, st):
        @pl.when(pl.program_id(0) == 0)
        def _():
            st[...] = jnp.zeros_like(st)

        for c in range(GLA_CB):
            rows = slice(c * CHUNK, (c + 1) * CHUNK)
            pre = _dot(misc_ref[rows, :], w_ref[...]) + bg_ref[...]
            la = _log_sigmoid(pre) / GLA_TAU
            gc = _dot_hi(tri_ref[...], la)
            gt = gc[CHUNK - 1:CHUNK, :]
            kdec = k_ref[rows, :] * jnp.exp(gt - gc)
            delta = _dot_tn(v_ref[rows, :], kdec.astype(BF16))
            s_new = st[...] * jnp.exp(gt) + delta * mask_ref[...]
            st[...] = s_new
            st_ref[c] = s_new
            o = _dot_nt(q_ref[rows, :], s_new.astype(BF16)) * (64.0 ** -0.5)
            br = br_ref[rows, :]
            for hd in range(4):
                cols = slice(hd * 128, (hd + 1) * 128)
                oh = o[:, cols]
                rs = lax.rsqrt(jnp.mean(oh * oh, axis=-1, keepdims=True) + RMS_EPS)
                brh = br[:, cols]
                yb_ref[rows, cols] = (oh * rs * ng_ref[:, cols] * (brh * _sigmoid(brh))).astype(BF16)

    return _pcall(body, name=name, grid=(t // tm,),
                  in_specs=[_rows(tm, 256, BQ // 256), _rows(tm, 256, BK // 256), _rows(tm, BW, BV // BW),
                            _rows(tm, 128, MISC // 128), _rows(tm, BW, BR // BW), _fix((128, 256)),
                            _fix((1, 256)), _fix((1, BW)), _fix((CHUNK, CHUNK)), _fix((BW, 256))],
                  out_specs=[_rows(tm, BW), pl.BlockSpec((GLA_CB, BW, 256), lambda i: (i, 0, 0))],
                  out_shape=[jax.ShapeDtypeStruct((t, BW), BF16), jax.ShapeDtypeStruct((nc, BW, 256), F32)],
                  scratch_shapes=[pltpu.VMEM((BW, 256), F32)],
                  compiler_params=_cp("arbitrary"))(zb, z, zb, zb, z, wg2p, bg, ng, tri, mask)


def gla_bwd(dyb, z, zb, states, wg2p, bg, ng, *, name):
    t = z.shape[0]
    tm = GLA_CB * CHUNK
    nb = t // tm
    tri, mask = _gla_consts()
    triu = tri.T

    def body(dy_ref, q_ref, k_ref, v_ref, misc_ref, br_ref, st_ref, sp_ref, w_ref, bg_ref, ng_ref,
             tri_ref, triu_ref, mask_ref,
             dq_ref, dk_ref, dv_ref, dbr_ref, dmisc_ref, dpre_ref, dbg_ref, dng_ref, cc):
        i = pl.program_id(0)

        @pl.when(i == 0)
        def _():
            cc[...] = jnp.zeros_like(cc)
            dbg_ref[...] = jnp.zeros_like(dbg_ref)
            dng_ref[...] = jnp.zeros_like(dng_ref)

        last_row = lax.broadcasted_iota(jnp.int32, (CHUNK, 256), 0) == CHUNK - 1
        for c in range(GLA_CB - 1, -1, -1):
            rows = slice(c * CHUNK, (c + 1) * CHUNK)
            pre = _dot(misc_ref[rows, :], w_ref[...]) + bg_ref[...]
            la = _log_sigmoid(pre) / GLA_TAU
            gc = _dot_hi(tri_ref[...], la)
            gt = gc[CHUNK - 1:CHUNK, :]
            eg = jnp.exp(gt - gc)
            kdec = k_ref[rows, :] * eg
            e = jnp.exp(gt)
            s_n = st_ref[c]
            if c > 0:
                s_prev = st_ref[c - 1]
            else:
                s_prev = jnp.where(i == nb - 1, 0.0, sp_ref[0])
            sb = s_n.astype(BF16)
            qb = q_ref[rows, :]
            o = _dot_nt(qb, sb) * (64.0 ** -0.5)
            br = br_ref[rows, :]
            dy = dy_ref[rows, :]
            do_parts = []
            for hd in range(4):
                cols = slice(hd * 128, (hd + 1) * 128)
                oh = o[:, cols]
                rs = lax.rsqrt(jnp.mean(oh * oh, axis=-1, keepdims=True) + RMS_EPS)
                ohat = oh * rs
                brh = br[:, cols]
                sg = _sigmoid(brh)
                dyh = dy[:, cols]
                ngh = ng_ref[:, cols]
                don = dyh * (brh * sg)
                dbr_ref[rows, cols] = (dyh * (ohat * ngh) * sg * (1.0 + brh * (1.0 - sg))).astype(BF16)
                dng_ref[:, cols] += jnp.sum(don * ohat, axis=0, keepdims=True)
                doh = don * ngh
                do_parts.append(rs * (doh - ohat * jnp.mean(doh * ohat, axis=-1, keepdims=True)))
            dob = jnp.concatenate(do_parts, axis=1).astype(BF16)
            dq_ref[rows, :] = (_dot(dob, sb) * (64.0 ** -0.5)).astype(BF16)
            dst = cc[...] + _dot_tn(dob, qb) * (64.0 ** -0.5) * mask_ref[...]
            dsb = dst.astype(BF16)
            dkdec = _dot(v_ref[rows, :], dsb)
            dv_ref[rows, :] = _dot_nt(kdec.astype(BF16), dsb).astype(BF16)
            dgt = jnp.sum(dst * s_prev, axis=0, keepdims=True) * e
            dk_ref[rows, :] = (dkdec * eg).astype(BF16)
            dd = dkdec * kdec
            dgt = dgt + jnp.sum(dd, axis=0, keepdims=True)
            dgc = jnp.where(last_row, dgt - dd, -dd)
            dla = _dot_hi(triu_ref[...], dgc)
            dpre = dla * (1.0 / GLA_TAU) * _sigmoid(-pre)
            dpb = dpre.astype(BF16)
            dpre_ref[rows, :] = dpb
            dmisc_ref[rows, :] = _dot_nt(dpb, w_ref[...])
            dbg_ref[...] += jnp.sum(dpre, axis=0, keepdims=True)
            cc[...] = dst * e

    rev = lambda w, col: pl.BlockSpec((tm, w), lambda i: (nb - 1 - i, col))
    return _pcall(body, name=name, grid=(nb,),
                  in_specs=[rev(BW, 0), rev(256, BQ // 256), rev(256, BK // 256), rev(BW, BV // BW),
                            rev(128, MISC // 128), rev(BW, BR // BW),
                            pl.BlockSpec((GLA_CB, BW, 256), lambda i: (nb - 1 - i, 0, 0)),
                            pl.BlockSpec((1, BW, 256), lambda i: (jnp.maximum((nb - 1 - i) * GLA_CB - 1, 0), 0, 0)),
                            _fix((128, 256)), _fix((1, 256)), _fix((1, BW)),
                            _fix((CHUNK, CHUNK)), _fix((CHUNK, CHUNK)), _fix((BW, 256))],
                  out_specs=[rev(256, 0), rev(256, 0), rev(BW, 0), rev(BW, 0), rev(128, 0), rev(256, 0),
                             _fix((1, 256)), _fix((1, BW))],
                  out_shape=[jax.ShapeDtypeStruct((t, 256), BF16), jax.ShapeDtypeStruct((t, 256), BF16),
                             jax.ShapeDtypeStruct((t, BW), BF16), jax.ShapeDtypeStruct((t, BW), BF16),
                             jax.ShapeDtypeStruct((t, 128), F32), jax.ShapeDtypeStruct((t, 256), BF16),
                             jax.ShapeDtypeStruct((1, 256), F32), jax.ShapeDtypeStruct((1, BW), F32)],
                  scratch_shapes=[pltpu.VMEM((BW, 256), F32)],
                  compiler_params=_cp("arbitrary"))(dyb, zb, z, zb, zb, z, states, states, wg2p, bg, ng,
                                                    tri, triu, mask)


FOX_SCALE = 64.0 ** -0.5
NEG = -1e30


def fox_fcum(z, bfp, *, name):
    t = z.shape[0]
    tm = min(256, t)
    tri = (jnp.arange(tm)[:, None] >= jnp.arange(tm)[None, :]).astype(F32)

    def body(m_ref, b_ref, tri_ref, o_ref, cc):
        @pl.when(pl.program_id(0) == 0)
        def _():
            cc[...] = jnp.zeros_like(cc)

        lf = _log_sigmoid(m_ref[...] + b_ref[...])
        cs = _dot_hi(tri_ref[...], lf) + cc[...]
        o_ref[...] = cs
        cc[...] = cs[tm - 1:tm, :]

    return _pcall(body, name=name, grid=(t // tm,),
                  in_specs=[_rows(tm, 128, MISC // 128), _fix((1, 128)), _fix((tm, tm))],
                  out_specs=_rows(tm, 128), out_shape=jax.ShapeDtypeStruct((t, 128), F32),
                  scratch_shapes=[pltpu.VMEM((1, 128), F32)],
                  compiler_params=_cp("arbitrary"))(z, bfp, tri)


def fox_dcf(dfc, z, bfp, dmisc_g, *, name):
    t = z.shape[0]
    tm = min(256, t)
    nb = t // tm
    triu = (jnp.arange(tm)[:, None] <= jnp.arange(tm)[None, :]).astype(F32)

    def body(d_ref, m_ref, b_ref, g_ref, tri_ref, o_ref, dbf_ref, cc):
        @pl.when(pl.program_id(0) == 0)
        def _():
            cc[...] = jnp.zeros_like(cc)
            dbf_ref[...] = jnp.zeros_like(dbf_ref)

        rc = _dot_hi(tri_ref[...], d_ref[...]) + cc[...]
        cc[...] = rc[0:1, :]
        dcf = rc * _sigmoid(-(m_ref[...] + b_ref[...]))
        o_ref[...] = (dcf + g_ref[...]).astype(BF16)
        dbf_ref[...] += jnp.sum(dcf, axis=0, keepdims=True)

    rev = lambda col: pl.BlockSpec((tm, 128), lambda i: (nb - 1 - i, col))
    return _pcall(body, name=name, grid=(nb,),
                  in_specs=[rev(0), rev(MISC // 128), _fix((1, 128)), rev(0), _fix((tm, tm))],
                  out_specs=[rev(0), _fix((1, 128))],
                  out_shape=[jax.ShapeDtypeStruct((t, 128), BF16), jax.ShapeDtypeStruct((1, 128), F32)],
                  scratch_shapes=[pltpu.VMEM((1, 128), F32)],
                  compiler_params=_cp("arbitrary"))(dfc, z, bfp, dmisc_g, triu)


def fox_delta(dyc, ycf, *, name):
    t = dyc.shape[0]
    tm = min(256, t)
    seg = ((jnp.arange(BW)[:, None] // 64) == jnp.arange(128)[None, :]).astype(F32)

    def body(d_ref, o_ref, s_ref, out_ref):
        out_ref[...] = _dot_hi(d_ref[...] * o_ref[...], s_ref[...])

    return _pcall(body, name=name, grid=(t // tm,),
                  in_specs=[_rows(tm, BW), _rows(tm, BW), _fix((BW, 128))],
                  out_specs=_rows(tm, 128), out_shape=jax.ShapeDtypeStruct((t, 128), F32),
                  compiler_params=_cp("parallel"))(dyc, ycf, seg)


def fox_fwd_t(zb, frow, fkb, *, name):
    t = zb.shape[0]
    tq = min(512, t)
    nq = t // tq
    rep = tq // 128

    pairs = [(i, j) for i in range(nq) for j in range(i + 1)]
    qi_tab = jnp.asarray([p[0] for p in pairs], jnp.int32)
    kj_tab = jnp.asarray([p[1] for p in pairs], jnp.int32)

    def body(qi_ref, kj_ref, q_ref, k_ref, v_ref, fq_ref, fk_ref, y_ref, yf_ref, lse_ref, m_s, l_s, acc):
        step = pl.program_id(1)
        i, j = qi_ref[step], kj_ref[step]

        @pl.when(j == 0)
        def _():
            m_s[...] = jnp.full_like(m_s, NEG)
            l_s[...] = jnp.zeros_like(l_s)
            acc[...] = jnp.zeros_like(acc)

        lo = lax.broadcasted_iota(jnp.int32, (tq, 128), 1) < 64

        def work(diagonal):
            q = q_ref[...]
            k = k_ref[...]
            v = v_ref[...]
            if diagonal:
                key = lax.broadcasted_iota(jnp.int32, (tq, tq), 0)
                qry = lax.broadcasted_iota(jnp.int32, (tq, tq), 1)
                keep = key <= qry
            for hh in range(2):
                sel = lo if hh == 0 else jnp.logical_not(lo)
                qh = jnp.where(sel, q, jnp.zeros_like(q))
                s = _dot_nt(k, qh) + fq_ref[hh] - jnp.tile(fk_ref[hh], (1, rep))
                if diagonal:
                    s = jnp.where(keep, s, NEG)
                m_old = m_s[hh]
                m_new = jnp.maximum(m_old, jnp.max(s, axis=0, keepdims=True))
                p = jnp.exp(s - m_new)
                corr = jnp.exp(m_old - m_new)
                l_s[hh] = l_s[hh] * corr + jnp.sum(p, axis=0, keepdims=True)
                m_s[hh] = m_new
                pv = _dot_tn(v, p.astype(BF16))
                rows = slice(64 * hh, 64 * hh + 64)
                acc[rows, :] = acc[rows, :] * corr + pv[rows, :]

        @pl.when(j < i)
        def _():
            work(False)

        @pl.when(j == i)
        def _():
            work(True)
            first = lax.broadcasted_iota(jnp.int32, (128, tq), 0) < 64
            out = (acc[...] * jnp.where(first, 1.0 / l_s[0], 1.0 / l_s[1])).T
            y_ref[...] = out.astype(BF16)
            yf_ref[...] = out
            lse_ref[...] = m_s[...] + jnp.log(l_s[...])

    kv = lambda off: pl.BlockSpec((tq, 128), lambda h, s, qi, kj: (kj[s], off // 128 + h))
    gs = pltpu.PrefetchScalarGridSpec(
        num_scalar_prefetch=2, grid=(4, len(pairs)),
        in_specs=[pl.BlockSpec((tq, 128), lambda h, s, qi, kj: (qi[s], CQ // 128 + h)), kv(CK), kv(CV),
                  pl.BlockSpec((2, 1, tq), lambda h, s, qi, kj: (h, 0, qi[s])),
                  pl.BlockSpec((2, tq, 128), lambda h, s, qi, kj: (h, kj[s], 0))],
        out_specs=[pl.BlockSpec((tq, 128), lambda h, s, qi, kj: (qi[s], h)),
                   pl.BlockSpec((tq, 128), lambda h, s, qi, kj: (qi[s], h)),
                   pl.BlockSpec((2, 1, tq), lambda h, s, qi, kj: (h, 0, qi[s]))],
        scratch_shapes=[pltpu.VMEM((2, 1, tq), F32), pltpu.VMEM((2, 1, tq), F32), pltpu.VMEM((128, tq), F32)])
    return _pcall(body, name=name, grid_spec=gs,
                  out_shape=[jax.ShapeDtypeStruct((t, BW), BF16), jax.ShapeDtypeStruct((t, BW), F32),
                             jax.ShapeDtypeStruct((FOX_H, 1, t), F32)],
                  compiler_params=_cp("parallel", "arbitrary"))(qi_tab, kj_tab, zb, zb, zb, frow, fkb)


def fox_bwd_t(zb, dyc, frow, fkb, lse, dl, *, name):
    t = zb.shape[0]
    tq = min(512, t)
    nq = t // tq
    rep = tq // 128

    pairs = [(j, i) for j in range(nq) for i in range(j, nq)]
    kj_tab = jnp.asarray([p[0] for p in pairs], jnp.int32)
    qi_tab = jnp.asarray([p[1] for p in pairs], jnp.int32)

    def body(kj_ref, qi_ref, q_ref, k_ref, v_ref, do_ref, fq_ref, fk_ref, lse_ref, dl_ref,
             dq_ref, dk_ref, dv_ref, dfk_ref, dfq_ref, dk_s, dv_s, df_s, dq_s):
        step = pl.program_id(1)
        j, i = kj_ref[step], qi_ref[step]

        @pl.when(step == 0)
        def _():
            dq_s[...] = jnp.zeros_like(dq_s)
            dfq_ref[...] = jnp.zeros_like(dfq_ref)

        @pl.when(i == j)
        def _():
            dk_s[...] = jnp.zeros_like(dk_s)
            dv_s[...] = jnp.zeros_like(dv_s)
            df_s[...] = jnp.zeros_like(df_s)

        lo = lax.broadcasted_iota(jnp.int32, (tq, 128), 1) < 64

        def work(diagonal):
            q = q_ref[...]
            k = k_ref[...]
            v = v_ref[...]
            dob = do_ref[...].astype(BF16)
            if diagonal:
                key = lax.broadcasted_iota(jnp.int32, (tq, tq), 0)
                qry = lax.broadcasted_iota(jnp.int32, (tq, tq), 1)
                keep = key <= qry
            dvs, dks = [], []
            for hh in range(2):
                sel = lo if hh == 0 else jnp.logical_not(lo)
                qh = jnp.where(sel, q, jnp.zeros_like(q))
                doh = jnp.where(sel, dob, jnp.zeros_like(dob))
                p = jnp.exp(_dot_nt(k, qh) + (fq_ref[hh] - lse_ref[hh]) - jnp.tile(fk_ref[hh], (1, rep)))
                if diagonal:
                    p = jnp.where(keep, p, 0.0)
                ds = p * (_dot_nt(v, doh) - dl_ref[hh])
                dsb = ds.astype(BF16)
                dvs.append(_dot(p.astype(BF16), dob))
                dks.append(_dot(dsb, q))
                rows = slice(64 * hh, 64 * hh + 64)
                dq_s[i, rows, :] += _dot_tn(k, dsb)[rows, :]
                part = ds[:, 0:128]
                for r in range(1, rep):
                    part = part + ds[:, 128 * r:128 * (r + 1)]
                df_s[hh] += part
                dfq_ref[hh, i] += jnp.sum(ds, axis=0, keepdims=True)
            dv_s[...] += jnp.where(lo, dvs[0], dvs[1])
            dk_s[...] += jnp.where(lo, dks[0], dks[1])

        @pl.when(i > j)
        def _():
            work(False)

        @pl.when(i == j)
        def _():
            work(True)
            dq_ref[...] = dq_s[i].T.astype(BF16)

        @pl.when(i == nq - 1)
        def _():
            dk_ref[...] = dk_s[...].astype(BF16)
            dv_ref[...] = dv_s[...].astype(BF16)
            for hh in range(2):
                dfk_ref[hh] = -jnp.sum(df_s[hh].T, axis=0, keepdims=True)

    row = lambda: pl.BlockSpec((2, 1, tq), lambda h, s, kj, qi: (h, 0, qi[s]))
    gs = pltpu.PrefetchScalarGridSpec(
        num_scalar_prefetch=2, grid=(4, len(pairs)),
        in_specs=[pl.BlockSpec((tq, 128), lambda h, s, kj, qi: (qi[s], CQ // 128 + h)),
                  pl.BlockSpec((tq, 128), lambda h, s, kj, qi: (kj[s], CK // 128 + h)),
                  pl.BlockSpec((tq, 128), lambda h, s, kj, qi: (kj[s], CV // 128 + h)),
                  pl.BlockSpec((tq, 128), lambda h, s, kj, qi: (qi[s], h)),
                  row(), pl.BlockSpec((2, tq, 128), lambda h, s, kj, qi: (h, kj[s], 0)), row(), row()],
        out_specs=[pl.BlockSpec((tq, 128), lambda h, s, kj, qi: (kj[s], h)),
                   pl.BlockSpec((tq, 128), lambda h, s, kj, qi: (kj[s], h)),
                   pl.BlockSpec((tq, 128), lambda h, s, kj, qi: (kj[s], h)),
                   pl.BlockSpec((2, 1, tq), lambda h, s, kj, qi: (h, 0, kj[s])),
                   pl.BlockSpec((2, nq, 1, tq), lambda h, s, kj, qi: (h, 0, 0, 0))],
        scratch_shapes=[pltpu.VMEM((tq, 128), F32), pltpu.VMEM((tq, 128), F32), pltpu.VMEM((2, tq, 128), F32),
                        pltpu.VMEM((nq, 128, tq), F32)])
    return _pcall(body, name=name, grid_spec=gs,
                  out_shape=[jax.ShapeDtypeStruct((t, BW), BF16), jax.ShapeDtypeStruct((t, BW), BF16),
                             jax.ShapeDtypeStruct((t, BW), BF16), jax.ShapeDtypeStruct((FOX_H, 1, t), F32),
                             jax.ShapeDtypeStruct((FOX_H, nq, 1, tq), F32)],
                  compiler_params=_cp("parallel", "arbitrary"))(kj_tab, qi_tab, zb, zb, zb, dyc, frow, fkb, lse, dl)


def merge_fwd(ya, yb, yc, wbr, z, *, name):
    t = ya.shape[0]
    tm = min(512, t)

    def body(ya_ref, yb_ref, yc_ref, w_ref, g0_ref, g1_ref, g2_ref, o_ref):
        m = _sigmoid(g0_ref[...]) * _dot(ya_ref[...], w_ref[0])
        m = m + _sigmoid(g1_ref[...]) * _dot(yb_ref[...], w_ref[1])
        m = m + _sigmoid(g2_ref[...]) * _dot(yc_ref[...], w_ref[2])
        o_ref[...] = m.astype(BF16)

    return _pcall(body, name=name, grid=(t // tm,),
                  in_specs=[_rows(tm, BW)] * 3 + [_fix((3, BW, D))]
                  + [_rows(tm, D, G0 // D + j) for j in range(3)],
                  out_specs=_rows(tm, D), out_shape=jax.ShapeDtypeStruct((t, D), BF16),
                  compiler_params=_cp("parallel"))(ya, yb, yc, wbr, z, z, z)


def merge_bwd(doutb, wo, l, ya, yb, yc, wbr, z, *, name):
    t = ya.shape[0]
    tm = min(256, t)

    def body(do_ref, wo_ref, ya_ref, yb_ref, yc_ref, w_ref, g0_ref, g1_ref, g2_ref,
             dya_ref, dyb_ref, dyc_ref, dp0_ref, dp1_ref, dp2_ref, dg0_ref, dg1_ref, dg2_ref):
        dm = _dot_nt(do_ref[...], wo_ref[...])
        ys = (ya_ref, yb_ref, yc_ref)
        gs = (g0_ref, g1_ref, g2_ref)
        dys = (dya_ref, dyb_ref, dyc_ref)
        dps = (dp0_ref, dp1_ref, dp2_ref)
        dgs = (dg0_ref, dg1_ref, dg2_ref)
        for j in range(3):
            s = _sigmoid(gs[j][...])
            pj = _dot(ys[j][...], w_ref[j])
            dpb = (dm * s).astype(BF16)
            dps[j][...] = dpb
            dgs[j][...] = (dm * pj * s * (1.0 - s)).astype(BF16)
            dys[j][...] = _dot_nt(dpb, w_ref[j])

    yshape = jax.ShapeDtypeStruct((t, BW), F32)
    dshape = jax.ShapeDtypeStruct((t, D), BF16)
    return _pcall(body, name=name, grid=(t // tm,),
                  in_specs=[_rows(tm, D), _layer(l, (D, D))] + [_rows(tm, BW)] * 3
                  + [_fix((3, BW, D))] + [_rows(tm, D, G0 // D + j) for j in range(3)],
                  out_specs=[_rows(tm, BW)] * 3 + [_rows(tm, D)] * 6,
                  out_shape=[yshape] * 3 + [dshape] * 6,
                  compiler_params=_cp("parallel"))(doutb, wo, ya, yb, yc, wbr, z, z, z)


def adamw(w, g, m, v, *, name):
    nl, r, c = w.shape
    tm = _row_tile(r)

    def body(w_ref, g_ref, m_ref, v_ref, d_ref, mo_ref, vo_ref):
        gg = g_ref[...]
        mn = ADAM_B1 * m_ref[...] + (1.0 - ADAM_B1) * gg
        vn = ADAM_B2 * v_ref[...] + (1.0 - ADAM_B2) * (gg * gg)
        m_hat = mn / (1.0 - ADAM_B1 ** ADAM_STEP)
        v_hat = vn / (1.0 - ADAM_B2 ** ADAM_STEP)
        d_ref[...] = -ADAM_LR * (m_hat / (jnp.sqrt(v_hat) + ADAM_EPS) + ADAM_WD * w_ref[...])
        mo_ref[...] = mn
        vo_ref[...] = vn

    shp = jax.ShapeDtypeStruct((nl, r, c), F32)
    blk = pl.BlockSpec((None, tm, c), lambda l, i: (l, i, 0))
    return _pcall(body, name=name, grid=(nl, r // tm), in_specs=[blk] * 4, out_specs=[blk] * 3,
                  out_shape=[shp] * 3, compiler_params=_cp("parallel", "parallel"))(w, g, m, v)


def _place():
    return lax.axis_index("x"), lax.axis_index("y"), lax.axis_index("c")


def _remote(src, dst, send_sems, recv_sems, k, to):
    return pltpu.make_async_remote_copy(src_ref=src, dst_ref=dst, send_sem=send_sems.at[k],
                                        recv_sem=recv_sems.at[k], device_id=to, device_id_type=MESH)


HBM = pl.BlockSpec(memory_space=pltpu.HBM)
SEM = pl.BlockSpec(memory_space=pltpu.SEMAPHORE)
EFFECT = pltpu.SideEffectType.DATAFLOW_SIDE_EFFECTING


def gather_first(shards):
    n = len(shards)

    def body(*refs):
        ins, outs, lands = refs[:n], refs[n:2 * n], refs[2 * n:3 * n]
        send_sems, recv_sems, own_send, own_recv = refs[3 * n:]
        x, y, c = _place()
        sib = (x, y, 1 - c)
        chips = [(1 - x, y), (x, 1 - y), (1 - x, 1 - y)]
        k_me = 2 * x + y
        mine = []
        for t in range(n):
            mine.append(_remote(ins[t].at[0], outs[t].at[0, k_me], own_send, own_recv, 2 * t, sib))
            mine.append(_remote(ins[t].at[1], lands[t].at[0, k_me], own_send, own_recv, 2 * t + 1, sib))
        for cp in mine:
            cp.start()

        def slot(t, chip):
            return outs[t].at[0, 2 * chip[0] + chip[1]]

        @pl.when(c == 0)
        def _():
            first = [_remote(ins[t].at[0], outs[t].at[0, k_me], send_sems, recv_sems, 6 * t + j, (*chip, 0))
                     for t in range(n) for j, chip in enumerate(chips)]
            for cp in first:
                cp.start()
            passed = []
            for t in range(n):
                for j, chip in enumerate(chips):
                    _remote(slot(t, chip), slot(t, chip), send_sems, recv_sems, 6 * t + j, (*chip, 0)).wait_recv()
                    cp = _remote(slot(t, chip), slot(t, chip), send_sems, recv_sems, 6 * t + 3 + j, sib)
                    cp.start()
                    passed.append(cp)
            for cp in first + passed:
                cp.wait_send()

        @pl.when(c == 1)
        def _():
            for t in range(n):
                for j, chip in enumerate(chips):
                    _remote(slot(t, chip), slot(t, chip), send_sems, recv_sems, 6 * t + 3 + j, sib).wait_recv()

        for cp in mine:
            cp.wait()

    shape = [jax.ShapeDtypeStruct((1, 4) + s.shape[1:], s.dtype) for s in shards]
    out = _pcall(body, name="gather_first", in_specs=[ANY] * n, out_specs=[ANY] * (2 * n), out_shape=shape + shape,
                 scratch_shapes=[pltpu.SemaphoreType.DMA((6 * n,)), pltpu.SemaphoreType.DMA((6 * n,)),
                                 pltpu.SemaphoreType.DMA((2 * n,)), pltpu.SemaphoreType.DMA((2 * n,))])(*shards)
    return out[:n], out[n:]


def _rest_copies(ins, lands, send_sems, recv_sems):
    x, y, c = _place()
    chips = [(1 - x, y), (x, 1 - y), (1 - x, 1 - y)]
    copies, arrivals = [], []
    for t in range(len(ins)):
        for j, chip in enumerate(chips):
            for to in range(2):
                copies.append(pltpu.make_async_remote_copy(
                    src_ref=ins[t].at[1], dst_ref=lands[t].at[0, 2 * x + y], send_sem=send_sems.at[6 * t + 2 * j + to],
                    recv_sem=recv_sems.at[3 * t + j], device_id=(*chip, to), device_id_type=MESH))
            blk = lands[t].at[0, 2 * chip[0] + chip[1]]
            arrivals.append(pltpu.make_async_remote_copy(
                src_ref=blk, dst_ref=blk, send_sem=send_sems.at[6 * t + 2 * j], recv_sem=recv_sems.at[3 * t + j],
                device_id=(*chip, 1), device_id_type=MESH))
    return copies, arrivals


def gather_rest_start(shards, lands):
    n = len(shards)

    def body(*refs):
        ins, lds = refs[:n], refs[n:2 * n]
        send_sems, recv_sems = refs[2 * n], refs[2 * n + 1]
        token = refs[-1]
        copies, _ = _rest_copies(ins, lds, send_sems, recv_sems)

        @pl.when(lax.axis_index("c") == 1)
        def _():
            for cp in copies:
                cp.start()

        token[...] = jnp.zeros_like(token)

    hbm = lambda a: pltpu.with_memory_space_constraint(a, pltpu.HBM)
    out = _pcall(body, name="gather_rest_start", in_specs=[HBM] * (2 * n),
                 out_specs=[SEM, SEM] + [HBM] * (2 * n) + [pl.BlockSpec(memory_space=pltpu.VMEM)],
                 out_shape=[pltpu.SemaphoreType.DMA((6 * n,)), pltpu.SemaphoreType.DMA((3 * n,))]
                 + [pltpu.HBM(a.shape, a.dtype) for a in shards] + [pltpu.HBM(a.shape, a.dtype) for a in lands]
                 + [jax.ShapeDtypeStruct((8, 128), F32)],
                 input_output_aliases={i: 2 + i for i in range(2 * n)},
                 compiler_params=pltpu.CompilerParams(has_side_effects=EFFECT))(
                     *[hbm(a) for a in shards], *[hbm(a) for a in lands])
    return out[0], out[1], out[2:2 + n], out[2 + n:2 + 2 * n], out[-1]


def gather_rest_wait(send_sems, recv_sems, srcs, lands, after):
    n = len(srcs)

    def body(*refs):
        ins, lds = refs[:n], refs[n:2 * n]
        s_sems, r_sems = refs[2 * n], refs[2 * n + 1]
        copies, arrivals = _rest_copies(ins, lds, s_sems, r_sems)

        @pl.when(lax.axis_index("c") == 1)
        def _():
            for cp in copies:
                cp.wait_send()

        for cp in arrivals:
            cp.wait_recv()

    out = _pcall(body, name="gather_rest_wait", in_specs=[HBM] * (2 * n) + [SEM, SEM, ANY],
                 out_specs=[HBM] * (2 * n),
                 out_shape=[pltpu.HBM(a.shape, a.dtype) for a in srcs] + [pltpu.HBM(a.shape, a.dtype) for a in lands],
                 input_output_aliases={i: i for i in range(2 * n)},
                 compiler_params=pltpu.CompilerParams(has_side_effects=EFFECT))(
                     *srcs, *lands, send_sems, recv_sems, after)
    return out[n:]


def pair_send(gl, owner, layer):
    n = len(gl)

    def body(*refs):
        ins, outs = refs[:n], refs[n:2 * n]
        send_sems, recv_sems = refs[2 * n:]
        x, y, c = _place()
        sib = (x, y, 1 - c)
        cps = [_remote(ins[t], outs[t], send_sems, recv_sems, t, sib) for t in range(n)]
        for core in range(2):
            @pl.when(c == core)
            def _():
                for cp in _owned(cps, owner, 1 - core, per=1):
                    cp.start()
                for cp in _owned(cps, owner, 1 - core, per=1):
                    cp.wait_send()
                for cp in _owned(cps, owner, core, per=1):
                    cp.wait_recv()

    return _pcall(body, name="pair_send_l%d" % layer, in_specs=[ANY] * n, out_specs=[ANY] * n,
                  out_shape=[jax.ShapeDtypeStruct(a.shape, a.dtype) for a in gl],
                  scratch_shapes=[pltpu.SemaphoreType.DMA((n,)), pltpu.SemaphoreType.DMA((n,))])(*gl)


def _chip_copies(ins, outs, send_sems, recv_sems):
    x, y, c = _place()
    chips = [(1 - x, y), (x, 1 - y), (1 - x, 1 - y)]
    return [_remote(ins[t].at[2 * chip[0] + chip[1]], outs[t].at[j], send_sems, recv_sems, 3 * t + j, (*chip, c))
            for t in range(len(ins)) for j, chip in enumerate(chips)]


def _owned(cps, owner, core, per=3):
    return [cp for k, cp in enumerate(cps) if owner[k // per] == core]


def chip_send(s1, owner, layer):
    n = len(s1)

    def body(*refs):
        ins, outs = refs[:n], refs[n:2 * n]
        send_sems, recv_sems = refs[2 * n:]
        cps = _chip_copies(ins, outs, send_sems, recv_sems)
        for core in range(2):
            @pl.when(lax.axis_index("c") == core)
            def _():
                for cp in _owned(cps, owner, core):
                    cp.start()
                for cp in _owned(cps, owner, core):
                    cp.wait()

    return _pcall(body, name="chip_send_l%d" % layer, in_specs=[ANY] * n, out_specs=[ANY] * n,
                  out_shape=[jax.ShapeDtypeStruct((3,) + a.shape[1:], a.dtype) for a in s1],
                  scratch_shapes=[pltpu.SemaphoreType.DMA((3 * n,)), pltpu.SemaphoreType.DMA((3 * n,))])(*s1)


def chip_send_start(s1, owner, layer):
    n = len(s1)
    land = [lax.empty((3,) + a.shape[1:], a.dtype) for a in s1]

    def body(*refs):
        ins, lands = refs[:n], refs[n:2 * n]
        send_sems, recv_sems = refs[2 * n], refs[2 * n + 1]
        token = refs[-1]
        cps = _chip_copies(ins, lands, send_sems, recv_sems)
        for core in range(2):
            @pl.when(lax.axis_index("c") == core)
            def _():
                for cp in _owned(cps, owner, core):
                    cp.start()

        token[...] = jnp.zeros_like(token)

    hbm = lambda a: pltpu.with_memory_space_constraint(a, pltpu.HBM)
    out = _pcall(body, name="chip_send_start_l%d" % layer, in_specs=[HBM] * (2 * n),
                 out_specs=[SEM, SEM] + [HBM] * (2 * n) + [pl.BlockSpec(memory_space=pltpu.VMEM)],
                 out_shape=[pltpu.SemaphoreType.DMA((3 * n,)), pltpu.SemaphoreType.DMA((3 * n,))]
                 + [pltpu.HBM(a.shape, a.dtype) for a in s1] + [pltpu.HBM(a.shape, a.dtype) for a in land]
                 + [jax.ShapeDtypeStruct((8, 128), F32)],
                 input_output_aliases={i: 2 + i for i in range(2 * n)},
                 compiler_params=pltpu.CompilerParams(has_side_effects=EFFECT))(
                     *[hbm(a) for a in s1], *[hbm(a) for a in land])
    return out[0], out[1], out[2:2 + n], out[2 + n:2 + 2 * n], out[-1]


def chip_send_wait(send_sems, recv_sems, srcs, lands, after, owner, layer):
    n = len(srcs)

    def body(*refs):
        ins, lds = refs[:n], refs[n:2 * n]
        s_sems, r_sems = refs[2 * n], refs[2 * n + 1]
        cps = _chip_copies(ins, lds, s_sems, r_sems)
        for core in range(2):
            @pl.when(lax.axis_index("c") == core)
            def _():
                for cp in _owned(cps, owner, core):
                    cp.wait_send()
                    cp.wait_recv()

    out = _pcall(body, name="chip_send_wait_l%d" % layer, in_specs=[HBM] * (2 * n) + [SEM, SEM, ANY],
                 out_specs=[HBM] * (2 * n),
                 out_shape=[pltpu.HBM(a.shape, a.dtype) for a in srcs] + [pltpu.HBM(a.shape, a.dtype) for a in lands],
                 input_output_aliases={i: i for i in range(2 * n)},
                 compiler_params=pltpu.CompilerParams(has_side_effects=EFFECT))(
                     *srcs, *lands, send_sems, recv_sems, after)
    return out[n:]


def pair_share(s2, owner):
    n = len(s2)

    def body(*refs):
        ins, outs = refs[:n], refs[n:2 * n]
        send_sems, recv_sems = refs[2 * n:]
        x, y, c = _place()
        sib = (x, y, 1 - c)
        cps = [_remote(ins[t], outs[t], send_sems, recv_sems, t, sib) for t in range(n)]
        for core in range(2):
            @pl.when(c == core)
            def _():
                for cp in _owned(cps, owner, core, per=1):
                    cp.start()
                for cp in _owned(cps, owner, core, per=1):
                    cp.wait_send()
                for cp in _owned(cps, owner, 1 - core, per=1):
                    cp.wait_recv()

    return _pcall(body, name="pair_share", in_specs=[ANY] * n, out_specs=[ANY] * n,
                  out_shape=[jax.ShapeDtypeStruct(a.shape, a.dtype) for a in s2],
                  input_output_aliases={t: t for t in range(n)},
                  scratch_shapes=[pltpu.SemaphoreType.DMA((n,)), pltpu.SemaphoreType.DMA((n,))])(*s2)


def small_exchange(gs):
    rows, width = gs.shape

    def body(g_ref, o_ref, send_sems, recv_sems):
        x, y, c = _place()
        cps = []
        for r in range(1, 8):
            dx, dy, dc = (r >> 2) & 1, (r >> 1) & 1, r & 1
            to = (x if dx == 0 else 1 - x, y if dy == 0 else 1 - y, c if dc == 0 else 1 - c)
            cps.append(_remote(g_ref, o_ref.at[r - 1], send_sems, recv_sems, r - 1, to))
        for cp in cps:
            cp.start()
        for cp in cps:
            cp.wait()

    return _pcall(body, name="small_exchange", in_specs=[ANY], out_specs=ANY,
                  out_shape=jax.ShapeDtypeStruct((7, rows, width), gs.dtype),
                  scratch_shapes=[pltpu.SemaphoreType.DMA((7,)), pltpu.SemaphoreType.DMA((7,))])(gs)


def _row_tile(rows):
    return _pick(rows, (256, 352, 128, 64, 32, 16))


def pair_add_layer(g, rb, core, owner, *, name):
    _, rows, width = g.shape
    tr = _row_tile(rows)

    def body(c_ref, g_ref, r_ref, o_ref, ob_ref):
        @pl.when(c_ref[0] == owner)
        def _():
            s = g_ref[...] + r_ref[...]
            o_ref[...] = s
            ob_ref[...] = s.astype(BF16)

    def at(k, i, c_ref):
        mine = c_ref[0] == owner
        return (jnp.where(mine, k, 0), jnp.where(mine, i, 0), 0)

    blk = pl.BlockSpec((None, tr, width), at)
    gs = pltpu.PrefetchScalarGridSpec(num_scalar_prefetch=1, grid=(4, rows // tr), in_specs=[blk, blk],
                                      out_specs=[blk, blk])
    return _pcall(body, name=name, grid_spec=gs,
                  out_shape=[jax.ShapeDtypeStruct(g.shape, F32), jax.ShapeDtypeStruct(g.shape, BF16)],
                  compiler_params=_cp("parallel", "parallel"))(core, g, rb)


def chip_add_layers(s1, rb2, chip, core, owner, *, name):
    _, rows, width = s1[0].shape
    tr = _row_tile(rows)

    def body(k_ref, c_ref, s0_ref, s1_ref, r0_ref, r1_ref, o_ref):
        @pl.when(c_ref[0] == owner)
        def _():
            first = pl.program_id(0) == 0
            s = jnp.where(first, s0_ref[...], s1_ref[...])
            r = jnp.where(first, r0_ref[...], r1_ref[...]).astype(F32)
            o_ref[...] = ((s + r[0]) + r[1]) + r[2]

    def s_spec(layer):
        def at(l, i, k_ref, c_ref):
            use = jnp.logical_and(l == layer, c_ref[0] == owner)
            return (jnp.where(use, k_ref[0], 0), jnp.where(use, i, 0), 0)
        return pl.BlockSpec((None, tr, width), at)

    def r_spec(layer):
        def at(l, i, k_ref, c_ref):
            return (0, jnp.where(jnp.logical_and(l == layer, c_ref[0] == owner), i, 0), 0)
        return pl.BlockSpec((3, tr, width), at)

    def out_at(l, i, k_ref, c_ref):
        mine = c_ref[0] == owner
        return (jnp.where(mine, l, 0), jnp.where(mine, i, 0), 0)

    gs = pltpu.PrefetchScalarGridSpec(
        num_scalar_prefetch=2, grid=(DEPTH, rows // tr),
        in_specs=[s_spec(0), s_spec(1), r_spec(0), r_spec(1)],
        out_specs=pl.BlockSpec((None, tr, width), out_at))
    return _pcall(body, name=name, grid_spec=gs, out_shape=jax.ShapeDtypeStruct((DEPTH, rows, width), F32),
                  compiler_params=_cp("parallel", "parallel"))(chip, core, s1[0], s1[1], rb2[0], rb2[1])


def small_add(gs_own, slots, me):
    rows, width = gs_own.shape
    tr = _pick(rows, (64, 32, 16, 8))

    def body(me_ref, g_ref, s_ref, o_ref):
        me_v = me_ref[0]
        total = None
        for d in range(8):
            rel = jnp.bitwise_xor(me_v, d)
            val = jnp.where(rel == 0, g_ref[...], s_ref[jnp.maximum(rel - 1, 0)])
            total = val if total is None else total + val
        o_ref[...] = total

    gs = pltpu.PrefetchScalarGridSpec(
        num_scalar_prefetch=1, grid=(rows // tr,),
        in_specs=[pl.BlockSpec((tr, width), lambda i, m_ref: (i, 0)),
                  pl.BlockSpec((7, tr, width), lambda i, m_ref: (0, i, 0))],
        out_specs=pl.BlockSpec((tr, width), lambda i, m_ref: (i, 0)))
    return _pcall(body, name="small_add", grid_spec=gs, out_shape=jax.ShapeDtypeStruct((rows, width), F32),
                  compiler_params=_cp("parallel"))(me, gs_own, slots)


SHARDED = (("ffn1_w_up", (D, UPW)), ("ffn1_w_down", (DFF // 4, D)), ("w_in", (D, D_IN // 4)),
           ("conv_w", (4, BW // 4)), ("gla_w_g2", (LOW_W, 64)), ("w_branch", (3 * BW, D // 4)),
           ("w_out", (D // 4, D)), ("ffn2_w_up", (D, UPW)), ("ffn2_w_down", (DFF // 4, D)),
           ("ple_w_proj", (PLE, D // 4)), ("ple_w_gate", (D // 4, D)))
OWNER = tuple(0 if n in ("ffn1_w_up", "w_in", "w_out") else 1 for n, _ in SHARDED)
SMALL = ("ln1_g", "ln1_b", "conv_b", "lru_wa", "lru_ba", "lru_wx", "lru_bx", "lru_lambda", "gla_b_g",
         "gla_norm_g", "fox_b_f", "ln2_g", "ln2_b", "ln3_g", "ln3_b", "ple_b_gate", "ln4_g", "ln4_b")
WEIGHTS = ('ffn1_w_up', 'ffn1_w_down', 'ln1_g', 'ln1_b', 'w_in', 'conv_w', 'conv_b', 'lru_wa', 'lru_ba',
           'lru_wx', 'lru_bx', 'lru_lambda', 'gla_w_g2', 'gla_b_g', 'gla_norm_g', 'fox_b_f', 'w_branch',
           'w_out', 'ln2_g', 'ln2_b', 'ffn2_w_up', 'ffn2_w_down', 'ln3_g', 'ln3_b', 'ple_w_proj',
           'ple_w_gate', 'ple_b_gate', 'ln4_g', 'ln4_b')


def _cols_join(parts):
    return jnp.concatenate([parts[k] for k in range(4)], axis=-1)


def _cols_split(full):
    r, c4 = full.shape
    return full.reshape(r, 4, c4 // 4).transpose(1, 0, 2)


def _regroup_in(w):
    pad = jnp.zeros(w.shape[:-1] + (ZW - D_IN,), w.dtype)
    fox_q = (w[..., 2576:3088] * FOX_SCALE).astype(w.dtype)
    return jnp.concatenate([w[..., 0:2048], w[..., 2064:2576], fox_q, w[..., 3088:4112], w[..., 4120:7192],
                            w[..., 2048:2064], w[..., 4112:4120], pad], axis=-1)


_IN_RUNS = ((0, 2048, 0, 1.0), (2048, 2064, 7168, 1.0), (2064, 2576, 2048, 1.0), (2576, 3088, CQ, FOX_SCALE),
            (3088, 4112, CK, 1.0), (4112, 4120, 7184, 1.0), (4120, D_IN, 4096, 1.0))


def _regroup_out_shards(g):
    w = D_IN // 4
    shards = []
    for k in range(4):
        pieces = []
        for a, b, new, f in _IN_RUNS:
            lo, hi = max(a, k * w), min(b, (k + 1) * w)
            if lo < hi:
                piece = g[:, new + lo - a:new + hi - a]
                pieces.append(piece if f == 1.0 else piece * f)
        shards.append(jnp.concatenate(pieces, axis=1))
    return jnp.stack(shards)


def _block_diag(w):
    eye = jnp.eye(8, dtype=w.dtype)
    return (eye[:, None, :, None] * w[:, :, None, :]).reshape(BW, BW)


def _diag_blocks(dense):
    return jnp.stack([dense[64 * n:64 * (n + 1), 64 * n:64 * (n + 1)] for n in range(8)])


def _layer_weights(gw, small, l):
    w = {"up1": gw["ffn1_w_up"], "up2": gw["ffn2_w_up"],
         "dn1": gw["ffn1_w_down"].reshape(1, DFF, D), "dn2": gw["ffn2_w_down"].reshape(1, DFF, D),
         "wo": gw["w_out"].reshape(1, D, D), "wgt": gw["ple_w_gate"].reshape(1, D, D)}
    w["win"] = _regroup_in(_cols_join(gw["w_in"][0]))
    w["cw"] = _cols_join(gw["conv_w"][0])
    w["wa"] = _block_diag(small["lru_wa"][l]).astype(BF16)
    w["wx"] = _block_diag(small["lru_wx"][l]).astype(BF16)
    w["wg2p"] = jnp.pad(_cols_join(gw["gla_w_g2"][0]), ((0, 128 - LOW_W), (0, 0)))
    w["wbr"] = _cols_join(gw["w_branch"][0].reshape(4, 3, BW, D // 4))
    w["wp"] = _cols_join(gw["ple_w_proj"][0])
    for n in ("ln1_g", "ln1_b", "ln2_g", "ln2_b", "ln3_g", "ln3_b", "ln4_g", "ln4_b", "conv_b", "lru_ba",
              "lru_bx", "lru_lambda", "gla_b_g", "gla_norm_g", "ple_b_gate"):
        w[n] = small[n][l][None, :]
    w["bfp"] = jnp.pad(small["fox_b_f"][l], (LOW_W, 128 - LOW_W - FOX_H))[None, :]
    return w


def _heads_t(a):
    ht = a[:, LOW_W:LOW_W + FOX_H].T
    return ht[:, None, :], jnp.broadcast_to(ht[:, :, None], ht.shape + (128,))


def _layer_fwd(x, xb, pb, w, l):
    s = {"x0": x, "x0b": xb}
    tag = "l%d_" % l
    gate, up, act = ffn_up(xb, w["up1"], 0, name=tag + "ffn1_up")
    r1, x1, x1b = matmul_res_ln(act, w["dn1"], 0, x, w["ln1_g"], w["ln1_b"], mm_scale=0.5, name=tag + "ffn1_down")
    s.update(gate1=gate, up1=up, act1=act, r1=r1, x1=x1, x1b=x1b)
    z, zb = matmul(x1b, w["win"], also_bf16=True, tm=1024, tn=_pick(ZW, (2432,)), name=tag + "mix_in")
    xc, xcb, h, ya = lru_fwd(z, w["cw"], w["conv_b"], w["wa"], w["wx"], w["lru_ba"], w["lru_bx"],
                             w["lru_lambda"], name=tag + "lru_fwd")
    yb, states = gla_fwd(z, zb, w["wg2p"], w["gla_b_g"], w["gla_norm_g"], name=tag + "gla_fwd")
    fcum = fox_fcum(z, w["bfp"], name=tag + "fox_fcum")
    fq, fk = _heads_t(fcum)
    yc, ycf, lse = fox_fwd_t(zb, fq, fk, name=tag + "fox_fwd")
    merged = merge_fwd(ya, yb, yc, w["wbr"], z, name=tag + "merge_fwd")
    r2, x2, x2b = matmul_res_ln(merged, w["wo"], 0, x1, w["ln2_g"], w["ln2_b"], mm_scale=1.0, name=tag + "mix_out")
    s.update(z=z, zb=zb, xc=xc, xcb=xcb, h=h, ya=ya, yb=yb, states=states, fq=fq, fk=fk, yc=yc, ycf=ycf,
             lse=lse, merged=merged, r2=r2, x2=x2, x2b=x2b)
    gate, up, act = ffn_up(x2b, w["up2"], 0, name=tag + "ffn2_up")
    r3, x3, x3b = matmul_res_ln(act, w["dn2"], 0, x2, w["ln3_g"], w["ln3_b"], mm_scale=0.5, name=tag + "ffn2_down")
    s.update(gate2=gate, up2=up, act2=act, r3=r3, x3=x3, x3b=x3b)
    r4, x4, x4b = ple_fwd(x3b, x3, pb, w["wgt"], 0, w["wp"], w["ple_b_gate"], w["ln4_g"], w["ln4_b"],
                          name=tag + "ple_fwd")
    s.update(r4=r4, pb=pb)
    return x4, x4b, s


def _ffn_bwd(dy, s, w, n, xin_b, l, tag):
    k = {"1": ("r1", "ln1_g", "gate1", "up1", "act1"), "2": ("r3", "ln3_g", "gate2", "up2", "act2")}[n]
    dr, dfb, dg, db = ln_bwd(dy, s[k[0]], w[k[1]], out_scale=0.5, name=tag + "ln_bwd")
    dgate, dup = ffn_down_bwd(dfb, w["dn" + n], 0, s[k[2]], s[k[3]], name=tag + "down_bwd")
    dx = ffn_dx(dgate, dup, w["up" + n], 0, dr, name=tag + "dx")
    dwup = matmul_tn_up(xin_b, dgate, dup, name=tag + "dw_up")
    dwdn = matmul_tn(s[k[4]], dfb, name=tag + "dw_down").reshape(4, DFF // 4, D)
    return dx, dwup, dwdn, dg[0], db[0]


def _layer_bwd(dy, s, w, l):
    g = {}
    tag = "l%d_" % l
    dr4, dglb, dpeb, dg4, db4, dbg = ple_bwd(dy, s["r4"], s["x3b"], s["pb"], w["wgt"], 0, w["wp"], w["ple_b_gate"],
                                             w["ln4_g"], name=tag + "ple_bwd")
    dx3 = matmul(dglb, w["wgt"], nt=True, b_lead=(0,), res=dr4, res_scale=ALPHA, tm=1024, tn=1024,
                 name=tag + "ple_dx")
    g["ple_w_gate"] = matmul_tn(s["x3b"], dglb, name=tag + "ple_dw_gate").reshape(4, D // 4, D)
    g["ple_w_proj"] = _cols_split(matmul_tn(s["pb"], dpeb, name=tag + "ple_dw_proj"))
    g["ln4_g"], g["ln4_b"], g["ple_b_gate"] = dg4[0], db4[0], dbg[0]
    dx2, g["ffn2_w_up"], g["ffn2_w_down"], g["ln3_g"], g["ln3_b"] = _ffn_bwd(dx3, s, w, "2", s["x2b"], l,
                                                                             tag + "ffn2_")
    dr2, doutb, dg2, db2 = ln_bwd(dx2, s["r2"], w["ln2_g"], out_scale=1.0, name=tag + "mix_ln_bwd")
    g["ln2_g"], g["ln2_b"] = dg2[0], db2[0]
    g["w_out"] = matmul_tn(s["merged"], doutb, name=tag + "dw_out").reshape(4, D // 4, D)
    z, zb = s["z"], s["zb"]
    (dya, dyb, dyc, dp0, dp1, dp2, dgl0, dgl1, dgl2) = merge_bwd(
        doutb, w["wo"], 0, s["ya"], s["yb"], s["yc"], w["wbr"], z, name=tag + "merge_bwd")
    dwbr = jnp.stack([matmul_tn(s["ya"], dp0, name=tag + "dw_br0"), matmul_tn(s["yb"], dp1, name=tag + "dw_br1"),
                      matmul_tn(s["yc"], dp2, name=tag + "dw_br2")])
    g["w_branch"] = _cols_split(dwbr.reshape(3 * BW, D))
    day, dxc, dprb, dpib, dba, dbx, dlam = lru_bwd(dya, z, s["h"], s["xc"], w["wa"], w["wx"],
                                                   w["lru_ba"], w["lru_bx"], w["lru_lambda"], name=tag + "lru_bwd")
    dax, dcw, dcb = conv_bwd(dxc, z, w["cw"], name=tag + "conv_bwd")
    g["lru_wa"] = _diag_blocks(matmul_tn(s["xcb"], dprb, name=tag + "dw_lru_a"))
    g["lru_wx"] = _diag_blocks(matmul_tn(s["xcb"], dpib, name=tag + "dw_lru_x"))
    g["lru_ba"], g["lru_bx"], g["lru_lambda"] = dba[0], dbx[0], dlam[0]
    g["conv_w"], g["conv_b"] = _cols_split(dcw), dcb[0]
    dbq, dbk, dbv, dbr, dmisc_g, dpreb, dbgg, dng = gla_bwd(dyb, z, zb, s["states"], w["wg2p"], w["gla_b_g"],
                                                            w["gla_norm_g"], name=tag + "gla_bwd")
    miscb = zb[:, MISC:]
    g["gla_w_g2"] = _cols_split(matmul_tn(miscb, dpreb, name=tag + "dw_g2")[:LOW_W])
    g["gla_b_g"], g["gla_norm_g"] = dbgg[0], dng[0]
    dl = fox_delta(dyc, s["ycf"], name=tag + "fox_delta")
    t = z.shape[0]
    dlq = dl[:, :FOX_H].T[:, None, :]
    dcq, dck, dcv, dfk, dfq = fox_bwd_t(zb, dyc, s["fq"], s["fk"], s["lse"], dlq, name=tag + "fox_bwd")
    dfc = jnp.pad((dfk[:, 0, :] + dfq.reshape(FOX_H, t)).T, ((0, 0), (LOW_W, 128 - LOW_W - FOX_H)))
    dmiscb, dbf = fox_dcf(dfc, z, w["bfp"], dmisc_g, name=tag + "fox_dcf")
    g["fox_b_f"] = dbf[0, LOW_W:LOW_W + FOX_H]
    dz = jnp.concatenate([dax, day, dbq, dbk, dbv, dbr, dcq, dck, dcv, dgl0, dgl1, dgl2, dmiscb], axis=1)
    dx1 = matmul(dz, w["win"], nt=True, res=dr2, res_scale=ALPHA, tm=1024, tn=1024, tk=_pick(ZW, (2432,)),
                 name=tag + "mix_dx")
    g["w_in"] = _regroup_out_shards(matmul_tn(s["x1b"], dz, name=tag + "dw_in"))
    dx0, g["ffn1_w_up"], g["ffn1_w_down"], g["ln1_g"], g["ln1_b"] = _ffn_bwd(dx1, s, w, "1", s["x0b"], l,
                                                                             tag + "ffn1_")
    return dx0, g


def _local_step(x, p, target, gathered, small, after_last_layer=None):
    xcur = x
    xb = xcur.astype(BF16)
    layer_w, saved = [], []
    for l in range(DEPTH):
        w = _layer_weights(gathered(l, xcur), small, l)
        xcur, xb, s = _layer_fwd(xcur, xb, p[l].astype(BF16), w, l)
        layer_w.append(w)
        saved.append(s)
    dy, sq = loss_head(xcur, target, name="loss_head")
    grads = [None] * DEPTH
    for l in reversed(range(DEPTH)):
        dy, grads[l] = _layer_bwd(dy, saved[l], layer_w[l], l)
        if l == DEPTH - 1 and after_last_layer is not None:
            layer_w[l - 1]["ln4_g"] = layer_w[l - 1]["ln4_g"] + after_last_layer(grads[l])
    return 0.5 * jnp.sum(sq) / float(D), dy, grads


def kernel(x, p, ffn1_w_up, ffn1_w_down, ln1_g, ln1_b, w_in, conv_w, conv_b, lru_wa, lru_ba, lru_wx, lru_bx, lru_lambda, gla_w_g2, gla_b_g, gla_norm_g, fox_b_f, w_branch, w_out, ln2_g, ln2_b, ffn2_w_up, ffn2_w_down, ln3_g, ln3_b, ple_w_proj, ple_w_gate, ple_b_gate, ln4_g, ln4_b, loss_target, m_ffn1_w_up, m_ffn1_w_down, m_ln1_g, m_ln1_b, m_w_in, m_conv_w, m_conv_b, m_lru_wa, m_lru_ba, m_lru_wx, m_lru_bx, m_lru_lambda, m_gla_w_g2, m_gla_b_g, m_gla_norm_g, m_fox_b_f, m_w_branch, m_w_out, m_ln2_g, m_ln2_b, m_ffn2_w_up, m_ffn2_w_down, m_ln3_g, m_ln3_b, m_ple_w_proj, m_ple_w_gate, m_ple_b_gate, m_ln4_g, m_ln4_b, v_ffn1_w_up, v_ffn1_w_down, v_ln1_g, v_ln1_b, v_w_in, v_conv_w, v_conv_b, v_lru_wa, v_lru_ba, v_lru_wx, v_lru_bx, v_lru_lambda, v_gla_w_g2, v_gla_b_g, v_gla_norm_g, v_fox_b_f, v_w_branch, v_w_out, v_ln2_g, v_ln2_b, v_ffn2_w_up, v_ffn2_w_down, v_ln3_g, v_ln3_b, v_ple_w_proj, v_ple_w_gate, v_ple_b_gate, v_ln4_g, v_ln4_b):
    args = dict(locals())
    wts = {n: args[n] for n in WEIGHTS}
    mom = {n: args["m_" + n] for n in WEIGHTS}
    var = {n: args["v_" + n] for n in WEIGHTS}
    cx, cy, cc = lax.axis_index("x"), lax.axis_index("y"), lax.axis_index("c")

    names = [n for n, _ in SHARDED]
    shards = [wts[n].reshape((DEPTH,) + rc).astype(F32 if n == "conv_w" else BF16) for n, rc in SHARDED]
    first, lands = gather_first(shards)
    rest_send, rest_recv, rest_srcs, rest_lands, rest_token = gather_rest_start(shards, lands)
    small = {n: wts[n] for n in SMALL}
    small["ln1_g"] = small["ln1_g"] + rest_token[0, 0]

    def gathered(l, after):
        if l == 0:
            return dict(zip(names, first))
        return dict(zip(names, gather_rest_wait(rest_send, rest_recv, rest_srcs, rest_lands, after)))

    flight = {}
    core = jnp.reshape(cc, (1,)).astype(jnp.int32)
    chip = jnp.reshape(2 * cx + cy, (1,)).astype(jnp.int32)

    def chip_sum(gl, layer):
        lst = [gl[n] for n in names]
        rb = pair_send(lst, OWNER, layer)
        return [pair_add_layer(a, r, core, own, name="pair_add_l%d_%s" % (layer, n))
                for n, own, a, r in zip(names, OWNER, lst, rb)]

    def start_last_layer(gl):
        s1 = chip_sum(gl, DEPTH - 1)
        send_sems, recv_sems, srcs, lands, token = chip_send_start([sb for _, sb in s1], OWNER, DEPTH - 1)
        flight.update(s1=[sf for sf, _ in s1], sems=(send_sems, recv_sems), srcs=srcs, lands=lands)
        return token[0, 0]

    loss_local, dx, grads = _local_step(x[0], p[:, 0], loss_target[0], gathered, small, start_last_layer)
    loss = lax.psum(loss_local, ("x", "y", "c"))
    grad_x = dx[None]

    s1_first = chip_sum(grads[0], 0)
    rb2_first = chip_send([sb for _, sb in s1_first], OWNER, 0)
    rb2_last = chip_send_wait(*flight["sems"], flight["srcs"], flight["lands"], dx, OWNER, DEPTH - 1)
    s2 = [chip_add_layers((sf0, sf1), (r0, r1), chip, core, own, name="chip_add_" + n)
          for n, own, (sf0, _), sf1, r0, r1 in zip(names, OWNER, s1_first, flight["s1"], rb2_first, rb2_last)]
    gsh = dict(zip(names, pair_share(s2, OWNER)))

    pieces, spans, row = [], {}, 0
    for n in SMALL:
        flat = jnp.stack([grads[l][n] for l in range(DEPTH)]).reshape(-1)
        rows = -(-flat.shape[0] // (8 * PACK_W)) * 8
        pieces.append(jnp.pad(flat, (0, rows * PACK_W - flat.shape[0])).reshape(rows, PACK_W))
        spans[n] = (row, rows)
        row += rows
    gs = jnp.concatenate(pieces, axis=0)
    me = jnp.reshape(4 * cx + 2 * cy + cc, (1,)).astype(jnp.int32)
    gsum = small_add(gs, small_exchange(gs), me)

    gout, delta, new_m, new_v = {}, {}, {}, {}
    for n in WEIGHTS:
        shp = wts[n].shape
        if n in gsh:
            view = gsh[n].shape
            g = gsh[n]
        else:
            view = (1, DEPTH, wts[n].size // DEPTH)
            r0, rows = spans[n]
            g = gsum[r0:r0 + rows].reshape(-1)[:wts[n].size].reshape(view)
        d, mn, vn = adamw(wts[n].reshape(view), g, mom[n].reshape(view), var[n].reshape(view), name="adamw_" + n)
        gout[n], delta[n], new_m[n], new_v[n] = g.reshape(shp), d.reshape(shp), mn.reshape(shp), vn.reshape(shp)

    return (loss, grad_x, *[gout[n] for n in WEIGHTS], *[delta[n] for n in WEIGHTS],
            *[new_m[n] for n in WEIGHTS], *[new_v[n] for n in WEIGHTS])
```

```python
import functools
import math

import jax
import jax.numpy as jnp
from jax import lax
from jax.experimental import pallas as pl
from jax.experimental.pallas import tpu as pltpu

F32 = jnp.float32
BF16 = jnp.bfloat16

D = 1024
DFF = 2816
BW = 512
PLE = 256
DEPTH = 2
ALPHA = (2 * DEPTH) ** 0.25
LN_EPS = 1e-5
RMS_EPS = 1e-6
LRU_C = 8.0
GLA_TAU = 16.0
CHUNK = 64
D_IN = 7192
ZW = 7296
AX, AY, BQ, BK, BV, BR, CQ, CK, CV, G0, MISC = 0, 512, 1024, 1280, 1536, 2048, 2560, 3072, 3584, 4096, 7168
LOW_W, FOX_H = 16, 8
ADAM_LR, ADAM_B1, ADAM_B2, ADAM_EPS, ADAM_WD, ADAM_STEP = 0.001, 0.9, 0.999, 1e-08, 0.01, 10
PACK_W = 1024
VMEM_LIMIT = 56 << 20

MESH = pl.DeviceIdType.MESH
ANY = pl.BlockSpec(memory_space=pl.ANY)


def _pcall(body, **kw):
    return pl.pallas_call(body, **kw)


def _cp(*dims):
    return pltpu.CompilerParams(dimension_semantics=dims, vmem_limit_bytes=VMEM_LIMIT)


def _dot(a, b):
    return jnp.dot(a, b, preferred_element_type=F32)


def _dot_nt(a, b):
    return lax.dot_general(a, b, (((1,), (1,)), ((), ())), preferred_element_type=F32)


def _dot_tn(a, b):
    return lax.dot_general(a, b, (((0,), (0,)), ((), ())), preferred_element_type=F32)


def _dot_hi(a, b):
    return jnp.dot(a, b, preferred_element_type=F32, precision=lax.Precision.HIGHEST)


def _sigmoid(x):
    return 1.0 / (1.0 + jnp.exp(-x))


def _softplus(x):
    return jnp.maximum(x, 0.0) + jnp.log(1.0 + jnp.exp(-jnp.abs(x)))


def _log_sigmoid(x):
    return -_softplus(-x)


def _expm1(x):
    poly = x * (1.0 + x * (0.5 + x * (1.0 / 6.0 + x * (1.0 / 24.0 + x * (1.0 / 120.0 + x * (1.0 / 720.0))))))
    return jnp.where(jnp.abs(x) < 0.1, poly, jnp.exp(x) - 1.0)


_GELU_C = math.sqrt(2.0 / math.pi)


def _gelu(x):
    return 0.5 * x * (1.0 + jnp.tanh(_GELU_C * (x + 0.044715 * x * x * x)))


def _gelu_grad(x):
    t = jnp.tanh(_GELU_C * (x + 0.044715 * x * x * x))
    return 0.5 * (1.0 + t) + 0.5 * x * (1.0 - t * t) * _GELU_C * (1.0 + 3.0 * 0.044715 * x * x)


def _ln_stats(r):
    mu = jnp.mean(r, axis=-1, keepdims=True)
    xc = r - mu
    var = jnp.mean(xc * xc, axis=-1, keepdims=True)
    return xc, lax.rsqrt(var + LN_EPS)


def _pick(n, cands):
    for c in cands:
        if n % c == 0:
            return c
    return n


def _rows(tm, w, col=0):
    return pl.BlockSpec((tm, w), lambda i: (i, col))


def _fix(shape):
    nd = len(shape)
    return pl.BlockSpec(shape, lambda i: (0,) * nd)


def _col_chunks(n, width=256):
    return [slice(c, min(c + width, n)) for c in range(0, n, width)]


def _layer(l, shape):
    nd = len(shape)
    return pl.BlockSpec((None,) + tuple(shape), lambda i: (l,) + (0,) * nd)


def matmul(a, b, *, name, nt=False, b_lead=(), res=None, res_scale=1.0, also_bf16=False, tm=512, tn=512,
           tk=None):
    m, k = a.shape
    n = b.shape[-2] if nt else b.shape[-1]
    tm, tn = min(tm, m), min(tn, n)
    tk = k if tk is None else tk
    nk = k // tk
    has_res = res is not None
    lead = tuple(b_lead)
    dot = _dot_nt if nt else _dot

    def body(*refs):
        a_ref, b_ref = refs[0], refs[1]
        pos = 2
        r_ref = None
        if has_res:
            r_ref = refs[pos]
            pos += 1
        o_ref = refs[pos]
        pos += 1
        ob_ref = None
        if also_bf16:
            ob_ref = refs[pos]
            pos += 1

        def finish(v):
            if has_res:
                v = v + res_scale * r_ref[...]
            o_ref[...] = v
            if also_bf16:
                ob_ref[...] = v.astype(BF16)

        if nk == 1:
            finish(dot(a_ref[...], b_ref[...]))
            return
        acc = refs[pos]
        kk = pl.program_id(2)

        @pl.when(kk == 0)
        def _():
            acc[...] = jnp.zeros_like(acc)

        acc[...] += dot(a_ref[...], b_ref[...])

        @pl.when(kk == nk - 1)
        def _():
            finish(acc[...])

    none = (None,) * len(lead)
    if nt:
        b_spec = pl.BlockSpec(none + (tn, tk), lambda j, i, kk: lead + (j, kk))
    else:
        b_spec = pl.BlockSpec(none + (tk, tn), lambda j, i, kk: lead + (kk, j))
    in_specs = [pl.BlockSpec((tm, tk), lambda j, i, kk: (i, kk)), b_spec]
    args = [a, b]
    if has_res:
        in_specs.append(pl.BlockSpec((tm, tn), lambda j, i, kk: (i, j)))
        args.append(res)
    out_shape = [jax.ShapeDtypeStruct((m, n), F32)]
    out_specs = [pl.BlockSpec((tm, tn), lambda j, i, kk: (i, j))]
    if also_bf16:
        out_shape.append(jax.ShapeDtypeStruct((m, n), BF16))
        out_specs.append(pl.BlockSpec((tm, tn), lambda j, i, kk: (i, j)))
    out = _pcall(body, name=name, grid=(n // tn, m // tm, nk), in_specs=in_specs, out_specs=out_specs,
                 out_shape=out_shape, scratch_shapes=[pltpu.VMEM((tm, tn), F32)] if nk > 1 else [],
                 compiler_params=_cp("parallel", "parallel", "arbitrary"))(*args)
    return out if also_bf16 else out[0]


def matmul_tn(a, b, *, name):
    t, k = a.shape
    n = b.shape[1]
    tk = _pick(k, (1024, 1408, 512, 256, 128))
    tn = _pick(n, (1024, 1408, 2432, 512, 256, 128))
    tt = min(1024 if tk * tn > (1 << 20) else 2048, t)
    nt = t // tt

    def body(a_ref, b_ref, o_ref):
        @pl.when(pl.program_id(2) == 0)
        def _():
            o_ref[...] = jnp.zeros_like(o_ref)

        o_ref[...] += _dot_tn(a_ref[...], b_ref[...])

    return _pcall(body, name=name, grid=(k // tk, n // tn, nt),
                  in_specs=[pl.BlockSpec((tt, tk), lambda i, j, s: (s, i)),
                            pl.BlockSpec((tt, tn), lambda i, j, s: (s, j))],
                  out_specs=pl.BlockSpec((tk, tn), lambda i, j, s: (i, j)),
                  out_shape=jax.ShapeDtypeStruct((k, n), F32),
                  compiler_params=_cp("parallel", "parallel", "arbitrary"))(a, b)


UPW = 1408


def matmul_tn_up(a, dgate, dup, *, name):
    t, k = a.shape
    tt = min(1024, t)
    tk = 1024

    def body(a_ref, g_ref, u_ref, o_ref):
        j = pl.program_id(1)

        @pl.when(pl.program_id(2) == 0)
        def _():
            o_ref[...] = jnp.zeros_like(o_ref)

        @pl.when(j < 2)
        def _():
            o_ref[...] += _dot_tn(a_ref[...], g_ref[...])

        @pl.when(j >= 2)
        def _():
            o_ref[...] += _dot_tn(a_ref[...], u_ref[...])

    return _pcall(body, name=name, grid=(k // tk, 4, t // tt),
                  in_specs=[pl.BlockSpec((tt, tk), lambda i, j, s: (s, i)),
                            pl.BlockSpec((tt, UPW), lambda i, j, s: (jnp.where(j < 2, s, 0), jnp.minimum(j, 1))),
                            pl.BlockSpec((tt, UPW), lambda i, j, s: (jnp.where(j >= 2, s, 0), jnp.maximum(j - 2, 0)))],
                  out_specs=pl.BlockSpec((None, tk, UPW), lambda i, j, s: (j, i, 0)),
                  out_shape=jax.ShapeDtypeStruct((4, k, UPW), F32),
                  compiler_params=_cp("parallel", "parallel", "arbitrary"))(a, dgate, dup)


def ffn_dx(dgate, dup, wup, l, res, *, name):
    t = dgate.shape[0]
    tm, tn = min(1024, t), 1024

    def body(g_ref, u_ref, w_ref, r_ref, o_ref, acc):
        kk = pl.program_id(2)

        @pl.when(kk == 0)
        def _():
            acc[...] = jnp.zeros_like(acc)

        @pl.when(kk < 2)
        def _():
            acc[...] += _dot_nt(g_ref[...], w_ref[...])

        @pl.when(kk >= 2)
        def _():
            acc[...] += _dot_nt(u_ref[...], w_ref[...])

        @pl.when(kk == 3)
        def _():
            o_ref[...] = acc[...] + ALPHA * r_ref[...]

    return _pcall(body, name=name, grid=(D // tn, t // tm, 4),
                  in_specs=[pl.BlockSpec((tm, UPW), lambda j, i, kk: (i, jnp.minimum(kk, 1))),
                            pl.BlockSpec((tm, UPW), lambda j, i, kk: (i, jnp.maximum(kk - 2, 0))),
                            pl.BlockSpec((None, None, tn, UPW), lambda j, i, kk: (l, kk, j, 0)),
                            pl.BlockSpec((tm, tn), lambda j, i, kk: (i, j))],
                  out_specs=pl.BlockSpec((tm, tn), lambda j, i, kk: (i, j)),
                  out_shape=jax.ShapeDtypeStruct((t, D), F32),
                  scratch_shapes=[pltpu.VMEM((tm, tn), F32)],
                  compiler_params=_cp("parallel", "parallel", "arbitrary"))(dgate, dup, wup, res)


def ffn_up(xb, wup, l, *, name):
    t = xb.shape[0]
    tm, tn = min(1024, t), UPW

    def body(x_ref, wg_ref, wu_ref, g_ref, u_ref, a_ref):
        x = x_ref[...]
        for cols in _col_chunks(tn):
            g = _dot(x, wg_ref[:, cols])
            u = _dot(x, wu_ref[:, cols])
            g_ref[:, cols] = g.astype(BF16)
            u_ref[:, cols] = u.astype(BF16)
            a_ref[:, cols] = (g * _sigmoid(g) * u).astype(BF16)

    blk = pl.BlockSpec((tm, tn), lambda j, i: (i, j))
    return _pcall(body, name=name, grid=(DFF // tn, t // tm),
                  in_specs=[pl.BlockSpec((tm, D), lambda j, i: (i, 0)),
                            pl.BlockSpec((None, None, D, tn), lambda j, i: (l, j, 0, 0)),
                            pl.BlockSpec((None, None, D, tn), lambda j, i: (l, 2 + j, 0, 0))],
                  out_specs=[blk, blk, blk],
                  out_shape=[jax.ShapeDtypeStruct((t, DFF), BF16)] * 3,
                  compiler_params=_cp("parallel", "parallel"))(xb, wup, wup)


def matmul_res_ln(a, w, l, res, g, b, *, mm_scale, name):
    t, k = a.shape
    tm = min(512, t)

    def body(a_ref, w_ref, res_ref, g_ref, b_ref, r_ref, y_ref, yb_ref):
        f = _dot(a_ref[...], w_ref[...])
        r = ALPHA * res_ref[...] + mm_scale * f
        xc, rstd = _ln_stats(r)
        y = xc * rstd * g_ref[...] + b_ref[...]
        r_ref[...] = r
        y_ref[...] = y
        yb_ref[...] = y.astype(BF16)

    return _pcall(body, name=name, grid=(t // tm,),
                  in_specs=[_rows(tm, k), _layer(l, (k, D)), _rows(tm, D), _fix((1, D)), _fix((1, D))],
                  out_specs=[_rows(tm, D)] * 3,
                  out_shape=[jax.ShapeDtypeStruct((t, D), F32), jax.ShapeDtypeStruct((t, D), F32),
                             jax.ShapeDtypeStruct((t, D), BF16)],
                  compiler_params=_cp("parallel"))(a, w, res, g, b)


def ln_bwd(dy, r, g, *, out_scale, name):
    t = dy.shape[0]
    tm = min(512, t)

    def body(dy_ref, r_ref, g_ref, dr_ref, drb_ref, dg_ref, db_ref):
        @pl.when(pl.program_id(0) == 0)
        def _():
            dg_ref[...] = jnp.zeros_like(dg_ref)
            db_ref[...] = jnp.zeros_like(db_ref)

        xc, rstd = _ln_stats(r_ref[...])
        xhat = xc * rstd
        d = dy_ref[...]
        dxh = d * g_ref[...]
        dr = rstd * (dxh - jnp.mean(dxh, axis=-1, keepdims=True)
                     - xhat * jnp.mean(dxh * xhat, axis=-1, keepdims=True))
        dr_ref[...] = dr
        drb_ref[...] = (out_scale * dr).astype(BF16)
        dg_ref[...] += jnp.sum(d * xhat, axis=0, keepdims=True)
        db_ref[...] += jnp.sum(d, axis=0, keepdims=True)

    return _pcall(body, name=name, grid=(t // tm,),
                  in_specs=[_rows(tm, D), _rows(tm, D), _fix((1, D))],
                  out_specs=[_rows(tm, D), _rows(tm, D), _fix((1, D)), _fix((1, D))],
                  out_shape=[jax.ShapeDtypeStruct((t, D), F32), jax.ShapeDtypeStruct((t, D), BF16),
                             jax.ShapeDtypeStruct((1, D), F32), jax.ShapeDtypeStruct((1, D), F32)],
                  compiler_params=_cp("arbitrary"))(dy, r, g)


def ffn_down_bwd(dfb, wd, l, gate, up, *, name):
    t = dfb.shape[0]
    tm, tn = min(1024, t), UPW
    nj = DFF // tn

    def body(df_ref, w_ref, g_ref, u_ref, dg_ref, du_ref):
        df = df_ref[...]
        for cols in _col_chunks(tn):
            da = _dot_nt(df, w_ref[cols, :])
            g = g_ref[:, cols].astype(F32)
            s = _sigmoid(g)
            gs = g * s
            dg_ref[:, cols] = (da * u_ref[:, cols].astype(F32) * (s + gs * (1.0 - s))).astype(BF16)
            du_ref[:, cols] = (da * gs).astype(BF16)

    blk = pl.BlockSpec((tm, tn), lambda j, i: (i, j))
    return _pcall(body, name=name, grid=(nj, t // tm),
                  in_specs=[pl.BlockSpec((tm, D), lambda j, i: (i, 0)),
                            pl.BlockSpec((None, tn, D), lambda j, i: (l, j, 0)), blk, blk],
                  out_specs=[blk, blk],
                  out_shape=[jax.ShapeDtypeStruct((t, DFF), BF16), jax.ShapeDtypeStruct((t, DFF), BF16)],
                  compiler_params=_cp("parallel", "parallel"))(dfb, wd, gate, up)


def ple_fwd(xb, x, pb, wgate, l, wproj, bgate, g, b, *, name):
    t = x.shape[0]
    tm = min(512, t)

    def body(xb_ref, x_ref, p_ref, wg_ref, wp_ref, bg_ref, g_ref, b_ref, r_ref, y_ref, yb_ref):
        gl = _dot(xb_ref[...], wg_ref[...]) + bg_ref[...]
        pe = _dot(p_ref[...], wp_ref[...])
        r = ALPHA * x_ref[...] + _sigmoid(gl) * pe
        xc, rstd = _ln_stats(r)
        y = xc * rstd * g_ref[...] + b_ref[...]
        r_ref[...] = r
        y_ref[...] = y
        yb_ref[...] = y.astype(BF16)

    return _pcall(body, name=name, grid=(t // tm,),
                  in_specs=[_rows(tm, D), _rows(tm, D), _rows(tm, PLE), _layer(l, (D, D)), _fix((PLE, D)),
                            _fix((1, D)), _fix((1, D)), _fix((1, D))],
                  out_specs=[_rows(tm, D)] * 3,
                  out_shape=[jax.ShapeDtypeStruct((t, D), F32), jax.ShapeDtypeStruct((t, D), F32),
                             jax.ShapeDtypeStruct((t, D), BF16)],
                  compiler_params=_cp("parallel"))(xb, x, pb, wgate, wproj, bgate, g, b)


def ple_bwd(dy, r, xb, pb, wgate, l, wproj, bgate, g, *, name):
    t = dy.shape[0]
    tm = min(512, t)

    def body(dy_ref, r_ref, xb_ref, p_ref, wg_ref, wp_ref, bg_ref, g_ref,
             dr_ref, dgl_ref, dpe_ref, dg_ref, db_ref, dbg_ref):
        @pl.when(pl.program_id(0) == 0)
        def _():
            dg_ref[...] = jnp.zeros_like(dg_ref)
            db_ref[...] = jnp.zeros_like(db_ref)
            dbg_ref[...] = jnp.zeros_like(dbg_ref)

        xc, rstd = _ln_stats(r_ref[...])
        xhat = xc * rstd
        d = dy_ref[...]
        dxh = d * g_ref[...]
        dr = rstd * (dxh - jnp.mean(dxh, axis=-1, keepdims=True)
                     - xhat * jnp.mean(dxh * xhat, axis=-1, keepdims=True))
        s = _sigmoid(_dot(xb_ref[...], wg_ref[...]) + bg_ref[...])
        pe = _dot(p_ref[...], wp_ref[...])
        dgl = dr * pe * s * (1.0 - s)
        dr_ref[...] = dr
        dgl_ref[...] = dgl.astype(BF16)
        dpe_ref[...] = (dr * s).astype(BF16)
        dg_ref[...] += jnp.sum(d * xhat, axis=0, keepdims=True)
        db_ref[...] += jnp.sum(d, axis=0, keepdims=True)
        dbg_ref[...] += jnp.sum(dgl, axis=0, keepdims=True)

    vec = jax.ShapeDtypeStruct((1, D), F32)
    return _pcall(body, name=name, grid=(t // tm,),
                  in_specs=[_rows(tm, D), _rows(tm, D), _rows(tm, D), _rows(tm, PLE), _layer(l, (D, D)),
                            _fix((PLE, D)), _fix((1, D)), _fix((1, D))],
                  out_specs=[_rows(tm, D), _rows(tm, D), _rows(tm, D), _fix((1, D)), _fix((1, D)), _fix((1, D))],
                  out_shape=[jax.ShapeDtypeStruct((t, D), F32), jax.ShapeDtypeStruct((t, D), BF16),
                             jax.ShapeDtypeStruct((t, D), BF16), vec, vec, vec],
                  compiler_params=_cp("arbitrary"))(dy, r, xb, pb, wgate, wproj, bgate, g)


def loss_head(y, tgt, *, name):
    t = y.shape[0]
    tm = min(256, t)

    def body(y_ref, t_ref, dy_ref, sq_ref):
        @pl.when(pl.program_id(0) == 0)
        def _():
            sq_ref[...] = jnp.zeros_like(sq_ref)

        e = y_ref[...] - t_ref[...]
        dy_ref[...] = e / float(D)
        sq_ref[...] += jnp.sum(e * e, axis=0, keepdims=True)

    return _pcall(body, name=name, grid=(t // tm,),
                  in_specs=[_rows(tm, D), _rows(tm, D)],
                  out_specs=[_rows(tm, D), _fix((1, D))],
                  out_shape=[jax.ShapeDtypeStruct((t, D), F32), jax.ShapeDtypeStruct((1, D), F32)],
                  compiler_params=_cp("arbitrary"))(y, tgt)


def _lru_gates(xc, wa_ref, wx_ref, ba_ref, bx_ref, lam_ref):
    xcb = xc.astype(BF16)
    r = _sigmoid(_dot(xcb, wa_ref[...]) + ba_ref[...])
    ig = _sigmoid(_dot(xcb, wx_ref[...]) + bx_ref[...])
    sp = _softplus(-lam_ref[...])
    la = -LRU_C * r * sp
    a = jnp.exp(la)
    mult = jnp.sqrt(-_expm1(2.0 * la))
    return r, ig, sp, la, a, mult


def lru_fwd(z, cw, cb, wa, wx, ba, bx, lam, *, name):
    t = z.shape[0]
    tm = min(512, t)
    hb = tm // 8

    def body(ax_ref, prev_ref, ay_ref, cw_ref, cb_ref, wa_ref, wx_ref, ba_ref, bx_ref, lam_ref,
             xc_ref, xcb_ref, h_ref, ya_ref, xs, a_s, b_s, hc):
        i = pl.program_id(0)

        @pl.when(i == 0)
        def _():
            hc[...] = jnp.zeros_like(hc)

        xs[0:8, :] = jnp.where(i == 0, 0.0, prev_ref[...])
        xs[8:, :] = ax_ref[...]
        xc = cb_ref[...] + cw_ref[0:1, :] * xs[5:5 + tm, :]
        for k in range(1, 4):
            xc = xc + cw_ref[k:k + 1, :] * xs[5 + k:5 + k + tm, :]
        r, ig, sp, la, a, mult = _lru_gates(xc, wa_ref, wx_ref, ba_ref, bx_ref, lam_ref)
        a_s[...] = a
        b_s[...] = mult * (ig * xc)
        xc_ref[...] = xc
        xcb_ref[...] = xc.astype(BF16)

        def step(g, h):
            base = pl.multiple_of(g * 8, 8)
            a8 = a_s[pl.ds(base, 8), :]
            b8 = b_s[pl.ds(base, 8), :]
            for j in range(8):
                h = a8[j:j + 1, :] * h + b8[j:j + 1, :]
                h_ref[pl.ds(base + j, 1), :] = h
            return h

        hc[...] = lax.fori_loop(0, tm // 8, step, hc[...])
        ya_ref[...] = (_gelu(ay_ref[...]) * h_ref[...]).astype(BF16)

    vec = _fix((1, BW))
    return _pcall(body, name=name, grid=(t // tm,),
                  in_specs=[_rows(tm, BW, AX // BW),
                            pl.BlockSpec((8, BW), lambda i: (jnp.maximum(i * hb - 1, 0), AX // BW)),
                            _rows(tm, BW, AY // BW), _fix((4, BW)), vec, _fix((BW, BW)), _fix((BW, BW)),
                            vec, vec, vec],
                  out_specs=[_rows(tm, BW)] * 4,
                  out_shape=[jax.ShapeDtypeStruct((t, BW), F32), jax.ShapeDtypeStruct((t, BW), BF16),
                             jax.ShapeDtypeStruct((t, BW), F32), jax.ShapeDtypeStruct((t, BW), BF16)],
                  scratch_shapes=[pltpu.VMEM((tm + 8, BW), F32), pltpu.VMEM((tm, BW), F32),
                                  pltpu.VMEM((tm, BW), F32), pltpu.VMEM((1, BW), F32)],
                  compiler_params=_cp("arbitrary"))(z, z, z, cw, cb, wa, wx, ba, bx, lam)


def lru_bwd(dya, z, h, xc, wa, wx, ba, bx, lam, *, name):
    t = dya.shape[0]
    tm = min(512, t)
    nb = t // tm
    hb = tm // 8

    def body(dya_ref, ay_ref, h_ref, hprev_ref, xc_ref, wa_ref, wx_ref, ba_ref, bx_ref,
             lam_ref, day_ref, dxc_ref, dpr_ref, dpi_ref, dba_ref, dbx_ref, dlam_ref,
             hs, a_s, g_s, d_s, cc):
        i = pl.program_id(0)

        @pl.when(i == 0)
        def _():
            cc[...] = jnp.zeros_like(cc)
            dba_ref[...] = jnp.zeros_like(dba_ref)
            dbx_ref[...] = jnp.zeros_like(dbx_ref)
            dlam_ref[...] = jnp.zeros_like(dlam_ref)

        xc = xc_ref[...]
        r, ig, sp, la, a, mult = _lru_gates(xc, wa_ref, wx_ref, ba_ref, bx_ref, lam_ref)
        ay = ay_ref[...]
        dya = dya_ref[...]
        hcur = h_ref[...]
        day_ref[...] = (dya * hcur * _gelu_grad(ay)).astype(BF16)
        a_s[...] = a
        g_s[...] = dya * _gelu(ay)

        def step(gg, cin):
            g = tm // 8 - 1 - gg
            base = pl.multiple_of(g * 8, 8)
            a8 = a_s[pl.ds(base, 8), :]
            g8 = g_s[pl.ds(base, 8), :]
            for j in range(7, -1, -1):
                d = g8[j:j + 1, :] + cin
                d_s[pl.ds(base + j, 1), :] = d
                cin = a8[j:j + 1, :] * d
            return cin

        cc[...] = lax.fori_loop(0, tm // 8, step, cc[...])
        dht = d_s[...]
        hs[0:8, :] = jnp.where(i == nb - 1, 0.0, hprev_ref[...])
        hs[8:, :] = hcur
        da = dht * hs[7:7 + tm, :]
        dmult = dht * ig * xc
        dig = dht * mult * xc
        dla = da * a - dmult * a * a / mult
        dpr = dla * (-LRU_C * sp) * r * (1.0 - r)
        dpi = dig * ig * (1.0 - ig)
        dprb = dpr.astype(BF16)
        dpib = dpi.astype(BF16)
        dxc_ref[...] = dht * mult * ig + _dot_nt(dprb, wa_ref[...]) + _dot_nt(dpib, wx_ref[...])
        dpr_ref[...] = dprb
        dpi_ref[...] = dpib
        dba_ref[...] += jnp.sum(dpr, axis=0, keepdims=True)
        dbx_ref[...] += jnp.sum(dpi, axis=0, keepdims=True)
        dlam_ref[...] += jnp.sum(dla * (-LRU_C * r), axis=0, keepdims=True) * (-_sigmoid(-lam_ref[...]))

    vec = _fix((1, BW))
    mat = _fix((BW, BW))
    rev = lambda col: pl.BlockSpec((tm, BW), lambda i: (nb - 1 - i, col))
    vshape = jax.ShapeDtypeStruct((1, BW), F32)
    return _pcall(body, name=name, grid=(nb,),
                  in_specs=[rev(0), rev(AY // BW), rev(0),
                            pl.BlockSpec((8, BW), lambda i: (jnp.maximum((nb - 1 - i) * hb - 1, 0), 0)),
                            rev(0), mat, mat, vec, vec, vec],
                  out_specs=[rev(0), rev(0), rev(0), rev(0), vec, vec, vec],
                  out_shape=[jax.ShapeDtypeStruct((t, BW), BF16), jax.ShapeDtypeStruct((t, BW), F32),
                             jax.ShapeDtypeStruct((t, BW), BF16), jax.ShapeDtypeStruct((t, BW), BF16),
                             vshape, vshape, vshape],
                  scratch_shapes=[pltpu.VMEM((tm + 8, BW), F32), pltpu.VMEM((tm, BW), F32),
                                  pltpu.VMEM((tm, BW), F32), pltpu.VMEM((tm, BW), F32),
                                  pltpu.VMEM((1, BW), F32)],
                  compiler_params=_cp("arbitrary"))(dya, z, h, h, xc, wa, wx, ba, bx, lam)


def conv_bwd(dxc, z, cw, *, name):
    t = dxc.shape[0]
    tm = min(512, t)
    nb = t // tm
    hb = tm // 8

    def body(d_ref, dnext_ref, ax_ref, prev_ref, cw_ref, dax_ref, dcw_ref, dcb_ref, ds, xs):
        i = pl.program_id(0)

        @pl.when(i == 0)
        def _():
            dcw_ref[...] = jnp.zeros_like(dcw_ref)
            dcb_ref[...] = jnp.zeros_like(dcb_ref)

        d = d_ref[...]
        ds[0:tm, :] = d
        ds[tm:, :] = jnp.where(i == nb - 1, 0.0, dnext_ref[...])
        xs[0:8, :] = jnp.where(i == 0, 0.0, prev_ref[...])
        xs[8:, :] = ax_ref[...]
        dax = cw_ref[3:4, :] * d
        for k in range(3):
            dax = dax + cw_ref[k:k + 1, :] * ds[3 - k:3 - k + tm, :]
        dax_ref[...] = dax.astype(BF16)
        for k in range(4):
            dcw_ref[k:k + 1, :] += jnp.sum(d * xs[5 + k:5 + k + tm, :], axis=0, keepdims=True)
        dcb_ref[...] += jnp.sum(d, axis=0, keepdims=True)

    return _pcall(body, name=name, grid=(nb,),
                  in_specs=[_rows(tm, BW),
                            pl.BlockSpec((8, BW), lambda i: (jnp.minimum((i + 1) * hb, nb * hb - 1), 0)),
                            _rows(tm, BW, AX // BW),
                            pl.BlockSpec((8, BW), lambda i: (jnp.maximum(i * hb - 1, 0), AX // BW)),
                            _fix((4, BW))],
                  out_specs=[_rows(tm, BW), _fix((4, BW)), _fix((1, BW))],
                  out_shape=[jax.ShapeDtypeStruct((t, BW), BF16), jax.ShapeDtypeStruct((4, BW), F32),
                             jax.ShapeDtypeStruct((1, BW), F32)],
                  scratch_shapes=[pltpu.VMEM((tm + 8, BW), F32), pltpu.VMEM((tm + 8, BW), F32)],
                  compiler_params=_cp("arbitrary"))(dxc, dxc, z, z, cw)


GLA_CB = 4


def _gla_consts():
    tri = (jnp.arange(CHUNK)[:, None] >= jnp.arange(CHUNK)[None, :]).astype(F32)
    mask = ((jnp.arange(BW)[:, None] // 128) == (jnp.arange(256)[None, :] // 64)).astype(F32)
    return tri, mask


def gla_fwd(z, zb, wg2p, bg, ng, *, name):
    t = z.shape[0]
    tm = GLA_CB * CHUNK
    nc = t // CHUNK
    tri, mask = _gla_consts()

    def body(q_ref, k_ref, v_ref, misc_ref, br_ref, w_ref, bg_ref, ng_ref, tri_ref, mask_ref,
             yb_ref, st_ref, st):
        @pl.when(pl.program_id(0) == 0)
        def _():
            st[...] = jnp.zeros_like(st)

        for c in range(GLA_CB):
            rows = slice(c * CHUNK, (c + 1) * CHUNK)
            pre = _dot(misc_ref[rows, :], w_ref[...]) + bg_ref[...]
            la = _log_sigmoid(pre) / GLA_TAU
            gc = _dot_hi(tri_ref[...], la)
            gt = gc[CHUNK - 1:CHUNK, :]
            kdec = k_ref[rows, :] * jnp.exp(gt - gc)
            delta = _dot_tn(v_ref[rows, :], kdec.astype(BF16))
            s_new = st[...] * jnp.exp(gt) + delta * mask_ref[...]
            st[...] = s_new
            st_ref[c] = s_new
            o = _dot_nt(q_ref[rows, :], s_new.astype(BF16)) * (64.0 ** -0.5)
            br = br_ref[rows, :]
            for hd in range(4):
                cols = slice(hd * 128, (hd + 1) * 128)
                oh = o[:, cols]
                rs = lax.rsqrt(jnp.mean(oh * oh, axis=-1, keepdims=True) + RMS_EPS)
                brh = br[:, cols]
                yb_ref[rows, cols] = (oh * rs * ng_ref[:, cols] * (brh * _sigmoid(brh))).astype(BF16)

    return _pcall(body, name=name, grid=(t // tm,),
                  in_specs=[_rows(tm, 256, BQ // 256), _rows(tm, 256, BK // 256), _rows(tm, BW, BV // BW),
                            _rows(tm, 128, MISC // 128), _rows(tm, BW, BR // BW), _fix((128, 256)),
                            _fix((1, 256)), _fix((1, BW)), _fix((CHUNK, CHUNK)), _fix((BW, 256))],
                  out_specs=[_rows(tm, BW), pl.BlockSpec((GLA_CB, BW, 256), lambda i: (i, 0, 0))],
                  out_shape=[jax.ShapeDtypeStruct((t, BW), BF16), jax.ShapeDtypeStruct((nc, BW, 256), F32)],
                  scratch_shapes=[pltpu.VMEM((BW, 256), F32)],
                  compiler_params=_cp("arbitrary"))(zb, z, zb, zb, z, wg2p, bg, ng, tri, mask)


def gla_bwd(dyb, z, zb, states, wg2p, bg, ng, *, name):
    t = z.shape[0]
    tm = GLA_CB * CHUNK
    nb = t // tm
    tri, mask = _gla_consts()
    triu = tri.T

    def body(dy_ref, q_ref, k_ref, v_ref, misc_ref, br_ref, st_ref, sp_ref, w_ref, bg_ref, ng_ref,
             tri_ref, triu_ref, mask_ref,
             dq_ref, dk_ref, dv_ref, dbr_ref, dmisc_ref, dpre_ref, dbg_ref, dng_ref, cc):
        i = pl.program_id(0)

        @pl.when(i == 0)
        def _():
            cc[...] = jnp.zeros_like(cc)
            dbg_ref[...] = jnp.zeros_like(dbg_ref)
            dng_ref[...] = jnp.zeros_like(dng_ref)

        last_row = lax.broadcasted_iota(jnp.int32, (CHUNK, 256), 0) == CHUNK - 1
        for c in range(GLA_CB - 1, -1, -1):
            rows = slice(c * CHUNK, (c + 1) * CHUNK)
            pre = _dot(misc_ref[rows, :], w_ref[...]) + bg_ref[...]
            la = _log_sigmoid(pre) / GLA_TAU
            gc = _dot_hi(tri_ref[...], la)
            gt = gc[CHUNK - 1:CHUNK, :]
            eg = jnp.exp(gt - gc)
            kdec = k_ref[rows, :] * eg
            e = jnp.exp(gt)
            s_n = st_ref[c]
            if c > 0:
                s_prev = st_ref[c - 1]
            else:
                s_prev = jnp.where(i == nb - 1, 0.0, sp_ref[0])
            sb = s_n.astype(BF16)
            qb = q_ref[rows, :]
            o = _dot_nt(qb, sb) * (64.0 ** -0.5)
            br = br_ref[rows, :]
            dy = dy_ref[rows, :]
            do_parts = []
            for hd in range(4):
                cols = slice(hd * 128, (hd + 1) * 128)
                oh = o[:, cols]
                rs = lax.rsqrt(jnp.mean(oh * oh, axis=-1, keepdims=True) + RMS_EPS)
                ohat = oh * rs
                brh = br[:, cols]
                sg = _sigmoid(brh)
                dyh = dy[:, cols]
                ngh = ng_ref[:, cols]
                don = dyh * (brh * sg)
                dbr_ref[rows, cols] = (dyh * (ohat * ngh) * sg * (1.0 + brh * (1.0 - sg))).astype(BF16)
                dng_ref[:, cols] += jnp.sum(don * ohat, axis=0, keepdims=True)
                doh = don * ngh
                do_parts.append(rs * (doh - ohat * jnp.mean(doh * ohat, axis=-1, keepdims=True)))
            dob = jnp.concatenate(do_parts, axis=1).astype(BF16)
            dq_ref[rows, :] = (_dot(dob, sb) * (64.0 ** -0.5)).astype(BF16)
            dst = cc[...] + _dot_tn(dob, qb) * (64.0 ** -0.5) * mask_ref[...]
            dsb = dst.astype(BF16)
            dkdec = _dot(v_ref[rows, :], dsb)
            dv_ref[rows, :] = _dot_nt(kdec.astype(BF16), dsb).astype(BF16)
            dgt = jnp.sum(dst * s_prev, axis=0, keepdims=True) * e
            dk_ref[rows, :] = (dkdec * eg).astype(BF16)
            dd = dkdec * kdec
            dgt = dgt + jnp.sum(dd, axis=0, keepdims=True)
            dgc = jnp.where(last_row, dgt - dd, -dd)
            dla = _dot_hi(triu_ref[...], dgc)
            dpre = dla * (1.0 / GLA_TAU) * _sigmoid(-pre)
            dpb = dpre.astype(BF16)
            dpre_ref[rows, :] = dpb
            dmisc_ref[rows, :] = _dot_nt(dpb, w_ref[...])
            dbg_ref[...] += jnp.sum(dpre, axis=0, keepdims=True)
            cc[...] = dst * e

    rev = lambda w, col: pl.BlockSpec((tm, w), lambda i: (nb - 1 - i, col))
    return _pcall(body, name=name, grid=(nb,),
                  in_specs=[rev(BW, 0), rev(256, BQ // 256), rev(256, BK // 256), rev(BW, BV // BW),
                            rev(128, MISC // 128), rev(BW, BR // BW),
                            pl.BlockSpec((GLA_CB, BW, 256), lambda i: (nb - 1 - i, 0, 0)),
                            pl.BlockSpec((1, BW, 256), lambda i: (jnp.maximum((nb - 1 - i) * GLA_CB - 1, 0), 0, 0)),
                            _fix((128, 256)), _fix((1, 256)), _fix((1, BW)),
                            _fix((CHUNK, CHUNK)), _fix((CHUNK, CHUNK)), _fix((BW, 256))],
                  out_specs=[rev(256, 0), rev(256, 0), rev(BW, 0), rev(BW, 0), rev(128, 0), rev(256, 0),
                             _fix((1, 256)), _fix((1, BW))],
                  out_shape=[jax.ShapeDtypeStruct((t, 256), BF16), jax.ShapeDtypeStruct((t, 256), BF16),
                             jax.ShapeDtypeStruct((t, BW), BF16), jax.ShapeDtypeStruct((t, BW), BF16),
                             jax.ShapeDtypeStruct((t, 128), F32), jax.ShapeDtypeStruct((t, 256), BF16),
                             jax.ShapeDtypeStruct((1, 256), F32), jax.ShapeDtypeStruct((1, BW), F32)],
                  scratch_shapes=[pltpu.VMEM((BW, 256), F32)],
                  compiler_params=_cp("arbitrary"))(dyb, zb, z, zb, zb, z, states, states, wg2p, bg, ng,
                                                    tri, triu, mask)


FOX_SCALE = 64.0 ** -0.5
NEG = -1e30


def fox_fcum(z, bfp, *, name):
    t = z.shape[0]
    tm = min(256, t)
    tri = (jnp.arange(tm)[:, None] >= jnp.arange(tm)[None, :]).astype(F32)

    def body(m_ref, b_ref, tri_ref, o_ref, cc):
        @pl.when(pl.program_id(0) == 0)
        def _():
            cc[...] = jnp.zeros_like(cc)

        lf = _log_sigmoid(m_ref[...] + b_ref[...])
        cs = _dot_hi(tri_ref[...], lf) + cc[...]
        o_ref[...] = cs
        cc[...] = cs[tm - 1:tm, :]

    return _pcall(body, name=name, grid=(t // tm,),
                  in_specs=[_rows(tm, 128, MISC // 128), _fix((1, 128)), _fix((tm, tm))],
                  out_specs=_rows(tm, 128), out_shape=jax.ShapeDtypeStruct((t, 128), F32),
                  scratch_shapes=[pltpu.VMEM((1, 128), F32)],
                  compiler_params=_cp("arbitrary"))(z, bfp, tri)


def fox_dcf(dfc, z, bfp, dmisc_g, *, name):
    t = z.shape[0]
    tm = min(256, t)
    nb = t // tm
    triu = (jnp.arange(tm)[:, None] <= jnp.arange(tm)[None, :]).astype(F32)

    def body(d_ref, m_ref, b_ref, g_ref, tri_ref, o_ref, dbf_ref, cc):
        @pl.when(pl.program_id(0) == 0)
        def _():
            cc[...] = jnp.zeros_like(cc)
            dbf_ref[...] = jnp.zeros_like(dbf_ref)

        rc = _dot_hi(tri_ref[...], d_ref[...]) + cc[...]
        cc[...] = rc[0:1, :]
        dcf = rc * _sigmoid(-(m_ref[...] + b_ref[...]))
        o_ref[...] = (dcf + g_ref[...]).astype(BF16)
        dbf_ref[...] += jnp.sum(dcf, axis=0, keepdims=True)

    rev = lambda col: pl.BlockSpec((tm, 128), lambda i: (nb - 1 - i, col))
    return _pcall(body, name=name, grid=(nb,),
                  in_specs=[rev(0), rev(MISC // 128), _fix((1, 128)), rev(0), _fix((tm, tm))],
                  out_specs=[rev(0), _fix((1, 128))],
                  out_shape=[jax.ShapeDtypeStruct((t, 128), BF16), jax.ShapeDtypeStruct((1, 128), F32)],
                  scratch_shapes=[pltpu.VMEM((1, 128), F32)],
                  compiler_params=_cp("arbitrary"))(dfc, z, bfp, dmisc_g, triu)


def fox_delta(dyc, ycf, *, name):
    t = dyc.shape[0]
    tm = min(256, t)
    seg = ((jnp.arange(BW)[:, None] // 64) == jnp.arange(128)[None, :]).astype(F32)

    def body(d_ref, o_ref, s_ref, out_ref):
        out_ref[...] = _dot_hi(d_ref[...] * o_ref[...], s_ref[...])

    return _pcall(body, name=name, grid=(t // tm,),
                  in_specs=[_rows(tm, BW), _rows(tm, BW), _fix((BW, 128))],
                  out_specs=_rows(tm, 128), out_shape=jax.ShapeDtypeStruct((t, 128), F32),
                  compiler_params=_cp("parallel"))(dyc, ycf, seg)


def fox_fwd_t(zb, frow, fkb, *, name):
    t = zb.shape[0]
    tq = min(512, t)
    nq = t // tq
    rep = tq // 128

    pairs = [(i, j) for i in range(nq) for j in range(i + 1)]
    qi_tab = jnp.asarray([p[0] for p in pairs], jnp.int32)
    kj_tab = jnp.asarray([p[1] for p in pairs], jnp.int32)

    def body(qi_ref, kj_ref, q_ref, k_ref, v_ref, fq_ref, fk_ref, y_ref, yf_ref, lse_ref, m_s, l_s, acc):
        step = pl.program_id(1)
        i, j = qi_ref[step], kj_ref[step]

        @pl.when(j == 0)
        def _():
            m_s[...] = jnp.full_like(m_s, NEG)
            l_s[...] = jnp.zeros_like(l_s)
            acc[...] = jnp.zeros_like(acc)

        lo = lax.broadcasted_iota(jnp.int32, (tq, 128), 1) < 64

        def work(diagonal):
            q = q_ref[...]
            k = k_ref[...]
            v = v_ref[...]
            if diagonal:
                key = lax.broadcasted_iota(jnp.int32, (tq, tq), 0)
                qry = lax.broadcasted_iota(jnp.int32, (tq, tq), 1)
                keep = key <= qry
            for hh in range(2):
                sel = lo if hh == 0 else jnp.logical_not(lo)
                qh = jnp.where(sel, q, jnp.zeros_like(q))
                s = _dot_nt(k, qh) + fq_ref[hh] - jnp.tile(fk_ref[hh], (1, rep))
                if diagonal:
                    s = jnp.where(keep, s, NEG)
                m_old = m_s[hh]
                m_new = jnp.maximum(m_old, jnp.max(s, axis=0, keepdims=True))
                p = jnp.exp(s - m_new)
                corr = jnp.exp(m_old - m_new)
                l_s[hh] = l_s[hh] * corr + jnp.sum(p, axis=0, keepdims=True)
                m_s[hh] = m_new
                pv = _dot_tn(v, p.astype(BF16))
                rows = slice(64 * hh, 64 * hh + 64)
                acc[rows, :] = acc[rows, :] * corr + pv[rows, :]

        @pl.when(j < i)
        def _():
            work(False)

        @pl.when(j == i)
        def _():
            work(True)
            first = lax.broadcasted_iota(jnp.int32, (128, tq), 0) < 64
            out = (acc[...] * jnp.where(first, 1.0 / l_s[0], 1.0 / l_s[1])).T
            y_ref[...] = out.astype(BF16)
            yf_ref[...] = out
            lse_ref[...] = m_s[...] + jnp.log(l_s[...])

    kv = lambda off: pl.BlockSpec((tq, 128), lambda h, s, qi, kj: (kj[s], off // 128 + h))
    gs = pltpu.PrefetchScalarGridSpec(
        num_scalar_prefetch=2, grid=(4, len(pairs)),
        in_specs=[pl.BlockSpec((tq, 128), lambda h, s, qi, kj: (qi[s], CQ // 128 + h)), kv(CK), kv(CV),
                  pl.BlockSpec((2, 1, tq), lambda h, s, qi, kj: (h, 0, qi[s])),
                  pl.BlockSpec((2, tq, 128), lambda h, s, qi, kj: (h, kj[s], 0))],
        out_specs=[pl.BlockSpec((tq, 128), lambda h, s, qi, kj: (qi[s], h)),
                   pl.BlockSpec((tq, 128), lambda h, s, qi, kj: (qi[s], h)),
                   pl.BlockSpec((2, 1, tq), lambda h, s, qi, kj: (h, 0, qi[s]))],
        scratch_shapes=[pltpu.VMEM((2, 1, tq), F32), pltpu.VMEM((2, 1, tq), F32), pltpu.VMEM((128, tq), F32)])
    return _pcall(body, name=name, grid_spec=gs,
                  out_shape=[jax.ShapeDtypeStruct((t, BW), BF16), jax.ShapeDtypeStruct((t, BW), F32),
                             jax.ShapeDtypeStruct((FOX_H, 1, t), F32)],
                  compiler_params=_cp("parallel", "arbitrary"))(qi_tab, kj_tab, zb, zb, zb, frow, fkb)


def fox_bwd_t(zb, dyc, frow, fkb, lse, dl, *, name):
    t = zb.shape[0]
    tq = min(512, t)
    nq = t // tq
    rep = tq // 128

    pairs = [(j, i) for j in range(nq) for i in range(j, nq)]
    kj_tab = jnp.asarray([p[0] for p in pairs], jnp.int32)
    qi_tab = jnp.asarray([p[1] for p in pairs], jnp.int32)

    def body(kj_ref, qi_ref, q_ref, k_ref, v_ref, do_ref, fq_ref, fk_ref, lse_ref, dl_ref,
             dq_ref, dk_ref, dv_ref, dfk_ref, dfq_ref, dk_s, dv_s, df_s, dq_s):
        step = pl.program_id(1)
        j, i = kj_ref[step], qi_ref[step]

        @pl.when(step == 0)
        def _():
            dq_s[...] = jnp.zeros_like(dq_s)
            dfq_ref[...] = jnp.zeros_like(dfq_ref)

        @pl.when(i == j)
        def _():
            dk_s[...] = jnp.zeros_like(dk_s)
            dv_s[...] = jnp.zeros_like(dv_s)
            df_s[...] = jnp.zeros_like(df_s)

        lo = lax.broadcasted_iota(jnp.int32, (tq, 128), 1) < 64

        def work(diagonal):
            q = q_ref[...]
            k = k_ref[...]
            v = v_ref[...]
            dob = do_ref[...].astype(BF16)
            if diagonal:
                key = lax.broadcasted_iota(jnp.int32, (tq, tq), 0)
                qry = lax.broadcasted_iota(jnp.int32, (tq, tq), 1)
                keep = key <= qry
            dvs, dks = [], []
            for hh in range(2):
                sel = lo if hh == 0 else jnp.logical_not(lo)
                qh = jnp.where(sel, q, jnp.zeros_like(q))
                doh = jnp.where(sel, dob, jnp.zeros_like(dob))
                p = jnp.exp(_dot_nt(k, qh) + (fq_ref[hh] - lse_ref[hh]) - jnp.tile(fk_ref[hh], (1, rep)))
                if diagonal:
                    p = jnp.where(keep, p, 0.0)
                ds = p * (_dot_nt(v, doh) - dl_ref[hh])
                dsb = ds.astype(BF16)
                dvs.append(_dot(p.astype(BF16), dob))
                dks.append(_dot(dsb, q))
                rows = slice(64 * hh, 64 * hh + 64)
                dq_s[i, rows, :] += _dot_tn(k, dsb)[rows, :]
                part = ds[:, 0:128]
                for r in range(1, rep):
                    part = part + ds[:, 128 * r:128 * (r + 1)]
                df_s[hh] += part
                dfq_ref[hh, i] += jnp.sum(ds, axis=0, keepdims=True)
            dv_s[...] += jnp.where(lo, dvs[0], dvs[1])
            dk_s[...] += jnp.where(lo, dks[0], dks[1])

        @pl.when(i > j)
        def _():
            work(False)

        @pl.when(i == j)
        def _():
            work(True)
            dq_ref[...] = dq_s[i].T.astype(BF16)

        @pl.when(i == nq - 1)
        def _():
            dk_ref[...] = dk_s[...].astype(BF16)
            dv_ref[...] = dv_s[...].astype(BF16)
            for hh in range(2):
                dfk_ref[hh] = -jnp.sum(df_s[hh].T, axis=0, keepdims=True)

    row = lambda: pl.BlockSpec((2, 1, tq), lambda h, s, kj, qi: (h, 0, qi[s]))
    gs = pltpu.PrefetchScalarGridSpec(
        num_scalar_prefetch=2, grid=(4, len(pairs)),
        in_specs=[pl.BlockSpec((tq, 128), lambda h, s, kj, qi: (qi[s], CQ // 128 + h)),
                  pl.BlockSpec((tq, 128), lambda h, s, kj, qi: (kj[s], CK // 128 + h)),
                  pl.BlockSpec((tq, 128), lambda h, s, kj, qi: (kj[s], CV // 128 + h)),
                  pl.BlockSpec((tq, 128), lambda h, s, kj, qi: (qi[s], h)),
                  row(), pl.BlockSpec((2, tq, 128), lambda h, s, kj, qi: (h, kj[s], 0)), row(), row()],
        out_specs=[pl.BlockSpec((tq, 128), lambda h, s, kj, qi: (kj[s], h)),
                   pl.BlockSpec((tq, 128), lambda h, s, kj, qi: (kj[s], h)),
                   pl.BlockSpec((tq, 128), lambda h, s, kj, qi: (kj[s], h)),
                   pl.BlockSpec((2, 1, tq), lambda h, s, kj, qi: (h, 0, kj[s])),
                   pl.BlockSpec((2, nq, 1, tq), lambda h, s, kj, qi: (h, 0, 0, 0))],
        scratch_shapes=[pltpu.VMEM((tq, 128), F32), pltpu.VMEM((tq, 128), F32), pltpu.VMEM((2, tq, 128), F32),
                        pltpu.VMEM((nq, 128, tq), F32)])
    return _pcall(body, name=name, grid_spec=gs,
                  out_shape=[jax.ShapeDtypeStruct((t, BW), BF16), jax.ShapeDtypeStruct((t, BW), BF16),
                             jax.ShapeDtypeStruct((t, BW), BF16), jax.ShapeDtypeStruct((FOX_H, 1, t), F32),
                             jax.ShapeDtypeStruct((FOX_H, nq, 1, tq), F32)],
                  compiler_params=_cp("parallel", "arbitrary"))(kj_tab, qi_tab, zb, zb, zb, dyc, frow, fkb, lse, dl)


def merge_fwd(ya, yb, yc, wbr, z, *, name):
    t = ya.shape[0]
    tm = min(512, t)

    def body(ya_ref, yb_ref, yc_ref, w_ref, g0_ref, g1_ref, g2_ref, o_ref):
        m = _sigmoid(g0_ref[...]) * _dot(ya_ref[...], w_ref[0])
        m = m + _sigmoid(g1_ref[...]) * _dot(yb_ref[...], w_ref[1])
        m = m + _sigmoid(g2_ref[...]) * _dot(yc_ref[...], w_ref[2])
        o_ref[...] = m.astype(BF16)

    return _pcall(body, name=name, grid=(t // tm,),
                  in_specs=[_rows(tm, BW)] * 3 + [_fix((3, BW, D))]
                  + [_rows(tm, D, G0 // D + j) for j in range(3)],
                  out_specs=_rows(tm, D), out_shape=jax.ShapeDtypeStruct((t, D), BF16),
                  compiler_params=_cp("parallel"))(ya, yb, yc, wbr, z, z, z)


def merge_bwd(doutb, wo, l, ya, yb, yc, wbr, z, *, name):
    t = ya.shape[0]
    tm = min(256, t)

    def body(do_ref, wo_ref, ya_ref, yb_ref, yc_ref, w_ref, g0_ref, g1_ref, g2_ref,
             dya_ref, dyb_ref, dyc_ref, dp0_ref, dp1_ref, dp2_ref, dg0_ref, dg1_ref, dg2_ref):
        dm = _dot_nt(do_ref[...], wo_ref[...])
        ys = (ya_ref, yb_ref, yc_ref)
        gs = (g0_ref, g1_ref, g2_ref)
        dys = (dya_ref, dyb_ref, dyc_ref)
        dps = (dp0_ref, dp1_ref, dp2_ref)
        dgs = (dg0_ref, dg1_ref, dg2_ref)
        for j in range(3):
            s = _sigmoid(gs[j][...])
            pj = _dot(ys[j][...], w_ref[j])
            dpb = (dm * s).astype(BF16)
            dps[j][...] = dpb
            dgs[j][...] = (dm * pj * s * (1.0 - s)).astype(BF16)
            dys[j][...] = _dot_nt(dpb, w_ref[j])

    yshape = jax.ShapeDtypeStruct((t, BW), F32)
    dshape = jax.ShapeDtypeStruct((t, D), BF16)
    return _pcall(body, name=name, grid=(t // tm,),
                  in_specs=[_rows(tm, D), _layer(l, (D, D))] + [_rows(tm, BW)] * 3
                  + [_fix((3, BW, D))] + [_rows(tm, D, G0 // D + j) for j in range(3)],
                  out_specs=[_rows(tm, BW)] * 3 + [_rows(tm, D)] * 6,
                  out_shape=[yshape] * 3 + [dshape] * 6,
                  compiler_params=_cp("parallel"))(doutb, wo, ya, yb, yc, wbr, z, z, z)


def adamw(w, g, m, v, *, name):
    nl, r, c = w.shape
    tm = _row_tile(r)

    def body(w_ref, g_ref, m_ref, v_ref, d_ref, mo_ref, vo_ref):
        gg = g_ref[...]
        mn = ADAM_B1 * m_ref[...] + (1.0 - ADAM_B1) * gg
        vn = ADAM_B2 * v_ref[...] + (1.0 - ADAM_B2) * (gg * gg)
        m_hat = mn / (1.0 - ADAM_B1 ** ADAM_STEP)
        v_hat = vn / (1.0 - ADAM_B2 ** ADAM_STEP)
        d_ref[...] = -ADAM_LR * (m_hat / (jnp.sqrt(v_hat) + ADAM_EPS) + ADAM_WD * w_ref[...])
        mo_ref[...] = mn
        vo_ref[...] = vn

    shp = jax.ShapeDtypeStruct((nl, r, c), F32)
    blk = pl.BlockSpec((None, tm, c), lambda l, i: (l, i, 0))
    return _pcall(body, name=name, grid=(nl, r // tm), in_specs=[blk] * 4, out_specs=[blk] * 3,
                  out_shape=[shp] * 3, compiler_params=_cp("parallel", "parallel"))(w, g, m, v)


def _place():
    return lax.axis_index("x"), lax.axis_index("y"), lax.axis_index("c")


def _remote(src, dst, send_sems, recv_sems, k, to):
    return pltpu.make_async_remote_copy(src_ref=src, dst_ref=dst, send_sem=send_sems.at[k],
                                        recv_sem=recv_sems.at[k], device_id=to, device_id_type=MESH)


HBM = pl.BlockSpec(memory_space=pltpu.HBM)
SEM = pl.BlockSpec(memory_space=pltpu.SEMAPHORE)
EFFECT = pltpu.SideEffectType.DATAFLOW_SIDE_EFFECTING


def gather_first(shards):
    n = len(shards)

    def body(*refs):
        ins, outs, lands = refs[:n], refs[n:2 * n], refs[2 * n:3 * n]
        send_sems, recv_sems, own_send, own_recv = refs[3 * n:]
        x, y, c = _place()
        sib = (x, y, 1 - c)
        chips = [(1 - x, y), (x, 1 - y), (1 - x, 1 - y)]
        k_me = 2 * x + y
        mine = []
        for t in range(n):
            mine.append(_remote(ins[t].at[0], outs[t].at[0, k_me], own_send, own_recv, 2 * t, sib))
            mine.append(_remote(ins[t].at[1], lands[t].at[0, k_me], own_send, own_recv, 2 * t + 1, sib))
        for cp in mine:
            cp.start()

        def slot(t, chip):
            return outs[t].at[0, 2 * chip[0] + chip[1]]

        @pl.when(c == 0)
        def _():
            first = [_remote(ins[t].at[0], outs[t].at[0, k_me], send_sems, recv_sems, 6 * t + j, (*chip, 0))
                     for t in range(n) for j, chip in enumerate(chips)]
            for cp in first:
                cp.start()
            passed = []
            for t in range(n):
                for j, chip in enumerate(chips):
                    _remote(slot(t, chip), slot(t, chip), send_sems, recv_sems, 6 * t + j, (*chip, 0)).wait_recv()
                    cp = _remote(slot(t, chip), slot(t, chip), send_sems, recv_sems, 6 * t + 3 + j, sib)
                    cp.start()
                    passed.append(cp)
            for cp in first + passed:
                cp.wait_send()

        @pl.when(c == 1)
        def _():
            for t in range(n):
                for j, chip in enumerate(chips):
                    _remote(slot(t, chip), slot(t, chip), send_sems, recv_sems, 6 * t + 3 + j, sib).wait_recv()

        for cp in mine:
            cp.wait()

    shape = [jax.ShapeDtypeStruct((1, 4) + s.shape[1:], s.dtype) for s in shards]
    out = _pcall(body, name="gather_first", in_specs=[ANY] * n, out_specs=[ANY] * (2 * n), out_shape=shape + shape,
                 scratch_shapes=[pltpu.SemaphoreType.DMA((6 * n,)), pltpu.SemaphoreType.DMA((6 * n,)),
                                 pltpu.SemaphoreType.DMA((2 * n,)), pltpu.SemaphoreType.DMA((2 * n,))])(*shards)
    return out[:n], out[n:]


def _rest_copies(ins, lands, send_sems, recv_sems):
    x, y, c = _place()
    chips = [(1 - x, y), (x, 1 - y), (1 - x, 1 - y)]
    copies, arrivals = [], []
    for t in range(len(ins)):
        for j, chip in enumerate(chips):
            for to in range(2):
                copies.append(pltpu.make_async_remote_copy(
                    src_ref=ins[t].at[1], dst_ref=lands[t].at[0, 2 * x + y], send_sem=send_sems.at[6 * t + 2 * j + to],
                    recv_sem=recv_sems.at[3 * t + j], device_id=(*chip, to), device_id_type=MESH))
            blk = lands[t].at[0, 2 * chip[0] + chip[1]]
            arrivals.append(pltpu.make_async_remote_copy(
                src_ref=blk, dst_ref=blk, send_sem=send_sems.at[6 * t + 2 * j], recv_sem=recv_sems.at[3 * t + j],
                device_id=(*chip, 1), device_id_type=MESH))
    return copies, arrivals


def gather_rest_start(shards, lands):
    n = len(shards)

    def body(*refs):
        ins, lds = refs[:n], refs[n:2 * n]
        send_sems, recv_sems = refs[2 * n], refs[2 * n + 1]
        token = refs[-1]
        copies, _ = _rest_copies(ins, lds, send_sems, recv_sems)

        @pl.when(lax.axis_index("c") == 1)
        def _():
            for cp in copies:
                cp.start()

        token[...] = jnp.zeros_like(token)

    hbm = lambda a: pltpu.with_memory_space_constraint(a, pltpu.HBM)
    out = _pcall(body, name="gather_rest_start", in_specs=[HBM] * (2 * n),
                 out_specs=[SEM, SEM] + [HBM] * (2 * n) + [pl.BlockSpec(memory_space=pltpu.VMEM)],
                 out_shape=[pltpu.SemaphoreType.DMA((6 * n,)), pltpu.SemaphoreType.DMA((3 * n,))]
                 + [pltpu.HBM(a.shape, a.dtype) for a in shards] + [pltpu.HBM(a.shape, a.dtype) for a in lands]
                 + [jax.ShapeDtypeStruct((8, 128), F32)],
                 input_output_aliases={i: 2 + i for i in range(2 * n)},
                 compiler_params=pltpu.CompilerParams(has_side_effects=EFFECT))(
                     *[hbm(a) for a in shards], *[hbm(a) for a in lands])
    return out[0], out[1], out[2:2 + n], out[2 + n:2 + 2 * n], out[-1]


def gather_rest_wait(send_sems, recv_sems, srcs, lands, after):
    n = len(srcs)

    def body(*refs):
        ins, lds = refs[:n], refs[n:2 * n]
        s_sems, r_sems = refs[2 * n], refs[2 * n + 1]
        copies, arrivals = _rest_copies(ins, lds, s_sems, r_sems)

        @pl.when(lax.axis_index("c") == 1)
        def _():
            for cp in copies:
                cp.wait_send()

        for cp in arrivals:
            cp.wait_recv()

    out = _pcall(body, name="gather_rest_wait", in_specs=[HBM] * (2 * n) + [SEM, SEM, ANY],
                 out_specs=[HBM] * (2 * n),
                 out_shape=[pltpu.HBM(a.shape, a.dtype) for a in srcs] + [pltpu.HBM(a.shape, a.dtype) for a in lands],
                 input_output_aliases={i: i for i in range(2 * n)},
                 compiler_params=pltpu.CompilerParams(has_side_effects=EFFECT))(
                     *srcs, *lands, send_sems, recv_sems, after)
    return out[n:]


def pair_send(gl, owner, layer):
    n = len(gl)

    def body(*refs):
        ins, outs = refs[:n], refs[n:2 * n]
        send_sems, recv_sems = refs[2 * n:]
        x, y, c = _place()
        sib = (x, y, 1 - c)
        cps = [_remote(ins[t], outs[t], send_sems, recv_sems, t, sib) for t in range(n)]
        for core in range(2):
            @pl.when(c == core)
            def _():
                for cp in _owned(cps, owner, 1 - core, per=1):
                    cp.start()
                for cp in _owned(cps, owner, 1 - core, per=1):
                    cp.wait_send()
                for cp in _owned(cps, owner, core, per=1):
                    cp.wait_recv()

    return _pcall(body, name="pair_send_l%d" % layer, in_specs=[ANY] * n, out_specs=[ANY] * n,
                  out_shape=[jax.ShapeDtypeStruct(a.shape, a.dtype) for a in gl],
                  scratch_shapes=[pltpu.SemaphoreType.DMA((n,)), pltpu.SemaphoreType.DMA((n,))])(*gl)


def _chip_copies(ins, outs, send_sems, recv_sems):
    x, y, c = _place()
    chips = [(1 - x, y), (x, 1 - y), (1 - x, 1 - y)]
    return [_remote(ins[t].at[2 * chip[0] + chip[1]], outs[t].at[j], send_sems, recv_sems, 3 * t + j, (*chip, c))
            for t in range(len(ins)) for j, chip in enumerate(chips)]


def _owned(cps, owner, core, per=3):
    return [cp for k, cp in enumerate(cps) if owner[k // per] == core]


def chip_send(s1, owner, layer):
    n = len(s1)

    def body(*refs):
        ins, outs = refs[:n], refs[n:2 * n]
        send_sems, recv_sems = refs[2 * n:]
        cps = _chip_copies(ins, outs, send_sems, recv_sems)
        for core in range(2):
            @pl.when(lax.axis_index("c") == core)
            def _():
                for cp in _owned(cps, owner, core):
                    cp.start()
                for cp in _owned(cps, owner, core):
                    cp.wait()

    return _pcall(body, name="chip_send_l%d" % layer, in_specs=[ANY] * n, out_specs=[ANY] * n,
                  out_shape=[jax.ShapeDtypeStruct((3,) + a.shape[1:], a.dtype) for a in s1],
                  scratch_shapes=[pltpu.SemaphoreType.DMA((3 * n,)), pltpu.SemaphoreType.DMA((3 * n,))])(*s1)


def chip_send_start(s1, owner, layer):
    n = len(s1)
    land = [lax.empty((3,) + a.shape[1:], a.dtype) for a in s1]

    def body(*refs):
        ins, lands = refs[:n], refs[n:2 * n]
        send_sems, recv_sems = refs[2 * n], refs[2 * n + 1]
        token = refs[-1]
        cps = _chip_copies(ins, lands, send_sems, recv_sems)
        for core in range(2):
            @pl.when(lax.axis_index("c") == core)
            def _():
                for cp in _owned(cps, owner, core):
                    cp.start()

        token[...] = jnp.zeros_like(token)

    hbm = lambda a: pltpu.with_memory_space_constraint(a, pltpu.HBM)
    out = _pcall(body, name="chip_send_start_l%d" % layer, in_specs=[HBM] * (2 * n),
                 out_specs=[SEM, SEM] + [HBM] * (2 * n) + [pl.BlockSpec(memory_space=pltpu.VMEM)],
                 out_shape=[pltpu.SemaphoreType.DMA((3 * n,)), pltpu.SemaphoreType.DMA((3 * n,))]
                 + [pltpu.HBM(a.shape, a.dtype) for a in s1] + [pltpu.HBM(a.shape, a.dtype) for a in land]
                 + [jax.ShapeDtypeStruct((8, 128), F32)],
                 input_output_aliases={i: 2 + i for i in range(2 * n)},
                 compiler_params=pltpu.CompilerParams(has_side_effects=EFFECT))(
                     *[hbm(a) for a in s1], *[hbm(a) for a in land])
    return out[0], out[1], out[2:2 + n], out[2 + n:2 + 2 * n], out[-1]


def chip_send_wait(send_sems, recv_sems, srcs, lands, after, owner, layer):
    n = len(srcs)

    def body(*refs):
        ins, lds = refs[:n], refs[n:2 * n]
        s_sems, r_sems = refs[2 * n], refs[2 * n + 1]
        cps = _chip_copies(ins, lds, s_sems, r_sems)
        for core in range(2):
            @pl.when(lax.axis_index("c") == core)
            def _():
                for cp in _owned(cps, owner, core):
                    cp.wait_send()
                    cp.wait_recv()

    out = _pcall(body, name="chip_send_wait_l%d" % layer, in_specs=[HBM] * (2 * n) + [SEM, SEM, ANY],
                 out_specs=[HBM] * (2 * n),
                 out_shape=[pltpu.HBM(a.shape, a.dtype) for a in srcs] + [pltpu.HBM(a.shape, a.dtype) for a in lands],
                 input_output_aliases={i: i for i in range(2 * n)},
                 compiler_params=pltpu.CompilerParams(has_side_effects=EFFECT))(
                     *srcs, *lands, send_sems, recv_sems, after)
    return out[n:]


def pair_share(s2, owner):
    n = len(s2)

    def body(*refs):
        ins, outs = refs[:n], refs[n:2 * n]
        send_sems, recv_sems = refs[2 * n:]
        x, y, c = _place()
        sib = (x, y, 1 - c)
        cps = [_remote(ins[t], outs[t], send_sems, recv_sems, t, sib) for t in range(n)]
        for core in range(2):
            @pl.when(c == core)
            def _():
                for cp in _owned(cps, owner, core, per=1):
                    cp.start()
                for cp in _owned(cps, owner, core, per=1):
                    cp.wait_send()
                for cp in _owned(cps, owner, 1 - core, per=1):
                    cp.wait_recv()

    return _pcall(body, name="pair_share", in_specs=[ANY] * n, out_specs=[ANY] * n,
                  out_shape=[jax.ShapeDtypeStruct(a.shape, a.dtype) for a in s2],
                  input_output_aliases={t: t for t in range(n)},
                  scratch_shapes=[pltpu.SemaphoreType.DMA((n,)), pltpu.SemaphoreType.DMA((n,))])(*s2)


def small_exchange(gs):
    rows, width = gs.shape

    def body(g_ref, o_ref, send_sems, recv_sems):
        x, y, c = _place()
        cps = []
        for r in range(1, 8):
            dx, dy, dc = (r >> 2) & 1, (r >> 1) & 1, r & 1
            to = (x if dx == 0 else 1 - x, y if dy == 0 else 1 - y, c if dc == 0 else 1 - c)
            cps.append(_remote(g_ref, o_ref.at[r - 1], send_sems, recv_sems, r - 1, to))
        for cp in cps:
            cp.start()
        for cp in cps:
            cp.wait()

    return _pcall(body, name="small_exchange", in_specs=[ANY], out_specs=ANY,
                  out_shape=jax.ShapeDtypeStruct((7, rows, width), gs.dtype),
                  scratch_shapes=[pltpu.SemaphoreType.DMA((7,)), pltpu.SemaphoreType.DMA((7,))])(gs)


def _row_tile(rows):
    return _pick(rows, (256, 352, 128, 64, 32, 16))


def pair_add_layer(g, rb, core, owner, *, name):
    _, rows, width = g.shape
    tr = _row_tile(rows)

    def body(c_ref, g_ref, r_ref, o_ref, ob_ref):
        @pl.when(c_ref[0] == owner)
        def _():
            s = g_ref[...] + r_ref[...]
            o_ref[...] = s
            ob_ref[...] = s.astype(BF16)

    def at(k, i, c_ref):
        mine = c_ref[0] == owner
        return (jnp.where(mine, k, 0), jnp.where(mine, i, 0), 0)

    blk = pl.BlockSpec((None, tr, width), at)
    gs = pltpu.PrefetchScalarGridSpec(num_scalar_prefetch=1, grid=(4, rows // tr), in_specs=[blk, blk],
                                      out_specs=[blk, blk])
    return _pcall(body, name=name, grid_spec=gs,
                  out_shape=[jax.ShapeDtypeStruct(g.shape, F32), jax.ShapeDtypeStruct(g.shape, BF16)],
                  compiler_params=_cp("parallel", "parallel"))(core, g, rb)


def chip_add_layers(s1, rb2, chip, core, owner, *, name):
    _, rows, width = s1[0].shape
    tr = _row_tile(rows)

    def body(k_ref, c_ref, s0_ref, s1_ref, r0_ref, r1_ref, o_ref):
        @pl.when(c_ref[0] == owner)
        def _():
            first = pl.program_id(0) == 0
            s = jnp.where(first, s0_ref[...], s1_ref[...])
            r = jnp.where(first, r0_ref[...], r1_ref[...]).astype(F32)
            o_ref[...] = ((s + r[0]) + r[1]) + r[2]

    def s_spec(layer):
        def at(l, i, k_ref, c_ref):
            use = jnp.logical_and(l == layer, c_ref[0] == owner)
            return (jnp.where(use, k_ref[0], 0), jnp.where(use, i, 0), 0)
        return pl.BlockSpec((None, tr, width), at)

    def r_spec(layer):
        def at(l, i, k_ref, c_ref):
            return (0, jnp.where(jnp.logical_and(l == layer, c_ref[0] == owner), i, 0), 0)
        return pl.BlockSpec((3, tr, width), at)

    def out_at(l, i, k_ref, c_ref):
        mine = c_ref[0] == owner
        return (jnp.where(mine, l, 0), jnp.where(mine, i, 0), 0)

    gs = pltpu.PrefetchScalarGridSpec(
        num_scalar_prefetch=2, grid=(DEPTH, rows // tr),
        in_specs=[s_spec(0), s_spec(1), r_spec(0), r_spec(1)],
        out_specs=pl.BlockSpec((None, tr, width), out_at))
    return _pcall(body, name=name, grid_spec=gs, out_shape=jax.ShapeDtypeStruct((DEPTH, rows, width), F32),
                  compiler_params=_cp("parallel", "parallel"))(chip, core, s1[0], s1[1], rb2[0], rb2[1])


def small_add(gs_own, slots, me):
    rows, width = gs_own.shape
    tr = _pick(rows, (64, 32, 16, 8))

    def body(me_ref, g_ref, s_ref, o_ref):
        me_v = me_ref[0]
        total = None
        for d in range(8):
            rel = jnp.bitwise_xor(me_v, d)
            val = jnp.where(rel == 0, g_ref[...], s_ref[jnp.maximum(rel - 1, 0)])
            total = val if total is None else total + val
        o_ref[...] = total

    gs = pltpu.PrefetchScalarGridSpec(
        num_scalar_prefetch=1, grid=(rows // tr,),
        in_specs=[pl.BlockSpec((tr, width), lambda i, m_ref: (i, 0)),
                  pl.BlockSpec((7, tr, width), lambda i, m_ref: (0, i, 0))],
        out_specs=pl.BlockSpec((tr, width), lambda i, m_ref: (i, 0)))
    return _pcall(body, name="small_add", grid_spec=gs, out_shape=jax.ShapeDtypeStruct((rows, width), F32),
                  compiler_params=_cp("parallel"))(me, gs_own, slots)


SHARDED = (("ffn1_w_up", (D, UPW)), ("ffn1_w_down", (DFF // 4, D)), ("w_in", (D, D_IN // 4)),
           ("conv_w", (4, BW // 4)), ("gla_w_g2", (LOW_W, 64)), ("w_branch", (3 * BW, D // 4)),
           ("w_out", (D // 4, D)), ("ffn2_w_up", (D, UPW)), ("ffn2_w_down", (DFF // 4, D)),
           ("ple_w_proj", (PLE, D // 4)), ("ple_w_gate", (D // 4, D)))
OWNER = tuple(0 if n in ("ffn1_w_up", "w_in", "w_out") else 1 for n, _ in SHARDED)
SMALL = ("ln1_g", "ln1_b", "conv_b", "lru_wa", "lru_ba", "lru_wx", "lru_bx", "lru_lambda", "gla_b_g",
         "gla_norm_g", "fox_b_f", "ln2_g", "ln2_b", "ln3_g", "ln3_b", "ple_b_gate", "ln4_g", "ln4_b")
WEIGHTS = ('ffn1_w_up', 'ffn1_w_down', 'ln1_g', 'ln1_b', 'w_in', 'conv_w', 'conv_b', 'lru_wa', 'lru_ba',
           'lru_wx', 'lru_bx', 'lru_lambda', 'gla_w_g2', 'gla_b_g', 'gla_norm_g', 'fox_b_f', 'w_branch',
           'w_out', 'ln2_g', 'ln2_b', 'ffn2_w_up', 'ffn2_w_down', 'ln3_g', 'ln3_b', 'ple_w_proj',
           'ple_w_gate', 'ple_b_gate', 'ln4_g', 'ln4_b')


def _cols_join(parts):
    return jnp.concatenate([parts[k] for k in range(4)], axis=-1)


def _cols_split(full):
    r, c4 = full.shape
    return full.reshape(r, 4, c4 // 4).transpose(1, 0, 2)


def _regroup_in(w):
    pad = jnp.zeros(w.shape[:-1] + (ZW - D_IN,), w.dtype)
    fox_q = (w[..., 2576:3088] * FOX_SCALE).astype(w.dtype)
    return jnp.concatenate([w[..., 0:2048], w[..., 2064:2576], fox_q, w[..., 3088:4112], w[..., 4120:7192],
                            w[..., 2048:2064], w[..., 4112:4120], pad], axis=-1)


_IN_RUNS = ((0, 2048, 0, 1.0), (2048, 2064, 7168, 1.0), (2064, 2576, 2048, 1.0), (2576, 3088, CQ, FOX_SCALE),
            (3088, 4112, CK, 1.0), (4112, 4120, 7184, 1.0), (4120, D_IN, 4096, 1.0))


def _regroup_out_shards(g):
    w = D_IN // 4
    shards = []
    for k in range(4):
        pieces = []
        for a, b, new, f in _IN_RUNS:
            lo, hi = max(a, k * w), min(b, (k + 1) * w)
            if lo < hi:
                piece = g[:, new + lo - a:new + hi - a]
                pieces.append(piece if f == 1.0 else piece * f)
        shards.append(jnp.concatenate(pieces, axis=1))
    return jnp.stack(shards)


def _block_diag(w):
    eye = jnp.eye(8, dtype=w.dtype)
    return (eye[:, None, :, None] * w[:, :, None, :]).reshape(BW, BW)


def _diag_blocks(dense):
    return jnp.stack([dense[64 * n:64 * (n + 1), 64 * n:64 * (n + 1)] for n in range(8)])


def _layer_weights(gw, small, l):
    w = {"up1": gw["ffn1_w_up"], "up2": gw["ffn2_w_up"],
         "dn1": gw["ffn1_w_down"].reshape(1, DFF, D), "dn2": gw["ffn2_w_down"].reshape(1, DFF, D),
         "wo": gw["w_out"].reshape(1, D, D), "wgt": gw["ple_w_gate"].reshape(1, D, D)}
    w["win"] = _regroup_in(_cols_join(gw["w_in"][0]))
    w["cw"] = _cols_join(gw["conv_w"][0])
    w["wa"] = _block_diag(small["lru_wa"][l]).astype(BF16)
    w["wx"] = _block_diag(small["lru_wx"][l]).astype(BF16)
    w["wg2p"] = jnp.pad(_cols_join(gw["gla_w_g2"][0]), ((0, 128 - LOW_W), (0, 0)))
    w["wbr"] = _cols_join(gw["w_branch"][0].reshape(4, 3, BW, D // 4))
    w["wp"] = _cols_join(gw["ple_w_proj"][0])
    for n in ("ln1_g", "ln1_b", "ln2_g", "ln2_b", "ln3_g", "ln3_b", "ln4_g", "ln4_b", "conv_b", "lru_ba",
              "lru_bx", "lru_lambda", "gla_b_g", "gla_norm_g", "ple_b_gate"):
        w[n] = small[n][l][None, :]
    w["bfp"] = jnp.pad(small["fox_b_f"][l], (LOW_W, 128 - LOW_W - FOX_H))[None, :]
    return w


def _heads_t(a):
    ht = a[:, LOW_W:LOW_W + FOX_H].T
    return ht[:, None, :], jnp.broadcast_to(ht[:, :, None], ht.shape + (128,))


def _layer_fwd(x, xb, pb, w, l):
    s = {"x0": x, "x0b": xb}
    tag = "l%d_" % l
    gate, up, act = ffn_up(xb, w["up1"], 0, name=tag + "ffn1_up")
    r1, x1, x1b = matmul_res_ln(act, w["dn1"], 0, x, w["ln1_g"], w["ln1_b"], mm_scale=0.5, name=tag + "ffn1_down")
    s.update(gate1=gate, up1=up, act1=act, r1=r1, x1=x1, x1b=x1b)
    z, zb = matmul(x1b, w["win"], also_bf16=True, tm=1024, tn=_pick(ZW, (2432,)), name=tag + "mix_in")
    xc, xcb, h, ya = lru_fwd(z, w["cw"], w["conv_b"], w["wa"], w["wx"], w["lru_ba"], w["lru_bx"],
                             w["lru_lambda"], name=tag + "lru_fwd")
    yb, states = gla_fwd(z, zb, w["wg2p"], w["gla_b_g"], w["gla_norm_g"], name=tag + "gla_fwd")
    fcum = fox_fcum(z, w["bfp"], name=tag + "fox_fcum")
    fq, fk = _heads_t(fcum)
    yc, ycf, lse = fox_fwd_t(zb, fq, fk, name=tag + "fox_fwd")
    merged = merge_fwd(ya, yb, yc, w["wbr"], z, name=tag + "merge_fwd")
    r2, x2, x2b = matmul_res_ln(merged, w["wo"], 0, x1, w["ln2_g"], w["ln2_b"], mm_scale=1.0, name=tag + "mix_out")
    s.update(z=z, zb=zb, xc=xc, xcb=xcb, h=h, ya=ya, yb=yb, states=states, fq=fq, fk=fk, yc=yc, ycf=ycf,
             lse=lse, merged=merged, r2=r2, x2=x2, x2b=x2b)
    gate, up, act = ffn_up(x2b, w["up2"], 0, name=tag + "ffn2_up")
    r3, x3, x3b = matmul_res_ln(act, w["dn2"], 0, x2, w["ln3_g"], w["ln3_b"], mm_scale=0.5, name=tag + "ffn2_down")
    s.update(gate2=gate, up2=up, act2=act, r3=r3, x3=x3, x3b=x3b)
    r4, x4, x4b = ple_fwd(x3b, x3, pb, w["wgt"], 0, w["wp"], w["ple_b_gate"], w["ln4_g"], w["ln4_b"],
                          name=tag + "ple_fwd")
    s.update(r4=r4, pb=pb)
    return x4, x4b, s


def _ffn_bwd(dy, s, w, n, xin_b, l, tag):
    k = {"1": ("r1", "ln1_g", "gate1", "up1", "act1"), "2": ("r3", "ln3_g", "gate2", "up2", "act2")}[n]
    dr, dfb, dg, db = ln_bwd(dy, s[k[0]], w[k[1]], out_scale=0.5, name=tag + "ln_bwd")
    dgate, dup = ffn_down_bwd(dfb, w["dn" + n], 0, s[k[2]], s[k[3]], name=tag + "down_bwd")
    dx = ffn_dx(dgate, dup, w["up" + n], 0, dr, name=tag + "dx")
    dwup = matmul_tn_up(xin_b, dgate, dup, name=tag + "dw_up")
    dwdn = matmul_tn(s[k[4]], dfb, name=tag + "dw_down").reshape(4, DFF // 4, D)
    return dx, dwup, dwdn, dg[0], db[0]


def _layer_bwd(dy, s, w, l):
    g = {}
    tag = "l%d_" % l
    dr4, dglb, dpeb, dg4, db4, dbg = ple_bwd(dy, s["r4"], s["x3b"], s["pb"], w["wgt"], 0, w["wp"], w["ple_b_gate"],
                                             w["ln4_g"], name=tag + "ple_bwd")
    dx3 = matmul(dglb, w["wgt"], nt=True, b_lead=(0,), res=dr4, res_scale=ALPHA, tm=1024, tn=1024,
                 name=tag + "ple_dx")
    g["ple_w_gate"] = matmul_tn(s["x3b"], dglb, name=tag + "ple_dw_gate").reshape(4, D // 4, D)
    g["ple_w_proj"] = _cols_split(matmul_tn(s["pb"], dpeb, name=tag + "ple_dw_proj"))
    g["ln4_g"], g["ln4_b"], g["ple_b_gate"] = dg4[0], db4[0], dbg[0]
    dx2, g["ffn2_w_up"], g["ffn2_w_down"], g["ln3_g"], g["ln3_b"] = _ffn_bwd(dx3, s, w, "2", s["x2b"], l,
                                                                             tag + "ffn2_")
    dr2, doutb, dg2, db2 = ln_bwd(dx2, s["r2"], w["ln2_g"], out_scale=1.0, name=tag + "mix_ln_bwd")
    g["ln2_g"], g["ln2_b"] = dg2[0], db2[0]
    g["w_out"] = matmul_tn(s["merged"], doutb, name=tag + "dw_out").reshape(4, D // 4, D)
    z, zb = s["z"], s["zb"]
    (dya, dyb, dyc, dp0, dp1, dp2, dgl0, dgl1, dgl2) = merge_bwd(
        doutb, w["wo"], 0, s["ya"], s["yb"], s["yc"], w["wbr"], z, name=tag + "merge_bwd")
    dwbr = jnp.stack([matmul_tn(s["ya"], dp0, name=tag + "dw_br0"), matmul_tn(s["yb"], dp1, name=tag + "dw_br1"),
                      matmul_tn(s["yc"], dp2, name=tag + "dw_br2")])
    g["w_branch"] = _cols_split(dwbr.reshape(3 * BW, D))
    day, dxc, dprb, dpib, dba, dbx, dlam = lru_bwd(dya, z, s["h"], s["xc"], w["wa"], w["wx"],
                                                   w["lru_ba"], w["lru_bx"], w["lru_lambda"], name=tag + "lru_bwd")
    dax, dcw, dcb = conv_bwd(dxc, z, w["cw"], name=tag + "conv_bwd")
    g["lru_wa"] = _diag_blocks(matmul_tn(s["xcb"], dprb, name=tag + "dw_lru_a"))
    g["lru_wx"] = _diag_blocks(matmul_tn(s["xcb"], dpib, name=tag + "dw_lru_x"))
    g["lru_ba"], g["lru_bx"], g["lru_lambda"] = dba[0], dbx[0], dlam[0]
    g["conv_w"], g["conv_b"] = _cols_split(dcw), dcb[0]
    dbq, dbk, dbv, dbr, dmisc_g, dpreb, dbgg, dng = gla_bwd(dyb, z, zb, s["states"], w["wg2p"], w["gla_b_g"],
                                                            w["gla_norm_g"], name=tag + "gla_bwd")
    miscb = zb[:, MISC:]
    g["gla_w_g2"] = _cols_split(matmul_tn(miscb, dpreb, name=tag + "dw_g2")[:LOW_W])
    g["gla_b_g"], g["gla_norm_g"] = dbgg[0], dng[0]
    dl = fox_delta(dyc, s["ycf"], name=tag + "fox_delta")
    t = z.shape[0]
    dlq = dl[:, :FOX_H].T[:, None, :]
    dcq, dck, dcv, dfk, dfq = fox_bwd_t(zb, dyc, s["fq"], s["fk"], s["lse"], dlq, name=tag + "fox_bwd")
    dfc = jnp.pad((dfk[:, 0, :] + dfq.reshape(FOX_H, t)).T, ((0, 0), (LOW_W, 128 - LOW_W - FOX_H)))
    dmiscb, dbf = fox_dcf(dfc, z, w["bfp"], dmisc_g, name=tag + "fox_dcf")
    g["fox_b_f"] = dbf[0, LOW_W:LOW_W + FOX_H]
    dz = jnp.concatenate([dax, day, dbq, dbk, dbv, dbr, dcq, dck, dcv, dgl0, dgl1, dgl2, dmiscb], axis=1)
    dx1 = matmul(dz, w["win"], nt=True, res=dr2, res_scale=ALPHA, tm=1024, tn=1024, tk=_pick(ZW, (2432,)),
                 name=tag + "mix_dx")
    g["w_in"] = _regroup_out_shards(matmul_tn(s["x1b"], dz, name=tag + "dw_in"))
    dx0, g["ffn1_w_up"], g["ffn1_w_down"], g["ln1_g"], g["ln1_b"] = _ffn_bwd(dx1, s, w, "1", s["x0b"], l,
                                                                             tag + "ffn1_")
    return dx0, g


def _local_step(x, p, target, gathered, small, after_last_layer=None):
    xcur = x
    xb = xcur.astype(BF16)
    layer_w, saved = [], []
    for l in range(DEPTH):
        w = _layer_weights(gathered(l, xcur), small, l)
        xcur, xb, s = _layer_fwd(xcur, xb, p[l].astype(BF16), w, l)
        layer_w.append(w)
        saved.append(s)
    dy, sq = loss_head(xcur, target, name="loss_head")
    grads = [None] * DEPTH
    for l in reversed(range(DEPTH)):
        dy, grads[l] = _layer_bwd(dy, saved[l], layer_w[l], l)
        if l == DEPTH - 1 and after_last_layer is not None:
            layer_w[l - 1]["ln4_g"] = layer_w[l - 1]["ln4_g"] + after_last_layer(grads[l])
    return 0.5 * jnp.sum(sq) / float(D), dy, grads


def kernel(x, p, ffn1_w_up, ffn1_w_down, ln1_g, ln1_b, w_in, conv_w, conv_b, lru_wa, lru_ba, lru_wx, lru_bx, lru_lambda, gla_w_g2, gla_b_g, gla_norm_g, fox_b_f, w_branch, w_out, ln2_g, ln2_b, ffn2_w_up, ffn2_w_down, ln3_g, ln3_b, ple_w_proj, ple_w_gate, ple_b_gate, ln4_g, ln4_b, loss_target, m_ffn1_w_up, m_ffn1_w_down, m_ln1_g, m_ln1_b, m_w_in, m_conv_w, m_conv_b, m_lru_wa, m_lru_ba, m_lru_wx, m_lru_bx, m_lru_lambda, m_gla_w_g2, m_gla_b_g, m_gla_norm_g, m_fox_b_f, m_w_branch, m_w_out, m_ln2_g, m_ln2_b, m_ffn2_w_up, m_ffn2_w_down, m_ln3_g, m_ln3_b, m_ple_w_proj, m_ple_w_gate, m_ple_b_gate, m_ln4_g, m_ln4_b, v_ffn1_w_up, v_ffn1_w_down, v_ln1_g, v_ln1_b, v_w_in, v_conv_w, v_conv_b, v_lru_wa, v_lru_ba, v_lru_wx, v_lru_bx, v_lru_lambda, v_gla_w_g2, v_gla_b_g, v_gla_norm_g, v_fox_b_f, v_w_branch, v_w_out, v_ln2_g, v_ln2_b, v_ffn2_w_up, v_ffn2_w_down, v_ln3_g, v_ln3_b, v_ple_w_proj, v_ple_w_gate, v_ple_b_gate, v_ln4_g, v_ln4_b):
    args = dict(locals())
    wts = {n: args[n] for n in WEIGHTS}
    mom = {n: args["m_" + n] for n in WEIGHTS}
    var = {n: args["v_" + n] for n in WEIGHTS}
    cx, cy, cc = lax.axis_index("x"), lax.axis_index("y"), lax.axis_index("c")

    names = [n for n, _ in SHARDED]
    shards = [wts[n].reshape((DEPTH,) + rc).astype(F32 if n == "conv_w" else BF16) for n, rc in SHARDED]
    first, lands = gather_first(shards)
    rest_send, rest_recv, rest_srcs, rest_lands, rest_token = gather_rest_start(shards, lands)
    small = {n: wts[n] for n in SMALL}
    small["ln1_g"] = small["ln1_g"] + rest_token[0, 0]

    def gathered(l, after):
        if l == 0:
            return dict(zip(names, first))
        return dict(zip(names, gather_rest_wait(rest_send, rest_recv, rest_srcs, rest_lands, after)))

    flight = {}
    core = jnp.reshape(cc, (1,)).astype(jnp.int32)
    chip = jnp.reshape(2 * cx + cy, (1,)).astype(jnp.int32)

    def chip_sum(gl, layer):
        lst = [gl[n] for n in names]
        rb = pair_send(lst, OWNER, layer)
        return [pair_add_layer(a, r, core, own, name="pair_add_l%d_%s" % (layer, n))
                for n, own, a, r in zip(names, OWNER, lst, rb)]

    def start_last_layer(gl):
        s1 = chip_sum(gl, DEPTH - 1)
        send_sems, recv_sems, srcs, lands, token = chip_send_start([sb for _, sb in s1], OWNER, DEPTH - 1)
        flight.update(s1=[sf for sf, _ in s1], sems=(send_sems, recv_sems), srcs=srcs, lands=lands)
        return token[0, 0]

    loss_local, dx, grads = _local_step(x[0], p[:, 0], loss_target[0], gathered, small, start_last_layer)
    loss = lax.psum(loss_local, ("x", "y", "c"))
    grad_x = dx[None]

    s1_first = chip_sum(grads[0], 0)
    first_send, first_recv, first_srcs, first_lands, first_token = chip_send_start(
        [sb for _, sb in s1_first], OWNER, 0)

    pieces, spans, row = [], {}, 0
    for n in SMALL:
        flat = jnp.stack([grads[l][n] for l in range(DEPTH)]).reshape(-1)
        rows = -(-flat.shape[0] // (8 * PACK_W)) * 8
        pieces.append(jnp.pad(flat, (0, rows * PACK_W - flat.shape[0])).reshape(rows, PACK_W))
        spans[n] = (row, rows)
        row += rows
    gs = jnp.concatenate(pieces, axis=0) + first_token[0, 0]
    me = jnp.reshape(4 * cx + 2 * cy + cc, (1,)).astype(jnp.int32)
    gsum = small_add(gs, small_exchange(gs), me)

    gout, delta, new_m, new_v = {}, {}, {}, {}

    def update(n, view, g):
        shp = wts[n].shape
        d, mn, vn = adamw(wts[n].reshape(view), g, mom[n].reshape(view), var[n].reshape(view), name="adamw_" + n)
        gout[n], delta[n], new_m[n], new_v[n] = g.reshape(shp), d.reshape(shp), mn.reshape(shp), vn.reshape(shp)

    for n in SMALL:
        view = (1, DEPTH, wts[n].size // DEPTH)
        r0, rows = spans[n]
        update(n, view, gsum[r0:r0 + rows].reshape(-1)[:wts[n].size].reshape(view))

    rb2_first = chip_send_wait(first_send, first_recv, first_srcs, first_lands, new_v[SMALL[-1]], OWNER, 0)
    rb2_last = chip_send_wait(*flight["sems"], flight["srcs"], flight["lands"], dx, OWNER, DEPTH - 1)
    s2 = [chip_add_layers((sf0, sf1), (r0, r1), chip, core, own, name="chip_add_" + n)
          for n, own, (sf0, _), sf1, r0, r1 in zip(names, OWNER, s1_first, flight["s1"], rb2_first, rb2_last)]
    for n, gsh in zip(names, pair_share(s2, OWNER)):
        update(n, gsh.shape, gsh)

    return (loss, grad_x, *[gout[n] for n in WEIGHTS], *[delta[n] for n in WEIGHTS],
            *[new_m[n] for n in WEIGHTS], *[new_v[n] for n in WEIGHTS])
```

```python
import functools
import math

import jax
import jax.numpy as jnp
from jax import lax
from jax.experimental import pallas as pl
from jax.experimental.pallas import tpu as pltpu

F32 = jnp.float32
BF16 = jnp.bfloat16

D = 1024
DFF = 2816
BW = 512
PLE = 256
DEPTH = 2
ALPHA = (2 * DEPTH) ** 0.25
LN_EPS = 1e-5
RMS_EPS = 1e-6
LRU_C = 8.0
GLA_TAU = 16.0
CHUNK = 64
D_IN = 7192
ZW = 7296
AX, AY, BQ, BK, BV, BR, CQ, CK, CV, G0, MISC = 0, 512, 1024, 1280, 1536, 2048, 2560, 3072, 3584, 4096, 7168
LOW_W, FOX_H = 16, 8
ADAM_LR, ADAM_B1, ADAM_B2, ADAM_EPS, ADAM_WD, ADAM_STEP = 0.001, 0.9, 0.999, 1e-08, 0.01, 10
PACK_W = 1024
VMEM_LIMIT = 56 << 20

MESH = pl.DeviceIdType.MESH
ANY = pl.BlockSpec(memory_space=pl.ANY)


def _pcall(body, **kw):
    return pl.pallas_call(body, **kw)


def _cp(*dims):
    return pltpu.CompilerParams(dimension_semantics=dims, vmem_limit_bytes=VMEM_LIMIT)


def _dot(a, b):
    return jnp.dot(a, b, preferred_element_type=F32)


def _dot_nt(a, b):
    return lax.dot_general(a, b, (((1,), (1,)), ((), ())), preferred_element_type=F32)


def _dot_tn(a, b):
    return lax.dot_general(a, b, (((0,), (0,)), ((), ())), preferred_element_type=F32)


def _dot_hi(a, b):
    return jnp.dot(a, b, preferred_element_type=F32, precision=lax.Precision.HIGHEST)


def _sigmoid(x):
    return 1.0 / (1.0 + jnp.exp(-x))


def _softplus(x):
    return jnp.maximum(x, 0.0) + jnp.log(1.0 + jnp.exp(-jnp.abs(x)))


def _log_sigmoid(x):
    return -_softplus(-x)


def _expm1(x):
    poly = x * (1.0 + x * (0.5 + x * (1.0 / 6.0 + x * (1.0 / 24.0 + x * (1.0 / 120.0 + x * (1.0 / 720.0))))))
    return jnp.where(jnp.abs(x) < 0.1, poly, jnp.exp(x) - 1.0)


_GELU_C = math.sqrt(2.0 / math.pi)


def _gelu(x):
    return 0.5 * x * (1.0 + jnp.tanh(_GELU_C * (x + 0.044715 * x * x * x)))


def _gelu_grad(x):
    t = jnp.tanh(_GELU_C * (x + 0.044715 * x * x * x))
    return 0.5 * (1.0 + t) + 0.5 * x * (1.0 - t * t) * _GELU_C * (1.0 + 3.0 * 0.044715 * x * x)


def _ln_stats(r):
    mu = jnp.mean(r, axis=-1, keepdims=True)
    xc = r - mu
    var = jnp.mean(xc * xc, axis=-1, keepdims=True)
    return xc, lax.rsqrt(var + LN_EPS)


def _pick(n, cands):
    for c in cands:
        if n % c == 0:
            return c
    return n


def _rows(tm, w, col=0):
    return pl.BlockSpec((tm, w), lambda i: (i, col))


def _fix(shape):
    nd = len(shape)
    return pl.BlockSpec(shape, lambda i: (0,) * nd)


def _col_chunks(n, width=256):
    return [slice(c, min(c + width, n)) for c in range(0, n, width)]


def _layer(l, shape):
    nd = len(shape)
    return pl.BlockSpec((None,) + tuple(shape), lambda i: (l,) + (0,) * nd)


def matmul(a, b, *, name, nt=False, b_lead=(), res=None, res_scale=1.0, also_bf16=False, tm=512, tn=512,
           tk=None):
    m, k = a.shape
    n = b.shape[-2] if nt else b.shape[-1]
    tm, tn = min(tm, m), min(tn, n)
    tk = k if tk is None else tk
    nk = k // tk
    has_res = res is not None
    lead = tuple(b_lead)
    dot = _dot_nt if nt else _dot

    def body(*refs):
        a_ref, b_ref = refs[0], refs[1]
        pos = 2
        r_ref = None
        if has_res:
            r_ref = refs[pos]
            pos += 1
        o_ref = refs[pos]
        pos += 1
        ob_ref = None
        if also_bf16:
            ob_ref = refs[pos]
            pos += 1

        def finish(v):
            if has_res:
                v = v + res_scale * r_ref[...]
            o_ref[...] = v
            if also_bf16:
                ob_ref[...] = v.astype(BF16)

        if nk == 1:
            finish(dot(a_ref[...], b_ref[...]))
            return
        acc = refs[pos]
        kk = pl.program_id(2)

        @pl.when(kk == 0)
        def _():
            acc[...] = jnp.zeros_like(acc)

        acc[...] += dot(a_ref[...], b_ref[...])

        @pl.when(kk == nk - 1)
        def _():
            finish(acc[...])

    none = (None,) * len(lead)
    if nt:
        b_spec = pl.BlockSpec(none + (tn, tk), lambda j, i, kk: lead + (j, kk))
    else:
        b_spec = pl.BlockSpec(none + (tk, tn), lambda j, i, kk: lead + (kk, j))
    in_specs = [pl.BlockSpec((tm, tk), lambda j, i, kk: (i, kk)), b_spec]
    args = [a, b]
    if has_res:
        in_specs.append(pl.BlockSpec((tm, tn), lambda j, i, kk: (i, j)))
        args.append(res)
    out_shape = [jax.ShapeDtypeStruct((m, n), F32)]
    out_specs = [pl.BlockSpec((tm, tn), lambda j, i, kk: (i, j))]
    if also_bf16:
        out_shape.append(jax.ShapeDtypeStruct((m, n), BF16))
        out_specs.append(pl.BlockSpec((tm, tn), lambda j, i, kk: (i, j)))
    out = _pcall(body, name=name, grid=(n // tn, m // tm, nk), in_specs=in_specs, out_specs=out_specs,
                 out_shape=out_shape, scratch_shapes=[pltpu.VMEM((tm, tn), F32)] if nk > 1 else [],
                 compiler_params=_cp("parallel", "parallel", "arbitrary"))(*args)
    return out if also_bf16 else out[0]


def matmul_tn(a, b, *, name):
    t, k = a.shape
    n = b.shape[1]
    tk = _pick(k, (1024, 1408, 512, 256, 128))
    tn = _pick(n, (1024, 1408, 2432, 512, 256, 128))
    tt = min(1024 if tk * tn > (1 << 20) else 2048, t)
    nt = t // tt

    def body(a_ref, b_ref, o_ref):
        @pl.when(pl.program_id(2) == 0)
        def _():
            o_ref[...] = jnp.zeros_like(o_ref)

        o_ref[...] += _dot_tn(a_ref[...], b_ref[...])

    return _pcall(body, name=name, grid=(k // tk, n // tn, nt),
                  in_specs=[pl.BlockSpec((tt, tk), lambda i, j, s: (s, i)),
                            pl.BlockSpec((tt, tn), lambda i, j, s: (s, j))],
                  out_specs=pl.BlockSpec((tk, tn), lambda i, j, s: (i, j)),
                  out_shape=jax.ShapeDtypeStruct((k, n), F32),
                  compiler_params=_cp("parallel", "parallel", "arbitrary"))(a, b)


UPW = 1408


def matmul_tn_up(a, dgate, dup, *, name):
    t, k = a.shape
    tt = min(1024, t)
    tk = 1024

    def body(a_ref, g_ref, u_ref, o_ref):
        j = pl.program_id(1)

        @pl.when(pl.program_id(2) == 0)
        def _():
            o_ref[...] = jnp.zeros_like(o_ref)

        @pl.when(j < 2)
        def _():
            o_ref[...] += _dot_tn(a_ref[...], g_ref[...])

        @pl.when(j >= 2)
        def _():
            o_ref[...] += _dot_tn(a_ref[...], u_ref[...])

    return _pcall(body, name=name, grid=(k // tk, 4, t // tt),
                  in_specs=[pl.BlockSpec((tt, tk), lambda i, j, s: (s, i)),
                            pl.BlockSpec((tt, UPW), lambda i, j, s: (jnp.where(j < 2, s, 0), jnp.minimum(j, 1))),
                            pl.BlockSpec((tt, UPW), lambda i, j, s: (jnp.where(j >= 2, s, 0), jnp.maximum(j - 2, 0)))],
                  out_specs=pl.BlockSpec((None, tk, UPW), lambda i, j, s: (j, i, 0)),
                  out_shape=jax.ShapeDtypeStruct((4, k, UPW), F32),
                  compiler_params=_cp("parallel", "parallel", "arbitrary"))(a, dgate, dup)


def ffn_dx(dgate, dup, wup, l, res, *, name):
    t = dgate.shape[0]
    tm, tn = min(1024, t), 1024

    def body(g_ref, u_ref, w_ref, r_ref, o_ref, acc):
        kk = pl.program_id(2)

        @pl.when(kk == 0)
        def _():
            acc[...] = jnp.zeros_like(acc)

        @pl.when(kk < 2)
        def _():
            acc[...] += _dot_nt(g_ref[...], w_ref[...])

        @pl.when(kk >= 2)
        def _():
            acc[...] += _dot_nt(u_ref[...], w_ref[...])

        @pl.when(kk == 3)
        def _():
            o_ref[...] = acc[...] + ALPHA * r_ref[...]

    return _pcall(body, name=name, grid=(D // tn, t // tm, 4),
                  in_specs=[pl.BlockSpec((tm, UPW), lambda j, i, kk: (i, jnp.minimum(kk, 1))),
                            pl.BlockSpec((tm, UPW), lambda j, i, kk: (i, jnp.maximum(kk - 2, 0))),
                            pl.BlockSpec((None, None, tn, UPW), lambda j, i, kk: (l, kk, j, 0)),
                            pl.BlockSpec((tm, tn), lambda j, i, kk: (i, j))],
                  out_specs=pl.BlockSpec((tm, tn), lambda j, i, kk: (i, j)),
                  out_shape=jax.ShapeDtypeStruct((t, D), F32),
                  scratch_shapes=[pltpu.VMEM((tm, tn), F32)],
                  compiler_params=_cp("parallel", "parallel", "arbitrary"))(dgate, dup, wup, res)


def ffn_up(xb, wup, l, *, name):
    t = xb.shape[0]
    tm, tn = min(1024, t), UPW

    def body(x_ref, wg_ref, wu_ref, g_ref, u_ref, a_ref):
        x = x_ref[...]
        for cols in _col_chunks(tn):
            g = _dot(x, wg_ref[:, cols])
            u = _dot(x, wu_ref[:, cols])
            g_ref[:, cols] = g.astype(BF16)
            u_ref[:, cols] = u.astype(BF16)
            a_ref[:, cols] = (g * _sigmoid(g) * u).astype(BF16)

    blk = pl.BlockSpec((tm, tn), lambda j, i: (i, j))
    return _pcall(body, name=name, grid=(DFF // tn, t // tm),
                  in_specs=[pl.BlockSpec((tm, D), lambda j, i: (i, 0)),
                            pl.BlockSpec((None, None, D, tn), lambda j, i: (l, j, 0, 0)),
                            pl.BlockSpec((None, None, D, tn), lambda j, i: (l, 2 + j, 0, 0))],
                  out_specs=[blk, blk, blk],
                  out_shape=[jax.ShapeDtypeStruct((t, DFF), BF16)] * 3,
                  compiler_params=_cp("parallel", "parallel"))(xb, wup, wup)


def matmul_res_ln(a, w, l, res, g, b, *, mm_scale, name):
    t, k = a.shape
    tm = min(512, t)

    def body(a_ref, w_ref, res_ref, g_ref, b_ref, r_ref, y_ref, yb_ref):
        f = _dot(a_ref[...], w_ref[...])
        r = ALPHA * res_ref[...] + mm_scale * f
        xc, rstd = _ln_stats(r)
        y = xc * rstd * g_ref[...] + b_ref[...]
        r_ref[...] = r
        y_ref[...] = y
        yb_ref[...] = y.astype(BF16)

    return _pcall(body, name=name, grid=(t // tm,),
                  in_specs=[_rows(tm, k), _layer(l, (k, D)), _rows(tm, D), _fix((1, D)), _fix((1, D))],
                  out_specs=[_rows(tm, D)] * 3,
                  out_shape=[jax.ShapeDtypeStruct((t, D), F32), jax.ShapeDtypeStruct((t, D), F32),
                             jax.ShapeDtypeStruct((t, D), BF16)],
                  compiler_params=_cp("parallel"))(a, w, res, g, b)


def ln_bwd(dy, r, g, *, out_scale, name):
    t = dy.shape[0]
    tm = min(512, t)

    def body(dy_ref, r_ref, g_ref, dr_ref, drb_ref, dg_ref, db_ref):
        @pl.when(pl.program_id(0) == 0)
        def _():
            dg_ref[...] = jnp.zeros_like(dg_ref)
            db_ref[...] = jnp.zeros_like(db_ref)

        xc, rstd = _ln_stats(r_ref[...])
        xhat = xc * rstd
        d = dy_ref[...]
        dxh = d * g_ref[...]
        dr = rstd * (dxh - jnp.mean(dxh, axis=-1, keepdims=True)
                     - xhat * jnp.mean(dxh * xhat, axis=-1, keepdims=True))
        dr_ref[...] = dr
        drb_ref[...] = (out_scale * dr).astype(BF16)
        dg_ref[...] += jnp.sum(d * xhat, axis=0, keepdims=True)
        db_ref[...] += jnp.sum(d, axis=0, keepdims=True)

    return _pcall(body, name=name, grid=(t // tm,),
                  in_specs=[_rows(tm, D), _rows(tm, D), _fix((1, D))],
                  out_specs=[_rows(tm, D), _rows(tm, D), _fix((1, D)), _fix((1, D))],
                  out_shape=[jax.ShapeDtypeStruct((t, D), F32), jax.ShapeDtypeStruct((t, D), BF16),
                             jax.ShapeDtypeStruct((1, D), F32), jax.ShapeDtypeStruct((1, D), F32)],
                  compiler_params=_cp("arbitrary"))(dy, r, g)


def ffn_down_bwd(dfb, wd, l, gate, up, *, name):
    t = dfb.shape[0]
    tm, tn = min(1024, t), UPW
    nj = DFF // tn

    def body(df_ref, w_ref, g_ref, u_ref, dg_ref, du_ref):
        df = df_ref[...]
        for cols in _col_chunks(tn):
            da = _dot_nt(df, w_ref[cols, :])
            g = g_ref[:, cols].astype(F32)
            s = _sigmoid(g)
            gs = g * s
            dg_ref[:, cols] = (da * u_ref[:, cols].astype(F32) * (s + gs * (1.0 - s))).astype(BF16)
            du_ref[:, cols] = (da * gs).astype(BF16)

    blk = pl.BlockSpec((tm, tn), lambda j, i: (i, j))
    return _pcall(body, name=name, grid=(nj, t // tm),
                  in_specs=[pl.BlockSpec((tm, D), lambda j, i: (i, 0)),
                            pl.BlockSpec((None, tn, D), lambda j, i: (l, j, 0)), blk, blk],
                  out_specs=[blk, blk],
                  out_shape=[jax.ShapeDtypeStruct((t, DFF), BF16), jax.ShapeDtypeStruct((t, DFF), BF16)],
                  compiler_params=_cp("parallel", "parallel"))(dfb, wd, gate, up)


def ple_fwd(xb, x, pb, wgate, l, wproj, bgate, g, b, *, name):
    t = x.shape[0]
    tm = min(512, t)

    def body(xb_ref, x_ref, p_ref, wg_ref, wp_ref, bg_ref, g_ref, b_ref, r_ref, y_ref, yb_ref):
        gl = _dot(xb_ref[...], wg_ref[...]) + bg_ref[...]
        pe = _dot(p_ref[...], wp_ref[...])
        r = ALPHA * x_ref[...] + _sigmoid(gl) * pe
        xc, rstd = _ln_stats(r)
        y = xc * rstd * g_ref[...] + b_ref[...]
        r_ref[...] = r
        y_ref[...] = y
        yb_ref[...] = y.astype(BF16)

    return _pcall(body, name=name, grid=(t // tm,),
                  in_specs=[_rows(tm, D), _rows(tm, D), _rows(tm, PLE), _layer(l, (D, D)), _fix((PLE, D)),
                            _fix((1, D)), _fix((1, D)), _fix((1, D))],
                  out_specs=[_rows(tm, D)] * 3,
                  out_shape=[jax.ShapeDtypeStruct((t, D), F32), jax.ShapeDtypeStruct((t, D), F32),
                             jax.ShapeDtypeStruct((t, D), BF16)],
                  compiler_params=_cp("parallel"))(xb, x, pb, wgate, wproj, bgate, g, b)


def ple_bwd(dy, r, xb, pb, wgate, l, wproj, bgate, g, *, name):
    t = dy.shape[0]
    tm = min(512, t)

    def body(dy_ref, r_ref, xb_ref, p_ref, wg_ref, wp_ref, bg_ref, g_ref,
             dr_ref, dgl_ref, dpe_ref, dg_ref, db_ref, dbg_ref):
        @pl.when(pl.program_id(0) == 0)
        def _():
            dg_ref[...] = jnp.zeros_like(dg_ref)
            db_ref[...] = jnp.zeros_like(db_ref)
            dbg_ref[...] = jnp.zeros_like(dbg_ref)

        xc, rstd = _ln_stats(r_ref[...])
        xhat = xc * rstd
        d = dy_ref[...]
        dxh = d * g_ref[...]
        dr = rstd * (dxh - jnp.mean(dxh, axis=-1, keepdims=True)
                     - xhat * jnp.mean(dxh * xhat, axis=-1, keepdims=True))
        s = _sigmoid(_dot(xb_ref[...], wg_ref[...]) + bg_ref[...])
        pe = _dot(p_ref[...], wp_ref[...])
        dgl = dr * pe * s * (1.0 - s)
        dr_ref[...] = dr
        dgl_ref[...] = dgl.astype(BF16)
        dpe_ref[...] = (dr * s).astype(BF16)
        dg_ref[...] += jnp.sum(d * xhat, axis=0, keepdims=True)
        db_ref[...] += jnp.sum(d, axis=0, keepdims=True)
        dbg_ref[...] += jnp.sum(dgl, axis=0, keepdims=True)

    vec = jax.ShapeDtypeStruct((1, D), F32)
    return _pcall(body, name=name, grid=(t // tm,),
                  in_specs=[_rows(tm, D), _rows(tm, D), _rows(tm, D), _rows(tm, PLE), _layer(l, (D, D)),
                            _fix((PLE, D)), _fix((1, D)), _fix((1, D))],
                  out_specs=[_rows(tm, D), _rows(tm, D), _rows(tm, D), _fix((1, D)), _fix((1, D)), _fix((1, D))],
                  out_shape=[jax.ShapeDtypeStruct((t, D), F32), jax.ShapeDtypeStruct((t, D), BF16),
                             jax.ShapeDtypeStruct((t, D), BF16), vec, vec, vec],
                  compiler_params=_cp("arbitrary"))(dy, r, xb, pb, wgate, wproj, bgate, g)


def loss_head(y, tgt, *, name):
    t = y.shape[0]
    tm = min(256, t)

    def body(y_ref, t_ref, dy_ref, sq_ref):
        @pl.when(pl.program_id(0) == 0)
        def _():
            sq_ref[...] = jnp.zeros_like(sq_ref)

        e = y_ref[...] - t_ref[...]
        dy_ref[...] = e / float(D)
        sq_ref[...] += jnp.sum(e * e, axis=0, keepdims=True)

    return _pcall(body, name=name, grid=(t // tm,),
                  in_specs=[_rows(tm, D), _rows(tm, D)],
                  out_specs=[_rows(tm, D), _fix((1, D))],
                  out_shape=[jax.ShapeDtypeStruct((t, D), F32), jax.ShapeDtypeStruct((1, D), F32)],
                  compiler_params=_cp("arbitrary"))(y, tgt)


def _lru_gates(xc, wa_ref, wx_ref, ba_ref, bx_ref, lam_ref):
    xcb = xc.astype(BF16)
    r = _sigmoid(_dot(xcb, wa_ref[...]) + ba_ref[...])
    ig = _sigmoid(_dot(xcb, wx_ref[...]) + bx_ref[...])
    sp = _softplus(-lam_ref[...])
    la = -LRU_C * r * sp
    a = jnp.exp(la)
    mult = jnp.sqrt(-_expm1(2.0 * la))
    return r, ig, sp, la, a, mult


def lru_fwd(z, cw, cb, wa, wx, ba, bx, lam, *, name):
    t = z.shape[0]
    tm = min(512, t)
    hb = tm // 8

    def body(ax_ref, prev_ref, ay_ref, cw_ref, cb_ref, wa_ref, wx_ref, ba_ref, bx_ref, lam_ref,
             xc_ref, xcb_ref, h_ref, ya_ref, xs, a_s, b_s, hc):
        i = pl.program_id(0)

        @pl.when(i == 0)
        def _():
            hc[...] = jnp.zeros_like(hc)

        xs[0:8, :] = jnp.where(i == 0, 0.0, prev_ref[...])
        xs[8:, :] = ax_ref[...]
        xc = cb_ref[...] + cw_ref[0:1, :] * xs[5:5 + tm, :]
        for k in range(1, 4):
            xc = xc + cw_ref[k:k + 1, :] * xs[5 + k:5 + k + tm, :]
        r, ig, sp, la, a, mult = _lru_gates(xc, wa_ref, wx_ref, ba_ref, bx_ref, lam_ref)
        a_s[...] = a
        b_s[...] = mult * (ig * xc)
        xc_ref[...] = xc
        xcb_ref[...] = xc.astype(BF16)

        def step(g, h):
            base = pl.multiple_of(g * 8, 8)
            a8 = a_s[pl.ds(base, 8), :]
            b8 = b_s[pl.ds(base, 8), :]
            for j in range(8):
                h = a8[j:j + 1, :] * h + b8[j:j + 1, :]
                h_ref[pl.ds(base + j, 1), :] = h
            return h

        hc[...] = lax.fori_loop(0, tm // 8, step, hc[...])
        ya_ref[...] = (_gelu(ay_ref[...]) * h_ref[...]).astype(BF16)

    vec = _fix((1, BW))
    return _pcall(body, name=name, grid=(t // tm,),
                  in_specs=[_rows(tm, BW, AX // BW),
                            pl.BlockSpec((8, BW), lambda i: (jnp.maximum(i * hb - 1, 0), AX // BW)),
                            _rows(tm, BW, AY // BW), _fix((4, BW)), vec, _fix((BW, BW)), _fix((BW, BW)),
                            vec, vec, vec],
                  out_specs=[_rows(tm, BW)] * 4,
                  out_shape=[jax.ShapeDtypeStruct((t, BW), F32), jax.ShapeDtypeStruct((t, BW), BF16),
                             jax.ShapeDtypeStruct((t, BW), F32), jax.ShapeDtypeStruct((t, BW), BF16)],
                  scratch_shapes=[pltpu.VMEM((tm + 8, BW), F32), pltpu.VMEM((tm, BW), F32),
                                  pltpu.VMEM((tm, BW), F32), pltpu.VMEM((1, BW), F32)],
                  compiler_params=_cp("arbitrary"))(z, z, z, cw, cb, wa, wx, ba, bx, lam)


def lru_bwd(dya, z, h, xc, wa, wx, ba, bx, lam, *, name):
    t = dya.shape[0]
    tm = min(512, t)
    nb = t // tm
    hb = tm // 8

    def body(dya_ref, ay_ref, h_ref, hprev_ref, xc_ref, wa_ref, wx_ref, ba_ref, bx_ref,
             lam_ref, day_ref, dxc_ref, dpr_ref, dpi_ref, dba_ref, dbx_ref, dlam_ref,
             hs, a_s, g_s, d_s, cc):
        i = pl.program_id(0)

        @pl.when(i == 0)
        def _():
            cc[...] = jnp.zeros_like(cc)
            dba_ref[...] = jnp.zeros_like(dba_ref)
            dbx_ref[...] = jnp.zeros_like(dbx_ref)
            dlam_ref[...] = jnp.zeros_like(dlam_ref)

        xc = xc_ref[...]
        r, ig, sp, la, a, mult = _lru_gates(xc, wa_ref, wx_ref, ba_ref, bx_ref, lam_ref)
        ay = ay_ref[...]
        dya = dya_ref[...]
        hcur = h_ref[...]
        day_ref[...] = (dya * hcur * _gelu_grad(ay)).astype(BF16)
        a_s[...] = a
        g_s[...] = dya * _gelu(ay)

        def step(gg, cin):
            g = tm // 8 - 1 - gg
            base = pl.multiple_of(g * 8, 8)
            a8 = a_s[pl.ds(base, 8), :]
            g8 = g_s[pl.ds(base, 8), :]
            for j in range(7, -1, -1):
                d = g8[j:j + 1, :] + cin
                d_s[pl.ds(base + j, 1), :] = d
                cin = a8[j:j + 1, :] * d
            return cin

        cc[...] = lax.fori_loop(0, tm // 8, step, cc[...])
        dht = d_s[...]
        hs[0:8, :] = jnp.where(i == nb - 1, 0.0, hprev_ref[...])
        hs[8:, :] = hcur
        da = dht * hs[7:7 + tm, :]
        dmult = dht * ig * xc
        dig = dht * mult * xc
        dla = da * a - dmult * a * a / mult
        dpr = dla * (-LRU_C * sp) * r * (1.0 - r)
        dpi = dig * ig * (1.0 - ig)
        dprb = dpr.astype(BF16)
        dpib = dpi.astype(BF16)
        dxc_ref[...] = dht * mult * ig + _dot_nt(dprb, wa_ref[...]) + _dot_nt(dpib, wx_ref[...])
        dpr_ref[...] = dprb
        dpi_ref[...] = dpib
        dba_ref[...] += jnp.sum(dpr, axis=0, keepdims=True)
        dbx_ref[...] += jnp.sum(dpi, axis=0, keepdims=True)
        dlam_ref[...] += jnp.sum(dla * (-LRU_C * r), axis=0, keepdims=True) * (-_sigmoid(-lam_ref[...]))

    vec = _fix((1, BW))
    mat = _fix((BW, BW))
    rev = lambda col: pl.BlockSpec((tm, BW), lambda i: (nb - 1 - i, col))
    vshape = jax.ShapeDtypeStruct((1, BW), F32)
    return _pcall(body, name=name, grid=(nb,),
                  in_specs=[rev(0), rev(AY // BW), rev(0),
                            pl.BlockSpec((8, BW), lambda i: (jnp.maximum((nb - 1 - i) * hb - 1, 0), 0)),
                            rev(0), mat, mat, vec, vec, vec],
                  out_specs=[rev(0), rev(0), rev(0), rev(0), vec, vec, vec],
                  out_shape=[jax.ShapeDtypeStruct((t, BW), BF16), jax.ShapeDtypeStruct((t, BW), F32),
                             jax.ShapeDtypeStruct((t, BW), BF16), jax.ShapeDtypeStruct((t, BW), BF16),
                             vshape, vshape, vshape],
                  scratch_shapes=[pltpu.VMEM((tm + 8, BW), F32), pltpu.VMEM((tm, BW), F32),
                                  pltpu.VMEM((tm, BW), F32), pltpu.VMEM((tm, BW), F32),
                                  pltpu.VMEM((1, BW), F32)],
                  compiler_params=_cp("arbitrary"))(dya, z, h, h, xc, wa, wx, ba, bx, lam)


def conv_bwd(dxc, z, cw, *, name):
    t = dxc.shape[0]
    tm = min(512, t)
    nb = t // tm
    hb = tm // 8

    def body(d_ref, dnext_ref, ax_ref, prev_ref, cw_ref, dax_ref, dcw_ref, dcb_ref, ds, xs):
        i = pl.program_id(0)

        @pl.when(i == 0)
        def _():
            dcw_ref[...] = jnp.zeros_like(dcw_ref)
            dcb_ref[...] = jnp.zeros_like(dcb_ref)

        d = d_ref[...]
        ds[0:tm, :] = d
        ds[tm:, :] = jnp.where(i == nb - 1, 0.0, dnext_ref[...])
        xs[0:8, :] = jnp.where(i == 0, 0.0, prev_ref[...])
        xs[8:, :] = ax_ref[...]
        dax = cw_ref[3:4, :] * d
        for k in range(3):
            dax = dax + cw_ref[k:k + 1, :] * ds[3 - k:3 - k + tm, :]
        dax_ref[...] = dax.astype(BF16)
        for k in range(4):
            dcw_ref[k:k + 1, :] += jnp.sum(d * xs[5 + k:5 + k + tm, :], axis=0, keepdims=True)
        dcb_ref[...] += jnp.sum(d, axis=0, keepdims=True)

    return _pcall(body, name=name, grid=(nb,),
                  in_specs=[_rows(tm, BW),
                            pl.BlockSpec((8, BW), lambda i: (jnp.minimum((i + 1) * hb, nb * hb - 1), 0)),
                            _rows(tm, BW, AX // BW),
                            pl.BlockSpec((8, BW), lambda i: (jnp.maximum(i * hb - 1, 0), AX // BW)),
                            _fix((4, BW))],
                  out_specs=[_rows(tm, BW), _fix((4, BW)), _fix((1, BW))],
                  out_shape=[jax.ShapeDtypeStruct((t, BW), BF16), jax.ShapeDtypeStruct((4, BW), F32),
                             jax.ShapeDtypeStruct((1, BW), F32)],
                  scratch_shapes=[pltpu.VMEM((tm + 8, BW), F32), pltpu.VMEM((tm + 8, BW), F32)],
                  compiler_params=_cp("arbitrary"))(dxc, dxc, z, z, cw)


GLA_CB = 4


def _gla_consts():
    tri = (jnp.arange(CHUNK)[:, None] >= jnp.arange(CHUNK)[None, :]).astype(F32)
    mask = ((jnp.arange(BW)[:, None] // 128) == (jnp.arange(256)[None, :] // 64)).astype(F32)
    return tri, mask


def gla_fwd(z, zb, wg2p, bg, ng, *, name):
    t = z.shape[0]
    tm = GLA_CB * CHUNK
    nc = t // CHUNK
    tri, mask = _gla_consts()

    def body(q_ref, k_ref, v_ref, misc_ref, br_ref, w_ref, bg_ref, ng_ref, tri_ref, mask_ref,
             yb_ref, st_ref, st):
        @pl.when(pl.program_id(0) == 0)
        def _():
            st[...] = jnp.zeros_like(st)

        for c in range(GLA_CB):
            rows = slice(c * CHUNK, (c + 1) * CHUNK)
            pre = _dot(misc_ref[rows, :], w_ref[...]) + bg_ref[...]
            la = _log_sigmoid(pre) / GLA_TAU
            gc = _dot_hi(tri_ref[...], la)
            gt = gc[CHUNK - 1:CHUNK, :]
            kdec = k_ref[rows, :] * jnp.exp(gt - gc)
            delta = _dot_tn(v_ref[rows, :], kdec.astype(BF16))
            s_new = st[...] * jnp.exp(gt) + delta * mask_ref[...]
            st[...] = s_new
            st_ref[c] = s_new
            o = _dot_nt(q_ref[rows, :], s_new.astype(BF16)) * (64.0 ** -0.5)
            br = br_ref[rows, :]
            for hd in range(4):
                cols = slice(hd * 128, (hd + 1) * 128)
                oh = o[:, cols]
                rs = lax.rsqrt(jnp.mean(oh * oh, axis=-1, keepdims=True) + RMS_EPS)
                brh = br[:, cols]
                yb_ref[rows, cols] = (oh * rs * ng_ref[:, cols] * (brh * _sigmoid(brh))).astype(BF16)

    return _pcall(body, name=name, grid=(t // tm,),
                  in_specs=[_rows(tm, 256, BQ // 256), _rows(tm, 256, BK // 256), _rows(tm, BW, BV // BW),
                            _rows(tm, 128, MISC // 128), _rows(tm, BW, BR // BW), _fix((128, 256)),
                            _fix((1, 256)), _fix((1, BW)), _fix((CHUNK, CHUNK)), _fix((BW, 256))],
                  out_specs=[_rows(tm, BW), pl.BlockSpec((GLA_CB, BW, 256), lambda i: (i, 0, 0))],
                  out_shape=[jax.ShapeDtypeStruct((t, BW), BF16), jax.ShapeDtypeStruct((nc, BW, 256), F32)],
                  scratch_shapes=[pltpu.VMEM((BW, 256), F32)],
                  compiler_params=_cp("arbitrary"))(zb, z, zb, zb, z, wg2p, bg, ng, tri, mask)


def gla_bwd(dyb, z, zb, states, wg2p, bg, ng, *, name):
    t = z.shape[0]
    tm = GLA_CB * CHUNK
    nb = t // tm
    tri, mask = _gla_consts()
    triu = tri.T

    def body(dy_ref, q_ref, k_ref, v_ref, misc_ref, br_ref, st_ref, sp_ref, w_ref, bg_ref, ng_ref,
             tri_ref, triu_ref, mask_ref,
             dq_ref, dk_ref, dv_ref, dbr_ref, dmisc_ref, dpre_ref, dbg_ref, dng_ref, cc):
        i = pl.program_id(0)

        @pl.when(i == 0)
        def _():
            cc[...] = jnp.zeros_like(cc)
            dbg_ref[...] = jnp.zeros_like(dbg_ref)
            dng_ref[...] = jnp.zeros_like(dng_ref)

        last_row = lax.broadcasted_iota(jnp.int32, (CHUNK, 256), 0) == CHUNK - 1
        for c in range(GLA_CB - 1, -1, -1):
            rows = slice(c * CHUNK, (c + 1) * CHUNK)
            pre = _dot(misc_ref[rows, :], w_ref[...]) + bg_ref[...]
            la = _log_sigmoid(pre) / GLA_TAU
            gc = _dot_hi(tri_ref[...], la)
            gt = gc[CHUNK - 1:CHUNK, :]
            eg = jnp.exp(gt - gc)
            kdec = k_ref[rows, :] * eg
            e = jnp.exp(gt)
            s_n = st_ref[c]
            if c > 0:
                s_prev = st_ref[c - 1]
            else:
                s_prev = jnp.where(i == nb - 1, 0.0, sp_ref[0])
            sb = s_n.astype(BF16)
            qb = q_ref[rows, :]
            o = _dot_nt(qb, sb) * (64.0 ** -0.5)
            br = br_ref[rows, :]
            dy = dy_ref[rows, :]
            do_parts = []
            for hd in range(4):
                cols = slice(hd * 128, (hd + 1) * 128)
                oh = o[:, cols]
                rs = lax.rsqrt(jnp.mean(oh * oh, axis=-1, keepdims=True) + RMS_EPS)
                ohat = oh * rs
                brh = br[:, cols]
                sg = _sigmoid(brh)
                dyh = dy[:, cols]
                ngh = ng_ref[:, cols]
                don = dyh * (brh * sg)
                dbr_ref[rows, cols] = (dyh * (ohat * ngh) * sg * (1.0 + brh * (1.0 - sg))).astype(BF16)
                dng_ref[:, cols] += jnp.sum(don * ohat, axis=0, keepdims=True)
                doh = don * ngh
                do_parts.append(rs * (doh - ohat * jnp.mean(doh * ohat, axis=-1, keepdims=True)))
            dob = jnp.concatenate(do_parts, axis=1).astype(BF16)
            dq_ref[rows, :] = (_dot(dob, sb) * (64.0 ** -0.5)).astype(BF16)
            dst = cc[...] + _dot_tn(dob, qb) * (64.0 ** -0.5) * mask_ref[...]
            dsb = dst.astype(BF16)
            dkdec = _dot(v_ref[rows, :], dsb)
            dv_ref[rows, :] = _dot_nt(kdec.astype(BF16), dsb).astype(BF16)
            dgt = jnp.sum(dst * s_prev, axis=0, keepdims=True) * e
            dk_ref[rows, :] = (dkdec * eg).astype(BF16)
            dd = dkdec * kdec
            dgt = dgt + jnp.sum(dd, axis=0, keepdims=True)
            dgc = jnp.where(last_row, dgt - dd, -dd)
            dla = _dot_hi(triu_ref[...], dgc)
            dpre = dla * (1.0 / GLA_TAU) * _sigmoid(-pre)
            dpb = dpre.astype(BF16)
            dpre_ref[rows, :] = dpb
            dmisc_ref[rows, :] = _dot_nt(dpb, w_ref[...])
            dbg_ref[...] += jnp.sum(dpre, axis=0, keepdims=True)
            cc[...] = dst * e

    rev = lambda w, col: pl.BlockSpec((tm, w), lambda i: (nb - 1 - i, col))
    return _pcall(body, name=name, grid=(nb,),
                  in_specs=[rev(BW, 0), rev(256, BQ // 256), rev(256, BK // 256), rev(BW, BV // BW),
                            rev(128, MISC // 128), rev(BW, BR // BW),
                            pl.BlockSpec((GLA_CB, BW, 256), lambda i: (nb - 1 - i, 0, 0)),
                            pl.BlockSpec((1, BW, 256), lambda i: (jnp.maximum((nb - 1 - i) * GLA_CB - 1, 0), 0, 0)),
                            _fix((128, 256)), _fix((1, 256)), _fix((1, BW)),
                            _fix((CHUNK, CHUNK)), _fix((CHUNK, CHUNK)), _fix((BW, 256))],
                  out_specs=[rev(256, 0), rev(256, 0), rev(BW, 0), rev(BW, 0), rev(128, 0), rev(256, 0),
                             _fix((1, 256)), _fix((1, BW))],
                  out_shape=[jax.ShapeDtypeStruct((t, 256), BF16), jax.ShapeDtypeStruct((t, 256), BF16),
                             jax.ShapeDtypeStruct((t, BW), BF16), jax.ShapeDtypeStruct((t, BW), BF16),
                             jax.ShapeDtypeStruct((t, 128), F32), jax.ShapeDtypeStruct((t, 256), BF16),
                             jax.ShapeDtypeStruct((1, 256), F32), jax.ShapeDtypeStruct((1, BW), F32)],
                  scratch_shapes=[pltpu.VMEM((BW, 256), F32)],
                  compiler_params=_cp("arbitrary"))(dyb, zb, z, zb, zb, z, states, states, wg2p, bg, ng,
                                                    tri, triu, mask)


FOX_SCALE = 64.0 ** -0.5
NEG = -1e30


def fox_fcum(z, bfp, *, name):
    t = z.shape[0]
    tm = min(256, t)
    tri = (jnp.arange(tm)[:, None] >= jnp.arange(tm)[None, :]).astype(F32)

    def body(m_ref, b_ref, tri_ref, o_ref, cc):
        @pl.when(pl.program_id(0) == 0)
        def _():
            cc[...] = jnp.zeros_like(cc)

        lf = _log_sigmoid(m_ref[...] + b_ref[...])
        cs = _dot_hi(tri_ref[...], lf) + cc[...]
        o_ref[...] = cs
        cc[...] = cs[tm - 1:tm, :]

    return _pcall(body, name=name, grid=(t // tm,),
                  in_specs=[_rows(tm, 128, MISC // 128), _fix((1, 128)), _fix((tm, tm))],
                  out_specs=_rows(tm, 128), out_shape=jax.ShapeDtypeStruct((t, 128), F32),
                  scratch_shapes=[pltpu.VMEM((1, 128), F32)],
                  compiler_params=_cp("arbitrary"))(z, bfp, tri)


def fox_dcf(dfc, z, bfp, dmisc_g, *, name):
    t = z.shape[0]
    tm = min(256, t)
    nb = t // tm
    triu = (jnp.arange(tm)[:, None] <= jnp.arange(tm)[None, :]).astype(F32)

    def body(d_ref, m_ref, b_ref, g_ref, tri_ref, o_ref, dbf_ref, cc):
        @pl.when(pl.program_id(0) == 0)
        def _():
            cc[...] = jnp.zeros_like(cc)
            dbf_ref[...] = jnp.zeros_like(dbf_ref)

        rc = _dot_hi(tri_ref[...], d_ref[...]) + cc[...]
        cc[...] = rc[0:1, :]
        dcf = rc * _sigmoid(-(m_ref[...] + b_ref[...]))
        o_ref[...] = (dcf + g_ref[...]).astype(BF16)
        dbf_ref[...] += jnp.sum(dcf, axis=0, keepdims=True)

    rev = lambda col: pl.BlockSpec((tm, 128), lambda i: (nb - 1 - i, col))
    return _pcall(body, name=name, grid=(nb,),
                  in_specs=[rev(0), rev(MISC // 128), _fix((1, 128)), rev(0), _fix((tm, tm))],
                  out_specs=[rev(0), _fix((1, 128))],
                  out_shape=[jax.ShapeDtypeStruct((t, 128), BF16), jax.ShapeDtypeStruct((1, 128), F32)],
                  scratch_shapes=[pltpu.VMEM((1, 128), F32)],
                  compiler_params=_cp("arbitrary"))(dfc, z, bfp, dmisc_g, triu)


def fox_delta(dyc, ycf, *, name):
    t = dyc.shape[0]
    tm = min(256, t)
    seg = ((jnp.arange(BW)[:, None] // 64) == jnp.arange(128)[None, :]).astype(F32)

    def body(d_ref, o_ref, s_ref, out_ref):
        out_ref[...] = _dot_hi(d_ref[...] * o_ref[...], s_ref[...])

    return _pcall(body, name=name, grid=(t // tm,),
                  in_specs=[_rows(tm, BW), _rows(tm, BW), _fix((BW, 128))],
                  out_specs=_rows(tm, 128), out_shape=jax.ShapeDtypeStruct((t, 128), F32),
                  compiler_params=_cp("parallel"))(dyc, ycf, seg)


def fox_fwd_t(zb, frow, fkb, *, name):
    t = zb.shape[0]
    tq = min(512, t)
    nq = t // tq
    rep = tq // 128

    pairs = [(i, j) for i in range(nq) for j in range(i + 1)]
    qi_tab = jnp.asarray([p[0] for p in pairs], jnp.int32)
    kj_tab = jnp.asarray([p[1] for p in pairs], jnp.int32)

    def body(qi_ref, kj_ref, q_ref, k_ref, v_ref, fq_ref, fk_ref, y_ref, yf_ref, lse_ref, m_s, l_s, acc):
        step = pl.program_id(1)
        i, j = qi_ref[step], kj_ref[step]

        @pl.when(j == 0)
        def _():
            m_s[...] = jnp.full_like(m_s, NEG)
            l_s[...] = jnp.zeros_like(l_s)
            acc[...] = jnp.zeros_like(acc)

        lo = lax.broadcasted_iota(jnp.int32, (tq, 128), 1) < 64

        def work(diagonal):
            q = q_ref[...]
            k = k_ref[...]
            v = v_ref[...]
            if diagonal:
                key = lax.broadcasted_iota(jnp.int32, (tq, tq), 0)
                qry = lax.broadcasted_iota(jnp.int32, (tq, tq), 1)
                keep = key <= qry
            for hh in range(2):
                sel = lo if hh == 0 else jnp.logical_not(lo)
                qh = jnp.where(sel, q, jnp.zeros_like(q))
                s = _dot_nt(k, qh) + fq_ref[hh] - jnp.tile(fk_ref[hh], (1, rep))
                if diagonal:
                    s = jnp.where(keep, s, NEG)
                m_old = m_s[hh]
                m_new = jnp.maximum(m_old, jnp.max(s, axis=0, keepdims=True))
                p = jnp.exp(s - m_new)
                corr = jnp.exp(m_old - m_new)
                l_s[hh] = l_s[hh] * corr + jnp.sum(p, axis=0, keepdims=True)
                m_s[hh] = m_new
                pv = _dot_tn(v, p.astype(BF16))
                rows = slice(64 * hh, 64 * hh + 64)
                acc[rows, :] = acc[rows, :] * corr + pv[rows, :]

        @pl.when(j < i)
        def _():
            work(False)

        @pl.when(j == i)
        def _():
            work(True)
            first = lax.broadcasted_iota(jnp.int32, (128, tq), 0) < 64
            out = (acc[...] * jnp.where(first, 1.0 / l_s[0], 1.0 / l_s[1])).T
            y_ref[...] = out.astype(BF16)
            yf_ref[...] = out
            lse_ref[...] = m_s[...] + jnp.log(l_s[...])

    kv = lambda off: pl.BlockSpec((tq, 128), lambda h, s, qi, kj: (kj[s], off // 128 + h))
    gs = pltpu.PrefetchScalarGridSpec(
        num_scalar_prefetch=2, grid=(4, len(pairs)),
        in_specs=[pl.BlockSpec((tq, 128), lambda h, s, qi, kj: (qi[s], CQ // 128 + h)), kv(CK), kv(CV),
                  pl.BlockSpec((2, 1, tq), lambda h, s, qi, kj: (h, 0, qi[s])),
                  pl.BlockSpec((2, tq, 128), lambda h, s, qi, kj: (h, kj[s], 0))],
        out_specs=[pl.BlockSpec((tq, 128), lambda h, s, qi, kj: (qi[s], h)),
                   pl.BlockSpec((tq, 128), lambda h, s, qi, kj: (qi[s], h)),
                   pl.BlockSpec((2, 1, tq), lambda h, s, qi, kj: (h, 0, qi[s]))],
        scratch_shapes=[pltpu.VMEM((2, 1, tq), F32), pltpu.VMEM((2, 1, tq), F32), pltpu.VMEM((128, tq), F32)])
    return _pcall(body, name=name, grid_spec=gs,
                  out_shape=[jax.ShapeDtypeStruct((t, BW), BF16), jax.ShapeDtypeStruct((t, BW), F32),
                             jax.ShapeDtypeStruct((FOX_H, 1, t), F32)],
                  compiler_params=_cp("parallel", "arbitrary"))(qi_tab, kj_tab, zb, zb, zb, frow, fkb)


def fox_bwd_t(zb, dyc, frow, fkb, lse, dl, *, name):
    t = zb.shape[0]
    tq = min(512, t)
    nq = t // tq
    rep = tq // 128

    pairs = [(j, i) for j in range(nq) for i in range(j, nq)]
    kj_tab = jnp.asarray([p[0] for p in pairs], jnp.int32)
    qi_tab = jnp.asarray([p[1] for p in pairs], jnp.int32)

    def body(kj_ref, qi_ref, q_ref, k_ref, v_ref, do_ref, fq_ref, fk_ref, lse_ref, dl_ref,
             dq_ref, dk_ref, dv_ref, dfk_ref, dfq_ref, dk_s, dv_s, df_s, dq_s):
        step = pl.program_id(1)
        j, i = kj_ref[step], qi_ref[step]

        @pl.when(step == 0)
        def _():
            dq_s[...] = jnp.zeros_like(dq_s)
            dfq_ref[...] = jnp.zeros_like(dfq_ref)

        @pl.when(i == j)
        def _():
            dk_s[...] = jnp.zeros_like(dk_s)
            dv_s[...] = jnp.zeros_like(dv_s)
            df_s[...] = jnp.zeros_like(df_s)

        lo = lax.broadcasted_iota(jnp.int32, (tq, 128), 1) < 64

        def work(diagonal):
            q = q_ref[...]
            k = k_ref[...]
            v = v_ref[...]
            dob = do_ref[...].astype(BF16)
            if diagonal:
                key = lax.broadcasted_iota(jnp.int32, (tq, tq), 0)
                qry = lax.broadcasted_iota(jnp.int32, (tq, tq), 1)
                keep = key <= qry
            dvs, dks = [], []
            for hh in range(2):
                sel = lo if hh == 0 else jnp.logical_not(lo)
                qh = jnp.where(sel, q, jnp.zeros_like(q))
                doh = jnp.where(sel, dob, jnp.zeros_like(dob))
                p = jnp.exp(_dot_nt(k, qh) + (fq_ref[hh] - lse_ref[hh]) - jnp.tile(fk_ref[hh], (1, rep)))
                if diagonal:
                    p = jnp.where(keep, p, 0.0)
                ds = p * (_dot_nt(v, doh) - dl_ref[hh])
                dsb = ds.astype(BF16)
                dvs.append(_dot(p.astype(BF16), dob))
                dks.append(_dot(dsb, q))
                rows = slice(64 * hh, 64 * hh + 64)
                dq_s[i, rows, :] += _dot_tn(k, dsb)[rows, :]
                part = ds[:, 0:128]
                for r in range(1, rep):
                    part = part + ds[:, 128 * r:128 * (r + 1)]
                df_s[hh] += part
                dfq_ref[hh, i] += jnp.sum(ds, axis=0, keepdims=True)
            dv_s[...] += jnp.where(lo, dvs[0], dvs[1])
            dk_s[...] += jnp.where(lo, dks[0], dks[1])

        @pl.when(i > j)
        def _():
            work(False)

        @pl.when(i == j)
        def _():
            work(True)
            dq_ref[...] = dq_s[i].T.astype(BF16)

        @pl.when(i == nq - 1)
        def _():
            dk_ref[...] = dk_s[...].astype(BF16)
            dv_ref[...] = dv_s[...].astype(BF16)
            for hh in range(2):
                dfk_ref[hh] = -jnp.sum(df_s[hh].T, axis=0, keepdims=True)

    row = lambda: pl.BlockSpec((2, 1, tq), lambda h, s, kj, qi: (h, 0, qi[s]))
    gs = pltpu.PrefetchScalarGridSpec(
        num_scalar_prefetch=2, grid=(4, len(pairs)),
        in_specs=[pl.BlockSpec((tq, 128), lambda h, s, kj, qi: (qi[s], CQ // 128 + h)),
                  pl.BlockSpec((tq, 128), lambda h, s, kj, qi: (kj[s], CK // 128 + h)),
                  pl.BlockSpec((tq, 128), lambda h, s, kj, qi: (kj[s], CV // 128 + h)),
                  pl.BlockSpec((tq, 128), lambda h, s, kj, qi: (qi[s], h)),
                  row(), pl.BlockSpec((2, tq, 128), lambda h, s, kj, qi: (h, kj[s], 0)), row(), row()],
        out_specs=[pl.BlockSpec((tq, 128), lambda h, s, kj, qi: (kj[s], h)),
                   pl.BlockSpec((tq, 128), lambda h, s, kj, qi: (kj[s], h)),
                   pl.BlockSpec((tq, 128), lambda h, s, kj, qi: (kj[s], h)),
                   pl.BlockSpec((2, 1, tq), lambda h, s, kj, qi: (h, 0, kj[s])),
                   pl.BlockSpec((2, nq, 1, tq), lambda h, s, kj, qi: (h, 0, 0, 0))],
        scratch_shapes=[pltpu.VMEM((tq, 128), F32), pltpu.VMEM((tq, 128), F32), pltpu.VMEM((2, tq, 128), F32),
                        pltpu.VMEM((nq, 128, tq), F32)])
    return _pcall(body, name=name, grid_spec=gs,
                  out_shape=[jax.ShapeDtypeStruct((t, BW), BF16), jax.ShapeDtypeStruct((t, BW), BF16),
                             jax.ShapeDtypeStruct((t, BW), BF16), jax.ShapeDtypeStruct((FOX_H, 1, t), F32),
                             jax.ShapeDtypeStruct((FOX_H, nq, 1, tq), F32)],
                  compiler_params=_cp("parallel", "arbitrary"))(kj_tab, qi_tab, zb, zb, zb, dyc, frow, fkb, lse, dl)


def merge_fwd(ya, yb, yc, wbr, z, *, name):
    t = ya.shape[0]
    tm = min(512, t)

    def body(ya_ref, yb_ref, yc_ref, w_ref, g0_ref, g1_ref, g2_ref, o_ref):
        m = _sigmoid(g0_ref[...]) * _dot(ya_ref[...], w_ref[0])
        m = m + _sigmoid(g1_ref[...]) * _dot(yb_ref[...], w_ref[1])
        m = m + _sigmoid(g2_ref[...]) * _dot(yc_ref[...], w_ref[2])
        o_ref[...] = m.astype(BF16)

    return _pcall(body, name=name, grid=(t // tm,),
                  in_specs=[_rows(tm, BW)] * 3 + [_fix((3, BW, D))]
                  + [_rows(tm, D, G0 // D + j) for j in range(3)],
                  out_specs=_rows(tm, D), out_shape=jax.ShapeDtypeStruct((t, D), BF16),
                  compiler_params=_cp("parallel"))(ya, yb, yc, wbr, z, z, z)


def merge_bwd(doutb, wo, l, ya, yb, yc, wbr, z, *, name):
    t = ya.shape[0]
    tm = min(256, t)

    def body(do_ref, wo_ref, ya_ref, yb_ref, yc_ref, w_ref, g0_ref, g1_ref, g2_ref,
             dya_ref, dyb_ref, dyc_ref, dp0_ref, dp1_ref, dp2_ref, dg0_ref, dg1_ref, dg2_ref):
        dm = _dot_nt(do_ref[...], wo_ref[...])
        ys = (ya_ref, yb_ref, yc_ref)
        gs = (g0_ref, g1_ref, g2_ref)
        dys = (dya_ref, dyb_ref, dyc_ref)
        dps = (dp0_ref, dp1_ref, dp2_ref)
        dgs = (dg0_ref, dg1_ref, dg2_ref)
        for j in range(3):
            s = _sigmoid(gs[j][...])
            pj = _dot(ys[j][...], w_ref[j])
            dpb = (dm * s).astype(BF16)
            dps[j][...] = dpb
            dgs[j][...] = (dm * pj * s * (1.0 - s)).astype(BF16)
            dys[j][...] = _dot_nt(dpb, w_ref[j])

    yshape = jax.ShapeDtypeStruct((t, BW), F32)
    dshape = jax.ShapeDtypeStruct((t, D), BF16)
    return _pcall(body, name=name, grid=(t // tm,),
                  in_specs=[_rows(tm, D), _layer(l, (D, D))] + [_rows(tm, BW)] * 3
                  + [_fix((3, BW, D))] + [_rows(tm, D, G0 // D + j) for j in range(3)],
                  out_specs=[_rows(tm, BW)] * 3 + [_rows(tm, D)] * 6,
                  out_shape=[yshape] * 3 + [dshape] * 6,
                  compiler_params=_cp("parallel"))(doutb, wo, ya, yb, yc, wbr, z, z, z)


def adamw(w, g, m, v, *, name):
    nl, r, c = w.shape
    tm = _row_tile(r)

    def body(w_ref, g_ref, m_ref, v_ref, d_ref, mo_ref, vo_ref):
        gg = g_ref[...]
        mn = ADAM_B1 * m_ref[...] + (1.0 - ADAM_B1) * gg
        vn = ADAM_B2 * v_ref[...] + (1.0 - ADAM_B2) * (gg * gg)
        m_hat = mn / (1.0 - ADAM_B1 ** ADAM_STEP)
        v_hat = vn / (1.0 - ADAM_B2 ** ADAM_STEP)
        d_ref[...] = -ADAM_LR * (m_hat / (jnp.sqrt(v_hat) + ADAM_EPS) + ADAM_WD * w_ref[...])
        mo_ref[...] = mn
        vo_ref[...] = vn

    shp = jax.ShapeDtypeStruct((nl, r, c), F32)
    blk = pl.BlockSpec((None, tm, c), lambda l, i: (l, i, 0))
    return _pcall(body, name=name, grid=(nl, r // tm), in_specs=[blk] * 4, out_specs=[blk] * 3,
                  out_shape=[shp] * 3, compiler_params=_cp("parallel", "parallel"))(w, g, m, v)


def _place():
    return lax.axis_index("x"), lax.axis_index("y"), lax.axis_index("c")


def _remote(src, dst, send_sems, recv_sems, k, to):
    return pltpu.make_async_remote_copy(src_ref=src, dst_ref=dst, send_sem=send_sems.at[k],
                                        recv_sem=recv_sems.at[k], device_id=to, device_id_type=MESH)


HBM = pl.BlockSpec(memory_space=pltpu.HBM)
SEM = pl.BlockSpec(memory_space=pltpu.SEMAPHORE)
EFFECT = pltpu.SideEffectType.DATAFLOW_SIDE_EFFECTING


def gather_first(shards):
    n = len(shards)

    def body(*refs):
        ins, outs, lands = refs[:n], refs[n:2 * n], refs[2 * n:3 * n]
        send_sems, recv_sems, own_send, own_recv = refs[3 * n:]
        x, y, c = _place()
        sib = (x, y, 1 - c)
        chips = [(1 - x, y), (x, 1 - y), (1 - x, 1 - y)]
        k_me = 2 * x + y
        mine = []
        for t in range(n):
            mine.append(_remote(ins[t].at[0], outs[t].at[0, k_me], own_send, own_recv, 2 * t, sib))
            mine.append(_remote(ins[t].at[1], lands[t].at[0, k_me], own_send, own_recv, 2 * t + 1, sib))
        for cp in mine:
            cp.start()

        def slot(t, chip):
            return outs[t].at[0, 2 * chip[0] + chip[1]]

        @pl.when(c == 0)
        def _():
            first = [_remote(ins[t].at[0], outs[t].at[0, k_me], send_sems, recv_sems, 6 * t + j, (*chip, 0))
                     for t in range(n) for j, chip in enumerate(chips)]
            for cp in first:
                cp.start()
            passed = []
            for t in range(n):
                for j, chip in enumerate(chips):
                    _remote(slot(t, chip), slot(t, chip), send_sems, recv_sems, 6 * t + j, (*chip, 0)).wait_recv()
                    cp = _remote(slot(t, chip), slot(t, chip), send_sems, recv_sems, 6 * t + 3 + j, sib)
                    cp.start()
                    passed.append(cp)
            for cp in first + passed:
                cp.wait_send()

        @pl.when(c == 1)
        def _():
            for t in range(n):
                for j, chip in enumerate(chips):
                    _remote(slot(t, chip), slot(t, chip), send_sems, recv_sems, 6 * t + 3 + j, sib).wait_recv()

        for cp in mine:
            cp.wait()

    shape = [jax.ShapeDtypeStruct((1, 4) + s.shape[1:], s.dtype) for s in shards]
    out = _pcall(body, name="gather_first", in_specs=[ANY] * n, out_specs=[ANY] * (2 * n), out_shape=shape + shape,
                 scratch_shapes=[pltpu.SemaphoreType.DMA((6 * n,)), pltpu.SemaphoreType.DMA((6 * n,)),
                                 pltpu.SemaphoreType.DMA((2 * n,)), pltpu.SemaphoreType.DMA((2 * n,))])(*shards)
    return out[:n], out[n:]


def _rest_copies(ins, lands, send_sems, recv_sems):
    x, y, c = _place()
    chips = [(1 - x, y), (x, 1 - y), (1 - x, 1 - y)]
    copies, arrivals = [], []
    for t in range(len(ins)):
        for j, chip in enumerate(chips):
            for to in range(2):
                copies.append(pltpu.make_async_remote_copy(
                    src_ref=ins[t].at[1], dst_ref=lands[t].at[0, 2 * x + y], send_sem=send_sems.at[6 * t + 2 * j + to],
                    recv_sem=recv_sems.at[3 * t + j], device_id=(*chip, to), device_id_type=MESH))
            blk = lands[t].at[0, 2 * chip[0] + chip[1]]
            arrivals.append(pltpu.make_async_remote_copy(
                src_ref=blk, dst_ref=blk, send_sem=send_sems.at[6 * t + 2 * j], recv_sem=recv_sems.at[3 * t + j],
                device_id=(*chip, 1), device_id_type=MESH))
    return copies, arrivals


def gather_rest_start(shards, lands):
    n = len(shards)

    def body(*refs):
        ins, lds = refs[:n], refs[n:2 * n]
        send_sems, recv_sems = refs[2 * n], refs[2 * n + 1]
        token = refs[-1]
        copies, _ = _rest_copies(ins, lds, send_sems, recv_sems)

        @pl.when(lax.axis_index("c") == 1)
        def _():
            for cp in copies:
                cp.start()

        token[...] = jnp.zeros_like(token)

    hbm = lambda a: pltpu.with_memory_space_constraint(a, pltpu.HBM)
    out = _pcall(body, name="gather_rest_start", in_specs=[HBM] * (2 * n),
                 out_specs=[SEM, SEM] + [HBM] * (2 * n) + [pl.BlockSpec(memory_space=pltpu.VMEM)],
                 out_shape=[pltpu.SemaphoreType.DMA((6 * n,)), pltpu.SemaphoreType.DMA((3 * n,))]
                 + [pltpu.HBM(a.shape, a.dtype) for a in shards] + [pltpu.HBM(a.shape, a.dtype) for a in lands]
                 + [jax.ShapeDtypeStruct((8, 128), F32)],
                 input_output_aliases={i: 2 + i for i in range(2 * n)},
                 compiler_params=pltpu.CompilerParams(has_side_effects=EFFECT))(
                     *[hbm(a) for a in shards], *[hbm(a) for a in lands])
    return out[0], out[1], out[2:2 + n], out[2 + n:2 + 2 * n], out[-1]


def gather_rest_wait(send_sems, recv_sems, srcs, lands, after):
    n = len(srcs)

    def body(*refs):
        ins, lds = refs[:n], refs[n:2 * n]
        s_sems, r_sems = refs[2 * n], refs[2 * n + 1]
        copies, arrivals = _rest_copies(ins, lds, s_sems, r_sems)

        @pl.when(lax.axis_index("c") == 1)
        def _():
            for cp in copies:
                cp.wait_send()

        for cp in arrivals:
            cp.wait_recv()

    out = _pcall(body, name="gather_rest_wait", in_specs=[HBM] * (2 * n) + [SEM, SEM, ANY],
                 out_specs=[HBM] * (2 * n),
                 out_shape=[pltpu.HBM(a.shape, a.dtype) for a in srcs] + [pltpu.HBM(a.shape, a.dtype) for a in lands],
                 input_output_aliases={i: i for i in range(2 * n)},
                 compiler_params=pltpu.CompilerParams(has_side_effects=EFFECT))(
                     *srcs, *lands, send_sems, recv_sems, after)
    return out[n:]


def pair_send(gl, owner, layer):
    n = len(gl)

    def body(*refs):
        ins, outs = refs[:n], refs[n:2 * n]
        send_sems, recv_sems = refs[2 * n:]
        x, y, c = _place()
        sib = (x, y, 1 - c)
        cps = [_remote(ins[t], outs[t], send_sems, recv_sems, t, sib) for t in range(n)]
        for core in range(2):
            @pl.when(c == core)
            def _():
                for cp in _owned(cps, owner, 1 - core, per=1):
                    cp.start()
                for cp in _owned(cps, owner, 1 - core, per=1):
                    cp.wait_send()
                for cp in _owned(cps, owner, core, per=1):
                    cp.wait_recv()

    return _pcall(body, name="pair_send_l%d" % layer, in_specs=[ANY] * n, out_specs=[ANY] * n,
                  out_shape=[jax.ShapeDtypeStruct(a.shape, a.dtype) for a in gl],
                  scratch_shapes=[pltpu.SemaphoreType.DMA((n,)), pltpu.SemaphoreType.DMA((n,))])(*gl)


def _chip_copies(ins, outs, send_sems, recv_sems):
    x, y, c = _place()
    chips = [(1 - x, y), (x, 1 - y), (1 - x, 1 - y)]
    return [_remote(ins[t].at[2 * chip[0] + chip[1]], outs[t].at[j], send_sems, recv_sems, 3 * t + j, (*chip, c))
            for t in range(len(ins)) for j, chip in enumerate(chips)]


def _owned(cps, owner, core, per=3):
    return [cp for k, cp in enumerate(cps) if owner[k // per] == core]


def chip_send(s1, owner, layer):
    n = len(s1)

    def body(*refs):
        ins, outs = refs[:n], refs[n:2 * n]
        send_sems, recv_sems = refs[2 * n:]
        cps = _chip_copies(ins, outs, send_sems, recv_sems)
        for core in range(2):
            @pl.when(lax.axis_index("c") == core)
            def _():
                for cp in _owned(cps, owner, core):
                    cp.start()
                for cp in _owned(cps, owner, core):
                    cp.wait()

    return _pcall(body, name="chip_send_l%d" % layer, in_specs=[ANY] * n, out_specs=[ANY] * n,
                  out_shape=[jax.ShapeDtypeStruct((3,) + a.shape[1:], a.dtype) for a in s1],
                  scratch_shapes=[pltpu.SemaphoreType.DMA((3 * n,)), pltpu.SemaphoreType.DMA((3 * n,))])(*s1)


def chip_send_start(s1, owner, layer):
    n = len(s1)
    land = [lax.empty((3,) + a.shape[1:], a.dtype) for a in s1]

    def body(*refs):
        ins, lands = refs[:n], refs[n:2 * n]
        send_sems, recv_sems = refs[2 * n], refs[2 * n + 1]
        token = refs[-1]
        cps = _chip_copies(ins, lands, send_sems, recv_sems)
        for core in range(2):
            @pl.when(lax.axis_index("c") == core)
            def _():
                for cp in _owned(cps, owner, core):
                    cp.start()

        token[...] = jnp.zeros_like(token)

    hbm = lambda a: pltpu.with_memory_space_constraint(a, pltpu.HBM)
    out = _pcall(body, name="chip_send_start_l%d" % layer, in_specs=[HBM] * (2 * n),
                 out_specs=[SEM, SEM] + [HBM] * (2 * n) + [pl.BlockSpec(memory_space=pltpu.VMEM)],
                 out_shape=[pltpu.SemaphoreType.DMA((3 * n,)), pltpu.SemaphoreType.DMA((3 * n,))]
                 + [pltpu.HBM(a.shape, a.dtype) for a in s1] + [pltpu.HBM(a.shape, a.dtype) for a in land]
                 + [jax.ShapeDtypeStruct((8, 128), F32)],
                 input_output_aliases={i: 2 + i for i in range(2 * n)},
                 compiler_params=pltpu.CompilerParams(has_side_effects=EFFECT))(
                     *[hbm(a) for a in s1], *[hbm(a) for a in land])
    return out[0], out[1], out[2:2 + n], out[2 + n:2 + 2 * n], out[-1]


def chip_send_wait(send_sems, recv_sems, srcs, lands, after, owner, layer):
    n = len(srcs)

    def body(*refs):
        ins, lds = refs[:n], refs[n:2 * n]
        s_sems, r_sems = refs[2 * n], refs[2 * n + 1]
        cps = _chip_copies(ins, lds, s_sems, r_sems)
        for core in range(2):
            @pl.when(lax.axis_index("c") == core)
            def _():
                for cp in _owned(cps, owner, core):
                    cp.wait_send()
                    cp.wait_recv()

    out = _pcall(body, name="chip_send_wait_l%d" % layer, in_specs=[HBM] * (2 * n) + [SEM, SEM, ANY],
                 out_specs=[HBM] * (2 * n),
                 out_shape=[pltpu.HBM(a.shape, a.dtype) for a in srcs] + [pltpu.HBM(a.shape, a.dtype) for a in lands],
                 input_output_aliases={i: i for i in range(2 * n)},
                 compiler_params=pltpu.CompilerParams(has_side_effects=EFFECT))(
                     *srcs, *lands, send_sems, recv_sems, after)
    return out[n:]


def pair_share(s2, owner):
    n = len(s2)

    def body(*refs):
        ins, outs = refs[:n], refs[n:2 * n]
        send_sems, recv_sems = refs[2 * n:]
        x, y, c = _place()
        sib = (x, y, 1 - c)
        cps = [_remote(ins[t], outs[t], send_sems, recv_sems, t, sib) for t in range(n)]
        for core in range(2):
            @pl.when(c == core)
            def _():
                for cp in _owned(cps, owner, core, per=1):
                    cp.start()
                for cp in _owned(cps, owner, core, per=1):
                    cp.wait_send()
                for cp in _owned(cps, owner, 1 - core, per=1):
                    cp.wait_recv()

    return _pcall(body, name="pair_share", in_specs=[ANY] * n, out_specs=[ANY] * n,
                  out_shape=[jax.ShapeDtypeStruct(a.shape, a.dtype) for a in s2],
                  input_output_aliases={t: t for t in range(n)},
                  scratch_shapes=[pltpu.SemaphoreType.DMA((n,)), pltpu.SemaphoreType.DMA((n,))])(*s2)


def small_exchange(gs):
    rows, width = gs.shape

    def body(g_ref, o_ref, send_sems, recv_sems):
        x, y, c = _place()
        cps = []
        for r in range(1, 8):
            dx, dy, dc = (r >> 2) & 1, (r >> 1) & 1, r & 1
            to = (x if dx == 0 else 1 - x, y if dy == 0 else 1 - y, c if dc == 0 else 1 - c)
            cps.append(_remote(g_ref, o_ref.at[r - 1], send_sems, recv_sems, r - 1, to))
        for cp in cps:
            cp.start()
        for cp in cps:
            cp.wait()

    return _pcall(body, name="small_exchange", in_specs=[ANY], out_specs=ANY,
                  out_shape=jax.ShapeDtypeStruct((7, rows, width), gs.dtype),
                  scratch_shapes=[pltpu.SemaphoreType.DMA((7,)), pltpu.SemaphoreType.DMA((7,))])(gs)


def _row_tile(rows):
    return _pick(rows, (256, 352, 128, 64, 32, 16))


def pair_add_layer(g, rb, core, owner, *, name):
    _, rows, width = g.shape
    tr = _row_tile(rows)

    def body(c_ref, g_ref, r_ref, o_ref, ob_ref):
        @pl.when(c_ref[0] == owner)
        def _():
            s = g_ref[...] + r_ref[...]
            o_ref[...] = s
            ob_ref[...] = s.astype(BF16)

    def at(k, i, c_ref):
        mine = c_ref[0] == owner
        return (jnp.where(mine, k, 0), jnp.where(mine, i, 0), 0)

    blk = pl.BlockSpec((None, tr, width), at)
    gs = pltpu.PrefetchScalarGridSpec(num_scalar_prefetch=1, grid=(4, rows // tr), in_specs=[blk, blk],
                                      out_specs=[blk, blk])
    return _pcall(body, name=name, grid_spec=gs,
                  out_shape=[jax.ShapeDtypeStruct(g.shape, F32), jax.ShapeDtypeStruct(g.shape, BF16)],
                  compiler_params=_cp("parallel", "parallel"))(core, g, rb)


def chip_add_layers(s1, rb2, chip, core, owner, *, name):
    _, rows, width = s1[0].shape
    tr = _row_tile(rows)

    def body(k_ref, c_ref, s0_ref, s1_ref, r0_ref, r1_ref, o_ref):
        @pl.when(c_ref[0] == owner)
        def _():
            first = pl.program_id(0) == 0
            s = jnp.where(first, s0_ref[...], s1_ref[...])
            r = jnp.where(first, r0_ref[...], r1_ref[...]).astype(F32)
            o_ref[...] = ((s + r[0]) + r[1]) + r[2]

    def s_spec(layer):
        def at(l, i, k_ref, c_ref):
            use = jnp.logical_and(l == layer, c_ref[0] == owner)
            return (jnp.where(use, k_ref[0], 0), jnp.where(use, i, 0), 0)
        return pl.BlockSpec((None, tr, width), at)

    def r_spec(layer):
        def at(l, i, k_ref, c_ref):
            return (0, jnp.where(jnp.logical_and(l == layer, c_ref[0] == owner), i, 0), 0)
        return pl.BlockSpec((3, tr, width), at)

    def out_at(l, i, k_ref, c_ref):
        mine = c_ref[0] == owner
        return (jnp.where(mine, l, 0), jnp.where(mine, i, 0), 0)

    gs = pltpu.PrefetchScalarGridSpec(
        num_scalar_prefetch=2, grid=(DEPTH, rows // tr),
        in_specs=[s_spec(0), s_spec(1), r_spec(0), r_spec(1)],
        out_specs=pl.BlockSpec((None, tr, width), out_at))
    return _pcall(body, name=name, grid_spec=gs, out_shape=jax.ShapeDtypeStruct((DEPTH, rows, width), F32),
                  compiler_params=_cp("parallel", "parallel"))(chip, core, s1[0], s1[1], rb2[0], rb2[1])


def small_add(gs_own, slots, me):
    rows, width = gs_own.shape
    tr = _pick(rows, (64, 32, 16, 8))

    def body(me_ref, g_ref, s_ref, o_ref):
        me_v = me_ref[0]
        total = None
        for d in range(8):
            rel = jnp.bitwise_xor(me_v, d)
            val = jnp.where(rel == 0, g_ref[...], s_ref[jnp.maximum(rel - 1, 0)])
            total = val if total is None else total + val
        o_ref[...] = total

    gs = pltpu.PrefetchScalarGridSpec(
        num_scalar_prefetch=1, grid=(rows // tr,),
        in_specs=[pl.BlockSpec((tr, width), lambda i, m_ref: (i, 0)),
                  pl.BlockSpec((7, tr, width), lambda i, m_ref: (0, i, 0))],
        out_specs=pl.BlockSpec((tr, width), lambda i, m_ref: (i, 0)))
    return _pcall(body, name="small_add", grid_spec=gs, out_shape=jax.ShapeDtypeStruct((rows, width), F32),
                  compiler_params=_cp("parallel"))(me, gs_own, slots)


SHARDED = (("ffn1_w_up", (D, UPW)), ("ffn1_w_down", (DFF // 4, D)), ("w_in", (D, D_IN // 4)),
           ("conv_w", (4, BW // 4)), ("gla_w_g2", (LOW_W, 64)), ("w_branch", (3 * BW, D // 4)),
           ("w_out", (D // 4, D)), ("ffn2_w_up", (D, UPW)), ("ffn2_w_down", (DFF // 4, D)),
           ("ple_w_proj", (PLE, D // 4)), ("ple_w_gate", (D // 4, D)))
OWNER = tuple(0 if n in ("ffn1_w_up", "w_in", "w_out") else 1 for n, _ in SHARDED)
SMALL = ("ln1_g", "ln1_b", "conv_b", "lru_wa", "lru_ba", "lru_wx", "lru_bx", "lru_lambda", "gla_b_g",
         "gla_norm_g", "fox_b_f", "ln2_g", "ln2_b", "ln3_g", "ln3_b", "ple_b_gate", "ln4_g", "ln4_b")
WEIGHTS = ('ffn1_w_up', 'ffn1_w_down', 'ln1_g', 'ln1_b', 'w_in', 'conv_w', 'conv_b', 'lru_wa', 'lru_ba',
           'lru_wx', 'lru_bx', 'lru_lambda', 'gla_w_g2', 'gla_b_g', 'gla_norm_g', 'fox_b_f', 'w_branch',
           'w_out', 'ln2_g', 'ln2_b', 'ffn2_w_up', 'ffn2_w_down', 'ln3_g', 'ln3_b', 'ple_w_proj',
           'ple_w_gate', 'ple_b_gate', 'ln4_g', 'ln4_b')


def _cols_join(parts):
    return jnp.concatenate([parts[k] for k in range(4)], axis=-1)


def _cols_split(full):
    r, c4 = full.shape
    return full.reshape(r, 4, c4 // 4).transpose(1, 0, 2)


def _regroup_in(w):
    pad = jnp.zeros(w.shape[:-1] + (ZW - D_IN,), w.dtype)
    fox_q = (w[..., 2576:3088] * FOX_SCALE).astype(w.dtype)
    return jnp.concatenate([w[..., 0:2048], w[..., 2064:2576], fox_q, w[..., 3088:4112], w[..., 4120:7192],
                            w[..., 2048:2064], w[..., 4112:4120], pad], axis=-1)


_IN_RUNS = ((0, 2048, 0, 1.0), (2048, 2064, 7168, 1.0), (2064, 2576, 2048, 1.0), (2576, 3088, CQ, FOX_SCALE),
            (3088, 4112, CK, 1.0), (4112, 4120, 7184, 1.0), (4120, D_IN, 4096, 1.0))


def _regroup_out_shards(g):
    w = D_IN // 4
    shards = []
    for k in range(4):
        pieces = []
        for a, b, new, f in _IN_RUNS:
            lo, hi = max(a, k * w), min(b, (k + 1) * w)
            if lo < hi:
                piece = g[:, new + lo - a:new + hi - a]
                pieces.append(piece if f == 1.0 else piece * f)
        shards.append(jnp.concatenate(pieces, axis=1))
    return jnp.stack(shards)


def _block_diag(w):
    eye = jnp.eye(8, dtype=w.dtype)
    return (eye[:, None, :, None] * w[:, :, None, :]).reshape(BW, BW)


def _diag_blocks(dense):
    return jnp.stack([dense[64 * n:64 * (n + 1), 64 * n:64 * (n + 1)] for n in range(8)])


def _layer_weights(gw, small, l):
    w = {"up1": gw["ffn1_w_up"], "up2": gw["ffn2_w_up"],
         "dn1": gw["ffn1_w_down"].reshape(1, DFF, D), "dn2": gw["ffn2_w_down"].reshape(1, DFF, D),
         "wo": gw["w_out"].reshape(1, D, D), "wgt": gw["ple_w_gate"].reshape(1, D, D)}
    w["win"] = _regroup_in(_cols_join(gw["w_in"][0]))
    w["cw"] = _cols_join(gw["conv_w"][0])
    w["wa"] = _block_diag(small["lru_wa"][l]).astype(BF16)
    w["wx"] = _block_diag(small["lru_wx"][l]).astype(BF16)
    w["wg2p"] = jnp.pad(_cols_join(gw["gla_w_g2"][0]), ((0, 128 - LOW_W), (0, 0)))
    w["wbr"] = _cols_join(gw["w_branch"][0].reshape(4, 3, BW, D // 4))
    w["wp"] = _cols_join(gw["ple_w_proj"][0])
    for n in ("ln1_g", "ln1_b", "ln2_g", "ln2_b", "ln3_g", "ln3_b", "ln4_g", "ln4_b", "conv_b", "lru_ba",
              "lru_bx", "lru_lambda", "gla_b_g", "gla_norm_g", "ple_b_gate"):
        w[n] = small[n][l][None, :]
    w["bfp"] = jnp.pad(small["fox_b_f"][l], (LOW_W, 128 - LOW_W - FOX_H))[None, :]
    return w


def _heads_t(a):
    ht = a[:, LOW_W:LOW_W + FOX_H].T
    return ht[:, None, :], jnp.broadcast_to(ht[:, :, None], ht.shape + (128,))


def _layer_fwd(x, xb, pb, w, l):
    s = {"x0": x, "x0b": xb}
    tag = "l%d_" % l
    gate, up, act = ffn_up(xb, w["up1"], 0, name=tag + "ffn1_up")
    r1, x1, x1b = matmul_res_ln(act, w["dn1"], 0, x, w["ln1_g"], w["ln1_b"], mm_scale=0.5, name=tag + "ffn1_down")
    s.update(gate1=gate, up1=up, act1=act, r1=r1, x1=x1, x1b=x1b)
    z, zb = matmul(x1b, w["win"], also_bf16=True, tm=1024, tn=_pick(ZW, (2432,)), name=tag + "mix_in")
    xc, xcb, h, ya = lru_fwd(z, w["cw"], w["conv_b"], w["wa"], w["wx"], w["lru_ba"], w["lru_bx"],
                             w["lru_lambda"], name=tag + "lru_fwd")
    yb, states = gla_fwd(z, zb, w["wg2p"], w["gla_b_g"], w["gla_norm_g"], name=tag + "gla_fwd")
    fcum = fox_fcum(z, w["bfp"], name=tag + "fox_fcum")
    fq, fk = _heads_t(fcum)
    yc, ycf, lse = fox_fwd_t(zb, fq, fk, name=tag + "fox_fwd")
    merged = merge_fwd(ya, yb, yc, w["wbr"], z, name=tag + "merge_fwd")
    r2, x2, x2b = matmul_res_ln(merged, w["wo"], 0, x1, w["ln2_g"], w["ln2_b"], mm_scale=1.0, name=tag + "mix_out")
    s.update(z=z, zb=zb, xc=xc, xcb=xcb, h=h, ya=ya, yb=yb, states=states, fq=fq, fk=fk, yc=yc, ycf=ycf,
             lse=lse, merged=merged, r2=r2, x2=x2, x2b=x2b)
    gate, up, act = ffn_up(x2b, w["up2"], 0, name=tag + "ffn2_up")
    r3, x3, x3b = matmul_res_ln(act, w["dn2"], 0, x2, w["ln3_g"], w["ln3_b"], mm_scale=0.5, name=tag + "ffn2_down")
    s.update(gate2=gate, up2=up, act2=act, r3=r3, x3=x3, x3b=x3b)
    r4, x4, x4b = ple_fwd(x3b, x3, pb, w["wgt"], 0, w["wp"], w["ple_b_gate"], w["ln4_g"], w["ln4_b"],
                          name=tag + "ple_fwd")
    s.update(r4=r4, pb=pb)
    return x4, x4b, s


def _ffn_bwd(dy, s, w, n, xin_b, l, tag):
    k = {"1": ("r1", "ln1_g", "gate1", "up1", "act1"), "2": ("r3", "ln3_g", "gate2", "up2", "act2")}[n]
    dr, dfb, dg, db = ln_bwd(dy, s[k[0]], w[k[1]], out_scale=0.5, name=tag + "ln_bwd")
    dgate, dup = ffn_down_bwd(dfb, w["dn" + n], 0, s[k[2]], s[k[3]], name=tag + "down_bwd")
    dx = ffn_dx(dgate, dup, w["up" + n], 0, dr, name=tag + "dx")
    dwup = matmul_tn_up(xin_b, dgate, dup, name=tag + "dw_up")
    dwdn = matmul_tn(s[k[4]], dfb, name=tag + "dw_down").reshape(4, DFF // 4, D)
    return dx, dwup, dwdn, dg[0], db[0]


def _layer_bwd(dy, s, w, l):
    g = {}
    tag = "l%d_" % l
    dr4, dglb, dpeb, dg4, db4, dbg = ple_bwd(dy, s["r4"], s["x3b"], s["pb"], w["wgt"], 0, w["wp"], w["ple_b_gate"],
                                             w["ln4_g"], name=tag + "ple_bwd")
    dx3 = matmul(dglb, w["wgt"], nt=True, b_lead=(0,), res=dr4, res_scale=ALPHA, tm=1024, tn=1024,
                 name=tag + "ple_dx")
    g["ple_w_gate"] = matmul_tn(s["x3b"], dglb, name=tag + "ple_dw_gate").reshape(4, D // 4, D)
    g["ple_w_proj"] = _cols_split(matmul_tn(s["pb"], dpeb, name=tag + "ple_dw_proj"))
    g["ln4_g"], g["ln4_b"], g["ple_b_gate"] = dg4[0], db4[0], dbg[0]
    dx2, g["ffn2_w_up"], g["ffn2_w_down"], g["ln3_g"], g["ln3_b"] = _ffn_bwd(dx3, s, w, "2", s["x2b"], l,
                                                                             tag + "ffn2_")
    dr2, doutb, dg2, db2 = ln_bwd(dx2, s["r2"], w["ln2_g"], out_scale=1.0, name=tag + "mix_ln_bwd")
    g["ln2_g"], g["ln2_b"] = dg2[0], db2[0]
    g["w_out"] = matmul_tn(s["merged"], doutb, name=tag + "dw_out").reshape(4, D // 4, D)
    z, zb = s["z"], s["zb"]
    (dya, dyb, dyc, dp0, dp1, dp2, dgl0, dgl1, dgl2) = merge_bwd(
        doutb, w["wo"], 0, s["ya"], s["yb"], s["yc"], w["wbr"], z, name=tag + "merge_bwd")
    dwbr = jnp.stack([matmul_tn(s["ya"], dp0, name=tag + "dw_br0"), matmul_tn(s["yb"], dp1, name=tag + "dw_br1"),
                      matmul_tn(s["yc"], dp2, name=tag + "dw_br2")])
    g["w_branch"] = _cols_split(dwbr.reshape(3 * BW, D))
    day, dxc, dprb, dpib, dba, dbx, dlam = lru_bwd(dya, z, s["h"], s["xc"], w["wa"], w["wx"],
                                                   w["lru_ba"], w["lru_bx"], w["lru_lambda"], name=tag + "lru_bwd")
    dax, dcw, dcb = conv_bwd(dxc, z, w["cw"], name=tag + "conv_bwd")
    g["lru_wa"] = _diag_blocks(matmul_tn(s["xcb"], dprb, name=tag + "dw_lru_a"))
    g["lru_wx"] = _diag_blocks(matmul_tn(s["xcb"], dpib, name=tag + "dw_lru_x"))
    g["lru_ba"], g["lru_bx"], g["lru_lambda"] = dba[0], dbx[0], dlam[0]
    g["conv_w"], g["conv_b"] = _cols_split(dcw), dcb[0]
    dbq, dbk, dbv, dbr, dmisc_g, dpreb, dbgg, dng = gla_bwd(dyb, z, zb, s["states"], w["wg2p"], w["gla_b_g"],
                                                            w["gla_norm_g"], name=tag + "gla_bwd")
    miscb = zb[:, MISC:]
    g["gla_w_g2"] = _cols_split(matmul_tn(miscb, dpreb, name=tag + "dw_g2")[:LOW_W])
    g["gla_b_g"], g["gla_norm_g"] = dbgg[0], dng[0]
    dl = fox_delta(dyc, s["ycf"], name=tag + "fox_delta")
    t = z.shape[0]
    dlq = dl[:, :FOX_H].T[:, None, :]
    dcq, dck, dcv, dfk, dfq = fox_bwd_t(zb, dyc, s["fq"], s["fk"], s["lse"], dlq, name=tag + "fox_bwd")
    dfc = jnp.pad((dfk[:, 0, :] + dfq.reshape(FOX_H, t)).T, ((0, 0), (LOW_W, 128 - LOW_W - FOX_H)))
    dmiscb, dbf = fox_dcf(dfc, z, w["bfp"], dmisc_g, name=tag + "fox_dcf")
    g["fox_b_f"] = dbf[0, LOW_W:LOW_W + FOX_H]
    dz = jnp.concatenate([dax, day, dbq, dbk, dbv, dbr, dcq, dck, dcv, dgl0, dgl1, dgl2, dmiscb], axis=1)
    dx1 = matmul(dz, w["win"], nt=True, res=dr2, res_scale=ALPHA, tm=1024, tn=1024, tk=_pick(ZW, (2432,)),
                 name=tag + "mix_dx")
    g["w_in"] = _regroup_out_shards(matmul_tn(s["x1b"], dz, name=tag + "dw_in"))
    dx0, g["ffn1_w_up"], g["ffn1_w_down"], g["ln1_g"], g["ln1_b"] = _ffn_bwd(dx1, s, w, "1", s["x0b"], l,
                                                                             tag + "ffn1_")
    return dx0, g


def _local_step(x, p, target, gathered, small, after_last_layer=None):
    xcur = x
    xb = xcur.astype(BF16)
    layer_w, saved = [], []
    for l in range(DEPTH):
        w = _layer_weights(gathered(l, xcur), small, l)
        xcur, xb, s = _layer_fwd(xcur, xb, p[l].astype(BF16), w, l)
        layer_w.append(w)
        saved.append(s)
    dy, sq = loss_head(xcur, target, name="loss_head")
    grads = [None] * DEPTH
    for l in reversed(range(DEPTH)):
        dy, grads[l] = _layer_bwd(dy, saved[l], layer_w[l], l)
        if l == DEPTH - 1 and after_last_layer is not None:
            layer_w[l - 1]["ln4_g"] = layer_w[l - 1]["ln4_g"] + after_last_layer(grads[l])
    return 0.5 * jnp.sum(sq) / float(D), dy, grads


def kernel(x, p, ffn1_w_up, ffn1_w_down, ln1_g, ln1_b, w_in, conv_w, conv_b, lru_wa, lru_ba, lru_wx, lru_bx, lru_lambda, gla_w_g2, gla_b_g, gla_norm_g, fox_b_f, w_branch, w_out, ln2_g, ln2_b, ffn2_w_up, ffn2_w_down, ln3_g, ln3_b, ple_w_proj, ple_w_gate, ple_b_gate, ln4_g, ln4_b, loss_target, m_ffn1_w_up, m_ffn1_w_down, m_ln1_g, m_ln1_b, m_w_in, m_conv_w, m_conv_b, m_lru_wa, m_lru_ba, m_lru_wx, m_lru_bx, m_lru_lambda, m_gla_w_g2, m_gla_b_g, m_gla_norm_g, m_fox_b_f, m_w_branch, m_w_out, m_ln2_g, m_ln2_b, m_ffn2_w_up, m_ffn2_w_down, m_ln3_g, m_ln3_b, m_ple_w_proj, m_ple_w_gate, m_ple_b_gate, m_ln4_g, m_ln4_b, v_ffn1_w_up, v_ffn1_w_down, v_ln1_g, v_ln1_b, v_w_in, v_conv_w, v_conv_b, v_lru_wa, v_lru_ba, v_lru_wx, v_lru_bx, v_lru_lambda, v_gla_w_g2, v_gla_b_g, v_gla_norm_g, v_fox_b_f, v_w_branch, v_w_out, v_ln2_g, v_ln2_b, v_ffn2_w_up, v_ffn2_w_down, v_ln3_g, v_ln3_b, v_ple_w_proj, v_ple_w_gate, v_ple_b_gate, v_ln4_g, v_ln4_b):
    args = dict(locals())
    wts = {n: args[n] for n in WEIGHTS}
    mom = {n: args["m_" + n] for n in WEIGHTS}
    var = {n: args["v_" + n] for n in WEIGHTS}
    cx, cy, cc = lax.axis_index("x"), lax.axis_index("y"), lax.axis_index("c")

    names = [n for n, _ in SHARDED]
    shards = [wts[n].reshape((DEPTH,) + rc).astype(F32 if n == "conv_w" else BF16) for n, rc in SHARDED]
    first, lands = gather_first(shards)
    rest_send, rest_recv, rest_srcs, rest_lands, rest_token = gather_rest_start(shards, lands)
    small = {n: wts[n] for n in SMALL}
    small["ln1_g"] = small["ln1_g"] + rest_token[0, 0]

    def gathered(l, after):
        if l == 0:
            return dict(zip(names, first))
        return dict(zip(names, gather_rest_wait(rest_send, rest_recv, rest_srcs, rest_lands, after)))

    flight = {}
    core = jnp.reshape(cc, (1,)).astype(jnp.int32)
    chip = jnp.reshape(2 * cx + cy, (1,)).astype(jnp.int32)

    def chip_sum(gl, layer):
        lst = [gl[n] for n in names]
        rb = pair_send(lst, OWNER, layer)
        return [pair_add_layer(a, r, core, own, name="pair_add_l%d_%s" % (layer, n))
                for n, own, a, r in zip(names, OWNER, lst, rb)]

    def start_last_layer(gl):
        s1 = chip_sum(gl, DEPTH - 1)
        send_sems, recv_sems, srcs, lands, token = chip_send_start([sb for _, sb in s1], OWNER, DEPTH - 1)
        flight.update(s1=[sf for sf, _ in s1], sems=(send_sems, recv_sems), srcs=srcs, lands=lands)
        return token[0, 0]

    loss_local, dx, grads = _local_step(x[0], p[:, 0], loss_target[0], gathered, small, start_last_layer)
    loss = lax.psum(loss_local, ("x", "y", "c"))
    grad_x = dx[None]

    pieces, spans, row = [], {}, 0
    for n in SMALL:
        flat = jnp.stack([grads[l][n] for l in range(DEPTH)]).reshape(-1)
        rows = -(-flat.shape[0] // (8 * PACK_W)) * 8
        pieces.append(jnp.pad(flat, (0, rows * PACK_W - flat.shape[0])).reshape(rows, PACK_W))
        spans[n] = (row, rows)
        row += rows
    gs = jnp.concatenate(pieces, axis=0)
    gs_all = small_exchange(gs)

    s1_first = chip_sum(grads[0], 0)
    first_send, first_recv, first_srcs, first_lands, first_token = chip_send_start(
        [sb for _, sb in s1_first], OWNER, 0)
    me = jnp.reshape(4 * cx + 2 * cy + cc, (1,)).astype(jnp.int32)
    gsum = small_add(gs + first_token[0, 0], gs_all, me)

    gout, delta, new_m, new_v = {}, {}, {}, {}

    def update(n, view, g):
        shp = wts[n].shape
        d, mn, vn = adamw(wts[n].reshape(view), g, mom[n].reshape(view), var[n].reshape(view), name="adamw_" + n)
        gout[n], delta[n], new_m[n], new_v[n] = g.reshape(shp), d.reshape(shp), mn.reshape(shp), vn.reshape(shp)

    for n in SMALL:
        view = (1, DEPTH, wts[n].size // DEPTH)
        r0, rows = spans[n]
        update(n, view, gsum[r0:r0 + rows].reshape(-1)[:wts[n].size].reshape(view))

    rb2_first = chip_send_wait(first_send, first_recv, first_srcs, first_lands, new_v[SMALL[-1]], OWNER, 0)
    rb2_last = chip_send_wait(*flight["sems"], flight["srcs"], flight["lands"], dx, OWNER, DEPTH - 1)
    s2 = [chip_add_layers((sf0, sf1), (r0, r1), chip, core, own, name="chip_add_" + n)
          for n, own, (sf0, _), sf1, r0, r1 in zip(names, OWNER, s1_first, flight["s1"], rb2_first, rb2_last)]
    for n, gsh in zip(names, pair_share(s2, OWNER)):
        update(n, gsh.shape, gsh)

    return (loss, grad_x, *[gout[n] for n in WEIGHTS], *[delta[n] for n in WEIGHTS],
            *[new_m[n] for n in WEIGHTS], *[new_v[n] for n in WEIGHTS])
```

```python
import functools
import math

import jax
import jax.numpy as jnp
from jax import lax
from jax.experimental import pallas as pl
from jax.experimental.pallas import tpu as pltpu

F32 = jnp.float32
BF16 = jnp.bfloat16

D = 1024
DFF = 2816
BW = 512
PLE = 256
DEPTH = 2
ALPHA = (2 * DEPTH) ** 0.25
LN_EPS = 1e-5
RMS_EPS = 1e-6
LRU_C = 8.0
GLA_TAU = 16.0
CHUNK = 64
D_IN = 7192
ZW = 7296
AX, AY, BQ, BK, BV, BR, CQ, CK, CV, G0, MISC = 0, 512, 1024, 1280, 1536, 2048, 2560, 3072, 3584, 4096, 7168
LOW_W, FOX_H = 16, 8
ADAM_LR, ADAM_B1, ADAM_B2, ADAM_EPS, ADAM_WD, ADAM_STEP = 0.001, 0.9, 0.999, 1e-08, 0.01, 10
PACK_W = 1024
VMEM_LIMIT = 56 << 20

MESH = pl.DeviceIdType.MESH
ANY = pl.BlockSpec(memory_space=pl.ANY)


def _pcall(body, **kw):
    return pl.pallas_call(body, **kw)


def _cp(*dims):
    return pltpu.CompilerParams(dimension_semantics=dims, vmem_limit_bytes=VMEM_LIMIT)


def _dot(a, b):
    return jnp.dot(a, b, preferred_element_type=F32)


def _dot_nt(a, b):
    return lax.dot_general(a, b, (((1,), (1,)), ((), ())), preferred_element_type=F32)


def _dot_tn(a, b):
    return lax.dot_general(a, b, (((0,), (0,)), ((), ())), preferred_element_type=F32)


def _dot_hi(a, b):
    return jnp.dot(a, b, preferred_element_type=F32, precision=lax.Precision.HIGHEST)


def _sigmoid(x):
    return 1.0 / (1.0 + jnp.exp(-x))


def _softplus(x):
    return jnp.maximum(x, 0.0) + jnp.log(1.0 + jnp.exp(-jnp.abs(x)))


def _log_sigmoid(x):
    return -_softplus(-x)


def _expm1(x):
    poly = x * (1.0 + x * (0.5 + x * (1.0 / 6.0 + x * (1.0 / 24.0 + x * (1.0 / 120.0 + x * (1.0 / 720.0))))))
    return jnp.where(jnp.abs(x) < 0.1, poly, jnp.exp(x) - 1.0)


_GELU_C = math.sqrt(2.0 / math.pi)


def _gelu(x):
    return 0.5 * x * (1.0 + jnp.tanh(_GELU_C * (x + 0.044715 * x * x * x)))


def _gelu_grad(x):
    t = jnp.tanh(_GELU_C * (x + 0.044715 * x * x * x))
    return 0.5 * (1.0 + t) + 0.5 * x * (1.0 - t * t) * _GELU_C * (1.0 + 3.0 * 0.044715 * x * x)


def _ln_stats(r):
    mu = jnp.mean(r, axis=-1, keepdims=True)
    xc = r - mu
    var = jnp.mean(xc * xc, axis=-1, keepdims=True)
    return xc, lax.rsqrt(var + LN_EPS)


def _pick(n, cands):
    for c in cands:
        if n % c == 0:
            return c
    return n


def _rows(tm, w, col=0):
    return pl.BlockSpec((tm, w), lambda i: (i, col))


def _fix(shape):
    nd = len(shape)
    return pl.BlockSpec(shape, lambda i: (0,) * nd)


def _col_chunks(n, width=256):
    return [slice(c, min(c + width, n)) for c in range(0, n, width)]


def _layer(l, shape):
    nd = len(shape)
    return pl.BlockSpec((None,) + tuple(shape), lambda i: (l,) + (0,) * nd)


def matmul(a, b, *, name, nt=False, b_lead=(), res=None, res_scale=1.0, also_bf16=False, tm=512, tn=512,
           tk=None):
    m, k = a.shape
    n = b.shape[-2] if nt else b.shape[-1]
    tm, tn = min(tm, m), min(tn, n)
    tk = k if tk is None else tk
    nk = k // tk
    has_res = res is not None
    lead = tuple(b_lead)
    dot = _dot_nt if nt else _dot

    def body(*refs):
        a_ref, b_ref = refs[0], refs[1]
        pos = 2
        r_ref = None
        if has_res:
            r_ref = refs[pos]
            pos += 1
        o_ref = refs[pos]
        pos += 1
        ob_ref = None
        if also_bf16:
            ob_ref = refs[pos]
            pos += 1

        def finish(v):
            if has_res:
                v = v + res_scale * r_ref[...]
            o_ref[...] = v
            if also_bf16:
                ob_ref[...] = v.astype(BF16)

        if nk == 1:
            finish(dot(a_ref[...], b_ref[...]))
            return
        acc = refs[pos]
        kk = pl.program_id(2)

        @pl.when(kk == 0)
        def _():
            acc[...] = jnp.zeros_like(acc)

        acc[...] += dot(a_ref[...], b_ref[...])

        @pl.when(kk == nk - 1)
        def _():
            finish(acc[...])

    none = (None,) * len(lead)
    if nt:
        b_spec = pl.BlockSpec(none + (tn, tk), lambda j, i, kk: lead + (j, kk))
    else:
        b_spec = pl.BlockSpec(none + (tk, tn), lambda j, i, kk: lead + (kk, j))
    in_specs = [pl.BlockSpec((tm, tk), lambda j, i, kk: (i, kk)), b_spec]
    args = [a, b]
    if has_res:
        in_specs.append(pl.BlockSpec((tm, tn), lambda j, i, kk: (i, j)))
        args.append(res)
    out_shape = [jax.ShapeDtypeStruct((m, n), F32)]
    out_specs = [pl.BlockSpec((tm, tn), lambda j, i, kk: (i, j))]
    if also_bf16:
        out_shape.append(jax.ShapeDtypeStruct((m, n), BF16))
        out_specs.append(pl.BlockSpec((tm, tn), lambda j, i, kk: (i, j)))
    out = _pcall(body, name=name, grid=(n // tn, m // tm, nk), in_specs=in_specs, out_specs=out_specs,
                 out_shape=out_shape, scratch_shapes=[pltpu.VMEM((tm, tn), F32)] if nk > 1 else [],
                 compiler_params=_cp("parallel", "parallel", "arbitrary"))(*args)
    return out if also_bf16 else out[0]


def matmul_tn(a, b, *, name):
    t, k = a.shape
    n = b.shape[1]
    tk = _pick(k, (1024, 1408, 512, 256, 128))
    tn = _pick(n, (1024, 1408, 2432, 512, 256, 128))
    tt = min(1024 if tk * tn > (1 << 20) else 2048, t)
    nt = t // tt

    def body(a_ref, b_ref, o_ref):
        @pl.when(pl.program_id(2) == 0)
        def _():
            o_ref[...] = jnp.zeros_like(o_ref)

        o_ref[...] += _dot_tn(a_ref[...], b_ref[...])

    return _pcall(body, name=name, grid=(k // tk, n // tn, nt),
                  in_specs=[pl.BlockSpec((tt, tk), lambda i, j, s: (s, i)),
                            pl.BlockSpec((tt, tn), lambda i, j, s: (s, j))],
                  out_specs=pl.BlockSpec((tk, tn), lambda i, j, s: (i, j)),
                  out_shape=jax.ShapeDtypeStruct((k, n), F32),
                  compiler_params=_cp("parallel", "parallel", "arbitrary"))(a, b)


UPW = 1408


def matmul_tn_up(a, dgate, dup, *, name):
    t, k = a.shape
    tt = min(1024, t)
    tk = 1024

    def body(a_ref, g_ref, u_ref, o_ref):
        j = pl.program_id(1)

        @pl.when(pl.program_id(2) == 0)
        def _():
            o_ref[...] = jnp.zeros_like(o_ref)

        @pl.when(j < 2)
        def _():
            o_ref[...] += _dot_tn(a_ref[...], g_ref[...])

        @pl.when(j >= 2)
        def _():
            o_ref[...] += _dot_tn(a_ref[...], u_ref[...])

    return _pcall(body, name=name, grid=(k // tk, 4, t // tt),
                  in_specs=[pl.BlockSpec((tt, tk), lambda i, j, s: (s, i)),
                            pl.BlockSpec((tt, UPW), lambda i, j, s: (jnp.where(j < 2, s, 0), jnp.minimum(j, 1))),
                            pl.BlockSpec((tt, UPW), lambda i, j, s: (jnp.where(j >= 2, s, 0), jnp.maximum(j - 2, 0)))],
                  out_specs=pl.BlockSpec((None, tk, UPW), lambda i, j, s: (j, i, 0)),
                  out_shape=jax.ShapeDtypeStruct((4, k, UPW), F32),
                  compiler_params=_cp("parallel", "parallel", "arbitrary"))(a, dgate, dup)


def ffn_dx(dgate, dup, wup, l, res, *, name):
    t = dgate.shape[0]
    tm, tn = min(1024, t), 1024

    def body(g_ref, u_ref, w_ref, r_ref, o_ref, acc):
        kk = pl.program_id(2)

        @pl.when(kk == 0)
        def _():
            acc[...] = jnp.zeros_like(acc)

        @pl.when(kk < 2)
        def _():
            acc[...] += _dot_nt(g_ref[...], w_ref[...])

        @pl.when(kk >= 2)
        def _():
            acc[...] += _dot_nt(u_ref[...], w_ref[...])

        @pl.when(kk == 3)
        def _():
            o_ref[...] = acc[...] + ALPHA * r_ref[...]

    return _pcall(body, name=name, grid=(D // tn, t // tm, 4),
                  in_specs=[pl.BlockSpec((tm, UPW), lambda j, i, kk: (i, jnp.minimum(kk, 1))),
                            pl.BlockSpec((tm, UPW), lambda j, i, kk: (i, jnp.maximum(kk - 2, 0))),
                            pl.BlockSpec((None, None, tn, UPW), lambda j, i, kk: (l, kk, j, 0)),
                            pl.BlockSpec((tm, tn), lambda j, i, kk: (i, j))],
                  out_specs=pl.BlockSpec((tm, tn), lambda j, i, kk: (i, j)),
                  out_shape=jax.ShapeDtypeStruct((t, D), F32),
                  scratch_shapes=[pltpu.VMEM((tm, tn), F32)],
                  compiler_params=_cp("parallel", "parallel", "arbitrary"))(dgate, dup, wup, res)


def ffn_up(xb, wup, l, *, name):
    t = xb.shape[0]
    tm, tn = min(1024, t), UPW

    def body(x_ref, wg_ref, wu_ref, g_ref, u_ref, a_ref):
        x = x_ref[...]
        for cols in _col_chunks(tn):
            g = _dot(x, wg_ref[:, cols])
            u = _dot(x, wu_ref[:, cols])
            g_ref[:, cols] = g.astype(BF16)
            u_ref[:, cols] = u.astype(BF16)
            a_ref[:, cols] = (g * _sigmoid(g) * u).astype(BF16)

    blk = pl.BlockSpec((tm, tn), lambda j, i: (i, j))
    return _pcall(body, name=name, grid=(DFF // tn, t // tm),
                  in_specs=[pl.BlockSpec((tm, D), lambda j, i: (i, 0)),
                            pl.BlockSpec((None, None, D, tn), lambda j, i: (l, j, 0, 0)),
                            pl.BlockSpec((None, None, D, tn), lambda j, i: (l, 2 + j, 0, 0))],
                  out_specs=[blk, blk, blk],
                  out_shape=[jax.ShapeDtypeStruct((t, DFF), BF16)] * 3,
                  compiler_params=_cp("parallel", "parallel"))(xb, wup, wup)


def matmul_res_ln(a, w, l, res, g, b, *, mm_scale, name):
    t, k = a.shape
    tm = min(512, t)

    def body(a_ref, w_ref, res_ref, g_ref, b_ref, r_ref, y_ref, yb_ref):
        f = _dot(a_ref[...], w_ref[...])
        r = ALPHA * res_ref[...] + mm_scale * f
        xc, rstd = _ln_stats(r)
        y = xc * rstd * g_ref[...] + b_ref[...]
        r_ref[...] = r
        y_ref[...] = y
        yb_ref[...] = y.astype(BF16)

    return _pcall(body, name=name, grid=(t // tm,),
                  in_specs=[_rows(tm, k), _layer(l, (k, D)), _rows(tm, D), _fix((1, D)), _fix((1, D))],
                  out_specs=[_rows(tm, D)] * 3,
                  out_shape=[jax.ShapeDtypeStruct((t, D), F32), jax.ShapeDtypeStruct((t, D), F32),
                             jax.ShapeDtypeStruct((t, D), BF16)],
                  compiler_params=_cp("parallel"))(a, w, res, g, b)


def ln_bwd(dy, r, g, *, out_scale, name):
    t = dy.shape[0]
    tm = min(512, t)

    def body(dy_ref, r_ref, g_ref, dr_ref, drb_ref, dg_ref, db_ref):
        @pl.when(pl.program_id(0) == 0)
        def _():
            dg_ref[...] = jnp.zeros_like(dg_ref)
            db_ref[...] = jnp.zeros_like(db_ref)

        xc, rstd = _ln_stats(r_ref[...])
        xhat = xc * rstd
        d = dy_ref[...]
        dxh = d * g_ref[...]
        dr = rstd * (dxh - jnp.mean(dxh, axis=-1, keepdims=True)
                     - xhat * jnp.mean(dxh * xhat, axis=-1, keepdims=True))
        dr_ref[...] = dr
        drb_ref[...] = (out_scale * dr).astype(BF16)
        dg_ref[...] += jnp.sum(d * xhat, axis=0, keepdims=True)
        db_ref[...] += jnp.sum(d, axis=0, keepdims=True)

    return _pcall(body, name=name, grid=(t // tm,),
                  in_specs=[_rows(tm, D), _rows(tm, D), _fix((1, D))],
                  out_specs=[_rows(tm, D), _rows(tm, D), _fix((1, D)), _fix((1, D))],
                  out_shape=[jax.ShapeDtypeStruct((t, D), F32), jax.ShapeDtypeStruct((t, D), BF16),
                             jax.ShapeDtypeStruct((1, D), F32), jax.ShapeDtypeStruct((1, D), F32)],
                  compiler_params=_cp("arbitrary"))(dy, r, g)


def ffn_down_bwd(dfb, wd, l, gate, up, *, name):
    t = dfb.shape[0]
    tm, tn = min(1024, t), UPW
    nj = DFF // tn

    def body(df_ref, w_ref, g_ref, u_ref, dg_ref, du_ref):
        df = df_ref[...]
        for cols in _col_chunks(tn):
            da = _dot_nt(df, w_ref[cols, :])
            g = g_ref[:, cols].astype(F32)
            s = _sigmoid(g)
            gs = g * s
            dg_ref[:, cols] = (da * u_ref[:, cols].astype(F32) * (s + gs * (1.0 - s))).astype(BF16)
            du_ref[:, cols] = (da * gs).astype(BF16)

    blk = pl.BlockSpec((tm, tn), lambda j, i: (i, j))
    return _pcall(body, name=name, grid=(nj, t // tm),
                  in_specs=[pl.BlockSpec((tm, D), lambda j, i: (i, 0)),
                            pl.BlockSpec((None, tn, D), lambda j, i: (l, j, 0)), blk, blk],
                  out_specs=[blk, blk],
                  out_shape=[jax.ShapeDtypeStruct((t, DFF), BF16), jax.ShapeDtypeStruct((t, DFF), BF16)],
                  compiler_params=_cp("parallel", "parallel"))(dfb, wd, gate, up)


def ple_fwd(xb, x, pb, wgate, l, wproj, bgate, g, b, *, name):
    t = x.shape[0]
    tm = min(512, t)

    def body(xb_ref, x_ref, p_ref, wg_ref, wp_ref, bg_ref, g_ref, b_ref, r_ref, y_ref, yb_ref):
        gl = _dot(xb_ref[...], wg_ref[...]) + bg_ref[...]
        pe = _dot(p_ref[...], wp_ref[...])
        r = ALPHA * x_ref[...] + _sigmoid(gl) * pe
        xc, rstd = _ln_stats(r)
        y = xc * rstd * g_ref[...] + b_ref[...]
        r_ref[...] = r
        y_ref[...] = y
        yb_ref[...] = y.astype(BF16)

    return _pcall(body, name=name, grid=(t // tm,),
                  in_specs=[_rows(tm, D), _rows(tm, D), _rows(tm, PLE), _layer(l, (D, D)), _fix((PLE, D)),
                            _fix((1, D)), _fix((1, D)), _fix((1, D))],
                  out_specs=[_rows(tm, D)] * 3,
                  out_shape=[jax.ShapeDtypeStruct((t, D), F32), jax.ShapeDtypeStruct((t, D), F32),
                             jax.ShapeDtypeStruct((t, D), BF16)],
                  compiler_params=_cp("parallel"))(xb, x, pb, wgate, wproj, bgate, g, b)


def ple_bwd(dy, r, xb, pb, wgate, l, wproj, bgate, g, *, name):
    t = dy.shape[0]
    tm = min(512, t)

    def body(dy_ref, r_ref, xb_ref, p_ref, wg_ref, wp_ref, bg_ref, g_ref,
             dr_ref, dgl_ref, dpe_ref, dg_ref, db_ref, dbg_ref):
        @pl.when(pl.program_id(0) == 0)
        def _():
            dg_ref[...] = jnp.zeros_like(dg_ref)
            db_ref[...] = jnp.zeros_like(db_ref)
            dbg_ref[...] = jnp.zeros_like(dbg_ref)

        xc, rstd = _ln_stats(r_ref[...])
        xhat = xc * rstd
        d = dy_ref[...]
        dxh = d * g_ref[...]
        dr = rstd * (dxh - jnp.mean(dxh, axis=-1, keepdims=True)
                     - xhat * jnp.mean(dxh * xhat, axis=-1, keepdims=True))
        s = _sigmoid(_dot(xb_ref[...], wg_ref[...]) + bg_ref[...])
        pe = _dot(p_ref[...], wp_ref[...])
        dgl = dr * pe * s * (1.0 - s)
        dr_ref[...] = dr
        dgl_ref[...] = dgl.astype(BF16)
        dpe_ref[...] = (dr * s).astype(BF16)
        dg_ref[...] += jnp.sum(d * xhat, axis=0, keepdims=True)
        db_ref[...] += jnp.sum(d, axis=0, keepdims=True)
        dbg_ref[...] += jnp.sum(dgl, axis=0, keepdims=True)

    vec = jax.ShapeDtypeStruct((1, D), F32)
    return _pcall(body, name=name, grid=(t // tm,),
                  in_specs=[_rows(tm, D), _rows(tm, D), _rows(tm, D), _rows(tm, PLE), _layer(l, (D, D)),
                            _fix((PLE, D)), _fix((1, D)), _fix((1, D))],
                  out_specs=[_rows(tm, D), _rows(tm, D), _rows(tm, D), _fix((1, D)), _fix((1, D)), _fix((1, D))],
                  out_shape=[jax.ShapeDtypeStruct((t, D), F32), jax.ShapeDtypeStruct((t, D), BF16),
                             jax.ShapeDtypeStruct((t, D), BF16), vec, vec, vec],
                  compiler_params=_cp("arbitrary"))(dy, r, xb, pb, wgate, wproj, bgate, g)


def loss_head(y, tgt, *, name):
    t = y.shape[0]
    tm = min(256, t)

    def body(y_ref, t_ref, dy_ref, sq_ref):
        @pl.when(pl.program_id(0) == 0)
        def _():
            sq_ref[...] = jnp.zeros_like(sq_ref)

        e = y_ref[...] - t_ref[...]
        dy_ref[...] = e / float(D)
        sq_ref[...] += jnp.sum(e * e, axis=0, keepdims=True)

    return _pcall(body, name=name, grid=(t // tm,),
                  in_specs=[_rows(tm, D), _rows(tm, D)],
                  out_specs=[_rows(tm, D), _fix((1, D))],
                  out_shape=[jax.ShapeDtypeStruct((t, D), F32), jax.ShapeDtypeStruct((1, D), F32)],
                  compiler_params=_cp("arbitrary"))(y, tgt)


def _lru_gates(xc, wa_ref, wx_ref, ba_ref, bx_ref, lam_ref):
    xcb = xc.astype(BF16)
    r = _sigmoid(_dot(xcb, wa_ref[...]) + ba_ref[...])
    ig = _sigmoid(_dot(xcb, wx_ref[...]) + bx_ref[...])
    sp = _softplus(-lam_ref[...])
    la = -LRU_C * r * sp
    a = jnp.exp(la)
    mult = jnp.sqrt(-_expm1(2.0 * la))
    return r, ig, sp, la, a, mult


def lru_fwd(z, cw, cb, wa, wx, ba, bx, lam, *, name):
    t = z.shape[0]
    tm = min(512, t)
    hb = tm // 8

    def body(ax_ref, prev_ref, ay_ref, cw_ref, cb_ref, wa_ref, wx_ref, ba_ref, bx_ref, lam_ref,
             xc_ref, xcb_ref, h_ref, ya_ref, xs, a_s, b_s, hc):
        i = pl.program_id(0)

        @pl.when(i == 0)
        def _():
            hc[...] = jnp.zeros_like(hc)

        xs[0:8, :] = jnp.where(i == 0, 0.0, prev_ref[...])
        xs[8:, :] = ax_ref[...]
        xc = cb_ref[...] + cw_ref[0:1, :] * xs[5:5 + tm, :]
        for k in range(1, 4):
            xc = xc + cw_ref[k:k + 1, :] * xs[5 + k:5 + k + tm, :]
        r, ig, sp, la, a, mult = _lru_gates(xc, wa_ref, wx_ref, ba_ref, bx_ref, lam_ref)
        a_s[...] = a
        b_s[...] = mult * (ig * xc)
        xc_ref[...] = xc
        xcb_ref[...] = xc.astype(BF16)

        def step(g, h):
            base = pl.multiple_of(g * 8, 8)
            a8 = a_s[pl.ds(base, 8), :]
            b8 = b_s[pl.ds(base, 8), :]
            for j in range(8):
                h = a8[j:j + 1, :] * h + b8[j:j + 1, :]
                h_ref[pl.ds(base + j, 1), :] = h
            return h

        hc[...] = lax.fori_loop(0, tm // 8, step, hc[...])
        ya_ref[...] = (_gelu(ay_ref[...]) * h_ref[...]).astype(BF16)

    vec = _fix((1, BW))
    return _pcall(body, name=name, grid=(t // tm,),
                  in_specs=[_rows(tm, BW, AX // BW),
                            pl.BlockSpec((8, BW), lambda i: (jnp.maximum(i * hb - 1, 0), AX // BW)),
                            _rows(tm, BW, AY // BW), _fix((4, BW)), vec, _fix((BW, BW)), _fix((BW, BW)),
                            vec, vec, vec],
                  out_specs=[_rows(tm, BW)] * 4,
                  out_shape=[jax.ShapeDtypeStruct((t, BW), F32), jax.ShapeDtypeStruct((t, BW), BF16),
                             jax.ShapeDtypeStruct((t, BW), F32), jax.ShapeDtypeStruct((t, BW), BF16)],
                  scratch_shapes=[pltpu.VMEM((tm + 8, BW), F32), pltpu.VMEM((tm, BW), F32),
                                  pltpu.VMEM((tm, BW), F32), pltpu.VMEM((1, BW), F32)],
                  compiler_params=_cp("arbitrary"))(z, z, z, cw, cb, wa, wx, ba, bx, lam)


def lru_bwd(dya, z, h, xc, wa, wx, ba, bx, lam, *, name):
    t = dya.shape[0]
    tm = min(512, t)
    nb = t // tm
    hb = tm // 8

    def body(dya_ref, ay_ref, h_ref, hprev_ref, xc_ref, wa_ref, wx_ref, ba_ref, bx_ref,
             lam_ref, day_ref, dxc_ref, dpr_ref, dpi_ref, dba_ref, dbx_ref, dlam_ref,
             hs, a_s, g_s, d_s, cc):
        i = pl.program_id(0)

        @pl.when(i == 0)
        def _():
            cc[...] = jnp.zeros_like(cc)
            dba_ref[...] = jnp.zeros_like(dba_ref)
            dbx_ref[...] = jnp.zeros_like(dbx_ref)
            dlam_ref[...] = jnp.zeros_like(dlam_ref)

        xc = xc_ref[...]
        r, ig, sp, la, a, mult = _lru_gates(xc, wa_ref, wx_ref, ba_ref, bx_ref, lam_ref)
        ay = ay_ref[...]
        dya = dya_ref[...]
        hcur = h_ref[...]
        day_ref[...] = (dya * hcur * _gelu_grad(ay)).astype(BF16)
        a_s[...] = a
        g_s[...] = dya * _gelu(ay)

        def step(gg, cin):
            g = tm // 8 - 1 - gg
            base = pl.multiple_of(g * 8, 8)
            a8 = a_s[pl.ds(base, 8), :]
            g8 = g_s[pl.ds(base, 8), :]
            for j in range(7, -1, -1):
                d = g8[j:j + 1, :] + cin
                d_s[pl.ds(base + j, 1), :] = d
                cin = a8[j:j + 1, :] * d
            return cin

        cc[...] = lax.fori_loop(0, tm // 8, step, cc[...])
        dht = d_s[...]
        hs[0:8, :] = jnp.where(i == nb - 1, 0.0, hprev_ref[...])
        hs[8:, :] = hcur
        da = dht * hs[7:7 + tm, :]
        dmult = dht * ig * xc
        dig = dht * mult * xc
        dla = da * a - dmult * a * a / mult
        dpr = dla * (-LRU_C * sp) * r * (1.0 - r)
        dpi = dig * ig * (1.0 - ig)
        dprb = dpr.astype(BF16)
        dpib = dpi.astype(BF16)
        dxc_ref[...] = dht * mult * ig + _dot_nt(dprb, wa_ref[...]) + _dot_nt(dpib, wx_ref[...])
        dpr_ref[...] = dprb
        dpi_ref[...] = dpib
        dba_ref[...] += jnp.sum(dpr, axis=0, keepdims=True)
        dbx_ref[...] += jnp.sum(dpi, axis=0, keepdims=True)
        dlam_ref[...] += jnp.sum(dla * (-LRU_C * r), axis=0, keepdims=True) * (-_sigmoid(-lam_ref[...]))

    vec = _fix((1, BW))
    mat = _fix((BW, BW))
    rev = lambda col: pl.BlockSpec((tm, BW), lambda i: (nb - 1 - i, col))
    vshape = jax.ShapeDtypeStruct((1, BW), F32)
    return _pcall(body, name=name, grid=(nb,),
                  in_specs=[rev(0), rev(AY // BW), rev(0),
                            pl.BlockSpec((8, BW), lambda i: (jnp.maximum((nb - 1 - i) * hb - 1, 0), 0)),
                            rev(0), mat, mat, vec, vec, vec],
                  out_specs=[rev(0), rev(0), rev(0), rev(0), vec, vec, vec],
                  out_shape=[jax.ShapeDtypeStruct((t, BW), BF16), jax.ShapeDtypeStruct((t, BW), F32),
                             jax.ShapeDtypeStruct((t, BW), BF16), jax.ShapeDtypeStruct((t, BW), BF16),
                             vshape, vshape, vshape],
                  scratch_shapes=[pltpu.VMEM((tm + 8, BW), F32), pltpu.VMEM((tm, BW), F32),
                                  pltpu.VMEM((tm, BW), F32), pltpu.VMEM((tm, BW), F32),
                                  pltpu.VMEM((1, BW), F32)],
                  compiler_params=_cp("arbitrary"))(dya, z, h, h, xc, wa, wx, ba, bx, lam)


def conv_bwd(dxc, z, cw, *, name):
    t = dxc.shape[0]
    tm = min(512, t)
    nb = t // tm
    hb = tm // 8

    def body(d_ref, dnext_ref, ax_ref, prev_ref, cw_ref, dax_ref, dcw_ref, dcb_ref, ds, xs):
        i = pl.program_id(0)

        @pl.when(i == 0)
        def _():
            dcw_ref[...] = jnp.zeros_like(dcw_ref)
            dcb_ref[...] = jnp.zeros_like(dcb_ref)

        d = d_ref[...]
        ds[0:tm, :] = d
        ds[tm:, :] = jnp.where(i == nb - 1, 0.0, dnext_ref[...])
        xs[0:8, :] = jnp.where(i == 0, 0.0, prev_ref[...])
        xs[8:, :] = ax_ref[...]
        dax = cw_ref[3:4, :] * d
        for k in range(3):
            dax = dax + cw_ref[k:k + 1, :] * ds[3 - k:3 - k + tm, :]
        dax_ref[...] = dax.astype(BF16)
        for k in range(4):
            dcw_ref[k:k + 1, :] += jnp.sum(d * xs[5 + k:5 + k + tm, :], axis=0, keepdims=True)
        dcb_ref[...] += jnp.sum(d, axis=0, keepdims=True)

    return _pcall(body, name=name, grid=(nb,),
                  in_specs=[_rows(tm, BW),
                            pl.BlockSpec((8, BW), lambda i: (jnp.minimum((i + 1) * hb, nb * hb - 1), 0)),
                            _rows(tm, BW, AX // BW),
                            pl.BlockSpec((8, BW), lambda i: (jnp.maximum(i * hb - 1, 0), AX // BW)),
                            _fix((4, BW))],
                  out_specs=[_rows(tm, BW), _fix((4, BW)), _fix((1, BW))],
                  out_shape=[jax.ShapeDtypeStruct((t, BW), BF16), jax.ShapeDtypeStruct((4, BW), F32),
                             jax.ShapeDtypeStruct((1, BW), F32)],
                  scratch_shapes=[pltpu.VMEM((tm + 8, BW), F32), pltpu.VMEM((tm + 8, BW), F32)],
                  compiler_params=_cp("arbitrary"))(dxc, dxc, z, z, cw)


GLA_CB = 4


def _gla_consts():
    tri = (jnp.arange(CHUNK)[:, None] >= jnp.arange(CHUNK)[None, :]).astype(F32)
    mask = ((jnp.arange(BW)[:, None] // 128) == (jnp.arange(256)[None, :] // 64)).astype(F32)
    return tri, mask


def gla_fwd(z, zb, wg2p, bg, ng, *, name):
    t = z.shape[0]
    tm = GLA_CB * CHUNK
    nc = t // CHUNK
    tri, mask = _gla_consts()

    def body(q_ref, k_ref, v_ref, misc_ref, br_ref, w_ref, bg_ref, ng_ref, tri_ref, mask_ref,
             yb_ref, st_ref, st):
        @pl.when(pl.program_id(0) == 0)
        def _():
            st[...] = jnp.zeros_like(st)

        for c in range(GLA_CB):
            rows = slice(c * CHUNK, (c + 1) * CHUNK)
            pre = _dot(misc_ref[rows, :], w_ref[...]) + bg_ref[...]
            la = _log_sigmoid(pre) / GLA_TAU
            gc = _dot_hi(tri_ref[...], la)
            gt = gc[CHUNK - 1:CHUNK, :]
            kdec = k_ref[rows, :] * jnp.exp(gt - gc)
            delta = _dot_tn(v_ref[rows, :], kdec.astype(BF16))
            s_new = st[...] * jnp.exp(gt) + delta * mask_ref[...]
            st[...] = s_new
            st_ref[c] = s_new
            o = _dot_nt(q_ref[rows, :], s_new.astype(BF16)) * (64.0 ** -0.5)
            br = br_ref[rows, :]
            for hd in range(4):
                cols = slice(hd * 128, (hd + 1) * 128)
                oh = o[:, cols]
                rs = lax.rsqrt(jnp.mean(oh * oh, axis=-1, keepdims=True) + RMS_EPS)
                brh = br[:, cols]
                yb_ref[rows, cols] = (oh * rs * ng_ref[:, cols] * (brh * _sigmoid(brh))).astype(BF16)

    return _pcall(body, name=name, grid=(t // tm,),
                  in_specs=[_rows(tm, 256, BQ // 256), _rows(tm, 256, BK // 256), _rows(tm, BW, BV // BW),
                            _rows(tm, 128, MISC // 128), _rows(tm, BW, BR // BW), _fix((128, 256)),
                            _fix((1, 256)), _fix((1, BW)), _fix((CHUNK, CHUNK)), _fix((BW, 256))],
                  out_specs=[_rows(tm, BW), pl.BlockSpec((GLA_CB, BW, 256), lambda i: (i, 0, 0))],
                  out_shape=[jax.ShapeDtypeStruct((t, BW), BF16), jax.ShapeDtypeStruct((nc, BW, 256), F32)],
                  scratch_shapes=[pltpu.VMEM((BW, 256), F32)],
                  compiler_params=_cp("arbitrary"))(zb, z, zb, zb, z, wg2p, bg, ng, tri, mask)


def gla_bwd(dyb, z, zb, states, wg2p, bg, ng, *, name):
    t = z.shape[0]
    tm = GLA_CB * CHUNK
    nb = t // tm
    tri, mask = _gla_consts()
    triu = tri.T

    def body(dy_ref, q_ref, k_ref, v_ref, misc_ref, br_ref, st_ref, sp_ref, w_ref, bg_ref, ng_ref,
             tri_ref, triu_ref, mask_ref,
             dq_ref, dk_ref, dv_ref, dbr_ref, dmisc_ref, dpre_ref, dbg_ref, dng_ref, cc):
        i = pl.program_id(0)

        @pl.when(i == 0)
        def _():
            cc[...] = jnp.zeros_like(cc)
            dbg_ref[...] = jnp.zeros_like(dbg_ref)
            dng_ref[...] = jnp.zeros_like(dng_ref)

        last_row = lax.broadcasted_iota(jnp.int32, (CHUNK, 256), 0) == CHUNK - 1
        for c in range(GLA_CB - 1, -1, -1):
            rows = slice(c * CHUNK, (c + 1) * CHUNK)
            pre = _dot(misc_ref[rows, :], w_ref[...]) + bg_ref[...]
            la = _log_sigmoid(pre) / GLA_TAU
            gc = _dot_hi(tri_ref[...], la)
            gt = gc[CHUNK - 1:CHUNK, :]
            eg = jnp.exp(gt - gc)
            kdec = k_ref[rows, :] * eg
            e = jnp.exp(gt)
            s_n = st_ref[c]
            if c > 0:
                s_prev = st_ref[c - 1]
            else:
                s_prev = jnp.where(i == nb - 1, 0.0, sp_ref[0])
            sb = s_n.astype(BF16)
            qb = q_ref[rows, :]
            o = _dot_nt(qb, sb) * (64.0 ** -0.5)
            br = br_ref[rows, :]
            dy = dy_ref[rows, :]
            do_parts = []
            for hd in range(4):
                cols = slice(hd * 128, (hd + 1) * 128)
                oh = o[:, cols]
                rs = lax.rsqrt(jnp.mean(oh * oh, axis=-1, keepdims=True) + RMS_EPS)
                ohat = oh * rs
                brh = br[:, cols]
                sg = _sigmoid(brh)
                dyh = dy[:, cols]
                ngh = ng_ref[:, cols]
                don = dyh * (brh * sg)
                dbr_ref[rows, cols] = (dyh * (ohat * ngh) * sg * (1.0 + brh * (1.0 - sg))).astype(BF16)
                dng_ref[:, cols] += jnp.sum(don * ohat, axis=0, keepdims=True)
                doh = don * ngh
                do_parts.append(rs * (doh - ohat * jnp.mean(doh * ohat, axis=-1, keepdims=True)))
            dob = jnp.concatenate(do_parts, axis=1).astype(BF16)
            dq_ref[rows, :] = (_dot(dob, sb) * (64.0 ** -0.5)).astype(BF16)
            dst = cc[...] + _dot_tn(dob, qb) * (64.0 ** -0.5) * mask_ref[...]
            dsb = dst.astype(BF16)
            dkdec = _dot(v_ref[rows, :], dsb)
            dv_ref[rows, :] = _dot_nt(kdec.astype(BF16), dsb).astype(BF16)
            dgt = jnp.sum(dst * s_prev, axis=0, keepdims=True) * e
            dk_ref[rows, :] = (dkdec * eg).astype(BF16)
            dd = dkdec * kdec
            dgt = dgt + jnp.sum(dd, axis=0, keepdims=True)
            dgc = jnp.where(last_row, dgt - dd, -dd)
            dla = _dot_hi(triu_ref[...], dgc)
            dpre = dla * (1.0 / GLA_TAU) * _sigmoid(-pre)
            dpb = dpre.astype(BF16)
            dpre_ref[rows, :] = dpb
            dmisc_ref[rows, :] = _dot_nt(dpb, w_ref[...])
            dbg_ref[...] += jnp.sum(dpre, axis=0, keepdims=True)
            cc[...] = dst * e

    rev = lambda w, col: pl.BlockSpec((tm, w), lambda i: (nb - 1 - i, col))
    return _pcall(body, name=name, grid=(nb,),
                  in_specs=[rev(BW, 0), rev(256, BQ // 256), rev(256, BK // 256), rev(BW, BV // BW),
                            rev(128, MISC // 128), rev(BW, BR // BW),
                            pl.BlockSpec((GLA_CB, BW, 256), lambda i: (nb - 1 - i, 0, 0)),
                            pl.BlockSpec((1, BW, 256), lambda i: (jnp.maximum((nb - 1 - i) * GLA_CB - 1, 0), 0, 0)),
                            _fix((128, 256)), _fix((1, 256)), _fix((1, BW)),
                            _fix((CHUNK, CHUNK)), _fix((CHUNK, CHUNK)), _fix((BW, 256))],
                  out_specs=[rev(256, 0), rev(256, 0), rev(BW, 0), rev(BW, 0), rev(128, 0), rev(256, 0),
                             _fix((1, 256)), _fix((1, BW))],
                  out_shape=[jax.ShapeDtypeStruct((t, 256), BF16), jax.ShapeDtypeStruct((t, 256), BF16),
                             jax.ShapeDtypeStruct((t, BW), BF16), jax.ShapeDtypeStruct((t, BW), BF16),
                             jax.ShapeDtypeStruct((t, 128), F32), jax.ShapeDtypeStruct((t, 256), BF16),
                             jax.ShapeDtypeStruct((1, 256), F32), jax.ShapeDtypeStruct((1, BW), F32)],
                  scratch_shapes=[pltpu.VMEM((BW, 256), F32)],
                  compiler_params=_cp("arbitrary"))(dyb, zb, z, zb, zb, z, states, states, wg2p, bg, ng,
                                                    tri, triu, mask)


FOX_SCALE = 64.0 ** -0.5
NEG = -1e30


def fox_fcum(z, bfp, *, name):
    t = z.shape[0]
    tm = min(256, t)
    tri = (jnp.arange(tm)[:, None] >= jnp.arange(tm)[None, :]).astype(F32)

    def body(m_ref, b_ref, tri_ref, o_ref, cc):
        @pl.when(pl.program_id(0) == 0)
        def _():
            cc[...] = jnp.zeros_like(cc)

        lf = _log_sigmoid(m_ref[...] + b_ref[...])
        cs = _dot_hi(tri_ref[...], lf) + cc[...]
        o_ref[...] = cs
        cc[...] = cs[tm - 1:tm, :]

    return _pcall(body, name=name, grid=(t // tm,),
                  in_specs=[_rows(tm, 128, MISC // 128), _fix((1, 128)), _fix((tm, tm))],
                  out_specs=_rows(tm, 128), out_shape=jax.ShapeDtypeStruct((t, 128), F32),
                  scratch_shapes=[pltpu.VMEM((1, 128), F32)],
                  compiler_params=_cp("arbitrary"))(z, bfp, tri)


def fox_dcf(dfc, z, bfp, dmisc_g, *, name):
    t = z.shape[0]
    tm = min(256, t)
    nb = t // tm
    triu = (jnp.arange(tm)[:, None] <= jnp.arange(tm)[None, :]).astype(F32)

    def body(d_ref, m_ref, b_ref, g_ref, tri_ref, o_ref, dbf_ref, cc):
        @pl.when(pl.program_id(0) == 0)
        def _():
            cc[...] = jnp.zeros_like(cc)
            dbf_ref[...] = jnp.zeros_like(dbf_ref)

        rc = _dot_hi(tri_ref[...], d_ref[...]) + cc[...]
        cc[...] = rc[0:1, :]
        dcf = rc * _sigmoid(-(m_ref[...] + b_ref[...]))
        o_ref[...] = (dcf + g_ref[...]).astype(BF16)
        dbf_ref[...] += jnp.sum(dcf, axis=0, keepdims=True)

    rev = lambda col: pl.BlockSpec((tm, 128), lambda i: (nb - 1 - i, col))
    return _pcall(body, name=name, grid=(nb,),
                  in_specs=[rev(0), rev(MISC // 128), _fix((1, 128)), rev(0), _fix((tm, tm))],
                  out_specs=[rev(0), _fix((1, 128))],
                  out_shape=[jax.ShapeDtypeStruct((t, 128), BF16), jax.ShapeDtypeStruct((1, 128), F32)],
                  scratch_shapes=[pltpu.VMEM((1, 128), F32)],
                  compiler_params=_cp("arbitrary"))(dfc, z, bfp, dmisc_g, triu)


def fox_delta(dyc, ycf, *, name):
    t = dyc.shape[0]
    tm = min(256, t)
    seg = ((jnp.arange(BW)[:, None] // 64) == jnp.arange(128)[None, :]).astype(F32)

    def body(d_ref, o_ref, s_ref, out_ref):
        out_ref[...] = _dot_hi(d_ref[...] * o_ref[...], s_ref[...])

    return _pcall(body, name=name, grid=(t // tm,),
                  in_specs=[_rows(tm, BW), _rows(tm, BW), _fix((BW, 128))],
                  out_specs=_rows(tm, 128), out_shape=jax.ShapeDtypeStruct((t, 128), F32),
                  compiler_params=_cp("parallel"))(dyc, ycf, seg)


def fox_fwd_t(zb, frow, fkb, *, name):
    t = zb.shape[0]
    tq = min(512, t)
    nq = t // tq
    rep = tq // 128

    pairs = [(i, j) for i in range(nq) for j in range(i + 1)]
    qi_tab = jnp.asarray([p[0] for p in pairs], jnp.int32)
    kj_tab = jnp.asarray([p[1] for p in pairs], jnp.int32)

    def body(qi_ref, kj_ref, q_ref, k_ref, v_ref, fq_ref, fk_ref, y_ref, yf_ref, lse_ref, m_s, l_s, acc):
        step = pl.program_id(1)
        i, j = qi_ref[step], kj_ref[step]

        @pl.when(j == 0)
        def _():
            m_s[...] = jnp.full_like(m_s, NEG)
            l_s[...] = jnp.zeros_like(l_s)
            acc[...] = jnp.zeros_like(acc)

        lo = lax.broadcasted_iota(jnp.int32, (tq, 128), 1) < 64

        def work(diagonal):
            q = q_ref[...]
            k = k_ref[...]
            v = v_ref[...]
            if diagonal:
                key = lax.broadcasted_iota(jnp.int32, (tq, tq), 0)
                qry = lax.broadcasted_iota(jnp.int32, (tq, tq), 1)
                keep = key <= qry
            for hh in range(2):
                sel = lo if hh == 0 else jnp.logical_not(lo)
                qh = jnp.where(sel, q, jnp.zeros_like(q))
                s = _dot_nt(k, qh) + fq_ref[hh] - jnp.tile(fk_ref[hh], (1, rep))
                if diagonal:
                    s = jnp.where(keep, s, NEG)
                m_old = m_s[hh]
                m_new = jnp.maximum(m_old, jnp.max(s, axis=0, keepdims=True))
                p = jnp.exp(s - m_new)
                corr = jnp.exp(m_old - m_new)
                l_s[hh] = l_s[hh] * corr + jnp.sum(p, axis=0, keepdims=True)
                m_s[hh] = m_new
                pv = _dot_tn(v, p.astype(BF16))
                rows = slice(64 * hh, 64 * hh + 64)
                acc[rows, :] = acc[rows, :] * corr + pv[rows, :]

        @pl.when(j < i)
        def _():
            work(False)

        @pl.when(j == i)
        def _():
            work(True)
            first = lax.broadcasted_iota(jnp.int32, (128, tq), 0) < 64
            out = (acc[...] * jnp.where(first, 1.0 / l_s[0], 1.0 / l_s[1])).T
            y_ref[...] = out.astype(BF16)
            yf_ref[...] = out
            lse_ref[...] = m_s[...] + jnp.log(l_s[...])

    kv = lambda off: pl.BlockSpec((tq, 128), lambda h, s, qi, kj: (kj[s], off // 128 + h))
    gs = pltpu.PrefetchScalarGridSpec(
        num_scalar_prefetch=2, grid=(4, len(pairs)),
        in_specs=[pl.BlockSpec((tq, 128), lambda h, s, qi, kj: (qi[s], CQ // 128 + h)), kv(CK), kv(CV),
                  pl.BlockSpec((2, 1, tq), lambda h, s, qi, kj: (h, 0, qi[s])),
                  pl.BlockSpec((2, tq, 128), lambda h, s, qi, kj: (h, kj[s], 0))],
        out_specs=[pl.BlockSpec((tq, 128), lambda h, s, qi, kj: (qi[s], h)),
                   pl.BlockSpec((tq, 128), lambda h, s, qi, kj: (qi[s], h)),
                   pl.BlockSpec((2, 1, tq), lambda h, s, qi, kj: (h, 0, qi[s]))],
        scratch_shapes=[pltpu.VMEM((2, 1, tq), F32), pltpu.VMEM((2, 1, tq), F32), pltpu.VMEM((128, tq), F32)])
    return _pcall(body, name=name, grid_spec=gs,
                  out_shape=[jax.ShapeDtypeStruct((t, BW), BF16), jax.ShapeDtypeStruct((t, BW), F32),
                             jax.ShapeDtypeStruct((FOX_H, 1, t), F32)],
                  compiler_params=_cp("parallel", "arbitrary"))(qi_tab, kj_tab, zb, zb, zb, frow, fkb)


def fox_bwd_t(zb, dyc, frow, fkb, lse, dl, *, name):
    t = zb.shape[0]
    tq = min(512, t)
    nq = t // tq
    rep = tq // 128

    pairs = [(j, i) for j in range(nq) for i in range(j, nq)]
    kj_tab = jnp.asarray([p[0] for p in pairs], jnp.int32)
    qi_tab = jnp.asarray([p[1] for p in pairs], jnp.int32)

    def body(kj_ref, qi_ref, q_ref, k_ref, v_ref, do_ref, fq_ref, fk_ref, lse_ref, dl_ref,
             dq_ref, dk_ref, dv_ref, dfk_ref, dfq_ref, dk_s, dv_s, df_s, dq_s):
        step = pl.program_id(1)
        j, i = kj_ref[step], qi_ref[step]

        @pl.when(step == 0)
        def _():
            dq_s[...] = jnp.zeros_like(dq_s)
            dfq_ref[...] = jnp.zeros_like(dfq_ref)

        @pl.when(i == j)
        def _():
            dk_s[...] = jnp.zeros_like(dk_s)
            dv_s[...] = jnp.zeros_like(dv_s)
            df_s[...] = jnp.zeros_like(df_s)

        lo = lax.broadcasted_iota(jnp.int32, (tq, 128), 1) < 64

        def work(diagonal):
            q = q_ref[...]
            k = k_ref[...]
            v = v_ref[...]
            dob = do_ref[...].astype(BF16)
            if diagonal:
                key = lax.broadcasted_iota(jnp.int32, (tq, tq), 0)
                qry = lax.broadcasted_iota(jnp.int32, (tq, tq), 1)
                keep = key <= qry
            dvs, dks = [], []
            for hh in range(2):
                sel = lo if hh == 0 else jnp.logical_not(lo)
                qh = jnp.where(sel, q, jnp.zeros_like(q))
                doh = jnp.where(sel, dob, jnp.zeros_like(dob))
                p = jnp.exp(_dot_nt(k, qh) + (fq_ref[hh] - lse_ref[hh]) - jnp.tile(fk_ref[hh], (1, rep)))
                if diagonal:
                    p = jnp.where(keep, p, 0.0)
                ds = p * (_dot_nt(v, doh) - dl_ref[hh])
                dsb = ds.astype(BF16)
                dvs.append(_dot(p.astype(BF16), dob))
                dks.append(_dot(dsb, q))
                rows = slice(64 * hh, 64 * hh + 64)
                dq_s[i, rows, :] += _dot_tn(k, dsb)[rows, :]
                part = ds[:, 0:128]
                for r in range(1, rep):
                    part = part + ds[:, 128 * r:128 * (r + 1)]
                df_s[hh] += part
                dfq_ref[hh, i] += jnp.sum(ds, axis=0, keepdims=True)
            dv_s[...] += jnp.where(lo, dvs[0], dvs[1])
            dk_s[...] += jnp.where(lo, dks[0], dks[1])

        @pl.when(i > j)
        def _():
            work(False)

        @pl.when(i == j)
        def _():
            work(True)
            dq_ref[...] = dq_s[i].T.astype(BF16)

        @pl.when(i == nq - 1)
        def _():
            dk_ref[...] = dk_s[...].astype(BF16)
            dv_ref[...] = dv_s[...].astype(BF16)
            for hh in range(2):
                dfk_ref[hh] = -jnp.sum(df_s[hh].T, axis=0, keepdims=True)

    row = lambda: pl.BlockSpec((2, 1, tq), lambda h, s, kj, qi: (h, 0, qi[s]))
    gs = pltpu.PrefetchScalarGridSpec(
        num_scalar_prefetch=2, grid=(4, len(pairs)),
        in_specs=[pl.BlockSpec((tq, 128), lambda h, s, kj, qi: (qi[s], CQ // 128 + h)),
                  pl.BlockSpec((tq, 128), lambda h, s, kj, qi: (kj[s], CK // 128 + h)),
                  pl.BlockSpec((tq, 128), lambda h, s, kj, qi: (kj[s], CV // 128 + h)),
                  pl.BlockSpec((tq, 128), lambda h, s, kj, qi: (qi[s], h)),
                  row(), pl.BlockSpec((2, tq, 128), lambda h, s, kj, qi: (h, kj[s], 0)), row(), row()],
        out_specs=[pl.BlockSpec((tq, 128), lambda h, s, kj, qi: (kj[s], h)),
                   pl.BlockSpec((tq, 128), lambda h, s, kj, qi: (kj[s], h)),
                   pl.BlockSpec((tq, 128), lambda h, s, kj, qi: (kj[s], h)),
                   pl.BlockSpec((2, 1, tq), lambda h, s, kj, qi: (h, 0, kj[s])),
                   pl.BlockSpec((2, nq, 1, tq), lambda h, s, kj, qi: (h, 0, 0, 0))],
        scratch_shapes=[pltpu.VMEM((tq, 128), F32), pltpu.VMEM((tq, 128), F32), pltpu.VMEM((2, tq, 128), F32),
                        pltpu.VMEM((nq, 128, tq), F32)])
    return _pcall(body, name=name, grid_spec=gs,
                  out_shape=[jax.ShapeDtypeStruct((t, BW), BF16), jax.ShapeDtypeStruct((t, BW), BF16),
                             jax.ShapeDtypeStruct((t, BW), BF16), jax.ShapeDtypeStruct((FOX_H, 1, t), F32),
                             jax.ShapeDtypeStruct((FOX_H, nq, 1, tq), F32)],
                  compiler_params=_cp("parallel", "arbitrary"))(kj_tab, qi_tab, zb, zb, zb, dyc, frow, fkb, lse, dl)


def merge_fwd(ya, yb, yc, wbr, z, *, name):
    t = ya.shape[0]
    tm = min(512, t)

    def body(ya_ref, yb_ref, yc_ref, w_ref, g0_ref, g1_ref, g2_ref, o_ref):
        m = _sigmoid(g0_ref[...]) * _dot(ya_ref[...], w_ref[0])
        m = m + _sigmoid(g1_ref[...]) * _dot(yb_ref[...], w_ref[1])
        m = m + _sigmoid(g2_ref[...]) * _dot(yc_ref[...], w_ref[2])
        o_ref[...] = m.astype(BF16)

    return _pcall(body, name=name, grid=(t // tm,),
                  in_specs=[_rows(tm, BW)] * 3 + [_fix((3, BW, D))]
                  + [_rows(tm, D, G0 // D + j) for j in range(3)],
                  out_specs=_rows(tm, D), out_shape=jax.ShapeDtypeStruct((t, D), BF16),
                  compiler_params=_cp("parallel"))(ya, yb, yc, wbr, z, z, z)


def merge_bwd(doutb, wo, l, ya, yb, yc, wbr, z, *, name):
    t = ya.shape[0]
    tm = min(512, t)

    def body(do_ref, wo_ref, ya_ref, yb_ref, yc_ref, w_ref, g0_ref, g1_ref, g2_ref,
             dya_ref, dyb_ref, dyc_ref, dp0_ref, dp1_ref, dp2_ref, dg0_ref, dg1_ref, dg2_ref):
        dm = _dot_nt(do_ref[...], wo_ref[...])
        ys = (ya_ref, yb_ref, yc_ref)
        gs = (g0_ref, g1_ref, g2_ref)
        dys = (dya_ref, dyb_ref, dyc_ref)
        dps = (dp0_ref, dp1_ref, dp2_ref)
        dgs = (dg0_ref, dg1_ref, dg2_ref)
        for j in range(3):
            s = _sigmoid(gs[j][...])
            pj = _dot(ys[j][...], w_ref[j])
            dpb = (dm * s).astype(BF16)
            dps[j][...] = dpb
            dgs[j][...] = (dm * pj * s * (1.0 - s)).astype(BF16)
            dys[j][...] = _dot_nt(dpb, w_ref[j])

    yshape = jax.ShapeDtypeStruct((t, BW), F32)
    dshape = jax.ShapeDtypeStruct((t, D), BF16)
    return _pcall(body, name=name, grid=(t // tm,),
                  in_specs=[_rows(tm, D), _layer(l, (D, D))] + [_rows(tm, BW)] * 3
                  + [_fix((3, BW, D))] + [_rows(tm, D, G0 // D + j) for j in range(3)],
                  out_specs=[_rows(tm, BW)] * 3 + [_rows(tm, D)] * 6,
                  out_shape=[yshape] * 3 + [dshape] * 6,
                  compiler_params=_cp("parallel"))(doutb, wo, ya, yb, yc, wbr, z, z, z)


def adamw(w, g, m, v, *, name):
    nl, r, c = w.shape
    tm = _row_tile(r)

    def body(w_ref, g_ref, m_ref, v_ref, d_ref, mo_ref, vo_ref):
        gg = g_ref[...]
        mn = ADAM_B1 * m_ref[...] + (1.0 - ADAM_B1) * gg
        vn = ADAM_B2 * v_ref[...] + (1.0 - ADAM_B2) * (gg * gg)
        m_hat = mn / (1.0 - ADAM_B1 ** ADAM_STEP)
        v_hat = vn / (1.0 - ADAM_B2 ** ADAM_STEP)
        d_ref[...] = -ADAM_LR * (m_hat / (jnp.sqrt(v_hat) + ADAM_EPS) + ADAM_WD * w_ref[...])
        mo_ref[...] = mn
        vo_ref[...] = vn

    shp = jax.ShapeDtypeStruct((nl, r, c), F32)
    blk = pl.BlockSpec((None, tm, c), lambda l, i: (l, i, 0))
    return _pcall(body, name=name, grid=(nl, r // tm), in_specs=[blk] * 4, out_specs=[blk] * 3,
                  out_shape=[shp] * 3, compiler_params=_cp("parallel", "parallel"))(w, g, m, v)


def _place():
    return lax.axis_index("x"), lax.axis_index("y"), lax.axis_index("c")


def _remote(src, dst, send_sems, recv_sems, k, to):
    return pltpu.make_async_remote_copy(src_ref=src, dst_ref=dst, send_sem=send_sems.at[k],
                                        recv_sem=recv_sems.at[k], device_id=to, device_id_type=MESH)


HBM = pl.BlockSpec(memory_space=pltpu.HBM)
SEM = pl.BlockSpec(memory_space=pltpu.SEMAPHORE)
EFFECT = pltpu.SideEffectType.DATAFLOW_SIDE_EFFECTING


def gather_first(shards):
    n = len(shards)

    def body(*refs):
        ins, outs, lands = refs[:n], refs[n:2 * n], refs[2 * n:3 * n]
        send_sems, recv_sems, own_send, own_recv = refs[3 * n:]
        x, y, c = _place()
        sib = (x, y, 1 - c)
        chips = [(1 - x, y), (x, 1 - y), (1 - x, 1 - y)]
        k_me = 2 * x + y
        mine = []
        for t in range(n):
            mine.append(_remote(ins[t].at[0], outs[t].at[0, k_me], own_send, own_recv, 2 * t, sib))
            mine.append(_remote(ins[t].at[1], lands[t].at[0, k_me], own_send, own_recv, 2 * t + 1, sib))
        for cp in mine:
            cp.start()

        def slot(t, chip):
            return outs[t].at[0, 2 * chip[0] + chip[1]]

        @pl.when(c == 0)
        def _():
            first = [_remote(ins[t].at[0], outs[t].at[0, k_me], send_sems, recv_sems, 6 * t + j, (*chip, 0))
                     for t in range(n) for j, chip in enumerate(chips)]
            for cp in first:
                cp.start()
            passed = []
            for t in range(n):
                for j, chip in enumerate(chips):
                    _remote(slot(t, chip), slot(t, chip), send_sems, recv_sems, 6 * t + j, (*chip, 0)).wait_recv()
                    cp = _remote(slot(t, chip), slot(t, chip), send_sems, recv_sems, 6 * t + 3 + j, sib)
                    cp.start()
                    passed.append(cp)
            for cp in first + passed:
                cp.wait_send()

        @pl.when(c == 1)
        def _():
            for t in range(n):
                for j, chip in enumerate(chips):
                    _remote(slot(t, chip), slot(t, chip), send_sems, recv_sems, 6 * t + 3 + j, sib).wait_recv()

        for cp in mine:
            cp.wait()

    shape = [jax.ShapeDtypeStruct((1, 4) + s.shape[1:], s.dtype) for s in shards]
    out = _pcall(body, name="gather_first", in_specs=[ANY] * n, out_specs=[ANY] * (2 * n), out_shape=shape + shape,
                 scratch_shapes=[pltpu.SemaphoreType.DMA((6 * n,)), pltpu.SemaphoreType.DMA((6 * n,)),
                                 pltpu.SemaphoreType.DMA((2 * n,)), pltpu.SemaphoreType.DMA((2 * n,))])(*shards)
    return out[:n], out[n:]


def _rest_copies(ins, lands, send_sems, recv_sems):
    x, y, c = _place()
    chips = [(1 - x, y), (x, 1 - y), (1 - x, 1 - y)]
    copies, arrivals = [], []
    for t in range(len(ins)):
        for j, chip in enumerate(chips):
            for to in range(2):
                copies.append(pltpu.make_async_remote_copy(
                    src_ref=ins[t].at[1], dst_ref=lands[t].at[0, 2 * x + y], send_sem=send_sems.at[6 * t + 2 * j + to],
                    recv_sem=recv_sems.at[3 * t + j], device_id=(*chip, to), device_id_type=MESH))
            blk = lands[t].at[0, 2 * chip[0] + chip[1]]
            arrivals.append(pltpu.make_async_remote_copy(
                src_ref=blk, dst_ref=blk, send_sem=send_sems.at[6 * t + 2 * j], recv_sem=recv_sems.at[3 * t + j],
                device_id=(*chip, 1), device_id_type=MESH))
    return copies, arrivals


def gather_rest_start(shards, lands):
    n = len(shards)

    def body(*refs):
        ins, lds = refs[:n], refs[n:2 * n]
        send_sems, recv_sems = refs[2 * n], refs[2 * n + 1]
        token = refs[-1]
        copies, _ = _rest_copies(ins, lds, send_sems, recv_sems)

        @pl.when(lax.axis_index("c") == 1)
        def _():
            for cp in copies:
                cp.start()

        token[...] = jnp.zeros_like(token)

    hbm = lambda a: pltpu.with_memory_space_constraint(a, pltpu.HBM)
    out = _pcall(body, name="gather_rest_start", in_specs=[HBM] * (2 * n),
                 out_specs=[SEM, SEM] + [HBM] * (2 * n) + [pl.BlockSpec(memory_space=pltpu.VMEM)],
                 out_shape=[pltpu.SemaphoreType.DMA((6 * n,)), pltpu.SemaphoreType.DMA((3 * n,))]
                 + [pltpu.HBM(a.shape, a.dtype) for a in shards] + [pltpu.HBM(a.shape, a.dtype) for a in lands]
                 + [jax.ShapeDtypeStruct((8, 128), F32)],
                 input_output_aliases={i: 2 + i for i in range(2 * n)},
                 compiler_params=pltpu.CompilerParams(has_side_effects=EFFECT))(
                     *[hbm(a) for a in shards], *[hbm(a) for a in lands])
    return out[0], out[1], out[2:2 + n], out[2 + n:2 + 2 * n], out[-1]


def gather_rest_wait(send_sems, recv_sems, srcs, lands, after):
    n = len(srcs)

    def body(*refs):
        ins, lds = refs[:n], refs[n:2 * n]
        s_sems, r_sems = refs[2 * n], refs[2 * n + 1]
        copies, arrivals = _rest_copies(ins, lds, s_sems, r_sems)

        @pl.when(lax.axis_index("c") == 1)
        def _():
            for cp in copies:
                cp.wait_send()

        for cp in arrivals:
            cp.wait_recv()

    out = _pcall(body, name="gather_rest_wait", in_specs=[HBM] * (2 * n) + [SEM, SEM, ANY],
                 out_specs=[HBM] * (2 * n),
                 out_shape=[pltpu.HBM(a.shape, a.dtype) for a in srcs] + [pltpu.HBM(a.shape, a.dtype) for a in lands],
                 input_output_aliases={i: i for i in range(2 * n)},
                 compiler_params=pltpu.CompilerParams(has_side_effects=EFFECT))(
                     *srcs, *lands, send_sems, recv_sems, after)
    return out[n:]


def pair_send(gl, owner, layer):
    n = len(gl)

    def body(*refs):
        ins, outs = refs[:n], refs[n:2 * n]
        send_sems, recv_sems = refs[2 * n:]
        x, y, c = _place()
        sib = (x, y, 1 - c)
        cps = [_remote(ins[t], outs[t], send_sems, recv_sems, t, sib) for t in range(n)]
        for core in range(2):
            @pl.when(c == core)
            def _():
                for cp in _owned(cps, owner, 1 - core, per=1):
                    cp.start()
                for cp in _owned(cps, owner, 1 - core, per=1):
                    cp.wait_send()
                for cp in _owned(cps, owner, core, per=1):
                    cp.wait_recv()

    return _pcall(body, name="pair_send_l%d" % layer, in_specs=[ANY] * n, out_specs=[ANY] * n,
                  out_shape=[jax.ShapeDtypeStruct(a.shape, a.dtype) for a in gl],
                  scratch_shapes=[pltpu.SemaphoreType.DMA((n,)), pltpu.SemaphoreType.DMA((n,))])(*gl)


def _chip_copies(ins, outs, send_sems, recv_sems):
    x, y, c = _place()
    chips = [(1 - x, y), (x, 1 - y), (1 - x, 1 - y)]
    return [_remote(ins[t].at[2 * chip[0] + chip[1]], outs[t].at[j], send_sems, recv_sems, 3 * t + j, (*chip, c))
            for t in range(len(ins)) for j, chip in enumerate(chips)]


def _owned(cps, owner, core, per=3):
    return [cp for k, cp in enumerate(cps) if owner[k // per] == core]


def chip_send(s1, owner, layer):
    n = len(s1)

    def body(*refs):
        ins, outs = refs[:n], refs[n:2 * n]
        send_sems, recv_sems = refs[2 * n:]
        cps = _chip_copies(ins, outs, send_sems, recv_sems)
        for core in range(2):
            @pl.when(lax.axis_index("c") == core)
            def _():
                for cp in _owned(cps, owner, core):
                    cp.start()
                for cp in _owned(cps, owner, core):
                    cp.wait()

    return _pcall(body, name="chip_send_l%d" % layer, in_specs=[ANY] * n, out_specs=[ANY] * n,
                  out_shape=[jax.ShapeDtypeStruct((3,) + a.shape[1:], a.dtype) for a in s1],
                  scratch_shapes=[pltpu.SemaphoreType.DMA((3 * n,)), pltpu.SemaphoreType.DMA((3 * n,))])(*s1)


def chip_send_start(s1, owner, layer):
    n = len(s1)
    land = [lax.empty((3,) + a.shape[1:], a.dtype) for a in s1]

    def body(*refs):
        ins, lands = refs[:n], refs[n:2 * n]
        send_sems, recv_sems = refs[2 * n], refs[2 * n + 1]
        token = refs[-1]
        cps = _chip_copies(ins, lands, send_sems, recv_sems)
        for core in range(2):
            @pl.when(lax.axis_index("c") == core)
            def _():
                for cp in _owned(cps, owner, core):
                    cp.start()

        token[...] = jnp.zeros_like(token)

    hbm = lambda a: pltpu.with_memory_space_constraint(a, pltpu.HBM)
    out = _pcall(body, name="chip_send_start_l%d" % layer, in_specs=[HBM] * (2 * n),
                 out_specs=[SEM, SEM] + [HBM] * (2 * n) + [pl.BlockSpec(memory_space=pltpu.VMEM)],
                 out_shape=[pltpu.SemaphoreType.DMA((3 * n,)), pltpu.SemaphoreType.DMA((3 * n,))]
                 + [pltpu.HBM(a.shape, a.dtype) for a in s1] + [pltpu.HBM(a.shape, a.dtype) for a in land]
                 + [jax.ShapeDtypeStruct((8, 128), F32)],
                 input_output_aliases={i: 2 + i for i in range(2 * n)},
                 compiler_params=pltpu.CompilerParams(has_side_effects=EFFECT))(
                     *[hbm(a) for a in s1], *[hbm(a) for a in land])
    return out[0], out[1], out[2:2 + n], out[2 + n:2 + 2 * n], out[-1]


def chip_send_wait(send_sems, recv_sems, srcs, lands, after, owner, layer):
    n = len(srcs)

    def body(*refs):
        ins, lds = refs[:n], refs[n:2 * n]
        s_sems, r_sems = refs[2 * n], refs[2 * n + 1]
        cps = _chip_copies(ins, lds, s_sems, r_sems)
        for core in range(2):
            @pl.when(lax.axis_index("c") == core)
            def _():
                for cp in _owned(cps, owner, core):
                    cp.wait_send()
                    cp.wait_recv()

    out = _pcall(body, name="chip_send_wait_l%d" % layer, in_specs=[HBM] * (2 * n) + [SEM, SEM, ANY],
                 out_specs=[HBM] * (2 * n),
                 out_shape=[pltpu.HBM(a.shape, a.dtype) for a in srcs] + [pltpu.HBM(a.shape, a.dtype) for a in lands],
                 input_output_aliases={i: i for i in range(2 * n)},
                 compiler_params=pltpu.CompilerParams(has_side_effects=EFFECT))(
                     *srcs, *lands, send_sems, recv_sems, after)
    return out[n:]


def pair_share(s2, owner):
    n = len(s2)

    def body(*refs):
        ins, outs = refs[:n], refs[n:2 * n]
        send_sems, recv_sems = refs[2 * n:]
        x, y, c = _place()
        sib = (x, y, 1 - c)
        cps = [_remote(ins[t], outs[t], send_sems, recv_sems, t, sib) for t in range(n)]
        for core in range(2):
            @pl.when(c == core)
            def _():
                for cp in _owned(cps, owner, core, per=1):
                    cp.start()
                for cp in _owned(cps, owner, core, per=1):
                    cp.wait_send()
                for cp in _owned(cps, owner, 1 - core, per=1):
                    cp.wait_recv()

    return _pcall(body, name="pair_share", in_specs=[ANY] * n, out_specs=[ANY] * n,
                  out_shape=[jax.ShapeDtypeStruct(a.shape, a.dtype) for a in s2],
                  input_output_aliases={t: t for t in range(n)},
                  scratch_shapes=[pltpu.SemaphoreType.DMA((n,)), pltpu.SemaphoreType.DMA((n,))])(*s2)


def small_exchange(gs):
    rows, width = gs.shape

    def body(g_ref, o_ref, send_sems, recv_sems):
        x, y, c = _place()
        cps = []
        for r in range(1, 8):
            dx, dy, dc = (r >> 2) & 1, (r >> 1) & 1, r & 1
            to = (x if dx == 0 else 1 - x, y if dy == 0 else 1 - y, c if dc == 0 else 1 - c)
            cps.append(_remote(g_ref, o_ref.at[r - 1], send_sems, recv_sems, r - 1, to))
        for cp in cps:
            cp.start()
        for cp in cps:
            cp.wait()

    return _pcall(body, name="small_exchange", in_specs=[ANY], out_specs=ANY,
                  out_shape=jax.ShapeDtypeStruct((7, rows, width), gs.dtype),
                  scratch_shapes=[pltpu.SemaphoreType.DMA((7,)), pltpu.SemaphoreType.DMA((7,))])(gs)


def _row_tile(rows):
    return _pick(rows, (256, 352, 128, 64, 32, 16))


def pair_add_layer(g, rb, core, owner, *, name):
    _, rows, width = g.shape
    tr = _row_tile(rows)

    def body(c_ref, g_ref, r_ref, o_ref, ob_ref):
        @pl.when(c_ref[0] == owner)
        def _():
            s = g_ref[...] + r_ref[...]
            o_ref[...] = s
            ob_ref[...] = s.astype(BF16)

    def at(k, i, c_ref):
        mine = c_ref[0] == owner
        return (jnp.where(mine, k, 0), jnp.where(mine, i, 0), 0)

    blk = pl.BlockSpec((None, tr, width), at)
    gs = pltpu.PrefetchScalarGridSpec(num_scalar_prefetch=1, grid=(4, rows // tr), in_specs=[blk, blk],
                                      out_specs=[blk, blk])
    return _pcall(body, name=name, grid_spec=gs,
                  out_shape=[jax.ShapeDtypeStruct(g.shape, F32), jax.ShapeDtypeStruct(g.shape, BF16)],
                  compiler_params=_cp("parallel", "parallel"))(core, g, rb)


def chip_add_layers(s1, rb2, chip, core, owner, *, name):
    _, rows, width = s1[0].shape
    tr = _row_tile(rows)

    def body(k_ref, c_ref, s0_ref, s1_ref, r0_ref, r1_ref, o_ref):
        @pl.when(c_ref[0] == owner)
        def _():
            first = pl.program_id(0) == 0
            s = jnp.where(first, s0_ref[...], s1_ref[...])
            r = jnp.where(first, r0_ref[...], r1_ref[...]).astype(F32)
            o_ref[...] = ((s + r[0]) + r[1]) + r[2]

    def s_spec(layer):
        def at(l, i, k_ref, c_ref):
            use = jnp.logical_and(l == layer, c_ref[0] == owner)
            return (jnp.where(use, k_ref[0], 0), jnp.where(use, i, 0), 0)
        return pl.BlockSpec((None, tr, width), at)

    def r_spec(layer):
        def at(l, i, k_ref, c_ref):
            return (0, jnp.where(jnp.logical_and(l == layer, c_ref[0] == owner), i, 0), 0)
        return pl.BlockSpec((3, tr, width), at)

    def out_at(l, i, k_ref, c_ref):
        mine = c_ref[0] == owner
        return (jnp.where(mine, l, 0), jnp.where(mine, i, 0), 0)

    gs = pltpu.PrefetchScalarGridSpec(
        num_scalar_prefetch=2, grid=(DEPTH, rows // tr),
        in_specs=[s_spec(0), s_spec(1), r_spec(0), r_spec(1)],
        out_specs=pl.BlockSpec((None, tr, width), out_at))
    return _pcall(body, name=name, grid_spec=gs, out_shape=jax.ShapeDtypeStruct((DEPTH, rows, width), F32),
                  compiler_params=_cp("parallel", "parallel"))(chip, core, s1[0], s1[1], rb2[0], rb2[1])


def small_add(gs_own, slots, me):
    rows, width = gs_own.shape
    tr = _pick(rows, (64, 32, 16, 8))

    def body(me_ref, g_ref, s_ref, o_ref):
        me_v = me_ref[0]
        total = None
        for d in range(8):
            rel = jnp.bitwise_xor(me_v, d)
            val = jnp.where(rel == 0, g_ref[...], s_ref[jnp.maximum(rel - 1, 0)])
            total = val if total is None else total + val
        o_ref[...] = total

    gs = pltpu.PrefetchScalarGridSpec(
        num_scalar_prefetch=1, grid=(rows // tr,),
        in_specs=[pl.BlockSpec((tr, width), lambda i, m_ref: (i, 0)),
                  pl.BlockSpec((7, tr, width), lambda i, m_ref: (0, i, 0))],
        out_specs=pl.BlockSpec((tr, width), lambda i, m_ref: (i, 0)))
    return _pcall(body, name="small_add", grid_spec=gs, out_shape=jax.ShapeDtypeStruct((rows, width), F32),
                  compiler_params=_cp("parallel"))(me, gs_own, slots)


SHARDED = (("ffn1_w_up", (D, UPW)), ("ffn1_w_down", (DFF // 4, D)), ("w_in", (D, D_IN // 4)),
           ("conv_w", (4, BW // 4)), ("gla_w_g2", (LOW_W, 64)), ("w_branch", (3 * BW, D // 4)),
           ("w_out", (D // 4, D)), ("ffn2_w_up", (D, UPW)), ("ffn2_w_down", (DFF // 4, D)),
           ("ple_w_proj", (PLE, D // 4)), ("ple_w_gate", (D // 4, D)))
OWNER = tuple(0 if n in ("ffn1_w_up", "w_in", "w_out") else 1 for n, _ in SHARDED)
SMALL = ("ln1_g", "ln1_b", "conv_b", "lru_wa", "lru_ba", "lru_wx", "lru_bx", "lru_lambda", "gla_b_g",
         "gla_norm_g", "fox_b_f", "ln2_g", "ln2_b", "ln3_g", "ln3_b", "ple_b_gate", "ln4_g", "ln4_b")
WEIGHTS = ('ffn1_w_up', 'ffn1_w_down', 'ln1_g', 'ln1_b', 'w_in', 'conv_w', 'conv_b', 'lru_wa', 'lru_ba',
           'lru_wx', 'lru_bx', 'lru_lambda', 'gla_w_g2', 'gla_b_g', 'gla_norm_g', 'fox_b_f', 'w_branch',
           'w_out', 'ln2_g', 'ln2_b', 'ffn2_w_up', 'ffn2_w_down', 'ln3_g', 'ln3_b', 'ple_w_proj',
           'ple_w_gate', 'ple_b_gate', 'ln4_g', 'ln4_b')


def _cols_join(parts):
    return jnp.concatenate([parts[k] for k in range(4)], axis=-1)


def _cols_split(full):
    r, c4 = full.shape
    return full.reshape(r, 4, c4 // 4).transpose(1, 0, 2)


def _regroup_in(w):
    pad = jnp.zeros(w.shape[:-1] + (ZW - D_IN,), w.dtype)
    fox_q = (w[..., 2576:3088] * FOX_SCALE).astype(w.dtype)
    return jnp.concatenate([w[..., 0:2048], w[..., 2064:2576], fox_q, w[..., 3088:4112], w[..., 4120:7192],
                            w[..., 2048:2064], w[..., 4112:4120], pad], axis=-1)


_IN_RUNS = ((0, 2048, 0, 1.0), (2048, 2064, 7168, 1.0), (2064, 2576, 2048, 1.0), (2576, 3088, CQ, FOX_SCALE),
            (3088, 4112, CK, 1.0), (4112, 4120, 7184, 1.0), (4120, D_IN, 4096, 1.0))


def _regroup_out_shards(g):
    w = D_IN // 4
    shards = []
    for k in range(4):
        pieces = []
        for a, b, new, f in _IN_RUNS:
            lo, hi = max(a, k * w), min(b, (k + 1) * w)
            if lo < hi:
                piece = g[:, new + lo - a:new + hi - a]
                pieces.append(piece if f == 1.0 else piece * f)
        shards.append(jnp.concatenate(pieces, axis=1))
    return jnp.stack(shards)


def _block_diag(w):
    eye = jnp.eye(8, dtype=w.dtype)
    return (eye[:, None, :, None] * w[:, :, None, :]).reshape(BW, BW)


def _diag_blocks(dense):
    return jnp.stack([dense[64 * n:64 * (n + 1), 64 * n:64 * (n + 1)] for n in range(8)])


def _layer_weights(gw, small, l):
    w = {"up1": gw["ffn1_w_up"], "up2": gw["ffn2_w_up"],
         "dn1": gw["ffn1_w_down"].reshape(1, DFF, D), "dn2": gw["ffn2_w_down"].reshape(1, DFF, D),
         "wo": gw["w_out"].reshape(1, D, D), "wgt": gw["ple_w_gate"].reshape(1, D, D)}
    w["win"] = _regroup_in(_cols_join(gw["w_in"][0]))
    w["cw"] = _cols_join(gw["conv_w"][0])
    w["wa"] = _block_diag(small["lru_wa"][l]).astype(BF16)
    w["wx"] = _block_diag(small["lru_wx"][l]).astype(BF16)
    w["wg2p"] = jnp.pad(_cols_join(gw["gla_w_g2"][0]), ((0, 128 - LOW_W), (0, 0)))
    w["wbr"] = _cols_join(gw["w_branch"][0].reshape(4, 3, BW, D // 4))
    w["wp"] = _cols_join(gw["ple_w_proj"][0])
    for n in ("ln1_g", "ln1_b", "ln2_g", "ln2_b", "ln3_g", "ln3_b", "ln4_g", "ln4_b", "conv_b", "lru_ba",
              "lru_bx", "lru_lambda", "gla_b_g", "gla_norm_g", "ple_b_gate"):
        w[n] = small[n][l][None, :]
    w["bfp"] = jnp.pad(small["fox_b_f"][l], (LOW_W, 128 - LOW_W - FOX_H))[None, :]
    return w


def _heads_t(a):
    ht = a[:, LOW_W:LOW_W + FOX_H].T
    return ht[:, None, :], jnp.broadcast_to(ht[:, :, None], ht.shape + (128,))


def _layer_fwd(x, xb, pb, w, l):
    s = {"x0": x, "x0b": xb}
    tag = "l%d_" % l
    gate, up, act = ffn_up(xb, w["up1"], 0, name=tag + "ffn1_up")
    r1, x1, x1b = matmul_res_ln(act, w["dn1"], 0, x, w["ln1_g"], w["ln1_b"], mm_scale=0.5, name=tag + "ffn1_down")
    s.update(gate1=gate, up1=up, act1=act, r1=r1, x1=x1, x1b=x1b)
    z, zb = matmul(x1b, w["win"], also_bf16=True, tm=1024, tn=_pick(ZW, (2432,)), name=tag + "mix_in")
    xc, xcb, h, ya = lru_fwd(z, w["cw"], w["conv_b"], w["wa"], w["wx"], w["lru_ba"], w["lru_bx"],
                             w["lru_lambda"], name=tag + "lru_fwd")
    yb, states = gla_fwd(z, zb, w["wg2p"], w["gla_b_g"], w["gla_norm_g"], name=tag + "gla_fwd")
    fcum = fox_fcum(z, w["bfp"], name=tag + "fox_fcum")
    fq, fk = _heads_t(fcum)
    yc, ycf, lse = fox_fwd_t(zb, fq, fk, name=tag + "fox_fwd")
    merged = merge_fwd(ya, yb, yc, w["wbr"], z, name=tag + "merge_fwd")
    r2, x2, x2b = matmul_res_ln(merged, w["wo"], 0, x1, w["ln2_g"], w["ln2_b"], mm_scale=1.0, name=tag + "mix_out")
    s.update(z=z, zb=zb, xc=xc, xcb=xcb, h=h, ya=ya, yb=yb, states=states, fq=fq, fk=fk, yc=yc, ycf=ycf,
             lse=lse, merged=merged, r2=r2, x2=x2, x2b=x2b)
    gate, up, act = ffn_up(x2b, w["up2"], 0, name=tag + "ffn2_up")
    r3, x3, x3b = matmul_res_ln(act, w["dn2"], 0, x2, w["ln3_g"], w["ln3_b"], mm_scale=0.5, name=tag + "ffn2_down")
    s.update(gate2=gate, up2=up, act2=act, r3=r3, x3=x3, x3b=x3b)
    r4, x4, x4b = ple_fwd(x3b, x3, pb, w["wgt"], 0, w["wp"], w["ple_b_gate"], w["ln4_g"], w["ln4_b"],
                          name=tag + "ple_fwd")
    s.update(r4=r4, pb=pb)
    return x4, x4b, s


def _ffn_bwd(dy, s, w, n, xin_b, l, tag):
    k = {"1": ("r1", "ln1_g", "gate1", "up1", "act1"), "2": ("r3", "ln3_g", "gate2", "up2", "act2")}[n]
    dr, dfb, dg, db = ln_bwd(dy, s[k[0]], w[k[1]], out_scale=0.5, name=tag + "ln_bwd")
    dgate, dup = ffn_down_bwd(dfb, w["dn" + n], 0, s[k[2]], s[k[3]], name=tag + "down_bwd")
    dx = ffn_dx(dgate, dup, w["up" + n], 0, dr, name=tag + "dx")
    dwup = matmul_tn_up(xin_b, dgate, dup, name=tag + "dw_up")
    dwdn = matmul_tn(s[k[4]], dfb, name=tag + "dw_down").reshape(4, DFF // 4, D)
    return dx, dwup, dwdn, dg[0], db[0]


def _layer_bwd(dy, s, w, l):
    g = {}
    tag = "l%d_" % l
    dr4, dglb, dpeb, dg4, db4, dbg = ple_bwd(dy, s["r4"], s["x3b"], s["pb"], w["wgt"], 0, w["wp"], w["ple_b_gate"],
                                             w["ln4_g"], name=tag + "ple_bwd")
    dx3 = matmul(dglb, w["wgt"], nt=True, b_lead=(0,), res=dr4, res_scale=ALPHA, tm=1024, tn=1024,
                 name=tag + "ple_dx")
    g["ple_w_gate"] = matmul_tn(s["x3b"], dglb, name=tag + "ple_dw_gate").reshape(4, D // 4, D)
    g["ple_w_proj"] = _cols_split(matmul_tn(s["pb"], dpeb, name=tag + "ple_dw_proj"))
    g["ln4_g"], g["ln4_b"], g["ple_b_gate"] = dg4[0], db4[0], dbg[0]
    dx2, g["ffn2_w_up"], g["ffn2_w_down"], g["ln3_g"], g["ln3_b"] = _ffn_bwd(dx3, s, w, "2", s["x2b"], l,
                                                                             tag + "ffn2_")
    dr2, doutb, dg2, db2 = ln_bwd(dx2, s["r2"], w["ln2_g"], out_scale=1.0, name=tag + "mix_ln_bwd")
    g["ln2_g"], g["ln2_b"] = dg2[0], db2[0]
    g["w_out"] = matmul_tn(s["merged"], doutb, name=tag + "dw_out").reshape(4, D // 4, D)
    z, zb = s["z"], s["zb"]
    (dya, dyb, dyc, dp0, dp1, dp2, dgl0, dgl1, dgl2) = merge_bwd(
        doutb, w["wo"], 0, s["ya"], s["yb"], s["yc"], w["wbr"], z, name=tag + "merge_bwd")
    dwbr = jnp.stack([matmul_tn(s["ya"], dp0, name=tag + "dw_br0"), matmul_tn(s["yb"], dp1, name=tag + "dw_br1"),
                      matmul_tn(s["yc"], dp2, name=tag + "dw_br2")])
    g["w_branch"] = _cols_split(dwbr.reshape(3 * BW, D))
    day, dxc, dprb, dpib, dba, dbx, dlam = lru_bwd(dya, z, s["h"], s["xc"], w["wa"], w["wx"],
                                                   w["lru_ba"], w["lru_bx"], w["lru_lambda"], name=tag + "lru_bwd")
    dax, dcw, dcb = conv_bwd(dxc, z, w["cw"], name=tag + "conv_bwd")
    g["lru_wa"] = _diag_blocks(matmul_tn(s["xcb"], dprb, name=tag + "dw_lru_a"))
    g["lru_wx"] = _diag_blocks(matmul_tn(s["xcb"], dpib, name=tag + "dw_lru_x"))
    g["lru_ba"], g["lru_bx"], g["lru_lambda"] = dba[0], dbx[0], dlam[0]
    g["conv_w"], g["conv_b"] = _cols_split(dcw), dcb[0]
    dbq, dbk, dbv, dbr, dmisc_g, dpreb, dbgg, dng = gla_bwd(dyb, z, zb, s["states"], w["wg2p"], w["gla_b_g"],
                                                            w["gla_norm_g"], name=tag + "gla_bwd")
    miscb = zb[:, MISC:]
    g["gla_w_g2"] = _cols_split(matmul_tn(miscb, dpreb, name=tag + "dw_g2")[:LOW_W])
    g["gla_b_g"], g["gla_norm_g"] = dbgg[0], dng[0]
    dl = fox_delta(dyc, s["ycf"], name=tag + "fox_delta")
    t = z.shape[0]
    dlq = dl[:, :FOX_H].T[:, None, :]
    dcq, dck, dcv, dfk, dfq = fox_bwd_t(zb, dyc, s["fq"], s["fk"], s["lse"], dlq, name=tag + "fox_bwd")
    dfc = jnp.pad((dfk[:, 0, :] + dfq.reshape(FOX_H, t)).T, ((0, 0), (LOW_W, 128 - LOW_W - FOX_H)))
    dmiscb, dbf = fox_dcf(dfc, z, w["bfp"], dmisc_g, name=tag + "fox_dcf")
    g["fox_b_f"] = dbf[0, LOW_W:LOW_W + FOX_H]
    dz = jnp.concatenate([dax, day, dbq, dbk, dbv, dbr, dcq, dck, dcv, dgl0, dgl1, dgl2, dmiscb], axis=1)
    dx1 = matmul(dz, w["win"], nt=True, res=dr2, res_scale=ALPHA, tm=1024, tn=1024, tk=_pick(ZW, (2432,)),
                 name=tag + "mix_dx")
    g["w_in"] = _regroup_out_shards(matmul_tn(s["x1b"], dz, name=tag + "dw_in"))
    dx0, g["ffn1_w_up"], g["ffn1_w_down"], g["ln1_g"], g["ln1_b"] = _ffn_bwd(dx1, s, w, "1", s["x0b"], l,
                                                                             tag + "ffn1_")
    return dx0, g


def _local_step(x, p, target, gathered, small, after_last_layer=None):
    xcur = x
    xb = xcur.astype(BF16)
    layer_w, saved = [], []
    for l in range(DEPTH):
        w = _layer_weights(gathered(l, xcur), small, l)
        xcur, xb, s = _layer_fwd(xcur, xb, p[l].astype(BF16), w, l)
        layer_w.append(w)
        saved.append(s)
    dy, sq = loss_head(xcur, target, name="loss_head")
    grads = [None] * DEPTH
    for l in reversed(range(DEPTH)):
        dy, grads[l] = _layer_bwd(dy, saved[l], layer_w[l], l)
        if l == DEPTH - 1 and after_last_layer is not None:
            layer_w[l - 1]["ln4_g"] = layer_w[l - 1]["ln4_g"] + after_last_layer(grads[l])
    return 0.5 * jnp.sum(sq) / float(D), dy, grads


def kernel(x, p, ffn1_w_up, ffn1_w_down, ln1_g, ln1_b, w_in, conv_w, conv_b, lru_wa, lru_ba, lru_wx, lru_bx, lru_lambda, gla_w_g2, gla_b_g, gla_norm_g, fox_b_f, w_branch, w_out, ln2_g, ln2_b, ffn2_w_up, ffn2_w_down, ln3_g, ln3_b, ple_w_proj, ple_w_gate, ple_b_gate, ln4_g, ln4_b, loss_target, m_ffn1_w_up, m_ffn1_w_down, m_ln1_g, m_ln1_b, m_w_in, m_conv_w, m_conv_b, m_lru_wa, m_lru_ba, m_lru_wx, m_lru_bx, m_lru_lambda, m_gla_w_g2, m_gla_b_g, m_gla_norm_g, m_fox_b_f, m_w_branch, m_w_out, m_ln2_g, m_ln2_b, m_ffn2_w_up, m_ffn2_w_down, m_ln3_g, m_ln3_b, m_ple_w_proj, m_ple_w_gate, m_ple_b_gate, m_ln4_g, m_ln4_b, v_ffn1_w_up, v_ffn1_w_down, v_ln1_g, v_ln1_b, v_w_in, v_conv_w, v_conv_b, v_lru_wa, v_lru_ba, v_lru_wx, v_lru_bx, v_lru_lambda, v_gla_w_g2, v_gla_b_g, v_gla_norm_g, v_fox_b_f, v_w_branch, v_w_out, v_ln2_g, v_ln2_b, v_ffn2_w_up, v_ffn2_w_down, v_ln3_g, v_ln3_b, v_ple_w_proj, v_ple_w_gate, v_ple_b_gate, v_ln4_g, v_ln4_b):
    args = dict(locals())
    wts = {n: args[n] for n in WEIGHTS}
    mom = {n: args["m_" + n] for n in WEIGHTS}
    var = {n: args["v_" + n] for n in WEIGHTS}
    cx, cy, cc = lax.axis_index("x"), lax.axis_index("y"), lax.axis_index("c")

    names = [n for n, _ in SHARDED]
    shards = [wts[n].reshape((DEPTH,) + rc).astype(F32 if n == "conv_w" else BF16) for n, rc in SHARDED]
    first, lands = gather_first(shards)
    rest_send, rest_recv, rest_srcs, rest_lands, rest_token = gather_rest_start(shards, lands)
    small = {n: wts[n] for n in SMALL}
    small["ln1_g"] = small["ln1_g"] + rest_token[0, 0]

    def gathered(l, after):
        if l == 0:
            return dict(zip(names, first))
        return dict(zip(names, gather_rest_wait(rest_send, rest_recv, rest_srcs, rest_lands, after)))

    flight = {}
    core = jnp.reshape(cc, (1,)).astype(jnp.int32)
    chip = jnp.reshape(2 * cx + cy, (1,)).astype(jnp.int32)

    def chip_sum(gl, layer):
        lst = [gl[n] for n in names]
        rb = pair_send(lst, OWNER, layer)
        return [pair_add_layer(a, r, core, own, name="pair_add_l%d_%s" % (layer, n))
                for n, own, a, r in zip(names, OWNER, lst, rb)]

    def start_last_layer(gl):
        s1 = chip_sum(gl, DEPTH - 1)
        send_sems, recv_sems, srcs, lands, token = chip_send_start([sb for _, sb in s1], OWNER, DEPTH - 1)
        flight.update(s1=[sf for sf, _ in s1], sems=(send_sems, recv_sems), srcs=srcs, lands=lands)
        return token[0, 0]

    loss_local, dx, grads = _local_step(x[0], p[:, 0], loss_target[0], gathered, small, start_last_layer)
    loss = lax.psum(loss_local, ("x", "y", "c"))
    grad_x = dx[None]

    pieces, spans, row = [], {}, 0
    for n in SMALL:
        flat = jnp.stack([grads[l][n] for l in range(DEPTH)]).reshape(-1)
        rows = -(-flat.shape[0] // (8 * PACK_W)) * 8
        pieces.append(jnp.pad(flat, (0, rows * PACK_W - flat.shape[0])).reshape(rows, PACK_W))
        spans[n] = (row, rows)
        row += rows
    gs = jnp.concatenate(pieces, axis=0)
    gs_all = small_exchange(gs)

    s1_first = chip_sum(grads[0], 0)
    first_send, first_recv, first_srcs, first_lands, first_token = chip_send_start(
        [sb for _, sb in s1_first], OWNER, 0)
    me = jnp.reshape(4 * cx + 2 * cy + cc, (1,)).astype(jnp.int32)
    gsum = small_add(gs + first_token[0, 0], gs_all, me)

    gout, delta, new_m, new_v = {}, {}, {}, {}

    def update(n, view, g):
        shp = wts[n].shape
        d, mn, vn = adamw(wts[n].reshape(view), g, mom[n].reshape(view), var[n].reshape(view), name="adamw_" + n)
        gout[n], delta[n], new_m[n], new_v[n] = g.reshape(shp), d.reshape(shp), mn.reshape(shp), vn.reshape(shp)

    for n in SMALL:
        view = (1, DEPTH, wts[n].size // DEPTH)
        r0, rows = spans[n]
        update(n, view, gsum[r0:r0 + rows].reshape(-1)[:wts[n].size].reshape(view))

    rb2_first = chip_send_wait(first_send, first_recv, first_srcs, first_lands, new_v[SMALL[-1]], OWNER, 0)
    rb2_last = chip_send_wait(*flight["sems"], flight["srcs"], flight["lands"], dx, OWNER, DEPTH - 1)
    s2 = [chip_add_layers((sf0, sf1), (r0, r1), chip, core, own, name="chip_add_" + n)
          for n, own, (sf0, _), sf1, r0, r1 in zip(names, OWNER, s1_first, flight["s1"], rb2_first, rb2_last)]
    for n, gsh in zip(names, pair_share(s2, OWNER)):
        update(n, gsh.shape, gsh)

    return (loss, grad_x, *[gout[n] for n in WEIGHTS], *[delta[n] for n in WEIGHTS],
            *[new_m[n] for n in WEIGHTS], *[new_v[n] for n in WEIGHTS])
```
